```python
import jax, jax.numpy as jnp
from jax import lax
import numpy as np

D_MODEL = 1024
BATCH = 8
SEQ = 8192
DEPTH = 1

GLA_HEADS = 4
GLA_DK = 64
GLA_DV = 128
GLA_GATE_RANK = 16
GLA_TAU = 16.0
GLA_CHUNK = 64
SWA_HEADS = 8
SWA_KV_HEADS = 2
SWA_HD = 64
SWA_WINDOW = 128
SWA_BLOCK = 128
ROPE_THETA = 500000.0
ROPE_DIM = SWA_HD // 4
D_FF = 2816
CONV_WIDTH = 3
EPS = 1e-6
MAX_POS_OFFSET = 4096

GLA_QK = GLA_HEADS * GLA_DK
GLA_V = GLA_HEADS * GLA_DV
SWA_Q = SWA_HEADS * SWA_HD
SWA_KV = SWA_KV_HEADS * SWA_HD
MIX_WIDTH = GLA_V + SWA_Q
IN_SPLITS = (GLA_QK, GLA_QK, GLA_V, GLA_GATE_RANK, GLA_V, SWA_Q, SWA_KV, SWA_KV)
IN_WIDTH = GLA_QK * 2 + GLA_V * 2 + GLA_GATE_RANK + SWA_Q + SWA_KV * 2

kernel_name = "hymba_gla_swa_sink_convffn_sandwich"


def rmsnorm(x, w):
    xf = x.astype(jnp.float32)
    y = xf * lax.rsqrt(jnp.mean(xf * xf, axis=-1, keepdims=True) + EPS)
    return (y * w.astype(jnp.float32)).astype(x.dtype)


def partial_rotary(x, positions):
    half = ROPE_DIM // 2
    inv_freq = ROPE_THETA ** (-jnp.arange(half, dtype=jnp.float32) * (2.0 / ROPE_DIM))
    ang = positions.astype(jnp.float32)[..., None] * inv_freq
    cos = jnp.cos(ang)[:, :, None, :]
    sin = jnp.sin(ang)[:, :, None, :]
    xr = x[..., :ROPE_DIM].astype(jnp.float32)
    x1, x2 = xr[..., :half], xr[..., half:]
    rot = jnp.concatenate([x1 * cos - x2 * sin, x2 * cos + x1 * sin], axis=-1)
    return jnp.concatenate([rot.astype(x.dtype), x[..., ROPE_DIM:]], axis=-1)


def gla_chunked(q, k, v, log_a):
    B, T, H, dk = q.shape
    dv = v.shape[-1]
    C = GLA_CHUNK
    n = T // C

    def to_chunks(t):
        return t.reshape(B, n, C, H, t.shape[-1]).transpose(1, 0, 3, 2, 4).astype(jnp.float32)

    qc = to_chunks(q * (dk ** -0.5))
    kc, vc, gc = to_chunks(k), to_chunks(v), to_chunks(log_a)
    causal = jnp.tril(jnp.ones((C, C), dtype=bool))[:, :, None]

    def step(S, inp):
        qi, ki, vi, gi = inp
        b = jnp.cumsum(gi, axis=2)
        o_inter = jnp.einsum('bhcd,bhde->bhce', qi * jnp.exp(b), S)
        diff = b[:, :, :, None, :] - b[:, :, None, :, :]
        decay = jnp.exp(jnp.where(causal, diff, -jnp.inf))
        A = jnp.einsum('bhid,bhjd,bhijd->bhij', qi, ki, decay)
        o_intra = jnp.einsum('bhij,bhje->bhie', A, vi)
        b_last = b[:, :, -1:, :]
        S_new = jnp.exp(b_last[:, :, 0, :])[..., None] * S + jnp.einsum(
            'bhcd,bhce->bhde', ki * jnp.exp(b_last - b), vi)
        return S_new, o_inter + o_intra

    S0 = jnp.zeros((B, H, dk, dv), jnp.float32)
    _, o = lax.scan(step, S0, (qc, kc, vc, gc))
    return o.transpose(1, 0, 3, 2, 4).reshape(B, T, H, dv).astype(v.dtype)


def swa_sink_attention(q, k, v, sinks):
    B, T, Hq, hd = q.shape
    Hkv = k.shape[2]
    G = Hq // Hkv
    W = SWA_BLOCK
    n = T // W
    qb = q.reshape(B, n, W, Hkv, G, hd).astype(jnp.float32)

    def with_prev(t):
        tb = t.reshape(B, n, W, Hkv, hd).astype(jnp.float32)
        prev = jnp.pad(tb, ((0, 0), (1, 0), (0, 0), (0, 0), (0, 0)))[:, :-1]
        return jnp.concatenate([prev, tb], axis=2)

    kb, vb = with_prev(k), with_prev(v)
    s = jnp.einsum('bnqhgd,bnshd->bhgnqs', qb, kb) * (hd ** -0.5)
    blk = jnp.arange(n)[:, None, None]
    qpos = blk * W + jnp.arange(W)[None, :, None]
    kpos = (blk - 1) * W + jnp.arange(2 * W)[None, None, :]
    mask = (kpos <= qpos) & (kpos > qpos - SWA_WINDOW) & (kpos >= 0)
    s = jnp.where(mask, s, -jnp.inf)
    sink = sinks.astype(jnp.float32).reshape(Hkv, G)[None, :, :, None, None]
    m = jnp.maximum(s.max(axis=-1), sink)
    p = jnp.exp(s - m[..., None])
    denom = p.sum(axis=-1) + jnp.exp(sink - m)
    o = jnp.einsum('bhgnqs,bnshd->bnqhgd', p, vb) / denom.transpose(0, 3, 4, 1, 2)[..., None]
    return o.reshape(B, T, Hq * hd).astype(q.dtype)


def causal_depthwise_conv(h, w, b):
    T = h.shape[1]
    hp = jnp.pad(h, ((0, 0), (CONV_WIDTH - 1, 0), (0, 0)))
    out = b
    for j in range(CONV_WIDTH):
        out = out + w[j] * hp[:, j:j + T]
    return out


def _fwd_setup_inputs(seed: int = 0) -> dict:
    key = jax.random.key(seed)
    ks = jax.random.split(key, 18)
    nrm = jax.random.normal
    f32 = jnp.float32
    x = nrm(ks[0], (BATCH, SEQ, D_MODEL), f32)
    offs = jax.random.randint(ks[1], (BATCH, 1), 0, MAX_POS_OFFSET, dtype=jnp.int32)
    positions = (offs + jnp.arange(SEQ, dtype=jnp.int32)[None, :]).astype(jnp.int32)
    gain = lambda k_, d: 1.0 + 0.05 * nrm(k_, (DEPTH, d), f32)
    return {
        "x": x,
        "positions": positions,
        "pre_mix_norm": gain(ks[2], D_MODEL),
        "w_in": nrm(ks[3], (DEPTH, D_MODEL, IN_WIDTH), f32) * D_MODEL ** -0.5,
        "gla_gate_up": nrm(ks[4], (DEPTH, GLA_GATE_RANK, GLA_QK), f32) * GLA_GATE_RANK ** -0.5,
        "gla_gate_bias": 0.1 * nrm(ks[5], (DEPTH, GLA_QK), f32),
        "gla_out_norm": gain(ks[6], GLA_DV),
        "swa_sinks": nrm(ks[7], (DEPTH, SWA_HEADS), f32),
        "w_out": nrm(ks[8], (DEPTH, MIX_WIDTH, D_MODEL), f32) * MIX_WIDTH ** -0.5,
        "post_mix_norm": gain(ks[9], D_MODEL),
        "pre_ffn_norm": gain(ks[10], D_MODEL),
        "w_up": nrm(ks[11], (DEPTH, D_MODEL, 2 * D_FF), f32) * D_MODEL ** -0.5,
        "conv_w": nrm(ks[12], (DEPTH, CONV_WIDTH, 2 * D_FF), f32) * CONV_WIDTH ** -0.5,
        "conv_b": 0.02 * nrm(ks[13], (DEPTH, 2 * D_FF), f32),
        "w_down": nrm(ks[14], (DEPTH, D_FF, D_MODEL), f32) * D_FF ** -0.5,
        "post_ffn_norm": gain(ks[15], D_MODEL),
    }


def _fwd_reference(x, positions, pre_mix_norm, w_in, gla_gate_up, gla_gate_bias, gla_out_norm,
              swa_sinks, w_out, post_mix_norm, pre_ffn_norm, w_up, conv_w, conv_b, w_down,
              post_ffn_norm):
    B, T, _ = x.shape
    split_points = np.cumsum(IN_SPLITS)[:-1].tolist()
    for l in range(DEPTH):
        h = rmsnorm(x, pre_mix_norm[l])
        proj = h @ w_in[l]
        gq, gk, gv, glr, gg, sq, sk, sv = jnp.split(proj, split_points, axis=-1)

        gate_logits = (glr @ gla_gate_up[l] + gla_gate_bias[l]).astype(jnp.float32)
        log_a = (jax.nn.log_sigmoid(gate_logits) / GLA_TAU).reshape(B, T, GLA_HEADS, GLA_DK)
        o_gla = gla_chunked(gq.reshape(B, T, GLA_HEADS, GLA_DK),
                            gk.reshape(B, T, GLA_HEADS, GLA_DK),
                            gv.reshape(B, T, GLA_HEADS, GLA_DV), log_a)
        o_gla = rmsnorm(o_gla, gla_out_norm[l]) * jax.nn.silu(gg.reshape(B, T, GLA_HEADS, GLA_DV))
        o_gla = o_gla.reshape(B, T, GLA_V)

        q = partial_rotary(sq.reshape(B, T, SWA_HEADS, SWA_HD), positions)
        k = partial_rotary(sk.reshape(B, T, SWA_KV_HEADS, SWA_HD), positions)
        v = sv.reshape(B, T, SWA_KV_HEADS, SWA_HD)
        o_swa = swa_sink_attention(q, k, v, swa_sinks[l])

        mix = jnp.concatenate([o_gla, o_swa], axis=-1) @ w_out[l]
        x = x + rmsnorm(mix, post_mix_norm[l])

        h = rmsnorm(x, pre_ffn_norm[l])
        u = causal_depthwise_conv(h @ w_up[l], conv_w[l], conv_b[l])
        val, gate = jnp.split(u, 2, axis=-1)
        y = (jax.nn.gelu(gate, approximate=True) * val) @ w_down[l]
        x = x + rmsnorm(y, post_ffn_norm[l])
    return x


import jax as _jax
import jax.numpy as _jnp

TWIN_FORMAT = 'train_step'
FWD_PARAMS = ['x', 'positions', 'pre_mix_norm', 'w_in', 'gla_gate_up', 'gla_gate_bias', 'gla_out_norm', 'swa_sinks', 'w_out', 'post_mix_norm', 'pre_ffn_norm', 'w_up', 'conv_w', 'conv_b', 'w_down', 'post_ffn_norm']
TWIN_WEIGHTS = ['pre_mix_norm', 'w_in', 'gla_gate_up', 'gla_gate_bias', 'gla_out_norm', 'swa_sinks', 'w_out', 'post_mix_norm', 'pre_ffn_norm', 'w_up', 'conv_w', 'conv_b', 'w_down', 'post_ffn_norm']
TWIN_DIFF_INPUT = 'x'
TWIN_INPUTS = ['x', 'positions', 'pre_mix_norm', 'w_in', 'gla_gate_up', 'gla_gate_bias', 'gla_out_norm', 'swa_sinks', 'w_out', 'post_mix_norm', 'pre_ffn_norm', 'w_up', 'conv_w', 'conv_b', 'w_down', 'post_ffn_norm', 'loss_target', 'm_pre_mix_norm', 'm_w_in', 'm_gla_gate_up', 'm_gla_gate_bias', 'm_gla_out_norm', 'm_swa_sinks', 'm_w_out', 'm_post_mix_norm', 'm_pre_ffn_norm', 'm_w_up', 'm_conv_w', 'm_conv_b', 'm_w_down', 'm_post_ffn_norm', 'v_pre_mix_norm', 'v_w_in', 'v_gla_gate_up', 'v_gla_gate_bias', 'v_gla_out_norm', 'v_swa_sinks', 'v_w_out', 'v_post_mix_norm', 'v_pre_ffn_norm', 'v_w_up', 'v_conv_w', 'v_conv_b', 'v_w_down', 'v_post_ffn_norm']
TWIN_OUTPUTS = ['loss', 'grad_x', 'grad_pre_mix_norm', 'grad_w_in', 'grad_gla_gate_up', 'grad_gla_gate_bias', 'grad_gla_out_norm', 'grad_swa_sinks', 'grad_w_out', 'grad_post_mix_norm', 'grad_pre_ffn_norm', 'grad_w_up', 'grad_conv_w', 'grad_conv_b', 'grad_w_down', 'grad_post_ffn_norm', 'delta_pre_mix_norm', 'delta_w_in', 'delta_gla_gate_up', 'delta_gla_gate_bias', 'delta_gla_out_norm', 'delta_swa_sinks', 'delta_w_out', 'delta_post_mix_norm', 'delta_pre_ffn_norm', 'delta_w_up', 'delta_conv_w', 'delta_conv_b', 'delta_w_down', 'delta_post_ffn_norm', 'new_m_pre_mix_norm', 'new_m_w_in', 'new_m_gla_gate_up', 'new_m_gla_gate_bias', 'new_m_gla_out_norm', 'new_m_swa_sinks', 'new_m_w_out', 'new_m_post_mix_norm', 'new_m_pre_ffn_norm', 'new_m_w_up', 'new_m_conv_w', 'new_m_conv_b', 'new_m_w_down', 'new_m_post_ffn_norm', 'new_v_pre_mix_norm', 'new_v_w_in', 'new_v_gla_gate_up', 'new_v_gla_gate_bias', 'new_v_gla_out_norm', 'new_v_swa_sinks', 'new_v_w_out', 'new_v_post_mix_norm', 'new_v_pre_ffn_norm', 'new_v_w_up', 'new_v_conv_w', 'new_v_conv_b', 'new_v_w_down', 'new_v_post_ffn_norm']
TWIN_LEAF_KINDS = {'loss': 'loss', 'grad_x': 'grad_x', 'grad_pre_mix_norm': 'grad_w', 'grad_w_in': 'grad_w', 'grad_gla_gate_up': 'grad_w', 'grad_gla_gate_bias': 'grad_w', 'grad_gla_out_norm': 'grad_w', 'grad_swa_sinks': 'grad_w', 'grad_w_out': 'grad_w', 'grad_post_mix_norm': 'grad_w', 'grad_pre_ffn_norm': 'grad_w', 'grad_w_up': 'grad_w', 'grad_conv_w': 'grad_w', 'grad_conv_b': 'grad_w', 'grad_w_down': 'grad_w', 'grad_post_ffn_norm': 'grad_w', 'delta_pre_mix_norm': 'delta_w', 'delta_w_in': 'delta_w', 'delta_gla_gate_up': 'delta_w', 'delta_gla_gate_bias': 'delta_w', 'delta_gla_out_norm': 'delta_w', 'delta_swa_sinks': 'delta_w', 'delta_w_out': 'delta_w', 'delta_post_mix_norm': 'delta_w', 'delta_pre_ffn_norm': 'delta_w', 'delta_w_up': 'delta_w', 'delta_conv_w': 'delta_w', 'delta_conv_b': 'delta_w', 'delta_w_down': 'delta_w', 'delta_post_ffn_norm': 'delta_w', 'new_m_pre_mix_norm': 'new_m', 'new_m_w_in': 'new_m', 'new_m_gla_gate_up': 'new_m', 'new_m_gla_gate_bias': 'new_m', 'new_m_gla_out_norm': 'new_m', 'new_m_swa_sinks': 'new_m', 'new_m_w_out': 'new_m', 'new_m_post_mix_norm': 'new_m', 'new_m_pre_ffn_norm': 'new_m', 'new_m_w_up': 'new_m', 'new_m_conv_w': 'new_m', 'new_m_conv_b': 'new_m', 'new_m_w_down': 'new_m', 'new_m_post_ffn_norm': 'new_m', 'new_v_pre_mix_norm': 'new_v', 'new_v_w_in': 'new_v', 'new_v_gla_gate_up': 'new_v', 'new_v_gla_gate_bias': 'new_v', 'new_v_gla_out_norm': 'new_v', 'new_v_swa_sinks': 'new_v', 'new_v_w_out': 'new_v', 'new_v_post_mix_norm': 'new_v', 'new_v_pre_ffn_norm': 'new_v', 'new_v_w_up': 'new_v', 'new_v_conv_w': 'new_v', 'new_v_conv_b': 'new_v', 'new_v_w_down': 'new_v', 'new_v_post_ffn_norm': 'new_v'}


def _forward(args):
    return _fwd_reference(*[args[k] for k in FWD_PARAMS])


def _output_shape():
    out = _jax.eval_shape(lambda: _forward(_fwd_setup_inputs(0)))
    return out.shape, out.dtype

N_MICROBATCH = 1
ADAM_LR = 0.001
ADAM_B1 = 0.9
ADAM_B2 = 0.999
ADAM_EPS = 1e-08
ADAM_WD = 0.01
ADAM_STEP = 10
PER_EXAMPLE_BATCH_AXIS = {'x': 0, 'positions': 0, 'loss_target': 0}
SHARED_INPUTS = []
_WEIGHT_DTYPES = {'pre_mix_norm': _jnp.float32, 'w_in': _jnp.float32, 'gla_gate_up': _jnp.float32, 'gla_gate_bias': _jnp.float32, 'gla_out_norm': _jnp.float32, 'swa_sinks': _jnp.float32, 'w_out': _jnp.float32, 'post_mix_norm': _jnp.float32, 'pre_ffn_norm': _jnp.float32, 'w_up': _jnp.float32, 'conv_w': _jnp.float32, 'conv_b': _jnp.float32, 'w_down': _jnp.float32, 'post_ffn_norm': _jnp.float32}
MOMENT_SCALE = {'pre_mix_norm': 1.299478e+00, 'w_in': 8.754601e-01, 'gla_gate_up': 1.472023e-01, 'gla_gate_bias': 5.923386e-01, 'gla_out_norm': 2.305388e+00, 'swa_sinks': 1.262616e-01, 'w_out': 6.934278e-01, 'post_mix_norm': 6.366596e+01, 'pre_ffn_norm': 7.401526e-01, 'w_up': 3.098335e-01, 'conv_w': 3.474024e-01, 'conv_b': 5.128667e-01, 'w_down': 5.767909e-01, 'post_ffn_norm': 6.345588e+01}


def _to_microbatches(a, axis):
    t = _jnp.moveaxis(a, axis, 0)
    t = t.reshape((N_MICROBATCH, t.shape[0] // N_MICROBATCH) + t.shape[1:])
    return _jnp.moveaxis(t, 1, axis + 1)


def setup_inputs(seed: int = 0) -> dict:
    inp = _fwd_setup_inputs(seed)
    key = _jax.random.fold_in(_jax.random.key(seed), 7919)
    shape, _ = _output_shape()
    out = dict(inp)
    out["loss_target"] = _jax.random.normal(_jax.random.fold_in(key, 0), shape, _jnp.float32)
    for i, name in enumerate(TWIN_WEIGHTS):
        w = inp[name].astype(_jnp.float32)
        if MOMENT_SCALE is None:
            s = _jnp.sqrt(_jnp.mean(_jnp.square(w)) + 1e-30)
        else:
            s = MOMENT_SCALE[name]
        km, kv = _jax.random.split(_jax.random.fold_in(key, i + 1))
        out[name] = w
        out["m_" + name] = s * _jax.random.normal(km, w.shape, _jnp.float32)
        out["v_" + name] = (s * s) * _jax.random.uniform(kv, w.shape, _jnp.float32, 0.5, 1.5)
    if N_MICROBATCH > 1:
        for name, axis in PER_EXAMPLE_BATCH_AXIS.items():
            out[name] = _to_microbatches(out[name], axis)
    return {'x': out['x'], 'positions': out['positions'], 'pre_mix_norm': out['pre_mix_norm'], 'w_in': out['w_in'], 'gla_gate_up': out['gla_gate_up'], 'gla_gate_bias': out['gla_gate_bias'], 'gla_out_norm': out['gla_out_norm'], 'swa_sinks': out['swa_sinks'], 'w_out': out['w_out'], 'post_mix_norm': out['post_mix_norm'], 'pre_ffn_norm': out['pre_ffn_norm'], 'w_up': out['w_up'], 'conv_w': out['conv_w'], 'conv_b': out['conv_b'], 'w_down': out['w_down'], 'post_ffn_norm': out['post_ffn_norm'], 'loss_target': out['loss_target'], 'm_pre_mix_norm': out['m_pre_mix_norm'], 'm_w_in': out['m_w_in'], 'm_gla_gate_up': out['m_gla_gate_up'], 'm_gla_gate_bias': out['m_gla_gate_bias'], 'm_gla_out_norm': out['m_gla_out_norm'], 'm_swa_sinks': out['m_swa_sinks'], 'm_w_out': out['m_w_out'], 'm_post_mix_norm': out['m_post_mix_norm'], 'm_pre_ffn_norm': out['m_pre_ffn_norm'], 'm_w_up': out['m_w_up'], 'm_conv_w': out['m_conv_w'], 'm_conv_b': out['m_conv_b'], 'm_w_down': out['m_w_down'], 'm_post_ffn_norm': out['m_post_ffn_norm'], 'v_pre_mix_norm': out['v_pre_mix_norm'], 'v_w_in': out['v_w_in'], 'v_gla_gate_up': out['v_gla_gate_up'], 'v_gla_gate_bias': out['v_gla_gate_bias'], 'v_gla_out_norm': out['v_gla_out_norm'], 'v_swa_sinks': out['v_swa_sinks'], 'v_w_out': out['v_w_out'], 'v_post_mix_norm': out['v_post_mix_norm'], 'v_pre_ffn_norm': out['v_pre_ffn_norm'], 'v_w_up': out['v_w_up'], 'v_conv_w': out['v_conv_w'], 'v_conv_b': out['v_conv_b'], 'v_w_down': out['v_w_down'], 'v_post_ffn_norm': out['v_post_ffn_norm']}


def _loss(weights, diff, rest, loss_target):
    with _jax.named_scope("forward"):
        args = {**rest, TWIN_DIFF_INPUT: diff, **{k: w.astype(_WEIGHT_DTYPES[k]) for k, w in weights.items()}}
        y = _forward(args)
    with _jax.named_scope("loss_head"):
        err = _jnp.square(y.astype(_jnp.float32) - loss_target)
        return 0.5 * _jnp.sum(_jnp.mean(err, axis=-1)) if err.ndim else 0.5 * err


def _adamw(w, g, m, v):
    m = ADAM_B1 * m + (1.0 - ADAM_B1) * g
    v = ADAM_B2 * v + (1.0 - ADAM_B2) * _jnp.square(g)
    m_hat = m / (1.0 - ADAM_B1 ** ADAM_STEP)
    v_hat = v / (1.0 - ADAM_B2 ** ADAM_STEP)
    delta = -ADAM_LR * (m_hat / (_jnp.sqrt(v_hat) + ADAM_EPS) + ADAM_WD * w)
    return delta, m, v


def reference(x, positions, pre_mix_norm, w_in, gla_gate_up, gla_gate_bias, gla_out_norm, swa_sinks, w_out, post_mix_norm, pre_ffn_norm, w_up, conv_w, conv_b, w_down, post_ffn_norm, loss_target, m_pre_mix_norm, m_w_in, m_gla_gate_up, m_gla_gate_bias, m_gla_out_norm, m_swa_sinks, m_w_out, m_post_mix_norm, m_pre_ffn_norm, m_w_up, m_conv_w, m_conv_b, m_w_down, m_post_ffn_norm, v_pre_mix_norm, v_w_in, v_gla_gate_up, v_gla_gate_bias, v_gla_out_norm, v_swa_sinks, v_w_out, v_post_mix_norm, v_pre_ffn_norm, v_w_up, v_conv_w, v_conv_b, v_w_down, v_post_ffn_norm):
    given = dict(x=x, positions=positions, pre_mix_norm=pre_mix_norm, w_in=w_in, gla_gate_up=gla_gate_up, gla_gate_bias=gla_gate_bias, gla_out_norm=gla_out_norm, swa_sinks=swa_sinks, w_out=w_out, post_mix_norm=post_mix_norm, pre_ffn_norm=pre_ffn_norm, w_up=w_up, conv_w=conv_w, conv_b=conv_b, w_down=w_down, post_ffn_norm=post_ffn_norm, loss_target=loss_target, m_pre_mix_norm=m_pre_mix_norm, m_w_in=m_w_in, m_gla_gate_up=m_gla_gate_up, m_gla_gate_bias=m_gla_gate_bias, m_gla_out_norm=m_gla_out_norm, m_swa_sinks=m_swa_sinks, m_w_out=m_w_out, m_post_mix_norm=m_post_mix_norm, m_pre_ffn_norm=m_pre_ffn_norm, m_w_up=m_w_up, m_conv_w=m_conv_w, m_conv_b=m_conv_b, m_w_down=m_w_down, m_post_ffn_norm=m_post_ffn_norm, v_pre_mix_norm=v_pre_mix_norm, v_w_in=v_w_in, v_gla_gate_up=v_gla_gate_up, v_gla_gate_bias=v_gla_gate_bias, v_gla_out_norm=v_gla_out_norm, v_swa_sinks=v_swa_sinks, v_w_out=v_w_out, v_post_mix_norm=v_post_mix_norm, v_pre_ffn_norm=v_pre_ffn_norm, v_w_up=v_w_up, v_conv_w=v_conv_w, v_conv_b=v_conv_b, v_w_down=v_w_down, v_post_ffn_norm=v_post_ffn_norm)
    weights = {n: given[n] for n in TWIN_WEIGHTS}
    shared = {n: given[n] for n in SHARED_INPUTS}
    per_example = {n: given[n] for n in ['x', 'positions']}
    grad_fn = _jax.value_and_grad(_loss, argnums=(0, 1))

    def one_microbatch(ex, loss_target):
        ex = dict(ex)
        diff = ex.pop(TWIN_DIFF_INPUT)
        return grad_fn(weights, diff, {**shared, **ex}, loss_target)

    if N_MICROBATCH == 1:
        loss, (grad_w, grad_x) = one_microbatch(per_example, given["loss_target"])
    else:
        def body(carry, xs):
            loss_sum, grad_sum = carry
            l_k, (gw_k, gx_k) = one_microbatch(xs[0], xs[1])
            with _jax.named_scope("update"):
                return (loss_sum + l_k, _jax.tree.map(_jnp.add, grad_sum, gw_k)), gx_k

        init = (_jnp.zeros((), _jnp.float32), _jax.tree.map(_jnp.zeros_like, weights))
        (loss, grad_w), grad_x = _jax.lax.scan(body, init, (per_example, given["loss_target"]))
    with _jax.named_scope("update"):
        delta_w, new_m, new_v = {}, {}, {}
        for n in TWIN_WEIGHTS:
            delta_w[n], new_m[n], new_v[n] = _adamw(weights[n], grad_w[n], given["m_" + n], given["v_" + n])
    return (loss, grad_x, *[grad_w[n] for n in TWIN_WEIGHTS], *[delta_w[n] for n in TWIN_WEIGHTS],
            *[new_m[n] for n in TWIN_WEIGHTS], *[new_v[n] for n in TWIN_WEIGHTS])
```

```python
import functools

import jax
import jax.numpy as jnp
from jax import lax
from jax.experimental import pallas as pl
from jax.experimental.pallas import tpu as pltpu

F32 = jnp.float32
BF16 = jnp.bfloat16
MESH = pl.DeviceIdType.MESH

D_MODEL = 1024
GLA_HEADS = 4
GLA_DK = 64
GLA_DV = 128
GLA_TAU = 16.0
GLA_CHUNK = 64
SWA_HEADS = 8
SWA_HD = 64
SWA_BLOCK = 128
ROPE_THETA = 500000.0
ROPE_DIM = 16
D_FF = 2816
EPS = 1e-6
GLA_QK = 256
GLA_V = 512
SWA_Q = 512
SWA_KV = 128
IN_WIDTH = 2320
IN_WIDTH_PAD = 2432
N_SHARD = 4

ADAM_LR = 0.001
ADAM_B1 = 0.9
ADAM_B2 = 0.999
ADAM_EPS = 1e-08
ADAM_WD = 0.01
ADAM_STEP = 10

LANES = 128
VMEM_LIMIT = 56 * 1024 * 1024
TM = 256
FF_CHUNK = 256
N_FF_CHUNK = D_FF // FF_CHUNK
GLA_BLOCK = 256

PACK_COLS = 1024
PACK_ROWS = 2976
PACK_HALF = PACK_ROWS // 2
SMALL_ROWS = 80


def _params(**kw):
    return pltpu.CompilerParams(vmem_limit_bytes=VMEM_LIMIT, **kw)


def _mm(a, b):
    return lax.dot_general(a.astype(BF16), b.astype(BF16), (((1,), (0,)), ((), ())), preferred_element_type=F32)


def _mm_nt(a, b):
    return lax.dot_general(a.astype(BF16), b.astype(BF16), (((1,), (1,)), ((), ())), preferred_element_type=F32)


def _mm_tn(a, b):
    return lax.dot_general(a.astype(BF16), b.astype(BF16), (((0,), (0,)), ((), ())), preferred_element_type=F32)


def _mm_f32(a, b):
    return lax.dot_general(a, b, (((1,), (0,)), ((), ())), preferred_element_type=F32, precision=lax.Precision.HIGHEST)


def _iota(shape, dim):
    return lax.broadcasted_iota(jnp.int32, shape, dim)


def _sigmoid(x):
    return 1.0 / (1.0 + jnp.exp(-x))


def _gelu_parts(x):
    c = 0.7978845608028654
    x2 = x * x
    t = jnp.tanh(c * (x + 0.044715 * (x2 * x)))
    cdf = 0.5 * (1.0 + t)
    dcdf = 0.5 * (1.0 - t * t) * c * (1.0 + 3.0 * 0.044715 * x2)
    return x * cdf, cdf + x * dcdf


def _rms_bwd(v, r, g, dout):
    gd = g * dout
    return r * gd - v * (r * r * r) * jnp.mean(v * gd, axis=-1, keepdims=True)


def _row_spec(tm, cols):
    return pl.BlockSpec((tm, cols), lambda i: (i, 0))


def _const_spec(shape):
    return pl.BlockSpec(shape, lambda i: (0,) * len(shape))


def _any_spec():
    return pl.BlockSpec(memory_space=pl.ANY)


def _load_once(src_hbm, dst_vmem, sem):
    @pl.when(pl.program_id(0) == 0)
    def _():
        cp = pltpu.make_async_copy(src_hbm, dst_vmem, sem)
        cp.start()
        cp.wait()


def _rotate(v, rc, rsa, rsb):
    return v * rc + pltpu.roll(v, 120, 1) * rsa + pltpu.roll(v, 8, 1) * rsb


def _rotate_bwd(dv, rc, rsa, rsb):
    return dv * rc + pltpu.roll(dv * rsa, 8, 1) + pltpu.roll(dv * rsb, 120, 1)


def _proj_fwd(x, g1, wp, gup, gbias, rc, rsa, rsb):
    T = x.shape[0]

    def body(x_ref, g1_ref, wp_hbm, gup_ref, gb_ref, rc_ref, rsa_ref, rsb_ref,
             h1_ref, q_ref, k_ref, v_ref, la_ref, gg_ref, sq_ref, kd_ref, vd_ref, glr_ref, wp_v, sem):
        _load_once(wp_hbm, wp_v, sem)
        xt = x_ref[...]
        r = lax.rsqrt(jnp.mean(xt * xt, axis=-1, keepdims=True) + EPS)
        h = (xt * r * g1_ref[...]).astype(BF16)
        h1_ref[...] = h
        q_ref[...] = _mm(h, wp_v[:, 0:256])
        k_ref[...] = _mm(h, wp_v[:, 256:512])
        v_ref[...] = _mm(h, wp_v[:, 512:1024])
        gg_ref[...] = _mm(h, wp_v[:, 1024:1536])
        glr = _mm(h, wp_v[:, 2304:2432]).astype(BF16)
        glr_ref[...] = glr
        z = _mm(glr, gup_ref[...]) + gb_ref[...]
        la_ref[...] = (jnp.minimum(z, 0.0) - jnp.log1p(jnp.exp(-jnp.abs(z)))) * (1.0 / GLA_TAU)
        rc_, rsa_, rsb_ = rc_ref[...], rsa_ref[...], rsb_ref[...]
        for s in range(4):
            qs = _mm(h, wp_v[:, 1536 + 128 * s:1664 + 128 * s])
            sq_ref[:, 128 * s:128 * s + 128] = (_rotate(qs, rc_, rsa_, rsb_) * 0.125).astype(BF16)
        lane = _iota((TM, LANES), 1)
        first = lane < 64
        kr = _rotate(_mm(h, wp_v[:, 2048:2176]), rc_, rsa_, rsb_)
        krr = pltpu.roll(kr, 64, 1)
        kd_ref[:, 0:128] = jnp.where(first, kr, krr).astype(BF16)
        kd_ref[:, 128:256] = jnp.where(first, krr, kr).astype(BF16)
        vr = _mm(h, wp_v[:, 2176:2304])
        vrr = pltpu.roll(vr, 64, 1)
        vd_ref[:, 0:128] = jnp.where(first, vr, vrr).astype(BF16)
        vd_ref[:, 128:256] = jnp.where(first, vrr, vr).astype(BF16)

    outs = [
        jax.ShapeDtypeStruct((T, D_MODEL), BF16),
        jax.ShapeDtypeStruct((T, GLA_QK), F32),
        jax.ShapeDtypeStruct((T, GLA_QK), F32),
        jax.ShapeDtypeStruct((T, GLA_V), F32),
        jax.ShapeDtypeStruct((T, GLA_QK), F32),
        jax.ShapeDtypeStruct((T, GLA_V), F32),
        jax.ShapeDtypeStruct((T, SWA_Q), BF16),
        jax.ShapeDtypeStruct((T, 256), BF16),
        jax.ShapeDtypeStruct((T, 256), BF16),
        jax.ShapeDtypeStruct((T, LANES), BF16),
    ]
    return pl.pallas_call(
        body, name="proj_fwd", grid=(T // TM,), out_shape=outs,
        in_specs=[_row_spec(TM, D_MODEL), _const_spec((1, D_MODEL)), _any_spec(), _const_spec((LANES, GLA_QK)),
                  _const_spec((1, GLA_QK)), _row_spec(TM, LANES), _row_spec(TM, LANES), _row_spec(TM, LANES)],
        out_specs=[_row_spec(TM, o.shape[1]) for o in outs],
        scratch_shapes=[pltpu.VMEM((D_MODEL, IN_WIDTH_PAD), BF16), pltpu.SemaphoreType.DMA],
        compiler_params=_params(),
    )(x, g1, wp, gup, gbias, rc, rsa, rsb)


def _gla_masks():
    C = GLA_CHUNK
    lane = _iota((C, LANES), 1)
    lane_masks = [(lane < 64).astype(F32), (lane >= 64).astype(F32)]
    causal = _iota((C, C), 0) >= _iota((C, C), 1)
    blk = ((_iota((256, LANES), 0) >> 7) == (_iota((256, LANES), 1) >> 6)).astype(F32)
    return lane_masks, causal, blk


def _gla_chunk_terms(q_ref, k_ref, b_ref, c, p):
    C = GLA_CHUNK
    rows = slice(C * c, C * c + C)
    cols = slice(LANES * p, LANES * p + LANES)
    bc = b_ref[rows, cols]
    bl = b_ref[C * c + C - 1:C * c + C, cols]
    bm = b_ref[C * c + C // 2 - 1:C * c + C // 2, cols]
    qs = q_ref[rows, cols] * 0.125
    kk = k_ref[rows, cols]
    eb = jnp.exp(bc)
    ekl = jnp.exp(bl - bc)
    eqm = jnp.exp(bc - bm)
    ekm = jnp.exp(bm - bc)
    return qs, kk, eb, ekl, eqm, ekm, jnp.exp(bl)


def _block_cumsum(la):
    n = la.shape[0]
    row, col = _iota((n, n), 0), _iota((n, n), 1)
    tril = (((row >> 6) == (col >> 6)) & (col <= row)).astype(F32)
    return _mm_f32(tril, la)


def _gla_fwd(q, k, v, la):
    T = q.shape[0]
    NB = GLA_BLOCK // GLA_CHUNK
    C = GLA_CHUNK

    def body(q_ref, k_ref, v_ref, la_ref, o_ref, s_ref, st_ref, b_ref):
        @pl.when(pl.program_id(0) == 0)
        def _():
            st_ref[...] = jnp.zeros_like(st_ref)

        lane_masks, causal, blk = _gla_masks()
        b_ref[...] = _block_cumsum(la_ref[...])
        for c in range(NB):
            rows = slice(C * c, C * c + C)
            for p in range(2):
                qs, kk, eb, ekl, eqm, ekm, gam = _gla_chunk_terms(q_ref, k_ref, b_ref, c, p)
                qh, kh, qm, km = qs * eb, kk * ekl, qs * eqm, kk * ekm
                vp = v_ref[rows, 256 * p:256 * p + 256]
                st = st_ref[p]
                s_ref[c, p] = st
                o_inter = _mm_nt(qh, st)
                for j in range(2):
                    a = jnp.where(causal, _mm_nt(qm * lane_masks[j], km), 0.0)
                    o_ref[rows, 256 * p + 128 * j:256 * p + 128 * j + 128] = (
                        o_inter[:, 128 * j:128 * j + 128] + _mm(a, vp[:, 128 * j:128 * j + 128]))
                st_ref[p] = st * gam + blk * _mm_tn(vp, kh)

    return pl.pallas_call(
        body, name="gla_fwd", grid=(T // GLA_BLOCK,),
        out_shape=[jax.ShapeDtypeStruct((T, GLA_V), F32), jax.ShapeDtypeStruct((T // C, 2, 256, LANES), F32)],
        in_specs=[_row_spec(GLA_BLOCK, GLA_QK), _row_spec(GLA_BLOCK, GLA_QK), _row_spec(GLA_BLOCK, GLA_V),
                  _row_spec(GLA_BLOCK, GLA_QK)],
        out_specs=[_row_spec(GLA_BLOCK, GLA_V), pl.BlockSpec((NB, 2, 256, LANES), lambda i: (i, 0, 0, 0))],
        scratch_shapes=[pltpu.VMEM((2, 256, LANES), F32), pltpu.VMEM((GLA_BLOCK, GLA_QK), F32)],
        compiler_params=_params(),
    )(q, k, v, la)


def _gla_bwd(q, k, v, la, s_all, do):
    T = q.shape[0]
    NB = GLA_BLOCK // GLA_CHUNK
    C = GLA_CHUNK
    nblk = T // GLA_BLOCK

    def body(q_ref, k_ref, v_ref, la_ref, s_ref, do_ref, dq_ref, dk_ref, dv_ref, dz_ref, dst_ref, b_ref):
        @pl.when(pl.program_id(0) == 0)
        def _():
            dst_ref[...] = jnp.zeros_like(dst_ref)

        lane_masks, causal, blk = _gla_masks()
        triu = (_iota((C, C), 1) >= _iota((C, C), 0)).astype(F32)
        b_ref[...] = _block_cumsum(la_ref[...])
        for c in reversed(range(NB)):
            rows = slice(C * c, C * c + C)
            for p in range(2):
                cols = slice(LANES * p, LANES * p + LANES)
                qs, kk, eb, ekl, eqm, ekm, gam = _gla_chunk_terms(q_ref, k_ref, b_ref, c, p)
                qh, kh, qm, km = qs * eb, kk * ekl, qs * eqm, kk * ekm
                vp = v_ref[rows, 256 * p:256 * p + 256]
                dop = do_ref[rows, 256 * p:256 * p + 256]
                st = s_ref[c, p]
                dst = dst_ref[p]
                dqh = _mm(dop, st)
                dkh = _mm(vp, dst)
                dvp = _mm_nt(kh, dst)
                dgam = jnp.sum(st * dst, axis=0, keepdims=True)
                dqm = jnp.zeros((C, LANES), F32)
                dkm = jnp.zeros((C, LANES), F32)
                for j in range(2):
                    hs = slice(128 * j, 128 * j + 128)
                    a = jnp.where(causal, _mm_nt(qm * lane_masks[j], km), 0.0)
                    da = jnp.where(causal, _mm_nt(dop[:, hs], vp[:, hs]), 0.0)
                    dv_ref[rows, 256 * p + 128 * j:256 * p + 128 * j + 128] = dvp[:, hs] + _mm_tn(a, dop[:, hs])
                    dqm = dqm + lane_masks[j] * _mm(da, km)
                    dkm = dkm + lane_masks[j] * _mm_tn(da, qm)
                dqs = dqh * eb + dqm * eqm
                dk = dkh * ekl + dkm * ekm
                db = dqs * qs - dk * kk
                extra = jnp.sum(dkh * kh, axis=0, keepdims=True) + dgam * gam
                dg = _mm_f32(triu, db) + extra
                dq_ref[rows, cols] = dqs * 0.125
                dk_ref[rows, cols] = dk
                dz_ref[rows, cols] = dg * (1.0 - jnp.exp(GLA_TAU * la_ref[rows, cols])) * (1.0 / GLA_TAU)
                dst_ref[p] = dst * gam + blk * _mm_tn(dop, qh)

    rev = lambda i: (nblk - 1 - i, 0)
    rspec = lambda cols: pl.BlockSpec((GLA_BLOCK, cols), rev)
    return pl.pallas_call(
        body, name="gla_bwd", grid=(nblk,),
        out_shape=[jax.ShapeDtypeStruct((T, GLA_QK), F32), jax.ShapeDtypeStruct((T, GLA_QK), F32),
                   jax.ShapeDtypeStruct((T, GLA_V), F32), jax.ShapeDtypeStruct((T, GLA_QK), F32)],
        in_specs=[rspec(GLA_QK), rspec(GLA_QK), rspec(GLA_V), rspec(GLA_QK),
                  pl.BlockSpec((NB, 2, 256, LANES), lambda i: (nblk - 1 - i, 0, 0, 0)), rspec(GLA_V)],
        out_specs=[rspec(GLA_QK), rspec(GLA_QK), rspec(GLA_V), rspec(GLA_QK)],
        scratch_shapes=[pltpu.VMEM((2, 256, LANES), F32), pltpu.VMEM((GLA_BLOCK, GLA_QK), F32)],
        compiler_params=_params(),
    )(q, k, v, la, s_all, do)


def _swa_masks(i):
    W = SWA_BLOCK
    r, c = _iota((W, W), 0), _iota((W, W), 1)
    return (c > r) & (i > 0), c <= r


def _swa_probs(qh, kp, kc, mask_p, mask_c, sink):
    neg = -1e30
    s_p = jnp.where(mask_p, _mm_nt(qh, kp), neg)
    s_c = jnp.where(mask_c, _mm_nt(qh, kc), neg)
    m = jnp.maximum(jnp.maximum(jnp.max(s_p, axis=-1, keepdims=True), jnp.max(s_c, axis=-1, keepdims=True)), sink)
    p_p = jnp.where(mask_p, jnp.exp(s_p - m), 0.0)
    p_c = jnp.where(mask_c, jnp.exp(s_c - m), 0.0)
    p_s = jnp.exp(sink - m)
    denom = jnp.sum(p_p, axis=-1, keepdims=True) + jnp.sum(p_c, axis=-1, keepdims=True) + p_s
    return p_p, p_c, p_s, denom


def _swa_fwd(sq, kd, vd, sinks):
    T = sq.shape[0]
    W = SWA_BLOCK
    prev = lambda i: (jnp.maximum(i - 1, 0), 0)

    def body(sink_ref, q_ref, kp_ref, kc_ref, vp_ref, vc_ref, o_ref):
        i = pl.program_id(0)
        mask_p, mask_c = _swa_masks(i)
        first = _iota((W, LANES), 1) < 64
        zero = jnp.zeros((W, LANES), BF16)
        for m in range(4):
            g = m // 2
            gs = slice(128 * g, 128 * g + 128)
            qpair = q_ref[:, 128 * m:128 * m + 128]
            res = []
            for jj in range(2):
                qh = jnp.where(first, qpair, zero) if jj == 0 else jnp.where(first, zero, qpair)
                p_p, p_c, _, denom = _swa_probs(qh, kp_ref[:, gs], kc_ref[:, gs], mask_p, mask_c, sink_ref[2 * m + jj])
                res.append((_mm(p_p, vp_ref[:, gs]) + _mm(p_c, vc_ref[:, gs])) / denom)
            o_ref[:, 128 * m:128 * m + 128] = jnp.where(first, res[0], res[1]).astype(BF16)

    return pl.pallas_call(
        body, name="swa_fwd", grid=(T // W,), out_shape=jax.ShapeDtypeStruct((T, SWA_Q), BF16),
        in_specs=[pl.BlockSpec(memory_space=pltpu.SMEM), _row_spec(W, SWA_Q), pl.BlockSpec((W, 256), prev),
                  _row_spec(W, 256), pl.BlockSpec((W, 256), prev), _row_spec(W, 256)],
        out_specs=_row_spec(W, SWA_Q),
        compiler_params=_params(),
    )(sinks, sq, kd, kd, vd, vd)


def _swa_bwd(sq, kd, vd, sinks, do):
    T = sq.shape[0]
    W = SWA_BLOCK
    n = T // W
    cur = lambda i: (jnp.minimum(i, n - 1), 0)
    prev = lambda i: (jnp.clip(i - 1, 0, n - 1), 0)

    def body(sink_ref, q_ref, kp_ref, kc_ref, vp_ref, vc_ref, do_ref, dq_ref, dk_ref, dv_ref, ds_ref, ck_ref, cv_ref):
        i = pl.program_id(0)

        @pl.when(i == 0)
        def _():
            ds_ref[...] = jnp.zeros_like(ds_ref)
            ck_ref[...] = jnp.zeros_like(ck_ref)
            cv_ref[...] = jnp.zeros_like(cv_ref)

        @pl.when(i < n)
        def _():
            mask_p, mask_c = _swa_masks(i)
            first = _iota((W, LANES), 1) < 64
            zero = jnp.zeros((W, LANES), BF16)
            for g in range(2):
                gs = slice(128 * g, 128 * g + 128)
                kp, kc, vp, vc = kp_ref[:, gs], kc_ref[:, gs], vp_ref[:, gs], vc_ref[:, gs]
                dk_p = jnp.zeros((W, LANES), F32)
                dk_c = jnp.zeros((W, LANES), F32)
                dv_p = jnp.zeros((W, LANES), F32)
                dv_c = jnp.zeros((W, LANES), F32)
                for mm in range(2):
                    m = 2 * g + mm
                    qpair = q_ref[:, 128 * m:128 * m + 128]
                    dopair = do_ref[:, 128 * m:128 * m + 128]
                    dq_pair = []
                    for jj in range(2):
                        sel = first if jj == 0 else jnp.logical_not(first)
                        qh = jnp.where(sel, qpair, zero)
                        doh = jnp.where(sel, dopair, zero)
                        p_p, p_c, p_s, denom = _swa_probs(qh, kp, kc, mask_p, mask_c, sink_ref[2 * m + jj])
                        inv = 1.0 / denom
                        p_p, p_c, p_s = p_p * inv, p_c * inv, p_s * inv
                        dp_p = _mm_nt(doh, vp)
                        dp_c = _mm_nt(doh, vc)
                        delta = jnp.sum(p_p * dp_p, axis=-1, keepdims=True) + jnp.sum(p_c * dp_c, axis=-1, keepdims=True)
                        ds_p = p_p * (dp_p - delta)
                        ds_c = p_c * (dp_c - delta)
                        h = 2 * m + jj
                        ds_ref[h:h + 1, :] = ds_ref[h:h + 1, :] - jnp.sum(p_s * delta)
                        dq_pair.append((_mm(ds_p, kp) + _mm(ds_c, kc)) * 0.125)
                        dk_p = dk_p + _mm_tn(ds_p, qh)
                        dk_c = dk_c + _mm_tn(ds_c, qh)
                        dv_p = dv_p + _mm_tn(p_p, doh)
                        dv_c = dv_c + _mm_tn(p_c, doh)
                    dq_ref[:, 128 * m:128 * m + 128] = jnp.where(first, dq_pair[0], dq_pair[1])
                dk_ref[:, gs] = ck_ref[:, gs] + dk_p
                dv_ref[:, gs] = cv_ref[:, gs] + dv_p
                ck_ref[:, gs] = dk_c
                cv_ref[:, gs] = dv_c

        @pl.when(i == n)
        def _():
            dk_ref[...] = ck_ref[...]
            dv_ref[...] = cv_ref[...]

    return pl.pallas_call(
        body, name="swa_bwd", grid=(n + 1,),
        out_shape=[jax.ShapeDtypeStruct((T, SWA_Q), F32), jax.ShapeDtypeStruct((T, 256), F32),
                   jax.ShapeDtypeStruct((T, 256), F32), jax.ShapeDtypeStruct((8, LANES), F32)],
        in_specs=[pl.BlockSpec(memory_space=pltpu.SMEM), pl.BlockSpec((W, SWA_Q), cur), pl.BlockSpec((W, 256), prev),
                  pl.BlockSpec((W, 256), cur), pl.BlockSpec((W, 256), prev), pl.BlockSpec((W, 256), cur),
                  pl.BlockSpec((W, SWA_Q), cur)],
        out_specs=[pl.BlockSpec((W, SWA_Q), cur), pl.BlockSpec((W, 256), prev), pl.BlockSpec((W, 256), prev),
                   _const_spec((8, LANES))],
        scratch_shapes=[pltpu.VMEM((W, 256), F32), pltpu.VMEM((W, 256), F32)],
        compiler_params=_params(),
    )(sinks, sq, kd, kd, vd, vd, do)


def _mix_out_fwd(x, og, gg, osw, gnorm, wout, g2):
    T = x.shape[0]

    def body(x_ref, og_ref, gg_ref, osw_ref, gn_ref, wout_ref, g2_ref, x1_ref, cat_ref, mix_ref):
        gn = gn_ref[...]
        for j in range(GLA_HEADS):
            hs = slice(128 * j, 128 * j + 128)
            o = og_ref[:, hs]
            r = lax.rsqrt(jnp.mean(o * o, axis=-1, keepdims=True) + EPS)
            gate = gg_ref[:, hs]
            cat_ref[:, hs] = (o * r * gn * (gate * _sigmoid(gate))).astype(BF16)
        cat_ref[:, GLA_V:] = osw_ref[...]
        mix = _mm(cat_ref[...], wout_ref[...])
        mix_ref[...] = mix
        r2 = lax.rsqrt(jnp.mean(mix * mix, axis=-1, keepdims=True) + EPS)
        x1_ref[...] = x_ref[...] + mix * r2 * g2_ref[...]

    return pl.pallas_call(
        body, name="mix_out_fwd", grid=(T // TM,),
        out_shape=[jax.ShapeDtypeStruct((T, D_MODEL), F32), jax.ShapeDtypeStruct((T, D_MODEL), BF16),
                   jax.ShapeDtypeStruct((T, D_MODEL), F32)],
        in_specs=[_row_spec(TM, D_MODEL), _row_spec(TM, GLA_V), _row_spec(TM, GLA_V), _row_spec(TM, SWA_Q),
                  _const_spec((1, LANES)), _const_spec((D_MODEL, D_MODEL)), _const_spec((1, D_MODEL))],
        out_specs=[_row_spec(TM, D_MODEL), _row_spec(TM, D_MODEL), _row_spec(TM, D_MODEL)],
        compiler_params=_params(),
    )(x, og, gg, osw, gnorm, wout, g2)


def _conv_fwd(up, prev1, prev2, w0, w1, w2, bias):
    row = _iota(up.shape, 0)
    m1 = jnp.where(row == 0, prev1, pltpu.roll(up, 1, 0))
    m2 = jnp.where(row == 0, prev2, jnp.where(row == 1, prev1, pltpu.roll(up, 2, 0)))
    return bias + w0 * m2 + w1 * m1 + w2 * up, m1, m2


def _ffn_fwd(x1, g3, wup, cw, cb, wdown, g4, target):
    T = x1.shape[0]

    def body(x1_ref, g3_ref, wup_hbm, cw_ref, cb_ref, wdn_hbm, g4_ref, tg_ref,
             h2_ref, up_ref, a_ref, y_ref, dx2_ref, loss_ref, tail_ref, wup_v, wdn_v, carry_ref, sems):
        _load_once(wup_hbm, wup_v, sems.at[0])
        _load_once(wdn_hbm, wdn_v, sems.at[1])

        @pl.when(pl.program_id(0) == 0)
        def _():
            carry_ref[...] = jnp.zeros_like(carry_ref)
            loss_ref[...] = jnp.zeros_like(loss_ref)

        x1 = x1_ref[...]
        r3 = lax.rsqrt(jnp.mean(x1 * x1, axis=-1, keepdims=True) + EPS)
        h2 = (x1 * r3 * g3_ref[...]).astype(BF16)
        h2_ref[...] = h2
        y = jnp.zeros((TM, D_MODEL), F32)
        for c in range(N_FF_CHUNK):
            u = []
            for half in range(2):
                cs = slice(D_FF * half + FF_CHUNK * c, D_FF * half + FF_CHUNK * c + FF_CHUNK)
                upb = _mm(h2, wup_v[:, cs]).astype(BF16)
                up_ref[:, cs] = upb
                upf = upb.astype(F32)
                conv, _, _ = _conv_fwd(upf, carry_ref[7:8, cs], carry_ref[6:7, cs],
                                       cw_ref[0:1, cs], cw_ref[1:2, cs], cw_ref[2:3, cs], cb_ref[:, cs])
                carry_ref[:, cs] = upf[TM - 8:TM, :]
                tail_ref[0, :, cs] = upf[TM - 8:TM, :]
                u.append(conv)
            act, _ = _gelu_parts(u[1])
            a = (act * u[0]).astype(BF16)
            a_ref[:, FF_CHUNK * c:FF_CHUNK * c + FF_CHUNK] = a
            y = y + _mm(a, wdn_v[FF_CHUNK * c:FF_CHUNK * c + FF_CHUNK, :])
        y_ref[...] = y
        r4 = lax.rsqrt(jnp.mean(y * y, axis=-1, keepdims=True) + EPS)
        err = x1 + y * r4 * g4_ref[...] - tg_ref[...]
        dx2_ref[...] = err * (1.0 / D_MODEL)
        loss_ref[...] = loss_ref[...] + jnp.sum(err * err) * (0.5 / D_MODEL)

    outs = [
        jax.ShapeDtypeStruct((T, D_MODEL), BF16),
        jax.ShapeDtypeStruct((T, 2 * D_FF), BF16),
        jax.ShapeDtypeStruct((T, D_FF), BF16),
        jax.ShapeDtypeStruct((T, D_MODEL), F32),
        jax.ShapeDtypeStruct((T, D_MODEL), F32),
        jax.ShapeDtypeStruct((8, LANES), F32),
        jax.ShapeDtypeStruct((T // TM, 8, 2 * D_FF), F32),
    ]
    return pl.pallas_call(
        body, name="ffn_fwd", grid=(T // TM,), out_shape=outs,
        in_specs=[_row_spec(TM, D_MODEL), _const_spec((1, D_MODEL)), _any_spec(), _const_spec((3, 2 * D_FF)),
                  _const_spec((1, 2 * D_FF)), _any_spec(), _const_spec((1, D_MODEL)), _row_spec(TM, D_MODEL)],
        out_specs=[_row_spec(TM, D_MODEL), _row_spec(TM, 2 * D_FF), _row_spec(TM, D_FF), _row_spec(TM, D_MODEL),
                   _row_spec(TM, D_MODEL), _const_spec((8, LANES)),
                   pl.BlockSpec((1, 8, 2 * D_FF), lambda i: (i, 0, 0))],
        scratch_shapes=[pltpu.VMEM((D_MODEL, 2 * D_FF), BF16), pltpu.VMEM((D_FF, D_MODEL), BF16),
                        pltpu.VMEM((8, 2 * D_FF), F32), pltpu.SemaphoreType.DMA((2,))],
        compiler_params=_params(),
    )(x1, g3, wup, cw, cb, wdown, g4, target)


def _ffn_bwd(dx2, y, g4, up, tails, cw, cb, wdown, wup, x1, g3):
    T = dx2.shape[0]
    nt = T // TM
    rev = lambda i: (nt - 1 - i, 0)
    halo = lambda i: (jnp.maximum(nt - 2 - i, 0), 0, 0)

    def body(dn_ref, y_ref, g4_ref, up_ref, halo_ref, cw_ref, cb_ref, wdn_hbm, wup_hbm, x1_ref, g3_ref,
             dy_ref, dup_ref, dx1_ref, dg4_ref, dg3_ref, dcb_ref, dcw_ref, wup_v, wdn_v, carry_ref, sems):
        _load_once(wup_hbm, wup_v, sems.at[0])
        _load_once(wdn_hbm, wdn_v, sems.at[1])
        i = pl.program_id(0)

        @pl.when(i == 0)
        def _():
            carry_ref[...] = jnp.zeros_like(carry_ref)
            dg4_ref[...] = jnp.zeros_like(dg4_ref)
            dg3_ref[...] = jnp.zeros_like(dg3_ref)
            dcb_ref[...] = jnp.zeros_like(dcb_ref)
            dcw_ref[...] = jnp.zeros_like(dcw_ref)

        has_prev = (i < nt - 1).astype(F32)
        dn = dn_ref[...]
        y = y_ref[...]
        g4v = g4_ref[...]
        r4 = lax.rsqrt(jnp.mean(y * y, axis=-1, keepdims=True) + EPS)
        dg4_ref[...] = dg4_ref[...] + jnp.sum(dn * y * r4, axis=0, keepdims=True)
        dy = _rms_bwd(y, r4, g4v, dn).astype(BF16)
        dy_ref[...] = dy
        row = _iota((TM, FF_CHUNK), 0)
        dh2 = jnp.zeros((TM, D_MODEL), F32)
        for c in range(N_FF_CHUNK):
            da = _mm_nt(dy, wdn_v[FF_CHUNK * c:FF_CHUNK * c + FF_CHUNK, :])
            u, m1s, m2s, ups, css = [], [], [], [], []
            for half in range(2):
                cs = slice(D_FF * half + FF_CHUNK * c, D_FF * half + FF_CHUNK * c + FF_CHUNK)
                upf = up_ref[:, cs].astype(F32)
                p1 = halo_ref[0, 7:8, cs] * has_prev
                p2 = halo_ref[0, 6:7, cs] * has_prev
                conv, m1, m2 = _conv_fwd(upf, p1, p2, cw_ref[0:1, cs], cw_ref[1:2, cs], cw_ref[2:3, cs], cb_ref[:, cs])
                u.append(conv)
                m1s.append(m1)
                m2s.append(m2)
                ups.append(upf)
                css.append(cs)
            act, dact = _gelu_parts(u[1])
            dus = [da * act, da * u[0] * dact]
            for half in range(2):
                cs, du = css[half], dus[half]
                dcb_ref[:, cs] = dcb_ref[:, cs] + jnp.sum(du, axis=0, keepdims=True)
                dcw_ref[0:1, cs] = dcw_ref[0:1, cs] + jnp.sum(du * m2s[half], axis=0, keepdims=True)
                dcw_ref[1:2, cs] = dcw_ref[1:2, cs] + jnp.sum(du * m1s[half], axis=0, keepdims=True)
                dcw_ref[2:3, cs] = dcw_ref[2:3, cs] + jnp.sum(du * ups[half], axis=0, keepdims=True)
                n1 = carry_ref[0:1, cs]
                n2 = carry_ref[1:2, cs]
                p1 = jnp.where(row == TM - 1, n1, pltpu.roll(du, TM - 1, 0))
                p2 = jnp.where(row == TM - 1, n2, jnp.where(row == TM - 2, n1, pltpu.roll(du, TM - 2, 0)))
                carry_ref[:, cs] = du[0:8, :]
                dup = (cw_ref[2:3, cs] * du + cw_ref[1:2, cs] * p1 + cw_ref[0:1, cs] * p2).astype(BF16)
                dup_ref[:, cs] = dup
                dh2 = dh2 + _mm_nt(dup, wup_v[:, cs])
        x1 = x1_ref[...]
        r3 = lax.rsqrt(jnp.mean(x1 * x1, axis=-1, keepdims=True) + EPS)
        dg3_ref[...] = dg3_ref[...] + jnp.sum(dh2 * x1 * r3, axis=0, keepdims=True)
        dx1_ref[...] = dn + _rms_bwd(x1, r3, g3_ref[...], dh2)

    outs = [
        jax.ShapeDtypeStruct((T, D_MODEL), BF16),
        jax.ShapeDtypeStruct((T, 2 * D_FF), BF16),
        jax.ShapeDtypeStruct((T, D_MODEL), F32),
        jax.ShapeDtypeStruct((1, D_MODEL), F32),
        jax.ShapeDtypeStruct((1, D_MODEL), F32),
        jax.ShapeDtypeStruct((1, 2 * D_FF), F32),
        jax.ShapeDtypeStruct((3, 2 * D_FF), F32),
    ]
    return pl.pallas_call(
        body, name="ffn_bwd", grid=(nt,), out_shape=outs,
        in_specs=[pl.BlockSpec((TM, D_MODEL), rev), pl.BlockSpec((TM, D_MODEL), rev), _const_spec((1, D_MODEL)),
                  pl.BlockSpec((TM, 2 * D_FF), rev), pl.BlockSpec((1, 8, 2 * D_FF), halo), _const_spec((3, 2 * D_FF)),
                  _const_spec((1, 2 * D_FF)), _any_spec(), _any_spec(), pl.BlockSpec((TM, D_MODEL), rev),
                  _const_spec((1, D_MODEL))],
        out_specs=[pl.BlockSpec((TM, D_MODEL), rev), pl.BlockSpec((TM, 2 * D_FF), rev), pl.BlockSpec((TM, D_MODEL), rev),
                   _const_spec((1, D_MODEL)), _const_spec((1, D_MODEL)), _const_spec((1, 2 * D_FF)),
                   _const_spec((3, 2 * D_FF))],
        scratch_shapes=[pltpu.VMEM((D_MODEL, 2 * D_FF), BF16), pltpu.VMEM((D_FF, D_MODEL), BF16),
                        pltpu.VMEM((8, 2 * D_FF), F32), pltpu.SemaphoreType.DMA((2,))],
        compiler_params=_params(),
    )(dx2, y, g4, up, tails, cw, cb, wdown, wup, x1, g3)


def _mix_out_bwd(dx1, mix, g2, wout, og, gg, gnorm):
    T = dx1.shape[0]

    def body(dx1_ref, mix_ref, g2_ref, wout_ref, og_ref, gg_ref, gn_ref,
             dmix_ref, dog_ref, dgg_ref, dosw_ref, dg2_ref, dgn_ref):
        @pl.when(pl.program_id(0) == 0)
        def _():
            dg2_ref[...] = jnp.zeros_like(dg2_ref)
            dgn_ref[...] = jnp.zeros_like(dgn_ref)

        dx1 = dx1_ref[...]
        mix = mix_ref[...]
        r2 = lax.rsqrt(jnp.mean(mix * mix, axis=-1, keepdims=True) + EPS)
        dg2_ref[...] = dg2_ref[...] + jnp.sum(dx1 * mix * r2, axis=0, keepdims=True)
        dmix = _rms_bwd(mix, r2, g2_ref[...], dx1).astype(BF16)
        dmix_ref[...] = dmix
        dcat = _mm_nt(dmix, wout_ref[...])
        dosw_ref[...] = dcat[:, GLA_V:].astype(BF16)
        gn = gn_ref[...]
        dgn = jnp.zeros((1, LANES), F32)
        for j in range(GLA_HEADS):
            hs = slice(128 * j, 128 * j + 128)
            o = og_ref[:, hs]
            r = lax.rsqrt(jnp.mean(o * o, axis=-1, keepdims=True) + EPS)
            gate = gg_ref[:, hs]
            sg = _sigmoid(gate)
            dgated = dcat[:, hs]
            dnorm = dgated * (gate * sg)
            dgg_ref[:, hs] = dgated * (o * r * gn) * (sg * (1.0 + gate * (1.0 - sg)))
            dgn = dgn + jnp.sum(dnorm * o * r, axis=0, keepdims=True)
            dog_ref[:, hs] = _rms_bwd(o, r, gn, dnorm)
        dgn_ref[...] = dgn_ref[...] + dgn

    return pl.pallas_call(
        body, name="mix_out_bwd", grid=(T // TM,),
        out_shape=[jax.ShapeDtypeStruct((T, D_MODEL), BF16), jax.ShapeDtypeStruct((T, GLA_V), F32),
                   jax.ShapeDtypeStruct((T, GLA_V), F32), jax.ShapeDtypeStruct((T, SWA_Q), BF16),
                   jax.ShapeDtypeStruct((1, D_MODEL), F32), jax.ShapeDtypeStruct((1, LANES), F32)],
        in_specs=[_row_spec(TM, D_MODEL), _row_spec(TM, D_MODEL), _const_spec((1, D_MODEL)),
                  _const_spec((D_MODEL, D_MODEL)), _row_spec(TM, GLA_V), _row_spec(TM, GLA_V), _const_spec((1, LANES))],
        out_specs=[_row_spec(TM, D_MODEL), _row_spec(TM, GLA_V), _row_spec(TM, GLA_V), _row_spec(TM, SWA_Q),
                   _const_spec((1, D_MODEL)), _const_spec((1, LANES))],
        compiler_params=_params(),
    )(dx1, mix, g2, wout, og, gg, gnorm)


def _proj_bwd(x, g1, wp, gup, glr, dq, dk, dv, dgg, dsq, dkd, dvd, dz, rc, rsa, rsb, dx1):
    T = x.shape[0]

    def body(x_ref, g1_ref, wp_hbm, gup_ref, glr_ref, dq_ref, dk_ref, dv_ref, dgg_ref, dsq_ref, dkd_ref, dvd_ref,
             dz_ref, rc_ref, rsa_ref, rsb_ref, dx1_ref, dx_ref, dp_ref, dg1_ref, dgup_ref, dgb_ref, wp_v, sem):
        _load_once(wp_hbm, wp_v, sem)

        @pl.when(pl.program_id(0) == 0)
        def _():
            dg1_ref[...] = jnp.zeros_like(dg1_ref)
            dgup_ref[...] = jnp.zeros_like(dgup_ref)
            dgb_ref[...] = jnp.zeros_like(dgb_ref)

        rc_, rsa_, rsb_ = rc_ref[...], rsa_ref[...], rsb_ref[...]
        dp_ref[:, 0:256] = dq_ref[...].astype(BF16)
        dp_ref[:, 256:512] = dk_ref[...].astype(BF16)
        dp_ref[:, 512:1024] = dv_ref[...].astype(BF16)
        dp_ref[:, 1024:1536] = dgg_ref[...].astype(BF16)
        for s in range(4):
            cs = slice(128 * s, 128 * s + 128)
            dp_ref[:, 1536 + 128 * s:1664 + 128 * s] = _rotate_bwd(dsq_ref[:, cs], rc_, rsa_, rsb_).astype(BF16)
        first = _iota((TM, LANES), 1) < 64
        dk0 = dkd_ref[:, 0:128]
        dk1 = dkd_ref[:, 128:256]
        dkr = jnp.where(first, dk0 + pltpu.roll(dk0, 64, 1), dk1 + pltpu.roll(dk1, 64, 1))
        dp_ref[:, 2048:2176] = _rotate_bwd(dkr, rc_, rsa_, rsb_).astype(BF16)
        dv0 = dvd_ref[:, 0:128]
        dv1 = dvd_ref[:, 128:256]
        dp_ref[:, 2176:2304] = jnp.where(first, dv0 + pltpu.roll(dv0, 64, 1), dv1 + pltpu.roll(dv1, 64, 1)).astype(BF16)
        dz = dz_ref[...]
        dzb = dz.astype(BF16)
        dp_ref[:, 2304:2432] = _mm_nt(dzb, gup_ref[...]).astype(BF16)
        dgup_ref[...] = dgup_ref[...] + _mm_tn(glr_ref[...], dzb)
        dgb_ref[...] = dgb_ref[...] + jnp.sum(dz, axis=0, keepdims=True)
        dh1 = _mm_nt(dp_ref[...], wp_v[...])
        xt = x_ref[...]
        r = lax.rsqrt(jnp.mean(xt * xt, axis=-1, keepdims=True) + EPS)
        dg1_ref[...] = dg1_ref[...] + jnp.sum(dh1 * xt * r, axis=0, keepdims=True)
        dx_ref[...] = dx1_ref[...] + _rms_bwd(xt, r, g1_ref[...], dh1)

    row = lambda cols: _row_spec(TM, cols)
    return pl.pallas_call(
        body, name="proj_bwd", grid=(T // TM,),
        out_shape=[jax.ShapeDtypeStruct((T, D_MODEL), F32), jax.ShapeDtypeStruct((T, IN_WIDTH_PAD), BF16),
                   jax.ShapeDtypeStruct((1, D_MODEL), F32), jax.ShapeDtypeStruct((LANES, GLA_QK), F32),
                   jax.ShapeDtypeStruct((1, GLA_QK), F32)],
        in_specs=[row(D_MODEL), _const_spec((1, D_MODEL)), _any_spec(), _const_spec((LANES, GLA_QK)), row(LANES),
                  row(GLA_QK), row(GLA_QK), row(GLA_V), row(GLA_V), row(SWA_Q), row(256), row(256), row(GLA_QK),
                  row(LANES), row(LANES), row(LANES), row(D_MODEL)],
        out_specs=[row(D_MODEL), row(IN_WIDTH_PAD), _const_spec((1, D_MODEL)), _const_spec((LANES, GLA_QK)),
                   _const_spec((1, GLA_QK))],
        scratch_shapes=[pltpu.VMEM((D_MODEL, IN_WIDTH_PAD), BF16), pltpu.SemaphoreType.DMA],
        compiler_params=_params(),
    )(x, g1, wp, gup, glr, dq, dk, dv, dgg, dsq, dkd, dvd, dz, rc, rsa, rsb, dx1)


def _matmul_tn(a, b, tn, name):
    T, M = a.shape
    N = b.shape[1]
    tk = min(512, T)
    nk = T // tk

    def body(a_ref, b_ref, o_ref):
        kk = pl.program_id(1)

        @pl.when(kk == 0)
        def _():
            o_ref[...] = jnp.zeros_like(o_ref)

        o_ref[...] = o_ref[...] + _mm_tn(a_ref[...], b_ref[...])

    return pl.pallas_call(
        body, name=name, grid=(N // tn, nk), out_shape=jax.ShapeDtypeStruct((M, N), F32),
        in_specs=[pl.BlockSpec((tk, M), lambda j, kk: (kk, 0)), pl.BlockSpec((tk, tn), lambda j, kk: (kk, j))],
        out_specs=pl.BlockSpec((M, tn), lambda j, kk: (0, j)),
        compiler_params=_params(),
    )(a, b)


def _adamw(w, g, m, v, rows, name):
    R, C = w.shape

    def body(w_ref, g_ref, m_ref, v_ref, d_ref, m2_ref, v2_ref):
        g_ = g_ref[...]
        m2 = ADAM_B1 * m_ref[...] + (1.0 - ADAM_B1) * g_
        v2 = ADAM_B2 * v_ref[...] + (1.0 - ADAM_B2) * (g_ * g_)
        m_hat = m2 / (1.0 - ADAM_B1 ** ADAM_STEP)
        v_hat = v2 / (1.0 - ADAM_B2 ** ADAM_STEP)
        d_ref[...] = -ADAM_LR * (m_hat / (jnp.sqrt(v_hat) + ADAM_EPS) + ADAM_WD * w_ref[...])
        m2_ref[...] = m2
        v2_ref[...] = v2

    spec = pl.BlockSpec((rows, C), lambda i: (i, 0))
    return pl.pallas_call(
        body, name=name, grid=(R // rows,), out_shape=[jax.ShapeDtypeStruct((R, C), F32)] * 3,
        in_specs=[spec] * 4, out_specs=[spec] * 3, compiler_params=_params(),
    )(w, g, m, v)


def _place():
    x, y, c = lax.axis_index("x"), lax.axis_index("y"), lax.axis_index("c")
    chips = [(1 - x, y), (x, 1 - y), (1 - x, 1 - y)]
    return x, y, c, chips


def _allgather_weights(wpack):
    H = PACK_HALF

    def body(w_ref, out_ref, send_sems, recv_sems, local_sem):
        x, y, c, chips = _place()
        sibling = (x, y, 1 - c)

        def block(px, py, half):
            return out_ref.at[2 * px + py, pl.ds(half * H, H), :]

        def copy(k, px, py, half, to, src=None):
            return pltpu.make_async_remote_copy(
                src_ref=block(px, py, half) if src is None else src, dst_ref=block(px, py, half),
                send_sem=send_sems.at[k], recv_sem=recv_sems.at[k], device_id=to, device_id_type=MESH)

        mine = pltpu.make_async_copy(w_ref, out_ref.at[2 * x + y], local_sem)
        mine.start()
        first = [copy(j, x, y, c, (*chip, c), src=w_ref.at[pl.ds(c * H, H), :]) for j, chip in enumerate(chips)]
        for cp in first:
            cp.start()
        passed = [copy(3 + j, *chip, c, sibling) for j, chip in enumerate(chips)]
        for j, chip in enumerate(chips):
            copy(j, *chip, c, (x, y, c)).wait_recv()
            passed[j].start()
        for j, chip in enumerate(chips):
            copy(3 + j, *chip, 1 - c, (x, y, c)).wait_recv()
        for cp in first + passed:
            cp.wait_send()
        mine.wait()

    return pl.pallas_call(
        body, name="allgather_weights", out_shape=jax.ShapeDtypeStruct((N_SHARD, PACK_ROWS, PACK_COLS), BF16),
        in_specs=[_any_spec()], out_specs=_any_spec(),
        scratch_shapes=[pltpu.SemaphoreType.DMA((6,)), pltpu.SemaphoreType.DMA((6,)), pltpu.SemaphoreType.DMA],
    )(wpack)


def _rs_pair_swap(gpack):
    H = PACK_HALF

    def body(g_ref, own_ref, got_ref, send_sem, recv_sem, local_sem):
        x, y, c, _ = _place()
        mine = pltpu.make_async_copy(g_ref.at[:, pl.ds(c * H, H), :], own_ref, local_sem)
        mine.start()
        swap = pltpu.make_async_remote_copy(
            src_ref=g_ref.at[:, pl.ds((1 - c) * H, H), :], dst_ref=got_ref, send_sem=send_sem, recv_sem=recv_sem,
            device_id=(x, y, 1 - c), device_id_type=MESH)
        swap.start()
        swap.wait()
        mine.wait()

    shape = jax.ShapeDtypeStruct((N_SHARD, H, PACK_COLS), F32)
    return pl.pallas_call(
        body, name="rs_pair_swap", out_shape=[shape, shape], in_specs=[_any_spec()], out_specs=[_any_spec(), _any_spec()],
        scratch_shapes=[pltpu.SemaphoreType.DMA, pltpu.SemaphoreType.DMA, pltpu.SemaphoreType.DMA],
    )(gpack)


RS_ROWS = 248


def _rs_add_pair(own, got):
    def body(a_ref, b_ref, o_ref):
        o_ref[...] = a_ref[...] + b_ref[...]

    spec = pl.BlockSpec((1, RS_ROWS, PACK_COLS), lambda s, r: (s, r, 0))
    return pl.pallas_call(
        body, name="rs_add_pair", grid=(N_SHARD, PACK_HALF // RS_ROWS), out_shape=jax.ShapeDtypeStruct(own.shape, F32),
        in_specs=[spec, spec], out_specs=spec, compiler_params=_params(),
    )(own, got)


def _rs_chip_scatter(part):
    def body(p_ref, out_ref, send_sems, recv_sems, local_sem):
        x, y, c, chips = _place()
        me = 2 * x + y
        mine = pltpu.make_async_copy(p_ref.at[me], out_ref.at[me], local_sem)
        mine.start()
        sends = [pltpu.make_async_remote_copy(
            src_ref=p_ref.at[2 * px + py], dst_ref=out_ref.at[me], send_sem=send_sems.at[j], recv_sem=recv_sems.at[j],
            device_id=(px, py, c), device_id_type=MESH) for j, (px, py) in enumerate(chips)]
        for cp in sends:
            cp.start()
        for j, (px, py) in enumerate(chips):
            pltpu.make_async_remote_copy(
                src_ref=p_ref.at[me], dst_ref=out_ref.at[2 * px + py], send_sem=send_sems.at[j], recv_sem=recv_sems.at[j],
                device_id=(px, py, c), device_id_type=MESH).wait_recv()
        for cp in sends:
            cp.wait_send()
        mine.wait()

    return pl.pallas_call(
        body, name="rs_chip_scatter", out_shape=jax.ShapeDtypeStruct(part.shape, F32),
        in_specs=[_any_spec()], out_specs=_any_spec(),
        scratch_shapes=[pltpu.SemaphoreType.DMA((3,)), pltpu.SemaphoreType.DMA((3,)), pltpu.SemaphoreType.DMA],
    )(part)


def _rs_sum_chips(parts):
    def body(p_ref, o_ref):
        o_ref[...] = ((p_ref[0] + p_ref[1]) + p_ref[2]) + p_ref[3]

    return pl.pallas_call(
        body, name="rs_sum_chips", grid=(PACK_HALF // RS_ROWS,), out_shape=jax.ShapeDtypeStruct((PACK_HALF, PACK_COLS), F32),
        in_specs=[pl.BlockSpec((N_SHARD, RS_ROWS, PACK_COLS), lambda r: (0, r, 0))],
        out_specs=pl.BlockSpec((RS_ROWS, PACK_COLS), lambda r: (r, 0)), compiler_params=_params(),
    )(parts)


def _rs_pair_share(half):
    H = PACK_HALF

    def body(h_ref, out_ref, send_sem, recv_sem, local_sem):
        x, y, c, _ = _place()
        mine = pltpu.make_async_copy(h_ref, out_ref.at[pl.ds(c * H, H), :], local_sem)
        mine.start()
        share = pltpu.make_async_remote_copy(
            src_ref=h_ref, dst_ref=out_ref.at[pl.ds(c * H, H), :], send_sem=send_sem, recv_sem=recv_sem,
            device_id=(x, y, 1 - c), device_id_type=MESH)
        share.start()
        pltpu.make_async_remote_copy(
            src_ref=h_ref, dst_ref=out_ref.at[pl.ds((1 - c) * H, H), :], send_sem=send_sem, recv_sem=recv_sem,
            device_id=(x, y, 1 - c), device_id_type=MESH).wait_recv()
        share.wait_send()
        mine.wait()

    return pl.pallas_call(
        body, name="rs_pair_share", out_shape=jax.ShapeDtypeStruct((PACK_ROWS, PACK_COLS), F32),
        in_specs=[_any_spec()], out_specs=_any_spec(),
        scratch_shapes=[pltpu.SemaphoreType.DMA, pltpu.SemaphoreType.DMA, pltpu.SemaphoreType.DMA],
    )(half)


def _allreduce_small(vec):
    def body(v_ref, o_ref, all_ref, send_sems, recv_sems):
        x, y, c, _ = _place()
        me = 4 * x + 2 * y + c
        all_ref[me] = v_ref[...]
        sends = []
        for k in range(1, 8):
            kx, ky, kc = (k >> 2) & 1, (k >> 1) & 1, k & 1
            peer = (x ^ kx, y ^ ky, c ^ kc)
            cp = pltpu.make_async_remote_copy(
                src_ref=v_ref, dst_ref=all_ref.at[me], send_sem=send_sems.at[k - 1], recv_sem=recv_sems.at[k - 1],
                device_id=peer, device_id_type=MESH)
            cp.start()
            sends.append(cp)
        for k in range(1, 8):
            kx, ky, kc = (k >> 2) & 1, (k >> 1) & 1, k & 1
            src = 4 * (x ^ kx) + 2 * (y ^ ky) + (c ^ kc)
            pltpu.make_async_remote_copy(
                src_ref=v_ref, dst_ref=all_ref.at[src], send_sem=send_sems.at[k - 1], recv_sem=recv_sems.at[k - 1],
                device_id=(x, y, c), device_id_type=MESH).wait_recv()
        for cp in sends:
            cp.wait_send()
        total = all_ref[0]
        for d in range(1, 8):
            total = total + all_ref[d]
        o_ref[...] = total

    vm = pl.BlockSpec(memory_space=pltpu.VMEM)
    return pl.pallas_call(
        body, name="allreduce_small", out_shape=jax.ShapeDtypeStruct(vec.shape, F32), in_specs=[vm], out_specs=vm,
        scratch_shapes=[pltpu.VMEM((8,) + vec.shape, F32), pltpu.SemaphoreType.DMA((7,)), pltpu.SemaphoreType.DMA((7,))],
    )(vec)


SHARD_SHAPES = (("w_in", (D_MODEL, IN_WIDTH // N_SHARD)), ("w_out", (D_MODEL // N_SHARD, D_MODEL)),
                ("w_up", (D_MODEL, 2 * D_FF // N_SHARD)), ("w_down", (D_FF // N_SHARD, D_MODEL)),
                ("gla_gate_up", (16, GLA_QK // N_SHARD)), ("conv_w", (3, 2 * D_FF // N_SHARD)))
SHARD_AXIS = {"w_in": 1, "w_out": 0, "w_up": 1, "w_down": 0, "gla_gate_up": 1, "conv_w": 1}


def _pack_rows(flat):
    pad = PACK_ROWS * PACK_COLS - flat.shape[0]
    return jnp.pad(flat, (0, pad)).reshape(PACK_ROWS, PACK_COLS)


def _pack_weights(shards):
    parts = []
    for name, _ in SHARD_SHAPES:
        w = shards[name]
        if name == "conv_w":
            parts.append(lax.bitcast_convert_type(w, BF16).reshape(-1))
        else:
            parts.append(w.astype(BF16).reshape(-1))
    return _pack_rows(jnp.concatenate(parts))


def _unpack_weights(wfull):
    flat = wfull.reshape(N_SHARD, PACK_ROWS * PACK_COLS)
    out, off = {}, 0
    for name, shape in SHARD_SHAPES:
        n = shape[0] * shape[1] * (2 if name == "conv_w" else 1)
        seg = flat[:, off:off + n]
        off += n
        if name == "conv_w":
            pieces = [lax.bitcast_convert_type(seg[s].reshape(shape + (2,)), F32) for s in range(N_SHARD)]
        else:
            pieces = [seg[s].reshape(shape) for s in range(N_SHARD)]
        out[name] = jnp.concatenate(pieces, axis=SHARD_AXIS[name])
    return out


def _pack_grads(grads):
    rows = []
    for s in range(N_SHARD):
        parts = []
        for name, shape in SHARD_SHAPES:
            g = grads[name]
            ax = SHARD_AXIS[name]
            n = shape[ax]
            parts.append(lax.slice_in_dim(g, s * n, (s + 1) * n, axis=ax).reshape(-1))
        rows.append(_pack_rows(jnp.concatenate(parts)))
    return jnp.stack(rows)


def _unpack_grads(full):
    flat = full.reshape(-1)
    out, off = {}, 0
    for name, shape in SHARD_SHAPES:
        n = shape[0] * shape[1]
        out[name] = flat[off:off + n].reshape(shape)
        off += n
    return out


def _permute_w_in(w):
    pad = jnp.zeros((w.shape[0], IN_WIDTH_PAD - IN_WIDTH), w.dtype)
    return jnp.concatenate([w[:, 0:1024], w[:, 1040:2320], w[:, 1024:1040], pad], axis=1)


def _unpermute_w_in(wp):
    return jnp.concatenate([wp[:, 0:1024], wp[:, 2304:2320], wp[:, 1024:2304]], axis=1)


def _rope_tables(positions):
    half = ROPE_DIM // 2
    inv_freq = ROPE_THETA ** (-jnp.arange(half, dtype=F32) * (2.0 / ROPE_DIM))
    ang = positions.astype(F32)[:, None] * inv_freq
    cos, sin = jnp.cos(ang), jnp.sin(ang)
    T = positions.shape[0]
    z = lambda n: jnp.zeros((T, n), F32)
    rc = jnp.concatenate([cos, cos, jnp.ones((T, 48), F32)], axis=1)
    rsa = jnp.concatenate([-sin, z(56)], axis=1)
    rsb = jnp.concatenate([z(8), sin, z(48)], axis=1)
    return tuple(jnp.concatenate([t, t], axis=1) for t in (rc, rsa, rsb))


SMALL_NAMES = (("pre_mix_norm", 1024), ("gla_gate_bias", 256), ("gla_out_norm", 128), ("swa_sinks", 8),
               ("post_mix_norm", 1024), ("pre_ffn_norm", 1024), ("conv_b", 5632), ("post_ffn_norm", 1024))


def _pack_small(vals, loss):
    parts = [vals[n].reshape(-1) for n, _ in SMALL_NAMES] + [loss.reshape(1)]
    flat = jnp.concatenate(parts)
    return jnp.pad(flat, (0, SMALL_ROWS * LANES - flat.shape[0])).reshape(SMALL_ROWS, LANES)


def _unpack_small(packed):
    flat = packed.reshape(-1)
    out, off = {}, 0
    for n, size in SMALL_NAMES:
        out[n] = flat[off:off + size].reshape(1, size)
        off += size
    return out, flat[off]


def _local_step(x, positions, target, w, small):
    rc, rsa, rsb = _rope_tables(positions)
    wp = _permute_w_in(w["w_in"])
    gup = jnp.pad(w["gla_gate_up"], ((0, LANES - 16), (0, 0)))
    g1, g2, g3, g4 = (small[n] for n in ("pre_mix_norm", "post_mix_norm", "pre_ffn_norm", "post_ffn_norm"))
    gbias, gnorm, cb = small["gla_gate_bias"], small["gla_out_norm"], small["conv_b"]
    sinks = small["swa_sinks"].reshape(-1)
    cw = w["conv_w"]

    h1, q, k, v, la, gg, sq, kd, vd, glr = _proj_fwd(x, g1, wp, gup, gbias, rc, rsa, rsb)
    og, s_all = _gla_fwd(q, k, v, la)
    osw = _swa_fwd(sq, kd, vd, sinks)
    x1, cat, mix = _mix_out_fwd(x, og, gg, osw, gnorm, w["w_out"], g2)
    h2, up, act, y, dx2, loss, tails = _ffn_fwd(x1, g3, w["w_up"], cw, cb, w["w_down"], g4, target)

    dy, dup, dx1, dg4, dg3, dcb, dcw = _ffn_bwd(dx2, y, g4, up, tails, cw, cb, w["w_down"], w["w_up"], x1, g3)
    dmix, dog, dgg, dosw, dg2, dgn = _mix_out_bwd(dx1, mix, g2, w["w_out"], og, gg, gnorm)
    dsq, dkd, dvd, dsink = _swa_bwd(sq, kd, vd, sinks, dosw)
    dq, dk, dv, dz = _gla_bwd(q, k, v, la, s_all, dog)
    dx, dproj, dg1, dgup, dgb = _proj_bwd(x, g1, wp, gup, glr, dq, dk, dv, dgg, dsq, dkd, dvd, dz, rc, rsa, rsb, dx1)

    grads = {
        "w_in": _unpermute_w_in(_matmul_tn(h1, dproj, IN_WIDTH_PAD, "grad_w_in")),
        "w_out": _matmul_tn(cat, dmix, D_MODEL, "grad_w_out"),
        "w_up": _matmul_tn(h2, dup, 2 * D_FF // N_SHARD, "grad_w_up"),
        "w_down": _matmul_tn(act, dy, D_MODEL, "grad_w_down"),
        "gla_gate_up": dgup[0:16],
        "conv_w": dcw,
    }
    small_grads = {
        "pre_mix_norm": dg1, "gla_gate_bias": dgb, "gla_out_norm": dgn, "swa_sinks": dsink[:, 0].reshape(1, 8),
        "post_mix_norm": dg2, "pre_ffn_norm": dg3, "conv_b": dcb, "post_ffn_norm": dg4,
    }
    return loss[0, 0], dx, grads, small_grads


ADAM_ROWS = {"w_in": 256, "w_out": 256, "w_up": 256, "w_down": 176}
WEIGHT_ORDER = ("pre_mix_norm", "w_in", "gla_gate_up", "gla_gate_bias", "gla_out_norm", "swa_sinks", "w_out",
                "post_mix_norm", "pre_ffn_norm", "w_up", "conv_w", "conv_b", "w_down", "post_ffn_norm")
TINY_ROWS = 152


def _pack_tiny(vals):
    flat = jnp.concatenate([vals[n].reshape(-1) for n in TINY_NAMES])
    return jnp.pad(flat, (0, TINY_ROWS * LANES - flat.shape[0])).reshape(TINY_ROWS, LANES)


TINY_NAMES = tuple(n for n, _ in SMALL_NAMES) + ("gla_gate_up", "conv_w")
TINY_SHAPES = {**{n: (1, s) for n, s in SMALL_NAMES}, "gla_gate_up": (16, 64), "conv_w": (3, 1408)}


def _unpack_tiny(packed):
    flat = packed.reshape(-1)
    out, off = {}, 0
    for n in TINY_NAMES:
        shape = TINY_SHAPES[n]
        size = shape[0] * shape[1]
        out[n] = flat[off:off + size].reshape(shape)
        off += size
    return out


def kernel(x, positions, pre_mix_norm, w_in, gla_gate_up, gla_gate_bias, gla_out_norm, swa_sinks, w_out, post_mix_norm, pre_ffn_norm, w_up, conv_w, conv_b, w_down, post_ffn_norm, loss_target, m_pre_mix_norm, m_w_in, m_gla_gate_up, m_gla_gate_bias, m_gla_out_norm, m_swa_sinks, m_w_out, m_post_mix_norm, m_pre_ffn_norm, m_w_up, m_conv_w, m_conv_b, m_w_down, m_post_ffn_norm, v_pre_mix_norm, v_w_in, v_gla_gate_up, v_gla_gate_bias, v_gla_out_norm, v_swa_sinks, v_w_out, v_post_mix_norm, v_pre_ffn_norm, v_w_up, v_conv_w, v_conv_b, v_w_down, v_post_ffn_norm):
    weights = dict(pre_mix_norm=pre_mix_norm, w_in=w_in, gla_gate_up=gla_gate_up, gla_gate_bias=gla_gate_bias,
                   gla_out_norm=gla_out_norm, swa_sinks=swa_sinks, w_out=w_out, post_mix_norm=post_mix_norm,
                   pre_ffn_norm=pre_ffn_norm, w_up=w_up, conv_w=conv_w, conv_b=conv_b, w_down=w_down,
                   post_ffn_norm=post_ffn_norm)
    mom = dict(pre_mix_norm=m_pre_mix_norm, w_in=m_w_in, gla_gate_up=m_gla_gate_up, gla_gate_bias=m_gla_gate_bias,
               gla_out_norm=m_gla_out_norm, swa_sinks=m_swa_sinks, w_out=m_w_out, post_mix_norm=m_post_mix_norm,
               pre_ffn_norm=m_pre_ffn_norm, w_up=m_w_up, conv_w=m_conv_w, conv_b=m_conv_b, w_down=m_w_down,
               post_ffn_norm=m_post_ffn_norm)
    var = dict(pre_mix_norm=v_pre_mix_norm, w_in=v_w_in, gla_gate_up=v_gla_gate_up, gla_gate_bias=v_gla_gate_bias,
               gla_out_norm=v_gla_out_norm, swa_sinks=v_swa_sinks, w_out=v_w_out, post_mix_norm=v_post_mix_norm,
               pre_ffn_norm=v_pre_ffn_norm, w_up=v_w_up, conv_w=v_conv_w, conv_b=v_conv_b, w_down=v_w_down,
               post_ffn_norm=v_post_ffn_norm)
    weights, mom, var = ({n: a[0] if a.ndim == 3 else a for n, a in d.items()} for d in (weights, mom, var))

    full = _unpack_weights(_allgather_weights(_pack_weights(weights)))
    small = {n: weights[n] for n, _ in SMALL_NAMES}
    loss, dx, grads, small_grads = _local_step(x[0], positions[0], loss_target[0], full, small)

    own, got = _rs_pair_swap(_pack_grads(grads))
    reduced = _rs_pair_share(_rs_sum_chips(_rs_chip_scatter(_rs_add_pair(own, got))))
    g_shard = _unpack_grads(reduced)
    g_small, loss_sum = _unpack_small(_allreduce_small(_pack_small(small_grads, loss)))
    g_all = {**g_small, **g_shard}

    delta, new_m, new_v = {}, {}, {}
    for n in ("w_in", "w_out", "w_up", "w_down"):
        delta[n], new_m[n], new_v[n] = _adamw(weights[n], g_all[n], mom[n], var[n], ADAM_ROWS[n], "adamw_" + n)
    tiny = _adamw(*(_pack_tiny({n: d[n] for n in TINY_NAMES}) for d in (weights, g_all, mom, var)), TINY_ROWS, "adamw_small")
    for res, packed in zip((delta, new_m, new_v), tiny):
        res.update(_unpack_tiny(packed))

    def lead(n, a):
        return a[None] if n in SHARD_AXIS else a

    outs = [loss_sum, dx[None]]
    for d in (g_all, delta, new_m, new_v):
        outs.extend(lead(n, d[n]) for n in WEIGHT_ORDER)
    return tuple(outs)
```

```python
import functools

import jax
import jax.numpy as jnp
from jax import lax
from jax.experimental import pallas as pl
from jax.experimental.pallas import tpu as pltpu

F32 = jnp.float32
BF16 = jnp.bfloat16
MESH = pl.DeviceIdType.MESH

D_MODEL = 1024
GLA_HEADS = 4
GLA_DK = 64
GLA_DV = 128
GLA_TAU = 16.0
GLA_CHUNK = 64
SWA_HEADS = 8
SWA_HD = 64
SWA_BLOCK = 128
ROPE_THETA = 500000.0
ROPE_DIM = 16
D_FF = 2816
EPS = 1e-6
GLA_QK = 256
GLA_V = 512
SWA_Q = 512
SWA_KV = 128
IN_WIDTH = 2320
IN_WIDTH_PAD = 2432
N_SHARD = 4

ADAM_LR = 0.001
ADAM_B1 = 0.9
ADAM_B2 = 0.999
ADAM_EPS = 1e-08
ADAM_WD = 0.01
ADAM_STEP = 10

LANES = 128
VMEM_LIMIT = 56 * 1024 * 1024
TM = 256
SHARD_FF = 2 * D_FF // N_SHARD
FF_PIECES = ((0, 512), (512, 512), (1024, 384))
GLA_BLOCK = 256

SMALL_ROWS = 80


def _params(**kw):
    return pltpu.CompilerParams(vmem_limit_bytes=VMEM_LIMIT, **kw)


def _mm(a, b):
    return lax.dot_general(a.astype(BF16), b.astype(BF16), (((1,), (0,)), ((), ())), preferred_element_type=F32)


def _mm_nt(a, b):
    return lax.dot_general(a.astype(BF16), b.astype(BF16), (((1,), (1,)), ((), ())), preferred_element_type=F32)


def _mm_tn(a, b):
    return lax.dot_general(a.astype(BF16), b.astype(BF16), (((0,), (0,)), ((), ())), preferred_element_type=F32)


def _mm_f32(a, b):
    return lax.dot_general(a, b, (((1,), (0,)), ((), ())), preferred_element_type=F32, precision=lax.Precision.HIGHEST)


def _iota(shape, dim):
    return lax.broadcasted_iota(jnp.int32, shape, dim)


def _sigmoid(x):
    return 1.0 / (1.0 + jnp.exp(-x))


def _gelu_parts(x):
    c = 0.7978845608028654
    x2 = x * x
    t = jnp.tanh(c * (x + 0.044715 * (x2 * x)))
    cdf = 0.5 * (1.0 + t)
    dcdf = 0.5 * (1.0 - t * t) * c * (1.0 + 3.0 * 0.044715 * x2)
    return x * cdf, cdf + x * dcdf


def _rms_bwd(v, r, g, dout):
    gd = g * dout
    return r * gd - v * (r * r * r) * jnp.mean(v * gd, axis=-1, keepdims=True)


def _row_spec(tm, cols):
    return pl.BlockSpec((tm, cols), lambda i: (i, 0))


def _const_spec(shape):
    return pl.BlockSpec(shape, lambda i: (0,) * len(shape))


def _any_spec():
    return pl.BlockSpec(memory_space=pl.ANY)


def _load_once(src_hbm, dst_vmem, sem):
    @pl.when(pl.program_id(0) == 0)
    def _():
        cp = pltpu.make_async_copy(src_hbm, dst_vmem, sem)
        cp.start()
        cp.wait()


def _rotate(v, rc, rsa, rsb):
    return v * rc + pltpu.roll(v, 120, 1) * rsa + pltpu.roll(v, 8, 1) * rsb


def _rotate_bwd(dv, rc, rsa, rsb):
    return dv * rc + pltpu.roll(dv * rsa, 8, 1) + pltpu.roll(dv * rsb, 120, 1)


def _proj_fwd(x, g1, wp, gup, gbias, rc, rsa, rsb):
    T = x.shape[0]

    def body(x_ref, g1_ref, wp_hbm, gup_ref, gb_ref, rc_ref, rsa_ref, rsb_ref,
             h1_ref, q_ref, k_ref, v_ref, la_ref, gg_ref, sq_ref, kd_ref, vd_ref, glr_ref, wp_v, sem):
        _load_once(wp_hbm, wp_v, sem)
        xt = x_ref[...]
        r = lax.rsqrt(jnp.mean(xt * xt, axis=-1, keepdims=True) + EPS)
        h = (xt * r * g1_ref[...]).astype(BF16)
        h1_ref[...] = h
        q_ref[...] = _mm(h, wp_v[:, 0:256])
        k_ref[...] = _mm(h, wp_v[:, 256:512])
        v_ref[...] = _mm(h, wp_v[:, 512:1024])
        gg_ref[...] = _mm(h, wp_v[:, 1024:1536])
        glr = _mm(h, wp_v[:, 2304:2432]).astype(BF16)
        glr_ref[...] = glr
        z = _mm(glr, gup_ref[...]) + gb_ref[...]
        la_ref[...] = (jnp.minimum(z, 0.0) - jnp.log1p(jnp.exp(-jnp.abs(z)))) * (1.0 / GLA_TAU)
        rc_, rsa_, rsb_ = rc_ref[...], rsa_ref[...], rsb_ref[...]
        for s in range(4):
            qs = _mm(h, wp_v[:, 1536 + 128 * s:1664 + 128 * s])
            sq_ref[:, 128 * s:128 * s + 128] = (_rotate(qs, rc_, rsa_, rsb_) * 0.125).astype(BF16)
        lane = _iota((TM, LANES), 1)
        first = lane < 64
        kr = _rotate(_mm(h, wp_v[:, 2048:2176]), rc_, rsa_, rsb_)
        krr = pltpu.roll(kr, 64, 1)
        kd_ref[:, 0:128] = jnp.where(first, kr, krr).astype(BF16)
        kd_ref[:, 128:256] = jnp.where(first, krr, kr).astype(BF16)
        vr = _mm(h, wp_v[:, 2176:2304])
        vrr = pltpu.roll(vr, 64, 1)
        vd_ref[:, 0:128] = jnp.where(first, vr, vrr).astype(BF16)
        vd_ref[:, 128:256] = jnp.where(first, vrr, vr).astype(BF16)

    outs = [
        jax.ShapeDtypeStruct((T, D_MODEL), BF16),
        jax.ShapeDtypeStruct((T, GLA_QK), F32),
        jax.ShapeDtypeStruct((T, GLA_QK), F32),
        jax.ShapeDtypeStruct((T, GLA_V), F32),
        jax.ShapeDtypeStruct((T, GLA_QK), F32),
        jax.ShapeDtypeStruct((T, GLA_V), F32),
        jax.ShapeDtypeStruct((T, SWA_Q), BF16),
        jax.ShapeDtypeStruct((T, 256), BF16),
        jax.ShapeDtypeStruct((T, 256), BF16),
        jax.ShapeDtypeStruct((T, LANES), BF16),
    ]
    return pl.pallas_call(
        body, name="proj_fwd", grid=(T // TM,), out_shape=outs,
        in_specs=[_row_spec(TM, D_MODEL), _const_spec((1, D_MODEL)), _any_spec(), _const_spec((LANES, GLA_QK)),
                  _const_spec((1, GLA_QK)), _row_spec(TM, LANES), _row_spec(TM, LANES), _row_spec(TM, LANES)],
        out_specs=[_row_spec(TM, o.shape[1]) for o in outs],
        scratch_shapes=[pltpu.VMEM((D_MODEL, IN_WIDTH_PAD), BF16), pltpu.SemaphoreType.DMA],
        compiler_params=_params(),
    )(x, g1, wp, gup, gbias, rc, rsa, rsb)


def _gla_masks():
    C = GLA_CHUNK
    lane = _iota((C, LANES), 1)
    lane_masks = [(lane < 64).astype(F32), (lane >= 64).astype(F32)]
    causal = _iota((C, C), 0) >= _iota((C, C), 1)
    blk = ((_iota((256, LANES), 0) >> 7) == (_iota((256, LANES), 1) >> 6)).astype(F32)
    return lane_masks, causal, blk


def _gla_chunk_terms(q_ref, k_ref, b_ref, c, p):
    C = GLA_CHUNK
    rows = slice(C * c, C * c + C)
    cols = slice(LANES * p, LANES * p + LANES)
    bc = b_ref[rows, cols]
    bl = b_ref[C * c + C - 1:C * c + C, cols]
    bm = b_ref[C * c + C // 2 - 1:C * c + C // 2, cols]
    qs = q_ref[rows, cols] * 0.125
    kk = k_ref[rows, cols]
    eb = jnp.exp(bc)
    ekl = jnp.exp(bl - bc)
    eqm = jnp.exp(bc - bm)
    ekm = jnp.exp(bm - bc)
    return qs, kk, eb, ekl, eqm, ekm, jnp.exp(bl)


def _block_cumsum(la):
    n = la.shape[0]
    row, col = _iota((n, n), 0), _iota((n, n), 1)
    tril = (((row >> 6) == (col >> 6)) & (col <= row)).astype(F32)
    return _mm_f32(tril, la)


def _gla_fwd(q, k, v, la):
    T = q.shape[0]
    NB = GLA_BLOCK // GLA_CHUNK
    C = GLA_CHUNK

    def body(q_ref, k_ref, v_ref, la_ref, o_ref, s_ref, st_ref, b_ref):
        @pl.when(pl.program_id(0) == 0)
        def _():
            st_ref[...] = jnp.zeros_like(st_ref)

        lane_masks, causal, blk = _gla_masks()
        b_ref[...] = _block_cumsum(la_ref[...])
        for c in range(NB):
            rows = slice(C * c, C * c + C)
            for p in range(2):
                qs, kk, eb, ekl, eqm, ekm, gam = _gla_chunk_terms(q_ref, k_ref, b_ref, c, p)
                qh, kh, qm, km = qs * eb, kk * ekl, qs * eqm, kk * ekm
                vp = v_ref[rows, 256 * p:256 * p + 256]
                st = st_ref[p]
                s_ref[c, p] = st
                o_inter = _mm_nt(qh, st)
                for j in range(2):
                    a = jnp.where(causal, _mm_nt(qm * lane_masks[j], km), 0.0)
                    o_ref[rows, 256 * p + 128 * j:256 * p + 128 * j + 128] = (
                        o_inter[:, 128 * j:128 * j + 128] + _mm(a, vp[:, 128 * j:128 * j + 128]))
                st_ref[p] = st * gam + blk * _mm_tn(vp, kh)

    return pl.pallas_call(
        body, name="gla_fwd", grid=(T // GLA_BLOCK,),
        out_shape=[jax.ShapeDtypeStruct((T, GLA_V), F32), jax.ShapeDtypeStruct((T // C, 2, 256, LANES), F32)],
        in_specs=[_row_spec(GLA_BLOCK, GLA_QK), _row_spec(GLA_BLOCK, GLA_QK), _row_spec(GLA_BLOCK, GLA_V),
                  _row_spec(GLA_BLOCK, GLA_QK)],
        out_specs=[_row_spec(GLA_BLOCK, GLA_V), pl.BlockSpec((NB, 2, 256, LANES), lambda i: (i, 0, 0, 0))],
        scratch_shapes=[pltpu.VMEM((2, 256, LANES), F32), pltpu.VMEM((GLA_BLOCK, GLA_QK), F32)],
        compiler_params=_params(),
    )(q, k, v, la)


def _gla_bwd(q, k, v, la, s_all, do):
    T = q.shape[0]
    NB = GLA_BLOCK // GLA_CHUNK
    C = GLA_CHUNK
    nblk = T // GLA_BLOCK

    def body(q_ref, k_ref, v_ref, la_ref, s_ref, do_ref, dq_ref, dk_ref, dv_ref, dz_ref, dst_ref, b_ref):
        @pl.when(pl.program_id(0) == 0)
        def _():
            dst_ref[...] = jnp.zeros_like(dst_ref)

        lane_masks, causal, blk = _gla_masks()
        triu = (_iota((C, C), 1) >= _iota((C, C), 0)).astype(F32)
        b_ref[...] = _block_cumsum(la_ref[...])
        for c in reversed(range(NB)):
            rows = slice(C * c, C * c + C)
            for p in range(2):
                cols = slice(LANES * p, LANES * p + LANES)
                qs, kk, eb, ekl, eqm, ekm, gam = _gla_chunk_terms(q_ref, k_ref, b_ref, c, p)
                qh, kh, qm, km = qs * eb, kk * ekl, qs * eqm, kk * ekm
                vp = v_ref[rows, 256 * p:256 * p + 256]
                dop = do_ref[rows, 256 * p:256 * p + 256]
                st = s_ref[c, p]
                dst = dst_ref[p]
                dqh = _mm(dop, st)
                dkh = _mm(vp, dst)
                dvp = _mm_nt(kh, dst)
                dgam = jnp.sum(st * dst, axis=0, keepdims=True)
                dqm = jnp.zeros((C, LANES), F32)
                dkm = jnp.zeros((C, LANES), F32)
                for j in range(2):
                    hs = slice(128 * j, 128 * j + 128)
                    a = jnp.where(causal, _mm_nt(qm * lane_masks[j], km), 0.0)
                    da = jnp.where(causal, _mm_nt(dop[:, hs], vp[:, hs]), 0.0)
                    dv_ref[rows, 256 * p + 128 * j:256 * p + 128 * j + 128] = dvp[:, hs] + _mm_tn(a, dop[:, hs])
                    dqm = dqm + lane_masks[j] * _mm(da, km)
                    dkm = dkm + lane_masks[j] * _mm_tn(da, qm)
                dqs = dqh * eb + dqm * eqm
                dk = dkh * ekl + dkm * ekm
                db = dqs * qs - dk * kk
                extra = jnp.sum(dkh * kh, axis=0, keepdims=True) + dgam * gam
                dg = _mm_f32(triu, db) + extra
                dq_ref[rows, cols] = dqs * 0.125
                dk_ref[rows, cols] = dk
                dz_ref[rows, cols] = dg * (1.0 - jnp.exp(GLA_TAU * la_ref[rows, cols])) * (1.0 / GLA_TAU)
                dst_ref[p] = dst * gam + blk * _mm_tn(dop, qh)

    rev = lambda i: (nblk - 1 - i, 0)
    rspec = lambda cols: pl.BlockSpec((GLA_BLOCK, cols), rev)
    return pl.pallas_call(
        body, name="gla_bwd", grid=(nblk,),
        out_shape=[jax.ShapeDtypeStruct((T, GLA_QK), F32), jax.ShapeDtypeStruct((T, GLA_QK), F32),
                   jax.ShapeDtypeStruct((T, GLA_V), F32), jax.ShapeDtypeStruct((T, GLA_QK), F32)],
        in_specs=[rspec(GLA_QK), rspec(GLA_QK), rspec(GLA_V), rspec(GLA_QK),
                  pl.BlockSpec((NB, 2, 256, LANES), lambda i: (nblk - 1 - i, 0, 0, 0)), rspec(GLA_V)],
        out_specs=[rspec(GLA_QK), rspec(GLA_QK), rspec(GLA_V), rspec(GLA_QK)],
        scratch_shapes=[pltpu.VMEM((2, 256, LANES), F32), pltpu.VMEM((GLA_BLOCK, GLA_QK), F32)],
        compiler_params=_params(),
    )(q, k, v, la, s_all, do)


def _swa_masks(i):
    W = SWA_BLOCK
    r, c = _iota((W, W), 0), _iota((W, W), 1)
    return (c > r) & (i > 0), c <= r


def _swa_probs(qh, kp, kc, mask_p, mask_c, sink):
    neg = -1e30
    s_p = jnp.where(mask_p, _mm_nt(qh, kp), neg)
    s_c = jnp.where(mask_c, _mm_nt(qh, kc), neg)
    m = jnp.maximum(jnp.maximum(jnp.max(s_p, axis=-1, keepdims=True), jnp.max(s_c, axis=-1, keepdims=True)), sink)
    p_p = jnp.where(mask_p, jnp.exp(s_p - m), 0.0)
    p_c = jnp.where(mask_c, jnp.exp(s_c - m), 0.0)
    p_s = jnp.exp(sink - m)
    denom = jnp.sum(p_p, axis=-1, keepdims=True) + jnp.sum(p_c, axis=-1, keepdims=True) + p_s
    return p_p, p_c, p_s, denom


def _swa_fwd(sq, kd, vd, sinks):
    T = sq.shape[0]
    W = SWA_BLOCK
    prev = lambda i: (jnp.maximum(i - 1, 0), 0)

    def body(sink_ref, q_ref, kp_ref, kc_ref, vp_ref, vc_ref, o_ref):
        i = pl.program_id(0)
        mask_p, mask_c = _swa_masks(i)
        first = _iota((W, LANES), 1) < 64
        zero = jnp.zeros((W, LANES), BF16)
        for m in range(4):
            g = m // 2
            gs = slice(128 * g, 128 * g + 128)
            qpair = q_ref[:, 128 * m:128 * m + 128]
            res = []
            for jj in range(2):
                qh = jnp.where(first, qpair, zero) if jj == 0 else jnp.where(first, zero, qpair)
                p_p, p_c, _, denom = _swa_probs(qh, kp_ref[:, gs], kc_ref[:, gs], mask_p, mask_c, sink_ref[2 * m + jj])
                res.append((_mm(p_p, vp_ref[:, gs]) + _mm(p_c, vc_ref[:, gs])) / denom)
            o_ref[:, 128 * m:128 * m + 128] = jnp.where(first, res[0], res[1]).astype(BF16)

    return pl.pallas_call(
        body, name="swa_fwd", grid=(T // W,), out_shape=jax.ShapeDtypeStruct((T, SWA_Q), BF16),
        in_specs=[pl.BlockSpec(memory_space=pltpu.SMEM), _row_spec(W, SWA_Q), pl.BlockSpec((W, 256), prev),
                  _row_spec(W, 256), pl.BlockSpec((W, 256), prev), _row_spec(W, 256)],
        out_specs=_row_spec(W, SWA_Q),
        compiler_params=_params(),
    )(sinks, sq, kd, kd, vd, vd)


def _swa_bwd(sq, kd, vd, sinks, do):
    T = sq.shape[0]
    W = SWA_BLOCK
    n = T // W
    cur = lambda i: (jnp.minimum(i, n - 1), 0)
    prev = lambda i: (jnp.clip(i - 1, 0, n - 1), 0)

    def body(sink_ref, q_ref, kp_ref, kc_ref, vp_ref, vc_ref, do_ref, dq_ref, dk_ref, dv_ref, ds_ref, ck_ref, cv_ref):
        i = pl.program_id(0)

        @pl.when(i == 0)
        def _():
            ds_ref[...] = jnp.zeros_like(ds_ref)
            ck_ref[...] = jnp.zeros_like(ck_ref)
            cv_ref[...] = jnp.zeros_like(cv_ref)

        @pl.when(i < n)
        def _():
            mask_p, mask_c = _swa_masks(i)
            first = _iota((W, LANES), 1) < 64
            zero = jnp.zeros((W, LANES), BF16)
            for g in range(2):
                gs = slice(128 * g, 128 * g + 128)
                kp, kc, vp, vc = kp_ref[:, gs], kc_ref[:, gs], vp_ref[:, gs], vc_ref[:, gs]
                dk_p = jnp.zeros((W, LANES), F32)
                dk_c = jnp.zeros((W, LANES), F32)
                dv_p = jnp.zeros((W, LANES), F32)
                dv_c = jnp.zeros((W, LANES), F32)
                for mm in range(2):
                    m = 2 * g + mm
                    qpair = q_ref[:, 128 * m:128 * m + 128]
                    dopair = do_ref[:, 128 * m:128 * m + 128]
                    dq_pair = []
                    for jj in range(2):
                        sel = first if jj == 0 else jnp.logical_not(first)
                        qh = jnp.where(sel, qpair, zero)
                        doh = jnp.where(sel, dopair, zero)
                        p_p, p_c, p_s, denom = _swa_probs(qh, kp, kc, mask_p, mask_c, sink_ref[2 * m + jj])
                        inv = 1.0 / denom
                        p_p, p_c, p_s = p_p * inv, p_c * inv, p_s * inv
                        dp_p = _mm_nt(doh, vp)
                        dp_c = _mm_nt(doh, vc)
                        delta = jnp.sum(p_p * dp_p, axis=-1, keepdims=True) + jnp.sum(p_c * dp_c, axis=-1, keepdims=True)
                        ds_p = p_p * (dp_p - delta)
                        ds_c = p_c * (dp_c - delta)
                        h = 2 * m + jj
                        ds_ref[h:h + 1, :] = ds_ref[h:h + 1, :] - jnp.sum(p_s * delta)
                        dq_pair.append((_mm(ds_p, kp) + _mm(ds_c, kc)) * 0.125)
                        dk_p = dk_p + _mm_tn(ds_p, qh)
                        dk_c = dk_c + _mm_tn(ds_c, qh)
                        dv_p = dv_p + _mm_tn(p_p, doh)
                        dv_c = dv_c + _mm_tn(p_c, doh)
                    dq_ref[:, 128 * m:128 * m + 128] = jnp.where(first, dq_pair[0], dq_pair[1])
                dk_ref[:, gs] = ck_ref[:, gs] + dk_p
                dv_ref[:, gs] = cv_ref[:, gs] + dv_p
                ck_ref[:, gs] = dk_c
                cv_ref[:, gs] = dv_c

        @pl.when(i == n)
        def _():
            dk_ref[...] = ck_ref[...]
            dv_ref[...] = cv_ref[...]

    return pl.pallas_call(
        body, name="swa_bwd", grid=(n + 1,),
        out_shape=[jax.ShapeDtypeStruct((T, SWA_Q), F32), jax.ShapeDtypeStruct((T, 256), F32),
                   jax.ShapeDtypeStruct((T, 256), F32), jax.ShapeDtypeStruct((8, LANES), F32)],
        in_specs=[pl.BlockSpec(memory_space=pltpu.SMEM), pl.BlockSpec((W, SWA_Q), cur), pl.BlockSpec((W, 256), prev),
                  pl.BlockSpec((W, 256), cur), pl.BlockSpec((W, 256), prev), pl.BlockSpec((W, 256), cur),
                  pl.BlockSpec((W, SWA_Q), cur)],
        out_specs=[pl.BlockSpec((W, SWA_Q), cur), pl.BlockSpec((W, 256), prev), pl.BlockSpec((W, 256), prev),
                   _const_spec((8, LANES))],
        scratch_shapes=[pltpu.VMEM((W, 256), F32), pltpu.VMEM((W, 256), F32)],
        compiler_params=_params(),
    )(sinks, sq, kd, kd, vd, vd, do)


def _mix_out_fwd(x, og, gg, osw, gnorm, wout, g2):
    T = x.shape[0]

    def body(x_ref, og_ref, gg_ref, osw_ref, gn_ref, wout_ref, g2_ref, x1_ref, cat_ref, mix_ref):
        gn = gn_ref[...]
        for j in range(GLA_HEADS):
            hs = slice(128 * j, 128 * j + 128)
            o = og_ref[:, hs]
            r = lax.rsqrt(jnp.mean(o * o, axis=-1, keepdims=True) + EPS)
            gate = gg_ref[:, hs]
            cat_ref[:, hs] = (o * r * gn * (gate * _sigmoid(gate))).astype(BF16)
        cat_ref[:, GLA_V:] = osw_ref[...]
        mix = _mm(cat_ref[...], wout_ref[...])
        mix_ref[...] = mix
        r2 = lax.rsqrt(jnp.mean(mix * mix, axis=-1, keepdims=True) + EPS)
        x1_ref[...] = x_ref[...] + mix * r2 * g2_ref[...]

    return pl.pallas_call(
        body, name="mix_out_fwd", grid=(T // TM,),
        out_shape=[jax.ShapeDtypeStruct((T, D_MODEL), F32), jax.ShapeDtypeStruct((T, D_MODEL), BF16),
                   jax.ShapeDtypeStruct((T, D_MODEL), F32)],
        in_specs=[_row_spec(TM, D_MODEL), _row_spec(TM, GLA_V), _row_spec(TM, GLA_V), _row_spec(TM, SWA_Q),
                  _const_spec((1, LANES)), _const_spec((D_MODEL, D_MODEL)), _const_spec((1, D_MODEL))],
        out_specs=[_row_spec(TM, D_MODEL), _row_spec(TM, D_MODEL), _row_spec(TM, D_MODEL)],
        compiler_params=_params(),
    )(x, og, gg, osw, gnorm, wout, g2)


def _conv_fwd(up, prev1, prev2, w0, w1, w2, bias):
    row = _iota(up.shape, 0)
    m1 = jnp.where(row == 0, prev1, pltpu.roll(up, 1, 0))
    m2 = jnp.where(row == 0, prev2, jnp.where(row == 1, prev1, pltpu.roll(up, 2, 0)))
    return bias + w0 * m2 + w1 * m1 + w2 * up, m1, m2


def _ff_pieces():
    return [(j, off, wd) for j in range(2) for off, wd in FF_PIECES]


def _ffn_fwd(x1, g3, wup, cw, cb, wdown, g4, target):
    T = x1.shape[0]

    def body(x1_ref, g3_ref, wup_hbm, cw_ref, cb_ref, wdn_hbm, g4_ref, tg_ref,
             h2_ref, up_ref, a_ref, y_ref, dx2_ref, loss_ref, tail_ref, wup_v, wdn_v, carry_ref, sems):
        _load_once(wup_hbm, wup_v, sems.at[0])
        _load_once(wdn_hbm, wdn_v, sems.at[1])

        @pl.when(pl.program_id(0) == 0)
        def _():
            carry_ref[...] = jnp.zeros_like(carry_ref)
            loss_ref[...] = jnp.zeros_like(loss_ref)

        x1 = x1_ref[...]
        r3 = lax.rsqrt(jnp.mean(x1 * x1, axis=-1, keepdims=True) + EPS)
        h2 = (x1 * r3 * g3_ref[...]).astype(BF16)
        h2_ref[...] = h2
        y = jnp.zeros((TM, D_MODEL), F32)
        for j, off, wd in _ff_pieces():
            base = SHARD_FF * j + off
            u = []
            for half in range(2):
                cs = slice(D_FF * half + base, D_FF * half + base + wd)
                upb = _mm(h2, wup_v[2 * half + j, :, off:off + wd]).astype(BF16)
                up_ref[:, cs] = upb
                upf = upb.astype(F32)
                conv, _, _ = _conv_fwd(upf, carry_ref[7:8, cs], carry_ref[6:7, cs],
                                       cw_ref[0:1, cs], cw_ref[1:2, cs], cw_ref[2:3, cs], cb_ref[:, cs])
                carry_ref[:, cs] = upf[TM - 8:TM, :]
                tail_ref[0, :, cs] = upf[TM - 8:TM, :]
                u.append(conv)
            act, _ = _gelu_parts(u[1])
            a = (act * u[0]).astype(BF16)
            a_ref[:, base:base + wd] = a
            y = y + _mm(a, wdn_v[base:base + wd, :])
        y_ref[...] = y
        r4 = lax.rsqrt(jnp.mean(y * y, axis=-1, keepdims=True) + EPS)
        err = x1 + y * r4 * g4_ref[...] - tg_ref[...]
        dx2_ref[...] = err * (1.0 / D_MODEL)
        loss_ref[...] = loss_ref[...] + jnp.sum(err * err) * (0.5 / D_MODEL)

    outs = [
        jax.ShapeDtypeStruct((T, D_MODEL), BF16),
        jax.ShapeDtypeStruct((T, 2 * D_FF), BF16),
        jax.ShapeDtypeStruct((T, D_FF), BF16),
        jax.ShapeDtypeStruct((T, D_MODEL), F32),
        jax.ShapeDtypeStruct((T, D_MODEL), F32),
        jax.ShapeDtypeStruct((8, LANES), F32),
        jax.ShapeDtypeStruct((T // TM, 8, 2 * D_FF), F32),
    ]
    return pl.pallas_call(
        body, name="ffn_fwd", grid=(T // TM,), out_shape=outs,
        in_specs=[_row_spec(TM, D_MODEL), _const_spec((1, D_MODEL)), _any_spec(), _const_spec((3, 2 * D_FF)),
                  _const_spec((1, 2 * D_FF)), _any_spec(), _const_spec((1, D_MODEL)), _row_spec(TM, D_MODEL)],
        out_specs=[_row_spec(TM, D_MODEL), _row_spec(TM, 2 * D_FF), _row_spec(TM, D_FF), _row_spec(TM, D_MODEL),
                   _row_spec(TM, D_MODEL), _const_spec((8, LANES)),
                   pl.BlockSpec((1, 8, 2 * D_FF), lambda i: (i, 0, 0))],
        scratch_shapes=[pltpu.VMEM((N_SHARD, D_MODEL, SHARD_FF), BF16), pltpu.VMEM((D_FF, D_MODEL), BF16),
                        pltpu.VMEM((8, 2 * D_FF), F32), pltpu.SemaphoreType.DMA((2,))],
        compiler_params=_params(),
    )(x1, g3, wup, cw, cb, wdown, g4, target)


def _ffn_bwd(dx2, y, g4, up, tails, cw, cb, wdown, wup, x1, g3):
    T = dx2.shape[0]
    nt = T // TM
    rev = lambda i: (nt - 1 - i, 0)
    halo = lambda i: (jnp.maximum(nt - 2 - i, 0), 0, 0)

    def body(dn_ref, y_ref, g4_ref, up_ref, halo_ref, cw_ref, cb_ref, wdn_hbm, wup_hbm, x1_ref, g3_ref,
             dy_ref, dup_ref, dx1_ref, dg4_ref, dg3_ref, dcb_ref, dcw_ref, wup_v, wdn_v, carry_ref, sems):
        _load_once(wup_hbm, wup_v, sems.at[0])
        _load_once(wdn_hbm, wdn_v, sems.at[1])
        i = pl.program_id(0)

        @pl.when(i == 0)
        def _():
            carry_ref[...] = jnp.zeros_like(carry_ref)
            dg4_ref[...] = jnp.zeros_like(dg4_ref)
            dg3_ref[...] = jnp.zeros_like(dg3_ref)
            dcb_ref[...] = jnp.zeros_like(dcb_ref)
            dcw_ref[...] = jnp.zeros_like(dcw_ref)

        has_prev = (i < nt - 1).astype(F32)
        dn = dn_ref[...]
        y = y_ref[...]
        g4v = g4_ref[...]
        r4 = lax.rsqrt(jnp.mean(y * y, axis=-1, keepdims=True) + EPS)
        dg4_ref[...] = dg4_ref[...] + jnp.sum(dn * y * r4, axis=0, keepdims=True)
        dy = _rms_bwd(y, r4, g4v, dn).astype(BF16)
        dy_ref[...] = dy
        dh2 = jnp.zeros((TM, D_MODEL), F32)
        for j, off, wd in _ff_pieces():
            base = SHARD_FF * j + off
            row = _iota((TM, wd), 0)
            da = _mm_nt(dy, wdn_v[base:base + wd, :])
            u, m1s, m2s, ups, css = [], [], [], [], []
            for half in range(2):
                cs = slice(D_FF * half + base, D_FF * half + base + wd)
                upf = up_ref[:, cs].astype(F32)
                p1 = halo_ref[0, 7:8, cs] * has_prev
                p2 = halo_ref[0, 6:7, cs] * has_prev
                conv, m1, m2 = _conv_fwd(upf, p1, p2, cw_ref[0:1, cs], cw_ref[1:2, cs], cw_ref[2:3, cs], cb_ref[:, cs])
                u.append(conv)
                m1s.append(m1)
                m2s.append(m2)
                ups.append(upf)
                css.append(cs)
            act, dact = _gelu_parts(u[1])
            dus = [da * act, da * u[0] * dact]
            for half in range(2):
                cs, du = css[half], dus[half]
                dcb_ref[:, cs] = dcb_ref[:, cs] + jnp.sum(du, axis=0, keepdims=True)
                dcw_ref[0:1, cs] = dcw_ref[0:1, cs] + jnp.sum(du * m2s[half], axis=0, keepdims=True)
                dcw_ref[1:2, cs] = dcw_ref[1:2, cs] + jnp.sum(du * m1s[half], axis=0, keepdims=True)
                dcw_ref[2:3, cs] = dcw_ref[2:3, cs] + jnp.sum(du * ups[half], axis=0, keepdims=True)
                n1 = carry_ref[0:1, cs]
                n2 = carry_ref[1:2, cs]
                p1 = jnp.where(row == TM - 1, n1, pltpu.roll(du, TM - 1, 0))
                p2 = jnp.where(row == TM - 1, n2, jnp.where(row == TM - 2, n1, pltpu.roll(du, TM - 2, 0)))
                carry_ref[:, cs] = du[0:8, :]
                dup = (cw_ref[2:3, cs] * du + cw_ref[1:2, cs] * p1 + cw_ref[0:1, cs] * p2).astype(BF16)
                dup_ref[:, cs] = dup
                dh2 = dh2 + _mm_nt(dup, wup_v[2 * half + j, :, off:off + wd])
        x1 = x1_ref[...]
        r3 = lax.rsqrt(jnp.mean(x1 * x1, axis=-1, keepdims=True) + EPS)
        dg3_ref[...] = dg3_ref[...] + jnp.sum(dh2 * x1 * r3, axis=0, keepdims=True)
        dx1_ref[...] = dn + _rms_bwd(x1, r3, g3_ref[...], dh2)

    outs = [
        jax.ShapeDtypeStruct((T, D_MODEL), BF16),
        jax.ShapeDtypeStruct((T, 2 * D_FF), BF16),
        jax.ShapeDtypeStruct((T, D_MODEL), F32),
        jax.ShapeDtypeStruct((1, D_MODEL), F32),
        jax.ShapeDtypeStruct((1, D_MODEL), F32),
        jax.ShapeDtypeStruct((1, 2 * D_FF), F32),
        jax.ShapeDtypeStruct((3, 2 * D_FF), F32),
    ]
    return pl.pallas_call(
        body, name="ffn_bwd", grid=(nt,), out_shape=outs,
        in_specs=[pl.BlockSpec((TM, D_MODEL), rev), pl.BlockSpec((TM, D_MODEL), rev), _const_spec((1, D_MODEL)),
                  pl.BlockSpec((TM, 2 * D_FF), rev), pl.BlockSpec((1, 8, 2 * D_FF), halo), _const_spec((3, 2 * D_FF)),
                  _const_spec((1, 2 * D_FF)), _any_spec(), _any_spec(), pl.BlockSpec((TM, D_MODEL), rev),
                  _const_spec((1, D_MODEL))],
        out_specs=[pl.BlockSpec((TM, D_MODEL), rev), pl.BlockSpec((TM, 2 * D_FF), rev), pl.BlockSpec((TM, D_MODEL), rev),
                   _const_spec((1, D_MODEL)), _const_spec((1, D_MODEL)), _const_spec((1, 2 * D_FF)),
                   _const_spec((3, 2 * D_FF))],
        scratch_shapes=[pltpu.VMEM((N_SHARD, D_MODEL, SHARD_FF), BF16), pltpu.VMEM((D_FF, D_MODEL), BF16),
                        pltpu.VMEM((8, 2 * D_FF), F32), pltpu.SemaphoreType.DMA((2,))],
        compiler_params=_params(),
    )(dx2, y, g4, up, tails, cw, cb, wdown, wup, x1, g3)


def _mix_out_bwd(dx1, mix, g2, wout, og, gg, gnorm):
    T = dx1.shape[0]

    def body(dx1_ref, mix_ref, g2_ref, wout_ref, og_ref, gg_ref, gn_ref,
             dmix_ref, dog_ref, dgg_ref, dosw_ref, dg2_ref, dgn_ref):
        @pl.when(pl.program_id(0) == 0)
        def _():
            dg2_ref[...] = jnp.zeros_like(dg2_ref)
            dgn_ref[...] = jnp.zeros_like(dgn_ref)

        dx1 = dx1_ref[...]
        mix = mix_ref[...]
        r2 = lax.rsqrt(jnp.mean(mix * mix, axis=-1, keepdims=True) + EPS)
        dg2_ref[...] = dg2_ref[...] + jnp.sum(dx1 * mix * r2, axis=0, keepdims=True)
        dmix = _rms_bwd(mix, r2, g2_ref[...], dx1).astype(BF16)
        dmix_ref[...] = dmix
        dcat = _mm_nt(dmix, wout_ref[...])
        dosw_ref[...] = dcat[:, GLA_V:].astype(BF16)
        gn = gn_ref[...]
        dgn = jnp.zeros((1, LANES), F32)
        for j in range(GLA_HEADS):
            hs = slice(128 * j, 128 * j + 128)
            o = og_ref[:, hs]
            r = lax.rsqrt(jnp.mean(o * o, axis=-1, keepdims=True) + EPS)
            gate = gg_ref[:, hs]
            sg = _sigmoid(gate)
            dgated = dcat[:, hs]
            dnorm = dgated * (gate * sg)
            dgg_ref[:, hs] = dgated * (o * r * gn) * (sg * (1.0 + gate * (1.0 - sg)))
            dgn = dgn + jnp.sum(dnorm * o * r, axis=0, keepdims=True)
            dog_ref[:, hs] = _rms_bwd(o, r, gn, dnorm)
        dgn_ref[...] = dgn_ref[...] + dgn

    return pl.pallas_call(
        body, name="mix_out_bwd", grid=(T // TM,),
        out_shape=[jax.ShapeDtypeStruct((T, D_MODEL), BF16), jax.ShapeDtypeStruct((T, GLA_V), F32),
                   jax.ShapeDtypeStruct((T, GLA_V), F32), jax.ShapeDtypeStruct((T, SWA_Q), BF16),
                   jax.ShapeDtypeStruct((1, D_MODEL), F32), jax.ShapeDtypeStruct((1, LANES), F32)],
        in_specs=[_row_spec(TM, D_MODEL), _row_spec(TM, D_MODEL), _const_spec((1, D_MODEL)),
                  _const_spec((D_MODEL, D_MODEL)), _row_spec(TM, GLA_V), _row_spec(TM, GLA_V), _const_spec((1, LANES))],
        out_specs=[_row_spec(TM, D_MODEL), _row_spec(TM, GLA_V), _row_spec(TM, GLA_V), _row_spec(TM, SWA_Q),
                   _const_spec((1, D_MODEL)), _const_spec((1, LANES))],
        compiler_params=_params(),
    )(dx1, mix, g2, wout, og, gg, gnorm)


def _proj_bwd(x, g1, wp, gup, glr, dq, dk, dv, dgg, dsq, dkd, dvd, dz, rc, rsa, rsb, dx1):
    T = x.shape[0]

    def body(x_ref, g1_ref, wp_hbm, gup_ref, glr_ref, dq_ref, dk_ref, dv_ref, dgg_ref, dsq_ref, dkd_ref, dvd_ref,
             dz_ref, rc_ref, rsa_ref, rsb_ref, dx1_ref, dx_ref, dp_ref, dg1_ref, dgup_ref, dgb_ref, wp_v, sem):
        _load_once(wp_hbm, wp_v, sem)

        @pl.when(pl.program_id(0) == 0)
        def _():
            dg1_ref[...] = jnp.zeros_like(dg1_ref)
            dgup_ref[...] = jnp.zeros_like(dgup_ref)
            dgb_ref[...] = jnp.zeros_like(dgb_ref)

        rc_, rsa_, rsb_ = rc_ref[...], rsa_ref[...], rsb_ref[...]
        dp_ref[:, 0:256] = dq_ref[...].astype(BF16)
        dp_ref[:, 256:512] = dk_ref[...].astype(BF16)
        dp_ref[:, 512:1024] = dv_ref[...].astype(BF16)
        dp_ref[:, 1024:1536] = dgg_ref[...].astype(BF16)
        for s in range(4):
            cs = slice(128 * s, 128 * s + 128)
            dp_ref[:, 1536 + 128 * s:1664 + 128 * s] = _rotate_bwd(dsq_ref[:, cs], rc_, rsa_, rsb_).astype(BF16)
        first = _iota((TM, LANES), 1) < 64
        dk0 = dkd_ref[:, 0:128]
        dk1 = dkd_ref[:, 128:256]
        dkr = jnp.where(first, dk0 + pltpu.roll(dk0, 64, 1), dk1 + pltpu.roll(dk1, 64, 1))
        dp_ref[:, 2048:2176] = _rotate_bwd(dkr, rc_, rsa_, rsb_).astype(BF16)
        dv0 = dvd_ref[:, 0:128]
        dv1 = dvd_ref[:, 128:256]
        dp_ref[:, 2176:2304] = jnp.where(first, dv0 + pltpu.roll(dv0, 64, 1), dv1 + pltpu.roll(dv1, 64, 1)).astype(BF16)
        dz = dz_ref[...]
        dzb = dz.astype(BF16)
        dp_ref[:, 2304:2432] = _mm_nt(dzb, gup_ref[...]).astype(BF16)
        dgup_ref[...] = dgup_ref[...] + _mm_tn(glr_ref[...], dzb)
        dgb_ref[...] = dgb_ref[...] + jnp.sum(dz, axis=0, keepdims=True)
        dh1 = _mm_nt(dp_ref[...], wp_v[...])
        xt = x_ref[...]
        r = lax.rsqrt(jnp.mean(xt * xt, axis=-1, keepdims=True) + EPS)
        dg1_ref[...] = dg1_ref[...] + jnp.sum(dh1 * xt * r, axis=0, keepdims=True)
        dx_ref[...] = dx1_ref[...] + _rms_bwd(xt, r, g1_ref[...], dh1)

    row = lambda cols: _row_spec(TM, cols)
    return pl.pallas_call(
        body, name="proj_bwd", grid=(T // TM,),
        out_shape=[jax.ShapeDtypeStruct((T, D_MODEL), F32), jax.ShapeDtypeStruct((T, IN_WIDTH_PAD), BF16),
                   jax.ShapeDtypeStruct((1, D_MODEL), F32), jax.ShapeDtypeStruct((LANES, GLA_QK), F32),
                   jax.ShapeDtypeStruct((1, GLA_QK), F32)],
        in_specs=[row(D_MODEL), _const_spec((1, D_MODEL)), _any_spec(), _const_spec((LANES, GLA_QK)), row(LANES),
                  row(GLA_QK), row(GLA_QK), row(GLA_V), row(GLA_V), row(SWA_Q), row(256), row(256), row(GLA_QK),
                  row(LANES), row(LANES), row(LANES), row(D_MODEL)],
        out_specs=[row(D_MODEL), row(IN_WIDTH_PAD), _const_spec((1, D_MODEL)), _const_spec((LANES, GLA_QK)),
                   _const_spec((1, GLA_QK))],
        scratch_shapes=[pltpu.VMEM((D_MODEL, IN_WIDTH_PAD), BF16), pltpu.SemaphoreType.DMA],
        compiler_params=_params(),
    )(x, g1, wp, gup, glr, dq, dk, dv, dgg, dsq, dkd, dvd, dz, rc, rsa, rsb, dx1)


def _matmul_tn(a, b, tn, name, column_blocks_major=False):
    T, M = a.shape
    N = b.shape[1]
    tk = min(512, T)
    nk = T // tk
    if column_blocks_major:
        out_shape = jax.ShapeDtypeStruct((N // tn, M, tn), F32)
        out_spec = pl.BlockSpec((None, M, tn), lambda j, kk: (j, 0, 0))
    else:
        out_shape = jax.ShapeDtypeStruct((M, N), F32)
        out_spec = pl.BlockSpec((M, tn), lambda j, kk: (0, j))

    def body(a_ref, b_ref, o_ref):
        kk = pl.program_id(1)

        @pl.when(kk == 0)
        def _():
            o_ref[...] = jnp.zeros_like(o_ref)

        o_ref[...] = o_ref[...] + _mm_tn(a_ref[...], b_ref[...])

    return pl.pallas_call(
        body, name=name, grid=(N // tn, nk), out_shape=out_shape,
        in_specs=[pl.BlockSpec((tk, M), lambda j, kk: (kk, 0)), pl.BlockSpec((tk, tn), lambda j, kk: (kk, j))],
        out_specs=out_spec,
        compiler_params=_params(),
    )(a, b)


def _adamw(w, g, m, v, rows, name):
    R, C = w.shape

    def body(w_ref, g_ref, m_ref, v_ref, d_ref, m2_ref, v2_ref):
        g_ = g_ref[...]
        m2 = ADAM_B1 * m_ref[...] + (1.0 - ADAM_B1) * g_
        v2 = ADAM_B2 * v_ref[...] + (1.0 - ADAM_B2) * (g_ * g_)
        m_hat = m2 / (1.0 - ADAM_B1 ** ADAM_STEP)
        v_hat = v2 / (1.0 - ADAM_B2 ** ADAM_STEP)
        d_ref[...] = -ADAM_LR * (m_hat / (jnp.sqrt(v_hat) + ADAM_EPS) + ADAM_WD * w_ref[...])
        m2_ref[...] = m2
        v2_ref[...] = v2

    spec = pl.BlockSpec((rows, C), lambda i: (i, 0))
    return pl.pallas_call(
        body, name=name, grid=(R // rows,), out_shape=[jax.ShapeDtypeStruct((R, C), F32)] * 3,
        in_specs=[spec] * 4, out_specs=[spec] * 3, compiler_params=_params(),
    )(w, g, m, v)


def _place():
    x, y, c = lax.axis_index("x"), lax.axis_index("y"), lax.axis_index("c")
    chips = [(1 - x, y), (x, 1 - y), (1 - x, 1 - y)]
    return x, y, c, chips


def _allgather_shards(parts):
    n = len(parts)

    def body(*refs):
        ins, outs = refs[:n], refs[n:2 * n]
        send_sems, recv_sems, local_sems = refs[2 * n:]
        x, y, c, chips = _place()
        sibling = (x, y, 1 - c)

        def block(k, px, py, half):
            H = parts[k].shape[0] // 2
            return outs[k].at[2 * px + py, pl.ds(half * H, H), :]

        def copy(k, j, px, py, half, to, src=None):
            return pltpu.make_async_remote_copy(
                src_ref=block(k, px, py, half) if src is None else src, dst_ref=block(k, px, py, half),
                send_sem=send_sems.at[6 * k + j], recv_sem=recv_sems.at[6 * k + j], device_id=to, device_id_type=MESH)

        mine = [pltpu.make_async_copy(ins[k], outs[k].at[2 * x + y], local_sems.at[k]) for k in range(n)]
        first, passed = [], []
        for k in range(n):
            H = parts[k].shape[0] // 2
            mine[k].start()
            for j, chip in enumerate(chips):
                first.append(copy(k, j, x, y, c, (*chip, c), src=ins[k].at[pl.ds(c * H, H), :]))
                first[-1].start()
        for j, chip in enumerate(chips):
            for k in range(n):
                copy(k, j, *chip, c, (x, y, c)).wait_recv()
                passed.append(copy(k, 3 + j, *chip, c, sibling))
                passed[-1].start()
        for j, chip in enumerate(chips):
            for k in range(n):
                copy(k, 3 + j, *chip, 1 - c, (x, y, c)).wait_recv()
        for cp in first + passed:
            cp.wait_send()
        for cp in mine:
            cp.wait()

    return pl.pallas_call(
        body, name="allgather_shards", out_shape=[jax.ShapeDtypeStruct((N_SHARD,) + p.shape, p.dtype) for p in parts],
        in_specs=[_any_spec()] * n, out_specs=[_any_spec()] * n,
        scratch_shapes=[pltpu.SemaphoreType.DMA((6 * n,)), pltpu.SemaphoreType.DMA((6 * n,)), pltpu.SemaphoreType.DMA((n,))],
    )(*parts)


def _d2d_pieces(rows, piece_rows):
    return [(r, piece_rows) for r in range(0, rows, piece_rows)]


def _rs_pair_swap(arrs, piece_rows):
    n = len(arrs)

    def body(*refs):
        ins, outs = refs[:n], refs[n:2 * n]
        send_sems, recv_sems = refs[2 * n:]
        x, y, c, _ = _place()
        sibling = (x, y, 1 - c)
        for k in range(n):
            H = arrs[k].shape[1] // 2
            for s in range(N_SHARD):
                for r, pr in _d2d_pieces(H, piece_rows[k]):
                    pltpu.make_async_remote_copy(
                        src_ref=ins[k].at[s, pl.ds((1 - c) * H + r, pr), :], dst_ref=outs[k].at[s, pl.ds(r, pr), :],
                        send_sem=send_sems.at[k], recv_sem=recv_sems.at[k], device_id=sibling, device_id_type=MESH).start()
        for k in range(n):
            H = arrs[k].shape[1] // 2
            whole = pltpu.make_async_remote_copy(
                src_ref=ins[k].at[:, pl.ds(0, H), :], dst_ref=outs[k], send_sem=send_sems.at[k], recv_sem=recv_sems.at[k],
                device_id=sibling, device_id_type=MESH)
            whole.wait_recv()
            whole.wait_send()

    return pl.pallas_call(
        body, name="rs_pair_swap",
        out_shape=[jax.ShapeDtypeStruct((N_SHARD, a.shape[1] // 2, a.shape[2]), F32) for a in arrs],
        in_specs=[_any_spec()] * n, out_specs=[_any_spec()] * n,
        scratch_shapes=[pltpu.SemaphoreType.DMA((n,)), pltpu.SemaphoreType.DMA((n,))],
    )(*arrs)


def _rs_add_pair(a, got, core, rows, name):
    _, H, C = got.shape
    nb = H // rows

    def body(c_ref, a_ref, b_ref, o_ref):
        o_ref[...] = a_ref[...] + b_ref[...]

    spec = pl.BlockSpec((1, rows, C), lambda s, r, c_ref: (s, r, 0))
    return pl.pallas_call(
        body, name=name, out_shape=jax.ShapeDtypeStruct(got.shape, F32),
        grid_spec=pltpu.PrefetchScalarGridSpec(
            num_scalar_prefetch=1, grid=(N_SHARD, nb),
            in_specs=[pl.BlockSpec((1, rows, C), lambda s, r, c_ref: (s, c_ref[0] * nb + r, 0)), spec], out_specs=spec),
        compiler_params=_params(),
    )(core, a, got)


def _rs_chip_scatter(parts):
    n = len(parts)

    def body(*refs):
        ins, outs = refs[:n], refs[n:2 * n]
        send_sems, recv_sems, local_sems = refs[2 * n:]
        x, y, c, chips = _place()
        me = 2 * x + y
        mine, sends = [], []
        for k in range(n):
            mine.append(pltpu.make_async_copy(ins[k].at[me], outs[k].at[me], local_sems.at[k]))
            mine[-1].start()
            for j, (px, py) in enumerate(chips):
                sends.append(pltpu.make_async_remote_copy(
                    src_ref=ins[k].at[2 * px + py], dst_ref=outs[k].at[me], send_sem=send_sems.at[3 * k + j],
                    recv_sem=recv_sems.at[3 * k + j], device_id=(px, py, c), device_id_type=MESH))
                sends[-1].start()
        for k in range(n):
            for j, (px, py) in enumerate(chips):
                pltpu.make_async_remote_copy(
                    src_ref=ins[k].at[me], dst_ref=outs[k].at[2 * px + py], send_sem=send_sems.at[3 * k + j],
                    recv_sem=recv_sems.at[3 * k + j], device_id=(px, py, c), device_id_type=MESH).wait_recv()
        for cp in sends:
            cp.wait_send()
        for cp in mine:
            cp.wait()

    return pl.pallas_call(
        body, name="rs_chip_scatter", out_shape=[jax.ShapeDtypeStruct(p.shape, F32) for p in parts],
        in_specs=[_any_spec()] * n, out_specs=[_any_spec()] * n,
        scratch_shapes=[pltpu.SemaphoreType.DMA((3 * n,)), pltpu.SemaphoreType.DMA((3 * n,)), pltpu.SemaphoreType.DMA((n,))],
    )(*parts)


def _rs_sum_chips(parts, rows, name):
    _, H, C = parts.shape

    def body(p_ref, o_ref):
        o_ref[...] = ((p_ref[0] + p_ref[1]) + p_ref[2]) + p_ref[3]

    return pl.pallas_call(
        body, name=name, grid=(H // rows,), out_shape=jax.ShapeDtypeStruct((H, C), F32),
        in_specs=[pl.BlockSpec((N_SHARD, rows, C), lambda r: (0, r, 0))],
        out_specs=pl.BlockSpec((rows, C), lambda r: (r, 0)), compiler_params=_params(),
    )(parts)


def _rs_pair_share(halves, piece_rows):
    n = len(halves)

    def body(*refs):
        ins, outs = refs[:n], refs[n:2 * n]
        send_sems, recv_sems, local_sems = refs[2 * n:]
        x, y, c, _ = _place()
        sibling = (x, y, 1 - c)
        mine = []
        for k in range(n):
            H = halves[k].shape[0]
            mine.append(pltpu.make_async_copy(ins[k], outs[k].at[pl.ds(c * H, H), :], local_sems.at[k]))
            mine[-1].start()
            for r, pr in _d2d_pieces(H, piece_rows[k]):
                pltpu.make_async_remote_copy(
                    src_ref=ins[k].at[pl.ds(r, pr), :], dst_ref=outs[k].at[pl.ds(c * H + r, pr), :],
                    send_sem=send_sems.at[k], recv_sem=recv_sems.at[k], device_id=sibling, device_id_type=MESH).start()
        for k in range(n):
            H = halves[k].shape[0]
            whole = pltpu.make_async_remote_copy(
                src_ref=ins[k], dst_ref=outs[k].at[pl.ds((1 - c) * H, H), :], send_sem=send_sems.at[k],
                recv_sem=recv_sems.at[k], device_id=sibling, device_id_type=MESH)
            whole.wait_recv()
            whole.wait_send()
        for cp in mine:
            cp.wait()

    return pl.pallas_call(
        body, name="rs_pair_share", out_shape=[jax.ShapeDtypeStruct((2 * h.shape[0], h.shape[1]), F32) for h in halves],
        in_specs=[_any_spec()] * n, out_specs=[_any_spec()] * n,
        scratch_shapes=[pltpu.SemaphoreType.DMA((n,)), pltpu.SemaphoreType.DMA((n,)), pltpu.SemaphoreType.DMA((n,))],
    )(*halves)


def _allreduce_small(vec):
    def body(v_ref, o_ref, all_ref, send_sems, recv_sems):
        x, y, c, _ = _place()
        me = 4 * x + 2 * y + c
        all_ref[me] = v_ref[...]
        sends = []
        for k in range(1, 8):
            kx, ky, kc = (k >> 2) & 1, (k >> 1) & 1, k & 1
            peer = (x ^ kx, y ^ ky, c ^ kc)
            cp = pltpu.make_async_remote_copy(
                src_ref=v_ref, dst_ref=all_ref.at[me], send_sem=send_sems.at[k - 1], recv_sem=recv_sems.at[k - 1],
                device_id=peer, device_id_type=MESH)
            cp.start()
            sends.append(cp)
        for k in range(1, 8):
            kx, ky, kc = (k >> 2) & 1, (k >> 1) & 1, k & 1
            src = 4 * (x ^ kx) + 2 * (y ^ ky) + (c ^ kc)
            pltpu.make_async_remote_copy(
                src_ref=v_ref, dst_ref=all_ref.at[src], send_sem=send_sems.at[k - 1], recv_sem=recv_sems.at[k - 1],
                device_id=(x, y, c), device_id_type=MESH).wait_recv()
        for cp in sends:
            cp.wait_send()
        total = all_ref[0]
        for d in range(1, 8):
            total = total + all_ref[d]
        o_ref[...] = total

    vm = pl.BlockSpec(memory_space=pltpu.VMEM)
    return pl.pallas_call(
        body, name="allreduce_small", out_shape=jax.ShapeDtypeStruct(vec.shape, F32), in_specs=[vm], out_specs=vm,
        scratch_shapes=[pltpu.VMEM((8,) + vec.shape, F32), pltpu.SemaphoreType.DMA((7,)), pltpu.SemaphoreType.DMA((7,))],
    )(vec)


BIG_NAMES = ("w_in", "w_out", "w_up", "w_down")
MATRIX_NAMES = BIG_NAMES + ("gla_gate_up", "conv_w")
GATE_SHARD = (16, GLA_QK // N_SHARD)
CONVW_SHARD = (3, SHARD_FF)
SMALL_W_ROWS = 96
SMALL_G_ROWS = 48
PIECE_ROWS = (128, 128, 64, 88, SMALL_G_ROWS // 2)
ADD_ROWS = (256, 128, 256, 176, SMALL_G_ROWS // 2)


def _pad_rows(flat, rows):
    return jnp.pad(flat, (0, rows * LANES - flat.shape[0])).reshape(rows, LANES)


def _pack_small_weights(gate_up, conv_w):
    bits = lax.bitcast_convert_type(conv_w, BF16)
    return _pad_rows(jnp.concatenate([gate_up.astype(BF16).reshape(-1), bits.reshape(-1)]), SMALL_W_ROWS)


def _unpack_small_weights(packed):
    flat = packed.reshape(N_SHARD, -1)
    n_gate = GATE_SHARD[0] * GATE_SHARD[1]
    n_conv = 2 * CONVW_SHARD[0] * CONVW_SHARD[1]
    gate = flat[:, :n_gate].reshape((N_SHARD,) + GATE_SHARD)
    conv = lax.bitcast_convert_type(flat[:, n_gate:n_gate + n_conv].reshape((N_SHARD,) + CONVW_SHARD + (2,)), F32)
    return (jnp.transpose(gate, (1, 0, 2)).reshape(16, GLA_QK), jnp.transpose(conv, (1, 0, 2)).reshape(3, 2 * D_FF))


def _pack_small_grads(dgate, dconv):
    rows = []
    for s in range(N_SHARD):
        g = dgate[:, GATE_SHARD[1] * s:GATE_SHARD[1] * (s + 1)].reshape(-1)
        cw = dconv[:, SHARD_FF * s:SHARD_FF * (s + 1)].reshape(-1)
        rows.append(_pad_rows(jnp.concatenate([g, cw]), SMALL_G_ROWS))
    return jnp.stack(rows)


def _unpack_small_grads(packed):
    flat = packed.reshape(-1)
    n_gate = GATE_SHARD[0] * GATE_SHARD[1]
    n_conv = CONVW_SHARD[0] * CONVW_SHARD[1]
    return flat[:n_gate].reshape(GATE_SHARD), flat[n_gate:n_gate + n_conv].reshape(CONVW_SHARD)


def _permute_w_in(w):
    pad = jnp.zeros((w.shape[0], IN_WIDTH_PAD - IN_WIDTH), w.dtype)
    return jnp.concatenate([w[:, 0:1024], w[:, 1040:2320], w[:, 1024:1040], pad], axis=1)


def _unpermute_w_in(wp):
    return jnp.concatenate([wp[:, 0:1024], wp[:, 2304:2320], wp[:, 1024:2304]], axis=1)


def _rope_tables(positions):
    half = ROPE_DIM // 2
    inv_freq = ROPE_THETA ** (-jnp.arange(half, dtype=F32) * (2.0 / ROPE_DIM))
    ang = positions.astype(F32)[:, None] * inv_freq
    cos, sin = jnp.cos(ang), jnp.sin(ang)
    T = positions.shape[0]
    z = lambda n: jnp.zeros((T, n), F32)
    rc = jnp.concatenate([cos, cos, jnp.ones((T, 48), F32)], axis=1)
    rsa = jnp.concatenate([-sin, z(56)], axis=1)
    rsb = jnp.concatenate([z(8), sin, z(48)], axis=1)
    return tuple(jnp.concatenate([t, t], axis=1) for t in (rc, rsa, rsb))


SMALL_NAMES = (("pre_mix_norm", 1024), ("gla_gate_bias", 256), ("gla_out_norm", 128), ("swa_sinks", 8),
               ("post_mix_norm", 1024), ("pre_ffn_norm", 1024), ("conv_b", 5632), ("post_ffn_norm", 1024))


def _pack_small(vals, loss):
    parts = [vals[n].reshape(-1) for n, _ in SMALL_NAMES] + [loss.reshape(1)]
    flat = jnp.concatenate(parts)
    return jnp.pad(flat, (0, SMALL_ROWS * LANES - flat.shape[0])).reshape(SMALL_ROWS, LANES)


def _unpack_small(packed):
    flat = packed.reshape(-1)
    out, off = {}, 0
    for n, size in SMALL_NAMES:
        out[n] = flat[off:off + size].reshape(1, size)
        off += size
    return out, flat[off]


def _local_step(x, positions, target, w, small):
    rc, rsa, rsb = _rope_tables(positions)
    wp = w["wp"]
    gup = jnp.pad(w["gla_gate_up"], ((0, LANES - 16), (0, 0)))
    g1, g2, g3, g4 = (small[n] for n in ("pre_mix_norm", "post_mix_norm", "pre_ffn_norm", "post_ffn_norm"))
    gbias, gnorm, cb = small["gla_gate_bias"], small["gla_out_norm"], small["conv_b"]
    sinks = small["swa_sinks"].reshape(-1)
    cw = w["conv_w"]

    h1, q, k, v, la, gg, sq, kd, vd, glr = _proj_fwd(x, g1, wp, gup, gbias, rc, rsa, rsb)
    og, s_all = _gla_fwd(q, k, v, la)
    osw = _swa_fwd(sq, kd, vd, sinks)
    x1, cat, mix = _mix_out_fwd(x, og, gg, osw, gnorm, w["w_out"], g2)
    h2, up, act, y, dx2, loss, tails = _ffn_fwd(x1, g3, w["w_up4"], cw, cb, w["w_down"], g4, target)

    dy, dup, dx1, dg4, dg3, dcb, dcw = _ffn_bwd(dx2, y, g4, up, tails, cw, cb, w["w_down"], w["w_up4"], x1, g3)
    dmix, dog, dgg, dosw, dg2, dgn = _mix_out_bwd(dx1, mix, g2, w["w_out"], og, gg, gnorm)
    dsq, dkd, dvd, dsink = _swa_bwd(sq, kd, vd, sinks, dosw)
    dq, dk, dv, dz = _gla_bwd(q, k, v, la, s_all, dog)
    dx, dproj, dg1, dgup, dgb = _proj_bwd(x, g1, wp, gup, glr, dq, dk, dv, dgg, dsq, dkd, dvd, dz, rc, rsa, rsb, dx1)

    grads = {
        "wp": _matmul_tn(h1, dproj, IN_WIDTH_PAD, "grad_w_in"),
        "w_out": _matmul_tn(cat, dmix, D_MODEL, "grad_w_out"),
        "w_up4": _matmul_tn(h2, dup, SHARD_FF, "grad_w_up", column_blocks_major=True),
        "w_down": _matmul_tn(act, dy, D_MODEL, "grad_w_down"),
        "gla_gate_up": dgup[0:16],
        "conv_w": dcw,
    }
    small_grads = {
        "pre_mix_norm": dg1, "gla_gate_bias": dgb, "gla_out_norm": dgn, "swa_sinks": dsink[:, 0].reshape(1, 8),
        "post_mix_norm": dg2, "pre_ffn_norm": dg3, "conv_b": dcb, "post_ffn_norm": dg4,
    }
    return loss[0, 0], dx, grads, small_grads


ADAM_ROWS = {"w_in": 256, "w_out": 256, "w_up": 256, "w_down": 176}
WEIGHT_ORDER = ("pre_mix_norm", "w_in", "gla_gate_up", "gla_gate_bias", "gla_out_norm", "swa_sinks", "w_out",
                "post_mix_norm", "pre_ffn_norm", "w_up", "conv_w", "conv_b", "w_down", "post_ffn_norm")
TINY_ROWS = 152


def _pack_tiny(vals):
    flat = jnp.concatenate([vals[n].reshape(-1) for n in TINY_NAMES])
    return jnp.pad(flat, (0, TINY_ROWS * LANES - flat.shape[0])).reshape(TINY_ROWS, LANES)


TINY_NAMES = tuple(n for n, _ in SMALL_NAMES) + ("gla_gate_up", "conv_w")
TINY_SHAPES = {**{n: (1, s) for n, s in SMALL_NAMES}, "gla_gate_up": (16, 64), "conv_w": (3, 1408)}


def _unpack_tiny(packed):
    flat = packed.reshape(-1)
    out, off = {}, 0
    for n in TINY_NAMES:
        shape = TINY_SHAPES[n]
        size = shape[0] * shape[1]
        out[n] = flat[off:off + size].reshape(shape)
        off += size
    return out


def kernel(x, positions, pre_mix_norm, w_in, gla_gate_up, gla_gate_bias, gla_out_norm, swa_sinks, w_out, post_mix_norm, pre_ffn_norm, w_up, conv_w, conv_b, w_down, post_ffn_norm, loss_target, m_pre_mix_norm, m_w_in, m_gla_gate_up, m_gla_gate_bias, m_gla_out_norm, m_swa_sinks, m_w_out, m_post_mix_norm, m_pre_ffn_norm, m_w_up, m_conv_w, m_conv_b, m_w_down, m_post_ffn_norm, v_pre_mix_norm, v_w_in, v_gla_gate_up, v_gla_gate_bias, v_gla_out_norm, v_swa_sinks, v_w_out, v_post_mix_norm, v_pre_ffn_norm, v_w_up, v_conv_w, v_conv_b, v_w_down, v_post_ffn_norm):
    weights = dict(pre_mix_norm=pre_mix_norm, w_in=w_in, gla_gate_up=gla_gate_up, gla_gate_bias=gla_gate_bias,
                   gla_out_norm=gla_out_norm, swa_sinks=swa_sinks, w_out=w_out, post_mix_norm=post_mix_norm,
                   pre_ffn_norm=pre_ffn_norm, w_up=w_up, conv_w=conv_w, conv_b=conv_b, w_down=w_down,
                   post_ffn_norm=post_ffn_norm)
    mom = dict(pre_mix_norm=m_pre_mix_norm, w_in=m_w_in, gla_gate_up=m_gla_gate_up, gla_gate_bias=m_gla_gate_bias,
               gla_out_norm=m_gla_out_norm, swa_sinks=m_swa_sinks, w_out=m_w_out, post_mix_norm=m_post_mix_norm,
               pre_ffn_norm=m_pre_ffn_norm, w_up=m_w_up, conv_w=m_conv_w, conv_b=m_conv_b, w_down=m_w_down,
               post_ffn_norm=m_post_ffn_norm)
    var = dict(pre_mix_norm=v_pre_mix_norm, w_in=v_w_in, gla_gate_up=v_gla_gate_up, gla_gate_bias=v_gla_gate_bias,
               gla_out_norm=v_gla_out_norm, swa_sinks=v_swa_sinks, w_out=v_w_out, post_mix_norm=v_post_mix_norm,
               pre_ffn_norm=v_pre_ffn_norm, w_up=v_w_up, conv_w=v_conv_w, conv_b=v_conv_b, w_down=v_w_down,
               post_ffn_norm=v_post_ffn_norm)
    weights, mom, var = ({n: a[0] if a.ndim == 3 else a for n, a in d.items()} for d in (weights, mom, var))

    win4, wout4, wup4, wdown4, small4 = _allgather_shards(
        [weights[n].astype(BF16) for n in BIG_NAMES] + [_pack_small_weights(weights["gla_gate_up"], weights["conv_w"])])
    gate_full, convw_full = _unpack_small_weights(small4)
    full = {
        "wp": _permute_w_in(jnp.transpose(win4, (1, 0, 2)).reshape(D_MODEL, IN_WIDTH)),
        "w_out": wout4.reshape(D_MODEL, D_MODEL), "w_up4": wup4, "w_down": wdown4.reshape(D_FF, D_MODEL),
        "gla_gate_up": gate_full, "conv_w": convw_full,
    }
    small = {n: weights[n] for n, _ in SMALL_NAMES}
    loss, dx, grads, small_grads = _local_step(x[0], positions[0], loss_target[0], full, small)

    per_shard = [
        jnp.transpose(_unpermute_w_in(grads["wp"]).reshape(D_MODEL, N_SHARD, IN_WIDTH // N_SHARD), (1, 0, 2)),
        grads["w_out"].reshape(N_SHARD, D_MODEL // N_SHARD, D_MODEL), grads["w_up4"],
        grads["w_down"].reshape(N_SHARD, D_FF // N_SHARD, D_MODEL),
        _pack_small_grads(grads["gla_gate_up"], grads["conv_w"]),
    ]
    labels = BIG_NAMES + ("small",)
    core = lax.axis_index("c").astype(jnp.int32).reshape(1)
    got = _rs_pair_swap(per_shard, PIECE_ROWS)
    partial = [_rs_add_pair(a, g, core, rows, "rs_add_pair_" + n) for a, g, rows, n in zip(per_shard, got, ADD_ROWS, labels)]
    landed = _rs_chip_scatter(partial)
    halves = [_rs_sum_chips(p, rows, "rs_sum_chips_" + n) for p, rows, n in zip(landed, ADD_ROWS, labels)]
    reduced = _rs_pair_share(halves, PIECE_ROWS)
    g_small, loss_sum = _unpack_small(_allreduce_small(_pack_small(small_grads, loss)))
    g_gate, g_convw = _unpack_small_grads(reduced[4])
    g_all = {**g_small, **dict(zip(BIG_NAMES, reduced[:4])), "gla_gate_up": g_gate, "conv_w": g_convw}

    delta, new_m, new_v = {}, {}, {}
    for n in BIG_NAMES:
        delta[n], new_m[n], new_v[n] = _adamw(weights[n], g_all[n], mom[n], var[n], ADAM_ROWS[n], "adamw_" + n)
    tiny = _adamw(*(_pack_tiny({n: d[n] for n in TINY_NAMES}) for d in (weights, g_all, mom, var)), TINY_ROWS, "adamw_small")
    for res, packed in zip((delta, new_m, new_v), tiny):
        res.update(_unpack_tiny(packed))

    def lead(n, a):
        return a[None] if n in MATRIX_NAMES else a

    outs = [loss_sum, dx[None]]
    for d in (g_all, delta, new_m, new_v):
        outs.extend(lead(n, d[n]) for n in WEIGHT_ORDER)
    return tuple(outs)
```

```python
import functools

import jax
import jax.numpy as jnp
from jax import lax
from jax.experimental import pallas as pl
from jax.experimental.pallas import tpu as pltpu

F32 = jnp.float32
BF16 = jnp.bfloat16
MESH = pl.DeviceIdType.MESH

D_MODEL = 1024
GLA_HEADS = 4
GLA_DK = 64
GLA_DV = 128
GLA_TAU = 16.0
GLA_CHUNK = 64
SWA_HEADS = 8
SWA_HD = 64
SWA_BLOCK = 128
ROPE_THETA = 500000.0
ROPE_DIM = 16
D_FF = 2816
EPS = 1e-6
GLA_QK = 256
GLA_V = 512
SWA_Q = 512
SWA_KV = 128
IN_WIDTH = 2320
IN_WIDTH_PAD = 2432
N_SHARD = 4

ADAM_LR = 0.001
ADAM_B1 = 0.9
ADAM_B2 = 0.999
ADAM_EPS = 1e-08
ADAM_WD = 0.01
ADAM_STEP = 10

LANES = 128
VMEM_LIMIT = 56 * 1024 * 1024
TM = 256
SHARD_FF = 2 * D_FF // N_SHARD
FF_PIECES = ((0, 512), (512, 512), (1024, 384))
GLA_BLOCK = 256

SMALL_ROWS = 80


def _params(**kw):
    return pltpu.CompilerParams(vmem_limit_bytes=VMEM_LIMIT, **kw)


def _mm(a, b):
    return lax.dot_general(a.astype(BF16), b.astype(BF16), (((1,), (0,)), ((), ())), preferred_element_type=F32)


def _mm_nt(a, b):
    return lax.dot_general(a.astype(BF16), b.astype(BF16), (((1,), (1,)), ((), ())), preferred_element_type=F32)


def _mm_tn(a, b):
    return lax.dot_general(a.astype(BF16), b.astype(BF16), (((0,), (0,)), ((), ())), preferred_element_type=F32)


def _mm_f32(a, b):
    return lax.dot_general(a, b, (((1,), (0,)), ((), ())), preferred_element_type=F32, precision=lax.Precision.HIGHEST)


def _iota(shape, dim):
    return lax.broadcasted_iota(jnp.int32, shape, dim)


def _sigmoid(x):
    return 1.0 / (1.0 + jnp.exp(-x))


def _gelu_parts(x):
    c = 0.7978845608028654
    x2 = x * x
    t = jnp.tanh(c * (x + 0.044715 * (x2 * x)))
    cdf = 0.5 * (1.0 + t)
    dcdf = 0.5 * (1.0 - t * t) * c * (1.0 + 3.0 * 0.044715 * x2)
    return x * cdf, cdf + x * dcdf


def _rms_bwd(v, r, g, dout):
    gd = g * dout
    return r * gd - v * (r * r * r) * jnp.mean(v * gd, axis=-1, keepdims=True)


def _row_spec(tm, cols):
    return pl.BlockSpec((tm, cols), lambda i: (i, 0))


def _const_spec(shape):
    return pl.BlockSpec(shape, lambda i: (0,) * len(shape))


def _any_spec():
    return pl.BlockSpec(memory_space=pl.ANY)


def _load_once(src_hbm, dst_vmem, sem):
    @pl.when(pl.program_id(0) == 0)
    def _():
        cp = pltpu.make_async_copy(src_hbm, dst_vmem, sem)
        cp.start()
        cp.wait()


def _rotate(v, rc, rsa, rsb):
    return v * rc + pltpu.roll(v, 120, 1) * rsa + pltpu.roll(v, 8, 1) * rsb


def _rotate_bwd(dv, rc, rsa, rsb):
    return dv * rc + pltpu.roll(dv * rsa, 8, 1) + pltpu.roll(dv * rsb, 120, 1)


def _proj_fwd(x, g1, wp, gup, gbias, rc, rsa, rsb):
    T = x.shape[0]

    def body(x_ref, g1_ref, wp_hbm, gup_ref, gb_ref, rc_ref, rsa_ref, rsb_ref,
             h1_ref, q_ref, k_ref, v_ref, la_ref, gg_ref, sq_ref, kd_ref, vd_ref, glr_ref, wp_v, sem):
        _load_once(wp_hbm, wp_v, sem)
        xt = x_ref[...]
        r = lax.rsqrt(jnp.mean(xt * xt, axis=-1, keepdims=True) + EPS)
        h = (xt * r * g1_ref[...]).astype(BF16)
        h1_ref[...] = h
        q_ref[...] = _mm(h, wp_v[:, 0:256])
        k_ref[...] = _mm(h, wp_v[:, 256:512])
        v_ref[...] = _mm(h, wp_v[:, 512:1024])
        gg_ref[...] = _mm(h, wp_v[:, 1024:1536])
        glr = _mm(h, wp_v[:, 2304:2432]).astype(BF16)
        glr_ref[...] = glr
        z = _mm(glr, gup_ref[...]) + gb_ref[...]
        la_ref[...] = (jnp.minimum(z, 0.0) - jnp.log1p(jnp.exp(-jnp.abs(z)))) * (1.0 / GLA_TAU)
        rc_, rsa_, rsb_ = rc_ref[...], rsa_ref[...], rsb_ref[...]
        for s in range(4):
            qs = _mm(h, wp_v[:, 1536 + 128 * s:1664 + 128 * s])
            sq_ref[:, 128 * s:128 * s + 128] = (_rotate(qs, rc_, rsa_, rsb_) * 0.125).astype(BF16)
        lane = _iota((TM, LANES), 1)
        first = lane < 64
        kr = _rotate(_mm(h, wp_v[:, 2048:2176]), rc_, rsa_, rsb_)
        krr = pltpu.roll(kr, 64, 1)
        kd_ref[:, 0:128] = jnp.where(first, kr, krr).astype(BF16)
        kd_ref[:, 128:256] = jnp.where(first, krr, kr).astype(BF16)
        vr = _mm(h, wp_v[:, 2176:2304])
        vrr = pltpu.roll(vr, 64, 1)
        vd_ref[:, 0:128] = jnp.where(first, vr, vrr).astype(BF16)
        vd_ref[:, 128:256] = jnp.where(first, vrr, vr).astype(BF16)

    outs = [
        jax.ShapeDtypeStruct((T, D_MODEL), BF16),
        jax.ShapeDtypeStruct((T, GLA_QK), F32),
        jax.ShapeDtypeStruct((T, GLA_QK), F32),
        jax.ShapeDtypeStruct((T, GLA_V), F32),
        jax.ShapeDtypeStruct((T, GLA_QK), F32),
        jax.ShapeDtypeStruct((T, GLA_V), F32),
        jax.ShapeDtypeStruct((T, SWA_Q), BF16),
        jax.ShapeDtypeStruct((T, 256), BF16),
        jax.ShapeDtypeStruct((T, 256), BF16),
        jax.ShapeDtypeStruct((T, LANES), BF16),
    ]
    return pl.pallas_call(
        body, name="proj_fwd", grid=(T // TM,), out_shape=outs,
        in_specs=[_row_spec(TM, D_MODEL), _const_spec((1, D_MODEL)), _any_spec(), _const_spec((LANES, GLA_QK)),
                  _const_spec((1, GLA_QK)), _row_spec(TM, LANES), _row_spec(TM, LANES), _row_spec(TM, LANES)],
        out_specs=[_row_spec(TM, o.shape[1]) for o in outs],
        scratch_shapes=[pltpu.VMEM((D_MODEL, IN_WIDTH_PAD), BF16), pltpu.SemaphoreType.DMA],
        compiler_params=_params(),
    )(x, g1, wp, gup, gbias, rc, rsa, rsb)


def _gla_masks():
    C = GLA_CHUNK
    lane = _iota((C, LANES), 1)
    lane_masks = [(lane < 64).astype(F32), (lane >= 64).astype(F32)]
    causal = _iota((C, C), 0) >= _iota((C, C), 1)
    blk = ((_iota((256, LANES), 0) >> 7) == (_iota((256, LANES), 1) >> 6)).astype(F32)
    return lane_masks, causal, blk


def _gla_chunk_terms(q_ref, k_ref, b_ref, c, p):
    C = GLA_CHUNK
    rows = slice(C * c, C * c + C)
    cols = slice(LANES * p, LANES * p + LANES)
    bc = b_ref[rows, cols]
    bl = b_ref[C * c + C - 1:C * c + C, cols]
    bm = b_ref[C * c + C // 2 - 1:C * c + C // 2, cols]
    qs = q_ref[rows, cols] * 0.125
    kk = k_ref[rows, cols]
    eb = jnp.exp(bc)
    ekl = jnp.exp(bl - bc)
    eqm = jnp.exp(bc - bm)
    ekm = jnp.exp(bm - bc)
    return qs, kk, eb, ekl, eqm, ekm, jnp.exp(bl)


def _block_cumsum(la):
    n = la.shape[0]
    row, col = _iota((n, n), 0), _iota((n, n), 1)
    tril = (((row >> 6) == (col >> 6)) & (col <= row)).astype(F32)
    return _mm_f32(tril, la)


def _gla_fwd(q, k, v, la):
    T = q.shape[0]
    NB = GLA_BLOCK // GLA_CHUNK
    C = GLA_CHUNK

    def body(q_ref, k_ref, v_ref, la_ref, o_ref, s_ref, st_ref, b_ref):
        @pl.when(pl.program_id(0) == 0)
        def _():
            st_ref[...] = jnp.zeros_like(st_ref)

        lane_masks, causal, blk = _gla_masks()
        b_ref[...] = _block_cumsum(la_ref[...])
        for c in range(NB):
            rows = slice(C * c, C * c + C)
            for p in range(2):
                qs, kk, eb, ekl, eqm, ekm, gam = _gla_chunk_terms(q_ref, k_ref, b_ref, c, p)
                qh, kh, qm, km = qs * eb, kk * ekl, qs * eqm, kk * ekm
                vp = v_ref[rows, 256 * p:256 * p + 256]
                st = st_ref[p]
                s_ref[c, p] = st
                o_inter = _mm_nt(qh, st)
                for j in range(2):
                    a = jnp.where(causal, _mm_nt(qm * lane_masks[j], km), 0.0)
                    o_ref[rows, 256 * p + 128 * j:256 * p + 128 * j + 128] = (
                        o_inter[:, 128 * j:128 * j + 128] + _mm(a, vp[:, 128 * j:128 * j + 128]))
                st_ref[p] = st * gam + blk * _mm_tn(vp, kh)

    return pl.pallas_call(
        body, name="gla_fwd", grid=(T // GLA_BLOCK,),
        out_shape=[jax.ShapeDtypeStruct((T, GLA_V), F32), jax.ShapeDtypeStruct((T // C, 2, 256, LANES), F32)],
        in_specs=[_row_spec(GLA_BLOCK, GLA_QK), _row_spec(GLA_BLOCK, GLA_QK), _row_spec(GLA_BLOCK, GLA_V),
                  _row_spec(GLA_BLOCK, GLA_QK)],
        out_specs=[_row_spec(GLA_BLOCK, GLA_V), pl.BlockSpec((NB, 2, 256, LANES), lambda i: (i, 0, 0, 0))],
        scratch_shapes=[pltpu.VMEM((2, 256, LANES), F32), pltpu.VMEM((GLA_BLOCK, GLA_QK), F32)],
        compiler_params=_params(),
    )(q, k, v, la)


def _gla_bwd(q, k, v, la, s_all, do):
    T = q.shape[0]
    NB = GLA_BLOCK // GLA_CHUNK
    C = GLA_CHUNK
    nblk = T // GLA_BLOCK

    def body(q_ref, k_ref, v_ref, la_ref, s_ref, do_ref, dq_ref, dk_ref, dv_ref, dz_ref, dst_ref, b_ref):
        @pl.when(pl.program_id(0) == 0)
        def _():
            dst_ref[...] = jnp.zeros_like(dst_ref)

        lane_masks, causal, blk = _gla_masks()
        triu = (_iota((C, C), 1) >= _iota((C, C), 0)).astype(F32)
        b_ref[...] = _block_cumsum(la_ref[...])
        for c in reversed(range(NB)):
            rows = slice(C * c, C * c + C)
            for p in range(2):
                cols = slice(LANES * p, LANES * p + LANES)
                qs, kk, eb, ekl, eqm, ekm, gam = _gla_chunk_terms(q_ref, k_ref, b_ref, c, p)
                qh, kh, qm, km = qs * eb, kk * ekl, qs * eqm, kk * ekm
                vp = v_ref[rows, 256 * p:256 * p + 256]
                dop = do_ref[rows, 256 * p:256 * p + 256]
                st = s_ref[c, p]
                dst = dst_ref[p]
                dqh = _mm(dop, st)
                dkh = _mm(vp, dst)
                dvp = _mm_nt(kh, dst)
                dgam = jnp.sum(st * dst, axis=0, keepdims=True)
                dqm = jnp.zeros((C, LANES), F32)
                dkm = jnp.zeros((C, LANES), F32)
                for j in range(2):
                    hs = slice(128 * j, 128 * j + 128)
                    a = jnp.where(causal, _mm_nt(qm * lane_masks[j], km), 0.0)
                    da = jnp.where(causal, _mm_nt(dop[:, hs], vp[:, hs]), 0.0)
                    dv_ref[rows, 256 * p + 128 * j:256 * p + 128 * j + 128] = dvp[:, hs] + _mm_tn(a, dop[:, hs])
                    dqm = dqm + lane_masks[j] * _mm(da, km)
                    dkm = dkm + lane_masks[j] * _mm_tn(da, qm)
                dqs = dqh * eb + dqm * eqm
                dk = dkh * ekl + dkm * ekm
                db = dqs * qs - dk * kk
                extra = jnp.sum(dkh * kh, axis=0, keepdims=True) + dgam * gam
                dg = _mm_f32(triu, db) + extra
                dq_ref[rows, cols] = dqs * 0.125
                dk_ref[rows, cols] = dk
                dz_ref[rows, cols] = dg * (1.0 - jnp.exp(GLA_TAU * la_ref[rows, cols])) * (1.0 / GLA_TAU)
                dst_ref[p] = dst * gam + blk * _mm_tn(dop, qh)

    rev = lambda i: (nblk - 1 - i, 0)
    rspec = lambda cols: pl.BlockSpec((GLA_BLOCK, cols), rev)
    return pl.pallas_call(
        body, name="gla_bwd", grid=(nblk,),
        out_shape=[jax.ShapeDtypeStruct((T, GLA_QK), F32), jax.ShapeDtypeStruct((T, GLA_QK), F32),
                   jax.ShapeDtypeStruct((T, GLA_V), F32), jax.ShapeDtypeStruct((T, GLA_QK), F32)],
        in_specs=[rspec(GLA_QK), rspec(GLA_QK), rspec(GLA_V), rspec(GLA_QK),
                  pl.BlockSpec((NB, 2, 256, LANES), lambda i: (nblk - 1 - i, 0, 0, 0)), rspec(GLA_V)],
        out_specs=[rspec(GLA_QK), rspec(GLA_QK), rspec(GLA_V), rspec(GLA_QK)],
        scratch_shapes=[pltpu.VMEM((2, 256, LANES), F32), pltpu.VMEM((GLA_BLOCK, GLA_QK), F32)],
        compiler_params=_params(),
    )(q, k, v, la, s_all, do)


def _swa_masks(i):
    W = SWA_BLOCK
    r, c = _iota((W, W), 0), _iota((W, W), 1)
    return (c > r) & (i > 0), c <= r


def _swa_probs(qh, kp, kc, mask_p, mask_c, sink):
    neg = -1e30
    s_p = jnp.where(mask_p, _mm_nt(qh, kp), neg)
    s_c = jnp.where(mask_c, _mm_nt(qh, kc), neg)
    m = jnp.maximum(jnp.maximum(jnp.max(s_p, axis=-1, keepdims=True), jnp.max(s_c, axis=-1, keepdims=True)), sink)
    p_p = jnp.where(mask_p, jnp.exp(s_p - m), 0.0)
    p_c = jnp.where(mask_c, jnp.exp(s_c - m), 0.0)
    p_s = jnp.exp(sink - m)
    denom = jnp.sum(p_p, axis=-1, keepdims=True) + jnp.sum(p_c, axis=-1, keepdims=True) + p_s
    return p_p, p_c, p_s, denom


def _swa_fwd(sq, kd, vd, sinks):
    T = sq.shape[0]
    W = SWA_BLOCK
    prev = lambda i: (jnp.maximum(i - 1, 0), 0)

    def body(sink_ref, q_ref, kp_ref, kc_ref, vp_ref, vc_ref, o_ref):
        i = pl.program_id(0)
        mask_p, mask_c = _swa_masks(i)
        first = _iota((W, LANES), 1) < 64
        zero = jnp.zeros((W, LANES), BF16)
        for m in range(4):
            g = m // 2
            gs = slice(128 * g, 128 * g + 128)
            qpair = q_ref[:, 128 * m:128 * m + 128]
            res = []
            for jj in range(2):
                qh = jnp.where(first, qpair, zero) if jj == 0 else jnp.where(first, zero, qpair)
                p_p, p_c, _, denom = _swa_probs(qh, kp_ref[:, gs], kc_ref[:, gs], mask_p, mask_c, sink_ref[2 * m + jj])
                res.append((_mm(p_p, vp_ref[:, gs]) + _mm(p_c, vc_ref[:, gs])) / denom)
            o_ref[:, 128 * m:128 * m + 128] = jnp.where(first, res[0], res[1]).astype(BF16)

    return pl.pallas_call(
        body, name="swa_fwd", grid=(T // W,), out_shape=jax.ShapeDtypeStruct((T, SWA_Q), BF16),
        in_specs=[pl.BlockSpec(memory_space=pltpu.SMEM), _row_spec(W, SWA_Q), pl.BlockSpec((W, 256), prev),
                  _row_spec(W, 256), pl.BlockSpec((W, 256), prev), _row_spec(W, 256)],
        out_specs=_row_spec(W, SWA_Q),
        compiler_params=_params(),
    )(sinks, sq, kd, kd, vd, vd)


def _swa_bwd(sq, kd, vd, sinks, do):
    T = sq.shape[0]
    W = SWA_BLOCK
    n = T // W
    cur = lambda i: (jnp.minimum(i, n - 1), 0)
    prev = lambda i: (jnp.clip(i - 1, 0, n - 1), 0)

    def body(sink_ref, q_ref, kp_ref, kc_ref, vp_ref, vc_ref, do_ref, dq_ref, dk_ref, dv_ref, ds_ref, ck_ref, cv_ref):
        i = pl.program_id(0)

        @pl.when(i == 0)
        def _():
            ds_ref[...] = jnp.zeros_like(ds_ref)
            ck_ref[...] = jnp.zeros_like(ck_ref)
            cv_ref[...] = jnp.zeros_like(cv_ref)

        @pl.when(i < n)
        def _():
            mask_p, mask_c = _swa_masks(i)
            first = _iota((W, LANES), 1) < 64
            zero = jnp.zeros((W, LANES), BF16)
            for g in range(2):
                gs = slice(128 * g, 128 * g + 128)
                kp, kc, vp, vc = kp_ref[:, gs], kc_ref[:, gs], vp_ref[:, gs], vc_ref[:, gs]
                dk_p = jnp.zeros((W, LANES), F32)
                dk_c = jnp.zeros((W, LANES), F32)
                dv_p = jnp.zeros((W, LANES), F32)
                dv_c = jnp.zeros((W, LANES), F32)
                for mm in range(2):
                    m = 2 * g + mm
                    qpair = q_ref[:, 128 * m:128 * m + 128]
                    dopair = do_ref[:, 128 * m:128 * m + 128]
                    dq_pair = []
                    for jj in range(2):
                        sel = first if jj == 0 else jnp.logical_not(first)
                        qh = jnp.where(sel, qpair, zero)
                        doh = jnp.where(sel, dopair, zero)
                        p_p, p_c, p_s, denom = _swa_probs(qh, kp, kc, mask_p, mask_c, sink_ref[2 * m + jj])
                        inv = 1.0 / denom
                        p_p, p_c, p_s = p_p * inv, p_c * inv, p_s * inv
                        dp_p = _mm_nt(doh, vp)
                        dp_c = _mm_nt(doh, vc)
                        delta = jnp.sum(p_p * dp_p, axis=-1, keepdims=True) + jnp.sum(p_c * dp_c, axis=-1, keepdims=True)
                        ds_p = p_p * (dp_p - delta)
                        ds_c = p_c * (dp_c - delta)
                        h = 2 * m + jj
                        ds_ref[h:h + 1, :] = ds_ref[h:h + 1, :] - jnp.sum(p_s * delta)
                        dq_pair.append((_mm(ds_p, kp) + _mm(ds_c, kc)) * 0.125)
                        dk_p = dk_p + _mm_tn(ds_p, qh)
                        dk_c = dk_c + _mm_tn(ds_c, qh)
                        dv_p = dv_p + _mm_tn(p_p, doh)
                        dv_c = dv_c + _mm_tn(p_c, doh)
                    dq_ref[:, 128 * m:128 * m + 128] = jnp.where(first, dq_pair[0], dq_pair[1])
                dk_ref[:, gs] = ck_ref[:, gs] + dk_p
                dv_ref[:, gs] = cv_ref[:, gs] + dv_p
                ck_ref[:, gs] = dk_c
                cv_ref[:, gs] = dv_c

        @pl.when(i == n)
        def _():
            dk_ref[...] = ck_ref[...]
            dv_ref[...] = cv_ref[...]

    return pl.pallas_call(
        body, name="swa_bwd", grid=(n + 1,),
        out_shape=[jax.ShapeDtypeStruct((T, SWA_Q), F32), jax.ShapeDtypeStruct((T, 256), F32),
                   jax.ShapeDtypeStruct((T, 256), F32), jax.ShapeDtypeStruct((8, LANES), F32)],
        in_specs=[pl.BlockSpec(memory_space=pltpu.SMEM), pl.BlockSpec((W, SWA_Q), cur), pl.BlockSpec((W, 256), prev),
                  pl.BlockSpec((W, 256), cur), pl.BlockSpec((W, 256), prev), pl.BlockSpec((W, 256), cur),
                  pl.BlockSpec((W, SWA_Q), cur)],
        out_specs=[pl.BlockSpec((W, SWA_Q), cur), pl.BlockSpec((W, 256), prev), pl.BlockSpec((W, 256), prev),
                   _const_spec((8, LANES))],
        scratch_shapes=[pltpu.VMEM((W, 256), F32), pltpu.VMEM((W, 256), F32)],
        compiler_params=_params(),
    )(sinks, sq, kd, kd, vd, vd, do)


def _mix_out_fwd(x, og, gg, osw, gnorm, wout, g2):
    T = x.shape[0]

    def body(x_ref, og_ref, gg_ref, osw_ref, gn_ref, wout_ref, g2_ref, x1_ref, cat_ref, mix_ref):
        gn = gn_ref[...]
        for j in range(GLA_HEADS):
            hs = slice(128 * j, 128 * j + 128)
            o = og_ref[:, hs]
            r = lax.rsqrt(jnp.mean(o * o, axis=-1, keepdims=True) + EPS)
            gate = gg_ref[:, hs]
            cat_ref[:, hs] = (o * r * gn * (gate * _sigmoid(gate))).astype(BF16)
        cat_ref[:, GLA_V:] = osw_ref[...]
        mix = _mm(cat_ref[...], wout_ref[...])
        mix_ref[...] = mix
        r2 = lax.rsqrt(jnp.mean(mix * mix, axis=-1, keepdims=True) + EPS)
        x1_ref[...] = x_ref[...] + mix * r2 * g2_ref[...]

    return pl.pallas_call(
        body, name="mix_out_fwd", grid=(T // TM,),
        out_shape=[jax.ShapeDtypeStruct((T, D_MODEL), F32), jax.ShapeDtypeStruct((T, D_MODEL), BF16),
                   jax.ShapeDtypeStruct((T, D_MODEL), F32)],
        in_specs=[_row_spec(TM, D_MODEL), _row_spec(TM, GLA_V), _row_spec(TM, GLA_V), _row_spec(TM, SWA_Q),
                  _const_spec((1, LANES)), _const_spec((D_MODEL, D_MODEL)), _const_spec((1, D_MODEL))],
        out_specs=[_row_spec(TM, D_MODEL), _row_spec(TM, D_MODEL), _row_spec(TM, D_MODEL)],
        compiler_params=_params(),
    )(x, og, gg, osw, gnorm, wout, g2)


def _conv_fwd(up, prev1, prev2, w0, w1, w2, bias):
    row = _iota(up.shape, 0)
    m1 = jnp.where(row == 0, prev1, pltpu.roll(up, 1, 0))
    m2 = jnp.where(row == 0, prev2, jnp.where(row == 1, prev1, pltpu.roll(up, 2, 0)))
    return bias + w0 * m2 + w1 * m1 + w2 * up, m1, m2


def _ff_pieces():
    return [(j, off, wd) for j in range(2) for off, wd in FF_PIECES]


def _ffn_fwd(x1, g3, wup, cw, cb, wdown, g4, target):
    T = x1.shape[0]

    def body(x1_ref, g3_ref, wup_hbm, cw_ref, cb_ref, wdn_hbm, g4_ref, tg_ref,
             h2_ref, up_ref, a_ref, y_ref, dx2_ref, loss_ref, tail_ref, wup_v, wdn_v, carry_ref, sems):
        _load_once(wup_hbm, wup_v, sems.at[0])
        _load_once(wdn_hbm, wdn_v, sems.at[1])

        @pl.when(pl.program_id(0) == 0)
        def _():
            carry_ref[...] = jnp.zeros_like(carry_ref)
            loss_ref[...] = jnp.zeros_like(loss_ref)

        x1 = x1_ref[...]
        r3 = lax.rsqrt(jnp.mean(x1 * x1, axis=-1, keepdims=True) + EPS)
        h2 = (x1 * r3 * g3_ref[...]).astype(BF16)
        h2_ref[...] = h2
        y = jnp.zeros((TM, D_MODEL), F32)
        for j, off, wd in _ff_pieces():
            base = SHARD_FF * j + off
            u = []
            for half in range(2):
                cs = slice(D_FF * half + base, D_FF * half + base + wd)
                upb = _mm(h2, wup_v[2 * half + j, :, off:off + wd]).astype(BF16)
                up_ref[:, cs] = upb
                upf = upb.astype(F32)
                conv, _, _ = _conv_fwd(upf, carry_ref[7:8, cs], carry_ref[6:7, cs],
                                       cw_ref[0:1, cs], cw_ref[1:2, cs], cw_ref[2:3, cs], cb_ref[:, cs])
                carry_ref[:, cs] = upf[TM - 8:TM, :]
                tail_ref[0, :, cs] = upf[TM - 8:TM, :]
                u.append(conv)
            act, _ = _gelu_parts(u[1])
            a = (act * u[0]).astype(BF16)
            a_ref[:, base:base + wd] = a
            y = y + _mm(a, wdn_v[base:base + wd, :])
        y_ref[...] = y
        r4 = lax.rsqrt(jnp.mean(y * y, axis=-1, keepdims=True) + EPS)
        err = x1 + y * r4 * g4_ref[...] - tg_ref[...]
        dx2_ref[...] = err * (1.0 / D_MODEL)
        loss_ref[...] = loss_ref[...] + jnp.sum(err * err) * (0.5 / D_MODEL)

    outs = [
        jax.ShapeDtypeStruct((T, D_MODEL), BF16),
        jax.ShapeDtypeStruct((T, 2 * D_FF), BF16),
        jax.ShapeDtypeStruct((T, D_FF), BF16),
        jax.ShapeDtypeStruct((T, D_MODEL), F32),
        jax.ShapeDtypeStruct((T, D_MODEL), F32),
        jax.ShapeDtypeStruct((8, LANES), F32),
        jax.ShapeDtypeStruct((T // TM, 8, 2 * D_FF), F32),
    ]
    return pl.pallas_call(
        body, name="ffn_fwd", grid=(T // TM,), out_shape=outs,
        in_specs=[_row_spec(TM, D_MODEL), _const_spec((1, D_MODEL)), _any_spec(), _const_spec((3, 2 * D_FF)),
                  _const_spec((1, 2 * D_FF)), _any_spec(), _const_spec((1, D_MODEL)), _row_spec(TM, D_MODEL)],
        out_specs=[_row_spec(TM, D_MODEL), _row_spec(TM, 2 * D_FF), _row_spec(TM, D_FF), _row_spec(TM, D_MODEL),
                   _row_spec(TM, D_MODEL), _const_spec((8, LANES)),
                   pl.BlockSpec((1, 8, 2 * D_FF), lambda i: (i, 0, 0))],
        scratch_shapes=[pltpu.VMEM((N_SHARD, D_MODEL, SHARD_FF), BF16), pltpu.VMEM((D_FF, D_MODEL), BF16),
                        pltpu.VMEM((8, 2 * D_FF), F32), pltpu.SemaphoreType.DMA((2,))],
        compiler_params=_params(),
    )(x1, g3, wup, cw, cb, wdown, g4, target)


def _ffn_bwd(dx2, y, g4, up, tails, cw, cb, wdown, wup, x1, g3):
    T = dx2.shape[0]
    nt = T // TM
    rev = lambda i: (nt - 1 - i, 0)
    halo = lambda i: (jnp.maximum(nt - 2 - i, 0), 0, 0)

    def body(dn_ref, y_ref, g4_ref, up_ref, halo_ref, cw_ref, cb_ref, wdn_hbm, wup_hbm, x1_ref, g3_ref,
             dy_ref, dup_ref, dx1_ref, dg4_ref, dg3_ref, dcb_ref, dcw_ref, wup_v, wdn_v, carry_ref, sems):
        _load_once(wup_hbm, wup_v, sems.at[0])
        _load_once(wdn_hbm, wdn_v, sems.at[1])
        i = pl.program_id(0)

        @pl.when(i == 0)
        def _():
            carry_ref[...] = jnp.zeros_like(carry_ref)
            dg4_ref[...] = jnp.zeros_like(dg4_ref)
            dg3_ref[...] = jnp.zeros_like(dg3_ref)
            dcb_ref[...] = jnp.zeros_like(dcb_ref)
            dcw_ref[...] = jnp.zeros_like(dcw_ref)

        has_prev = (i < nt - 1).astype(F32)
        dn = dn_ref[...]
        y = y_ref[...]
        g4v = g4_ref[...]
        r4 = lax.rsqrt(jnp.mean(y * y, axis=-1, keepdims=True) + EPS)
        dg4_ref[...] = dg4_ref[...] + jnp.sum(dn * y * r4, axis=0, keepdims=True)
        dy = _rms_bwd(y, r4, g4v, dn).astype(BF16)
        dy_ref[...] = dy
        dh2 = jnp.zeros((TM, D_MODEL), F32)
        for j, off, wd in _ff_pieces():
            base = SHARD_FF * j + off
            row = _iota((TM, wd), 0)
            da = _mm_nt(dy, wdn_v[base:base + wd, :])
            u, m1s, m2s, ups, css = [], [], [], [], []
            for half in range(2):
                cs = slice(D_FF * half + base, D_FF * half + base + wd)
                upf = up_ref[:, cs].astype(F32)
                p1 = halo_ref[0, 7:8, cs] * has_prev
                p2 = halo_ref[0, 6:7, cs] * has_prev
                conv, m1, m2 = _conv_fwd(upf, p1, p2, cw_ref[0:1, cs], cw_ref[1:2, cs], cw_ref[2:3, cs], cb_ref[:, cs])
                u.append(conv)
                m1s.append(m1)
                m2s.append(m2)
                ups.append(upf)
                css.append(cs)
            act, dact = _gelu_parts(u[1])
            dus = [da * act, da * u[0] * dact]
            for half in range(2):
                cs, du = css[half], dus[half]
                dcb_ref[:, cs] = dcb_ref[:, cs] + jnp.sum(du, axis=0, keepdims=True)
                dcw_ref[0:1, cs] = dcw_ref[0:1, cs] + jnp.sum(du * m2s[half], axis=0, keepdims=True)
                dcw_ref[1:2, cs] = dcw_ref[1:2, cs] + jnp.sum(du * m1s[half], axis=0, keepdims=True)
                dcw_ref[2:3, cs] = dcw_ref[2:3, cs] + jnp.sum(du * ups[half], axis=0, keepdims=True)
                n1 = carry_ref[0:1, cs]
                n2 = carry_ref[1:2, cs]
                p1 = jnp.where(row == TM - 1, n1, pltpu.roll(du, TM - 1, 0))
                p2 = jnp.where(row == TM - 1, n2, jnp.where(row == TM - 2, n1, pltpu.roll(du, TM - 2, 0)))
                carry_ref[:, cs] = du[0:8, :]
                dup = (cw_ref[2:3, cs] * du + cw_ref[1:2, cs] * p1 + cw_ref[0:1, cs] * p2).astype(BF16)
                dup_ref[:, cs] = dup
                dh2 = dh2 + _mm_nt(dup, wup_v[2 * half + j, :, off:off + wd])
        x1 = x1_ref[...]
        r3 = lax.rsqrt(jnp.mean(x1 * x1, axis=-1, keepdims=True) + EPS)
        dg3_ref[...] = dg3_ref[...] + jnp.sum(dh2 * x1 * r3, axis=0, keepdims=True)
        dx1_ref[...] = dn + _rms_bwd(x1, r3, g3_ref[...], dh2)

    outs = [
        jax.ShapeDtypeStruct((T, D_MODEL), BF16),
        jax.ShapeDtypeStruct((T, 2 * D_FF), BF16),
        jax.ShapeDtypeStruct((T, D_MODEL), F32),
        jax.ShapeDtypeStruct((1, D_MODEL), F32),
        jax.ShapeDtypeStruct((1, D_MODEL), F32),
        jax.ShapeDtypeStruct((1, 2 * D_FF), F32),
        jax.ShapeDtypeStruct((3, 2 * D_FF), F32),
    ]
    return pl.pallas_call(
        body, name="ffn_bwd", grid=(nt,), out_shape=outs,
        in_specs=[pl.BlockSpec((TM, D_MODEL), rev), pl.BlockSpec((TM, D_MODEL), rev), _const_spec((1, D_MODEL)),
                  pl.BlockSpec((TM, 2 * D_FF), rev), pl.BlockSpec((1, 8, 2 * D_FF), halo), _const_spec((3, 2 * D_FF)),
                  _const_spec((1, 2 * D_FF)), _any_spec(), _any_spec(), pl.BlockSpec((TM, D_MODEL), rev),
                  _const_spec((1, D_MODEL))],
        out_specs=[pl.BlockSpec((TM, D_MODEL), rev), pl.BlockSpec((TM, 2 * D_FF), rev), pl.BlockSpec((TM, D_MODEL), rev),
                   _const_spec((1, D_MODEL)), _const_spec((1, D_MODEL)), _const_spec((1, 2 * D_FF)),
                   _const_spec((3, 2 * D_FF))],
        scratch_shapes=[pltpu.VMEM((N_SHARD, D_MODEL, SHARD_FF), BF16), pltpu.VMEM((D_FF, D_MODEL), BF16),
                        pltpu.VMEM((8, 2 * D_FF), F32), pltpu.SemaphoreType.DMA((2,))],
        compiler_params=_params(),
    )(dx2, y, g4, up, tails, cw, cb, wdown, wup, x1, g3)


def _mix_out_bwd(dx1, mix, g2, wout, og, gg, gnorm):
    T = dx1.shape[0]

    def body(dx1_ref, mix_ref, g2_ref, wout_ref, og_ref, gg_ref, gn_ref,
             dmix_ref, dog_ref, dgg_ref, dosw_ref, dg2_ref, dgn_ref):
        @pl.when(pl.program_id(0) == 0)
        def _():
            dg2_ref[...] = jnp.zeros_like(dg2_ref)
            dgn_ref[...] = jnp.zeros_like(dgn_ref)

        dx1 = dx1_ref[...]
        mix = mix_ref[...]
        r2 = lax.rsqrt(jnp.mean(mix * mix, axis=-1, keepdims=True) + EPS)
        dg2_ref[...] = dg2_ref[...] + jnp.sum(dx1 * mix * r2, axis=0, keepdims=True)
        dmix = _rms_bwd(mix, r2, g2_ref[...], dx1).astype(BF16)
        dmix_ref[...] = dmix
        dcat = _mm_nt(dmix, wout_ref[...])
        dosw_ref[...] = dcat[:, GLA_V:].astype(BF16)
        gn = gn_ref[...]
        dgn = jnp.zeros((1, LANES), F32)
        for j in range(GLA_HEADS):
            hs = slice(128 * j, 128 * j + 128)
            o = og_ref[:, hs]
            r = lax.rsqrt(jnp.mean(o * o, axis=-1, keepdims=True) + EPS)
            gate = gg_ref[:, hs]
            sg = _sigmoid(gate)
            dgated = dcat[:, hs]
            dnorm = dgated * (gate * sg)
            dgg_ref[:, hs] = dgated * (o * r * gn) * (sg * (1.0 + gate * (1.0 - sg)))
            dgn = dgn + jnp.sum(dnorm * o * r, axis=0, keepdims=True)
            dog_ref[:, hs] = _rms_bwd(o, r, gn, dnorm)
        dgn_ref[...] = dgn_ref[...] + dgn

    return pl.pallas_call(
        body, name="mix_out_bwd", grid=(T // TM,),
        out_shape=[jax.ShapeDtypeStruct((T, D_MODEL), BF16), jax.ShapeDtypeStruct((T, GLA_V), F32),
                   jax.ShapeDtypeStruct((T, GLA_V), F32), jax.ShapeDtypeStruct((T, SWA_Q), BF16),
                   jax.ShapeDtypeStruct((1, D_MODEL), F32), jax.ShapeDtypeStruct((1, LANES), F32)],
        in_specs=[_row_spec(TM, D_MODEL), _row_spec(TM, D_MODEL), _const_spec((1, D_MODEL)),
                  _const_spec((D_MODEL, D_MODEL)), _row_spec(TM, GLA_V), _row_spec(TM, GLA_V), _const_spec((1, LANES))],
        out_specs=[_row_spec(TM, D_MODEL), _row_spec(TM, GLA_V), _row_spec(TM, GLA_V), _row_spec(TM, SWA_Q),
                   _const_spec((1, D_MODEL)), _const_spec((1, LANES))],
        compiler_params=_params(),
    )(dx1, mix, g2, wout, og, gg, gnorm)


def _proj_bwd(x, g1, wp, gup, glr, dq, dk, dv, dgg, dsq, dkd, dvd, dz, rc, rsa, rsb, dx1):
    T = x.shape[0]

    def body(x_ref, g1_ref, wp_hbm, gup_ref, glr_ref, dq_ref, dk_ref, dv_ref, dgg_ref, dsq_ref, dkd_ref, dvd_ref,
             dz_ref, rc_ref, rsa_ref, rsb_ref, dx1_ref, dx_ref, dp_ref, dg1_ref, dgup_ref, dgb_ref, wp_v, sem):
        _load_once(wp_hbm, wp_v, sem)

        @pl.when(pl.program_id(0) == 0)
        def _():
            dg1_ref[...] = jnp.zeros_like(dg1_ref)
            dgup_ref[...] = jnp.zeros_like(dgup_ref)
            dgb_ref[...] = jnp.zeros_like(dgb_ref)

        rc_, rsa_, rsb_ = rc_ref[...], rsa_ref[...], rsb_ref[...]
        dp_ref[:, 0:256] = dq_ref[...].astype(BF16)
        dp_ref[:, 256:512] = dk_ref[...].astype(BF16)
        dp_ref[:, 512:1024] = dv_ref[...].astype(BF16)
        dp_ref[:, 1024:1536] = dgg_ref[...].astype(BF16)
        for s in range(4):
            cs = slice(128 * s, 128 * s + 128)
            dp_ref[:, 1536 + 128 * s:1664 + 128 * s] = _rotate_bwd(dsq_ref[:, cs], rc_, rsa_, rsb_).astype(BF16)
        first = _iota((TM, LANES), 1) < 64
        dk0 = dkd_ref[:, 0:128]
        dk1 = dkd_ref[:, 128:256]
        dkr = jnp.where(first, dk0 + pltpu.roll(dk0, 64, 1), dk1 + pltpu.roll(dk1, 64, 1))
        dp_ref[:, 2048:2176] = _rotate_bwd(dkr, rc_, rsa_, rsb_).astype(BF16)
        dv0 = dvd_ref[:, 0:128]
        dv1 = dvd_ref[:, 128:256]
        dp_ref[:, 2176:2304] = jnp.where(first, dv0 + pltpu.roll(dv0, 64, 1), dv1 + pltpu.roll(dv1, 64, 1)).astype(BF16)
        dz = dz_ref[...]
        dzb = dz.astype(BF16)
        dp_ref[:, 2304:2432] = _mm_nt(dzb, gup_ref[...]).astype(BF16)
        dgup_ref[...] = dgup_ref[...] + _mm_tn(glr_ref[...], dzb)
        dgb_ref[...] = dgb_ref[...] + jnp.sum(dz, axis=0, keepdims=True)
        dh1 = _mm_nt(dp_ref[...], wp_v[...])
        xt = x_ref[...]
        r = lax.rsqrt(jnp.mean(xt * xt, axis=-1, keepdims=True) + EPS)
        dg1_ref[...] = dg1_ref[...] + jnp.sum(dh1 * xt * r, axis=0, keepdims=True)
        dx_ref[...] = dx1_ref[...] + _rms_bwd(xt, r, g1_ref[...], dh1)

    row = lambda cols: _row_spec(TM, cols)
    return pl.pallas_call(
        body, name="proj_bwd", grid=(T // TM,),
        out_shape=[jax.ShapeDtypeStruct((T, D_MODEL), F32), jax.ShapeDtypeStruct((T, IN_WIDTH_PAD), BF16),
                   jax.ShapeDtypeStruct((1, D_MODEL), F32), jax.ShapeDtypeStruct((LANES, GLA_QK), F32),
                   jax.ShapeDtypeStruct((1, GLA_QK), F32)],
        in_specs=[row(D_MODEL), _const_spec((1, D_MODEL)), _any_spec(), _const_spec((LANES, GLA_QK)), row(LANES),
                  row(GLA_QK), row(GLA_QK), row(GLA_V), row(GLA_V), row(SWA_Q), row(256), row(256), row(GLA_QK),
                  row(LANES), row(LANES), row(LANES), row(D_MODEL)],
        out_specs=[row(D_MODEL), row(IN_WIDTH_PAD), _const_spec((1, D_MODEL)), _const_spec((LANES, GLA_QK)),
                   _const_spec((1, GLA_QK))],
        scratch_shapes=[pltpu.VMEM((D_MODEL, IN_WIDTH_PAD), BF16), pltpu.SemaphoreType.DMA],
        compiler_params=_params(),
    )(x, g1, wp, gup, glr, dq, dk, dv, dgg, dsq, dkd, dvd, dz, rc, rsa, rsb, dx1)


def _matmul_tn(a, b, tn, name, column_blocks_major=False):
    T, M = a.shape
    N = b.shape[1]
    tk = min(512, T)
    nk = T // tk
    if column_blocks_major:
        out_shape = jax.ShapeDtypeStruct((N // tn, M, tn), F32)
        out_spec = pl.BlockSpec((None, M, tn), lambda j, kk: (j, 0, 0))
    else:
        out_shape = jax.ShapeDtypeStruct((M, N), F32)
        out_spec = pl.BlockSpec((M, tn), lambda j, kk: (0, j))

    def body(a_ref, b_ref, o_ref):
        kk = pl.program_id(1)

        @pl.when(kk == 0)
        def _():
            o_ref[...] = jnp.zeros_like(o_ref)

        o_ref[...] = o_ref[...] + _mm_tn(a_ref[...], b_ref[...])

    return pl.pallas_call(
        body, name=name, grid=(N // tn, nk), out_shape=out_shape,
        in_specs=[pl.BlockSpec((tk, M), lambda j, kk: (kk, 0)), pl.BlockSpec((tk, tn), lambda j, kk: (kk, j))],
        out_specs=out_spec,
        compiler_params=_params(),
    )(a, b)


def _adamw(w, g, m, v, rows, name):
    R, C = w.shape

    def body(w_ref, g_ref, m_ref, v_ref, d_ref, m2_ref, v2_ref):
        g_ = g_ref[...]
        m2 = ADAM_B1 * m_ref[...] + (1.0 - ADAM_B1) * g_
        v2 = ADAM_B2 * v_ref[...] + (1.0 - ADAM_B2) * (g_ * g_)
        m_hat = m2 / (1.0 - ADAM_B1 ** ADAM_STEP)
        v_hat = v2 / (1.0 - ADAM_B2 ** ADAM_STEP)
        d_ref[...] = -ADAM_LR * (m_hat / (jnp.sqrt(v_hat) + ADAM_EPS) + ADAM_WD * w_ref[...])
        m2_ref[...] = m2
        v2_ref[...] = v2

    spec = pl.BlockSpec((rows, C), lambda i: (i, 0))
    return pl.pallas_call(
        body, name=name, grid=(R // rows,), out_shape=[jax.ShapeDtypeStruct((R, C), F32)] * 3,
        in_specs=[spec] * 4, out_specs=[spec] * 3, compiler_params=_params(),
    )(w, g, m, v)


def _place():
    x, y, c = lax.axis_index("x"), lax.axis_index("y"), lax.axis_index("c")
    chips = [(1 - x, y), (x, 1 - y), (1 - x, 1 - y)]
    return x, y, c, chips


def _allgather_shards(parts):
    n = len(parts)
    units = [(k, r, AG_UNIT_ROWS[k]) for k in range(n) for r in range(0, parts[k].shape[0] // 2, AG_UNIT_ROWS[k])]
    nu = len(units)

    def body(*refs):
        ins, outs = refs[:n], refs[n:2 * n]
        send_sems, recv_sems, local_sems = refs[2 * n:]
        x, y, c, chips = _place()
        sibling = (x, y, 1 - c)

        def block(i, px, py, half):
            k, r, u = units[i]
            return outs[k].at[2 * px + py, pl.ds(half * (parts[k].shape[0] // 2) + r, u), :]

        def copy(i, j, px, py, half, to, src=None):
            return pltpu.make_async_remote_copy(
                src_ref=block(i, px, py, half) if src is None else src, dst_ref=block(i, px, py, half),
                send_sem=send_sems.at[nu * j + i], recv_sem=recv_sems.at[nu * j + i], device_id=to, device_id_type=MESH)

        for k in range(n):
            for r in range(0, parts[k].shape[0], AG_LOCAL_ROWS[k]):
                pltpu.make_async_copy(ins[k].at[pl.ds(r, AG_LOCAL_ROWS[k]), :],
                                      outs[k].at[2 * x + y, pl.ds(r, AG_LOCAL_ROWS[k]), :], local_sems.at[k]).start()
        first, passed = [], []
        for i, (k, r, u) in enumerate(units):
            for j, chip in enumerate(chips):
                src = ins[k].at[pl.ds(c * (parts[k].shape[0] // 2) + r, u), :]
                first.append(copy(i, j, x, y, c, (*chip, c), src=src))
                first[-1].start()
        for i in range(nu):
            for j, chip in enumerate(chips):
                copy(i, j, *chip, c, (x, y, c)).wait_recv()
                passed.append(copy(i, 3 + j, *chip, c, sibling))
                passed[-1].start()
        for i in range(nu):
            for j, chip in enumerate(chips):
                copy(i, 3 + j, *chip, 1 - c, (x, y, c)).wait_recv()
        for cp in first + passed:
            cp.wait_send()
        for k in range(n):
            pltpu.make_async_copy(ins[k], outs[k].at[2 * x + y], local_sems.at[k]).wait()

    return pl.pallas_call(
        body, name="allgather_shards", out_shape=[jax.ShapeDtypeStruct((N_SHARD,) + p.shape, p.dtype) for p in parts],
        in_specs=[_any_spec()] * n, out_specs=[_any_spec()] * n,
        scratch_shapes=[pltpu.SemaphoreType.DMA((6 * nu,)), pltpu.SemaphoreType.DMA((6 * nu,)), pltpu.SemaphoreType.DMA((n,))],
    )(*parts)


def _d2d_pieces(rows, piece_rows):
    return [(r, piece_rows) for r in range(0, rows, piece_rows)]


def _rs_pair_swap(arrs, piece_rows):
    n = len(arrs)

    def body(*refs):
        ins, outs = refs[:n], refs[n:2 * n]
        send_sems, recv_sems = refs[2 * n:]
        x, y, c, _ = _place()
        sibling = (x, y, 1 - c)
        for k in range(n):
            H = arrs[k].shape[1] // 2
            for s in range(N_SHARD):
                for r, pr in _d2d_pieces(H, piece_rows[k]):
                    pltpu.make_async_remote_copy(
                        src_ref=ins[k].at[s, pl.ds((1 - c) * H + r, pr), :], dst_ref=outs[k].at[s, pl.ds(r, pr), :],
                        send_sem=send_sems.at[k], recv_sem=recv_sems.at[k], device_id=sibling, device_id_type=MESH).start()
        for k in range(n):
            H = arrs[k].shape[1] // 2
            whole = pltpu.make_async_remote_copy(
                src_ref=ins[k].at[:, pl.ds(0, H), :], dst_ref=outs[k], send_sem=send_sems.at[k], recv_sem=recv_sems.at[k],
                device_id=sibling, device_id_type=MESH)
            whole.wait_recv()
            whole.wait_send()

    return pl.pallas_call(
        body, name="rs_pair_swap",
        out_shape=[jax.ShapeDtypeStruct((N_SHARD, a.shape[1] // 2, a.shape[2]), F32) for a in arrs],
        in_specs=[_any_spec()] * n, out_specs=[_any_spec()] * n,
        scratch_shapes=[pltpu.SemaphoreType.DMA((n,)), pltpu.SemaphoreType.DMA((n,))],
    )(*arrs)


def _rs_add_pair(a, got, core, rows, name):
    _, H, C = got.shape
    nb = H // rows

    def body(c_ref, a_ref, b_ref, o_ref):
        o_ref[...] = (a_ref[...] + b_ref[...]).astype(BF16)

    spec = pl.BlockSpec((1, rows, C), lambda s, r, c_ref: (s, r, 0))
    return pl.pallas_call(
        body, name=name, out_shape=jax.ShapeDtypeStruct(got.shape, BF16),
        grid_spec=pltpu.PrefetchScalarGridSpec(
            num_scalar_prefetch=1, grid=(N_SHARD, nb),
            in_specs=[pl.BlockSpec((1, rows, C), lambda s, r, c_ref: (s, c_ref[0] * nb + r, 0)), spec], out_specs=spec),
        compiler_params=_params(),
    )(core, a, got)


def _rs_chip_scatter(parts):
    n = len(parts)

    def body(*refs):
        ins, outs = refs[:n], refs[n:2 * n]
        send_sems, recv_sems, local_sems = refs[2 * n:]
        x, y, c, chips = _place()
        me = 2 * x + y
        sends = []
        for k in range(n):
            for r, pr in _d2d_pieces(parts[k].shape[1], SCATTER_LOCAL_ROWS[k]):
                pltpu.make_async_copy(ins[k].at[me, pl.ds(r, pr), :], outs[k].at[me, pl.ds(r, pr), :], local_sems.at[k]).start()
            for j, (px, py) in enumerate(chips):
                sends.append(pltpu.make_async_remote_copy(
                    src_ref=ins[k].at[2 * px + py], dst_ref=outs[k].at[me], send_sem=send_sems.at[3 * k + j],
                    recv_sem=recv_sems.at[3 * k + j], device_id=(px, py, c), device_id_type=MESH))
                sends[-1].start()
        for k in range(n):
            for j, (px, py) in enumerate(chips):
                pltpu.make_async_remote_copy(
                    src_ref=ins[k].at[me], dst_ref=outs[k].at[2 * px + py], send_sem=send_sems.at[3 * k + j],
                    recv_sem=recv_sems.at[3 * k + j], device_id=(px, py, c), device_id_type=MESH).wait_recv()
        for cp in sends:
            cp.wait_send()
        for k in range(n):
            pltpu.make_async_copy(ins[k].at[me], outs[k].at[me], local_sems.at[k]).wait()

    return pl.pallas_call(
        body, name="rs_chip_scatter", out_shape=[jax.ShapeDtypeStruct(p.shape, p.dtype) for p in parts],
        in_specs=[_any_spec()] * n, out_specs=[_any_spec()] * n,
        scratch_shapes=[pltpu.SemaphoreType.DMA((3 * n,)), pltpu.SemaphoreType.DMA((3 * n,)), pltpu.SemaphoreType.DMA((n,))],
    )(*parts)


def _rs_sum_chips(parts, rows, name):
    _, H, C = parts.shape

    def body(p_ref, o_ref):
        o_ref[...] = ((p_ref[0].astype(F32) + p_ref[1].astype(F32)) + p_ref[2].astype(F32)) + p_ref[3].astype(F32)

    return pl.pallas_call(
        body, name=name, grid=(H // rows,), out_shape=jax.ShapeDtypeStruct((H, C), F32),
        in_specs=[pl.BlockSpec((N_SHARD, rows, C), lambda r: (0, r, 0))],
        out_specs=pl.BlockSpec((rows, C), lambda r: (r, 0)), compiler_params=_params(),
    )(parts)


def _rs_pair_share(halves, piece_rows):
    n = len(halves)

    def body(*refs):
        ins, outs = refs[:n], refs[n:2 * n]
        send_sems, recv_sems, local_sems = refs[2 * n:]
        x, y, c, _ = _place()
        sibling = (x, y, 1 - c)
        for k in range(n):
            H = halves[k].shape[0]
            for r, pr in _d2d_pieces(H, piece_rows[k]):
                pltpu.make_async_copy(ins[k].at[pl.ds(r, pr), :], outs[k].at[pl.ds(c * H + r, pr), :], local_sems.at[k]).start()
                pltpu.make_async_remote_copy(
                    src_ref=ins[k].at[pl.ds(r, pr), :], dst_ref=outs[k].at[pl.ds(c * H + r, pr), :],
                    send_sem=send_sems.at[k], recv_sem=recv_sems.at[k], device_id=sibling, device_id_type=MESH).start()
        for k in range(n):
            H = halves[k].shape[0]
            whole = pltpu.make_async_remote_copy(
                src_ref=ins[k], dst_ref=outs[k].at[pl.ds((1 - c) * H, H), :], send_sem=send_sems.at[k],
                recv_sem=recv_sems.at[k], device_id=sibling, device_id_type=MESH)
            whole.wait_recv()
            whole.wait_send()
            pltpu.make_async_copy(ins[k], outs[k].at[pl.ds(c * H, H), :], local_sems.at[k]).wait()

    return pl.pallas_call(
        body, name="rs_pair_share", out_shape=[jax.ShapeDtypeStruct((2 * h.shape[0], h.shape[1]), F32) for h in halves],
        in_specs=[_any_spec()] * n, out_specs=[_any_spec()] * n,
        scratch_shapes=[pltpu.SemaphoreType.DMA((n,)), pltpu.SemaphoreType.DMA((n,)), pltpu.SemaphoreType.DMA((n,))],
    )(*halves)


def _allreduce_small(vec):
    def body(v_ref, o_ref, all_ref, send_sems, recv_sems):
        x, y, c, _ = _place()
        me = 4 * x + 2 * y + c
        all_ref[me] = v_ref[...]
        sends = []
        for k in range(1, 8):
            kx, ky, kc = (k >> 2) & 1, (k >> 1) & 1, k & 1
            peer = (x ^ kx, y ^ ky, c ^ kc)
            cp = pltpu.make_async_remote_copy(
                src_ref=v_ref, dst_ref=all_ref.at[me], send_sem=send_sems.at[k - 1], recv_sem=recv_sems.at[k - 1],
                device_id=peer, device_id_type=MESH)
            cp.start()
            sends.append(cp)
        for k in range(1, 8):
            kx, ky, kc = (k >> 2) & 1, (k >> 1) & 1, k & 1
            src = 4 * (x ^ kx) + 2 * (y ^ ky) + (c ^ kc)
            pltpu.make_async_remote_copy(
                src_ref=v_ref, dst_ref=all_ref.at[src], send_sem=send_sems.at[k - 1], recv_sem=recv_sems.at[k - 1],
                device_id=(x, y, c), device_id_type=MESH).wait_recv()
        for cp in sends:
            cp.wait_send()
        total = all_ref[0]
        for d in range(1, 8):
            total = total + all_ref[d]
        o_ref[...] = total

    vm = pl.BlockSpec(memory_space=pltpu.VMEM)
    return pl.pallas_call(
        body, name="allreduce_small", out_shape=jax.ShapeDtypeStruct(vec.shape, F32), in_specs=[vm], out_specs=vm,
        scratch_shapes=[pltpu.VMEM((8,) + vec.shape, F32), pltpu.SemaphoreType.DMA((7,)), pltpu.SemaphoreType.DMA((7,))],
    )(vec)


BIG_NAMES = ("w_in", "w_out", "w_up", "w_down")
MATRIX_NAMES = BIG_NAMES + ("gla_gate_up", "conv_w")
GATE_SHARD = (16, GLA_QK // N_SHARD)
CONVW_SHARD = (3, SHARD_FF)
SMALL_W_ROWS = 96
SMALL_G_ROWS = 64
PIECE_ROWS = (128, 128, 64, 88, SMALL_G_ROWS // 2)
ADD_ROWS = (256, 128, 256, 176, SMALL_G_ROWS // 2)
SCATTER_LOCAL_ROWS = (128, 128, 64, 176, SMALL_G_ROWS // 2)
AG_UNIT_ROWS = (256, 128, 128, 176, SMALL_W_ROWS // 2)
AG_LOCAL_ROWS = (128, 128, 64, 176, SMALL_W_ROWS)


def _pad_rows(flat, rows):
    return jnp.pad(flat, (0, rows * LANES - flat.shape[0])).reshape(rows, LANES)


def _pack_small_weights(gate_up, conv_w):
    bits = lax.bitcast_convert_type(conv_w, BF16)
    return _pad_rows(jnp.concatenate([gate_up.astype(BF16).reshape(-1), bits.reshape(-1)]), SMALL_W_ROWS)


def _unpack_small_weights(packed):
    flat = packed.reshape(N_SHARD, -1)
    n_gate = GATE_SHARD[0] * GATE_SHARD[1]
    n_conv = 2 * CONVW_SHARD[0] * CONVW_SHARD[1]
    gate = flat[:, :n_gate].reshape((N_SHARD,) + GATE_SHARD)
    conv = lax.bitcast_convert_type(flat[:, n_gate:n_gate + n_conv].reshape((N_SHARD,) + CONVW_SHARD + (2,)), F32)
    return (jnp.transpose(gate, (1, 0, 2)).reshape(16, GLA_QK), jnp.transpose(conv, (1, 0, 2)).reshape(3, 2 * D_FF))


def _pack_small_grads(dgate, dconv):
    rows = []
    for s in range(N_SHARD):
        g = dgate[:, GATE_SHARD[1] * s:GATE_SHARD[1] * (s + 1)].reshape(-1)
        cw = dconv[:, SHARD_FF * s:SHARD_FF * (s + 1)].reshape(-1)
        rows.append(_pad_rows(jnp.concatenate([g, cw]), SMALL_G_ROWS))
    return jnp.stack(rows)


def _unpack_small_grads(packed):
    flat = packed.reshape(-1)
    n_gate = GATE_SHARD[0] * GATE_SHARD[1]
    n_conv = CONVW_SHARD[0] * CONVW_SHARD[1]
    return flat[:n_gate].reshape(GATE_SHARD), flat[n_gate:n_gate + n_conv].reshape(CONVW_SHARD)


def _permute_w_in(w):
    pad = jnp.zeros((w.shape[0], IN_WIDTH_PAD - IN_WIDTH), w.dtype)
    return jnp.concatenate([w[:, 0:1024], w[:, 1040:2320], w[:, 1024:1040], pad], axis=1)


def _unpermute_w_in(wp):
    return jnp.concatenate([wp[:, 0:1024], wp[:, 2304:2320], wp[:, 1024:2304]], axis=1)


def _rope_tables(positions):
    half = ROPE_DIM // 2
    inv_freq = ROPE_THETA ** (-jnp.arange(half, dtype=F32) * (2.0 / ROPE_DIM))
    d = jnp.arange(LANES) % SWA_HD
    freq = jnp.where(d < ROPE_DIM, inv_freq[d % half], 0.0)
    ang = positions.astype(F32)[:, None] * freq
    cos, sin = jnp.cos(ang), jnp.sin(ang)
    return cos, jnp.where(d < half, -sin, 0.0), jnp.where((d >= half) & (d < ROPE_DIM), sin, 0.0)


SMALL_NAMES = (("pre_mix_norm", 1024), ("gla_gate_bias", 256), ("gla_out_norm", 128), ("swa_sinks", 8),
               ("post_mix_norm", 1024), ("pre_ffn_norm", 1024), ("conv_b", 5632), ("post_ffn_norm", 1024))


def _pack_small(vals, loss):
    parts = [vals[n].reshape(-1) for n, _ in SMALL_NAMES] + [loss.reshape(1)]
    flat = jnp.concatenate(parts)
    return jnp.pad(flat, (0, SMALL_ROWS * LANES - flat.shape[0])).reshape(SMALL_ROWS, LANES)


def _unpack_small(packed):
    flat = packed.reshape(-1)
    out, off = {}, 0
    for n, size in SMALL_NAMES:
        out[n] = flat[off:off + size].reshape(1, size)
        off += size
    return out, flat[off]


def _local_step(x, positions, target, w, small):
    rc, rsa, rsb = _rope_tables(positions)
    wp = w["wp"]
    gup = jnp.pad(w["gla_gate_up"], ((0, LANES - 16), (0, 0)))
    g1, g2, g3, g4 = (small[n] for n in ("pre_mix_norm", "post_mix_norm", "pre_ffn_norm", "post_ffn_norm"))
    gbias, gnorm, cb = small["gla_gate_bias"], small["gla_out_norm"], small["conv_b"]
    sinks = small["swa_sinks"].reshape(-1)
    cw = w["conv_w"]

    h1, q, k, v, la, gg, sq, kd, vd, glr = _proj_fwd(x, g1, wp, gup, gbias, rc, rsa, rsb)
    og, s_all = _gla_fwd(q, k, v, la)
    osw = _swa_fwd(sq, kd, vd, sinks)
    x1, cat, mix = _mix_out_fwd(x, og, gg, osw, gnorm, w["w_out"], g2)
    h2, up, act, y, dx2, loss, tails = _ffn_fwd(x1, g3, w["w_up4"], cw, cb, w["w_down"], g4, target)

    dy, dup, dx1, dg4, dg3, dcb, dcw = _ffn_bwd(dx2, y, g4, up, tails, cw, cb, w["w_down"], w["w_up4"], x1, g3)
    dmix, dog, dgg, dosw, dg2, dgn = _mix_out_bwd(dx1, mix, g2, w["w_out"], og, gg, gnorm)
    dsq, dkd, dvd, dsink = _swa_bwd(sq, kd, vd, sinks, dosw)
    dq, dk, dv, dz = _gla_bwd(q, k, v, la, s_all, dog)
    dx, dproj, dg1, dgup, dgb = _proj_bwd(x, g1, wp, gup, glr, dq, dk, dv, dgg, dsq, dkd, dvd, dz, rc, rsa, rsb, dx1)

    grads = {
        "wp": _matmul_tn(h1, dproj, IN_WIDTH_PAD, "grad_w_in"),
        "w_out": _matmul_tn(cat, dmix, D_MODEL, "grad_w_out"),
        "w_up4": _matmul_tn(h2, dup, SHARD_FF, "grad_w_up", column_blocks_major=True),
        "w_down": _matmul_tn(act, dy, D_MODEL, "grad_w_down"),
        "gla_gate_up": dgup[0:16],
        "conv_w": dcw,
    }
    small_grads = {
        "pre_mix_norm": dg1, "gla_gate_bias": dgb, "gla_out_norm": dgn, "swa_sinks": dsink[:, 0].reshape(1, 8),
        "post_mix_norm": dg2, "pre_ffn_norm": dg3, "conv_b": dcb, "post_ffn_norm": dg4,
    }
    return loss[0, 0], dx, grads, small_grads


ADAM_ROWS = {"w_in": 256, "w_out": 256, "w_up": 256, "w_down": 176}
WEIGHT_ORDER = ("pre_mix_norm", "w_in", "gla_gate_up", "gla_gate_bias", "gla_out_norm", "swa_sinks", "w_out",
                "post_mix_norm", "pre_ffn_norm", "w_up", "conv_w", "conv_b", "w_down", "post_ffn_norm")
TINY_ROWS = 152


def _pack_tiny(vals):
    flat = jnp.concatenate([vals[n].reshape(-1) for n in TINY_NAMES])
    return jnp.pad(flat, (0, TINY_ROWS * LANES - flat.shape[0])).reshape(TINY_ROWS, LANES)


TINY_NAMES = tuple(n for n, _ in SMALL_NAMES) + ("gla_gate_up", "conv_w")
TINY_SHAPES = {**{n: (1, s) for n, s in SMALL_NAMES}, "gla_gate_up": (16, 64), "conv_w": (3, 1408)}


def _unpack_tiny(packed):
    flat = packed.reshape(-1)
    out, off = {}, 0
    for n in TINY_NAMES:
        shape = TINY_SHAPES[n]
        size = shape[0] * shape[1]
        out[n] = flat[off:off + size].reshape(shape)
        off += size
    return out


def kernel(x, positions, pre_mix_norm, w_in, gla_gate_up, gla_gate_bias, gla_out_norm, swa_sinks, w_out, post_mix_norm, pre_ffn_norm, w_up, conv_w, conv_b, w_down, post_ffn_norm, loss_target, m_pre_mix_norm, m_w_in, m_gla_gate_up, m_gla_gate_bias, m_gla_out_norm, m_swa_sinks, m_w_out, m_post_mix_norm, m_pre_ffn_norm, m_w_up, m_conv_w, m_conv_b, m_w_down, m_post_ffn_norm, v_pre_mix_norm, v_w_in, v_gla_gate_up, v_gla_gate_bias, v_gla_out_norm, v_swa_sinks, v_w_out, v_post_mix_norm, v_pre_ffn_norm, v_w_up, v_conv_w, v_conv_b, v_w_down, v_post_ffn_norm):
    weights = dict(pre_mix_norm=pre_mix_norm, w_in=w_in, gla_gate_up=gla_gate_up, gla_gate_bias=gla_gate_bias,
                   gla_out_norm=gla_out_norm, swa_sinks=swa_sinks, w_out=w_out, post_mix_norm=post_mix_norm,
                   pre_ffn_norm=pre_ffn_norm, w_up=w_up, conv_w=conv_w, conv_b=conv_b, w_down=w_down,
                   post_ffn_norm=post_ffn_norm)
    mom = dict(pre_mix_norm=m_pre_mix_norm, w_in=m_w_in, gla_gate_up=m_gla_gate_up, gla_gate_bias=m_gla_gate_bias,
               gla_out_norm=m_gla_out_norm, swa_sinks=m_swa_sinks, w_out=m_w_out, post_mix_norm=m_post_mix_norm,
               pre_ffn_norm=m_pre_ffn_norm, w_up=m_w_up, conv_w=m_conv_w, conv_b=m_conv_b, w_down=m_w_down,
               post_ffn_norm=m_post_ffn_norm)
    var = dict(pre_mix_norm=v_pre_mix_norm, w_in=v_w_in, gla_gate_up=v_gla_gate_up, gla_gate_bias=v_gla_gate_bias,
               gla_out_norm=v_gla_out_norm, swa_sinks=v_swa_sinks, w_out=v_w_out, post_mix_norm=v_post_mix_norm,
               pre_ffn_norm=v_pre_ffn_norm, w_up=v_w_up, conv_w=v_conv_w, conv_b=v_conv_b, w_down=v_w_down,
               post_ffn_norm=v_post_ffn_norm)
    weights, mom, var = ({n: a[0] if a.ndim == 3 else a for n, a in d.items()} for d in (weights, mom, var))

    win4, wout4, wup4, wdown4, small4 = _allgather_shards(
        [weights[n].astype(BF16) for n in BIG_NAMES] + [_pack_small_weights(weights["gla_gate_up"], weights["conv_w"])])
    gate_full, convw_full = _unpack_small_weights(small4)
    full = {
        "wp": _permute_w_in(jnp.transpose(win4, (1, 0, 2)).reshape(D_MODEL, IN_WIDTH)),
        "w_out": wout4.reshape(D_MODEL, D_MODEL), "w_up4": wup4, "w_down": wdown4.reshape(D_FF, D_MODEL),
        "gla_gate_up": gate_full, "conv_w": convw_full,
    }
    small = {n: weights[n] for n, _ in SMALL_NAMES}
    loss, dx, grads, small_grads = _local_step(x[0], positions[0], loss_target[0], full, small)

    per_shard = [
        jnp.transpose(_unpermute_w_in(grads["wp"]).reshape(D_MODEL, N_SHARD, IN_WIDTH // N_SHARD), (1, 0, 2)),
        grads["w_out"].reshape(N_SHARD, D_MODEL // N_SHARD, D_MODEL), grads["w_up4"],
        grads["w_down"].reshape(N_SHARD, D_FF // N_SHARD, D_MODEL),
        _pack_small_grads(grads["gla_gate_up"], grads["conv_w"]),
    ]
    labels = BIG_NAMES + ("small",)
    core = lax.axis_index("c").astype(jnp.int32).reshape(1)
    got = _rs_pair_swap(per_shard, PIECE_ROWS)
    partial = [_rs_add_pair(a, g, core, rows, "rs_add_pair_" + n) for a, g, rows, n in zip(per_shard, got, ADD_ROWS, labels)]
    landed = _rs_chip_scatter(partial)
    halves = [_rs_sum_chips(p, rows, "rs_sum_chips_" + n) for p, rows, n in zip(landed, ADD_ROWS, labels)]
    reduced = _rs_pair_share(halves, PIECE_ROWS)
    g_small, loss_sum = _unpack_small(_allreduce_small(_pack_small(small_grads, loss)))
    g_gate, g_convw = _unpack_small_grads(reduced[4])
    g_all = {**g_small, **dict(zip(BIG_NAMES, reduced[:4])), "gla_gate_up": g_gate, "conv_w": g_convw}

    delta, new_m, new_v = {}, {}, {}
    for n in BIG_NAMES:
        delta[n], new_m[n], new_v[n] = _adamw(weights[n], g_all[n], mom[n], var[n], ADAM_ROWS[n], "adamw_" + n)
    tiny = _adamw(*(_pack_tiny({n: d[n] for n in TINY_NAMES}) for d in (weights, g_all, mom, var)), TINY_ROWS, "adamw_small")
    for res, packed in zip((delta, new_m, new_v), tiny):
        res.update(_unpack_tiny(packed))

    def lead(n, a):
        return a[None] if n in MATRIX_NAMES else a

    outs = [loss_sum, dx[None]]
    for d in (g_all, delta, new_m, new_v):
        outs.extend(lead(n, d[n]) for n in WEIGHT_ORDER)
    return tuple(outs)
```

```python
import functools

import jax
import jax.numpy as jnp
from jax import lax
from jax.experimental import pallas as pl
from jax.experimental.pallas import tpu as pltpu

F32 = jnp.float32
BF16 = jnp.bfloat16
MESH = pl.DeviceIdType.MESH

D_MODEL = 1024
GLA_HEADS = 4
GLA_DK = 64
GLA_DV = 128
GLA_TAU = 16.0
GLA_CHUNK = 64
SWA_HEADS = 8
SWA_HD = 64
SWA_BLOCK = 128
ROPE_THETA = 500000.0
ROPE_DIM = 16
D_FF = 2816
EPS = 1e-6
GLA_QK = 256
GLA_V = 512
SWA_Q = 512
SWA_KV = 128
IN_WIDTH = 2320
IN_WIDTH_PAD = 2432
N_SHARD = 4

ADAM_LR = 0.001
ADAM_B1 = 0.9
ADAM_B2 = 0.999
ADAM_EPS = 1e-08
ADAM_WD = 0.01
ADAM_STEP = 10

LANES = 128
VMEM_LIMIT = 56 * 1024 * 1024
TM = 256
SHARD_FF = 2 * D_FF // N_SHARD
FF_PIECES = ((0, 512), (512, 512), (1024, 384))
GLA_BLOCK = 256

SMALL_ROWS = 80


def _params(**kw):
    return pltpu.CompilerParams(vmem_limit_bytes=VMEM_LIMIT, **kw)


def _mm(a, b):
    return lax.dot_general(a.astype(BF16), b.astype(BF16), (((1,), (0,)), ((), ())), preferred_element_type=F32)


def _mm_nt(a, b):
    return lax.dot_general(a.astype(BF16), b.astype(BF16), (((1,), (1,)), ((), ())), preferred_element_type=F32)


def _mm_tn(a, b):
    return lax.dot_general(a.astype(BF16), b.astype(BF16), (((0,), (0,)), ((), ())), preferred_element_type=F32)


def _mm_f32(a, b):
    return lax.dot_general(a, b, (((1,), (0,)), ((), ())), preferred_element_type=F32, precision=lax.Precision.HIGHEST)


def _iota(shape, dim):
    return lax.broadcasted_iota(jnp.int32, shape, dim)


def _sigmoid(x):
    return 1.0 / (1.0 + jnp.exp(-x))


def _gelu_parts(x):
    c = 0.7978845608028654
    x2 = x * x
    t = jnp.tanh(c * (x + 0.044715 * (x2 * x)))
    cdf = 0.5 * (1.0 + t)
    dcdf = 0.5 * (1.0 - t * t) * c * (1.0 + 3.0 * 0.044715 * x2)
    return x * cdf, cdf + x * dcdf


def _rms_bwd(v, r, g, dout):
    gd = g * dout
    return r * gd - v * (r * r * r) * jnp.mean(v * gd, axis=-1, keepdims=True)


def _row_spec(tm, cols):
    return pl.BlockSpec((tm, cols), lambda i: (i, 0))


def _const_spec(shape):
    return pl.BlockSpec(shape, lambda i: (0,) * len(shape))


def _any_spec():
    return pl.BlockSpec(memory_space=pl.ANY)


def _load_once(src_hbm, dst_vmem, sem):
    @pl.when(pl.program_id(0) == 0)
    def _():
        cp = pltpu.make_async_copy(src_hbm, dst_vmem, sem)
        cp.start()
        cp.wait()


def _rotate(v, rc, rsa, rsb):
    return v * rc + pltpu.roll(v, 120, 1) * rsa + pltpu.roll(v, 8, 1) * rsb


def _rotate_bwd(dv, rc, rsa, rsb):
    return dv * rc + pltpu.roll(dv * rsa, 8, 1) + pltpu.roll(dv * rsb, 120, 1)


def _proj_fwd(x, g1, wp, gup, gbias, rc, rsa, rsb):
    T = x.shape[0]

    def body(x_ref, g1_ref, wp_hbm, gup_ref, gb_ref, rc_ref, rsa_ref, rsb_ref,
             h1_ref, q_ref, k_ref, v_ref, la_ref, gg_ref, sq_ref, kd_ref, vd_ref, glr_ref, wp_v, sem):
        _load_once(wp_hbm, wp_v, sem)
        xt = x_ref[...]
        r = lax.rsqrt(jnp.mean(xt * xt, axis=-1, keepdims=True) + EPS)
        h = (xt * r * g1_ref[...]).astype(BF16)
        h1_ref[...] = h
        q_ref[...] = _mm(h, wp_v[:, 0:256])
        k_ref[...] = _mm(h, wp_v[:, 256:512])
        v_ref[...] = _mm(h, wp_v[:, 512:1024])
        gg_ref[...] = _mm(h, wp_v[:, 1024:1536])
        glr = _mm(h, wp_v[:, 2304:2432]).astype(BF16)
        glr_ref[...] = glr
        z = _mm(glr, gup_ref[...]) + gb_ref[...]
        la_ref[...] = (jnp.minimum(z, 0.0) - jnp.log1p(jnp.exp(-jnp.abs(z)))) * (1.0 / GLA_TAU)
        rc_, rsa_, rsb_ = rc_ref[...], rsa_ref[...], rsb_ref[...]
        for s in range(4):
            qs = _mm(h, wp_v[:, 1536 + 128 * s:1664 + 128 * s])
            sq_ref[:, 128 * s:128 * s + 128] = (_rotate(qs, rc_, rsa_, rsb_) * 0.125).astype(BF16)
        lane = _iota((TM, LANES), 1)
        first = lane < 64
        kr = _rotate(_mm(h, wp_v[:, 2048:2176]), rc_, rsa_, rsb_)
        krr = pltpu.roll(kr, 64, 1)
        kd_ref[:, 0:128] = jnp.where(first, kr, krr).astype(BF16)
        kd_ref[:, 128:256] = jnp.where(first, krr, kr).astype(BF16)
        vr = _mm(h, wp_v[:, 2176:2304])
        vrr = pltpu.roll(vr, 64, 1)
        vd_ref[:, 0:128] = jnp.where(first, vr, vrr).astype(BF16)
        vd_ref[:, 128:256] = jnp.where(first, vrr, vr).astype(BF16)

    outs = [
        jax.ShapeDtypeStruct((T, D_MODEL), BF16),
        jax.ShapeDtypeStruct((T, GLA_QK), F32),
        jax.ShapeDtypeStruct((T, GLA_QK), F32),
        jax.ShapeDtypeStruct((T, GLA_V), F32),
        jax.ShapeDtypeStruct((T, GLA_QK), F32),
        jax.ShapeDtypeStruct((T, GLA_V), F32),
        jax.ShapeDtypeStruct((T, SWA_Q), BF16),
        jax.ShapeDtypeStruct((T, 256), BF16),
        jax.ShapeDtypeStruct((T, 256), BF16),
        jax.ShapeDtypeStruct((T, LANES), BF16),
    ]
    return pl.pallas_call(
        body, name="proj_fwd", grid=(T // TM,), out_shape=outs,
        in_specs=[_row_spec(TM, D_MODEL), _const_spec((1, D_MODEL)), _any_spec(), _const_spec((LANES, GLA_QK)),
                  _const_spec((1, GLA_QK)), _row_spec(TM, LANES), _row_spec(TM, LANES), _row_spec(TM, LANES)],
        out_specs=[_row_spec(TM, o.shape[1]) for o in outs],
        scratch_shapes=[pltpu.VMEM((D_MODEL, IN_WIDTH_PAD), BF16), pltpu.SemaphoreType.DMA],
        compiler_params=_params(),
    )(x, g1, wp, gup, gbias, rc, rsa, rsb)


GLA_NB = GLA_BLOCK // GLA_CHUNK


def _gla_masks():
    n = GLA_BLOCK
    lane = _iota((n, LANES), 1)
    lane_masks = [(lane < 64).astype(F32), (lane >= 64).astype(F32)]
    row, col = _iota((n, n), 0), _iota((n, n), 1)
    same_chunk = (row >> 6) == (col >> 6)
    blk = ((_iota((256, LANES), 0) >> 7) == (_iota((256, LANES), 1) >> 6)).astype(F32)
    return lane_masks, same_chunk & (col <= row), same_chunk & (col >= row), blk


def _chunk_rows(vals):
    return jnp.concatenate([jnp.broadcast_to(v, (GLA_CHUNK, LANES)) for v in vals], axis=0)


def _gla_block_terms(q_ref, k_ref, b_ref, p):
    C = GLA_CHUNK
    cols = slice(LANES * p, LANES * p + LANES)
    bc = b_ref[:, cols]
    bl_rows = [b_ref[C * c + C - 1:C * c + C, cols] for c in range(GLA_NB)]
    bl = _chunk_rows(bl_rows)
    bm = _chunk_rows([b_ref[C * c + C // 2 - 1:C * c + C // 2, cols] for c in range(GLA_NB)])
    qs = q_ref[:, cols] * 0.125
    kk = k_ref[:, cols]
    eb = jnp.exp(bc)
    ekl = jnp.exp(bl - bc)
    eqm = jnp.exp(bc - bm)
    ekm = jnp.exp(bm - bc)
    return qs, kk, eb, ekl, eqm, ekm, [jnp.exp(r) for r in bl_rows]


def _block_cumsum(la, mask):
    return _mm_f32(mask.astype(F32), la)


def _gla_fwd(q, k, v, la):
    T = q.shape[0]
    NB = GLA_BLOCK // GLA_CHUNK
    C = GLA_CHUNK

    def body(q_ref, k_ref, v_ref, la_ref, o_ref, s_ref, st_ref, b_ref):
        @pl.when(pl.program_id(0) == 0)
        def _():
            st_ref[...] = jnp.zeros_like(st_ref)

        lane_masks, causal, _, blk = _gla_masks()
        b_ref[...] = _block_cumsum(la_ref[...], causal)
        for p in range(2):
            qs, kk, eb, ekl, eqm, ekm, gam = _gla_block_terms(q_ref, k_ref, b_ref, p)
            qh, kh, qm, km = qs * eb, kk * ekl, qs * eqm, kk * ekm
            vp = v_ref[:, 256 * p:256 * p + 256]
            intra = []
            for j in range(2):
                a = jnp.where(causal, _mm_nt(qm * lane_masks[j], km), 0.0)
                intra.append(_mm(a, vp[:, 128 * j:128 * j + 128]))
            kv = [blk * _mm_tn(vp[C * c:C * c + C], kh[C * c:C * c + C]) for c in range(NB)]
            st = st_ref[p]
            inter = []
            for c in range(NB):
                s_ref[c, p] = st
                inter.append(_mm_nt(qh[C * c:C * c + C], st))
                st = st * gam[c] + kv[c]
            st_ref[p] = st
            o_ref[:, 256 * p:256 * p + 256] = jnp.concatenate(inter, axis=0) + jnp.concatenate(intra, axis=1)

    return pl.pallas_call(
        body, name="gla_fwd", grid=(T // GLA_BLOCK,),
        out_shape=[jax.ShapeDtypeStruct((T, GLA_V), F32), jax.ShapeDtypeStruct((T // C, 2, 256, LANES), F32)],
        in_specs=[_row_spec(GLA_BLOCK, GLA_QK), _row_spec(GLA_BLOCK, GLA_QK), _row_spec(GLA_BLOCK, GLA_V),
                  _row_spec(GLA_BLOCK, GLA_QK)],
        out_specs=[_row_spec(GLA_BLOCK, GLA_V), pl.BlockSpec((NB, 2, 256, LANES), lambda i: (i, 0, 0, 0))],
        scratch_shapes=[pltpu.VMEM((2, 256, LANES), F32), pltpu.VMEM((GLA_BLOCK, GLA_QK), F32)],
        compiler_params=_params(),
    )(q, k, v, la)


def _gla_bwd(q, k, v, la, s_all, do):
    T = q.shape[0]
    NB = GLA_BLOCK // GLA_CHUNK
    C = GLA_CHUNK
    nblk = T // GLA_BLOCK

    def body(q_ref, k_ref, v_ref, la_ref, s_ref, do_ref, dq_ref, dk_ref, dv_ref, dz_ref, dst_ref, b_ref):
        @pl.when(pl.program_id(0) == 0)
        def _():
            dst_ref[...] = jnp.zeros_like(dst_ref)

        lane_masks, causal, anti_causal, blk = _gla_masks()
        b_ref[...] = _block_cumsum(la_ref[...], causal)
        for p in range(2):
            cols = slice(LANES * p, LANES * p + LANES)
            qs, kk, eb, ekl, eqm, ekm, gam = _gla_block_terms(q_ref, k_ref, b_ref, p)
            qh, kh, qm, km = qs * eb, kk * ekl, qs * eqm, kk * ekm
            vp = v_ref[:, 256 * p:256 * p + 256]
            dop = do_ref[:, 256 * p:256 * p + 256]
            dqm = jnp.zeros((GLA_BLOCK, LANES), F32)
            dkm = jnp.zeros((GLA_BLOCK, LANES), F32)
            dv_intra = []
            for j in range(2):
                hs = slice(128 * j, 128 * j + 128)
                a = jnp.where(causal, _mm_nt(qm * lane_masks[j], km), 0.0)
                da = jnp.where(causal, _mm_nt(dop[:, hs], vp[:, hs]), 0.0)
                dv_intra.append(_mm_tn(a, dop[:, hs]))
                dqm = dqm + lane_masks[j] * _mm(da, km)
                dkm = dkm + lane_masks[j] * _mm_tn(da, qm)
            grow = [blk * _mm_tn(dop[C * c:C * c + C], qh[C * c:C * c + C]) for c in range(NB)]
            dst = dst_ref[p]
            dst_after = [None] * NB
            for c in reversed(range(NB)):
                dst_after[c] = dst
                dst = dst * gam[c] + grow[c]
            dst_ref[p] = dst
            dqh, dkh, dv_state, extra = [], [], [], []
            for c in range(NB):
                rows = slice(C * c, C * c + C)
                st = s_ref[c, p]
                dqh.append(_mm(dop[rows], st))
                dkh.append(_mm(vp[rows], dst_after[c]))
                dv_state.append(_mm_nt(kh[rows], dst_after[c]))
                extra.append(jnp.sum(dkh[c] * kh[rows], axis=0, keepdims=True)
                             + jnp.sum(st * dst_after[c], axis=0, keepdims=True) * gam[c])
            dqs = jnp.concatenate(dqh, axis=0) * eb + dqm * eqm
            dk = jnp.concatenate(dkh, axis=0) * ekl + dkm * ekm
            dg = _mm_f32(anti_causal.astype(F32), dqs * qs - dk * kk) + _chunk_rows(extra)
            dq_ref[:, cols] = dqs * 0.125
            dk_ref[:, cols] = dk
            dz_ref[:, cols] = dg * (1.0 - jnp.exp(GLA_TAU * la_ref[:, cols])) * (1.0 / GLA_TAU)
            dv_ref[:, 256 * p:256 * p + 256] = jnp.concatenate(dv_state, axis=0) + jnp.concatenate(dv_intra, axis=1)

    rev = lambda i: (nblk - 1 - i, 0)
    rspec = lambda cols: pl.BlockSpec((GLA_BLOCK, cols), rev)
    return pl.pallas_call(
        body, name="gla_bwd", grid=(nblk,),
        out_shape=[jax.ShapeDtypeStruct((T, GLA_QK), F32), jax.ShapeDtypeStruct((T, GLA_QK), F32),
                   jax.ShapeDtypeStruct((T, GLA_V), F32), jax.ShapeDtypeStruct((T, GLA_QK), F32)],
        in_specs=[rspec(GLA_QK), rspec(GLA_QK), rspec(GLA_V), rspec(GLA_QK),
                  pl.BlockSpec((NB, 2, 256, LANES), lambda i: (nblk - 1 - i, 0, 0, 0)), rspec(GLA_V)],
        out_specs=[rspec(GLA_QK), rspec(GLA_QK), rspec(GLA_V), rspec(GLA_QK)],
        scratch_shapes=[pltpu.VMEM((2, 256, LANES), F32), pltpu.VMEM((GLA_BLOCK, GLA_QK), F32)],
        compiler_params=_params(),
    )(q, k, v, la, s_all, do)


SWA_GROUP = 4


def _swa_stack(ref, g, first):
    parts = []
    for j in range(SWA_GROUP):
        m = 2 * g + j // 2
        pair = ref[:, 128 * m:128 * m + 128]
        zero = jnp.zeros_like(pair)
        parts.append(jnp.where(first, pair, zero) if j % 2 == 0 else jnp.where(first, zero, pair))
    return jnp.concatenate(parts, axis=0)


def _swa_unstack(rows, mm, first):
    W = SWA_BLOCK
    return jnp.where(first, rows[W * 2 * mm:W * (2 * mm + 1)], rows[W * (2 * mm + 1):W * (2 * mm + 2)])


def _swa_probs(qs, kp, kc, vp, vc, i, g, sink_ref, first4):
    W = SWA_BLOCK
    R = SWA_GROUP * W
    r, c = _iota((R, W), 0) & (W - 1), _iota((R, W), 1)
    neg = -1e30
    s_p = jnp.where((c > r) & (i > 0), _mm_nt(qs, kp), neg)
    s_c = jnp.where(c <= r, _mm_nt(qs, kc), neg)
    head = _iota((R, 1), 0) >> 7
    sink = jnp.where(head == 0, sink_ref[4 * g], jnp.where(head == 1, sink_ref[4 * g + 1],
                                                           jnp.where(head == 2, sink_ref[4 * g + 2], sink_ref[4 * g + 3])))
    m = jnp.maximum(jnp.max(jnp.maximum(s_p, s_c), axis=-1, keepdims=True), sink)
    p_p = jnp.exp(s_p - m)
    p_c = jnp.exp(s_c - m)
    p_s = jnp.exp(sink - m)
    one = jnp.ones((W, LANES), BF16)
    first = _iota((W, LANES), 1) < 64
    acc = _mm(p_p, jnp.where(first, vp, one)) + _mm(p_c, jnp.where(first, vc, one))
    rolled = pltpu.roll(acc, 64, 1)
    denom = jnp.where(first4, rolled, acc) + p_s
    return p_p, p_c, p_s, denom, acc, rolled


def _swa_fwd(sq, kd, vd, sinks):
    T = sq.shape[0]
    W = SWA_BLOCK
    prev = lambda i: (jnp.maximum(i - 1, 0), 0)

    def body(sink_ref, q_ref, kp_ref, kc_ref, vp_ref, vc_ref, o_ref):
        i = pl.program_id(0)
        first4 = _iota((SWA_GROUP * W, LANES), 1) < 64
        first = _iota((W, LANES), 1) < 64
        for g in range(2):
            gs = slice(128 * g, 128 * g + 128)
            qs = _swa_stack(q_ref, g, first)
            _, _, _, denom, acc, rolled = _swa_probs(qs, kp_ref[:, gs], kc_ref[:, gs], vp_ref[:, gs], vc_ref[:, gs],
                                                     i, g, sink_ref, first4)
            pv = jnp.where(first4, acc, rolled)
            o = pv / denom
            for mm in range(2):
                m = 2 * g + mm
                o_ref[:, 128 * m:128 * m + 128] = _swa_unstack(o, mm, first).astype(BF16)

    return pl.pallas_call(
        body, name="swa_fwd", grid=(T // W,), out_shape=jax.ShapeDtypeStruct((T, SWA_Q), BF16),
        in_specs=[pl.BlockSpec(memory_space=pltpu.SMEM), _row_spec(W, SWA_Q), pl.BlockSpec((W, 256), prev),
                  _row_spec(W, 256), pl.BlockSpec((W, 256), prev), _row_spec(W, 256)],
        out_specs=_row_spec(W, SWA_Q),
        compiler_params=_params(),
    )(sinks, sq, kd, kd, vd, vd)


def _swa_bwd(sq, kd, vd, sinks, do):
    T = sq.shape[0]
    W = SWA_BLOCK
    n = T // W
    cur = lambda i: (jnp.minimum(i, n - 1), 0)
    prev = lambda i: (jnp.clip(i - 1, 0, n - 1), 0)

    def body(sink_ref, q_ref, kp_ref, kc_ref, vp_ref, vc_ref, do_ref, dq_ref, dk_ref, dv_ref, ds_ref, ck_ref, cv_ref):
        i = pl.program_id(0)

        @pl.when(i == 0)
        def _():
            ds_ref[...] = jnp.zeros_like(ds_ref)
            ck_ref[...] = jnp.zeros_like(ck_ref)
            cv_ref[...] = jnp.zeros_like(cv_ref)

        @pl.when(i < n)
        def _():
            first4 = _iota((SWA_GROUP * W, LANES), 1) < 64
            first = _iota((W, LANES), 1) < 64
            for g in range(2):
                gs = slice(128 * g, 128 * g + 128)
                kp, kc, vp, vc = kp_ref[:, gs], kc_ref[:, gs], vp_ref[:, gs], vc_ref[:, gs]
                qs = _swa_stack(q_ref, g, first)
                dos = _swa_stack(do_ref, g, first)
                p_p, p_c, p_s, denom, _, _ = _swa_probs(qs, kp, kc, vp, vc, i, g, sink_ref, first4)
                inv = 1.0 / denom
                p_p, p_c = p_p * inv, p_c * inv
                dp_p = _mm_nt(dos, vp)
                dp_c = _mm_nt(dos, vc)
                delta = jnp.sum(p_p * dp_p + p_c * dp_c, axis=-1, keepdims=True)
                ds_p = p_p * (dp_p - delta)
                ds_c = p_c * (dp_c - delta)
                rows = slice(SWA_GROUP * W * g, SWA_GROUP * W * (g + 1))
                ds_ref[rows, :] = ds_ref[rows, :] - (p_s * delta) * inv
                dq = (_mm(ds_p, kp) + _mm(ds_c, kc)) * 0.125
                for mm in range(2):
                    m = 2 * g + mm
                    dq_ref[:, 128 * m:128 * m + 128] = _swa_unstack(dq, mm, first)
                dk_ref[:, gs] = ck_ref[:, gs] + _mm_tn(ds_p, qs)
                dv_ref[:, gs] = cv_ref[:, gs] + _mm_tn(p_p, dos)
                ck_ref[:, gs] = _mm_tn(ds_c, qs)
                cv_ref[:, gs] = _mm_tn(p_c, dos)

        @pl.when(i == n)
        def _():
            dk_ref[...] = ck_ref[...]
            dv_ref[...] = cv_ref[...]

    return pl.pallas_call(
        body, name="swa_bwd", grid=(n + 1,),
        out_shape=[jax.ShapeDtypeStruct((T, SWA_Q), F32), jax.ShapeDtypeStruct((T, 256), F32),
                   jax.ShapeDtypeStruct((T, 256), F32), jax.ShapeDtypeStruct((SWA_HEADS * W, LANES), F32)],
        in_specs=[pl.BlockSpec(memory_space=pltpu.SMEM), pl.BlockSpec((W, SWA_Q), cur), pl.BlockSpec((W, 256), prev),
                  pl.BlockSpec((W, 256), cur), pl.BlockSpec((W, 256), prev), pl.BlockSpec((W, 256), cur),
                  pl.BlockSpec((W, SWA_Q), cur)],
        out_specs=[pl.BlockSpec((W, SWA_Q), cur), pl.BlockSpec((W, 256), prev), pl.BlockSpec((W, 256), prev),
                   _const_spec((SWA_HEADS * W, LANES))],
        scratch_shapes=[pltpu.VMEM((W, 256), F32), pltpu.VMEM((W, 256), F32)],
        compiler_params=_params(),
    )(sinks, sq, kd, kd, vd, vd, do)


def _mix_out_fwd(x, og, gg, osw, gnorm, wout, g2):
    T = x.shape[0]

    def body(x_ref, og_ref, gg_ref, osw_ref, gn_ref, wout_ref, g2_ref, x1_ref, cat_ref, mix_ref):
        gn = gn_ref[...]
        for j in range(GLA_HEADS):
            hs = slice(128 * j, 128 * j + 128)
            o = og_ref[:, hs]
            r = lax.rsqrt(jnp.mean(o * o, axis=-1, keepdims=True) + EPS)
            gate = gg_ref[:, hs]
            cat_ref[:, hs] = (o * r * gn * (gate * _sigmoid(gate))).astype(BF16)
        cat_ref[:, GLA_V:] = osw_ref[...]
        mix = _mm(cat_ref[...], wout_ref[...])
        mix_ref[...] = mix
        r2 = lax.rsqrt(jnp.mean(mix * mix, axis=-1, keepdims=True) + EPS)
        x1_ref[...] = x_ref[...] + mix * r2 * g2_ref[...]

    return pl.pallas_call(
        body, name="mix_out_fwd", grid=(T // TM,),
        out_shape=[jax.ShapeDtypeStruct((T, D_MODEL), F32), jax.ShapeDtypeStruct((T, D_MODEL), BF16),
                   jax.ShapeDtypeStruct((T, D_MODEL), F32)],
        in_specs=[_row_spec(TM, D_MODEL), _row_spec(TM, GLA_V), _row_spec(TM, GLA_V), _row_spec(TM, SWA_Q),
                  _const_spec((1, LANES)), _const_spec((D_MODEL, D_MODEL)), _const_spec((1, D_MODEL))],
        out_specs=[_row_spec(TM, D_MODEL), _row_spec(TM, D_MODEL), _row_spec(TM, D_MODEL)],
        compiler_params=_params(),
    )(x, og, gg, osw, gnorm, wout, g2)


HALO = 8


def _rows_before(v, prev1, prev2):
    row = _iota(v.shape, 0)
    m1 = jnp.where(row == 0, prev1, pltpu.roll(v, 1, 0))
    m2 = jnp.where(row == 0, prev2, jnp.where(row == 1, prev1, pltpu.roll(v, 2, 0)))
    return m1, m2


def _rows_after(v, next1, next2):
    n = v.shape[0]
    row = _iota(v.shape, 0)
    p1 = jnp.where(row == n - 1, next1, pltpu.roll(v, n - 1, 0))
    p2 = jnp.where(row == n - 1, next2, jnp.where(row == n - 2, next1, pltpu.roll(v, n - 2, 0)))
    return p1, p2


def _ff_pieces():
    return [(j, off, wd) for j in range(2) for off, wd in FF_PIECES]


def _ffn_fwd(x1, g3, wup, cw, cb, wdown, g4, target):
    T = x1.shape[0]

    def body(x1_ref, g3_ref, wup_hbm, cw_ref, cb_ref, wdn_hbm, g4_ref, tg_ref,
             h2_ref, up_ref, a_ref, c1_ref, c2_ref, y_ref, dx2_ref, loss_ref, wup_v, wdn_v, carry_ref, sems):
        _load_once(wup_hbm, wup_v, sems.at[0])
        _load_once(wdn_hbm, wdn_v, sems.at[1])

        @pl.when(pl.program_id(0) == 0)
        def _():
            carry_ref[...] = jnp.zeros_like(carry_ref)
            loss_ref[...] = jnp.zeros_like(loss_ref)

        x1 = x1_ref[...]
        r3 = lax.rsqrt(jnp.mean(x1 * x1, axis=-1, keepdims=True) + EPS)
        h2 = (x1 * r3 * g3_ref[...]).astype(BF16)
        h2_ref[...] = h2
        y = jnp.zeros((TM, D_MODEL), F32)
        for j, off, wd in _ff_pieces():
            base = SHARD_FF * j + off
            u = []
            for half in range(2):
                cs = slice(D_FF * half + base, D_FF * half + base + wd)
                upb = _mm(h2, wup_v[2 * half + j, :, off:off + wd]).astype(BF16)
                up_ref[:, cs] = upb
                upf = upb.astype(F32)
                m1, m2 = _rows_before(upf, carry_ref[HALO - 1:HALO, cs], carry_ref[HALO - 2:HALO - 1, cs])
                u.append(cb_ref[:, cs] + cw_ref[0:1, cs] * m2 + cw_ref[1:2, cs] * m1 + cw_ref[2:3, cs] * upf)
                carry_ref[:, cs] = upf[TM - HALO:TM, :]
            act, dact = _gelu_parts(u[1])
            a = (act * u[0]).astype(BF16)
            out = slice(base, base + wd)
            a_ref[:, out] = a
            c1_ref[:, out] = act.astype(BF16)
            c2_ref[:, out] = (u[0] * dact).astype(BF16)
            y = y + _mm(a, wdn_v[out, :])
        y_ref[...] = y
        r4 = lax.rsqrt(jnp.mean(y * y, axis=-1, keepdims=True) + EPS)
        err = x1 + y * r4 * g4_ref[...] - tg_ref[...]
        dx2_ref[...] = err * (1.0 / D_MODEL)
        loss_ref[...] = loss_ref[...] + jnp.sum(err * err) * (0.5 / D_MODEL)

    outs = [
        jax.ShapeDtypeStruct((T, D_MODEL), BF16),
        jax.ShapeDtypeStruct((T, 2 * D_FF), BF16),
        jax.ShapeDtypeStruct((T, D_FF), BF16),
        jax.ShapeDtypeStruct((T, D_FF), BF16),
        jax.ShapeDtypeStruct((T, D_FF), BF16),
        jax.ShapeDtypeStruct((T, D_MODEL), F32),
        jax.ShapeDtypeStruct((T, D_MODEL), F32),
        jax.ShapeDtypeStruct((8, LANES), F32),
    ]
    return pl.pallas_call(
        body, name="ffn_fwd", grid=(T // TM,), out_shape=outs,
        in_specs=[_row_spec(TM, D_MODEL), _const_spec((1, D_MODEL)), _any_spec(), _const_spec((3, 2 * D_FF)),
                  _const_spec((1, 2 * D_FF)), _any_spec(), _const_spec((1, D_MODEL)), _row_spec(TM, D_MODEL)],
        out_specs=[_row_spec(TM, D_MODEL), _row_spec(TM, 2 * D_FF), _row_spec(TM, D_FF), _row_spec(TM, D_FF),
                   _row_spec(TM, D_FF), _row_spec(TM, D_MODEL), _row_spec(TM, D_MODEL), _const_spec((8, LANES))],
        scratch_shapes=[pltpu.VMEM((N_SHARD, D_MODEL, SHARD_FF), BF16), pltpu.VMEM((D_FF, D_MODEL), BF16),
                        pltpu.VMEM((HALO, 2 * D_FF), F32), pltpu.SemaphoreType.DMA((2,))],
        compiler_params=_params(),
    )(x1, g3, wup, cw, cb, wdown, g4, target)


def _ffn_bwd(dx2, y, g4, up, c1, c2, cw, wdown, wup, x1, g3):
    T = dx2.shape[0]
    nt = T // TM
    rev = lambda i: (nt - 1 - i, 0)

    def body(dn_ref, y_ref, g4_ref, up_ref, c1_ref, c2_ref, cw_ref, wdn_hbm, wup_hbm, x1_ref, g3_ref,
             dy_ref, dup_ref, dx1_ref, dg4_ref, dg3_ref, dcb_ref, dcw_ref, wup_v, wdn_v, carry_ref, sems):
        _load_once(wup_hbm, wup_v, sems.at[0])
        _load_once(wdn_hbm, wdn_v, sems.at[1])

        @pl.when(pl.program_id(0) == 0)
        def _():
            carry_ref[...] = jnp.zeros_like(carry_ref)
            dg4_ref[...] = jnp.zeros_like(dg4_ref)
            dg3_ref[...] = jnp.zeros_like(dg3_ref)
            dcb_ref[...] = jnp.zeros_like(dcb_ref)
            dcw_ref[...] = jnp.zeros_like(dcw_ref)

        dn = dn_ref[...]
        y = y_ref[...]
        g4v = g4_ref[...]
        r4 = lax.rsqrt(jnp.mean(y * y, axis=-1, keepdims=True) + EPS)
        dg4_ref[...] = dg4_ref[...] + jnp.sum(dn * y * r4, axis=0, keepdims=True)
        dy = _rms_bwd(y, r4, g4v, dn).astype(BF16)
        dy_ref[...] = dy
        dh2 = jnp.zeros((TM, D_MODEL), F32)
        for j, off, wd in _ff_pieces():
            base = SHARD_FF * j + off
            da = _mm_nt(dy, wdn_v[base:base + wd, :])
            for half, coef_ref in enumerate((c1_ref, c2_ref)):
                cs = slice(D_FF * half + base, D_FF * half + base + wd)
                du = da * coef_ref[:, base:base + wd].astype(F32)
                p1, p2 = _rows_after(du, carry_ref[0:1, cs], carry_ref[1:2, cs])
                carry_ref[:, cs] = du[0:HALO, :]
                upf = up_ref[:, cs].astype(F32)
                dcb_ref[:, cs] = dcb_ref[:, cs] + jnp.sum(du, axis=0, keepdims=True)
                dcw_ref[0:1, cs] = dcw_ref[0:1, cs] + jnp.sum(p2 * upf, axis=0, keepdims=True)
                dcw_ref[1:2, cs] = dcw_ref[1:2, cs] + jnp.sum(p1 * upf, axis=0, keepdims=True)
                dcw_ref[2:3, cs] = dcw_ref[2:3, cs] + jnp.sum(du * upf, axis=0, keepdims=True)
                dup = (cw_ref[2:3, cs] * du + cw_ref[1:2, cs] * p1 + cw_ref[0:1, cs] * p2).astype(BF16)
                dup_ref[:, cs] = dup
                dh2 = dh2 + _mm_nt(dup, wup_v[2 * half + j, :, off:off + wd])
        x1 = x1_ref[...]
        r3 = lax.rsqrt(jnp.mean(x1 * x1, axis=-1, keepdims=True) + EPS)
        dg3_ref[...] = dg3_ref[...] + jnp.sum(dh2 * x1 * r3, axis=0, keepdims=True)
        dx1_ref[...] = dn + _rms_bwd(x1, r3, g3_ref[...], dh2)

    outs = [
        jax.ShapeDtypeStruct((T, D_MODEL), BF16),
        jax.ShapeDtypeStruct((T, 2 * D_FF), BF16),
        jax.ShapeDtypeStruct((T, D_MODEL), F32),
        jax.ShapeDtypeStruct((1, D_MODEL), F32),
        jax.ShapeDtypeStruct((1, D_MODEL), F32),
        jax.ShapeDtypeStruct((1, 2 * D_FF), F32),
        jax.ShapeDtypeStruct((3, 2 * D_FF), F32),
    ]
    return pl.pallas_call(
        body, name="ffn_bwd", grid=(nt,), out_shape=outs,
        in_specs=[pl.BlockSpec((TM, D_MODEL), rev), pl.BlockSpec((TM, D_MODEL), rev), _const_spec((1, D_MODEL)),
                  pl.BlockSpec((TM, 2 * D_FF), rev), pl.BlockSpec((TM, D_FF), rev), pl.BlockSpec((TM, D_FF), rev),
                  _const_spec((3, 2 * D_FF)), _any_spec(), _any_spec(), pl.BlockSpec((TM, D_MODEL), rev),
                  _const_spec((1, D_MODEL))],
        out_specs=[pl.BlockSpec((TM, D_MODEL), rev), pl.BlockSpec((TM, 2 * D_FF), rev), pl.BlockSpec((TM, D_MODEL), rev),
                   _const_spec((1, D_MODEL)), _const_spec((1, D_MODEL)), _const_spec((1, 2 * D_FF)),
                   _const_spec((3, 2 * D_FF))],
        scratch_shapes=[pltpu.VMEM((N_SHARD, D_MODEL, SHARD_FF), BF16), pltpu.VMEM((D_FF, D_MODEL), BF16),
                        pltpu.VMEM((HALO, 2 * D_FF), F32), pltpu.SemaphoreType.DMA((2,))],
        compiler_params=_params(),
    )(dx2, y, g4, up, c1, c2, cw, wdown, wup, x1, g3)


def _mix_out_bwd(dx1, mix, g2, wout, og, gg, gnorm):
    T = dx1.shape[0]

    def body(dx1_ref, mix_ref, g2_ref, wout_ref, og_ref, gg_ref, gn_ref,
             dmix_ref, dog_ref, dgg_ref, dosw_ref, dg2_ref, dgn_ref):
        @pl.when(pl.program_id(0) == 0)
        def _():
            dg2_ref[...] = jnp.zeros_like(dg2_ref)
            dgn_ref[...] = jnp.zeros_like(dgn_ref)

        dx1 = dx1_ref[...]
        mix = mix_ref[...]
        r2 = lax.rsqrt(jnp.mean(mix * mix, axis=-1, keepdims=True) + EPS)
        dg2_ref[...] = dg2_ref[...] + jnp.sum(dx1 * mix * r2, axis=0, keepdims=True)
        dmix = _rms_bwd(mix, r2, g2_ref[...], dx1).astype(BF16)
        dmix_ref[...] = dmix
        dcat = _mm_nt(dmix, wout_ref[...])
        dosw_ref[...] = dcat[:, GLA_V:].astype(BF16)
        gn = gn_ref[...]
        dgn = jnp.zeros((1, LANES), F32)
        for j in range(GLA_HEADS):
            hs = slice(128 * j, 128 * j + 128)
            o = og_ref[:, hs]
            r = lax.rsqrt(jnp.mean(o * o, axis=-1, keepdims=True) + EPS)
            gate = gg_ref[:, hs]
            sg = _sigmoid(gate)
            dgated = dcat[:, hs]
            dnorm = dgated * (gate * sg)
            dgg_ref[:, hs] = dgated * (o * r * gn) * (sg * (1.0 + gate * (1.0 - sg)))
            dgn = dgn + jnp.sum(dnorm * o * r, axis=0, keepdims=True)
            dog_ref[:, hs] = _rms_bwd(o, r, gn, dnorm)
        dgn_ref[...] = dgn_ref[...] + dgn

    return pl.pallas_call(
        body, name="mix_out_bwd", grid=(T // TM,),
        out_shape=[jax.ShapeDtypeStruct((T, D_MODEL), BF16), jax.ShapeDtypeStruct((T, GLA_V), F32),
                   jax.ShapeDtypeStruct((T, GLA_V), F32), jax.ShapeDtypeStruct((T, SWA_Q), BF16),
                   jax.ShapeDtypeStruct((1, D_MODEL), F32), jax.ShapeDtypeStruct((1, LANES), F32)],
        in_specs=[_row_spec(TM, D_MODEL), _row_spec(TM, D_MODEL), _const_spec((1, D_MODEL)),
                  _const_spec((D_MODEL, D_MODEL)), _row_spec(TM, GLA_V), _row_spec(TM, GLA_V), _const_spec((1, LANES))],
        out_specs=[_row_spec(TM, D_MODEL), _row_spec(TM, GLA_V), _row_spec(TM, GLA_V), _row_spec(TM, SWA_Q),
                   _const_spec((1, D_MODEL)), _const_spec((1, LANES))],
        compiler_params=_params(),
    )(dx1, mix, g2, wout, og, gg, gnorm)


def _proj_bwd(x, g1, wp, gup, glr, dq, dk, dv, dgg, dsq, dkd, dvd, dz, rc, rsa, rsb, dx1):
    T = x.shape[0]

    def body(x_ref, g1_ref, wp_hbm, gup_ref, glr_ref, dq_ref, dk_ref, dv_ref, dgg_ref, dsq_ref, dkd_ref, dvd_ref,
             dz_ref, rc_ref, rsa_ref, rsb_ref, dx1_ref, dx_ref, dp_ref, dg1_ref, dgup_ref, dgb_ref, wp_v, sem):
        _load_once(wp_hbm, wp_v, sem)

        @pl.when(pl.program_id(0) == 0)
        def _():
            dg1_ref[...] = jnp.zeros_like(dg1_ref)
            dgup_ref[...] = jnp.zeros_like(dgup_ref)
            dgb_ref[...] = jnp.zeros_like(dgb_ref)

        rc_, rsa_, rsb_ = rc_ref[...], rsa_ref[...], rsb_ref[...]
        dp_ref[:, 0:256] = dq_ref[...].astype(BF16)
        dp_ref[:, 256:512] = dk_ref[...].astype(BF16)
        dp_ref[:, 512:1024] = dv_ref[...].astype(BF16)
        dp_ref[:, 1024:1536] = dgg_ref[...].astype(BF16)
        for s in range(4):
            cs = slice(128 * s, 128 * s + 128)
            dp_ref[:, 1536 + 128 * s:1664 + 128 * s] = _rotate_bwd(dsq_ref[:, cs], rc_, rsa_, rsb_).astype(BF16)
        first = _iota((TM, LANES), 1) < 64
        dk0 = dkd_ref[:, 0:128]
        dk1 = dkd_ref[:, 128:256]
        dkr = jnp.where(first, dk0 + pltpu.roll(dk0, 64, 1), dk1 + pltpu.roll(dk1, 64, 1))
        dp_ref[:, 2048:2176] = _rotate_bwd(dkr, rc_, rsa_, rsb_).astype(BF16)
        dv0 = dvd_ref[:, 0:128]
        dv1 = dvd_ref[:, 128:256]
        dp_ref[:, 2176:2304] = jnp.where(first, dv0 + pltpu.roll(dv0, 64, 1), dv1 + pltpu.roll(dv1, 64, 1)).astype(BF16)
        dz = dz_ref[...]
        dzb = dz.astype(BF16)
        dp_ref[:, 2304:2432] = _mm_nt(dzb, gup_ref[...]).astype(BF16)
        dgup_ref[...] = dgup_ref[...] + _mm_tn(glr_ref[...], dzb)
        dgb_ref[...] = dgb_ref[...] + jnp.sum(dz, axis=0, keepdims=True)
        dh1 = _mm_nt(dp_ref[...], wp_v[...])
        xt = x_ref[...]
        r = lax.rsqrt(jnp.mean(xt * xt, axis=-1, keepdims=True) + EPS)
        dg1_ref[...] = dg1_ref[...] + jnp.sum(dh1 * xt * r, axis=0, keepdims=True)
        dx_ref[...] = dx1_ref[...] + _rms_bwd(xt, r, g1_ref[...], dh1)

    row = lambda cols: _row_spec(TM, cols)
    return pl.pallas_call(
        body, name="proj_bwd", grid=(T // TM,),
        out_shape=[jax.ShapeDtypeStruct((T, D_MODEL), F32), jax.ShapeDtypeStruct((T, IN_WIDTH_PAD), BF16),
                   jax.ShapeDtypeStruct((1, D_MODEL), F32), jax.ShapeDtypeStruct((LANES, GLA_QK), F32),
                   jax.ShapeDtypeStruct((1, GLA_QK), F32)],
        in_specs=[row(D_MODEL), _const_spec((1, D_MODEL)), _any_spec(), _const_spec((LANES, GLA_QK)), row(LANES),
                  row(GLA_QK), row(GLA_QK), row(GLA_V), row(GLA_V), row(SWA_Q), row(256), row(256), row(GLA_QK),
                  row(LANES), row(LANES), row(LANES), row(D_MODEL)],
        out_specs=[row(D_MODEL), row(IN_WIDTH_PAD), _const_spec((1, D_MODEL)), _const_spec((LANES, GLA_QK)),
                   _const_spec((1, GLA_QK))],
        scratch_shapes=[pltpu.VMEM((D_MODEL, IN_WIDTH_PAD), BF16), pltpu.SemaphoreType.DMA],
        compiler_params=_params(),
    )(x, g1, wp, gup, glr, dq, dk, dv, dgg, dsq, dkd, dvd, dz, rc, rsa, rsb, dx1)


def _matmul_tn(a, b, tn, name, column_blocks_major=False):
    T, M = a.shape
    N = b.shape[1]
    tk = 512 if T % 512 == 0 else TM
    nk = T // tk
    if column_blocks_major:
        out_shape = jax.ShapeDtypeStruct((N // tn, M, tn), F32)
        out_spec = pl.BlockSpec((None, M, tn), lambda j, kk: (j, 0, 0))
    else:
        out_shape = jax.ShapeDtypeStruct((M, N), F32)
        out_spec = pl.BlockSpec((M, tn), lambda j, kk: (0, j))

    def body(a_ref, b_ref, o_ref):
        kk = pl.program_id(1)

        @pl.when(kk == 0)
        def _():
            o_ref[...] = jnp.zeros_like(o_ref)

        o_ref[...] = o_ref[...] + _mm_tn(a_ref[...], b_ref[...])

    return pl.pallas_call(
        body, name=name, grid=(N // tn, nk), out_shape=out_shape,
        in_specs=[pl.BlockSpec((tk, M), lambda j, kk: (kk, 0)), pl.BlockSpec((tk, tn), lambda j, kk: (kk, j))],
        out_specs=out_spec,
        compiler_params=_params(),
    )(a, b)


def _adamw(w, g, m, v, rows, name):
    R, C = w.shape

    def body(w_ref, g_ref, m_ref, v_ref, d_ref, m2_ref, v2_ref):
        g_ = g_ref[...]
        m2 = ADAM_B1 * m_ref[...] + (1.0 - ADAM_B1) * g_
        v2 = ADAM_B2 * v_ref[...] + (1.0 - ADAM_B2) * (g_ * g_)
        m_hat = m2 / (1.0 - ADAM_B1 ** ADAM_STEP)
        v_hat = v2 / (1.0 - ADAM_B2 ** ADAM_STEP)
        d_ref[...] = -ADAM_LR * (m_hat / (jnp.sqrt(v_hat) + ADAM_EPS) + ADAM_WD * w_ref[...])
        m2_ref[...] = m2
        v2_ref[...] = v2

    spec = pl.BlockSpec((rows, C), lambda i: (i, 0))
    return pl.pallas_call(
        body, name=name, grid=(R // rows,), out_shape=[jax.ShapeDtypeStruct((R, C), F32)] * 3,
        in_specs=[spec] * 4, out_specs=[spec] * 3, compiler_params=_params(),
    )(w, g, m, v)


def _place():
    x, y, c = lax.axis_index("x"), lax.axis_index("y"), lax.axis_index("c")
    chips = [(1 - x, y), (x, 1 - y), (1 - x, 1 - y)]
    return x, y, c, chips


class _staged_copies:
    def __init__(self, srcs, dsts, stage, sems):
        n = len(srcs)
        self.loads = [pltpu.make_async_copy(srcs[k], stage[k], sems.at[k]) for k in range(n)]
        self.stores = [pltpu.make_async_copy(stage[k], dsts[k], sems.at[n + k]) for k in range(n)]

    def load(self):
        for cp in self.loads:
            cp.start()

    def store(self):
        for ld, st in zip(self.loads, self.stores):
            ld.wait()
            st.start()

    def finish(self):
        for cp in self.stores:
            cp.wait()


def _allgather_shards(parts):
    n = len(parts)
    units = [(k, r, AG_UNIT_ROWS[k]) for k in range(n) for r in range(0, parts[k].shape[0] // 2, AG_UNIT_ROWS[k])]
    nu = len(units)

    def body(*refs):
        ins, outs, stage = refs[:n], refs[n:2 * n], refs[2 * n:3 * n]
        send_sems, recv_sems, local_sems = refs[3 * n:]
        x, y, c, chips = _place()
        sibling = (x, y, 1 - c)
        own = _staged_copies(ins, [o.at[2 * x + y] for o in outs], stage, local_sems)

        def block(i, px, py, half):
            k, r, u = units[i]
            return outs[k].at[2 * px + py, pl.ds(half * (parts[k].shape[0] // 2) + r, u), :]

        def copy(i, j, px, py, half, to, src=None):
            return pltpu.make_async_remote_copy(
                src_ref=block(i, px, py, half) if src is None else src, dst_ref=block(i, px, py, half),
                send_sem=send_sems.at[nu * j + i], recv_sem=recv_sems.at[nu * j + i], device_id=to, device_id_type=MESH)

        own.load()
        first, passed = [], []
        for i, (k, r, u) in enumerate(units):
            for j, chip in enumerate(chips):
                src = ins[k].at[pl.ds(c * (parts[k].shape[0] // 2) + r, u), :]
                first.append(copy(i, j, x, y, c, (*chip, c), src=src))
                first[-1].start()
        own.store()
        for i in range(nu):
            for j, chip in enumerate(chips):
                copy(i, j, *chip, c, (x, y, c)).wait_recv()
                passed.append(copy(i, 3 + j, *chip, c, sibling))
                passed[-1].start()
        for i in range(nu):
            for j, chip in enumerate(chips):
                copy(i, 3 + j, *chip, 1 - c, (x, y, c)).wait_recv()
        for cp in first + passed:
            cp.wait_send()
        own.finish()

    return pl.pallas_call(
        body, name="allgather_shards", out_shape=[jax.ShapeDtypeStruct((N_SHARD,) + p.shape, p.dtype) for p in parts],
        in_specs=[_any_spec()] * n, out_specs=[_any_spec()] * n,
        scratch_shapes=[pltpu.VMEM(p.shape, p.dtype) for p in parts] + [
            pltpu.SemaphoreType.DMA((6 * nu,)), pltpu.SemaphoreType.DMA((6 * nu,)), pltpu.SemaphoreType.DMA((2 * n,))],
        compiler_params=_params(),
    )(*parts)


def _d2d_pieces(rows, piece_rows):
    return [(r, piece_rows) for r in range(0, rows, piece_rows)]


def _rs_pair_swap(arrs, piece_rows):
    n = len(arrs)

    def body(*refs):
        ins, outs = refs[:n], refs[n:2 * n]
        send_sems, recv_sems = refs[2 * n:]
        x, y, c, _ = _place()
        sibling = (x, y, 1 - c)
        for k in range(n):
            H = arrs[k].shape[1] // 2
            for s in range(N_SHARD):
                for r, pr in _d2d_pieces(H, piece_rows[k]):
                    pltpu.make_async_remote_copy(
                        src_ref=ins[k].at[s, pl.ds((1 - c) * H + r, pr), :], dst_ref=outs[k].at[s, pl.ds(r, pr), :],
                        send_sem=send_sems.at[k], recv_sem=recv_sems.at[k], device_id=sibling, device_id_type=MESH).start()
        for k in range(n):
            H = arrs[k].shape[1] // 2
            whole = pltpu.make_async_remote_copy(
                src_ref=ins[k].at[:, pl.ds(0, H), :], dst_ref=outs[k], send_sem=send_sems.at[k], recv_sem=recv_sems.at[k],
                device_id=sibling, device_id_type=MESH)
            whole.wait_recv()
            whole.wait_send()

    return pl.pallas_call(
        body, name="rs_pair_swap",
        out_shape=[jax.ShapeDtypeStruct((N_SHARD, a.shape[1] // 2, a.shape[2]), F32) for a in arrs],
        in_specs=[_any_spec()] * n, out_specs=[_any_spec()] * n,
        scratch_shapes=[pltpu.SemaphoreType.DMA((n,)), pltpu.SemaphoreType.DMA((n,))],
    )(*arrs)


def _rs_add_pair(a, got, core, rows, name):
    _, H, C = got.shape
    nb = H // rows

    def body(c_ref, a_ref, b_ref, o_ref):
        o_ref[...] = (a_ref[...] + b_ref[...]).astype(BF16)

    spec = pl.BlockSpec((1, rows, C), lambda s, r, c_ref: (s, r, 0))
    return pl.pallas_call(
        body, name=name, out_shape=jax.ShapeDtypeStruct(got.shape, BF16),
        grid_spec=pltpu.PrefetchScalarGridSpec(
            num_scalar_prefetch=1, grid=(N_SHARD, nb),
            in_specs=[pl.BlockSpec((1, rows, C), lambda s, r, c_ref: (s, c_ref[0] * nb + r, 0)), spec], out_specs=spec),
        compiler_params=_params(),
    )(core, a, got)


def _rs_chip_scatter(parts):
    n = len(parts)

    def body(*refs):
        ins, outs, stage = refs[:n], refs[n:2 * n], refs[2 * n:3 * n]
        send_sems, recv_sems, local_sems = refs[3 * n:]
        x, y, c, chips = _place()
        me = 2 * x + y
        own = _staged_copies([i.at[me] for i in ins], [o.at[me] for o in outs], stage, local_sems)
        own.load()
        sends = []
        for k in range(n):
            for j, (px, py) in enumerate(chips):
                sends.append(pltpu.make_async_remote_copy(
                    src_ref=ins[k].at[2 * px + py], dst_ref=outs[k].at[me], send_sem=send_sems.at[3 * k + j],
                    recv_sem=recv_sems.at[3 * k + j], device_id=(px, py, c), device_id_type=MESH))
                sends[-1].start()
        own.store()
        for k in range(n):
            for j, (px, py) in enumerate(chips):
                pltpu.make_async_remote_copy(
                    src_ref=ins[k].at[me], dst_ref=outs[k].at[2 * px + py], send_sem=send_sems.at[3 * k + j],
                    recv_sem=recv_sems.at[3 * k + j], device_id=(px, py, c), device_id_type=MESH).wait_recv()
        for cp in sends:
            cp.wait_send()
        own.finish()

    return pl.pallas_call(
        body, name="rs_chip_scatter", out_shape=[jax.ShapeDtypeStruct(p.shape, p.dtype) for p in parts],
        in_specs=[_any_spec()] * n, out_specs=[_any_spec()] * n,
        scratch_shapes=[pltpu.VMEM(p.shape[1:], p.dtype) for p in parts] + [
            pltpu.SemaphoreType.DMA((3 * n,)), pltpu.SemaphoreType.DMA((3 * n,)), pltpu.SemaphoreType.DMA((2 * n,))],
        compiler_params=_params(),
    )(*parts)


def _rs_sum_chips(parts, rows, name):
    _, H, C = parts.shape

    def body(p_ref, o_ref):
        o_ref[...] = ((p_ref[0].astype(F32) + p_ref[1].astype(F32)) + p_ref[2].astype(F32)) + p_ref[3].astype(F32)

    return pl.pallas_call(
        body, name=name, grid=(H // rows,), out_shape=jax.ShapeDtypeStruct((H, C), F32),
        in_specs=[pl.BlockSpec((N_SHARD, rows, C), lambda r: (0, r, 0))],
        out_specs=pl.BlockSpec((rows, C), lambda r: (r, 0)), compiler_params=_params(),
    )(parts)


def _rs_pair_share(halves, piece_rows):
    n = len(halves)

    def body(*refs):
        ins, outs, stage = refs[:n], refs[n:2 * n], refs[2 * n:3 * n]
        send_sems, recv_sems, local_sems = refs[3 * n:]
        x, y, c, _ = _place()
        sibling = (x, y, 1 - c)
        own = _staged_copies(ins, [outs[k].at[pl.ds(c * halves[k].shape[0], halves[k].shape[0]), :] for k in range(n)],
                             stage, local_sems)
        own.load()
        for k in range(n):
            H = halves[k].shape[0]
            for r, pr in _d2d_pieces(H, piece_rows[k]):
                pltpu.make_async_remote_copy(
                    src_ref=ins[k].at[pl.ds(r, pr), :], dst_ref=outs[k].at[pl.ds(c * H + r, pr), :],
                    send_sem=send_sems.at[k], recv_sem=recv_sems.at[k], device_id=sibling, device_id_type=MESH).start()
        own.store()
        for k in range(n):
            H = halves[k].shape[0]
            whole = pltpu.make_async_remote_copy(
                src_ref=ins[k], dst_ref=outs[k].at[pl.ds((1 - c) * H, H), :], send_sem=send_sems.at[k],
                recv_sem=recv_sems.at[k], device_id=sibling, device_id_type=MESH)
            whole.wait_recv()
            whole.wait_send()
        own.finish()

    return pl.pallas_call(
        body, name="rs_pair_share", out_shape=[jax.ShapeDtypeStruct((2 * h.shape[0], h.shape[1]), F32) for h in halves],
        in_specs=[_any_spec()] * n, out_specs=[_any_spec()] * n,
        scratch_shapes=[pltpu.VMEM(h.shape, F32) for h in halves] + [
            pltpu.SemaphoreType.DMA((n,)), pltpu.SemaphoreType.DMA((n,)), pltpu.SemaphoreType.DMA((2 * n,))],
        compiler_params=_params(),
    )(*halves)


def _allreduce_small(vec):
    def body(v_ref, o_ref, all_ref, send_sems, recv_sems):
        x, y, c, _ = _place()
        me = 4 * x + 2 * y + c
        all_ref[me] = v_ref[...]
        sends = []
        for k in range(1, 8):
            kx, ky, kc = (k >> 2) & 1, (k >> 1) & 1, k & 1
            peer = (x ^ kx, y ^ ky, c ^ kc)
            cp = pltpu.make_async_remote_copy(
                src_ref=v_ref, dst_ref=all_ref.at[me], send_sem=send_sems.at[k - 1], recv_sem=recv_sems.at[k - 1],
                device_id=peer, device_id_type=MESH)
            cp.start()
            sends.append(cp)
        for k in range(1, 8):
            kx, ky, kc = (k >> 2) & 1, (k >> 1) & 1, k & 1
            src = 4 * (x ^ kx) + 2 * (y ^ ky) + (c ^ kc)
            pltpu.make_async_remote_copy(
                src_ref=v_ref, dst_ref=all_ref.at[src], send_sem=send_sems.at[k - 1], recv_sem=recv_sems.at[k - 1],
                device_id=(x, y, c), device_id_type=MESH).wait_recv()
        for cp in sends:
            cp.wait_send()
        total = all_ref[0]
        for d in range(1, 8):
            total = total + all_ref[d]
        o_ref[...] = total

    vm = pl.BlockSpec(memory_space=pltpu.VMEM)
    return pl.pallas_call(
        body, name="allreduce_small", out_shape=jax.ShapeDtypeStruct(vec.shape, F32), in_specs=[vm], out_specs=vm,
        scratch_shapes=[pltpu.VMEM((8,) + vec.shape, F32), pltpu.SemaphoreType.DMA((7,)), pltpu.SemaphoreType.DMA((7,))],
    )(vec)


BIG_NAMES = ("w_in", "w_out", "w_up", "w_down")
MATRIX_NAMES = BIG_NAMES + ("gla_gate_up", "conv_w")
GATE_SHARD = (16, GLA_QK // N_SHARD)
CONVW_SHARD = (3, SHARD_FF)
SMALL_W_ROWS = 96
SMALL_G_ROWS = 64
PIECE_ROWS = (128, 128, 64, 88, SMALL_G_ROWS // 2)
ADD_ROWS = (256, 128, 256, 176, SMALL_G_ROWS // 2)
AG_UNIT_ROWS = (256, 128, 128, 176, SMALL_W_ROWS // 2)


def _pad_rows(flat, rows):
    return jnp.pad(flat, (0, rows * LANES - flat.shape[0])).reshape(rows, LANES)


def _pack_small_weights(gate_up, conv_w):
    bits = lax.bitcast_convert_type(conv_w, BF16)
    return _pad_rows(jnp.concatenate([gate_up.astype(BF16).reshape(-1), bits.reshape(-1)]), SMALL_W_ROWS)


def _unpack_small_weights(packed):
    flat = packed.reshape(N_SHARD, -1)
    n_gate = GATE_SHARD[0] * GATE_SHARD[1]
    n_conv = 2 * CONVW_SHARD[0] * CONVW_SHARD[1]
    gate = flat[:, :n_gate].reshape((N_SHARD,) + GATE_SHARD)
    conv = lax.bitcast_convert_type(flat[:, n_gate:n_gate + n_conv].reshape((N_SHARD,) + CONVW_SHARD + (2,)), F32)
    return (jnp.transpose(gate, (1, 0, 2)).reshape(16, GLA_QK), jnp.transpose(conv, (1, 0, 2)).reshape(3, 2 * D_FF))


def _pack_small_grads(dgate, dconv):
    rows = []
    for s in range(N_SHARD):
        g = dgate[:, GATE_SHARD[1] * s:GATE_SHARD[1] * (s + 1)].reshape(-1)
        cw = dconv[:, SHARD_FF * s:SHARD_FF * (s + 1)].reshape(-1)
        rows.append(_pad_rows(jnp.concatenate([g, cw]), SMALL_G_ROWS))
    return jnp.stack(rows)


def _unpack_small_grads(packed):
    flat = packed.reshape(-1)
    n_gate = GATE_SHARD[0] * GATE_SHARD[1]
    n_conv = CONVW_SHARD[0] * CONVW_SHARD[1]
    return flat[:n_gate].reshape(GATE_SHARD), flat[n_gate:n_gate + n_conv].reshape(CONVW_SHARD)


def _permute_w_in(w):
    pad = jnp.zeros((w.shape[0], IN_WIDTH_PAD - IN_WIDTH), w.dtype)
    return jnp.concatenate([w[:, 0:1024], w[:, 1040:2320], w[:, 1024:1040], pad], axis=1)


def _unpermute_w_in(wp):
    return jnp.concatenate([wp[:, 0:1024], wp[:, 2304:2320], wp[:, 1024:2304]], axis=1)


def _rope_tables(positions):
    half = ROPE_DIM // 2
    inv_freq = ROPE_THETA ** (-jnp.arange(half, dtype=F32) * (2.0 / ROPE_DIM))
    d = jnp.arange(LANES) % SWA_HD
    freq = jnp.where(d < ROPE_DIM, inv_freq[d % half], 0.0)
    ang = positions.astype(F32)[:, None] * freq
    cos, sin = jnp.cos(ang), jnp.sin(ang)
    return cos, jnp.where(d < half, -sin, 0.0), jnp.where((d >= half) & (d < ROPE_DIM), sin, 0.0)


SMALL_NAMES = (("pre_mix_norm", 1024), ("gla_gate_bias", 256), ("gla_out_norm", 128), ("swa_sinks", 8),
               ("post_mix_norm", 1024), ("pre_ffn_norm", 1024), ("conv_b", 5632), ("post_ffn_norm", 1024))


def _pack_small(vals, loss):
    parts = [vals[n].reshape(-1) for n, _ in SMALL_NAMES] + [loss.reshape(1)]
    flat = jnp.concatenate(parts)
    return jnp.pad(flat, (0, SMALL_ROWS * LANES - flat.shape[0])).reshape(SMALL_ROWS, LANES)


def _unpack_small(packed):
    flat = packed.reshape(-1)
    out, off = {}, 0
    for n, size in SMALL_NAMES:
        out[n] = flat[off:off + size].reshape(1, size)
        off += size
    return out, flat[off]


def _local_step(x, positions, target, w, small):
    rc, rsa, rsb = _rope_tables(positions)
    wp = w["wp"]
    gup = jnp.pad(w["gla_gate_up"], ((0, LANES - 16), (0, 0)))
    g1, g2, g3, g4 = (small[n] for n in ("pre_mix_norm", "post_mix_norm", "pre_ffn_norm", "post_ffn_norm"))
    gbias, gnorm, cb = small["gla_gate_bias"], small["gla_out_norm"], small["conv_b"]
    sinks = small["swa_sinks"].reshape(-1)
    cw = w["conv_w"]

    h1, q, k, v, la, gg, sq, kd, vd, glr = _proj_fwd(x, g1, wp, gup, gbias, rc, rsa, rsb)
    og, s_all = _gla_fwd(q, k, v, la)
    osw = _swa_fwd(sq, kd, vd, sinks)
    x1, cat, mix = _mix_out_fwd(x, og, gg, osw, gnorm, w["w_out"], g2)
    h2, up, act, c1, c2, y, dx2, loss = _ffn_fwd(x1, g3, w["w_up4"], cw, cb, w["w_down"], g4, target)

    dy, dup, dx1, dg4, dg3, dcb, dcw = _ffn_bwd(dx2, y, g4, up, c1, c2, cw, w["w_down"], w["w_up4"], x1, g3)
    dmix, dog, dgg, dosw, dg2, dgn = _mix_out_bwd(dx1, mix, g2, w["w_out"], og, gg, gnorm)
    dsq, dkd, dvd, dsink = _swa_bwd(sq, kd, vd, sinks, dosw)
    dq, dk, dv, dz = _gla_bwd(q, k, v, la, s_all, dog)
    dx, dproj, dg1, dgup, dgb = _proj_bwd(x, g1, wp, gup, glr, dq, dk, dv, dgg, dsq, dkd, dvd, dz, rc, rsa, rsb, dx1)

    grads = {
        "wp": _matmul_tn(h1, dproj, IN_WIDTH_PAD, "grad_w_in"),
        "w_out": _matmul_tn(cat, dmix, D_MODEL, "grad_w_out"),
        "w_up4": _matmul_tn(h2, dup, SHARD_FF, "grad_w_up", column_blocks_major=True),
        "w_down": _matmul_tn(act, dy, D_MODEL, "grad_w_down"),
        "gla_gate_up": dgup[0:16],
        "conv_w": dcw,
    }
    small_grads = {
        "pre_mix_norm": dg1, "gla_gate_bias": dgb, "gla_out_norm": dgn, "swa_sinks": jnp.sum(dsink[:, 0].reshape(SWA_HEADS, SWA_BLOCK), axis=1).reshape(1, SWA_HEADS),
        "post_mix_norm": dg2, "pre_ffn_norm": dg3, "conv_b": dcb, "post_ffn_norm": dg4,
    }
    return loss[0, 0], dx, grads, small_grads


ADAM_ROWS = {"w_in": 256, "w_out": 256, "w_up": 256, "w_down": 176}
WEIGHT_ORDER = ("pre_mix_norm", "w_in", "gla_gate_up", "gla_gate_bias", "gla_out_norm", "swa_sinks", "w_out",
                "post_mix_norm", "pre_ffn_norm", "w_up", "conv_w", "conv_b", "w_down", "post_ffn_norm")
TINY_ROWS = 152


def _pack_tiny(vals):
    flat = jnp.concatenate([vals[n].reshape(-1) for n in TINY_NAMES])
    return jnp.pad(flat, (0, TINY_ROWS * LANES - flat.shape[0])).reshape(TINY_ROWS, LANES)


TINY_NAMES = tuple(n for n, _ in SMALL_NAMES) + ("gla_gate_up", "conv_w")
TINY_SHAPES = {**{n: (1, s) for n, s in SMALL_NAMES}, "gla_gate_up": (16, 64), "conv_w": (3, 1408)}


def _unpack_tiny(packed):
    flat = packed.reshape(-1)
    out, off = {}, 0
    for n in TINY_NAMES:
        shape = TINY_SHAPES[n]
        size = shape[0] * shape[1]
        out[n] = flat[off:off + size].reshape(shape)
        off += size
    return out


def kernel(x, positions, pre_mix_norm, w_in, gla_gate_up, gla_gate_bias, gla_out_norm, swa_sinks, w_out, post_mix_norm, pre_ffn_norm, w_up, conv_w, conv_b, w_down, post_ffn_norm, loss_target, m_pre_mix_norm, m_w_in, m_gla_gate_up, m_gla_gate_bias, m_gla_out_norm, m_swa_sinks, m_w_out, m_post_mix_norm, m_pre_ffn_norm, m_w_up, m_conv_w, m_conv_b, m_w_down, m_post_ffn_norm, v_pre_mix_norm, v_w_in, v_gla_gate_up, v_gla_gate_bias, v_gla_out_norm, v_swa_sinks, v_w_out, v_post_mix_norm, v_pre_ffn_norm, v_w_up, v_conv_w, v_conv_b, v_w_down, v_post_ffn_norm):
    weights = dict(pre_mix_norm=pre_mix_norm, w_in=w_in, gla_gate_up=gla_gate_up, gla_gate_bias=gla_gate_bias,
                   gla_out_norm=gla_out_norm, swa_sinks=swa_sinks, w_out=w_out, post_mix_norm=post_mix_norm,
                   pre_ffn_norm=pre_ffn_norm, w_up=w_up, conv_w=conv_w, conv_b=conv_b, w_down=w_down,
                   post_ffn_norm=post_ffn_norm)
    mom = dict(pre_mix_norm=m_pre_mix_norm, w_in=m_w_in, gla_gate_up=m_gla_gate_up, gla_gate_bias=m_gla_gate_bias,
               gla_out_norm=m_gla_out_norm, swa_sinks=m_swa_sinks, w_out=m_w_out, post_mix_norm=m_post_mix_norm,
               pre_ffn_norm=m_pre_ffn_norm, w_up=m_w_up, conv_w=m_conv_w, conv_b=m_conv_b, w_down=m_w_down,
               post_ffn_norm=m_post_ffn_norm)
    var = dict(pre_mix_norm=v_pre_mix_norm, w_in=v_w_in, gla_gate_up=v_gla_gate_up, gla_gate_bias=v_gla_gate_bias,
               gla_out_norm=v_gla_out_norm, swa_sinks=v_swa_sinks, w_out=v_w_out, post_mix_norm=v_post_mix_norm,
               pre_ffn_norm=v_pre_ffn_norm, w_up=v_w_up, conv_w=v_conv_w, conv_b=v_conv_b, w_down=v_w_down,
               post_ffn_norm=v_post_ffn_norm)
    weights, mom, var = ({n: a[0] if a.ndim == 3 else a for n, a in d.items()} for d in (weights, mom, var))

    win4, wout4, wup4, wdown4, small4 = _allgather_shards(
        [weights[n].astype(BF16) for n in BIG_NAMES] + [_pack_small_weights(weights["gla_gate_up"], weights["conv_w"])])
    gate_full, convw_full = _unpack_small_weights(small4)
    full = {
        "wp": _permute_w_in(jnp.transpose(win4, (1, 0, 2)).reshape(D_MODEL, IN_WIDTH)),
        "w_out": wout4.reshape(D_MODEL, D_MODEL), "w_up4": wup4, "w_down": wdown4.reshape(D_FF, D_MODEL),
        "gla_gate_up": gate_full, "conv_w": convw_full,
    }
    small = {n: weights[n] for n, _ in SMALL_NAMES}
    loss, dx, grads, small_grads = _local_step(x[0], positions[0], loss_target[0], full, small)

    per_shard = [
        jnp.transpose(_unpermute_w_in(grads["wp"]).reshape(D_MODEL, N_SHARD, IN_WIDTH // N_SHARD), (1, 0, 2)),
        grads["w_out"].reshape(N_SHARD, D_MODEL // N_SHARD, D_MODEL), grads["w_up4"],
        grads["w_down"].reshape(N_SHARD, D_FF // N_SHARD, D_MODEL),
        _pack_small_grads(grads["gla_gate_up"], grads["conv_w"]),
    ]
    labels = BIG_NAMES + ("small",)
    core = lax.axis_index("c").astype(jnp.int32).reshape(1)
    got = _rs_pair_swap(per_shard, PIECE_ROWS)
    partial = [_rs_add_pair(a, g, core, rows, "rs_add_pair_" + n) for a, g, rows, n in zip(per_shard, got, ADD_ROWS, labels)]
    landed = _rs_chip_scatter(partial)
    halves = [_rs_sum_chips(p, rows, "rs_sum_chips_" + n) for p, rows, n in zip(landed, ADD_ROWS, labels)]
    reduced = _rs_pair_share(halves, PIECE_ROWS)
    g_small, loss_sum = _unpack_small(_allreduce_small(_pack_small(small_grads, loss)))
    g_gate, g_convw = _unpack_small_grads(reduced[4])
    g_all = {**g_small, **dict(zip(BIG_NAMES, reduced[:4])), "gla_gate_up": g_gate, "conv_w": g_convw}

    delta, new_m, new_v = {}, {}, {}
    for n in BIG_NAMES:
        delta[n], new_m[n], new_v[n] = _adamw(weights[n], g_all[n], mom[n], var[n], ADAM_ROWS[n], "adamw_" + n)
    tiny = _adamw(*(_pack_tiny({n: d[n] for n in TINY_NAMES}) for d in (weights, g_all, mom, var)), TINY_ROWS, "adamw_small")
    for res, packed in zip((delta, new_m, new_v), tiny):
        res.update(_unpack_tiny(packed))

    def lead(n, a):
        return a[None] if n in MATRIX_NAMES else a

    outs = [loss_sum, dx[None]]
    for d in (g_all, delta, new_m, new_v):
        outs.extend(lead(n, d[n]) for n in WEIGHT_ORDER)
    return tuple(outs)
```

```python
import functools

import jax
import jax.numpy as jnp
from jax import lax
from jax.experimental import pallas as pl
from jax.experimental.pallas import tpu as pltpu

F32 = jnp.float32
BF16 = jnp.bfloat16
MESH = pl.DeviceIdType.MESH

D_MODEL = 1024
GLA_HEADS = 4
GLA_DK = 64
GLA_DV = 128
GLA_TAU = 16.0
GLA_CHUNK = 64
SWA_HEADS = 8
SWA_HD = 64
SWA_BLOCK = 128
ROPE_THETA = 500000.0
ROPE_DIM = 16
D_FF = 2816
EPS = 1e-6
GLA_QK = 256
GLA_V = 512
SWA_Q = 512
SWA_KV = 128
IN_WIDTH = 2320
IN_WIDTH_PAD = 2432
N_SHARD = 4

ADAM_LR = 0.001
ADAM_B1 = 0.9
ADAM_B2 = 0.999
ADAM_EPS = 1e-08
ADAM_WD = 0.01
ADAM_STEP = 10

LANES = 128
VMEM_LIMIT = 56 * 1024 * 1024
TM = 256
SHARD_FF = 2 * D_FF // N_SHARD
FF_PIECES = ((0, 512), (512, 512), (1024, 384))
GLA_BLOCK = 256

SMALL_ROWS = 80


def _params(**kw):
    return pltpu.CompilerParams(vmem_limit_bytes=VMEM_LIMIT, **kw)


def _mm(a, b):
    return lax.dot_general(a.astype(BF16), b.astype(BF16), (((1,), (0,)), ((), ())), preferred_element_type=F32)


def _mm_nt(a, b):
    return lax.dot_general(a.astype(BF16), b.astype(BF16), (((1,), (1,)), ((), ())), preferred_element_type=F32)


def _mm_tn(a, b):
    return lax.dot_general(a.astype(BF16), b.astype(BF16), (((0,), (0,)), ((), ())), preferred_element_type=F32)


def _mm_f32(a, b):
    return lax.dot_general(a, b, (((1,), (0,)), ((), ())), preferred_element_type=F32, precision=lax.Precision.HIGHEST)


def _iota(shape, dim):
    return lax.broadcasted_iota(jnp.int32, shape, dim)


def _sigmoid(x):
    return 1.0 / (1.0 + jnp.exp(-x))


def _gelu_parts(x):
    c = 0.7978845608028654
    x2 = x * x
    t = jnp.tanh(c * (x + 0.044715 * (x2 * x)))
    cdf = 0.5 * (1.0 + t)
    dcdf = 0.5 * (1.0 - t * t) * c * (1.0 + 3.0 * 0.044715 * x2)
    return x * cdf, cdf + x * dcdf


def _rms_bwd(v, r, g, dout):
    gd = g * dout
    return r * gd - v * (r * r * r) * jnp.mean(v * gd, axis=-1, keepdims=True)


def _row_spec(tm, cols):
    return pl.BlockSpec((tm, cols), lambda i: (i, 0))


def _const_spec(shape):
    return pl.BlockSpec(shape, lambda i: (0,) * len(shape))


def _any_spec():
    return pl.BlockSpec(memory_space=pl.ANY)


def _load_once(src_hbm, dst_vmem, sem):
    @pl.when(pl.program_id(0) == 0)
    def _():
        cp = pltpu.make_async_copy(src_hbm, dst_vmem, sem)
        cp.start()
        cp.wait()


def _rotate(v, rc, rsa, rsb):
    return v * rc + pltpu.roll(v, 120, 1) * rsa + pltpu.roll(v, 8, 1) * rsb


def _rotate_bwd(dv, rc, rsa, rsb):
    return dv * rc + pltpu.roll(dv * rsa, 8, 1) + pltpu.roll(dv * rsb, 120, 1)


def _proj_fwd(x, g1, wp, gup, gbias, rc, rsa, rsb):
    T = x.shape[0]

    def body(x_ref, g1_ref, wp_hbm, gup_ref, gb_ref, rc_ref, rsa_ref, rsb_ref,
             h1_ref, q_ref, k_ref, v_ref, la_ref, gg_ref, sq_ref, kd_ref, vd_ref, glr_ref, wp_v, sem):
        _load_once(wp_hbm, wp_v, sem)
        xt = x_ref[...]
        r = lax.rsqrt(jnp.mean(xt * xt, axis=-1, keepdims=True) + EPS)
        h = (xt * r * g1_ref[...]).astype(BF16)
        h1_ref[...] = h
        q_ref[...] = _mm(h, wp_v[:, 0:256])
        k_ref[...] = _mm(h, wp_v[:, 256:512])
        v_ref[...] = _mm(h, wp_v[:, 512:1024])
        gg_ref[...] = _mm(h, wp_v[:, 1024:1536])
        glr = _mm(h, wp_v[:, 2304:2432]).astype(BF16)
        glr_ref[...] = glr
        z = _mm(glr, gup_ref[...]) + gb_ref[...]
        la_ref[...] = (jnp.minimum(z, 0.0) - jnp.log1p(jnp.exp(-jnp.abs(z)))) * (1.0 / GLA_TAU)
        rc_, rsa_, rsb_ = rc_ref[...], rsa_ref[...], rsb_ref[...]
        for s in range(4):
            qs = _mm(h, wp_v[:, 1536 + 128 * s:1664 + 128 * s])
            sq_ref[:, 128 * s:128 * s + 128] = (_rotate(qs, rc_, rsa_, rsb_) * 0.125).astype(BF16)
        lane = _iota((TM, LANES), 1)
        first = lane < 64
        kr = _rotate(_mm(h, wp_v[:, 2048:2176]), rc_, rsa_, rsb_)
        krr = pltpu.roll(kr, 64, 1)
        kd_ref[:, 0:128] = jnp.where(first, kr, krr).astype(BF16)
        kd_ref[:, 128:256] = jnp.where(first, krr, kr).astype(BF16)
        vr = _mm(h, wp_v[:, 2176:2304])
        vrr = pltpu.roll(vr, 64, 1)
        vd_ref[:, 0:128] = jnp.where(first, vr, vrr).astype(BF16)
        vd_ref[:, 128:256] = jnp.where(first, vrr, vr).astype(BF16)

    outs = [
        jax.ShapeDtypeStruct((T, D_MODEL), BF16),
        jax.ShapeDtypeStruct((T, GLA_QK), F32),
        jax.ShapeDtypeStruct((T, GLA_QK), F32),
        jax.ShapeDtypeStruct((T, GLA_V), F32),
        jax.ShapeDtypeStruct((T, GLA_QK), F32),
        jax.ShapeDtypeStruct((T, GLA_V), F32),
        jax.ShapeDtypeStruct((T, SWA_Q), BF16),
        jax.ShapeDtypeStruct((T, 256), BF16),
        jax.ShapeDtypeStruct((T, 256), BF16),
        jax.ShapeDtypeStruct((T, LANES), BF16),
    ]
    return pl.pallas_call(
        body, name="proj_fwd", grid=(T // TM,), out_shape=outs,
        in_specs=[_row_spec(TM, D_MODEL), _const_spec((1, D_MODEL)), _any_spec(), _const_spec((LANES, GLA_QK)),
                  _const_spec((1, GLA_QK)), _row_spec(TM, LANES), _row_spec(TM, LANES), _row_spec(TM, LANES)],
        out_specs=[_row_spec(TM, o.shape[1]) for o in outs],
        scratch_shapes=[pltpu.VMEM((D_MODEL, IN_WIDTH_PAD), BF16), pltpu.SemaphoreType.DMA],
        compiler_params=_params(),
    )(x, g1, wp, gup, gbias, rc, rsa, rsb)


GLA_NB = GLA_BLOCK // GLA_CHUNK


def _gla_masks():
    n = GLA_BLOCK
    lane = _iota((n, LANES), 1)
    lane_masks = [(lane < 64).astype(F32), (lane >= 64).astype(F32)]
    row, col = _iota((n, n), 0), _iota((n, n), 1)
    same_chunk = (row >> 6) == (col >> 6)
    blk = ((_iota((256, LANES), 0) >> 7) == (_iota((256, LANES), 1) >> 6)).astype(F32)
    return lane_masks, same_chunk & (col <= row), same_chunk & (col >= row), blk


def _chunk_rows(vals):
    return jnp.concatenate([jnp.broadcast_to(v, (GLA_CHUNK, LANES)) for v in vals], axis=0)


def _gla_block_terms(q_ref, k_ref, b_ref, p):
    C = GLA_CHUNK
    cols = slice(LANES * p, LANES * p + LANES)
    bc = b_ref[:, cols]
    bl_rows = [b_ref[C * c + C - 1:C * c + C, cols] for c in range(GLA_NB)]
    bl = _chunk_rows(bl_rows)
    bm = _chunk_rows([b_ref[C * c + C // 2 - 1:C * c + C // 2, cols] for c in range(GLA_NB)])
    qs = q_ref[:, cols] * 0.125
    kk = k_ref[:, cols]
    eb = jnp.exp(bc)
    ekl = jnp.exp(bl - bc)
    eqm = jnp.exp(bc - bm)
    ekm = jnp.exp(bm - bc)
    return qs, kk, eb, ekl, eqm, ekm, [jnp.exp(r) for r in bl_rows]


def _block_cumsum(la, mask):
    return _mm_f32(mask.astype(F32), la)


def _gla_fwd(q, k, v, la):
    T = q.shape[0]
    NB = GLA_BLOCK // GLA_CHUNK
    C = GLA_CHUNK

    def body(q_ref, k_ref, v_ref, la_ref, o_ref, s_ref, st_ref, b_ref):
        @pl.when(pl.program_id(0) == 0)
        def _():
            st_ref[...] = jnp.zeros_like(st_ref)

        lane_masks, causal, _, blk = _gla_masks()
        b_ref[...] = _block_cumsum(la_ref[...], causal)
        for p in range(2):
            qs, kk, eb, ekl, eqm, ekm, gam = _gla_block_terms(q_ref, k_ref, b_ref, p)
            qh, kh, qm, km = qs * eb, kk * ekl, qs * eqm, kk * ekm
            vp = v_ref[:, 256 * p:256 * p + 256]
            intra = []
            for j in range(2):
                a = jnp.where(causal, _mm_nt(qm * lane_masks[j], km), 0.0)
                intra.append(_mm(a, vp[:, 128 * j:128 * j + 128]))
            kv = [blk * _mm_tn(vp[C * c:C * c + C], kh[C * c:C * c + C]) for c in range(NB)]
            st = st_ref[p]
            inter = []
            for c in range(NB):
                s_ref[c, p] = st
                inter.append(_mm_nt(qh[C * c:C * c + C], st))
                st = st * gam[c] + kv[c]
            st_ref[p] = st
            o_ref[:, 256 * p:256 * p + 256] = jnp.concatenate(inter, axis=0) + jnp.concatenate(intra, axis=1)

    return pl.pallas_call(
        body, name="gla_fwd", grid=(T // GLA_BLOCK,),
        out_shape=[jax.ShapeDtypeStruct((T, GLA_V), F32), jax.ShapeDtypeStruct((T // C, 2, 256, LANES), F32)],
        in_specs=[_row_spec(GLA_BLOCK, GLA_QK), _row_spec(GLA_BLOCK, GLA_QK), _row_spec(GLA_BLOCK, GLA_V),
                  _row_spec(GLA_BLOCK, GLA_QK)],
        out_specs=[_row_spec(GLA_BLOCK, GLA_V), pl.BlockSpec((NB, 2, 256, LANES), lambda i: (i, 0, 0, 0))],
        scratch_shapes=[pltpu.VMEM((2, 256, LANES), F32), pltpu.VMEM((GLA_BLOCK, GLA_QK), F32)],
        compiler_params=_params(),
    )(q, k, v, la)


def _gla_bwd(q, k, v, la, s_all, do):
    T = q.shape[0]
    NB = GLA_BLOCK // GLA_CHUNK
    C = GLA_CHUNK
    nblk = T // GLA_BLOCK

    def body(q_ref, k_ref, v_ref, la_ref, s_ref, do_ref, dq_ref, dk_ref, dv_ref, dz_ref, dst_ref, b_ref):
        @pl.when(pl.program_id(0) == 0)
        def _():
            dst_ref[...] = jnp.zeros_like(dst_ref)

        lane_masks, causal, anti_causal, blk = _gla_masks()
        b_ref[...] = _block_cumsum(la_ref[...], causal)
        for p in range(2):
            cols = slice(LANES * p, LANES * p + LANES)
            qs, kk, eb, ekl, eqm, ekm, gam = _gla_block_terms(q_ref, k_ref, b_ref, p)
            qh, kh, qm, km = qs * eb, kk * ekl, qs * eqm, kk * ekm
            vp = v_ref[:, 256 * p:256 * p + 256]
            dop = do_ref[:, 256 * p:256 * p + 256]
            dqm = jnp.zeros((GLA_BLOCK, LANES), F32)
            dkm = jnp.zeros((GLA_BLOCK, LANES), F32)
            dv_intra = []
            for j in range(2):
                hs = slice(128 * j, 128 * j + 128)
                a = jnp.where(causal, _mm_nt(qm * lane_masks[j], km), 0.0)
                da = jnp.where(causal, _mm_nt(dop[:, hs], vp[:, hs]), 0.0)
                dv_intra.append(_mm_tn(a, dop[:, hs]))
                dqm = dqm + lane_masks[j] * _mm(da, km)
                dkm = dkm + lane_masks[j] * _mm_tn(da, qm)
            grow = [blk * _mm_tn(dop[C * c:C * c + C], qh[C * c:C * c + C]) for c in range(NB)]
            dst = dst_ref[p]
            dst_after = [None] * NB
            for c in reversed(range(NB)):
                dst_after[c] = dst
                dst = dst * gam[c] + grow[c]
            dst_ref[p] = dst
            dqh, dkh, dv_state, extra = [], [], [], []
            for c in range(NB):
                rows = slice(C * c, C * c + C)
                st = s_ref[c, p]
                dqh.append(_mm(dop[rows], st))
                dkh.append(_mm(vp[rows], dst_after[c]))
                dv_state.append(_mm_nt(kh[rows], dst_after[c]))
                extra.append(jnp.sum(dkh[c] * kh[rows], axis=0, keepdims=True)
                             + jnp.sum(st * dst_after[c], axis=0, keepdims=True) * gam[c])
            dqs = jnp.concatenate(dqh, axis=0) * eb + dqm * eqm
            dk = jnp.concatenate(dkh, axis=0) * ekl + dkm * ekm
            dg = _mm_f32(anti_causal.astype(F32), dqs * qs - dk * kk) + _chunk_rows(extra)
            dq_ref[:, cols] = dqs * 0.125
            dk_ref[:, cols] = dk
            dz_ref[:, cols] = dg * (1.0 - jnp.exp(GLA_TAU * la_ref[:, cols])) * (1.0 / GLA_TAU)
            dv_ref[:, 256 * p:256 * p + 256] = jnp.concatenate(dv_state, axis=0) + jnp.concatenate(dv_intra, axis=1)

    rev = lambda i: (nblk - 1 - i, 0)
    rspec = lambda cols: pl.BlockSpec((GLA_BLOCK, cols), rev)
    return pl.pallas_call(
        body, name="gla_bwd", grid=(nblk,),
        out_shape=[jax.ShapeDtypeStruct((T, GLA_QK), F32), jax.ShapeDtypeStruct((T, GLA_QK), F32),
                   jax.ShapeDtypeStruct((T, GLA_V), F32), jax.ShapeDtypeStruct((T, GLA_QK), F32)],
        in_specs=[rspec(GLA_QK), rspec(GLA_QK), rspec(GLA_V), rspec(GLA_QK),
                  pl.BlockSpec((NB, 2, 256, LANES), lambda i: (nblk - 1 - i, 0, 0, 0)), rspec(GLA_V)],
        out_specs=[rspec(GLA_QK), rspec(GLA_QK), rspec(GLA_V), rspec(GLA_QK)],
        scratch_shapes=[pltpu.VMEM((2, 256, LANES), F32), pltpu.VMEM((GLA_BLOCK, GLA_QK), F32)],
        compiler_params=_params(),
    )(q, k, v, la, s_all, do)


SWA_GROUP = 4


def _swa_stack(ref, g, first):
    parts = []
    for j in range(SWA_GROUP):
        m = 2 * g + j // 2
        pair = ref[:, 128 * m:128 * m + 128]
        zero = jnp.zeros_like(pair)
        parts.append(jnp.where(first, pair, zero) if j % 2 == 0 else jnp.where(first, zero, pair))
    return jnp.concatenate(parts, axis=0)


def _swa_unstack(rows, mm, first):
    W = SWA_BLOCK
    return jnp.where(first, rows[W * 2 * mm:W * (2 * mm + 1)], rows[W * (2 * mm + 1):W * (2 * mm + 2)])


def _swa_probs(qs, kp, kc, vp, vc, i, g, sink_ref, first4):
    W = SWA_BLOCK
    R = SWA_GROUP * W
    r, c = _iota((R, W), 0) & (W - 1), _iota((R, W), 1)
    neg = -1e30
    s_p = jnp.where((c > r) & (i > 0), _mm_nt(qs, kp), neg)
    s_c = jnp.where(c <= r, _mm_nt(qs, kc), neg)
    head = _iota((R, 1), 0) >> 7
    sink = jnp.where(head == 0, sink_ref[4 * g], jnp.where(head == 1, sink_ref[4 * g + 1],
                                                           jnp.where(head == 2, sink_ref[4 * g + 2], sink_ref[4 * g + 3])))
    m = jnp.maximum(jnp.max(jnp.maximum(s_p, s_c), axis=-1, keepdims=True), sink)
    p_p = jnp.exp(s_p - m)
    p_c = jnp.exp(s_c - m)
    p_s = jnp.exp(sink - m)
    one = jnp.ones((W, LANES), BF16)
    first = _iota((W, LANES), 1) < 64
    acc = _mm(p_p, jnp.where(first, vp, one)) + _mm(p_c, jnp.where(first, vc, one))
    rolled = pltpu.roll(acc, 64, 1)
    denom = jnp.where(first4, rolled, acc) + p_s
    return p_p, p_c, p_s, denom, acc, rolled


def _swa_fwd(sq, kd, vd, sinks):
    T = sq.shape[0]
    W = SWA_BLOCK
    prev = lambda i: (jnp.maximum(i - 1, 0), 0)

    def body(sink_ref, q_ref, kp_ref, kc_ref, vp_ref, vc_ref, o_ref):
        i = pl.program_id(0)
        first4 = _iota((SWA_GROUP * W, LANES), 1) < 64
        first = _iota((W, LANES), 1) < 64
        for g in range(2):
            gs = slice(128 * g, 128 * g + 128)
            qs = _swa_stack(q_ref, g, first)
            _, _, _, denom, acc, rolled = _swa_probs(qs, kp_ref[:, gs], kc_ref[:, gs], vp_ref[:, gs], vc_ref[:, gs],
                                                     i, g, sink_ref, first4)
            pv = jnp.where(first4, acc, rolled)
            o = pv / denom
            for mm in range(2):
                m = 2 * g + mm
                o_ref[:, 128 * m:128 * m + 128] = _swa_unstack(o, mm, first).astype(BF16)

    return pl.pallas_call(
        body, name="swa_fwd", grid=(T // W,), out_shape=jax.ShapeDtypeStruct((T, SWA_Q), BF16),
        in_specs=[pl.BlockSpec(memory_space=pltpu.SMEM), _row_spec(W, SWA_Q), pl.BlockSpec((W, 256), prev),
                  _row_spec(W, 256), pl.BlockSpec((W, 256), prev), _row_spec(W, 256)],
        out_specs=_row_spec(W, SWA_Q),
        compiler_params=_params(),
    )(sinks, sq, kd, kd, vd, vd)


def _swa_bwd(sq, kd, vd, sinks, do):
    T = sq.shape[0]
    W = SWA_BLOCK
    n = T // W
    cur = lambda i: (jnp.minimum(i, n - 1), 0)
    prev = lambda i: (jnp.clip(i - 1, 0, n - 1), 0)

    def body(sink_ref, q_ref, kp_ref, kc_ref, vp_ref, vc_ref, do_ref, dq_ref, dk_ref, dv_ref, ds_ref, ck_ref, cv_ref):
        i = pl.program_id(0)

        @pl.when(i == 0)
        def _():
            ds_ref[...] = jnp.zeros_like(ds_ref)
            ck_ref[...] = jnp.zeros_like(ck_ref)
            cv_ref[...] = jnp.zeros_like(cv_ref)

        @pl.when(i < n)
        def _():
            first4 = _iota((SWA_GROUP * W, LANES), 1) < 64
            first = _iota((W, LANES), 1) < 64
            for g in range(2):
                gs = slice(128 * g, 128 * g + 128)
                kp, kc, vp, vc = kp_ref[:, gs], kc_ref[:, gs], vp_ref[:, gs], vc_ref[:, gs]
                qs = _swa_stack(q_ref, g, first)
                dos = _swa_stack(do_ref, g, first)
                p_p, p_c, p_s, denom, _, _ = _swa_probs(qs, kp, kc, vp, vc, i, g, sink_ref, first4)
                inv = 1.0 / denom
                p_p, p_c = p_p * inv, p_c * inv
                dp_p = _mm_nt(dos, vp)
                dp_c = _mm_nt(dos, vc)
                delta = jnp.sum(p_p * dp_p + p_c * dp_c, axis=-1, keepdims=True)
                ds_p = p_p * (dp_p - delta)
                ds_c = p_c * (dp_c - delta)
                rows = slice(SWA_GROUP * W * g, SWA_GROUP * W * (g + 1))
                ds_ref[rows, :] = ds_ref[rows, :] - (p_s * delta) * inv
                dq = (_mm(ds_p, kp) + _mm(ds_c, kc)) * 0.125
                for mm in range(2):
                    m = 2 * g + mm
                    dq_ref[:, 128 * m:128 * m + 128] = _swa_unstack(dq, mm, first)
                dk_ref[:, gs] = ck_ref[:, gs] + _mm_tn(ds_p, qs)
                dv_ref[:, gs] = cv_ref[:, gs] + _mm_tn(p_p, dos)
                ck_ref[:, gs] = _mm_tn(ds_c, qs)
                cv_ref[:, gs] = _mm_tn(p_c, dos)

        @pl.when(i == n)
        def _():
            dk_ref[...] = ck_ref[...]
            dv_ref[...] = cv_ref[...]

    return pl.pallas_call(
        body, name="swa_bwd", grid=(n + 1,),
        out_shape=[jax.ShapeDtypeStruct((T, SWA_Q), F32), jax.ShapeDtypeStruct((T, 256), F32),
                   jax.ShapeDtypeStruct((T, 256), F32), jax.ShapeDtypeStruct((SWA_HEADS * W, LANES), F32)],
        in_specs=[pl.BlockSpec(memory_space=pltpu.SMEM), pl.BlockSpec((W, SWA_Q), cur), pl.BlockSpec((W, 256), prev),
                  pl.BlockSpec((W, 256), cur), pl.BlockSpec((W, 256), prev), pl.BlockSpec((W, 256), cur),
                  pl.BlockSpec((W, SWA_Q), cur)],
        out_specs=[pl.BlockSpec((W, SWA_Q), cur), pl.BlockSpec((W, 256), prev), pl.BlockSpec((W, 256), prev),
                   _const_spec((SWA_HEADS * W, LANES))],
        scratch_shapes=[pltpu.VMEM((W, 256), F32), pltpu.VMEM((W, 256), F32)],
        compiler_params=_params(),
    )(sinks, sq, kd, kd, vd, vd, do)


def _mix_out_fwd(x, og, gg, osw, gnorm, wout, g2):
    T = x.shape[0]

    def body(x_ref, og_ref, gg_ref, osw_ref, gn_ref, wout_ref, g2_ref, x1_ref, cat_ref, mix_ref):
        gn = gn_ref[...]
        for j in range(GLA_HEADS):
            hs = slice(128 * j, 128 * j + 128)
            o = og_ref[:, hs]
            r = lax.rsqrt(jnp.mean(o * o, axis=-1, keepdims=True) + EPS)
            gate = gg_ref[:, hs]
            cat_ref[:, hs] = (o * r * gn * (gate * _sigmoid(gate))).astype(BF16)
        cat_ref[:, GLA_V:] = osw_ref[...]
        mix = _mm(cat_ref[...], wout_ref[...])
        mix_ref[...] = mix
        r2 = lax.rsqrt(jnp.mean(mix * mix, axis=-1, keepdims=True) + EPS)
        x1_ref[...] = x_ref[...] + mix * r2 * g2_ref[...]

    return pl.pallas_call(
        body, name="mix_out_fwd", grid=(T // TM,),
        out_shape=[jax.ShapeDtypeStruct((T, D_MODEL), F32), jax.ShapeDtypeStruct((T, D_MODEL), BF16),
                   jax.ShapeDtypeStruct((T, D_MODEL), F32)],
        in_specs=[_row_spec(TM, D_MODEL), _row_spec(TM, GLA_V), _row_spec(TM, GLA_V), _row_spec(TM, SWA_Q),
                  _const_spec((1, LANES)), _const_spec((D_MODEL, D_MODEL)), _const_spec((1, D_MODEL))],
        out_specs=[_row_spec(TM, D_MODEL), _row_spec(TM, D_MODEL), _row_spec(TM, D_MODEL)],
        compiler_params=_params(),
    )(x, og, gg, osw, gnorm, wout, g2)


HALO = 8


def _rows_before(v, prev1, prev2):
    row = _iota(v.shape, 0)
    m1 = jnp.where(row == 0, prev1, pltpu.roll(v, 1, 0))
    m2 = jnp.where(row == 0, prev2, jnp.where(row == 1, prev1, pltpu.roll(v, 2, 0)))
    return m1, m2


def _rows_after(v, next1, next2):
    n = v.shape[0]
    row = _iota(v.shape, 0)
    p1 = jnp.where(row == n - 1, next1, pltpu.roll(v, n - 1, 0))
    p2 = jnp.where(row == n - 1, next2, jnp.where(row == n - 2, next1, pltpu.roll(v, n - 2, 0)))
    return p1, p2


def _ff_pieces():
    return [(j, off, wd) for j in range(2) for off, wd in FF_PIECES]


def _ffn_fwd(x1, g3, wup, cw, cb, wdown, g4, target):
    T = x1.shape[0]

    def body(x1_ref, g3_ref, wup_hbm, cw_ref, cb_ref, wdn_hbm, g4_ref, tg_ref,
             h2_ref, up_ref, a_ref, c1_ref, c2_ref, y_ref, dx2_ref, loss_ref, wup_v, wdn_v, carry_ref, sems):
        _load_once(wup_hbm, wup_v, sems.at[0])
        _load_once(wdn_hbm, wdn_v, sems.at[1])

        @pl.when(pl.program_id(0) == 0)
        def _():
            carry_ref[...] = jnp.zeros_like(carry_ref)
            loss_ref[...] = jnp.zeros_like(loss_ref)

        x1 = x1_ref[...]
        r3 = lax.rsqrt(jnp.mean(x1 * x1, axis=-1, keepdims=True) + EPS)
        h2 = (x1 * r3 * g3_ref[...]).astype(BF16)
        h2_ref[...] = h2
        y = jnp.zeros((TM, D_MODEL), F32)
        for j, off, wd in _ff_pieces():
            base = SHARD_FF * j + off
            u = []
            for half in range(2):
                cs = slice(D_FF * half + base, D_FF * half + base + wd)
                upb = _mm(h2, wup_v[2 * half + j, :, off:off + wd]).astype(BF16)
                up_ref[:, cs] = upb
                upf = upb.astype(F32)
                m1, m2 = _rows_before(upf, carry_ref[HALO - 1:HALO, cs], carry_ref[HALO - 2:HALO - 1, cs])
                u.append(cb_ref[:, cs] + cw_ref[0:1, cs] * m2 + cw_ref[1:2, cs] * m1 + cw_ref[2:3, cs] * upf)
                carry_ref[:, cs] = upf[TM - HALO:TM, :]
            act, dact = _gelu_parts(u[1])
            a = (act * u[0]).astype(BF16)
            out = slice(base, base + wd)
            a_ref[:, out] = a
            c1_ref[:, out] = act.astype(BF16)
            c2_ref[:, out] = (u[0] * dact).astype(BF16)
            y = y + _mm(a, wdn_v[out, :])
        y_ref[...] = y
        r4 = lax.rsqrt(jnp.mean(y * y, axis=-1, keepdims=True) + EPS)
        err = x1 + y * r4 * g4_ref[...] - tg_ref[...]
        dx2_ref[...] = err * (1.0 / D_MODEL)
        loss_ref[...] = loss_ref[...] + jnp.sum(err * err) * (0.5 / D_MODEL)

    outs = [
        jax.ShapeDtypeStruct((T, D_MODEL), BF16),
        jax.ShapeDtypeStruct((T, 2 * D_FF), BF16),
        jax.ShapeDtypeStruct((T, D_FF), BF16),
        jax.ShapeDtypeStruct((T, D_FF), BF16),
        jax.ShapeDtypeStruct((T, D_FF), BF16),
        jax.ShapeDtypeStruct((T, D_MODEL), F32),
        jax.ShapeDtypeStruct((T, D_MODEL), F32),
        jax.ShapeDtypeStruct((8, LANES), F32),
    ]
    return pl.pallas_call(
        body, name="ffn_fwd", grid=(T // TM,), out_shape=outs,
        in_specs=[_row_spec(TM, D_MODEL), _const_spec((1, D_MODEL)), _any_spec(), _const_spec((3, 2 * D_FF)),
                  _const_spec((1, 2 * D_FF)), _any_spec(), _const_spec((1, D_MODEL)), _row_spec(TM, D_MODEL)],
        out_specs=[_row_spec(TM, D_MODEL), _row_spec(TM, 2 * D_FF), _row_spec(TM, D_FF), _row_spec(TM, D_FF),
                   _row_spec(TM, D_FF), _row_spec(TM, D_MODEL), _row_spec(TM, D_MODEL), _const_spec((8, LANES))],
        scratch_shapes=[pltpu.VMEM((N_SHARD, D_MODEL, SHARD_FF), BF16), pltpu.VMEM((D_FF, D_MODEL), BF16),
                        pltpu.VMEM((HALO, 2 * D_FF), F32), pltpu.SemaphoreType.DMA((2,))],
        compiler_params=_params(),
    )(x1, g3, wup, cw, cb, wdown, g4, target)


def _ffn_bwd(dx2, y, g4, up, c1, c2, cw, wdown, wup, x1, g3):
    T = dx2.shape[0]
    nt = T // TM
    rev = lambda i: (nt - 1 - i, 0)

    def body(dn_ref, y_ref, g4_ref, up_ref, c1_ref, c2_ref, cw_ref, wdn_hbm, wup_hbm, x1_ref, g3_ref,
             dy_ref, dup_ref, dx1_ref, dg4_ref, dg3_ref, dcb_ref, dcw_ref, wup_v, wdn_v, carry_ref, sems):
        _load_once(wup_hbm, wup_v, sems.at[0])
        _load_once(wdn_hbm, wdn_v, sems.at[1])

        @pl.when(pl.program_id(0) == 0)
        def _():
            carry_ref[...] = jnp.zeros_like(carry_ref)
            dg4_ref[...] = jnp.zeros_like(dg4_ref)
            dg3_ref[...] = jnp.zeros_like(dg3_ref)
            dcb_ref[...] = jnp.zeros_like(dcb_ref)
            dcw_ref[...] = jnp.zeros_like(dcw_ref)

        dn = dn_ref[...]
        y = y_ref[...]
        g4v = g4_ref[...]
        r4 = lax.rsqrt(jnp.mean(y * y, axis=-1, keepdims=True) + EPS)
        dg4_ref[...] = dg4_ref[...] + jnp.sum(dn * y * r4, axis=0, keepdims=True)
        dy = _rms_bwd(y, r4, g4v, dn).astype(BF16)
        dy_ref[...] = dy
        dh2 = jnp.zeros((TM, D_MODEL), F32)
        for j, off, wd in _ff_pieces():
            base = SHARD_FF * j + off
            da = _mm_nt(dy, wdn_v[base:base + wd, :])
            for half, coef_ref in enumerate((c1_ref, c2_ref)):
                cs = slice(D_FF * half + base, D_FF * half + base + wd)
                du = da * coef_ref[:, base:base + wd].astype(F32)
                p1, p2 = _rows_after(du, carry_ref[0:1, cs], carry_ref[1:2, cs])
                carry_ref[:, cs] = du[0:HALO, :]
                upf = up_ref[:, cs].astype(F32)
                dcb_ref[:, cs] = dcb_ref[:, cs] + jnp.sum(du, axis=0, keepdims=True)
                dcw_ref[0:1, cs] = dcw_ref[0:1, cs] + jnp.sum(p2 * upf, axis=0, keepdims=True)
                dcw_ref[1:2, cs] = dcw_ref[1:2, cs] + jnp.sum(p1 * upf, axis=0, keepdims=True)
                dcw_ref[2:3, cs] = dcw_ref[2:3, cs] + jnp.sum(du * upf, axis=0, keepdims=True)
                dup = (cw_ref[2:3, cs] * du + cw_ref[1:2, cs] * p1 + cw_ref[0:1, cs] * p2).astype(BF16)
                dup_ref[:, cs] = dup
                dh2 = dh2 + _mm_nt(dup, wup_v[2 * half + j, :, off:off + wd])
        x1 = x1_ref[...]
        r3 = lax.rsqrt(jnp.mean(x1 * x1, axis=-1, keepdims=True) + EPS)
        dg3_ref[...] = dg3_ref[...] + jnp.sum(dh2 * x1 * r3, axis=0, keepdims=True)
        dx1_ref[...] = dn + _rms_bwd(x1, r3, g3_ref[...], dh2)

    outs = [
        jax.ShapeDtypeStruct((T, D_MODEL), BF16),
        jax.ShapeDtypeStruct((T, 2 * D_FF), BF16),
        jax.ShapeDtypeStruct((T, D_MODEL), F32),
        jax.ShapeDtypeStruct((1, D_MODEL), F32),
        jax.ShapeDtypeStruct((1, D_MODEL), F32),
        jax.ShapeDtypeStruct((1, 2 * D_FF), F32),
        jax.ShapeDtypeStruct((3, 2 * D_FF), F32),
    ]
    return pl.pallas_call(
        body, name="ffn_bwd", grid=(nt,), out_shape=outs,
        in_specs=[pl.BlockSpec((TM, D_MODEL), rev), pl.BlockSpec((TM, D_MODEL), rev), _const_spec((1, D_MODEL)),
                  pl.BlockSpec((TM, 2 * D_FF), rev), pl.BlockSpec((TM, D_FF), rev), pl.BlockSpec((TM, D_FF), rev),
                  _const_spec((3, 2 * D_FF)), _any_spec(), _any_spec(), pl.BlockSpec((TM, D_MODEL), rev),
                  _const_spec((1, D_MODEL))],
        out_specs=[pl.BlockSpec((TM, D_MODEL), rev), pl.BlockSpec((TM, 2 * D_FF), rev), pl.BlockSpec((TM, D_MODEL), rev),
                   _const_spec((1, D_MODEL)), _const_spec((1, D_MODEL)), _const_spec((1, 2 * D_FF)),
                   _const_spec((3, 2 * D_FF))],
        scratch_shapes=[pltpu.VMEM((N_SHARD, D_MODEL, SHARD_FF), BF16), pltpu.VMEM((D_FF, D_MODEL), BF16),
                        pltpu.VMEM((HALO, 2 * D_FF), F32), pltpu.SemaphoreType.DMA((2,))],
        compiler_params=_params(),
    )(dx2, y, g4, up, c1, c2, cw, wdown, wup, x1, g3)


def _mix_out_bwd(dx1, mix, g2, wout, og, gg, gnorm):
    T = dx1.shape[0]

    def body(dx1_ref, mix_ref, g2_ref, wout_ref, og_ref, gg_ref, gn_ref,
             dmix_ref, dog_ref, dgg_ref, dosw_ref, dg2_ref, dgn_ref):
        @pl.when(pl.program_id(0) == 0)
        def _():
            dg2_ref[...] = jnp.zeros_like(dg2_ref)
            dgn_ref[...] = jnp.zeros_like(dgn_ref)

        dx1 = dx1_ref[...]
        mix = mix_ref[...]
        r2 = lax.rsqrt(jnp.mean(mix * mix, axis=-1, keepdims=True) + EPS)
        dg2_ref[...] = dg2_ref[...] + jnp.sum(dx1 * mix * r2, axis=0, keepdims=True)
        dmix = _rms_bwd(mix, r2, g2_ref[...], dx1).astype(BF16)
        dmix_ref[...] = dmix
        dcat = _mm_nt(dmix, wout_ref[...])
        dosw_ref[...] = dcat[:, GLA_V:].astype(BF16)
        gn = gn_ref[...]
        dgn = jnp.zeros((1, LANES), F32)
        for j in range(GLA_HEADS):
            hs = slice(128 * j, 128 * j + 128)
            o = og_ref[:, hs]
            r = lax.rsqrt(jnp.mean(o * o, axis=-1, keepdims=True) + EPS)
            gate = gg_ref[:, hs]
            sg = _sigmoid(gate)
            dgated = dcat[:, hs]
            dnorm = dgated * (gate * sg)
            dgg_ref[:, hs] = dgated * (o * r * gn) * (sg * (1.0 + gate * (1.0 - sg)))
            dgn = dgn + jnp.sum(dnorm * o * r, axis=0, keepdims=True)
            dog_ref[:, hs] = _rms_bwd(o, r, gn, dnorm)
        dgn_ref[...] = dgn_ref[...] + dgn

    return pl.pallas_call(
        body, name="mix_out_bwd", grid=(T // TM,),
        out_shape=[jax.ShapeDtypeStruct((T, D_MODEL), BF16), jax.ShapeDtypeStruct((T, GLA_V), F32),
                   jax.ShapeDtypeStruct((T, GLA_V), F32), jax.ShapeDtypeStruct((T, SWA_Q), BF16),
                   jax.ShapeDtypeStruct((1, D_MODEL), F32), jax.ShapeDtypeStruct((1, LANES), F32)],
        in_specs=[_row_spec(TM, D_MODEL), _row_spec(TM, D_MODEL), _const_spec((1, D_MODEL)),
                  _const_spec((D_MODEL, D_MODEL)), _row_spec(TM, GLA_V), _row_spec(TM, GLA_V), _const_spec((1, LANES))],
        out_specs=[_row_spec(TM, D_MODEL), _row_spec(TM, GLA_V), _row_spec(TM, GLA_V), _row_spec(TM, SWA_Q),
                   _const_spec((1, D_MODEL)), _const_spec((1, LANES))],
        compiler_params=_params(),
    )(dx1, mix, g2, wout, og, gg, gnorm)


def _proj_bwd(x, g1, wp, gup, glr, dq, dk, dv, dgg, dsq, dkd, dvd, dz, rc, rsa, rsb, dx1):
    T = x.shape[0]

    def body(x_ref, g1_ref, wp_hbm, gup_ref, glr_ref, dq_ref, dk_ref, dv_ref, dgg_ref, dsq_ref, dkd_ref, dvd_ref,
             dz_ref, rc_ref, rsa_ref, rsb_ref, dx1_ref, dx_ref, dp_ref, dg1_ref, dgup_ref, dgb_ref, wp_v, sem):
        _load_once(wp_hbm, wp_v, sem)

        @pl.when(pl.program_id(0) == 0)
        def _():
            dg1_ref[...] = jnp.zeros_like(dg1_ref)
            dgup_ref[...] = jnp.zeros_like(dgup_ref)
            dgb_ref[...] = jnp.zeros_like(dgb_ref)

        rc_, rsa_, rsb_ = rc_ref[...], rsa_ref[...], rsb_ref[...]
        dp_ref[:, 0:256] = dq_ref[...].astype(BF16)
        dp_ref[:, 256:512] = dk_ref[...].astype(BF16)
        dp_ref[:, 512:1024] = dv_ref[...].astype(BF16)
        dp_ref[:, 1024:1536] = dgg_ref[...].astype(BF16)
        for s in range(4):
            cs = slice(128 * s, 128 * s + 128)
            dp_ref[:, 1536 + 128 * s:1664 + 128 * s] = _rotate_bwd(dsq_ref[:, cs], rc_, rsa_, rsb_).astype(BF16)
        first = _iota((TM, LANES), 1) < 64
        dk0 = dkd_ref[:, 0:128]
        dk1 = dkd_ref[:, 128:256]
        dkr = jnp.where(first, dk0 + pltpu.roll(dk0, 64, 1), dk1 + pltpu.roll(dk1, 64, 1))
        dp_ref[:, 2048:2176] = _rotate_bwd(dkr, rc_, rsa_, rsb_).astype(BF16)
        dv0 = dvd_ref[:, 0:128]
        dv1 = dvd_ref[:, 128:256]
        dp_ref[:, 2176:2304] = jnp.where(first, dv0 + pltpu.roll(dv0, 64, 1), dv1 + pltpu.roll(dv1, 64, 1)).astype(BF16)
        dz = dz_ref[...]
        dzb = dz.astype(BF16)
        dp_ref[:, 2304:2432] = _mm_nt(dzb, gup_ref[...]).astype(BF16)
        dgup_ref[...] = dgup_ref[...] + _mm_tn(glr_ref[...], dzb)
        dgb_ref[...] = dgb_ref[...] + jnp.sum(dz, axis=0, keepdims=True)
        dh1 = _mm_nt(dp_ref[...], wp_v[...])
        xt = x_ref[...]
        r = lax.rsqrt(jnp.mean(xt * xt, axis=-1, keepdims=True) + EPS)
        dg1_ref[...] = dg1_ref[...] + jnp.sum(dh1 * xt * r, axis=0, keepdims=True)
        dx_ref[...] = dx1_ref[...] + _rms_bwd(xt, r, g1_ref[...], dh1)

    row = lambda cols: _row_spec(TM, cols)
    return pl.pallas_call(
        body, name="proj_bwd", grid=(T // TM,),
        out_shape=[jax.ShapeDtypeStruct((T, D_MODEL), F32), jax.ShapeDtypeStruct((T, IN_WIDTH_PAD), BF16),
                   jax.ShapeDtypeStruct((1, D_MODEL), F32), jax.ShapeDtypeStruct((LANES, GLA_QK), F32),
                   jax.ShapeDtypeStruct((1, GLA_QK), F32)],
        in_specs=[row(D_MODEL), _const_spec((1, D_MODEL)), _any_spec(), _const_spec((LANES, GLA_QK)), row(LANES),
                  row(GLA_QK), row(GLA_QK), row(GLA_V), row(GLA_V), row(SWA_Q), row(256), row(256), row(GLA_QK),
                  row(LANES), row(LANES), row(LANES), row(D_MODEL)],
        out_specs=[row(D_MODEL), row(IN_WIDTH_PAD), _const_spec((1, D_MODEL)), _const_spec((LANES, GLA_QK)),
                   _const_spec((1, GLA_QK))],
        scratch_shapes=[pltpu.VMEM((D_MODEL, IN_WIDTH_PAD), BF16), pltpu.SemaphoreType.DMA],
        compiler_params=_params(),
    )(x, g1, wp, gup, glr, dq, dk, dv, dgg, dsq, dkd, dvd, dz, rc, rsa, rsb, dx1)


def _matmul_tn(a, b, tn, name, column_blocks_major=False):
    T, M = a.shape
    N = b.shape[1]
    tk = 512 if T % 512 == 0 else TM
    nk = T // tk
    if column_blocks_major:
        out_shape = jax.ShapeDtypeStruct((N // tn, M, tn), F32)
        out_spec = pl.BlockSpec((None, M, tn), lambda j, kk: (j, 0, 0))
    else:
        out_shape = jax.ShapeDtypeStruct((M, N), F32)
        out_spec = pl.BlockSpec((M, tn), lambda j, kk: (0, j))

    def body(a_ref, b_ref, o_ref):
        kk = pl.program_id(1)

        @pl.when(kk == 0)
        def _():
            o_ref[...] = jnp.zeros_like(o_ref)

        o_ref[...] = o_ref[...] + _mm_tn(a_ref[...], b_ref[...])

    return pl.pallas_call(
        body, name=name, grid=(N // tn, nk), out_shape=out_shape,
        in_specs=[pl.BlockSpec((tk, M), lambda j, kk: (kk, 0)), pl.BlockSpec((tk, tn), lambda j, kk: (kk, j))],
        out_specs=out_spec,
        compiler_params=_params(),
    )(a, b)


def _adamw(w, g, m, v, rows, name):
    R, C = w.shape

    def body(w_ref, g_ref, m_ref, v_ref, d_ref, m2_ref, v2_ref):
        g_ = g_ref[...]
        m2 = ADAM_B1 * m_ref[...] + (1.0 - ADAM_B1) * g_
        v2 = ADAM_B2 * v_ref[...] + (1.0 - ADAM_B2) * (g_ * g_)
        m_hat = m2 / (1.0 - ADAM_B1 ** ADAM_STEP)
        v_hat = v2 / (1.0 - ADAM_B2 ** ADAM_STEP)
        d_ref[...] = -ADAM_LR * (m_hat / (jnp.sqrt(v_hat) + ADAM_EPS) + ADAM_WD * w_ref[...])
        m2_ref[...] = m2
        v2_ref[...] = v2

    spec = pl.BlockSpec((rows, C), lambda i: (i, 0))
    return pl.pallas_call(
        body, name=name, grid=(R // rows,), out_shape=[jax.ShapeDtypeStruct((R, C), F32)] * 3,
        in_specs=[spec] * 4, out_specs=[spec] * 3, compiler_params=_params(),
    )(w, g, m, v)


def _place():
    x, y, c = lax.axis_index("x"), lax.axis_index("y"), lax.axis_index("c")
    chips = [(1 - x, y), (x, 1 - y), (1 - x, 1 - y)]
    return x, y, c, chips


class _staged_copies:
    def __init__(self, srcs, dsts, stage, sems):
        n = len(srcs)
        self.loads = [pltpu.make_async_copy(srcs[k], stage[k], sems.at[k]) for k in range(n)]
        self.stores = [pltpu.make_async_copy(stage[k], dsts[k], sems.at[n + k]) for k in range(n)]

    def load(self):
        for cp in self.loads:
            cp.start()

    def store(self):
        for ld, st in zip(self.loads, self.stores):
            ld.wait()
            st.start()

    def finish(self):
        for cp in self.stores:
            cp.wait()


def _allgather_shards(parts, unit_rows):
    n = len(parts)
    units = [(k, r, unit_rows[k]) for k in range(n) for r in range(0, parts[k].shape[0] // 2, unit_rows[k])]
    nu = len(units)

    def body(*refs):
        ins, outs, stage = refs[:n], refs[n:2 * n], refs[2 * n:3 * n]
        send_sems, recv_sems, local_sems = refs[3 * n:]
        x, y, c, chips = _place()
        sibling = (x, y, 1 - c)
        own = _staged_copies(ins, [o.at[2 * x + y] for o in outs], stage, local_sems)

        def block(i, px, py, half):
            k, r, u = units[i]
            return outs[k].at[2 * px + py, pl.ds(half * (parts[k].shape[0] // 2) + r, u), :]

        def copy(i, j, px, py, half, to, src=None):
            return pltpu.make_async_remote_copy(
                src_ref=block(i, px, py, half) if src is None else src, dst_ref=block(i, px, py, half),
                send_sem=send_sems.at[nu * j + i], recv_sem=recv_sems.at[nu * j + i], device_id=to, device_id_type=MESH)

        own.load()
        first, passed = [], []
        for i, (k, r, u) in enumerate(units):
            for j, chip in enumerate(chips):
                src = ins[k].at[pl.ds(c * (parts[k].shape[0] // 2) + r, u), :]
                first.append(copy(i, j, x, y, c, (*chip, c), src=src))
                first[-1].start()
        own.store()
        for i in range(nu):
            for j, chip in enumerate(chips):
                copy(i, j, *chip, c, (x, y, c)).wait_recv()
                passed.append(copy(i, 3 + j, *chip, c, sibling))
                passed[-1].start()
        for i in range(nu):
            for j, chip in enumerate(chips):
                copy(i, 3 + j, *chip, 1 - c, (x, y, c)).wait_recv()
        for cp in first + passed:
            cp.wait_send()
        own.finish()

    return pl.pallas_call(
        body, name="allgather_shards", out_shape=[jax.ShapeDtypeStruct((N_SHARD,) + p.shape, p.dtype) for p in parts],
        in_specs=[_any_spec()] * n, out_specs=[_any_spec()] * n,
        scratch_shapes=[pltpu.VMEM(p.shape, p.dtype) for p in parts] + [
            pltpu.SemaphoreType.DMA((6 * nu,)), pltpu.SemaphoreType.DMA((6 * nu,)), pltpu.SemaphoreType.DMA((2 * n,))],
        compiler_params=_params(),
    )(*parts)


def _d2d_pieces(rows, piece_rows):
    return [(r, piece_rows) for r in range(0, rows, piece_rows)]


def _rs_pair_swap(arrs, piece_rows, name):
    n = len(arrs)

    def body(*refs):
        ins, outs = refs[:n], refs[n:2 * n]
        send_sems, recv_sems = refs[2 * n:]
        x, y, c, _ = _place()
        sibling = (x, y, 1 - c)
        for k in range(n):
            H = arrs[k].shape[1] // 2
            for s in range(N_SHARD):
                for r, pr in _d2d_pieces(H, piece_rows[k]):
                    pltpu.make_async_remote_copy(
                        src_ref=ins[k].at[s, pl.ds((1 - c) * H + r, pr), :], dst_ref=outs[k].at[s, pl.ds(r, pr), :],
                        send_sem=send_sems.at[k], recv_sem=recv_sems.at[k], device_id=sibling, device_id_type=MESH).start()
        for k in range(n):
            H = arrs[k].shape[1] // 2
            whole = pltpu.make_async_remote_copy(
                src_ref=ins[k].at[:, pl.ds(0, H), :], dst_ref=outs[k], send_sem=send_sems.at[k], recv_sem=recv_sems.at[k],
                device_id=sibling, device_id_type=MESH)
            whole.wait_recv()
            whole.wait_send()

    return pl.pallas_call(
        body, name=name,
        out_shape=[jax.ShapeDtypeStruct((N_SHARD, a.shape[1] // 2, a.shape[2]), F32) for a in arrs],
        in_specs=[_any_spec()] * n, out_specs=[_any_spec()] * n,
        scratch_shapes=[pltpu.SemaphoreType.DMA((n,)), pltpu.SemaphoreType.DMA((n,))],
    )(*arrs)


def _rs_add_pair(a, got, core, rows, name):
    _, H, C = got.shape
    nb = H // rows

    def body(c_ref, a_ref, b_ref, o_ref):
        o_ref[...] = (a_ref[...] + b_ref[...]).astype(BF16)

    spec = pl.BlockSpec((1, rows, C), lambda s, r, c_ref: (s, r, 0))
    return pl.pallas_call(
        body, name=name, out_shape=jax.ShapeDtypeStruct(got.shape, BF16),
        grid_spec=pltpu.PrefetchScalarGridSpec(
            num_scalar_prefetch=1, grid=(N_SHARD, nb),
            in_specs=[pl.BlockSpec((1, rows, C), lambda s, r, c_ref: (s, c_ref[0] * nb + r, 0)), spec], out_specs=spec),
        compiler_params=_params(),
    )(core, a, got)


def _rs_chip_scatter(parts):
    n = len(parts)

    def body(*refs):
        ins, outs, stage = refs[:n], refs[n:2 * n], refs[2 * n:3 * n]
        send_sems, recv_sems, local_sems = refs[3 * n:]
        x, y, c, chips = _place()
        me = 2 * x + y
        own = _staged_copies([i.at[me] for i in ins], [o.at[me] for o in outs], stage, local_sems)
        own.load()
        sends = []
        for k in range(n):
            for j, (px, py) in enumerate(chips):
                sends.append(pltpu.make_async_remote_copy(
                    src_ref=ins[k].at[2 * px + py], dst_ref=outs[k].at[me], send_sem=send_sems.at[3 * k + j],
                    recv_sem=recv_sems.at[3 * k + j], device_id=(px, py, c), device_id_type=MESH))
                sends[-1].start()
        own.store()
        for k in range(n):
            for j, (px, py) in enumerate(chips):
                pltpu.make_async_remote_copy(
                    src_ref=ins[k].at[me], dst_ref=outs[k].at[2 * px + py], send_sem=send_sems.at[3 * k + j],
                    recv_sem=recv_sems.at[3 * k + j], device_id=(px, py, c), device_id_type=MESH).wait_recv()
        for cp in sends:
            cp.wait_send()
        own.finish()

    return pl.pallas_call(
        body, name="rs_chip_scatter", out_shape=[jax.ShapeDtypeStruct(p.shape, p.dtype) for p in parts],
        in_specs=[_any_spec()] * n, out_specs=[_any_spec()] * n,
        scratch_shapes=[pltpu.VMEM(p.shape[1:], p.dtype) for p in parts] + [
            pltpu.SemaphoreType.DMA((3 * n,)), pltpu.SemaphoreType.DMA((3 * n,)), pltpu.SemaphoreType.DMA((2 * n,))],
        compiler_params=_params(),
    )(*parts)


def _rs_sum_chips(parts, rows, name):
    _, H, C = parts.shape

    def body(p_ref, o_ref):
        o_ref[...] = ((p_ref[0].astype(F32) + p_ref[1].astype(F32)) + p_ref[2].astype(F32)) + p_ref[3].astype(F32)

    return pl.pallas_call(
        body, name=name, grid=(H // rows,), out_shape=jax.ShapeDtypeStruct((H, C), F32),
        in_specs=[pl.BlockSpec((N_SHARD, rows, C), lambda r: (0, r, 0))],
        out_specs=pl.BlockSpec((rows, C), lambda r: (r, 0)), compiler_params=_params(),
    )(parts)


def _rs_pair_share(halves, piece_rows):
    n = len(halves)

    def body(*refs):
        ins, outs, stage = refs[:n], refs[n:2 * n], refs[2 * n:3 * n]
        send_sems, recv_sems, local_sems = refs[3 * n:]
        x, y, c, _ = _place()
        sibling = (x, y, 1 - c)
        own = _staged_copies(ins, [outs[k].at[pl.ds(c * halves[k].shape[0], halves[k].shape[0]), :] for k in range(n)],
                             stage, local_sems)
        own.load()
        for k in range(n):
            H = halves[k].shape[0]
            for r, pr in _d2d_pieces(H, piece_rows[k]):
                pltpu.make_async_remote_copy(
                    src_ref=ins[k].at[pl.ds(r, pr), :], dst_ref=outs[k].at[pl.ds(c * H + r, pr), :],
                    send_sem=send_sems.at[k], recv_sem=recv_sems.at[k], device_id=sibling, device_id_type=MESH).start()
        own.store()
        for k in range(n):
            H = halves[k].shape[0]
            whole = pltpu.make_async_remote_copy(
                src_ref=ins[k], dst_ref=outs[k].at[pl.ds((1 - c) * H, H), :], send_sem=send_sems.at[k],
                recv_sem=recv_sems.at[k], device_id=sibling, device_id_type=MESH)
            whole.wait_recv()
            whole.wait_send()
        own.finish()

    return pl.pallas_call(
        body, name="rs_pair_share", out_shape=[jax.ShapeDtypeStruct((2 * h.shape[0], h.shape[1]), F32) for h in halves],
        in_specs=[_any_spec()] * n, out_specs=[_any_spec()] * n,
        scratch_shapes=[pltpu.VMEM(h.shape, F32) for h in halves] + [
            pltpu.SemaphoreType.DMA((n,)), pltpu.SemaphoreType.DMA((n,)), pltpu.SemaphoreType.DMA((2 * n,))],
        compiler_params=_params(),
    )(*halves)


_HBM = pl.BlockSpec(memory_space=pltpu.HBM)
_SEM = pl.BlockSpec(memory_space=pltpu.SEMAPHORE)
_EFFECT = pltpu.SideEffectType.DATAFLOW_SIDE_EFFECTING


def _gather_plan(srcs, lands, x, y, c, chips):
    plan = []
    for k in range(len(srcs)):
        H = srcs[k].shape[0] // 2
        for px, py in chips:
            plan.append((srcs[k].at[pl.ds(c * H, H), :], lands[k].at[2 * x + y, pl.ds(c * H, H), :], (px, py, c),
                         lands[k].at[2 * px + py, pl.ds(c * H, H), :]))
    return plan


def _scatter_plan(srcs, lands, x, y, c, chips):
    plan = []
    for k in range(len(srcs)):
        for px, py in chips:
            plan.append((srcs[k].at[2 * px + py], lands[k].at[2 * x + y], (px, py, c), lands[k].at[2 * px + py]))
    return plan


def _ici_start(srcs, lands, make_plan, name):
    n = len(srcs)
    ncopy = 3 * n

    def body(*refs):
        ins, lnd = refs[:n], refs[n:2 * n]
        send_sems, recv_sems = refs[2 * n], refs[2 * n + 1]
        token = refs[-1]
        x, y, c, chips = _place()
        for i, (src, dst, peer, _) in enumerate(make_plan(ins, lnd, x, y, c, chips)):
            pltpu.make_async_remote_copy(src_ref=src, dst_ref=dst, send_sem=send_sems.at[i], recv_sem=recv_sems.at[i],
                                         device_id=peer, device_id_type=MESH).start()
        token[...] = jnp.zeros_like(token)

    arrays = list(srcs) + list(lands)
    return pl.pallas_call(
        body, name=name,
        out_shape=(pltpu.SemaphoreType.DMA((ncopy,)), pltpu.SemaphoreType.DMA((ncopy,)),
                   *[pltpu.HBM(a.shape, a.dtype) for a in arrays], jax.ShapeDtypeStruct((8, LANES), F32)),
        in_specs=[_HBM] * (2 * n), out_specs=(_SEM, _SEM, *[_HBM] * (2 * n), pl.BlockSpec(memory_space=pltpu.VMEM)),
        input_output_aliases={i: 2 + i for i in range(2 * n)},
        compiler_params=pltpu.CompilerParams(has_side_effects=_EFFECT),
    )(*[pltpu.with_memory_space_constraint(a, pltpu.HBM) for a in arrays])


def _ici_wait(started, after, make_plan, name):
    send_sems, recv_sems = started[0], started[1]
    arrays = list(started[2:-1])
    n = len(arrays) // 2

    def body(*refs):
        ins, lnd = refs[:n], refs[n:2 * n]
        send_sems, recv_sems = refs[2 * n], refs[2 * n + 1]
        x, y, c, chips = _place()
        for i, (src, _, peer, landed) in enumerate(make_plan(ins, lnd, x, y, c, chips)):
            cp = pltpu.make_async_remote_copy(src_ref=src, dst_ref=landed, send_sem=send_sems.at[i],
                                              recv_sem=recv_sems.at[i], device_id=peer, device_id_type=MESH)
            cp.wait_send()
            cp.wait_recv()

    outs = pl.pallas_call(
        body, name=name, out_shape=tuple(pltpu.HBM(a.shape, a.dtype) for a in arrays),
        in_specs=[_HBM] * (2 * n) + [_SEM, _SEM, pl.BlockSpec(memory_space=pl.ANY)], out_specs=tuple([_HBM] * (2 * n)),
        input_output_aliases={i: i for i in range(2 * n)},
        compiler_params=pltpu.CompilerParams(has_side_effects=_EFFECT),
    )(*arrays, send_sems, recv_sems, after)
    return list(outs[n:])


def _gather_finish(parts, lands):
    n = len(parts)

    def body(*refs):
        ins, lnd, outs, stage = refs[:n], refs[n:2 * n], refs[2 * n:3 * n], refs[3 * n:4 * n]
        send_sems, recv_sems, local_sems = refs[4 * n:]
        x, y, c, chips = _place()
        sibling = (x, y, 1 - c)
        own = _staged_copies(ins, [o.at[2 * x + y] for o in outs], stage, local_sems)
        own.load()
        sends = []
        for k in range(n):
            H = parts[k].shape[0] // 2
            for j, (px, py) in enumerate(chips):
                half = outs[k].at[2 * px + py, pl.ds(c * H, H), :]
                sends.append(pltpu.make_async_remote_copy(src_ref=half, dst_ref=half, send_sem=send_sems.at[3 * k + j],
                                                          recv_sem=recv_sems.at[3 * k + j], device_id=sibling, device_id_type=MESH))
                sends[-1].start()
        own.store()
        for k in range(n):
            H = parts[k].shape[0] // 2
            for j, (px, py) in enumerate(chips):
                other = outs[k].at[2 * px + py, pl.ds((1 - c) * H, H), :]
                pltpu.make_async_remote_copy(src_ref=other, dst_ref=other, send_sem=send_sems.at[3 * k + j],
                                             recv_sem=recv_sems.at[3 * k + j], device_id=sibling, device_id_type=MESH).wait_recv()
        for cp in sends:
            cp.wait_send()
        own.finish()

    return pl.pallas_call(
        body, name="gather_finish", out_shape=[jax.ShapeDtypeStruct(l.shape, l.dtype) for l in lands],
        in_specs=[_any_spec()] * (2 * n), out_specs=[_any_spec()] * n,
        input_output_aliases={n + k: k for k in range(n)},
        scratch_shapes=[pltpu.VMEM(p.shape, p.dtype) for p in parts] + [
            pltpu.SemaphoreType.DMA((3 * n,)), pltpu.SemaphoreType.DMA((3 * n,)), pltpu.SemaphoreType.DMA((2 * n,))],
        compiler_params=_params(),
    )(*parts, *lands)


def _place_own(part, chip, rows, name):
    _, H, C = part.shape

    def body(chip_ref, p_ref, o_ref):
        o_ref[...] = p_ref[...]

    spec = pl.BlockSpec((1, rows, C), lambda r, chip_ref: (chip_ref[0], r, 0))
    return pl.pallas_call(
        body, name=name, out_shape=jax.ShapeDtypeStruct(part.shape, part.dtype),
        grid_spec=pltpu.PrefetchScalarGridSpec(num_scalar_prefetch=1, grid=(H // rows,), in_specs=[spec], out_specs=spec),
        compiler_params=_params(),
    )(chip, part)


def _allreduce_small(vec):
    def body(v_ref, o_ref, all_ref, send_sems, recv_sems):
        x, y, c, _ = _place()
        me = 4 * x + 2 * y + c
        all_ref[me] = v_ref[...]
        sends = []
        for k in range(1, 8):
            kx, ky, kc = (k >> 2) & 1, (k >> 1) & 1, k & 1
            peer = (x ^ kx, y ^ ky, c ^ kc)
            cp = pltpu.make_async_remote_copy(
                src_ref=v_ref, dst_ref=all_ref.at[me], send_sem=send_sems.at[k - 1], recv_sem=recv_sems.at[k - 1],
                device_id=peer, device_id_type=MESH)
            cp.start()
            sends.append(cp)
        for k in range(1, 8):
            kx, ky, kc = (k >> 2) & 1, (k >> 1) & 1, k & 1
            src = 4 * (x ^ kx) + 2 * (y ^ ky) + (c ^ kc)
            pltpu.make_async_remote_copy(
                src_ref=v_ref, dst_ref=all_ref.at[src], send_sem=send_sems.at[k - 1], recv_sem=recv_sems.at[k - 1],
                device_id=(x, y, c), device_id_type=MESH).wait_recv()
        for cp in sends:
            cp.wait_send()
        total = all_ref[0]
        for d in range(1, 8):
            total = total + all_ref[d]
        o_ref[...] = total

    vm = pl.BlockSpec(memory_space=pltpu.VMEM)
    return pl.pallas_call(
        body, name="allreduce_small", out_shape=jax.ShapeDtypeStruct(vec.shape, F32), in_specs=[vm], out_specs=vm,
        scratch_shapes=[pltpu.VMEM((8,) + vec.shape, F32), pltpu.SemaphoreType.DMA((7,)), pltpu.SemaphoreType.DMA((7,))],
    )(vec)


BIG_NAMES = ("w_in", "w_out", "w_up", "w_down")
MATRIX_NAMES = BIG_NAMES + ("gla_gate_up", "conv_w")
LATE_NAMES = ("w_out", "w_up", "w_down")
RS_LABELS = BIG_NAMES + ("small",)
GATE_SHARD = (16, GLA_QK // N_SHARD)
CONVW_SHARD = (3, SHARD_FF)
SMALL_W_ROWS = 96
SMALL_G_ROWS = 64
PIECE_ROWS = (128, 128, 64, 88, SMALL_G_ROWS // 2)
ADD_ROWS = (256, 128, 256, 176, SMALL_G_ROWS // 2)
FIRST_UNIT_ROWS = (256, SMALL_W_ROWS // 2)


def _pad_rows(flat, rows):
    return jnp.pad(flat, (0, rows * LANES - flat.shape[0])).reshape(rows, LANES)


def _pack_small_weights(gate_up, conv_w):
    bits = lax.bitcast_convert_type(conv_w, BF16)
    return _pad_rows(jnp.concatenate([gate_up.astype(BF16).reshape(-1), bits.reshape(-1)]), SMALL_W_ROWS)


def _unpack_small_weights(packed):
    flat = packed.reshape(N_SHARD, -1)
    n_gate = GATE_SHARD[0] * GATE_SHARD[1]
    n_conv = 2 * CONVW_SHARD[0] * CONVW_SHARD[1]
    gate = flat[:, :n_gate].reshape((N_SHARD,) + GATE_SHARD)
    conv = lax.bitcast_convert_type(flat[:, n_gate:n_gate + n_conv].reshape((N_SHARD,) + CONVW_SHARD + (2,)), F32)
    return (jnp.transpose(gate, (1, 0, 2)).reshape(16, GLA_QK), jnp.transpose(conv, (1, 0, 2)).reshape(3, 2 * D_FF))


def _pack_small_grads(dgate, dconv):
    rows = []
    for s in range(N_SHARD):
        g = dgate[:, GATE_SHARD[1] * s:GATE_SHARD[1] * (s + 1)].reshape(-1)
        cw = dconv[:, SHARD_FF * s:SHARD_FF * (s + 1)].reshape(-1)
        rows.append(_pad_rows(jnp.concatenate([g, cw]), SMALL_G_ROWS))
    return jnp.stack(rows)


def _unpack_small_grads(packed):
    flat = packed.reshape(-1)
    n_gate = GATE_SHARD[0] * GATE_SHARD[1]
    n_conv = CONVW_SHARD[0] * CONVW_SHARD[1]
    return flat[:n_gate].reshape(GATE_SHARD), flat[n_gate:n_gate + n_conv].reshape(CONVW_SHARD)


def _permute_w_in(w):
    pad = jnp.zeros((w.shape[0], IN_WIDTH_PAD - IN_WIDTH), w.dtype)
    return jnp.concatenate([w[:, 0:1024], w[:, 1040:2320], w[:, 1024:1040], pad], axis=1)


def _unpermute_w_in(wp):
    return jnp.concatenate([wp[:, 0:1024], wp[:, 2304:2320], wp[:, 1024:2304]], axis=1)


def _rope_tables(positions):
    half = ROPE_DIM // 2
    inv_freq = ROPE_THETA ** (-jnp.arange(half, dtype=F32) * (2.0 / ROPE_DIM))
    d = jnp.arange(LANES) % SWA_HD
    freq = jnp.where(d < ROPE_DIM, inv_freq[d % half], 0.0)
    ang = positions.astype(F32)[:, None] * freq
    cos, sin = jnp.cos(ang), jnp.sin(ang)
    return cos, jnp.where(d < half, -sin, 0.0), jnp.where((d >= half) & (d < ROPE_DIM), sin, 0.0)


SMALL_NAMES = (("pre_mix_norm", 1024), ("gla_gate_bias", 256), ("gla_out_norm", 128), ("swa_sinks", 8),
               ("post_mix_norm", 1024), ("pre_ffn_norm", 1024), ("conv_b", 5632), ("post_ffn_norm", 1024))


def _pack_small(vals, loss):
    parts = [vals[n].reshape(-1) for n, _ in SMALL_NAMES] + [loss.reshape(1)]
    flat = jnp.concatenate(parts)
    return jnp.pad(flat, (0, SMALL_ROWS * LANES - flat.shape[0])).reshape(SMALL_ROWS, LANES)


def _unpack_small(packed):
    flat = packed.reshape(-1)
    out, off = {}, 0
    for n, size in SMALL_NAMES:
        out[n] = flat[off:off + size].reshape(1, size)
        off += size
    return out, flat[off]


def _local_step(x, positions, target, w, small, late_weights, early_grads):
    rc, rsa, rsb = _rope_tables(positions)
    wp = w["wp"]
    gup = jnp.pad(w["gla_gate_up"], ((0, LANES - 16), (0, 0)))
    g1, g2, g3, g4 = (small[n] for n in ("pre_mix_norm", "post_mix_norm", "pre_ffn_norm", "post_ffn_norm"))
    gbias, gnorm, cb = small["gla_gate_bias"], small["gla_out_norm"], small["conv_b"]
    sinks = small["swa_sinks"].reshape(-1)
    cw = w["conv_w"]

    h1, q, k, v, la, gg, sq, kd, vd, glr = _proj_fwd(x, g1, wp, gup, gbias, rc, rsa, rsb)
    og, s_all = _gla_fwd(q, k, v, la)
    osw = _swa_fwd(sq, kd, vd, sinks)
    w_out, w_up4, w_down = late_weights(osw)
    x1, cat, mix = _mix_out_fwd(x, og, gg, osw, gnorm, w_out, g2)
    h2, up, act, c1, c2, y, dx2, loss = _ffn_fwd(x1, g3, w_up4, cw, cb, w_down, g4, target)

    dy, dup, dx1, dg4, dg3, dcb, dcw = _ffn_bwd(dx2, y, g4, up, c1, c2, cw, w_down, w_up4, x1, g3)
    zero = early_grads(_matmul_tn(h2, dup, SHARD_FF, "grad_w_up", column_blocks_major=True),
                       _matmul_tn(act, dy, D_MODEL, "grad_w_down"))
    dmix, dog, dgg, dosw, dg2, dgn = _mix_out_bwd(dx1, mix, g2 + zero, w_out, og, gg, gnorm)
    dsq, dkd, dvd, dsink = _swa_bwd(sq, kd, vd, sinks, dosw)
    dq, dk, dv, dz = _gla_bwd(q, k, v, la, s_all, dog)
    dx, dproj, dg1, dgup, dgb = _proj_bwd(x, g1, wp, gup, glr, dq, dk, dv, dgg, dsq, dkd, dvd, dz, rc, rsa, rsb, dx1)

    grads = {
        "wp": _matmul_tn(h1, dproj, IN_WIDTH_PAD, "grad_w_in"),
        "w_out": _matmul_tn(cat, dmix, D_MODEL, "grad_w_out"),
        "gla_gate_up": dgup[0:16],
        "conv_w": dcw,
    }
    small_grads = {
        "pre_mix_norm": dg1, "gla_gate_bias": dgb, "gla_out_norm": dgn, "swa_sinks": jnp.sum(dsink[:, 0].reshape(SWA_HEADS, SWA_BLOCK), axis=1).reshape(1, SWA_HEADS),
        "post_mix_norm": dg2, "pre_ffn_norm": dg3, "conv_b": dcb, "post_ffn_norm": dg4,
    }
    return loss[0, 0], dx, grads, small_grads


ADAM_ROWS = {"w_in": 256, "w_out": 256, "w_up": 256, "w_down": 176}
WEIGHT_ORDER = ("pre_mix_norm", "w_in", "gla_gate_up", "gla_gate_bias", "gla_out_norm", "swa_sinks", "w_out",
                "post_mix_norm", "pre_ffn_norm", "w_up", "conv_w", "conv_b", "w_down", "post_ffn_norm")
TINY_ROWS = 152


def _pack_tiny(vals):
    flat = jnp.concatenate([vals[n].reshape(-1) for n in TINY_NAMES])
    return jnp.pad(flat, (0, TINY_ROWS * LANES - flat.shape[0])).reshape(TINY_ROWS, LANES)


TINY_NAMES = tuple(n for n, _ in SMALL_NAMES) + ("gla_gate_up", "conv_w")
TINY_SHAPES = {**{n: (1, s) for n, s in SMALL_NAMES}, "gla_gate_up": (16, 64), "conv_w": (3, 1408)}


def _unpack_tiny(packed):
    flat = packed.reshape(-1)
    out, off = {}, 0
    for n in TINY_NAMES:
        shape = TINY_SHAPES[n]
        size = shape[0] * shape[1]
        out[n] = flat[off:off + size].reshape(shape)
        off += size
    return out


def kernel(x, positions, pre_mix_norm, w_in, gla_gate_up, gla_gate_bias, gla_out_norm, swa_sinks, w_out, post_mix_norm, pre_ffn_norm, w_up, conv_w, conv_b, w_down, post_ffn_norm, loss_target, m_pre_mix_norm, m_w_in, m_gla_gate_up, m_gla_gate_bias, m_gla_out_norm, m_swa_sinks, m_w_out, m_post_mix_norm, m_pre_ffn_norm, m_w_up, m_conv_w, m_conv_b, m_w_down, m_post_ffn_norm, v_pre_mix_norm, v_w_in, v_gla_gate_up, v_gla_gate_bias, v_gla_out_norm, v_swa_sinks, v_w_out, v_post_mix_norm, v_pre_ffn_norm, v_w_up, v_conv_w, v_conv_b, v_w_down, v_post_ffn_norm):
    weights = dict(pre_mix_norm=pre_mix_norm, w_in=w_in, gla_gate_up=gla_gate_up, gla_gate_bias=gla_gate_bias,
                   gla_out_norm=gla_out_norm, swa_sinks=swa_sinks, w_out=w_out, post_mix_norm=post_mix_norm,
                   pre_ffn_norm=pre_ffn_norm, w_up=w_up, conv_w=conv_w, conv_b=conv_b, w_down=w_down,
                   post_ffn_norm=post_ffn_norm)
    mom = dict(pre_mix_norm=m_pre_mix_norm, w_in=m_w_in, gla_gate_up=m_gla_gate_up, gla_gate_bias=m_gla_gate_bias,
               gla_out_norm=m_gla_out_norm, swa_sinks=m_swa_sinks, w_out=m_w_out, post_mix_norm=m_post_mix_norm,
               pre_ffn_norm=m_pre_ffn_norm, w_up=m_w_up, conv_w=m_conv_w, conv_b=m_conv_b, w_down=m_w_down,
               post_ffn_norm=m_post_ffn_norm)
    var = dict(pre_mix_norm=v_pre_mix_norm, w_in=v_w_in, gla_gate_up=v_gla_gate_up, gla_gate_bias=v_gla_gate_bias,
               gla_out_norm=v_gla_out_norm, swa_sinks=v_swa_sinks, w_out=v_w_out, post_mix_norm=v_post_mix_norm,
               pre_ffn_norm=v_pre_ffn_norm, w_up=v_w_up, conv_w=v_conv_w, conv_b=v_conv_b, w_down=v_w_down,
               post_ffn_norm=v_post_ffn_norm)
    weights, mom, var = ({n: a[0] if a.ndim == 3 else a for n, a in d.items()} for d in (weights, mom, var))

    core = lax.axis_index("c").astype(jnp.int32).reshape(1)
    chip = (2 * lax.axis_index("x") + lax.axis_index("y")).astype(jnp.int32).reshape(1)
    small = {n: weights[n] for n, _ in SMALL_NAMES}

    win4, small4 = _allgather_shards(
        [weights["w_in"].astype(BF16), _pack_small_weights(weights["gla_gate_up"], weights["conv_w"])], FIRST_UNIT_ROWS)
    gate_full, convw_full = _unpack_small_weights(small4)
    first = {"wp": _permute_w_in(jnp.transpose(win4, (1, 0, 2)).reshape(D_MODEL, IN_WIDTH)),
             "gla_gate_up": gate_full, "conv_w": convw_full}
    late_shards = [weights[n].astype(BF16) for n in LATE_NAMES]
    gathering = _ici_start(late_shards, [lax.empty((N_SHARD,) + s.shape, BF16) for s in late_shards], _gather_plan, "gather_start")
    small["pre_mix_norm"] = small["pre_mix_norm"] + gathering[-1][0, 0]

    def late_weights(after):
        wout4, wup4, wdown4 = _gather_finish(late_shards, _ici_wait(gathering, after, _gather_plan, "gather_wait"))
        return wout4.reshape(D_MODEL, D_MODEL), wup4, wdown4.reshape(D_FF, D_MODEL)

    def pair_partials(arrs, which):
        got = _rs_pair_swap(arrs, [PIECE_ROWS[i] for i in which], "rs_pair_swap_" + RS_LABELS[which[0]])
        return [_rs_add_pair(a, g, core, ADD_ROWS[i], "rs_add_pair_" + RS_LABELS[i]) for a, g, i in zip(arrs, got, which)]

    early = {}

    def early_grads(g_up4, g_down):
        parts = pair_partials([g_up4, g_down.reshape(N_SHARD, D_FF // N_SHARD, D_MODEL)], (2, 3))
        lands = [_place_own(p, chip, ADD_ROWS[i], "rs_place_own_" + RS_LABELS[i]) for p, i in zip(parts, (2, 3))]
        early["scatter"] = _ici_start(parts, lands, _scatter_plan, "rs_scatter_start")
        return early["scatter"][-1][0, 0]

    loss, dx, grads, small_grads = _local_step(x[0], positions[0], loss_target[0], first, small, late_weights, early_grads)

    landed_ffn = _ici_wait(early["scatter"], dx, _scatter_plan, "rs_scatter_wait")
    rest = [jnp.transpose(_unpermute_w_in(grads["wp"]).reshape(D_MODEL, N_SHARD, IN_WIDTH // N_SHARD), (1, 0, 2)),
            grads["w_out"].reshape(N_SHARD, D_MODEL // N_SHARD, D_MODEL),
            _pack_small_grads(grads["gla_gate_up"], grads["conv_w"])]
    landed_rest = _rs_chip_scatter(pair_partials(rest, (0, 1, 4)))
    landed = [landed_rest[0], landed_rest[1], landed_ffn[0], landed_ffn[1], landed_rest[2]]
    halves = [_rs_sum_chips(p, rows, "rs_sum_chips_" + n) for p, rows, n in zip(landed, ADD_ROWS, RS_LABELS)]
    reduced = _rs_pair_share(halves, PIECE_ROWS)
    g_small, loss_sum = _unpack_small(_allreduce_small(_pack_small(small_grads, loss)))
    g_gate, g_convw = _unpack_small_grads(reduced[4])
    g_all = {**g_small, **dict(zip(BIG_NAMES, reduced[:4])), "gla_gate_up": g_gate, "conv_w": g_convw}

    delta, new_m, new_v = {}, {}, {}
    for n in BIG_NAMES:
        delta[n], new_m[n], new_v[n] = _adamw(weights[n], g_all[n], mom[n], var[n], ADAM_ROWS[n], "adamw_" + n)
    tiny = _adamw(*(_pack_tiny({n: d[n] for n in TINY_NAMES}) for d in (weights, g_all, mom, var)), TINY_ROWS, "adamw_small")
    for res, packed in zip((delta, new_m, new_v), tiny):
        res.update(_unpack_tiny(packed))

    def lead(n, a):
        return a[None] if n in MATRIX_NAMES else a

    outs = [loss_sum, dx[None]]
    for d in (g_all, delta, new_m, new_v):
        outs.extend(lead(n, d[n]) for n in WEIGHT_ORDER)
    return tuple(outs)
```

```python
import functools

import jax
import jax.numpy as jnp
from jax import lax
from jax.experimental import pallas as pl
from jax.experimental.pallas import tpu as pltpu

F32 = jnp.float32
BF16 = jnp.bfloat16
MESH = pl.DeviceIdType.MESH

D_MODEL = 1024
GLA_HEADS = 4
GLA_DK = 64
GLA_DV = 128
GLA_TAU = 16.0
GLA_CHUNK = 64
SWA_HEADS = 8
SWA_HD = 64
SWA_BLOCK = 128
ROPE_THETA = 500000.0
ROPE_DIM = 16
D_FF = 2816
EPS = 1e-6
GLA_QK = 256
GLA_V = 512
SWA_Q = 512
SWA_KV = 128
IN_WIDTH = 2320
IN_WIDTH_PAD = 2432
N_SHARD = 4

ADAM_LR = 0.001
ADAM_B1 = 0.9
ADAM_B2 = 0.999
ADAM_EPS = 1e-08
ADAM_WD = 0.01
ADAM_STEP = 10

LANES = 128
VMEM_LIMIT = 56 * 1024 * 1024
TM = 256
SHARD_FF = 2 * D_FF // N_SHARD
FF_PIECES = ((0, 512), (512, 512), (1024, 384))
GLA_BLOCK = 256

SMALL_ROWS = 80


def _params(**kw):
    return pltpu.CompilerParams(vmem_limit_bytes=VMEM_LIMIT, **kw)


def _mm(a, b):
    return lax.dot_general(a.astype(BF16), b.astype(BF16), (((1,), (0,)), ((), ())), preferred_element_type=F32)


def _mm_nt(a, b):
    return lax.dot_general(a.astype(BF16), b.astype(BF16), (((1,), (1,)), ((), ())), preferred_element_type=F32)


def _mm_tn(a, b):
    return lax.dot_general(a.astype(BF16), b.astype(BF16), (((0,), (0,)), ((), ())), preferred_element_type=F32)


def _mm_f32(a, b):
    return lax.dot_general(a, b, (((1,), (0,)), ((), ())), preferred_element_type=F32, precision=lax.Precision.HIGHEST)


def _iota(shape, dim):
    return lax.broadcasted_iota(jnp.int32, shape, dim)


def _sigmoid(x):
    return 1.0 / (1.0 + jnp.exp(-x))


def _gelu_parts(x):
    c = 0.7978845608028654
    x2 = x * x
    t = jnp.tanh(c * (x + 0.044715 * (x2 * x)))
    cdf = 0.5 * (1.0 + t)
    dcdf = 0.5 * (1.0 - t * t) * c * (1.0 + 3.0 * 0.044715 * x2)
    return x * cdf, cdf + x * dcdf


def _rms_bwd(v, r, g, dout):
    gd = g * dout
    return r * gd - v * (r * r * r) * jnp.mean(v * gd, axis=-1, keepdims=True)


def _row_spec(tm, cols):
    return pl.BlockSpec((tm, cols), lambda i: (i, 0))


def _const_spec(shape):
    return pl.BlockSpec(shape, lambda i: (0,) * len(shape))


def _any_spec():
    return pl.BlockSpec(memory_space=pl.ANY)


def _load_once(src_hbm, dst_vmem, sem):
    @pl.when(pl.program_id(0) == 0)
    def _():
        cp = pltpu.make_async_copy(src_hbm, dst_vmem, sem)
        cp.start()
        cp.wait()


def _rotate(v, rc, rsa, rsb):
    return v * rc + pltpu.roll(v, 120, 1) * rsa + pltpu.roll(v, 8, 1) * rsb


def _rotate_bwd(dv, rc, rsa, rsb):
    return dv * rc + pltpu.roll(dv * rsa, 8, 1) + pltpu.roll(dv * rsb, 120, 1)


def _proj_fwd(x, g1, wp, gup, gbias, rc, rsa, rsb):
    T = x.shape[0]

    def body(x_ref, g1_ref, wp_hbm, gup_ref, gb_ref, rc_ref, rsa_ref, rsb_ref,
             h1_ref, q_ref, k_ref, v_ref, la_ref, gg_ref, sq_ref, kd_ref, vd_ref, glr_ref, wp_v, sem):
        _load_once(wp_hbm, wp_v, sem)
        xt = x_ref[...]
        r = lax.rsqrt(jnp.mean(xt * xt, axis=-1, keepdims=True) + EPS)
        h = (xt * r * g1_ref[...]).astype(BF16)
        h1_ref[...] = h
        q_ref[...] = _mm(h, wp_v[:, 0:256])
        k_ref[...] = _mm(h, wp_v[:, 256:512])
        v_ref[...] = _mm(h, wp_v[:, 512:1024]).astype(BF16)
        gg_ref[...] = _mm(h, wp_v[:, 1024:1536])
        glr = _mm(h, wp_v[:, 2304:2432]).astype(BF16)
        glr_ref[...] = glr
        z = _mm(glr, gup_ref[...]) + gb_ref[...]
        la_ref[...] = (jnp.minimum(z, 0.0) - jnp.log1p(jnp.exp(-jnp.abs(z)))) * (1.0 / GLA_TAU)
        rc_, rsa_, rsb_ = rc_ref[...], rsa_ref[...], rsb_ref[...]
        for s in range(4):
            qs = _mm(h, wp_v[:, 1536 + 128 * s:1664 + 128 * s])
            sq_ref[:, 128 * s:128 * s + 128] = (_rotate(qs, rc_, rsa_, rsb_) * 0.125).astype(BF16)
        lane = _iota((TM, LANES), 1)
        first = lane < 64
        kr = _rotate(_mm(h, wp_v[:, 2048:2176]), rc_, rsa_, rsb_)
        krr = pltpu.roll(kr, 64, 1)
        kd_ref[:, 0:128] = jnp.where(first, kr, krr).astype(BF16)
        kd_ref[:, 128:256] = jnp.where(first, krr, kr).astype(BF16)
        vr = _mm(h, wp_v[:, 2176:2304])
        vrr = pltpu.roll(vr, 64, 1)
        vd_ref[:, 0:128] = jnp.where(first, vr, vrr).astype(BF16)
        vd_ref[:, 128:256] = jnp.where(first, vrr, vr).astype(BF16)

    outs = [
        jax.ShapeDtypeStruct((T, D_MODEL), BF16),
        jax.ShapeDtypeStruct((T, GLA_QK), F32),
        jax.ShapeDtypeStruct((T, GLA_QK), F32),
        jax.ShapeDtypeStruct((T, GLA_V), BF16),
        jax.ShapeDtypeStruct((T, GLA_QK), F32),
        jax.ShapeDtypeStruct((T, GLA_V), F32),
        jax.ShapeDtypeStruct((T, SWA_Q), BF16),
        jax.ShapeDtypeStruct((T, 256), BF16),
        jax.ShapeDtypeStruct((T, 256), BF16),
        jax.ShapeDtypeStruct((T, LANES), BF16),
    ]
    return pl.pallas_call(
        body, name="proj_fwd", grid=(T // TM,), out_shape=outs,
        in_specs=[_row_spec(TM, D_MODEL), _const_spec((1, D_MODEL)), _any_spec(), _const_spec((LANES, GLA_QK)),
                  _const_spec((1, GLA_QK)), _row_spec(TM, LANES), _row_spec(TM, LANES), _row_spec(TM, LANES)],
        out_specs=[_row_spec(TM, o.shape[1]) for o in outs],
        scratch_shapes=[pltpu.VMEM((D_MODEL, IN_WIDTH_PAD), BF16), pltpu.SemaphoreType.DMA],
        compiler_params=_params(),
    )(x, g1, wp, gup, gbias, rc, rsa, rsb)


GLA_NB = GLA_BLOCK // GLA_CHUNK


def _gla_masks():
    n = GLA_BLOCK
    lane = _iota((n, LANES), 1)
    lane_masks = [(lane < 64).astype(F32), (lane >= 64).astype(F32)]
    row, col = _iota((n, n), 0), _iota((n, n), 1)
    same_chunk = (row >> 6) == (col >> 6)
    blk = ((_iota((256, LANES), 0) >> 7) == (_iota((256, LANES), 1) >> 6)).astype(F32)
    return lane_masks, same_chunk & (col <= row), same_chunk & (col >= row), blk


def _chunk_rows(vals):
    return jnp.concatenate([jnp.broadcast_to(v, (GLA_CHUNK, LANES)) for v in vals], axis=0)


def _gla_block_terms(q_ref, k_ref, b_ref, p):
    C = GLA_CHUNK
    cols = slice(LANES * p, LANES * p + LANES)
    bc = b_ref[:, cols]
    bl_rows = [b_ref[C * c + C - 1:C * c + C, cols] for c in range(GLA_NB)]
    bl = _chunk_rows(bl_rows)
    bm = _chunk_rows([b_ref[C * c + C // 2 - 1:C * c + C // 2, cols] for c in range(GLA_NB)])
    qs = q_ref[:, cols] * 0.125
    kk = k_ref[:, cols]
    eb = jnp.exp(bc)
    ekl = jnp.exp(bl - bc)
    eqm = jnp.exp(bc - bm)
    ekm = jnp.exp(bm - bc)
    return qs, kk, eb, ekl, eqm, ekm, [jnp.exp(r) for r in bl_rows]


def _block_cumsum(la, mask):
    return _mm_f32(mask.astype(F32), la)


def _gla_fwd(q, k, v, la):
    T = q.shape[0]
    NB = GLA_BLOCK // GLA_CHUNK
    C = GLA_CHUNK

    def body(q_ref, k_ref, v_ref, la_ref, o_ref, s_ref, st_ref, b_ref):
        @pl.when(pl.program_id(0) == 0)
        def _():
            st_ref[...] = jnp.zeros_like(st_ref)

        lane_masks, causal, _, blk = _gla_masks()
        b_ref[...] = _block_cumsum(la_ref[...], causal)
        for p in range(2):
            qs, kk, eb, ekl, eqm, ekm, gam = _gla_block_terms(q_ref, k_ref, b_ref, p)
            qh, kh, qm, km = qs * eb, kk * ekl, qs * eqm, kk * ekm
            vp = v_ref[:, 256 * p:256 * p + 256]
            intra = []
            for j in range(2):
                a = jnp.where(causal, _mm_nt(qm * lane_masks[j], km), 0.0)
                intra.append(_mm(a, vp[:, 128 * j:128 * j + 128]))
            kv = [blk * _mm_tn(vp[C * c:C * c + C], kh[C * c:C * c + C]) for c in range(NB)]
            st = st_ref[p]
            inter = []
            for c in range(NB):
                s_ref[c, p] = st[0:LANES] + st[LANES:2 * LANES]
                inter.append(_mm_nt(qh[C * c:C * c + C], st))
                st = st * gam[c] + kv[c]
            st_ref[p] = st
            o_ref[:, 256 * p:256 * p + 256] = jnp.concatenate(inter, axis=0) + jnp.concatenate(intra, axis=1)

    return pl.pallas_call(
        body, name="gla_fwd", grid=(T // GLA_BLOCK,),
        out_shape=[jax.ShapeDtypeStruct((T, GLA_V), F32), jax.ShapeDtypeStruct((T // C, 2, LANES, LANES), F32)],
        in_specs=[_row_spec(GLA_BLOCK, GLA_QK), _row_spec(GLA_BLOCK, GLA_QK), _row_spec(GLA_BLOCK, GLA_V),
                  _row_spec(GLA_BLOCK, GLA_QK)],
        out_specs=[_row_spec(GLA_BLOCK, GLA_V), pl.BlockSpec((NB, 2, LANES, LANES), lambda i: (i, 0, 0, 0))],
        scratch_shapes=[pltpu.VMEM((2, 256, LANES), F32), pltpu.VMEM((GLA_BLOCK, GLA_QK), F32)],
        compiler_params=_params(),
    )(q, k, v, la)


def _gla_bwd(q, k, v, la, s_all, do):
    T = q.shape[0]
    NB = GLA_BLOCK // GLA_CHUNK
    C = GLA_CHUNK
    nblk = T // GLA_BLOCK

    def body(q_ref, k_ref, v_ref, la_ref, s_ref, do_ref, dq_ref, dk_ref, dv_ref, dz_ref, dst_ref, b_ref):
        @pl.when(pl.program_id(0) == 0)
        def _():
            dst_ref[...] = jnp.zeros_like(dst_ref)

        lane_masks, causal, anti_causal, blk = _gla_masks()
        b_ref[...] = _block_cumsum(la_ref[...], causal)
        for p in range(2):
            cols = slice(LANES * p, LANES * p + LANES)
            qs, kk, eb, ekl, eqm, ekm, gam = _gla_block_terms(q_ref, k_ref, b_ref, p)
            qh, kh, qm, km = qs * eb, kk * ekl, qs * eqm, kk * ekm
            vp = v_ref[:, 256 * p:256 * p + 256]
            dop = do_ref[:, 256 * p:256 * p + 256]
            dqm = jnp.zeros((GLA_BLOCK, LANES), F32)
            dkm = jnp.zeros((GLA_BLOCK, LANES), F32)
            dv_intra = []
            for j in range(2):
                hs = slice(128 * j, 128 * j + 128)
                a = jnp.where(causal, _mm_nt(qm * lane_masks[j], km), 0.0)
                da = jnp.where(causal, _mm_nt(dop[:, hs], vp[:, hs]), 0.0)
                dv_intra.append(_mm_tn(a, dop[:, hs]))
                dqm = dqm + lane_masks[j] * _mm(da, km)
                dkm = dkm + lane_masks[j] * _mm_tn(da, qm)
            grow = [blk * _mm_tn(dop[C * c:C * c + C], qh[C * c:C * c + C]) for c in range(NB)]
            dst = dst_ref[p]
            dst_after = [None] * NB
            for c in reversed(range(NB)):
                dst_after[c] = dst
                dst = dst * gam[c] + grow[c]
            dst_ref[p] = dst
            dqh, dkh, dv_state, extra = [], [], [], []
            for c in range(NB):
                rows = slice(C * c, C * c + C)
                packed = s_ref[c, p]
                st = jnp.concatenate([packed * lane_masks[0][0:LANES], packed * lane_masks[1][0:LANES]], axis=0)
                dqh.append(_mm(dop[rows], st))
                dkh.append(_mm(vp[rows], dst_after[c]))
                dv_state.append(_mm_nt(kh[rows], dst_after[c]))
                extra.append(jnp.sum(dkh[c] * kh[rows], axis=0, keepdims=True)
                             + jnp.sum(st * dst_after[c], axis=0, keepdims=True) * gam[c])
            dqs = jnp.concatenate(dqh, axis=0) * eb + dqm * eqm
            dk = jnp.concatenate(dkh, axis=0) * ekl + dkm * ekm
            dg = _mm_f32(anti_causal.astype(F32), dqs * qs - dk * kk) + _chunk_rows(extra)
            dq_ref[:, cols] = (dqs * 0.125).astype(BF16)
            dk_ref[:, cols] = dk.astype(BF16)
            dz_ref[:, cols] = dg * (1.0 - jnp.exp(GLA_TAU * la_ref[:, cols])) * (1.0 / GLA_TAU)
            dv_ref[:, 256 * p:256 * p + 256] = (jnp.concatenate(dv_state, axis=0) + jnp.concatenate(dv_intra, axis=1)).astype(BF16)

    rev = lambda i: (nblk - 1 - i, 0)
    rspec = lambda cols: pl.BlockSpec((GLA_BLOCK, cols), rev)
    return pl.pallas_call(
        body, name="gla_bwd", grid=(nblk,),
        out_shape=[jax.ShapeDtypeStruct((T, GLA_QK), BF16), jax.ShapeDtypeStruct((T, GLA_QK), BF16),
                   jax.ShapeDtypeStruct((T, GLA_V), BF16), jax.ShapeDtypeStruct((T, GLA_QK), F32)],
        in_specs=[rspec(GLA_QK), rspec(GLA_QK), rspec(GLA_V), rspec(GLA_QK),
                  pl.BlockSpec((NB, 2, LANES, LANES), lambda i: (nblk - 1 - i, 0, 0, 0)), rspec(GLA_V)],
        out_specs=[rspec(GLA_QK), rspec(GLA_QK), rspec(GLA_V), rspec(GLA_QK)],
        scratch_shapes=[pltpu.VMEM((2, 256, LANES), F32), pltpu.VMEM((GLA_BLOCK, GLA_QK), F32)],
        compiler_params=_params(),
    )(q, k, v, la, s_all, do)


SWA_GROUP = 4


def _swa_stack(ref, g, first):
    parts = []
    for j in range(SWA_GROUP):
        m = 2 * g + j // 2
        pair = ref[:, 128 * m:128 * m + 128]
        zero = jnp.zeros_like(pair)
        parts.append(jnp.where(first, pair, zero) if j % 2 == 0 else jnp.where(first, zero, pair))
    return jnp.concatenate(parts, axis=0)


def _swa_unstack(rows, mm, first):
    W = SWA_BLOCK
    return jnp.where(first, rows[W * 2 * mm:W * (2 * mm + 1)], rows[W * (2 * mm + 1):W * (2 * mm + 2)])


def _swa_probs(qs, kp, kc, vp, vc, i, g, sink_ref, first4):
    W = SWA_BLOCK
    R = SWA_GROUP * W
    r, c = _iota((R, W), 0) & (W - 1), _iota((R, W), 1)
    neg = -1e30
    s_p = jnp.where((c > r) & (i > 0), _mm_nt(qs, kp), neg)
    s_c = jnp.where(c <= r, _mm_nt(qs, kc), neg)
    head = _iota((R, 1), 0) >> 7
    sink = jnp.where(head == 0, sink_ref[4 * g], jnp.where(head == 1, sink_ref[4 * g + 1],
                                                           jnp.where(head == 2, sink_ref[4 * g + 2], sink_ref[4 * g + 3])))
    m = jnp.maximum(jnp.max(jnp.maximum(s_p, s_c), axis=-1, keepdims=True), sink)
    p_p = jnp.exp(s_p - m)
    p_c = jnp.exp(s_c - m)
    p_s = jnp.exp(sink - m)
    one = jnp.ones((W, LANES), BF16)
    first = _iota((W, LANES), 1) < 64
    acc = _mm(p_p, jnp.where(first, vp, one)) + _mm(p_c, jnp.where(first, vc, one))
    rolled = pltpu.roll(acc, 64, 1)
    denom = jnp.where(first4, rolled, acc) + p_s
    return p_p, p_c, p_s, denom, acc, rolled


def _swa_fwd(sq, kd, vd, sinks):
    T = sq.shape[0]
    W = SWA_BLOCK
    prev = lambda i: (jnp.maximum(i - 1, 0), 0)

    def body(sink_ref, q_ref, kp_ref, kc_ref, vp_ref, vc_ref, o_ref):
        i = pl.program_id(0)
        first4 = _iota((SWA_GROUP * W, LANES), 1) < 64
        first = _iota((W, LANES), 1) < 64
        for g in range(2):
            gs = slice(128 * g, 128 * g + 128)
            qs = _swa_stack(q_ref, g, first)
            _, _, _, denom, acc, rolled = _swa_probs(qs, kp_ref[:, gs], kc_ref[:, gs], vp_ref[:, gs], vc_ref[:, gs],
                                                     i, g, sink_ref, first4)
            pv = jnp.where(first4, acc, rolled)
            o = pv / denom
            for mm in range(2):
                m = 2 * g + mm
                o_ref[:, 128 * m:128 * m + 128] = _swa_unstack(o, mm, first).astype(BF16)

    return pl.pallas_call(
        body, name="swa_fwd", grid=(T // W,), out_shape=jax.ShapeDtypeStruct((T, SWA_Q), BF16),
        in_specs=[pl.BlockSpec(memory_space=pltpu.SMEM), _row_spec(W, SWA_Q), pl.BlockSpec((W, 256), prev),
                  _row_spec(W, 256), pl.BlockSpec((W, 256), prev), _row_spec(W, 256)],
        out_specs=_row_spec(W, SWA_Q),
        compiler_params=_params(),
    )(sinks, sq, kd, kd, vd, vd)


def _swa_bwd(sq, kd, vd, sinks, do):
    T = sq.shape[0]
    W = SWA_BLOCK
    n = T // W
    cur = lambda i: (jnp.minimum(i, n - 1), 0)
    prev = lambda i: (jnp.clip(i - 1, 0, n - 1), 0)

    def body(sink_ref, q_ref, kp_ref, kc_ref, vp_ref, vc_ref, do_ref, dq_ref, dk_ref, dv_ref, ds_ref, ck_ref, cv_ref):
        i = pl.program_id(0)

        @pl.when(i == 0)
        def _():
            ds_ref[...] = jnp.zeros_like(ds_ref)
            ck_ref[...] = jnp.zeros_like(ck_ref)
            cv_ref[...] = jnp.zeros_like(cv_ref)

        @pl.when(i < n)
        def _():
            first4 = _iota((SWA_GROUP * W, LANES), 1) < 64
            first = _iota((W, LANES), 1) < 64
            for g in range(2):
                gs = slice(128 * g, 128 * g + 128)
                kp, kc, vp, vc = kp_ref[:, gs], kc_ref[:, gs], vp_ref[:, gs], vc_ref[:, gs]
                qs = _swa_stack(q_ref, g, first)
                dos = _swa_stack(do_ref, g, first)
                p_p, p_c, p_s, denom, _, _ = _swa_probs(qs, kp, kc, vp, vc, i, g, sink_ref, first4)
                inv = 1.0 / denom
                p_p, p_c = p_p * inv, p_c * inv
                dp_p = _mm_nt(dos, vp)
                dp_c = _mm_nt(dos, vc)
                delta = jnp.sum(p_p * dp_p + p_c * dp_c, axis=-1, keepdims=True)
                ds_p = p_p * (dp_p - delta)
                ds_c = p_c * (dp_c - delta)
                rows = slice(SWA_GROUP * W * g, SWA_GROUP * W * (g + 1))
                ds_ref[rows, :] = ds_ref[rows, :] - (p_s * delta) * inv
                dq = (_mm(ds_p, kp) + _mm(ds_c, kc)) * 0.125
                for mm in range(2):
                    m = 2 * g + mm
                    dq_ref[:, 128 * m:128 * m + 128] = _swa_unstack(dq, mm, first)
                dk_ref[:, gs] = ck_ref[:, gs] + _mm_tn(ds_p, qs)
                dv_ref[:, gs] = cv_ref[:, gs] + _mm_tn(p_p, dos)
                ck_ref[:, gs] = _mm_tn(ds_c, qs)
                cv_ref[:, gs] = _mm_tn(p_c, dos)

        @pl.when(i == n)
        def _():
            dk_ref[...] = ck_ref[...]
            dv_ref[...] = cv_ref[...]

    return pl.pallas_call(
        body, name="swa_bwd", grid=(n + 1,),
        out_shape=[jax.ShapeDtypeStruct((T, SWA_Q), F32), jax.ShapeDtypeStruct((T, 256), F32),
                   jax.ShapeDtypeStruct((T, 256), F32), jax.ShapeDtypeStruct((SWA_HEADS * W, LANES), F32)],
        in_specs=[pl.BlockSpec(memory_space=pltpu.SMEM), pl.BlockSpec((W, SWA_Q), cur), pl.BlockSpec((W, 256), prev),
                  pl.BlockSpec((W, 256), cur), pl.BlockSpec((W, 256), prev), pl.BlockSpec((W, 256), cur),
                  pl.BlockSpec((W, SWA_Q), cur)],
        out_specs=[pl.BlockSpec((W, SWA_Q), cur), pl.BlockSpec((W, 256), prev), pl.BlockSpec((W, 256), prev),
                   _const_spec((SWA_HEADS * W, LANES))],
        scratch_shapes=[pltpu.VMEM((W, 256), F32), pltpu.VMEM((W, 256), F32)],
        compiler_params=_params(),
    )(sinks, sq, kd, kd, vd, vd, do)


def _mix_out_fwd(x, og, gg, osw, gnorm, wout, g2):
    T = x.shape[0]

    def body(x_ref, og_ref, gg_ref, osw_ref, gn_ref, wout_ref, g2_ref, x1_ref, cat_ref):
        gn = gn_ref[...]
        for j in range(GLA_HEADS):
            hs = slice(128 * j, 128 * j + 128)
            o = og_ref[:, hs]
            r = lax.rsqrt(jnp.mean(o * o, axis=-1, keepdims=True) + EPS)
            gate = gg_ref[:, hs]
            cat_ref[:, hs] = (o * r * gn * (gate * _sigmoid(gate))).astype(BF16)
        cat_ref[:, GLA_V:] = osw_ref[...]
        mix = _mm(cat_ref[...], wout_ref[...])
        r2 = lax.rsqrt(jnp.mean(mix * mix, axis=-1, keepdims=True) + EPS)
        x1_ref[...] = x_ref[...] + mix * r2 * g2_ref[...]

    return pl.pallas_call(
        body, name="mix_out_fwd", grid=(T // TM,),
        out_shape=[jax.ShapeDtypeStruct((T, D_MODEL), F32), jax.ShapeDtypeStruct((T, D_MODEL), BF16)],
        in_specs=[_row_spec(TM, D_MODEL), _row_spec(TM, GLA_V), _row_spec(TM, GLA_V), _row_spec(TM, SWA_Q),
                  _const_spec((1, LANES)), _const_spec((D_MODEL, D_MODEL)), _const_spec((1, D_MODEL))],
        out_specs=[_row_spec(TM, D_MODEL), _row_spec(TM, D_MODEL)],
        compiler_params=_params(),
    )(x, og, gg, osw, gnorm, wout, g2)


HALO = 8


def _rows_before(v, prev1, prev2):
    row = _iota(v.shape, 0)
    m1 = jnp.where(row == 0, prev1, pltpu.roll(v, 1, 0))
    m2 = jnp.where(row == 0, prev2, jnp.where(row == 1, prev1, pltpu.roll(v, 2, 0)))
    return m1, m2


def _rows_after(v, next1, next2):
    n = v.shape[0]
    row = _iota(v.shape, 0)
    p1 = jnp.where(row == n - 1, next1, pltpu.roll(v, n - 1, 0))
    p2 = jnp.where(row == n - 1, next2, jnp.where(row == n - 2, next1, pltpu.roll(v, n - 2, 0)))
    return p1, p2


def _ff_pieces():
    return [(j, off, wd) for j in range(2) for off, wd in FF_PIECES]


def _ffn_fwd(x1, g3, wup, cw, cb, wdown, g4, target):
    T = x1.shape[0]

    def body(x1_ref, g3_ref, wup_hbm, cw_ref, cb_ref, wdn_hbm, g4_ref, tg_ref,
             h2_ref, up_ref, a_ref, c1_ref, c2_ref, y_ref, dx2_ref, loss_ref, wup_v, wdn_v, carry_ref, sems):
        _load_once(wup_hbm, wup_v, sems.at[0])
        _load_once(wdn_hbm, wdn_v, sems.at[1])

        @pl.when(pl.program_id(0) == 0)
        def _():
            carry_ref[...] = jnp.zeros_like(carry_ref)
            loss_ref[...] = jnp.zeros_like(loss_ref)

        x1 = x1_ref[...]
        r3 = lax.rsqrt(jnp.mean(x1 * x1, axis=-1, keepdims=True) + EPS)
        h2 = (x1 * r3 * g3_ref[...]).astype(BF16)
        h2_ref[...] = h2
        for j, off, wd in _ff_pieces():
            base = SHARD_FF * j + off
            u = []
            for half in range(2):
                cs = slice(D_FF * half + base, D_FF * half + base + wd)
                upb = _mm(h2, wup_v[2 * half + j, :, off:off + wd]).astype(BF16)
                up_ref[:, cs] = upb
                upf = upb.astype(F32)
                m1, m2 = _rows_before(upf, carry_ref[HALO - 1:HALO, cs], carry_ref[HALO - 2:HALO - 1, cs])
                u.append(cb_ref[:, cs] + cw_ref[0:1, cs] * m2 + cw_ref[1:2, cs] * m1 + cw_ref[2:3, cs] * upf)
                carry_ref[:, cs] = upf[TM - HALO:TM, :]
            act, dact = _gelu_parts(u[1])
            a = (act * u[0]).astype(BF16)
            out = slice(base, base + wd)
            a_ref[:, out] = a
            c1_ref[:, out] = act.astype(BF16)
            c2_ref[:, out] = (u[0] * dact).astype(BF16)
        y = _mm(a_ref[...], wdn_v[...])
        y_ref[...] = y
        r4 = lax.rsqrt(jnp.mean(y * y, axis=-1, keepdims=True) + EPS)
        err = x1 + y * r4 * g4_ref[...] - tg_ref[...]
        dx2_ref[...] = err * (1.0 / D_MODEL)
        loss_ref[...] = loss_ref[...] + jnp.sum(err * err) * (0.5 / D_MODEL)

    outs = [
        jax.ShapeDtypeStruct((T, D_MODEL), BF16),
        jax.ShapeDtypeStruct((T, 2 * D_FF), BF16),
        jax.ShapeDtypeStruct((T, D_FF), BF16),
        jax.ShapeDtypeStruct((T, D_FF), BF16),
        jax.ShapeDtypeStruct((T, D_FF), BF16),
        jax.ShapeDtypeStruct((T, D_MODEL), F32),
        jax.ShapeDtypeStruct((T, D_MODEL), F32),
        jax.ShapeDtypeStruct((8, LANES), F32),
    ]
    return pl.pallas_call(
        body, name="ffn_fwd", grid=(T // TM,), out_shape=outs,
        in_specs=[_row_spec(TM, D_MODEL), _const_spec((1, D_MODEL)), _any_spec(), _const_spec((3, 2 * D_FF)),
                  _const_spec((1, 2 * D_FF)), _any_spec(), _const_spec((1, D_MODEL)), _row_spec(TM, D_MODEL)],
        out_specs=[_row_spec(TM, D_MODEL), _row_spec(TM, 2 * D_FF), _row_spec(TM, D_FF), _row_spec(TM, D_FF),
                   _row_spec(TM, D_FF), _row_spec(TM, D_MODEL), _row_spec(TM, D_MODEL), _const_spec((8, LANES))],
        scratch_shapes=[pltpu.VMEM((N_SHARD, D_MODEL, SHARD_FF), BF16), pltpu.VMEM((D_FF, D_MODEL), BF16),
                        pltpu.VMEM((HALO, 2 * D_FF), F32), pltpu.SemaphoreType.DMA((2,))],
        compiler_params=_params(),
    )(x1, g3, wup, cw, cb, wdown, g4, target)


def _ffn_bwd(dx2, y, g4, up, c1, c2, cw, wdown, wup, x1, g3):
    T = dx2.shape[0]
    nt = T // TM
    rev = lambda i: (nt - 1 - i, 0)

    def body(dn_ref, y_ref, g4_ref, up_ref, c1_ref, c2_ref, cw_ref, wdn_hbm, wup_hbm, x1_ref, g3_ref,
             dy_ref, dup_ref, dx1_ref, dg4_ref, dg3_ref, dcb_ref, dcw_ref, wup_v, wdn_v, carry_ref, sems):
        _load_once(wup_hbm, wup_v, sems.at[0])
        _load_once(wdn_hbm, wdn_v, sems.at[1])

        @pl.when(pl.program_id(0) == 0)
        def _():
            carry_ref[...] = jnp.zeros_like(carry_ref)
            dg4_ref[...] = jnp.zeros_like(dg4_ref)
            dg3_ref[...] = jnp.zeros_like(dg3_ref)
            dcb_ref[...] = jnp.zeros_like(dcb_ref)
            dcw_ref[...] = jnp.zeros_like(dcw_ref)

        dn = dn_ref[...]
        y = y_ref[...]
        g4v = g4_ref[...]
        r4 = lax.rsqrt(jnp.mean(y * y, axis=-1, keepdims=True) + EPS)
        dg4_ref[...] = dg4_ref[...] + jnp.sum(dn * y * r4, axis=0, keepdims=True)
        dy = _rms_bwd(y, r4, g4v, dn).astype(BF16)
        dy_ref[...] = dy
        dh2 = jnp.zeros((TM, D_MODEL), F32)
        for j, off, wd in _ff_pieces():
            base = SHARD_FF * j + off
            da = _mm_nt(dy, wdn_v[base:base + wd, :])
            for half, coef_ref in enumerate((c1_ref, c2_ref)):
                cs = slice(D_FF * half + base, D_FF * half + base + wd)
                du = da * coef_ref[:, base:base + wd].astype(F32)
                p1, p2 = _rows_after(du, carry_ref[0:1, cs], carry_ref[1:2, cs])
                carry_ref[:, cs] = du[0:HALO, :]
                upf = up_ref[:, cs].astype(F32)
                dcb_ref[:, cs] = dcb_ref[:, cs] + jnp.sum(du, axis=0, keepdims=True)
                dcw_ref[0:1, cs] = dcw_ref[0:1, cs] + jnp.sum(p2 * upf, axis=0, keepdims=True)
                dcw_ref[1:2, cs] = dcw_ref[1:2, cs] + jnp.sum(p1 * upf, axis=0, keepdims=True)
                dcw_ref[2:3, cs] = dcw_ref[2:3, cs] + jnp.sum(du * upf, axis=0, keepdims=True)
                dup = (cw_ref[2:3, cs] * du + cw_ref[1:2, cs] * p1 + cw_ref[0:1, cs] * p2).astype(BF16)
                dup_ref[:, cs] = dup
                dh2 = dh2 + _mm_nt(dup, wup_v[2 * half + j, :, off:off + wd])
        x1 = x1_ref[...]
        r3 = lax.rsqrt(jnp.mean(x1 * x1, axis=-1, keepdims=True) + EPS)
        dg3_ref[...] = dg3_ref[...] + jnp.sum(dh2 * x1 * r3, axis=0, keepdims=True)
        dx1_ref[...] = dn + _rms_bwd(x1, r3, g3_ref[...], dh2)

    outs = [
        jax.ShapeDtypeStruct((T, D_MODEL), BF16),
        jax.ShapeDtypeStruct((T, 2 * D_FF), BF16),
        jax.ShapeDtypeStruct((T, D_MODEL), F32),
        jax.ShapeDtypeStruct((1, D_MODEL), F32),
        jax.ShapeDtypeStruct((1, D_MODEL), F32),
        jax.ShapeDtypeStruct((1, 2 * D_FF), F32),
        jax.ShapeDtypeStruct((3, 2 * D_FF), F32),
    ]
    return pl.pallas_call(
        body, name="ffn_bwd", grid=(nt,), out_shape=outs,
        in_specs=[pl.BlockSpec((TM, D_MODEL), rev), pl.BlockSpec((TM, D_MODEL), rev), _const_spec((1, D_MODEL)),
                  pl.BlockSpec((TM, 2 * D_FF), rev), pl.BlockSpec((TM, D_FF), rev), pl.BlockSpec((TM, D_FF), rev),
                  _const_spec((3, 2 * D_FF)), _any_spec(), _any_spec(), pl.BlockSpec((TM, D_MODEL), rev),
                  _const_spec((1, D_MODEL))],
        out_specs=[pl.BlockSpec((TM, D_MODEL), rev), pl.BlockSpec((TM, 2 * D_FF), rev), pl.BlockSpec((TM, D_MODEL), rev),
                   _const_spec((1, D_MODEL)), _const_spec((1, D_MODEL)), _const_spec((1, 2 * D_FF)),
                   _const_spec((3, 2 * D_FF))],
        scratch_shapes=[pltpu.VMEM((N_SHARD, D_MODEL, SHARD_FF), BF16), pltpu.VMEM((D_FF, D_MODEL), BF16),
                        pltpu.VMEM((HALO, 2 * D_FF), F32), pltpu.SemaphoreType.DMA((2,))],
        compiler_params=_params(),
    )(dx2, y, g4, up, c1, c2, cw, wdown, wup, x1, g3)


def _mix_out_bwd(dx1, cat, g2, wout, og, gg, gnorm):
    T = dx1.shape[0]

    def body(dx1_ref, cat_ref, g2_ref, wout_ref, og_ref, gg_ref, gn_ref,
             dmix_ref, dog_ref, dgg_ref, dosw_ref, dg2_ref, dgn_ref):
        @pl.when(pl.program_id(0) == 0)
        def _():
            dg2_ref[...] = jnp.zeros_like(dg2_ref)
            dgn_ref[...] = jnp.zeros_like(dgn_ref)

        dx1 = dx1_ref[...]
        mix = _mm(cat_ref[...], wout_ref[...])
        r2 = lax.rsqrt(jnp.mean(mix * mix, axis=-1, keepdims=True) + EPS)
        dg2_ref[...] = dg2_ref[...] + jnp.sum(dx1 * mix * r2, axis=0, keepdims=True)
        dmix = _rms_bwd(mix, r2, g2_ref[...], dx1).astype(BF16)
        dmix_ref[...] = dmix
        dcat = _mm_nt(dmix, wout_ref[...])
        dosw_ref[...] = dcat[:, GLA_V:].astype(BF16)
        gn = gn_ref[...]
        dgn = jnp.zeros((1, LANES), F32)
        for j in range(GLA_HEADS):
            hs = slice(128 * j, 128 * j + 128)
            o = og_ref[:, hs]
            r = lax.rsqrt(jnp.mean(o * o, axis=-1, keepdims=True) + EPS)
            gate = gg_ref[:, hs]
            sg = _sigmoid(gate)
            dgated = dcat[:, hs]
            dnorm = dgated * (gate * sg)
            dgg_ref[:, hs] = (dgated * (o * r * gn) * (sg * (1.0 + gate * (1.0 - sg)))).astype(BF16)
            dgn = dgn + jnp.sum(dnorm * o * r, axis=0, keepdims=True)
            dog_ref[:, hs] = _rms_bwd(o, r, gn, dnorm)
        dgn_ref[...] = dgn_ref[...] + dgn

    return pl.pallas_call(
        body, name="mix_out_bwd", grid=(T // TM,),
        out_shape=[jax.ShapeDtypeStruct((T, D_MODEL), BF16), jax.ShapeDtypeStruct((T, GLA_V), F32),
                   jax.ShapeDtypeStruct((T, GLA_V), BF16), jax.ShapeDtypeStruct((T, SWA_Q), BF16),
                   jax.ShapeDtypeStruct((1, D_MODEL), F32), jax.ShapeDtypeStruct((1, LANES), F32)],
        in_specs=[_row_spec(TM, D_MODEL), _row_spec(TM, D_MODEL), _const_spec((1, D_MODEL)),
                  _const_spec((D_MODEL, D_MODEL)), _row_spec(TM, GLA_V), _row_spec(TM, GLA_V), _const_spec((1, LANES))],
        out_specs=[_row_spec(TM, D_MODEL), _row_spec(TM, GLA_V), _row_spec(TM, GLA_V), _row_spec(TM, SWA_Q),
                   _const_spec((1, D_MODEL)), _const_spec((1, LANES))],
        compiler_params=_params(),
    )(dx1, cat, g2, wout, og, gg, gnorm)


def _proj_bwd(x, g1, wp, gup, glr, dq, dk, dv, dgg, dsq, dkd, dvd, dz, rc, rsa, rsb, dx1):
    T = x.shape[0]

    def body(x_ref, g1_ref, wp_hbm, gup_ref, glr_ref, dq_ref, dk_ref, dv_ref, dgg_ref, dsq_ref, dkd_ref, dvd_ref,
             dz_ref, rc_ref, rsa_ref, rsb_ref, dx1_ref, dx_ref, dp_ref, dg1_ref, dgup_ref, dgb_ref, wp_v, sem):
        _load_once(wp_hbm, wp_v, sem)

        @pl.when(pl.program_id(0) == 0)
        def _():
            dg1_ref[...] = jnp.zeros_like(dg1_ref)
            dgup_ref[...] = jnp.zeros_like(dgup_ref)
            dgb_ref[...] = jnp.zeros_like(dgb_ref)

        rc_, rsa_, rsb_ = rc_ref[...], rsa_ref[...], rsb_ref[...]
        dp_ref[:, 0:256] = dq_ref[...]
        dp_ref[:, 256:512] = dk_ref[...]
        dp_ref[:, 512:1024] = dv_ref[...]
        dp_ref[:, 1024:1536] = dgg_ref[...]
        for s in range(4):
            cs = slice(128 * s, 128 * s + 128)
            dp_ref[:, 1536 + 128 * s:1664 + 128 * s] = _rotate_bwd(dsq_ref[:, cs], rc_, rsa_, rsb_).astype(BF16)
        first = _iota((TM, LANES), 1) < 64
        dk0 = dkd_ref[:, 0:128]
        dk1 = dkd_ref[:, 128:256]
        dkr = jnp.where(first, dk0 + pltpu.roll(dk0, 64, 1), dk1 + pltpu.roll(dk1, 64, 1))
        dp_ref[:, 2048:2176] = _rotate_bwd(dkr, rc_, rsa_, rsb_).astype(BF16)
        dv0 = dvd_ref[:, 0:128]
        dv1 = dvd_ref[:, 128:256]
        dp_ref[:, 2176:2304] = jnp.where(first, dv0 + pltpu.roll(dv0, 64, 1), dv1 + pltpu.roll(dv1, 64, 1)).astype(BF16)
        dz = dz_ref[...]
        dzb = dz.astype(BF16)
        dp_ref[:, 2304:2432] = _mm_nt(dzb, gup_ref[...]).astype(BF16)
        dgup_ref[...] = dgup_ref[...] + _mm_tn(glr_ref[...], dzb)
        dgb_ref[...] = dgb_ref[...] + jnp.sum(dz, axis=0, keepdims=True)
        dh1 = _mm_nt(dp_ref[...], wp_v[...])
        xt = x_ref[...]
        r = lax.rsqrt(jnp.mean(xt * xt, axis=-1, keepdims=True) + EPS)
        dg1_ref[...] = dg1_ref[...] + jnp.sum(dh1 * xt * r, axis=0, keepdims=True)
        dx_ref[...] = dx1_ref[...] + _rms_bwd(xt, r, g1_ref[...], dh1)

    row = lambda cols: _row_spec(TM, cols)
    return pl.pallas_call(
        body, name="proj_bwd", grid=(T // TM,),
        out_shape=[jax.ShapeDtypeStruct((T, D_MODEL), F32), jax.ShapeDtypeStruct((T, IN_WIDTH_PAD), BF16),
                   jax.ShapeDtypeStruct((1, D_MODEL), F32), jax.ShapeDtypeStruct((LANES, GLA_QK), F32),
                   jax.ShapeDtypeStruct((1, GLA_QK), F32)],
        in_specs=[row(D_MODEL), _const_spec((1, D_MODEL)), _any_spec(), _const_spec((LANES, GLA_QK)), row(LANES),
                  row(GLA_QK), row(GLA_QK), row(GLA_V), row(GLA_V), row(SWA_Q), row(256), row(256), row(GLA_QK),
                  row(LANES), row(LANES), row(LANES), row(D_MODEL)],
        out_specs=[row(D_MODEL), row(IN_WIDTH_PAD), _const_spec((1, D_MODEL)), _const_spec((LANES, GLA_QK)),
                   _const_spec((1, GLA_QK))],
        scratch_shapes=[pltpu.VMEM((D_MODEL, IN_WIDTH_PAD), BF16), pltpu.SemaphoreType.DMA],
        compiler_params=_params(),
    )(x, g1, wp, gup, glr, dq, dk, dv, dgg, dsq, dkd, dvd, dz, rc, rsa, rsb, dx1)


def _matmul_tn(a, b, tn, name, column_blocks_major=False):
    T, M = a.shape
    N = b.shape[1]
    tk = next(t for t in (1024, 512, TM) if T % t == 0)
    nk = T // tk
    if column_blocks_major:
        out_shape = jax.ShapeDtypeStruct((N // tn, M, tn), F32)
        out_spec = pl.BlockSpec((None, M, tn), lambda j, kk: (j, 0, 0))
    else:
        out_shape = jax.ShapeDtypeStruct((M, N), F32)
        out_spec = pl.BlockSpec((M, tn), lambda j, kk: (0, j))

    def body(a_ref, b_ref, o_ref):
        kk = pl.program_id(1)

        @pl.when(kk == 0)
        def _():
            o_ref[...] = jnp.zeros_like(o_ref)

        o_ref[...] = o_ref[...] + _mm_tn(a_ref[...], b_ref[...])

    return pl.pallas_call(
        body, name=name, grid=(N // tn, nk), out_shape=out_shape,
        in_specs=[pl.BlockSpec((tk, M), lambda j, kk: (kk, 0)), pl.BlockSpec((tk, tn), lambda j, kk: (kk, j))],
        out_specs=out_spec,
        compiler_params=_params(),
    )(a, b)


def _adamw(w, g, m, v, rows, name):
    R, C = w.shape

    def body(w_ref, g_ref, m_ref, v_ref, d_ref, m2_ref, v2_ref):
        g_ = g_ref[...]
        m2 = ADAM_B1 * m_ref[...] + (1.0 - ADAM_B1) * g_
        v2 = ADAM_B2 * v_ref[...] + (1.0 - ADAM_B2) * (g_ * g_)
        m_hat = m2 / (1.0 - ADAM_B1 ** ADAM_STEP)
        v_hat = v2 / (1.0 - ADAM_B2 ** ADAM_STEP)
        d_ref[...] = -ADAM_LR * (m_hat / (jnp.sqrt(v_hat) + ADAM_EPS) + ADAM_WD * w_ref[...])
        m2_ref[...] = m2
        v2_ref[...] = v2

    spec = pl.BlockSpec((rows, C), lambda i: (i, 0))
    return pl.pallas_call(
        body, name=name, grid=(R // rows,), out_shape=[jax.ShapeDtypeStruct((R, C), F32)] * 3,
        in_specs=[spec] * 4, out_specs=[spec] * 3, compiler_params=_params(),
    )(w, g, m, v)


def _place():
    x, y, c = lax.axis_index("x"), lax.axis_index("y"), lax.axis_index("c")
    chips = [(1 - x, y), (x, 1 - y), (1 - x, 1 - y)]
    return x, y, c, chips


class _staged_copies:
    def __init__(self, srcs, dsts, stage, sems):
        n = len(srcs)
        self.loads = [pltpu.make_async_copy(srcs[k], stage[k], sems.at[k]) for k in range(n)]
        self.stores = [pltpu.make_async_copy(stage[k], dsts[k], sems.at[n + k]) for k in range(n)]

    def load(self):
        for cp in self.loads:
            cp.start()

    def store(self):
        for ld, st in zip(self.loads, self.stores):
            ld.wait()
            st.start()

    def finish(self):
        for cp in self.stores:
            cp.wait()


def _allgather_shards(parts, unit_rows):
    n = len(parts)
    units = [(k, r, unit_rows[k]) for k in range(n) for r in range(0, parts[k].shape[0] // 2, unit_rows[k])]
    nu = len(units)

    def body(*refs):
        ins, outs, stage = refs[:n], refs[n:2 * n], refs[2 * n:3 * n]
        send_sems, recv_sems, local_sems = refs[3 * n:]
        x, y, c, chips = _place()
        sibling = (x, y, 1 - c)
        own = _staged_copies(ins, [o.at[2 * x + y] for o in outs], stage, local_sems)

        def block(i, px, py, half):
            k, r, u = units[i]
            return outs[k].at[2 * px + py, pl.ds(half * (parts[k].shape[0] // 2) + r, u), :]

        def copy(i, j, px, py, half, to, src=None):
            return pltpu.make_async_remote_copy(
                src_ref=block(i, px, py, half) if src is None else src, dst_ref=block(i, px, py, half),
                send_sem=send_sems.at[nu * j + i], recv_sem=recv_sems.at[nu * j + i], device_id=to, device_id_type=MESH)

        own.load()
        first, passed = [], []
        for i, (k, r, u) in enumerate(units):
            for j, chip in enumerate(chips):
                src = ins[k].at[pl.ds(c * (parts[k].shape[0] // 2) + r, u), :]
                first.append(copy(i, j, x, y, c, (*chip, c), src=src))
                first[-1].start()
        own.store()
        for i in range(nu):
            for j, chip in enumerate(chips):
                copy(i, j, *chip, c, (x, y, c)).wait_recv()
                passed.append(copy(i, 3 + j, *chip, c, sibling))
                passed[-1].start()
        for i in range(nu):
            for j, chip in enumerate(chips):
                copy(i, 3 + j, *chip, 1 - c, (x, y, c)).wait_recv()
        for cp in first + passed:
            cp.wait_send()
        own.finish()

    return pl.pallas_call(
        body, name="allgather_shards", out_shape=[jax.ShapeDtypeStruct((N_SHARD,) + p.shape, p.dtype) for p in parts],
        in_specs=[_any_spec()] * n, out_specs=[_any_spec()] * n,
        scratch_shapes=[pltpu.VMEM(p.shape, p.dtype) for p in parts] + [
            pltpu.SemaphoreType.DMA((6 * nu,)), pltpu.SemaphoreType.DMA((6 * nu,)), pltpu.SemaphoreType.DMA((2 * n,))],
        compiler_params=_params(),
    )(*parts)


def _d2d_pieces(rows, piece_rows):
    return [(r, piece_rows) for r in range(0, rows, piece_rows)]


def _rs_pair_swap(arrs, piece_rows, name):
    n = len(arrs)

    def body(*refs):
        ins, outs = refs[:n], refs[n:2 * n]
        send_sems, recv_sems = refs[2 * n:]
        x, y, c, _ = _place()
        sibling = (x, y, 1 - c)
        for k in range(n):
            H = arrs[k].shape[1] // 2
            for s in range(N_SHARD):
                for r, pr in _d2d_pieces(H, piece_rows[k]):
                    pltpu.make_async_remote_copy(
                        src_ref=ins[k].at[s, pl.ds((1 - c) * H + r, pr), :], dst_ref=outs[k].at[s, pl.ds(r, pr), :],
                        send_sem=send_sems.at[k], recv_sem=recv_sems.at[k], device_id=sibling, device_id_type=MESH).start()
        for k in range(n):
            H = arrs[k].shape[1] // 2
            whole = pltpu.make_async_remote_copy(
                src_ref=ins[k].at[:, pl.ds(0, H), :], dst_ref=outs[k], send_sem=send_sems.at[k], recv_sem=recv_sems.at[k],
                device_id=sibling, device_id_type=MESH)
            whole.wait_recv()
            whole.wait_send()

    return pl.pallas_call(
        body, name=name,
        out_shape=[jax.ShapeDtypeStruct((N_SHARD, a.shape[1] // 2, a.shape[2]), F32) for a in arrs],
        in_specs=[_any_spec()] * n, out_specs=[_any_spec()] * n,
        scratch_shapes=[pltpu.SemaphoreType.DMA((n,)), pltpu.SemaphoreType.DMA((n,))],
    )(*arrs)


def _rs_add_pair(a, got, core, rows, name):
    _, H, C = got.shape
    nb = H // rows

    def body(c_ref, a_ref, b_ref, o_ref):
        o_ref[...] = (a_ref[...] + b_ref[...]).astype(BF16)

    spec = pl.BlockSpec((1, rows, C), lambda s, r, c_ref: (s, r, 0))
    return pl.pallas_call(
        body, name=name, out_shape=jax.ShapeDtypeStruct(got.shape, BF16),
        grid_spec=pltpu.PrefetchScalarGridSpec(
            num_scalar_prefetch=1, grid=(N_SHARD, nb),
            in_specs=[pl.BlockSpec((1, rows, C), lambda s, r, c_ref: (s, c_ref[0] * nb + r, 0)), spec], out_specs=spec),
        compiler_params=_params(),
    )(core, a, got)


def _rs_chip_scatter(parts):
    n = len(parts)

    def body(*refs):
        ins, outs, stage = refs[:n], refs[n:2 * n], refs[2 * n:3 * n]
        send_sems, recv_sems, local_sems = refs[3 * n:]
        x, y, c, chips = _place()
        me = 2 * x + y
        own = _staged_copies([i.at[me] for i in ins], [o.at[me] for o in outs], stage, local_sems)
        own.load()
        sends = []
        for k in range(n):
            for j, (px, py) in enumerate(chips):
                sends.append(pltpu.make_async_remote_copy(
                    src_ref=ins[k].at[2 * px + py], dst_ref=outs[k].at[me], send_sem=send_sems.at[3 * k + j],
                    recv_sem=recv_sems.at[3 * k + j], device_id=(px, py, c), device_id_type=MESH))
                sends[-1].start()
        own.store()
        for k in range(n):
            for j, (px, py) in enumerate(chips):
                pltpu.make_async_remote_copy(
                    src_ref=ins[k].at[me], dst_ref=outs[k].at[2 * px + py], send_sem=send_sems.at[3 * k + j],
                    recv_sem=recv_sems.at[3 * k + j], device_id=(px, py, c), device_id_type=MESH).wait_recv()
        for cp in sends:
            cp.wait_send()
        own.finish()

    return pl.pallas_call(
        body, name="rs_chip_scatter", out_shape=[jax.ShapeDtypeStruct(p.shape, p.dtype) for p in parts],
        in_specs=[_any_spec()] * n, out_specs=[_any_spec()] * n,
        scratch_shapes=[pltpu.VMEM(p.shape[1:], p.dtype) for p in parts] + [
            pltpu.SemaphoreType.DMA((3 * n,)), pltpu.SemaphoreType.DMA((3 * n,)), pltpu.SemaphoreType.DMA((2 * n,))],
        compiler_params=_params(),
    )(*parts)


def _rs_sum_chips(parts, rows, name):
    _, H, C = parts.shape

    def body(p_ref, o_ref):
        o_ref[...] = ((p_ref[0].astype(F32) + p_ref[1].astype(F32)) + p_ref[2].astype(F32)) + p_ref[3].astype(F32)

    return pl.pallas_call(
        body, name=name, grid=(H // rows,), out_shape=jax.ShapeDtypeStruct((H, C), F32),
        in_specs=[pl.BlockSpec((N_SHARD, rows, C), lambda r: (0, r, 0))],
        out_specs=pl.BlockSpec((rows, C), lambda r: (r, 0)), compiler_params=_params(),
    )(parts)


def _rs_pair_share(halves, piece_rows):
    n = len(halves)

    def body(*refs):
        ins, outs, stage = refs[:n], refs[n:2 * n], refs[2 * n:3 * n]
        send_sems, recv_sems, local_sems = refs[3 * n:]
        x, y, c, _ = _place()
        sibling = (x, y, 1 - c)
        own = _staged_copies(ins, [outs[k].at[pl.ds(c * halves[k].shape[0], halves[k].shape[0]), :] for k in range(n)],
                             stage, local_sems)
        own.load()
        for k in range(n):
            H = halves[k].shape[0]
            for r, pr in _d2d_pieces(H, piece_rows[k]):
                pltpu.make_async_remote_copy(
                    src_ref=ins[k].at[pl.ds(r, pr), :], dst_ref=outs[k].at[pl.ds(c * H + r, pr), :],
                    send_sem=send_sems.at[k], recv_sem=recv_sems.at[k], device_id=sibling, device_id_type=MESH).start()
        own.store()
        for k in range(n):
            H = halves[k].shape[0]
            whole = pltpu.make_async_remote_copy(
                src_ref=ins[k], dst_ref=outs[k].at[pl.ds((1 - c) * H, H), :], send_sem=send_sems.at[k],
                recv_sem=recv_sems.at[k], device_id=sibling, device_id_type=MESH)
            whole.wait_recv()
            whole.wait_send()
        own.finish()

    return pl.pallas_call(
        body, name="rs_pair_share", out_shape=[jax.ShapeDtypeStruct((2 * h.shape[0], h.shape[1]), F32) for h in halves],
        in_specs=[_any_spec()] * n, out_specs=[_any_spec()] * n,
        scratch_shapes=[pltpu.VMEM(h.shape, F32) for h in halves] + [
            pltpu.SemaphoreType.DMA((n,)), pltpu.SemaphoreType.DMA((n,)), pltpu.SemaphoreType.DMA((2 * n,))],
        compiler_params=_params(),
    )(*halves)


_HBM = pl.BlockSpec(memory_space=pltpu.HBM)
_SEM = pl.BlockSpec(memory_space=pltpu.SEMAPHORE)
_EFFECT = pltpu.SideEffectType.DATAFLOW_SIDE_EFFECTING


def _gather_plan(srcs, lands, x, y, c, chips):
    plan = []
    for k in range(len(srcs)):
        H = srcs[k].shape[0] // 2
        for px, py in chips:
            plan.append((srcs[k].at[pl.ds(c * H, H), :], lands[k].at[2 * x + y, pl.ds(c * H, H), :], (px, py, c),
                         lands[k].at[2 * px + py, pl.ds(c * H, H), :]))
    return plan


def _scatter_plan(srcs, lands, x, y, c, chips):
    plan = []
    for k in range(len(srcs)):
        for px, py in chips:
            plan.append((srcs[k].at[2 * px + py], lands[k].at[2 * x + y], (px, py, c), lands[k].at[2 * px + py]))
    return plan


def _ici_start(srcs, lands, make_plan, name):
    n = len(srcs)
    ncopy = 3 * n

    def body(*refs):
        ins, lnd = refs[:n], refs[n:2 * n]
        send_sems, recv_sems = refs[2 * n], refs[2 * n + 1]
        token = refs[-1]
        x, y, c, chips = _place()
        for i, (src, dst, peer, _) in enumerate(make_plan(ins, lnd, x, y, c, chips)):
            pltpu.make_async_remote_copy(src_ref=src, dst_ref=dst, send_sem=send_sems.at[i], recv_sem=recv_sems.at[i],
                                         device_id=peer, device_id_type=MESH).start()
        token[...] = jnp.zeros_like(token)

    arrays = list(srcs) + list(lands)
    return pl.pallas_call(
        body, name=name,
        out_shape=(pltpu.SemaphoreType.DMA((ncopy,)), pltpu.SemaphoreType.DMA((ncopy,)),
                   *[pltpu.HBM(a.shape, a.dtype) for a in arrays], jax.ShapeDtypeStruct((8, LANES), F32)),
        in_specs=[_HBM] * (2 * n), out_specs=(_SEM, _SEM, *[_HBM] * (2 * n), pl.BlockSpec(memory_space=pltpu.VMEM)),
        input_output_aliases={i: 2 + i for i in range(2 * n)},
        compiler_params=pltpu.CompilerParams(has_side_effects=_EFFECT),
    )(*[pltpu.with_memory_space_constraint(a, pltpu.HBM) for a in arrays])


def _ici_wait(started, after, make_plan, name):
    send_sems, recv_sems = started[0], started[1]
    arrays = list(started[2:-1])
    n = len(arrays) // 2

    def body(*refs):
        ins, lnd = refs[:n], refs[n:2 * n]
        send_sems, recv_sems = refs[2 * n], refs[2 * n + 1]
        x, y, c, chips = _place()
        for i, (src, _, peer, landed) in enumerate(make_plan(ins, lnd, x, y, c, chips)):
            cp = pltpu.make_async_remote_copy(src_ref=src, dst_ref=landed, send_sem=send_sems.at[i],
                                              recv_sem=recv_sems.at[i], device_id=peer, device_id_type=MESH)
            cp.wait_send()
            cp.wait_recv()

    outs = pl.pallas_call(
        body, name=name, out_shape=tuple(pltpu.HBM(a.shape, a.dtype) for a in arrays),
        in_specs=[_HBM] * (2 * n) + [_SEM, _SEM, pl.BlockSpec(memory_space=pl.ANY)], out_specs=tuple([_HBM] * (2 * n)),
        input_output_aliases={i: i for i in range(2 * n)},
        compiler_params=pltpu.CompilerParams(has_side_effects=_EFFECT),
    )(*arrays, send_sems, recv_sems, after)
    return list(outs[n:])


def _gather_finish(parts, lands):
    n = len(parts)

    def body(*refs):
        ins, lnd, outs, stage = refs[:n], refs[n:2 * n], refs[2 * n:3 * n], refs[3 * n:4 * n]
        send_sems, recv_sems, local_sems = refs[4 * n:]
        x, y, c, chips = _place()
        sibling = (x, y, 1 - c)
        own = _staged_copies(ins, [o.at[2 * x + y] for o in outs], stage, local_sems)
        own.load()
        sends = []
        for k in range(n):
            H = parts[k].shape[0] // 2
            for j, (px, py) in enumerate(chips):
                half = outs[k].at[2 * px + py, pl.ds(c * H, H), :]
                sends.append(pltpu.make_async_remote_copy(src_ref=half, dst_ref=half, send_sem=send_sems.at[3 * k + j],
                                                          recv_sem=recv_sems.at[3 * k + j], device_id=sibling, device_id_type=MESH))
                sends[-1].start()
        own.store()
        for k in range(n):
            H = parts[k].shape[0] // 2
            for j, (px, py) in enumerate(chips):
                other = outs[k].at[2 * px + py, pl.ds((1 - c) * H, H), :]
                pltpu.make_async_remote_copy(src_ref=other, dst_ref=other, send_sem=send_sems.at[3 * k + j],
                                             recv_sem=recv_sems.at[3 * k + j], device_id=sibling, device_id_type=MESH).wait_recv()
        for cp in sends:
            cp.wait_send()
        own.finish()

    return pl.pallas_call(
        body, name="gather_finish", out_shape=[jax.ShapeDtypeStruct(l.shape, l.dtype) for l in lands],
        in_specs=[_any_spec()] * (2 * n), out_specs=[_any_spec()] * n,
        input_output_aliases={n + k: k for k in range(n)},
        scratch_shapes=[pltpu.VMEM(p.shape, p.dtype) for p in parts] + [
            pltpu.SemaphoreType.DMA((3 * n,)), pltpu.SemaphoreType.DMA((3 * n,)), pltpu.SemaphoreType.DMA((2 * n,))],
        compiler_params=_params(),
    )(*parts, *lands)


def _place_own(part, chip, rows, name):
    _, H, C = part.shape

    def body(chip_ref, p_ref, o_ref):
        o_ref[...] = p_ref[...]

    spec = pl.BlockSpec((1, rows, C), lambda r, chip_ref: (chip_ref[0], r, 0))
    return pl.pallas_call(
        body, name=name, out_shape=jax.ShapeDtypeStruct(part.shape, part.dtype),
        grid_spec=pltpu.PrefetchScalarGridSpec(num_scalar_prefetch=1, grid=(H // rows,), in_specs=[spec], out_specs=spec),
        compiler_params=_params(),
    )(chip, part)


def _allreduce_small(vec):
    def body(v_ref, o_ref, all_ref, send_sems, recv_sems):
        x, y, c, _ = _place()
        me = 4 * x + 2 * y + c
        all_ref[me] = v_ref[...]
        sends = []
        for k in range(1, 8):
            kx, ky, kc = (k >> 2) & 1, (k >> 1) & 1, k & 1
            peer = (x ^ kx, y ^ ky, c ^ kc)
            cp = pltpu.make_async_remote_copy(
                src_ref=v_ref, dst_ref=all_ref.at[me], send_sem=send_sems.at[k - 1], recv_sem=recv_sems.at[k - 1],
                device_id=peer, device_id_type=MESH)
            cp.start()
            sends.append(cp)
        for k in range(1, 8):
            kx, ky, kc = (k >> 2) & 1, (k >> 1) & 1, k & 1
            src = 4 * (x ^ kx) + 2 * (y ^ ky) + (c ^ kc)
            pltpu.make_async_remote_copy(
                src_ref=v_ref, dst_ref=all_ref.at[src], send_sem=send_sems.at[k - 1], recv_sem=recv_sems.at[k - 1],
                device_id=(x, y, c), device_id_type=MESH).wait_recv()
        for cp in sends:
            cp.wait_send()
        total = all_ref[0]
        for d in range(1, 8):
            total = total + all_ref[d]
        o_ref[...] = total

    vm = pl.BlockSpec(memory_space=pltpu.VMEM)
    return pl.pallas_call(
        body, name="allreduce_small", out_shape=jax.ShapeDtypeStruct(vec.shape, F32), in_specs=[vm], out_specs=vm,
        scratch_shapes=[pltpu.VMEM((8,) + vec.shape, F32), pltpu.SemaphoreType.DMA((7,)), pltpu.SemaphoreType.DMA((7,))],
    )(vec)


BIG_NAMES = ("w_in", "w_out", "w_up", "w_down")
MATRIX_NAMES = BIG_NAMES + ("gla_gate_up", "conv_w")
LATE_NAMES = ("w_out", "w_up", "w_down")
RS_LABELS = BIG_NAMES + ("small",)
GATE_SHARD = (16, GLA_QK // N_SHARD)
CONVW_SHARD = (3, SHARD_FF)
SMALL_W_ROWS = 96
SMALL_G_ROWS = 64
PIECE_ROWS = (128, 128, 64, 88, SMALL_G_ROWS // 2)
ADD_ROWS = (256, 128, 256, 176, SMALL_G_ROWS // 2)
FIRST_UNIT_ROWS = (256, SMALL_W_ROWS // 2)


def _pad_rows(flat, rows):
    return jnp.pad(flat, (0, rows * LANES - flat.shape[0])).reshape(rows, LANES)


def _pack_small_weights(gate_up, conv_w):
    bits = lax.bitcast_convert_type(conv_w, BF16)
    return _pad_rows(jnp.concatenate([gate_up.astype(BF16).reshape(-1), bits.reshape(-1)]), SMALL_W_ROWS)


def _unpack_small_weights(packed):
    flat = packed.reshape(N_SHARD, -1)
    n_gate = GATE_SHARD[0] * GATE_SHARD[1]
    n_conv = 2 * CONVW_SHARD[0] * CONVW_SHARD[1]
    gate = flat[:, :n_gate].reshape((N_SHARD,) + GATE_SHARD)
    conv = lax.bitcast_convert_type(flat[:, n_gate:n_gate + n_conv].reshape((N_SHARD,) + CONVW_SHARD + (2,)), F32)
    return (jnp.transpose(gate, (1, 0, 2)).reshape(16, GLA_QK), jnp.transpose(conv, (1, 0, 2)).reshape(3, 2 * D_FF))


def _pack_small_grads(dgate, dconv):
    rows = []
    for s in range(N_SHARD):
        g = dgate[:, GATE_SHARD[1] * s:GATE_SHARD[1] * (s + 1)].reshape(-1)
        cw = dconv[:, SHARD_FF * s:SHARD_FF * (s + 1)].reshape(-1)
        rows.append(_pad_rows(jnp.concatenate([g, cw]), SMALL_G_ROWS))
    return jnp.stack(rows)


def _unpack_small_grads(packed):
    flat = packed.reshape(-1)
    n_gate = GATE_SHARD[0] * GATE_SHARD[1]
    n_conv = CONVW_SHARD[0] * CONVW_SHARD[1]
    return flat[:n_gate].reshape(GATE_SHARD), flat[n_gate:n_gate + n_conv].reshape(CONVW_SHARD)


def _permute_w_in(w):
    pad = jnp.zeros((w.shape[0], IN_WIDTH_PAD - IN_WIDTH), w.dtype)
    return jnp.concatenate([w[:, 0:1024], w[:, 1040:2320], w[:, 1024:1040], pad], axis=1)


def _unpermute_w_in(wp):
    return jnp.concatenate([wp[:, 0:1024], wp[:, 2304:2320], wp[:, 1024:2304]], axis=1)


def _rope_tables(positions):
    half = ROPE_DIM // 2
    inv_freq = ROPE_THETA ** (-jnp.arange(half, dtype=F32) * (2.0 / ROPE_DIM))
    d = jnp.arange(LANES) % SWA_HD
    freq = jnp.where(d < ROPE_DIM, inv_freq[d % half], 0.0)
    ang = positions.astype(F32)[:, None] * freq
    cos, sin = jnp.cos(ang), jnp.sin(ang)
    return cos, jnp.where(d < half, -sin, 0.0), jnp.where((d >= half) & (d < ROPE_DIM), sin, 0.0)


SMALL_NAMES = (("pre_mix_norm", 1024), ("gla_gate_bias", 256), ("gla_out_norm", 128), ("swa_sinks", 8),
               ("post_mix_norm", 1024), ("pre_ffn_norm", 1024), ("conv_b", 5632), ("post_ffn_norm", 1024))


def _pack_small(vals, loss):
    parts = [vals[n].reshape(-1) for n, _ in SMALL_NAMES] + [loss.reshape(1)]
    flat = jnp.concatenate(parts)
    return jnp.pad(flat, (0, SMALL_ROWS * LANES - flat.shape[0])).reshape(SMALL_ROWS, LANES)


def _unpack_small(packed):
    flat = packed.reshape(-1)
    out, off = {}, 0
    for n, size in SMALL_NAMES:
        out[n] = flat[off:off + size].reshape(1, size)
        off += size
    return out, flat[off]


def _local_step(x, positions, target, w, small, late_weights, early_grads):
    rc, rsa, rsb = _rope_tables(positions)
    wp = w["wp"]
    gup = jnp.pad(w["gla_gate_up"], ((0, LANES - 16), (0, 0)))
    g1, g2, g3, g4 = (small[n] for n in ("pre_mix_norm", "post_mix_norm", "pre_ffn_norm", "post_ffn_norm"))
    gbias, gnorm, cb = small["gla_gate_bias"], small["gla_out_norm"], small["conv_b"]
    sinks = small["swa_sinks"].reshape(-1)
    cw = w["conv_w"]

    h1, q, k, v, la, gg, sq, kd, vd, glr = _proj_fwd(x, g1, wp, gup, gbias, rc, rsa, rsb)
    og, s_all = _gla_fwd(q, k, v, la)
    osw = _swa_fwd(sq, kd, vd, sinks)
    w_out, w_up4, w_down = late_weights(osw)
    x1, cat = _mix_out_fwd(x, og, gg, osw, gnorm, w_out, g2)
    h2, up, act, c1, c2, y, dx2, loss = _ffn_fwd(x1, g3, w_up4, cw, cb, w_down, g4, target)

    dy, dup, dx1, dg4, dg3, dcb, dcw = _ffn_bwd(dx2, y, g4, up, c1, c2, cw, w_down, w_up4, x1, g3)
    zero = early_grads(_matmul_tn(h2, dup, SHARD_FF, "grad_w_up", column_blocks_major=True),
                       _matmul_tn(act, dy, D_MODEL, "grad_w_down"))
    dmix, dog, dgg, dosw, dg2, dgn = _mix_out_bwd(dx1, cat, g2 + zero, w_out, og, gg, gnorm)
    dsq, dkd, dvd, dsink = _swa_bwd(sq, kd, vd, sinks, dosw)
    dq, dk, dv, dz = _gla_bwd(q, k, v, la, s_all, dog)
    dx, dproj, dg1, dgup, dgb = _proj_bwd(x, g1, wp, gup, glr, dq, dk, dv, dgg, dsq, dkd, dvd, dz, rc, rsa, rsb, dx1)

    grads = {
        "wp": _matmul_tn(h1, dproj, IN_WIDTH_PAD, "grad_w_in"),
        "w_out": _matmul_tn(cat, dmix, D_MODEL, "grad_w_out"),
        "gla_gate_up": dgup[0:16],
        "conv_w": dcw,
    }
    small_grads = {
        "pre_mix_norm": dg1, "gla_gate_bias": dgb, "gla_out_norm": dgn, "swa_sinks": jnp.sum(dsink[:, 0].reshape(SWA_HEADS, SWA_BLOCK), axis=1).reshape(1, SWA_HEADS),
        "post_mix_norm": dg2, "pre_ffn_norm": dg3, "conv_b": dcb, "post_ffn_norm": dg4,
    }
    return loss[0, 0], dx, grads, small_grads


ADAM_ROWS = {"w_in": 256, "w_out": 256, "w_up": 256, "w_down": 176}
WEIGHT_ORDER = ("pre_mix_norm", "w_in", "gla_gate_up", "gla_gate_bias", "gla_out_norm", "swa_sinks", "w_out",
                "post_mix_norm", "pre_ffn_norm", "w_up", "conv_w", "conv_b", "w_down", "post_ffn_norm")
TINY_ROWS = 152


def _pack_tiny(vals):
    flat = jnp.concatenate([vals[n].reshape(-1) for n in TINY_NAMES])
    return jnp.pad(flat, (0, TINY_ROWS * LANES - flat.shape[0])).reshape(TINY_ROWS, LANES)


TINY_NAMES = tuple(n for n, _ in SMALL_NAMES) + ("gla_gate_up", "conv_w")
TINY_SHAPES = {**{n: (1, s) for n, s in SMALL_NAMES}, "gla_gate_up": (16, 64), "conv_w": (3, 1408)}


def _unpack_tiny(packed):
    flat = packed.reshape(-1)
    out, off = {}, 0
    for n in TINY_NAMES:
        shape = TINY_SHAPES[n]
        size = shape[0] * shape[1]
        out[n] = flat[off:off + size].reshape(shape)
        off += size
    return out


def kernel(x, positions, pre_mix_norm, w_in, gla_gate_up, gla_gate_bias, gla_out_norm, swa_sinks, w_out, post_mix_norm, pre_ffn_norm, w_up, conv_w, conv_b, w_down, post_ffn_norm, loss_target, m_pre_mix_norm, m_w_in, m_gla_gate_up, m_gla_gate_bias, m_gla_out_norm, m_swa_sinks, m_w_out, m_post_mix_norm, m_pre_ffn_norm, m_w_up, m_conv_w, m_conv_b, m_w_down, m_post_ffn_norm, v_pre_mix_norm, v_w_in, v_gla_gate_up, v_gla_gate_bias, v_gla_out_norm, v_swa_sinks, v_w_out, v_post_mix_norm, v_pre_ffn_norm, v_w_up, v_conv_w, v_conv_b, v_w_down, v_post_ffn_norm):
    weights = dict(pre_mix_norm=pre_mix_norm, w_in=w_in, gla_gate_up=gla_gate_up, gla_gate_bias=gla_gate_bias,
                   gla_out_norm=gla_out_norm, swa_sinks=swa_sinks, w_out=w_out, post_mix_norm=post_mix_norm,
                   pre_ffn_norm=pre_ffn_norm, w_up=w_up, conv_w=conv_w, conv_b=conv_b, w_down=w_down,
                   post_ffn_norm=post_ffn_norm)
    mom = dict(pre_mix_norm=m_pre_mix_norm, w_in=m_w_in, gla_gate_up=m_gla_gate_up, gla_gate_bias=m_gla_gate_bias,
               gla_out_norm=m_gla_out_norm, swa_sinks=m_swa_sinks, w_out=m_w_out, post_mix_norm=m_post_mix_norm,
               pre_ffn_norm=m_pre_ffn_norm, w_up=m_w_up, conv_w=m_conv_w, conv_b=m_conv_b, w_down=m_w_down,
               post_ffn_norm=m_post_ffn_norm)
    var = dict(pre_mix_norm=v_pre_mix_norm, w_in=v_w_in, gla_gate_up=v_gla_gate_up, gla_gate_bias=v_gla_gate_bias,
               gla_out_norm=v_gla_out_norm, swa_sinks=v_swa_sinks, w_out=v_w_out, post_mix_norm=v_post_mix_norm,
               pre_ffn_norm=v_pre_ffn_norm, w_up=v_w_up, conv_w=v_conv_w, conv_b=v_conv_b, w_down=v_w_down,
               post_ffn_norm=v_post_ffn_norm)
    weights, mom, var = ({n: a[0] if a.ndim == 3 else a for n, a in d.items()} for d in (weights, mom, var))

    core = lax.axis_index("c").astype(jnp.int32).reshape(1)
    chip = (2 * lax.axis_index("x") + lax.axis_index("y")).astype(jnp.int32).reshape(1)
    small = {n: weights[n] for n, _ in SMALL_NAMES}

    win4, small4 = _allgather_shards(
        [weights["w_in"].astype(BF16), _pack_small_weights(weights["gla_gate_up"], weights["conv_w"])], FIRST_UNIT_ROWS)
    gate_full, convw_full = _unpack_small_weights(small4)
    first = {"wp": _permute_w_in(jnp.transpose(win4, (1, 0, 2)).reshape(D_MODEL, IN_WIDTH)),
             "gla_gate_up": gate_full, "conv_w": convw_full}
    late_shards = [weights[n].astype(BF16) for n in LATE_NAMES]
    gathering = _ici_start(late_shards, [lax.empty((N_SHARD,) + s.shape, BF16) for s in late_shards], _gather_plan, "gather_start")
    small["pre_mix_norm"] = small["pre_mix_norm"] + gathering[-1][0, 0]

    def late_weights(after):
        wout4, wup4, wdown4 = _gather_finish(late_shards, _ici_wait(gathering, after, _gather_plan, "gather_wait"))
        return wout4.reshape(D_MODEL, D_MODEL), wup4, wdown4.reshape(D_FF, D_MODEL)

    def pair_partials(arrs, which):
        got = _rs_pair_swap(arrs, [PIECE_ROWS[i] for i in which], "rs_pair_swap_" + RS_LABELS[which[0]])
        return [_rs_add_pair(a, g, core, ADD_ROWS[i], "rs_add_pair_" + RS_LABELS[i]) for a, g, i in zip(arrs, got, which)]

    early = {}

    def early_grads(g_up4, g_down):
        parts = pair_partials([g_up4, g_down.reshape(N_SHARD, D_FF // N_SHARD, D_MODEL)], (2, 3))
        lands = [_place_own(p, chip, ADD_ROWS[i], "rs_place_own_" + RS_LABELS[i]) for p, i in zip(parts, (2, 3))]
        early["scatter"] = _ici_start(parts, lands, _scatter_plan, "rs_scatter_start")
        return early["scatter"][-1][0, 0]

    loss, dx, grads, small_grads = _local_step(x[0], positions[0], loss_target[0], first, small, late_weights, early_grads)

    landed_ffn = _ici_wait(early["scatter"], dx, _scatter_plan, "rs_scatter_wait")
    rest = [jnp.transpose(_unpermute_w_in(grads["wp"]).reshape(D_MODEL, N_SHARD, IN_WIDTH // N_SHARD), (1, 0, 2)),
            grads["w_out"].reshape(N_SHARD, D_MODEL // N_SHARD, D_MODEL),
            _pack_small_grads(grads["gla_gate_up"], grads["conv_w"])]
    landed_rest = _rs_chip_scatter(pair_partials(rest, (0, 1, 4)))
    landed = [landed_rest[0], landed_rest[1], landed_ffn[0], landed_ffn[1], landed_rest[2]]
    halves = [_rs_sum_chips(p, rows, "rs_sum_chips_" + n) for p, rows, n in zip(landed, ADD_ROWS, RS_LABELS)]
    reduced = _rs_pair_share(halves, PIECE_ROWS)
    g_small, loss_sum = _unpack_small(_allreduce_small(_pack_small(small_grads, loss)))
    g_gate, g_convw = _unpack_small_grads(reduced[4])
    g_all = {**g_small, **dict(zip(BIG_NAMES, reduced[:4])), "gla_gate_up": g_gate, "conv_w": g_convw}

    delta, new_m, new_v = {}, {}, {}
    for n in BIG_NAMES:
        delta[n], new_m[n], new_v[n] = _adamw(weights[n], g_all[n], mom[n], var[n], ADAM_ROWS[n], "adamw_" + n)
    tiny = _adamw(*(_pack_tiny({n: d[n] for n in TINY_NAMES}) for d in (weights, g_all, mom, var)), TINY_ROWS, "adamw_small")
    for res, packed in zip((delta, new_m, new_v), tiny):
        res.update(_unpack_tiny(packed))

    def lead(n, a):
        return a[None] if n in MATRIX_NAMES else a

    outs = [loss_sum, dx[None]]
    for d in (g_all, delta, new_m, new_v):
        outs.extend(lead(n, d[n]) for n in WEIGHT_ORDER)
    return tuple(outs)
```

```python
import functools

import jax
import jax.numpy as jnp
from jax import lax
from jax.experimental import pallas as pl
from jax.experimental.pallas import tpu as pltpu

F32 = jnp.float32
BF16 = jnp.bfloat16
MESH = pl.DeviceIdType.MESH

D_MODEL = 1024
GLA_HEADS = 4
GLA_DK = 64
GLA_DV = 128
GLA_TAU = 16.0
GLA_CHUNK = 64
SWA_HEADS = 8
SWA_HD = 64
SWA_BLOCK = 128
ROPE_THETA = 500000.0
ROPE_DIM = 16
D_FF = 2816
EPS = 1e-6
GLA_QK = 256
GLA_V = 512
SWA_Q = 512
SWA_KV = 128
IN_WIDTH = 2320
IN_WIDTH_PAD = 2432
N_SHARD = 4

ADAM_LR = 0.001
ADAM_B1 = 0.9
ADAM_B2 = 0.999
ADAM_EPS = 1e-08
ADAM_WD = 0.01
ADAM_STEP = 10

LANES = 128
VMEM_LIMIT = 56 * 1024 * 1024
TM = 256
SHARD_FF = 2 * D_FF // N_SHARD
FF_PIECES = ((0, 512), (512, 512), (1024, 384))
GLA_BLOCK = 256


def _params(**kw):
    return pltpu.CompilerParams(vmem_limit_bytes=VMEM_LIMIT, **kw)


def _mm(a, b):
    return lax.dot_general(a.astype(BF16), b.astype(BF16), (((1,), (0,)), ((), ())), preferred_element_type=F32)


def _mm_nt(a, b):
    return lax.dot_general(a.astype(BF16), b.astype(BF16), (((1,), (1,)), ((), ())), preferred_element_type=F32)


def _mm_tn(a, b):
    return lax.dot_general(a.astype(BF16), b.astype(BF16), (((0,), (0,)), ((), ())), preferred_element_type=F32)


def _mm_f32(a, b):
    return lax.dot_general(a, b, (((1,), (0,)), ((), ())), preferred_element_type=F32, precision=lax.Precision.HIGHEST)


def _iota(shape, dim):
    return lax.broadcasted_iota(jnp.int32, shape, dim)


def _sigmoid(x):
    return 1.0 / (1.0 + jnp.exp(-x))


def _gelu_parts(x):
    c = 0.7978845608028654
    x2 = x * x
    t = jnp.tanh(c * (x + 0.044715 * (x2 * x)))
    cdf = 0.5 * (1.0 + t)
    dcdf = 0.5 * (1.0 - t * t) * c * (1.0 + 3.0 * 0.044715 * x2)
    return x * cdf, cdf + x * dcdf


def _rms_bwd(v, r, g, dout):
    gd = g * dout
    return r * gd - v * (r * r * r) * jnp.mean(v * gd, axis=-1, keepdims=True)


def _row_spec(tm, cols):
    return pl.BlockSpec((tm, cols), lambda i: (i, 0))


def _const_spec(shape):
    return pl.BlockSpec(shape, lambda i: (0,) * len(shape))


def _any_spec():
    return pl.BlockSpec(memory_space=pl.ANY)


def _load_once(src_hbm, dst_vmem, sem):
    @pl.when(pl.program_id(0) == 0)
    def _():
        cp = pltpu.make_async_copy(src_hbm, dst_vmem, sem)
        cp.start()
        cp.wait()


def _rotate(v, rc, rsa, rsb):
    return v * rc + pltpu.roll(v, 120, 1) * rsa + pltpu.roll(v, 8, 1) * rsb


def _rotate_bwd(dv, rc, rsa, rsb):
    return dv * rc + pltpu.roll(dv * rsa, 8, 1) + pltpu.roll(dv * rsb, 120, 1)


def _proj_fwd(x, g1, wp, gup, gbias, rc, rsa, rsb):
    T = x.shape[0]

    def body(x_ref, g1_ref, wp_hbm, gup_ref, gb_ref, rc_ref, rsa_ref, rsb_ref,
             h1_ref, q_ref, k_ref, v_ref, la_ref, gg_ref, sq_ref, kd_ref, vd_ref, glr_ref, wp_v, sem):
        _load_once(wp_hbm, wp_v, sem)
        xt = x_ref[...]
        r = lax.rsqrt(jnp.mean(xt * xt, axis=-1, keepdims=True) + EPS)
        h = (xt * r * g1_ref[...]).astype(BF16)
        h1_ref[...] = h
        q_ref[...] = _mm(h, wp_v[:, 0:256])
        k_ref[...] = _mm(h, wp_v[:, 256:512])
        v_ref[...] = _mm(h, wp_v[:, 512:1024]).astype(BF16)
        gg_ref[...] = _mm(h, wp_v[:, 1024:1536]).astype(BF16)
        glr = _mm(h, wp_v[:, 2304:2432]).astype(BF16)
        glr_ref[...] = glr
        z = _mm(glr, gup_ref[...]) + gb_ref[...]
        la_ref[...] = (jnp.minimum(z, 0.0) - jnp.log1p(jnp.exp(-jnp.abs(z)))) * (1.0 / GLA_TAU)
        rc_, rsa_, rsb_ = rc_ref[...], rsa_ref[...], rsb_ref[...]
        for s in range(4):
            qs = _mm(h, wp_v[:, 1536 + 128 * s:1664 + 128 * s])
            sq_ref[:, 128 * s:128 * s + 128] = (_rotate(qs, rc_, rsa_, rsb_) * 0.125).astype(BF16)
        lane = _iota((TM, LANES), 1)
        first = lane < 64
        kr = _rotate(_mm(h, wp_v[:, 2048:2176]), rc_, rsa_, rsb_)
        krr = pltpu.roll(kr, 64, 1)
        kd_ref[:, 0:128] = jnp.where(first, kr, krr).astype(BF16)
        kd_ref[:, 128:256] = jnp.where(first, krr, kr).astype(BF16)
        vr = _mm(h, wp_v[:, 2176:2304])
        vrr = pltpu.roll(vr, 64, 1)
        vd_ref[:, 0:128] = jnp.where(first, vr, vrr).astype(BF16)
        vd_ref[:, 128:256] = jnp.where(first, vrr, vr).astype(BF16)

    outs = [
        jax.ShapeDtypeStruct((T, D_MODEL), BF16),
        jax.ShapeDtypeStruct((T, GLA_QK), F32),
        jax.ShapeDtypeStruct((T, GLA_QK), F32),
        jax.ShapeDtypeStruct((T, GLA_V), BF16),
        jax.ShapeDtypeStruct((T, GLA_QK), F32),
        jax.ShapeDtypeStruct((T, GLA_V), BF16),
        jax.ShapeDtypeStruct((T, SWA_Q), BF16),
        jax.ShapeDtypeStruct((T, 256), BF16),
        jax.ShapeDtypeStruct((T, 256), BF16),
        jax.ShapeDtypeStruct((T, LANES), BF16),
    ]
    return pl.pallas_call(
        body, name="proj_fwd", grid=(T // TM,), out_shape=outs,
        in_specs=[_row_spec(TM, D_MODEL), _const_spec((1, D_MODEL)), _any_spec(), _const_spec((LANES, GLA_QK)),
                  _const_spec((1, GLA_QK)), _row_spec(TM, LANES), _row_spec(TM, LANES), _row_spec(TM, LANES)],
        out_specs=[_row_spec(TM, o.shape[1]) for o in outs],
        scratch_shapes=[pltpu.VMEM((D_MODEL, IN_WIDTH_PAD), BF16), pltpu.SemaphoreType.DMA],
        compiler_params=_params(),
    )(x, g1, wp, gup, gbias, rc, rsa, rsb)


GLA_NB = GLA_BLOCK // GLA_CHUNK


def _gla_masks():
    n = GLA_BLOCK
    lane = _iota((n, LANES), 1)
    lane_masks = [(lane < 64).astype(F32), (lane >= 64).astype(F32)]
    row, col = _iota((n, n), 0), _iota((n, n), 1)
    same_chunk = (row >> 6) == (col >> 6)
    blk = ((_iota((256, LANES), 0) >> 7) == (_iota((256, LANES), 1) >> 6)).astype(F32)
    return lane_masks, same_chunk & (col <= row), same_chunk & (col >= row), blk


def _chunk_rows(vals):
    return jnp.concatenate([jnp.broadcast_to(v, (GLA_CHUNK, LANES)) for v in vals], axis=0)


def _gla_block_terms(q_ref, k_ref, b_ref, p):
    C = GLA_CHUNK
    cols = slice(LANES * p, LANES * p + LANES)
    bc = b_ref[:, cols]
    bl_rows = [b_ref[C * c + C - 1:C * c + C, cols] for c in range(GLA_NB)]
    bl = _chunk_rows(bl_rows)
    bm = _chunk_rows([b_ref[C * c + C // 2 - 1:C * c + C // 2, cols] for c in range(GLA_NB)])
    qs = q_ref[:, cols] * 0.125
    kk = k_ref[:, cols]
    eb = jnp.exp(bc)
    ekl = jnp.exp(bl - bc)
    eqm = jnp.exp(bc - bm)
    ekm = jnp.exp(bm - bc)
    return qs, kk, eb, ekl, eqm, ekm, [jnp.exp(r) for r in bl_rows]


def _block_cumsum(la, mask):
    return _mm_f32(mask.astype(F32), la)


def _gla_fwd(q, k, v, la):
    T = q.shape[0]
    NB = GLA_BLOCK // GLA_CHUNK
    C = GLA_CHUNK

    def body(q_ref, k_ref, v_ref, la_ref, o_ref, s_ref, st_ref, b_ref):
        @pl.when(pl.program_id(0) == 0)
        def _():
            st_ref[...] = jnp.zeros_like(st_ref)

        lane_masks, causal, _, blk = _gla_masks()
        b_ref[...] = _block_cumsum(la_ref[...], causal)
        for p in range(2):
            qs, kk, eb, ekl, eqm, ekm, gam = _gla_block_terms(q_ref, k_ref, b_ref, p)
            qh, kh, qm, km = qs * eb, kk * ekl, qs * eqm, kk * ekm
            vp = v_ref[:, 256 * p:256 * p + 256]
            intra = []
            for j in range(2):
                a = jnp.where(causal, _mm_nt(qm * lane_masks[j], km), 0.0)
                intra.append(_mm(a, vp[:, 128 * j:128 * j + 128]))
            kv = [blk * _mm_tn(vp[C * c:C * c + C], kh[C * c:C * c + C]) for c in range(NB)]
            st = st_ref[p]
            inter = []
            for c in range(NB):
                s_ref[c, p] = st[0:LANES] + st[LANES:2 * LANES]
                inter.append(_mm_nt(qh[C * c:C * c + C], st))
                st = st * gam[c] + kv[c]
            st_ref[p] = st
            o_ref[:, 256 * p:256 * p + 256] = (jnp.concatenate(inter, axis=0) + jnp.concatenate(intra, axis=1)).astype(BF16)

    return pl.pallas_call(
        body, name="gla_fwd", grid=(T // GLA_BLOCK,),
        out_shape=[jax.ShapeDtypeStruct((T, GLA_V), BF16), jax.ShapeDtypeStruct((T // C, 2, LANES, LANES), F32)],
        in_specs=[_row_spec(GLA_BLOCK, GLA_QK), _row_spec(GLA_BLOCK, GLA_QK), _row_spec(GLA_BLOCK, GLA_V),
                  _row_spec(GLA_BLOCK, GLA_QK)],
        out_specs=[_row_spec(GLA_BLOCK, GLA_V), pl.BlockSpec((NB, 2, LANES, LANES), lambda i: (i, 0, 0, 0))],
        scratch_shapes=[pltpu.VMEM((2, 256, LANES), F32), pltpu.VMEM((GLA_BLOCK, GLA_QK), F32)],
        compiler_params=_params(),
    )(q, k, v, la)


def _gla_bwd(q, k, v, la, s_all, do):
    T = q.shape[0]
    NB = GLA_BLOCK // GLA_CHUNK
    C = GLA_CHUNK
    nblk = T // GLA_BLOCK

    def body(q_ref, k_ref, v_ref, la_ref, s_ref, do_ref, dq_ref, dk_ref, dv_ref, dz_ref, dst_ref, b_ref):
        @pl.when(pl.program_id(0) == 0)
        def _():
            dst_ref[...] = jnp.zeros_like(dst_ref)

        lane_masks, causal, anti_causal, blk = _gla_masks()
        b_ref[...] = _block_cumsum(la_ref[...], causal)
        for p in range(2):
            cols = slice(LANES * p, LANES * p + LANES)
            qs, kk, eb, ekl, eqm, ekm, gam = _gla_block_terms(q_ref, k_ref, b_ref, p)
            qh, kh, qm, km = qs * eb, kk * ekl, qs * eqm, kk * ekm
            vp = v_ref[:, 256 * p:256 * p + 256]
            dop = do_ref[:, 256 * p:256 * p + 256]
            dqm = jnp.zeros((GLA_BLOCK, LANES), F32)
            dkm = jnp.zeros((GLA_BLOCK, LANES), F32)
            dv_intra = []
            for j in range(2):
                hs = slice(128 * j, 128 * j + 128)
                a = jnp.where(causal, _mm_nt(qm * lane_masks[j], km), 0.0)
                da = jnp.where(causal, _mm_nt(dop[:, hs], vp[:, hs]), 0.0)
                dv_intra.append(_mm_tn(a, dop[:, hs]))
                dqm = dqm + lane_masks[j] * _mm(da, km)
                dkm = dkm + lane_masks[j] * _mm_tn(da, qm)
            grow = [blk * _mm_tn(dop[C * c:C * c + C], qh[C * c:C * c + C]) for c in range(NB)]
            dst = dst_ref[p]
            dst_after = [None] * NB
            for c in reversed(range(NB)):
                dst_after[c] = dst
                dst = dst * gam[c] + grow[c]
            dst_ref[p] = dst
            dqh, dkh, dv_state, extra = [], [], [], []
            for c in range(NB):
                rows = slice(C * c, C * c + C)
                packed = s_ref[c, p]
                st = jnp.concatenate([packed * lane_masks[0][0:LANES], packed * lane_masks[1][0:LANES]], axis=0)
                dqh.append(_mm(dop[rows], st))
                dkh.append(_mm(vp[rows], dst_after[c]))
                dv_state.append(_mm_nt(kh[rows], dst_after[c]))
                extra.append(jnp.sum(dkh[c] * kh[rows], axis=0, keepdims=True)
                             + jnp.sum(st * dst_after[c], axis=0, keepdims=True) * gam[c])
            dqs = jnp.concatenate(dqh, axis=0) * eb + dqm * eqm
            dk = jnp.concatenate(dkh, axis=0) * ekl + dkm * ekm
            dg = _mm_f32(anti_causal.astype(F32), dqs * qs - dk * kk) + _chunk_rows(extra)
            dq_ref[:, cols] = (dqs * 0.125).astype(BF16)
            dk_ref[:, cols] = dk.astype(BF16)
            dz_ref[:, cols] = dg * (1.0 - jnp.exp(GLA_TAU * la_ref[:, cols])) * (1.0 / GLA_TAU)
            dv_ref[:, 256 * p:256 * p + 256] = (jnp.concatenate(dv_state, axis=0) + jnp.concatenate(dv_intra, axis=1)).astype(BF16)

    rev = lambda i: (nblk - 1 - i, 0)
    rspec = lambda cols: pl.BlockSpec((GLA_BLOCK, cols), rev)
    return pl.pallas_call(
        body, name="gla_bwd", grid=(nblk,),
        out_shape=[jax.ShapeDtypeStruct((T, GLA_QK), BF16), jax.ShapeDtypeStruct((T, GLA_QK), BF16),
                   jax.ShapeDtypeStruct((T, GLA_V), BF16), jax.ShapeDtypeStruct((T, GLA_QK), F32)],
        in_specs=[rspec(GLA_QK), rspec(GLA_QK), rspec(GLA_V), rspec(GLA_QK),
                  pl.BlockSpec((NB, 2, LANES, LANES), lambda i: (nblk - 1 - i, 0, 0, 0)), rspec(GLA_V)],
        out_specs=[rspec(GLA_QK), rspec(GLA_QK), rspec(GLA_V), rspec(GLA_QK)],
        scratch_shapes=[pltpu.VMEM((2, 256, LANES), F32), pltpu.VMEM((GLA_BLOCK, GLA_QK), F32)],
        compiler_params=_params(),
    )(q, k, v, la, s_all, do)


SWA_GROUP = 4


def _swa_stack(ref, g, first):
    parts = []
    for j in range(SWA_GROUP):
        m = 2 * g + j // 2
        pair = ref[:, 128 * m:128 * m + 128]
        zero = jnp.zeros_like(pair)
        parts.append(jnp.where(first, pair, zero) if j % 2 == 0 else jnp.where(first, zero, pair))
    return jnp.concatenate(parts, axis=0)


def _swa_unstack(rows, mm, first):
    W = SWA_BLOCK
    return jnp.where(first, rows[W * 2 * mm:W * (2 * mm + 1)], rows[W * (2 * mm + 1):W * (2 * mm + 2)])


def _swa_probs(qs, kp, kc, vp, vc, i, g, sink_ref, first4):
    W = SWA_BLOCK
    R = SWA_GROUP * W
    r, c = _iota((R, W), 0) & (W - 1), _iota((R, W), 1)
    neg = -1e30
    s_p = jnp.where((c > r) & (i > 0), _mm_nt(qs, kp), neg)
    s_c = jnp.where(c <= r, _mm_nt(qs, kc), neg)
    head = _iota((R, 1), 0) >> 7
    sink = jnp.where(head == 0, sink_ref[4 * g], jnp.where(head == 1, sink_ref[4 * g + 1],
                                                           jnp.where(head == 2, sink_ref[4 * g + 2], sink_ref[4 * g + 3])))
    m = jnp.maximum(jnp.max(jnp.maximum(s_p, s_c), axis=-1, keepdims=True), sink)
    p_p = jnp.exp(s_p - m)
    p_c = jnp.exp(s_c - m)
    p_s = jnp.exp(sink - m)
    one = jnp.ones((W, LANES), BF16)
    first = _iota((W, LANES), 1) < 64
    acc = _mm(p_p, jnp.where(first, vp, one)) + _mm(p_c, jnp.where(first, vc, one))
    rolled = pltpu.roll(acc, 64, 1)
    denom = jnp.where(first4, rolled, acc) + p_s
    return p_p, p_c, p_s, denom, acc, rolled


def _swa_fwd(sq, kd, vd, sinks):
    T = sq.shape[0]
    W = SWA_BLOCK
    prev = lambda i: (jnp.maximum(i - 1, 0), 0)

    def body(sink_ref, q_ref, kp_ref, kc_ref, vp_ref, vc_ref, o_ref):
        i = pl.program_id(0)
        first4 = _iota((SWA_GROUP * W, LANES), 1) < 64
        first = _iota((W, LANES), 1) < 64
        for g in range(2):
            gs = slice(128 * g, 128 * g + 128)
            qs = _swa_stack(q_ref, g, first)
            _, _, _, denom, acc, rolled = _swa_probs(qs, kp_ref[:, gs], kc_ref[:, gs], vp_ref[:, gs], vc_ref[:, gs],
                                                     i, g, sink_ref, first4)
            pv = jnp.where(first4, acc, rolled)
            o = pv / denom
            for mm in range(2):
                m = 2 * g + mm
                o_ref[:, 128 * m:128 * m + 128] = _swa_unstack(o, mm, first).astype(BF16)

    return pl.pallas_call(
        body, name="swa_fwd", grid=(T // W,), out_shape=jax.ShapeDtypeStruct((T, SWA_Q), BF16),
        in_specs=[pl.BlockSpec(memory_space=pltpu.SMEM), _row_spec(W, SWA_Q), pl.BlockSpec((W, 256), prev),
                  _row_spec(W, 256), pl.BlockSpec((W, 256), prev), _row_spec(W, 256)],
        out_specs=_row_spec(W, SWA_Q),
        compiler_params=_params(),
    )(sinks, sq, kd, kd, vd, vd)


def _swa_bwd(sq, kd, vd, sinks, do):
    T = sq.shape[0]
    W = SWA_BLOCK
    n = T // W
    cur = lambda i: (jnp.minimum(i, n - 1), 0)
    prev = lambda i: (jnp.clip(i - 1, 0, n - 1), 0)

    def body(sink_ref, q_ref, kp_ref, kc_ref, vp_ref, vc_ref, do_ref, dq_ref, dk_ref, dv_ref, ds_ref, ck_ref, cv_ref):
        i = pl.program_id(0)

        @pl.when(i == 0)
        def _():
            ds_ref[...] = jnp.zeros_like(ds_ref)
            ck_ref[...] = jnp.zeros_like(ck_ref)
            cv_ref[...] = jnp.zeros_like(cv_ref)

        @pl.when(i < n)
        def _():
            first4 = _iota((SWA_GROUP * W, LANES), 1) < 64
            first = _iota((W, LANES), 1) < 64
            for g in range(2):
                gs = slice(128 * g, 128 * g + 128)
                kp, kc, vp, vc = kp_ref[:, gs], kc_ref[:, gs], vp_ref[:, gs], vc_ref[:, gs]
                qs = _swa_stack(q_ref, g, first)
                dos = _swa_stack(do_ref, g, first)
                p_p, p_c, p_s, denom, _, _ = _swa_probs(qs, kp, kc, vp, vc, i, g, sink_ref, first4)
                inv = 1.0 / denom
                p_p, p_c = p_p * inv, p_c * inv
                dp_p = _mm_nt(dos, vp)
                dp_c = _mm_nt(dos, vc)
                delta = jnp.sum(p_p * dp_p + p_c * dp_c, axis=-1, keepdims=True)
                ds_p = p_p * (dp_p - delta)
                ds_c = p_c * (dp_c - delta)
                rows = slice(SWA_GROUP * W * g, SWA_GROUP * W * (g + 1))
                ds_ref[rows, :] = ds_ref[rows, :] - (p_s * delta) * inv
                dq = (_mm(ds_p, kp) + _mm(ds_c, kc)) * 0.125
                for mm in range(2):
                    m = 2 * g + mm
                    dq_ref[:, 128 * m:128 * m + 128] = _swa_unstack(dq, mm, first).astype(BF16)
                dk_ref[:, gs] = (ck_ref[:, gs] + _mm_tn(ds_p, qs)).astype(BF16)
                dv_ref[:, gs] = (cv_ref[:, gs] + _mm_tn(p_p, dos)).astype(BF16)
                ck_ref[:, gs] = _mm_tn(ds_c, qs)
                cv_ref[:, gs] = _mm_tn(p_c, dos)

        @pl.when(i == n)
        def _():
            dk_ref[...] = ck_ref[...].astype(BF16)
            dv_ref[...] = cv_ref[...].astype(BF16)

    return pl.pallas_call(
        body, name="swa_bwd", grid=(n + 1,),
        out_shape=[jax.ShapeDtypeStruct((T, SWA_Q), BF16), jax.ShapeDtypeStruct((T, 256), BF16),
                   jax.ShapeDtypeStruct((T, 256), BF16), jax.ShapeDtypeStruct((SWA_HEADS * W, LANES), F32)],
        in_specs=[pl.BlockSpec(memory_space=pltpu.SMEM), pl.BlockSpec((W, SWA_Q), cur), pl.BlockSpec((W, 256), prev),
                  pl.BlockSpec((W, 256), cur), pl.BlockSpec((W, 256), prev), pl.BlockSpec((W, 256), cur),
                  pl.BlockSpec((W, SWA_Q), cur)],
        out_specs=[pl.BlockSpec((W, SWA_Q), cur), pl.BlockSpec((W, 256), prev), pl.BlockSpec((W, 256), prev),
                   _const_spec((SWA_HEADS * W, LANES))],
        scratch_shapes=[pltpu.VMEM((W, 256), F32), pltpu.VMEM((W, 256), F32)],
        compiler_params=_params(),
    )(sinks, sq, kd, kd, vd, vd, do)


def _mix_out_fwd(x, og, gg, osw, gnorm, wout, g2):
    T = x.shape[0]

    def body(x_ref, og_ref, gg_ref, osw_ref, gn_ref, wout_ref, g2_ref, x1_ref, cat_ref):
        gn = gn_ref[...]
        for j in range(GLA_HEADS):
            hs = slice(128 * j, 128 * j + 128)
            o = og_ref[:, hs].astype(F32)
            r = lax.rsqrt(jnp.mean(o * o, axis=-1, keepdims=True) + EPS)
            gate = gg_ref[:, hs].astype(F32)
            cat_ref[:, hs] = (o * r * gn * (gate * _sigmoid(gate))).astype(BF16)
        cat_ref[:, GLA_V:] = osw_ref[...]
        mix = _mm(cat_ref[...], wout_ref[...])
        r2 = lax.rsqrt(jnp.mean(mix * mix, axis=-1, keepdims=True) + EPS)
        x1_ref[...] = x_ref[...] + mix * r2 * g2_ref[...]

    return pl.pallas_call(
        body, name="mix_out_fwd", grid=(T // TM,),
        out_shape=[jax.ShapeDtypeStruct((T, D_MODEL), F32), jax.ShapeDtypeStruct((T, D_MODEL), BF16)],
        in_specs=[_row_spec(TM, D_MODEL), _row_spec(TM, GLA_V), _row_spec(TM, GLA_V), _row_spec(TM, SWA_Q),
                  _const_spec((1, LANES)), _const_spec((D_MODEL, D_MODEL)), _const_spec((1, D_MODEL))],
        out_specs=[_row_spec(TM, D_MODEL), _row_spec(TM, D_MODEL)],
        compiler_params=_params(),
    )(x, og, gg, osw, gnorm, wout, g2)


HALO = 8


def _rows_before(v, prev1, prev2):
    row = _iota(v.shape, 0)
    m1 = jnp.where(row == 0, prev1, pltpu.roll(v, 1, 0))
    m2 = jnp.where(row == 0, prev2, jnp.where(row == 1, prev1, pltpu.roll(v, 2, 0)))
    return m1, m2


def _rows_after(v, next1, next2):
    n = v.shape[0]
    row = _iota(v.shape, 0)
    p1 = jnp.where(row == n - 1, next1, pltpu.roll(v, n - 1, 0))
    p2 = jnp.where(row == n - 1, next2, jnp.where(row == n - 2, next1, pltpu.roll(v, n - 2, 0)))
    return p1, p2


def _ff_pieces():
    return [(j, off, wd) for j in range(2) for off, wd in FF_PIECES]


def _ffn_fwd(x1, g3, wup, cw, cb, wdown, g4, target):
    T = x1.shape[0]

    def body(x1_ref, g3_ref, wup_hbm, cw_ref, cb_ref, wdn_hbm, g4_ref, tg_ref,
             h2_ref, up_ref, a_ref, c1_ref, c2_ref, y_ref, dx2_ref, loss_ref, wup_v, wdn_v, carry_ref, sems):
        _load_once(wup_hbm, wup_v, sems.at[0])
        _load_once(wdn_hbm, wdn_v, sems.at[1])

        @pl.when(pl.program_id(0) == 0)
        def _():
            carry_ref[...] = jnp.zeros_like(carry_ref)
            loss_ref[...] = jnp.zeros_like(loss_ref)

        x1 = x1_ref[...]
        r3 = lax.rsqrt(jnp.mean(x1 * x1, axis=-1, keepdims=True) + EPS)
        h2 = (x1 * r3 * g3_ref[...]).astype(BF16)
        h2_ref[...] = h2
        for j, off, wd in _ff_pieces():
            base = SHARD_FF * j + off
            u = []
            for half in range(2):
                cs = slice(D_FF * half + base, D_FF * half + base + wd)
                upb = _mm(h2, wup_v[2 * half + j, :, off:off + wd]).astype(BF16)
                up_ref[:, cs] = upb
                upf = upb.astype(F32)
                m1, m2 = _rows_before(upf, carry_ref[HALO - 1:HALO, cs], carry_ref[HALO - 2:HALO - 1, cs])
                u.append(cb_ref[:, cs] + cw_ref[0:1, cs] * m2 + cw_ref[1:2, cs] * m1 + cw_ref[2:3, cs] * upf)
                carry_ref[:, cs] = upf[TM - HALO:TM, :]
            act, dact = _gelu_parts(u[1])
            a = (act * u[0]).astype(BF16)
            out = slice(base, base + wd)
            a_ref[:, out] = a
            c1_ref[:, out] = act.astype(BF16)
            c2_ref[:, out] = (u[0] * dact).astype(BF16)
        y = _mm(a_ref[...], wdn_v[...])
        y_ref[...] = y
        r4 = lax.rsqrt(jnp.mean(y * y, axis=-1, keepdims=True) + EPS)
        err = x1 + y * r4 * g4_ref[...] - tg_ref[...]
        dx2_ref[...] = err * (1.0 / D_MODEL)
        loss_ref[...] = loss_ref[...] + jnp.sum(err * err) * (0.5 / D_MODEL)

    outs = [
        jax.ShapeDtypeStruct((T, D_MODEL), BF16),
        jax.ShapeDtypeStruct((T, 2 * D_FF), BF16),
        jax.ShapeDtypeStruct((T, D_FF), BF16),
        jax.ShapeDtypeStruct((T, D_FF), BF16),
        jax.ShapeDtypeStruct((T, D_FF), BF16),
        jax.ShapeDtypeStruct((T, D_MODEL), F32),
        jax.ShapeDtypeStruct((T, D_MODEL), F32),
        jax.ShapeDtypeStruct((8, LANES), F32),
    ]
    return pl.pallas_call(
        body, name="ffn_fwd", grid=(T // TM,), out_shape=outs,
        in_specs=[_row_spec(TM, D_MODEL), _const_spec((1, D_MODEL)), _any_spec(), _const_spec((3, 2 * D_FF)),
                  _const_spec((1, 2 * D_FF)), _any_spec(), _const_spec((1, D_MODEL)), _row_spec(TM, D_MODEL)],
        out_specs=[_row_spec(TM, D_MODEL), _row_spec(TM, 2 * D_FF), _row_spec(TM, D_FF), _row_spec(TM, D_FF),
                   _row_spec(TM, D_FF), _row_spec(TM, D_MODEL), _row_spec(TM, D_MODEL), _const_spec((8, LANES))],
        scratch_shapes=[pltpu.VMEM((N_SHARD, D_MODEL, SHARD_FF), BF16), pltpu.VMEM((D_FF, D_MODEL), BF16),
                        pltpu.VMEM((HALO, 2 * D_FF), F32), pltpu.SemaphoreType.DMA((2,))],
        compiler_params=_params(),
    )(x1, g3, wup, cw, cb, wdown, g4, target)


def _ffn_bwd(dx2, y, g4, up, c1, c2, cw, wdown, wup, x1, g3):
    T = dx2.shape[0]
    nt = T // TM
    rev = lambda i: (nt - 1 - i, 0)

    def body(dn_ref, y_ref, g4_ref, up_ref, c1_ref, c2_ref, cw_ref, wdn_hbm, wup_hbm, x1_ref, g3_ref,
             dy_ref, dup_ref, dx1_ref, dg4_ref, dg3_ref, dcb_ref, dcw_ref, wup_v, wdn_v, carry_ref, sems):
        _load_once(wup_hbm, wup_v, sems.at[0])
        _load_once(wdn_hbm, wdn_v, sems.at[1])

        @pl.when(pl.program_id(0) == 0)
        def _():
            carry_ref[...] = jnp.zeros_like(carry_ref)
            dg4_ref[...] = jnp.zeros_like(dg4_ref)
            dg3_ref[...] = jnp.zeros_like(dg3_ref)
            dcb_ref[...] = jnp.zeros_like(dcb_ref)
            dcw_ref[...] = jnp.zeros_like(dcw_ref)

        dn = dn_ref[...]
        y = y_ref[...]
        g4v = g4_ref[...]
        r4 = lax.rsqrt(jnp.mean(y * y, axis=-1, keepdims=True) + EPS)
        dg4_ref[...] = dg4_ref[...] + jnp.sum(dn * y * r4, axis=0, keepdims=True)
        dy = _rms_bwd(y, r4, g4v, dn).astype(BF16)
        dy_ref[...] = dy
        dh2 = jnp.zeros((TM, D_MODEL), F32)
        for j, off, wd in _ff_pieces():
            base = SHARD_FF * j + off
            da = _mm_nt(dy, wdn_v[base:base + wd, :])
            for half, coef_ref in enumerate((c1_ref, c2_ref)):
                cs = slice(D_FF * half + base, D_FF * half + base + wd)
                du = da * coef_ref[:, base:base + wd].astype(F32)
                p1, p2 = _rows_after(du, carry_ref[0:1, cs], carry_ref[1:2, cs])
                carry_ref[:, cs] = du[0:HALO, :]
                upf = up_ref[:, cs].astype(F32)
                dcb_ref[:, cs] = dcb_ref[:, cs] + jnp.sum(du, axis=0, keepdims=True)
                dcw_ref[0:1, cs] = dcw_ref[0:1, cs] + jnp.sum(p2 * upf, axis=0, keepdims=True)
                dcw_ref[1:2, cs] = dcw_ref[1:2, cs] + jnp.sum(p1 * upf, axis=0, keepdims=True)
                dcw_ref[2:3, cs] = dcw_ref[2:3, cs] + jnp.sum(du * upf, axis=0, keepdims=True)
                dup = (cw_ref[2:3, cs] * du + cw_ref[1:2, cs] * p1 + cw_ref[0:1, cs] * p2).astype(BF16)
                dup_ref[:, cs] = dup
                dh2 = dh2 + _mm_nt(dup, wup_v[2 * half + j, :, off:off + wd])
        x1 = x1_ref[...]
        r3 = lax.rsqrt(jnp.mean(x1 * x1, axis=-1, keepdims=True) + EPS)
        dg3_ref[...] = dg3_ref[...] + jnp.sum(dh2 * x1 * r3, axis=0, keepdims=True)
        dx1_ref[...] = dn + _rms_bwd(x1, r3, g3_ref[...], dh2)

    outs = [
        jax.ShapeDtypeStruct((T, D_MODEL), BF16),
        jax.ShapeDtypeStruct((T, 2 * D_FF), BF16),
        jax.ShapeDtypeStruct((T, D_MODEL), F32),
        jax.ShapeDtypeStruct((1, D_MODEL), F32),
        jax.ShapeDtypeStruct((1, D_MODEL), F32),
        jax.ShapeDtypeStruct((1, 2 * D_FF), F32),
        jax.ShapeDtypeStruct((3, 2 * D_FF), F32),
    ]
    return pl.pallas_call(
        body, name="ffn_bwd", grid=(nt,), out_shape=outs,
        in_specs=[pl.BlockSpec((TM, D_MODEL), rev), pl.BlockSpec((TM, D_MODEL), rev), _const_spec((1, D_MODEL)),
                  pl.BlockSpec((TM, 2 * D_FF), rev), pl.BlockSpec((TM, D_FF), rev), pl.BlockSpec((TM, D_FF), rev),
                  _const_spec((3, 2 * D_FF)), _any_spec(), _any_spec(), pl.BlockSpec((TM, D_MODEL), rev),
                  _const_spec((1, D_MODEL))],
        out_specs=[pl.BlockSpec((TM, D_MODEL), rev), pl.BlockSpec((TM, 2 * D_FF), rev), pl.BlockSpec((TM, D_MODEL), rev),
                   _const_spec((1, D_MODEL)), _const_spec((1, D_MODEL)), _const_spec((1, 2 * D_FF)),
                   _const_spec((3, 2 * D_FF))],
        scratch_shapes=[pltpu.VMEM((N_SHARD, D_MODEL, SHARD_FF), BF16), pltpu.VMEM((D_FF, D_MODEL), BF16),
                        pltpu.VMEM((HALO, 2 * D_FF), F32), pltpu.SemaphoreType.DMA((2,))],
        compiler_params=_params(),
    )(dx2, y, g4, up, c1, c2, cw, wdown, wup, x1, g3)


def _mix_out_bwd(dx1, cat, g2, wout, og, gg, gnorm):
    T = dx1.shape[0]

    def body(dx1_ref, cat_ref, g2_ref, wout_ref, og_ref, gg_ref, gn_ref,
             dmix_ref, dog_ref, dgg_ref, dosw_ref, dg2_ref, dgn_ref):
        @pl.when(pl.program_id(0) == 0)
        def _():
            dg2_ref[...] = jnp.zeros_like(dg2_ref)
            dgn_ref[...] = jnp.zeros_like(dgn_ref)

        dx1 = dx1_ref[...]
        mix = _mm(cat_ref[...], wout_ref[...])
        r2 = lax.rsqrt(jnp.mean(mix * mix, axis=-1, keepdims=True) + EPS)
        dg2_ref[...] = dg2_ref[...] + jnp.sum(dx1 * mix * r2, axis=0, keepdims=True)
        dmix = _rms_bwd(mix, r2, g2_ref[...], dx1).astype(BF16)
        dmix_ref[...] = dmix
        dcat = _mm_nt(dmix, wout_ref[...])
        dosw_ref[...] = dcat[:, GLA_V:].astype(BF16)
        gn = gn_ref[...]
        dgn = jnp.zeros((1, LANES), F32)
        for j in range(GLA_HEADS):
            hs = slice(128 * j, 128 * j + 128)
            o = og_ref[:, hs].astype(F32)
            r = lax.rsqrt(jnp.mean(o * o, axis=-1, keepdims=True) + EPS)
            gate = gg_ref[:, hs].astype(F32)
            sg = _sigmoid(gate)
            dgated = dcat[:, hs]
            dnorm = dgated * (gate * sg)
            dgg_ref[:, hs] = (dgated * (o * r * gn) * (sg * (1.0 + gate * (1.0 - sg)))).astype(BF16)
            dgn = dgn + jnp.sum(dnorm * o * r, axis=0, keepdims=True)
            dog_ref[:, hs] = _rms_bwd(o, r, gn, dnorm)
        dgn_ref[...] = dgn_ref[...] + dgn

    return pl.pallas_call(
        body, name="mix_out_bwd", grid=(T // TM,),
        out_shape=[jax.ShapeDtypeStruct((T, D_MODEL), BF16), jax.ShapeDtypeStruct((T, GLA_V), F32),
                   jax.ShapeDtypeStruct((T, GLA_V), BF16), jax.ShapeDtypeStruct((T, SWA_Q), BF16),
                   jax.ShapeDtypeStruct((1, D_MODEL), F32), jax.ShapeDtypeStruct((1, LANES), F32)],
        in_specs=[_row_spec(TM, D_MODEL), _row_spec(TM, D_MODEL), _const_spec((1, D_MODEL)),
                  _const_spec((D_MODEL, D_MODEL)), _row_spec(TM, GLA_V), _row_spec(TM, GLA_V), _const_spec((1, LANES))],
        out_specs=[_row_spec(TM, D_MODEL), _row_spec(TM, GLA_V), _row_spec(TM, GLA_V), _row_spec(TM, SWA_Q),
                   _const_spec((1, D_MODEL)), _const_spec((1, LANES))],
        compiler_params=_params(),
    )(dx1, cat, g2, wout, og, gg, gnorm)


def _proj_bwd(x, g1, wp, gup, glr, dq, dk, dv, dgg, dsq, dkd, dvd, dz, rc, rsa, rsb, dx1):
    T = x.shape[0]

    def body(x_ref, g1_ref, wp_hbm, gup_ref, glr_ref, dq_ref, dk_ref, dv_ref, dgg_ref, dsq_ref, dkd_ref, dvd_ref,
             dz_ref, rc_ref, rsa_ref, rsb_ref, dx1_ref, dx_ref, dp_ref, dg1_ref, dgup_ref, dgb_ref, wp_v, sem):
        _load_once(wp_hbm, wp_v, sem)

        @pl.when(pl.program_id(0) == 0)
        def _():
            dg1_ref[...] = jnp.zeros_like(dg1_ref)
            dgup_ref[...] = jnp.zeros_like(dgup_ref)
            dgb_ref[...] = jnp.zeros_like(dgb_ref)

        rc_, rsa_, rsb_ = rc_ref[...], rsa_ref[...], rsb_ref[...]
        dp_ref[:, 0:256] = dq_ref[...]
        dp_ref[:, 256:512] = dk_ref[...]
        dp_ref[:, 512:1024] = dv_ref[...]
        dp_ref[:, 1024:1536] = dgg_ref[...]
        for s in range(4):
            cs = slice(128 * s, 128 * s + 128)
            dp_ref[:, 1536 + 128 * s:1664 + 128 * s] = _rotate_bwd(dsq_ref[:, cs].astype(F32), rc_, rsa_, rsb_).astype(BF16)
        first = _iota((TM, LANES), 1) < 64
        dk0 = dkd_ref[:, 0:128].astype(F32)
        dk1 = dkd_ref[:, 128:256].astype(F32)
        dkr = jnp.where(first, dk0 + pltpu.roll(dk0, 64, 1), dk1 + pltpu.roll(dk1, 64, 1))
        dp_ref[:, 2048:2176] = _rotate_bwd(dkr, rc_, rsa_, rsb_).astype(BF16)
        dv0 = dvd_ref[:, 0:128].astype(F32)
        dv1 = dvd_ref[:, 128:256].astype(F32)
        dp_ref[:, 2176:2304] = jnp.where(first, dv0 + pltpu.roll(dv0, 64, 1), dv1 + pltpu.roll(dv1, 64, 1)).astype(BF16)
        dz = dz_ref[...]
        dzb = dz.astype(BF16)
        dp_ref[:, 2304:2432] = _mm_nt(dzb, gup_ref[...]).astype(BF16)
        dgup_ref[...] = dgup_ref[...] + _mm_tn(glr_ref[...], dzb)
        dgb_ref[...] = dgb_ref[...] + jnp.sum(dz, axis=0, keepdims=True)
        dh1 = _mm_nt(dp_ref[...], wp_v[...])
        xt = x_ref[...]
        r = lax.rsqrt(jnp.mean(xt * xt, axis=-1, keepdims=True) + EPS)
        dg1_ref[...] = dg1_ref[...] + jnp.sum(dh1 * xt * r, axis=0, keepdims=True)
        dx_ref[...] = dx1_ref[...] + _rms_bwd(xt, r, g1_ref[...], dh1)

    row = lambda cols: _row_spec(TM, cols)
    return pl.pallas_call(
        body, name="proj_bwd", grid=(T // TM,),
        out_shape=[jax.ShapeDtypeStruct((T, D_MODEL), F32), jax.ShapeDtypeStruct((T, IN_WIDTH_PAD), BF16),
                   jax.ShapeDtypeStruct((1, D_MODEL), F32), jax.ShapeDtypeStruct((LANES, GLA_QK), F32),
                   jax.ShapeDtypeStruct((1, GLA_QK), F32)],
        in_specs=[row(D_MODEL), _const_spec((1, D_MODEL)), _any_spec(), _const_spec((LANES, GLA_QK)), row(LANES),
                  row(GLA_QK), row(GLA_QK), row(GLA_V), row(GLA_V), row(SWA_Q), row(256), row(256), row(GLA_QK),
                  row(LANES), row(LANES), row(LANES), row(D_MODEL)],
        out_specs=[row(D_MODEL), row(IN_WIDTH_PAD), _const_spec((1, D_MODEL)), _const_spec((LANES, GLA_QK)),
                   _const_spec((1, GLA_QK))],
        scratch_shapes=[pltpu.VMEM((D_MODEL, IN_WIDTH_PAD), BF16), pltpu.SemaphoreType.DMA],
        compiler_params=_params(),
    )(x, g1, wp, gup, glr, dq, dk, dv, dgg, dsq, dkd, dvd, dz, rc, rsa, rsb, dx1)


def _matmul_tn(a, b, tn, name, column_blocks_major=False):
    T, M = a.shape
    N = b.shape[1]
    tk = next(t for t in (1024, 512, TM) if T % t == 0)
    nk = T // tk
    if column_blocks_major:
        out_shape = jax.ShapeDtypeStruct((N // tn, M, tn), F32)
        out_spec = pl.BlockSpec((None, M, tn), lambda j, kk: (j, 0, 0))
    else:
        out_shape = jax.ShapeDtypeStruct((M, N), F32)
        out_spec = pl.BlockSpec((M, tn), lambda j, kk: (0, j))

    def body(a_ref, b_ref, o_ref):
        kk = pl.program_id(1)

        @pl.when(kk == 0)
        def _():
            o_ref[...] = jnp.zeros_like(o_ref)

        o_ref[...] = o_ref[...] + _mm_tn(a_ref[...], b_ref[...])

    return pl.pallas_call(
        body, name=name, grid=(N // tn, nk), out_shape=out_shape,
        in_specs=[pl.BlockSpec((tk, M), lambda j, kk: (kk, 0)), pl.BlockSpec((tk, tn), lambda j, kk: (kk, j))],
        out_specs=out_spec,
        compiler_params=_params(),
    )(a, b)


def _adamw_update(w_ref, g_ref, m_ref, v_ref, d_ref, m2_ref, v2_ref):
    g_ = g_ref[...]
    m2 = ADAM_B1 * m_ref[...] + (1.0 - ADAM_B1) * g_
    v2 = ADAM_B2 * v_ref[...] + (1.0 - ADAM_B2) * (g_ * g_)
    m_hat = m2 / (1.0 - ADAM_B1 ** ADAM_STEP)
    v_hat = v2 / (1.0 - ADAM_B2 ** ADAM_STEP)
    d_ref[...] = -ADAM_LR * (m_hat / (jnp.sqrt(v_hat) + ADAM_EPS) + ADAM_WD * w_ref[...])
    m2_ref[...] = m2
    v2_ref[...] = v2


def _adamw(w, g, m, v, rows, name):
    R, C = w.shape

    def body(*refs):
        _adamw_update(*refs)

    spec = pl.BlockSpec((rows, C), lambda i: (i, 0))
    return pl.pallas_call(
        body, name=name, grid=(R // rows,), out_shape=[jax.ShapeDtypeStruct((R, C), F32)] * 3,
        in_specs=[spec] * 4, out_specs=[spec] * 3, compiler_params=_params(),
    )(w, g, m, v)


def _adamw_small(ws, gs, ms, vs):
    n = len(ws)

    def body(*refs):
        w_, g_, m_, v_, d_, m2_, v2_ = (refs[n * i:n * (i + 1)] for i in range(7))
        for k in range(n):
            _adamw_update(w_[k], g_[k], m_[k], v_[k], d_[k], m2_[k], v2_[k])

    vm = pl.BlockSpec(memory_space=pltpu.VMEM)
    outs = pl.pallas_call(
        body, name="adamw_small", out_shape=[jax.ShapeDtypeStruct(w.shape, F32) for w in ws] * 3,
        in_specs=[vm] * (4 * n), out_specs=[vm] * (3 * n),
    )(*ws, *gs, *ms, *vs)
    return outs[:n], outs[n:2 * n], outs[2 * n:]


def _place():
    x, y, c = lax.axis_index("x"), lax.axis_index("y"), lax.axis_index("c")
    chips = [(1 - x, y), (x, 1 - y), (1 - x, 1 - y)]
    return x, y, c, chips


class _staged_copies:
    def __init__(self, srcs, dsts, stage, sems):
        n = len(srcs)
        self.loads = [pltpu.make_async_copy(srcs[k], stage[k], sems.at[k]) for k in range(n)]
        self.stores = [pltpu.make_async_copy(stage[k], dsts[k], sems.at[n + k]) for k in range(n)]

    def load(self):
        for cp in self.loads:
            cp.start()

    def store(self):
        for ld, st in zip(self.loads, self.stores):
            ld.wait()
            st.start()

    def finish(self):
        for cp in self.stores:
            cp.wait()


def _allgather_shards(parts, unit_rows):
    n = len(parts)
    units = [(k, r, unit_rows[k]) for k in range(n) for r in range(0, parts[k].shape[0] // 2, unit_rows[k])]
    nu = len(units)

    def body(*refs):
        ins, outs, stage = refs[:n], refs[n:2 * n], refs[2 * n:3 * n]
        send_sems, recv_sems, local_sems = refs[3 * n:]
        x, y, c, chips = _place()
        sibling = (x, y, 1 - c)
        own = _staged_copies(ins, [o.at[2 * x + y] for o in outs], stage, local_sems)

        def block(i, px, py, half):
            k, r, u = units[i]
            return outs[k].at[2 * px + py, pl.ds(half * (parts[k].shape[0] // 2) + r, u), :]

        def copy(i, j, px, py, half, to, src=None):
            return pltpu.make_async_remote_copy(
                src_ref=block(i, px, py, half) if src is None else src, dst_ref=block(i, px, py, half),
                send_sem=send_sems.at[nu * j + i], recv_sem=recv_sems.at[nu * j + i], device_id=to, device_id_type=MESH)

        own.load()
        first, passed = [], []
        for i, (k, r, u) in enumerate(units):
            for j, chip in enumerate(chips):
                src = ins[k].at[pl.ds(c * (parts[k].shape[0] // 2) + r, u), :]
                first.append(copy(i, j, x, y, c, (*chip, c), src=src))
                first[-1].start()
        own.store()
        for i in range(nu):
            for j, chip in enumerate(chips):
                copy(i, j, *chip, c, (x, y, c)).wait_recv()
                passed.append(copy(i, 3 + j, *chip, c, sibling))
                passed[-1].start()
        for i in range(nu):
            for j, chip in enumerate(chips):
                copy(i, 3 + j, *chip, 1 - c, (x, y, c)).wait_recv()
        for cp in first + passed:
            cp.wait_send()
        own.finish()

    return pl.pallas_call(
        body, name="allgather_shards", out_shape=[jax.ShapeDtypeStruct((N_SHARD,) + p.shape, p.dtype) for p in parts],
        in_specs=[_any_spec()] * n, out_specs=[_any_spec()] * n,
        scratch_shapes=[pltpu.VMEM(p.shape, p.dtype) for p in parts] + [
            pltpu.SemaphoreType.DMA((6 * nu,)), pltpu.SemaphoreType.DMA((6 * nu,)), pltpu.SemaphoreType.DMA((2 * n,))],
        compiler_params=_params(),
    )(*parts)


def _d2d_pieces(rows, piece_rows):
    return [(r, piece_rows) for r in range(0, rows, piece_rows)]


def _rs_pair_swap(arrs, piece_rows, name):
    n = len(arrs)

    def body(*refs):
        ins, outs = refs[:n], refs[n:2 * n]
        send_sems, recv_sems = refs[2 * n:]
        x, y, c, _ = _place()
        sibling = (x, y, 1 - c)
        for k in range(n):
            H = arrs[k].shape[1] // 2
            for s in range(N_SHARD):
                for r, pr in _d2d_pieces(H, piece_rows[k]):
                    pltpu.make_async_remote_copy(
                        src_ref=ins[k].at[s, pl.ds((1 - c) * H + r, pr), :], dst_ref=outs[k].at[s, pl.ds(r, pr), :],
                        send_sem=send_sems.at[k], recv_sem=recv_sems.at[k], device_id=sibling, device_id_type=MESH).start()
        for k in range(n):
            H = arrs[k].shape[1] // 2
            whole = pltpu.make_async_remote_copy(
                src_ref=ins[k].at[:, pl.ds(0, H), :], dst_ref=outs[k], send_sem=send_sems.at[k], recv_sem=recv_sems.at[k],
                device_id=sibling, device_id_type=MESH)
            whole.wait_recv()
            whole.wait_send()

    return pl.pallas_call(
        body, name=name,
        out_shape=[jax.ShapeDtypeStruct((N_SHARD, a.shape[1] // 2, a.shape[2]), F32) for a in arrs],
        in_specs=[_any_spec()] * n, out_specs=[_any_spec()] * n,
        scratch_shapes=[pltpu.SemaphoreType.DMA((n,)), pltpu.SemaphoreType.DMA((n,))],
    )(*arrs)


def _rs_add_pair(a, got, core, rows, name):
    _, H, C = got.shape
    nb = H // rows

    def body(c_ref, a_ref, b_ref, o_ref):
        o_ref[...] = (a_ref[...] + b_ref[...]).astype(BF16)

    spec = pl.BlockSpec((1, rows, C), lambda s, r, c_ref: (s, r, 0))
    return pl.pallas_call(
        body, name=name, out_shape=jax.ShapeDtypeStruct(got.shape, BF16),
        grid_spec=pltpu.PrefetchScalarGridSpec(
            num_scalar_prefetch=1, grid=(N_SHARD, nb),
            in_specs=[pl.BlockSpec((1, rows, C), lambda s, r, c_ref: (s, c_ref[0] * nb + r, 0)), spec], out_specs=spec),
        compiler_params=_params(),
    )(core, a, got)


def _rs_chip_scatter(parts):
    n = len(parts)

    def body(*refs):
        ins, outs, stage = refs[:n], refs[n:2 * n], refs[2 * n:3 * n]
        send_sems, recv_sems, local_sems = refs[3 * n:]
        x, y, c, chips = _place()
        me = 2 * x + y
        own = _staged_copies([i.at[me] for i in ins], [o.at[me] for o in outs], stage, local_sems)
        own.load()
        sends = []
        for k in range(n):
            for j, (px, py) in enumerate(chips):
                sends.append(pltpu.make_async_remote_copy(
                    src_ref=ins[k].at[2 * px + py], dst_ref=outs[k].at[me], send_sem=send_sems.at[3 * k + j],
                    recv_sem=recv_sems.at[3 * k + j], device_id=(px, py, c), device_id_type=MESH))
                sends[-1].start()
        own.store()
        for k in range(n):
            for j, (px, py) in enumerate(chips):
                pltpu.make_async_remote_copy(
                    src_ref=ins[k].at[me], dst_ref=outs[k].at[2 * px + py], send_sem=send_sems.at[3 * k + j],
                    recv_sem=recv_sems.at[3 * k + j], device_id=(px, py, c), device_id_type=MESH).wait_recv()
        for cp in sends:
            cp.wait_send()
        own.finish()

    return pl.pallas_call(
        body, name="rs_chip_scatter", out_shape=[jax.ShapeDtypeStruct(p.shape, p.dtype) for p in parts],
        in_specs=[_any_spec()] * n, out_specs=[_any_spec()] * n,
        scratch_shapes=[pltpu.VMEM(p.shape[1:], p.dtype) for p in parts] + [
            pltpu.SemaphoreType.DMA((3 * n,)), pltpu.SemaphoreType.DMA((3 * n,)), pltpu.SemaphoreType.DMA((2 * n,))],
        compiler_params=_params(),
    )(*parts)


def _rs_sum_chips(parts, rows, name):
    _, H, C = parts.shape

    def body(p_ref, o_ref):
        o_ref[...] = ((p_ref[0].astype(F32) + p_ref[1].astype(F32)) + p_ref[2].astype(F32)) + p_ref[3].astype(F32)

    return pl.pallas_call(
        body, name=name, grid=(H // rows,), out_shape=jax.ShapeDtypeStruct((H, C), F32),
        in_specs=[pl.BlockSpec((N_SHARD, rows, C), lambda r: (0, r, 0))],
        out_specs=pl.BlockSpec((rows, C), lambda r: (r, 0)), compiler_params=_params(),
    )(parts)


def _rs_pair_share(halves, piece_rows):
    n = len(halves)

    def body(*refs):
        ins, outs, stage = refs[:n], refs[n:2 * n], refs[2 * n:3 * n]
        send_sems, recv_sems, local_sems = refs[3 * n:]
        x, y, c, _ = _place()
        sibling = (x, y, 1 - c)
        own = _staged_copies(ins, [outs[k].at[pl.ds(c * halves[k].shape[0], halves[k].shape[0]), :] for k in range(n)],
                             stage, local_sems)
        own.load()
        for k in range(n):
            H = halves[k].shape[0]
            for r, pr in _d2d_pieces(H, piece_rows[k]):
                pltpu.make_async_remote_copy(
                    src_ref=ins[k].at[pl.ds(r, pr), :], dst_ref=outs[k].at[pl.ds(c * H + r, pr), :],
                    send_sem=send_sems.at[k], recv_sem=recv_sems.at[k], device_id=sibling, device_id_type=MESH).start()
        own.store()
        for k in range(n):
            H = halves[k].shape[0]
            whole = pltpu.make_async_remote_copy(
                src_ref=ins[k], dst_ref=outs[k].at[pl.ds((1 - c) * H, H), :], send_sem=send_sems.at[k],
                recv_sem=recv_sems.at[k], device_id=sibling, device_id_type=MESH)
            whole.wait_recv()
            whole.wait_send()
        own.finish()

    return pl.pallas_call(
        body, name="rs_pair_share", out_shape=[jax.ShapeDtypeStruct((2 * h.shape[0], h.shape[1]), F32) for h in halves],
        in_specs=[_any_spec()] * n, out_specs=[_any_spec()] * n,
        scratch_shapes=[pltpu.VMEM(h.shape, F32) for h in halves] + [
            pltpu.SemaphoreType.DMA((n,)), pltpu.SemaphoreType.DMA((n,)), pltpu.SemaphoreType.DMA((2 * n,))],
        compiler_params=_params(),
    )(*halves)


_HBM = pl.BlockSpec(memory_space=pltpu.HBM)
_SEM = pl.BlockSpec(memory_space=pltpu.SEMAPHORE)
_EFFECT = pltpu.SideEffectType.DATAFLOW_SIDE_EFFECTING


def _gather_plan(srcs, lands, x, y, c, chips):
    plan = []
    for k in range(len(srcs)):
        H = srcs[k].shape[0] // 2
        for px, py in chips:
            plan.append((srcs[k].at[pl.ds(c * H, H), :], lands[k].at[2 * x + y, pl.ds(c * H, H), :], (px, py, c),
                         lands[k].at[2 * px + py, pl.ds(c * H, H), :]))
    return plan


def _scatter_plan(srcs, lands, x, y, c, chips):
    plan = []
    for k in range(len(srcs)):
        for px, py in chips:
            plan.append((srcs[k].at[2 * px + py], lands[k].at[2 * x + y], (px, py, c), lands[k].at[2 * px + py]))
    return plan


def _ici_start(srcs, lands, make_plan, name):
    n = len(srcs)
    ncopy = 3 * n

    def body(*refs):
        ins, lnd = refs[:n], refs[n:2 * n]
        send_sems, recv_sems = refs[2 * n], refs[2 * n + 1]
        token = refs[-1]
        x, y, c, chips = _place()
        for i, (src, dst, peer, _) in enumerate(make_plan(ins, lnd, x, y, c, chips)):
            pltpu.make_async_remote_copy(src_ref=src, dst_ref=dst, send_sem=send_sems.at[i], recv_sem=recv_sems.at[i],
                                         device_id=peer, device_id_type=MESH).start()
        token[...] = jnp.zeros_like(token)

    arrays = list(srcs) + list(lands)
    return pl.pallas_call(
        body, name=name,
        out_shape=(pltpu.SemaphoreType.DMA((ncopy,)), pltpu.SemaphoreType.DMA((ncopy,)),
                   *[pltpu.HBM(a.shape, a.dtype) for a in arrays], jax.ShapeDtypeStruct((8, LANES), F32)),
        in_specs=[_HBM] * (2 * n), out_specs=(_SEM, _SEM, *[_HBM] * (2 * n), pl.BlockSpec(memory_space=pltpu.VMEM)),
        input_output_aliases={i: 2 + i for i in range(2 * n)},
        compiler_params=pltpu.CompilerParams(has_side_effects=_EFFECT),
    )(*[pltpu.with_memory_space_constraint(a, pltpu.HBM) for a in arrays])


def _ici_wait(started, after, make_plan, name):
    send_sems, recv_sems = started[0], started[1]
    arrays = list(started[2:-1])
    n = len(arrays) // 2

    def body(*refs):
        ins, lnd = refs[:n], refs[n:2 * n]
        send_sems, recv_sems = refs[2 * n], refs[2 * n + 1]
        x, y, c, chips = _place()
        for i, (src, _, peer, landed) in enumerate(make_plan(ins, lnd, x, y, c, chips)):
            cp = pltpu.make_async_remote_copy(src_ref=src, dst_ref=landed, send_sem=send_sems.at[i],
                                              recv_sem=recv_sems.at[i], device_id=peer, device_id_type=MESH)
            cp.wait_send()
            cp.wait_recv()

    outs = pl.pallas_call(
        body, name=name, out_shape=tuple(pltpu.HBM(a.shape, a.dtype) for a in arrays),
        in_specs=[_HBM] * (2 * n) + [_SEM, _SEM, pl.BlockSpec(memory_space=pl.ANY)], out_specs=tuple([_HBM] * (2 * n)),
        input_output_aliases={i: i for i in range(2 * n)},
        compiler_params=pltpu.CompilerParams(has_side_effects=_EFFECT),
    )(*arrays, send_sems, recv_sems, after)
    return list(outs[n:])


def _gather_finish(parts, lands):
    n = len(parts)

    def body(*refs):
        ins, lnd, outs, stage = refs[:n], refs[n:2 * n], refs[2 * n:3 * n], refs[3 * n:4 * n]
        send_sems, recv_sems, local_sems = refs[4 * n:]
        x, y, c, chips = _place()
        sibling = (x, y, 1 - c)
        own = _staged_copies(ins, [o.at[2 * x + y] for o in outs], stage, local_sems)
        own.load()
        sends = []
        for k in range(n):
            H = parts[k].shape[0] // 2
            for j, (px, py) in enumerate(chips):
                half = outs[k].at[2 * px + py, pl.ds(c * H, H), :]
                sends.append(pltpu.make_async_remote_copy(src_ref=half, dst_ref=half, send_sem=send_sems.at[3 * k + j],
                                                          recv_sem=recv_sems.at[3 * k + j], device_id=sibling, device_id_type=MESH))
                sends[-1].start()
        own.store()
        for k in range(n):
            H = parts[k].shape[0] // 2
            for j, (px, py) in enumerate(chips):
                other = outs[k].at[2 * px + py, pl.ds((1 - c) * H, H), :]
                pltpu.make_async_remote_copy(src_ref=other, dst_ref=other, send_sem=send_sems.at[3 * k + j],
                                             recv_sem=recv_sems.at[3 * k + j], device_id=sibling, device_id_type=MESH).wait_recv()
        for cp in sends:
            cp.wait_send()
        own.finish()

    return pl.pallas_call(
        body, name="gather_finish", out_shape=[jax.ShapeDtypeStruct(l.shape, l.dtype) for l in lands],
        in_specs=[_any_spec()] * (2 * n), out_specs=[_any_spec()] * n,
        input_output_aliases={n + k: k for k in range(n)},
        scratch_shapes=[pltpu.VMEM(p.shape, p.dtype) for p in parts] + [
            pltpu.SemaphoreType.DMA((3 * n,)), pltpu.SemaphoreType.DMA((3 * n,)), pltpu.SemaphoreType.DMA((2 * n,))],
        compiler_params=_params(),
    )(*parts, *lands)


def _place_own(part, chip, rows, name):
    _, H, C = part.shape

    def body(chip_ref, p_ref, o_ref):
        o_ref[...] = p_ref[...]

    spec = pl.BlockSpec((1, rows, C), lambda r, chip_ref: (chip_ref[0], r, 0))
    return pl.pallas_call(
        body, name=name, out_shape=jax.ShapeDtypeStruct(part.shape, part.dtype),
        grid_spec=pltpu.PrefetchScalarGridSpec(num_scalar_prefetch=1, grid=(H // rows,), in_specs=[spec], out_specs=spec),
        compiler_params=_params(),
    )(chip, part)


SMALL_COLS = 1024


def _small_rows(shapes):
    starts, row = [], 0
    for r, cdim in shapes:
        starts.append(row)
        row += r * (-(-cdim // SMALL_COLS))
    return starts, -(-row // 8) * 8


def _allreduce_small(arrays, sink_rows, loss):
    n = len(arrays)
    shapes = [a.shape for a in arrays] + [(1, SWA_HEADS), (1, 1)]
    starts, total_rows = _small_rows(shapes)

    def pieces(k):
        r, cdim = shapes[k]
        per = -(-cdim // SMALL_COLS)
        return [(i, SMALL_COLS * j, min(SMALL_COLS, cdim - SMALL_COLS * j), starts[k] + per * i + j)
                for i in range(r) for j in range(per)]

    def body(*refs):
        ins, sink_ref, loss_ref = refs[:n], refs[n], refs[n + 1]
        outs = refs[n + 2:2 * n + 4]
        mine, all_ref, tot_ref, send_sems, recv_sems = refs[2 * n + 4:]
        x, y, c, _ = _place()
        me = 4 * x + 2 * y + c
        mine[...] = jnp.zeros_like(mine)
        for k in range(n):
            for i, col, wd, row in pieces(k):
                mine[row:row + 1, 0:wd] = ins[k][i:i + 1, col:col + wd]
        lane = _iota((1, SMALL_COLS), 1)
        sinks = jnp.zeros((1, SMALL_COLS), F32)
        for h in range(SWA_HEADS):
            head = jnp.sum(sink_ref[SWA_BLOCK * h:SWA_BLOCK * (h + 1), :]) * (1.0 / LANES)
            sinks = jnp.where(lane == h, head, sinks)
        mine[starts[n]:starts[n] + 1, :] = sinks
        mine[starts[n + 1]:starts[n + 1] + 1, 0:LANES] = loss_ref[0:1, :]
        all_ref[me] = mine[...]
        sends = []
        for k in range(1, 8):
            kx, ky, kc = (k >> 2) & 1, (k >> 1) & 1, k & 1
            peer = (x ^ kx, y ^ ky, c ^ kc)
            cp = pltpu.make_async_remote_copy(
                src_ref=mine, dst_ref=all_ref.at[me], send_sem=send_sems.at[k - 1], recv_sem=recv_sems.at[k - 1],
                device_id=peer, device_id_type=MESH)
            cp.start()
            sends.append(cp)
        for k in range(1, 8):
            kx, ky, kc = (k >> 2) & 1, (k >> 1) & 1, k & 1
            src = 4 * (x ^ kx) + 2 * (y ^ ky) + (c ^ kc)
            pltpu.make_async_remote_copy(
                src_ref=mine, dst_ref=all_ref.at[src], send_sem=send_sems.at[k - 1], recv_sem=recv_sems.at[k - 1],
                device_id=(x, y, c), device_id_type=MESH).wait_recv()
        for cp in sends:
            cp.wait_send()
        total = all_ref[0]
        for d in range(1, 8):
            total = total + all_ref[d]
        tot_ref[...] = total
        for k in range(n + 2):
            for i, col, wd, row in pieces(k):
                outs[k][i:i + 1, col:col + wd] = tot_ref[row:row + 1, 0:wd]

    vm = pl.BlockSpec(memory_space=pltpu.VMEM)
    buf = pltpu.VMEM((total_rows, SMALL_COLS), F32)
    return pl.pallas_call(
        body, name="allreduce_small", out_shape=[jax.ShapeDtypeStruct(s, F32) for s in shapes],
        in_specs=[vm] * (n + 2), out_specs=[vm] * (n + 2),
        scratch_shapes=[buf, pltpu.VMEM((8, total_rows, SMALL_COLS), F32), buf,
                        pltpu.SemaphoreType.DMA((7,)), pltpu.SemaphoreType.DMA((7,))],
    )(*arrays, sink_rows, loss)


BIG_NAMES = ("w_in", "w_out", "w_up", "w_down")
MATRIX_NAMES = BIG_NAMES + ("gla_gate_up", "conv_w")
LATE_NAMES = ("w_out", "w_up", "w_down")
RS_LABELS = BIG_NAMES
GATE_SHARD = (16, GLA_QK // N_SHARD)
CONVW_SHARD = (3, SHARD_FF)
SMALL_W_ROWS = 96
PIECE_ROWS = (128, 128, 64, 88)
ADD_ROWS = (256, 128, 256, 176)
FIRST_UNIT_ROWS = (256, SMALL_W_ROWS // 2)


def _pad_rows(flat, rows):
    return jnp.pad(flat, (0, rows * LANES - flat.shape[0])).reshape(rows, LANES)


def _pack_small_weights(gate_up, conv_w):
    bits = lax.bitcast_convert_type(conv_w, BF16)
    return _pad_rows(jnp.concatenate([gate_up.astype(BF16).reshape(-1), bits.reshape(-1)]), SMALL_W_ROWS)


def _unpack_small_weights(packed):
    flat = packed.reshape(N_SHARD, -1)
    n_gate = GATE_SHARD[0] * GATE_SHARD[1]
    n_conv = 2 * CONVW_SHARD[0] * CONVW_SHARD[1]
    gate = flat[:, :n_gate].reshape((N_SHARD,) + GATE_SHARD)
    conv = lax.bitcast_convert_type(flat[:, n_gate:n_gate + n_conv].reshape((N_SHARD,) + CONVW_SHARD + (2,)), F32)
    return (jnp.transpose(gate, (1, 0, 2)).reshape(16, GLA_QK), jnp.transpose(conv, (1, 0, 2)).reshape(3, 2 * D_FF))


def _permute_w_in(w):
    pad = jnp.zeros((w.shape[0], IN_WIDTH_PAD - IN_WIDTH), w.dtype)
    return jnp.concatenate([w[:, 0:1024], w[:, 1040:2320], w[:, 1024:1040], pad], axis=1)


def _unpermute_w_in(wp):
    return jnp.concatenate([wp[:, 0:1024], wp[:, 2304:2320], wp[:, 1024:2304]], axis=1)


def _rope_tables(positions):
    half = ROPE_DIM // 2
    inv_freq = ROPE_THETA ** (-jnp.arange(half, dtype=F32) * (2.0 / ROPE_DIM))
    d = jnp.arange(LANES) % SWA_HD
    freq = jnp.where(d < ROPE_DIM, inv_freq[d % half], 0.0)
    ang = positions.astype(F32)[:, None] * freq
    cos, sin = jnp.cos(ang), jnp.sin(ang)
    return cos, jnp.where(d < half, -sin, 0.0), jnp.where((d >= half) & (d < ROPE_DIM), sin, 0.0)


SMALL_NAMES = (("pre_mix_norm", 1024), ("gla_gate_bias", 256), ("gla_out_norm", 128), ("swa_sinks", 8),
               ("post_mix_norm", 1024), ("pre_ffn_norm", 1024), ("conv_b", 5632), ("post_ffn_norm", 1024))


def _local_step(x, positions, target, w, small, late_weights, early_grads):
    rc, rsa, rsb = _rope_tables(positions)
    wp = w["wp"]
    gup = jnp.pad(w["gla_gate_up"], ((0, LANES - 16), (0, 0)))
    g1, g2, g3, g4 = (small[n] for n in ("pre_mix_norm", "post_mix_norm", "pre_ffn_norm", "post_ffn_norm"))
    gbias, gnorm, cb = small["gla_gate_bias"], small["gla_out_norm"], small["conv_b"]
    sinks = small["swa_sinks"].reshape(-1)
    cw = w["conv_w"]

    h1, q, k, v, la, gg, sq, kd, vd, glr = _proj_fwd(x, g1, wp, gup, gbias, rc, rsa, rsb)
    og, s_all = _gla_fwd(q, k, v, la)
    osw = _swa_fwd(sq, kd, vd, sinks)
    w_out, w_up4, w_down = late_weights(osw)
    x1, cat = _mix_out_fwd(x, og, gg, osw, gnorm, w_out, g2)
    h2, up, act, c1, c2, y, dx2, loss = _ffn_fwd(x1, g3, w_up4, cw, cb, w_down, g4, target)

    dy, dup, dx1, dg4, dg3, dcb, dcw = _ffn_bwd(dx2, y, g4, up, c1, c2, cw, w_down, w_up4, x1, g3)
    zero = early_grads(_matmul_tn(h2, dup, SHARD_FF, "grad_w_up", column_blocks_major=True),
                       _matmul_tn(act, dy, D_MODEL, "grad_w_down"))
    dmix, dog, dgg, dosw, dg2, dgn = _mix_out_bwd(dx1, cat, g2 + zero, w_out, og, gg, gnorm)
    dsq, dkd, dvd, dsink = _swa_bwd(sq, kd, vd, sinks, dosw)
    dq, dk, dv, dz = _gla_bwd(q, k, v, la, s_all, dog)
    dx, dproj, dg1, dgup, dgb = _proj_bwd(x, g1, wp, gup, glr, dq, dk, dv, dgg, dsq, dkd, dvd, dz, rc, rsa, rsb, dx1)

    grads = {
        "wp": _matmul_tn(h1, dproj, IN_WIDTH_PAD, "grad_w_in"),
        "w_out": _matmul_tn(cat, dmix, D_MODEL, "grad_w_out"),
        "gla_gate_up": dgup[0:16],
        "conv_w": dcw,
    }
    small_grads = {
        "pre_mix_norm": dg1, "gla_gate_bias": dgb, "gla_out_norm": dgn,
        "post_mix_norm": dg2, "pre_ffn_norm": dg3, "conv_b": dcb, "post_ffn_norm": dg4,
    }
    return loss, dx, grads, small_grads, dsink


ADAM_ROWS = {"w_in": 256, "w_out": 256, "w_up": 256, "w_down": 176}
WEIGHT_ORDER = ("pre_mix_norm", "w_in", "gla_gate_up", "gla_gate_bias", "gla_out_norm", "swa_sinks", "w_out",
                "post_mix_norm", "pre_ffn_norm", "w_up", "conv_w", "conv_b", "w_down", "post_ffn_norm")
def kernel(x, positions, pre_mix_norm, w_in, gla_gate_up, gla_gate_bias, gla_out_norm, swa_sinks, w_out, post_mix_norm, pre_ffn_norm, w_up, conv_w, conv_b, w_down, post_ffn_norm, loss_target, m_pre_mix_norm, m_w_in, m_gla_gate_up, m_gla_gate_bias, m_gla_out_norm, m_swa_sinks, m_w_out, m_post_mix_norm, m_pre_ffn_norm, m_w_up, m_conv_w, m_conv_b, m_w_down, m_post_ffn_norm, v_pre_mix_norm, v_w_in, v_gla_gate_up, v_gla_gate_bias, v_gla_out_norm, v_swa_sinks, v_w_out, v_post_mix_norm, v_pre_ffn_norm, v_w_up, v_conv_w, v_conv_b, v_w_down, v_post_ffn_norm):
    weights = dict(pre_mix_norm=pre_mix_norm, w_in=w_in, gla_gate_up=gla_gate_up, gla_gate_bias=gla_gate_bias,
                   gla_out_norm=gla_out_norm, swa_sinks=swa_sinks, w_out=w_out, post_mix_norm=post_mix_norm,
                   pre_ffn_norm=pre_ffn_norm, w_up=w_up, conv_w=conv_w, conv_b=conv_b, w_down=w_down,
                   post_ffn_norm=post_ffn_norm)
    mom = dict(pre_mix_norm=m_pre_mix_norm, w_in=m_w_in, gla_gate_up=m_gla_gate_up, gla_gate_bias=m_gla_gate_bias,
               gla_out_norm=m_gla_out_norm, swa_sinks=m_swa_sinks, w_out=m_w_out, post_mix_norm=m_post_mix_norm,
               pre_ffn_norm=m_pre_ffn_norm, w_up=m_w_up, conv_w=m_conv_w, conv_b=m_conv_b, w_down=m_w_down,
               post_ffn_norm=m_post_ffn_norm)
    var = dict(pre_mix_norm=v_pre_mix_norm, w_in=v_w_in, gla_gate_up=v_gla_gate_up, gla_gate_bias=v_gla_gate_bias,
               gla_out_norm=v_gla_out_norm, swa_sinks=v_swa_sinks, w_out=v_w_out, post_mix_norm=v_post_mix_norm,
               pre_ffn_norm=v_pre_ffn_norm, w_up=v_w_up, conv_w=v_conv_w, conv_b=v_conv_b, w_down=v_w_down,
               post_ffn_norm=v_post_ffn_norm)
    weights, mom, var = ({n: a[0] if a.ndim == 3 else a for n, a in d.items()} for d in (weights, mom, var))

    core = lax.axis_index("c").astype(jnp.int32).reshape(1)
    chip = (2 * lax.axis_index("x") + lax.axis_index("y")).astype(jnp.int32).reshape(1)
    small = {n: weights[n] for n, _ in SMALL_NAMES}

    win4, small4 = _allgather_shards(
        [weights["w_in"].astype(BF16), _pack_small_weights(weights["gla_gate_up"], weights["conv_w"])], FIRST_UNIT_ROWS)
    gate_full, convw_full = _unpack_small_weights(small4)
    first = {"wp": _permute_w_in(jnp.transpose(win4, (1, 0, 2)).reshape(D_MODEL, IN_WIDTH)),
             "gla_gate_up": gate_full, "conv_w": convw_full}
    late_shards = [weights[n].astype(BF16) for n in LATE_NAMES]
    gathering = _ici_start(late_shards, [lax.empty((N_SHARD,) + s.shape, BF16) for s in late_shards], _gather_plan, "gather_start")
    small["pre_mix_norm"] = small["pre_mix_norm"] + gathering[-1][0, 0]

    def late_weights(after):
        wout4, wup4, wdown4 = _gather_finish(late_shards, _ici_wait(gathering, after, _gather_plan, "gather_wait"))
        return wout4.reshape(D_MODEL, D_MODEL), wup4, wdown4.reshape(D_FF, D_MODEL)

    def pair_partials(arrs, which):
        got = _rs_pair_swap(arrs, [PIECE_ROWS[i] for i in which], "rs_pair_swap_" + RS_LABELS[which[0]])
        return [_rs_add_pair(a, g, core, ADD_ROWS[i], "rs_add_pair_" + RS_LABELS[i]) for a, g, i in zip(arrs, got, which)]

    early = {}

    def early_grads(g_up4, g_down):
        parts = pair_partials([g_up4, g_down.reshape(N_SHARD, D_FF // N_SHARD, D_MODEL)], (2, 3))
        lands = [_place_own(p, chip, ADD_ROWS[i], "rs_place_own_" + RS_LABELS[i]) for p, i in zip(parts, (2, 3))]
        early["scatter"] = _ici_start(parts, lands, _scatter_plan, "rs_scatter_start")
        return early["scatter"][-1][0, 0]

    loss, dx, grads, small_grads, sink_rows = _local_step(
        x[0], positions[0], loss_target[0], first, small, late_weights, early_grads)

    landed_ffn = _ici_wait(early["scatter"], dx, _scatter_plan, "rs_scatter_wait")
    rest = [jnp.transpose(_unpermute_w_in(grads["wp"]).reshape(D_MODEL, N_SHARD, IN_WIDTH // N_SHARD), (1, 0, 2)),
            grads["w_out"].reshape(N_SHARD, D_MODEL // N_SHARD, D_MODEL)]
    landed = list(_rs_chip_scatter(pair_partials(rest, (0, 1)))) + list(landed_ffn)
    halves = [_rs_sum_chips(p, rows, "rs_sum_chips_" + n) for p, rows, n in zip(landed, ADD_ROWS, BIG_NAMES)]
    reduced = _rs_pair_share(halves, PIECE_ROWS)
    vectors = [n for n, _ in SMALL_NAMES if n != "swa_sinks"]
    summed = _allreduce_small([small_grads[n] for n in vectors] + [grads["gla_gate_up"], grads["conv_w"]], sink_rows, loss)
    g_all = {**dict(zip(vectors, summed)), **dict(zip(BIG_NAMES, reduced)), "swa_sinks": summed[-2],
             "gla_gate_up": lax.dynamic_slice_in_dim(summed[-4], chip[0] * GATE_SHARD[1], GATE_SHARD[1], axis=1),
             "conv_w": lax.dynamic_slice_in_dim(summed[-3], chip[0] * SHARD_FF, SHARD_FF, axis=1)}
    loss_sum = summed[-1][0, 0]

    delta, new_m, new_v = {}, {}, {}
    for n in BIG_NAMES:
        delta[n], new_m[n], new_v[n] = _adamw(weights[n], g_all[n], mom[n], var[n], ADAM_ROWS[n], "adamw_" + n)
    tiny = [n for n, _ in SMALL_NAMES] + ["gla_gate_up", "conv_w"]
    for res, vals in zip((delta, new_m, new_v), _adamw_small(*([d[n] for n in tiny] for d in (weights, g_all, mom, var)))):
        res.update(zip(tiny, vals))

    def lead(n, a):
        return a[None] if n in MATRIX_NAMES else a

    outs = [loss_sum, dx[None]]
    for d in (g_all, delta, new_m, new_v):
        outs.extend(lead(n, d[n]) for n in WEIGHT_ORDER)
    return tuple(outs)
```

```python
import functools

import jax
import jax.numpy as jnp
from jax import lax
from jax.experimental import pallas as pl
from jax.experimental.pallas import tpu as pltpu

F32 = jnp.float32
BF16 = jnp.bfloat16
MESH = pl.DeviceIdType.MESH

D_MODEL = 1024
GLA_HEADS = 4
GLA_DK = 64
GLA_DV = 128
GLA_TAU = 16.0
GLA_CHUNK = 64
SWA_HEADS = 8
SWA_HD = 64
SWA_BLOCK = 128
ROPE_THETA = 500000.0
ROPE_DIM = 16
D_FF = 2816
EPS = 1e-6
GLA_QK = 256
GLA_V = 512
SWA_Q = 512
SWA_KV = 128
IN_WIDTH = 2320
IN_WIDTH_PAD = 2432
N_SHARD = 4

ADAM_LR = 0.001
ADAM_B1 = 0.9
ADAM_B2 = 0.999
ADAM_EPS = 1e-08
ADAM_WD = 0.01
ADAM_STEP = 10

LANES = 128
VMEM_LIMIT = 56 * 1024 * 1024
TM = 256
SHARD_FF = 2 * D_FF // N_SHARD
FF_PIECES = ((0, 1408),)
GLA_BLOCK = 256


def _wide_tile(T):
    return 2 * TM if T % (2 * TM) == 0 else TM


def _params(**kw):
    return pltpu.CompilerParams(vmem_limit_bytes=VMEM_LIMIT, **kw)


def _mm(a, b):
    return lax.dot_general(a.astype(BF16), b.astype(BF16), (((1,), (0,)), ((), ())), preferred_element_type=F32)


def _mm_nt(a, b):
    return lax.dot_general(a.astype(BF16), b.astype(BF16), (((1,), (1,)), ((), ())), preferred_element_type=F32)


def _mm_tn(a, b):
    return lax.dot_general(a.astype(BF16), b.astype(BF16), (((0,), (0,)), ((), ())), preferred_element_type=F32)


def _mm_f32(a, b):
    return lax.dot_general(a, b, (((1,), (0,)), ((), ())), preferred_element_type=F32, precision=lax.Precision.HIGHEST)


def _iota(shape, dim):
    return lax.broadcasted_iota(jnp.int32, shape, dim)


def _sigmoid(x):
    return 1.0 / (1.0 + jnp.exp(-x))


def _gelu_parts(x):
    c = 0.7978845608028654
    x2 = x * x
    t = jnp.tanh(c * (x + 0.044715 * (x2 * x)))
    cdf = 0.5 * (1.0 + t)
    dcdf = 0.5 * (1.0 - t * t) * c * (1.0 + 3.0 * 0.044715 * x2)
    return x * cdf, cdf + x * dcdf


def _rms_bwd(v, r, g, dout):
    gd = g * dout
    return r * gd - v * (r * r * r) * jnp.mean(v * gd, axis=-1, keepdims=True)


def _row_spec(tm, cols):
    return pl.BlockSpec((tm, cols), lambda i: (i, 0))


def _const_spec(shape):
    return pl.BlockSpec(shape, lambda i: (0,) * len(shape))


def _any_spec():
    return pl.BlockSpec(memory_space=pl.ANY)


def _load_once(src_hbm, dst_vmem, sem):
    @pl.when(pl.program_id(0) == 0)
    def _():
        cp = pltpu.make_async_copy(src_hbm, dst_vmem, sem)
        cp.start()
        cp.wait()


def _rotate(v, rc, rsa, rsb):
    return v * rc + pltpu.roll(v, 120, 1) * rsa + pltpu.roll(v, 8, 1) * rsb


def _rotate_bwd(dv, rc, rsa, rsb):
    return dv * rc + pltpu.roll(dv * rsa, 8, 1) + pltpu.roll(dv * rsb, 120, 1)


def _proj_fwd(x, g1, wp, gup, gbias, rc, rsa, rsb):
    T = x.shape[0]
    TM = _wide_tile(T)

    def body(x_ref, g1_ref, wp_hbm, gup_ref, gb_ref, rc_ref, rsa_ref, rsb_ref,
             h1_ref, q_ref, k_ref, v_ref, la_ref, gg_ref, sq_ref, kd_ref, vd_ref, glr_ref, wp_v, sem):
        _load_once(wp_hbm, wp_v, sem)
        xt = x_ref[...]
        r = lax.rsqrt(jnp.mean(xt * xt, axis=-1, keepdims=True) + EPS)
        h = (xt * r * g1_ref[...]).astype(BF16)
        h1_ref[...] = h
        q_ref[...] = _mm(h, wp_v[:, 0:256])
        k_ref[...] = _mm(h, wp_v[:, 256:512])
        v_ref[...] = _mm(h, wp_v[:, 512:1024]).astype(BF16)
        gg_ref[...] = _mm(h, wp_v[:, 1024:1536]).astype(BF16)
        glr = _mm(h, wp_v[:, 2304:2432]).astype(BF16)
        glr_ref[...] = glr
        z = _mm(glr, gup_ref[...]) + gb_ref[...]
        la_ref[...] = (jnp.minimum(z, 0.0) - jnp.log1p(jnp.exp(-jnp.abs(z)))) * (1.0 / GLA_TAU)
        rc_, rsa_, rsb_ = rc_ref[...], rsa_ref[...], rsb_ref[...]
        for s in range(4):
            qs = _mm(h, wp_v[:, 1536 + 128 * s:1664 + 128 * s])
            sq_ref[:, 128 * s:128 * s + 128] = (_rotate(qs, rc_, rsa_, rsb_) * 0.125).astype(BF16)
        lane = _iota((TM, LANES), 1)
        first = lane < 64
        kr = _rotate(_mm(h, wp_v[:, 2048:2176]), rc_, rsa_, rsb_)
        krr = pltpu.roll(kr, 64, 1)
        kd_ref[:, 0:128] = jnp.where(first, kr, krr).astype(BF16)
        kd_ref[:, 128:256] = jnp.where(first, krr, kr).astype(BF16)
        vr = _mm(h, wp_v[:, 2176:2304])
        vrr = pltpu.roll(vr, 64, 1)
        vd_ref[:, 0:128] = jnp.where(first, vr, vrr).astype(BF16)
        vd_ref[:, 128:256] = jnp.where(first, vrr, vr).astype(BF16)

    outs = [
        jax.ShapeDtypeStruct((T, D_MODEL), BF16),
        jax.ShapeDtypeStruct((T, GLA_QK), F32),
        jax.ShapeDtypeStruct((T, GLA_QK), F32),
        jax.ShapeDtypeStruct((T, GLA_V), BF16),
        jax.ShapeDtypeStruct((T, GLA_QK), F32),
        jax.ShapeDtypeStruct((T, GLA_V), BF16),
        jax.ShapeDtypeStruct((T, SWA_Q), BF16),
        jax.ShapeDtypeStruct((T, 256), BF16),
        jax.ShapeDtypeStruct((T, 256), BF16),
        jax.ShapeDtypeStruct((T, LANES), BF16),
    ]
    return pl.pallas_call(
        body, name="proj_fwd", grid=(T // TM,), out_shape=outs,
        in_specs=[_row_spec(TM, D_MODEL), _const_spec((1, D_MODEL)), _any_spec(), _const_spec((LANES, GLA_QK)),
                  _const_spec((1, GLA_QK)), _row_spec(TM, LANES), _row_spec(TM, LANES), _row_spec(TM, LANES)],
        out_specs=[_row_spec(TM, o.shape[1]) for o in outs],
        scratch_shapes=[pltpu.VMEM((D_MODEL, IN_WIDTH_PAD), BF16), pltpu.SemaphoreType.DMA],
        compiler_params=_params(),
    )(x, g1, wp, gup, gbias, rc, rsa, rsb)


GLA_NB = GLA_BLOCK // GLA_CHUNK


def _gla_masks():
    n = GLA_BLOCK
    lane = _iota((n, LANES), 1)
    lane_masks = [(lane < 64).astype(F32), (lane >= 64).astype(F32)]
    row, col = _iota((n, n), 0), _iota((n, n), 1)
    same_chunk = (row >> 6) == (col >> 6)
    blk = ((_iota((256, LANES), 0) >> 7) == (_iota((256, LANES), 1) >> 6)).astype(F32)
    return lane_masks, same_chunk & (col <= row), same_chunk & (col >= row), blk


def _chunk_rows(vals):
    return jnp.concatenate([jnp.broadcast_to(v, (GLA_CHUNK, LANES)) for v in vals], axis=0)


def _gla_block_terms(q_ref, k_ref, b_ref, p):
    C = GLA_CHUNK
    cols = slice(LANES * p, LANES * p + LANES)
    bc = b_ref[:, cols]
    bl_rows = [b_ref[C * c + C - 1:C * c + C, cols] for c in range(GLA_NB)]
    bl = _chunk_rows(bl_rows)
    bm = _chunk_rows([b_ref[C * c + C // 2 - 1:C * c + C // 2, cols] for c in range(GLA_NB)])
    qs = q_ref[:, cols] * 0.125
    kk = k_ref[:, cols]
    eb = jnp.exp(bc)
    ekl = jnp.exp(bl - bc)
    eqm = jnp.exp(bc - bm)
    ekm = jnp.exp(bm - bc)
    return qs, kk, eb, ekl, eqm, ekm, [jnp.exp(r) for r in bl_rows]


def _block_cumsum(la, mask):
    return _mm_f32(mask.astype(F32), la)


def _gla_fwd(q, k, v, la):
    T = q.shape[0]
    NB = GLA_BLOCK // GLA_CHUNK
    C = GLA_CHUNK

    def body(q_ref, k_ref, v_ref, la_ref, o_ref, s_ref, st_ref, b_ref):
        @pl.when(pl.program_id(0) == 0)
        def _():
            st_ref[...] = jnp.zeros_like(st_ref)

        lane_masks, causal, _, blk = _gla_masks()
        b_ref[...] = _block_cumsum(la_ref[...], causal)
        for p in range(2):
            qs, kk, eb, ekl, eqm, ekm, gam = _gla_block_terms(q_ref, k_ref, b_ref, p)
            qh, kh, qm, km = qs * eb, kk * ekl, qs * eqm, kk * ekm
            vp = v_ref[:, 256 * p:256 * p + 256]
            intra = []
            for j in range(2):
                a = jnp.where(causal, _mm_nt(qm * lane_masks[j], km), 0.0)
                intra.append(_mm(a, vp[:, 128 * j:128 * j + 128]))
            kv = [blk * _mm_tn(vp[C * c:C * c + C], kh[C * c:C * c + C]) for c in range(NB)]
            st = st_ref[p]
            inter = []
            for c in range(NB):
                s_ref[c, p] = st[0:LANES] + st[LANES:2 * LANES]
                inter.append(_mm_nt(qh[C * c:C * c + C], st))
                st = st * gam[c] + kv[c]
            st_ref[p] = st
            o_ref[:, 256 * p:256 * p + 256] = (jnp.concatenate(inter, axis=0) + jnp.concatenate(intra, axis=1)).astype(BF16)

    return pl.pallas_call(
        body, name="gla_fwd", grid=(T // GLA_BLOCK,),
        out_shape=[jax.ShapeDtypeStruct((T, GLA_V), BF16), jax.ShapeDtypeStruct((T // C, 2, LANES, LANES), F32)],
        in_specs=[_row_spec(GLA_BLOCK, GLA_QK), _row_spec(GLA_BLOCK, GLA_QK), _row_spec(GLA_BLOCK, GLA_V),
                  _row_spec(GLA_BLOCK, GLA_QK)],
        out_specs=[_row_spec(GLA_BLOCK, GLA_V), pl.BlockSpec((NB, 2, LANES, LANES), lambda i: (i, 0, 0, 0))],
        scratch_shapes=[pltpu.VMEM((2, 256, LANES), F32), pltpu.VMEM((GLA_BLOCK, GLA_QK), F32)],
        compiler_params=_params(),
    )(q, k, v, la)


def _gla_bwd(q, k, v, la, s_all, do):
    T = q.shape[0]
    NB = GLA_BLOCK // GLA_CHUNK
    C = GLA_CHUNK
    nblk = T // GLA_BLOCK

    def body(q_ref, k_ref, v_ref, la_ref, s_ref, do_ref, dq_ref, dk_ref, dv_ref, dz_ref, dst_ref, b_ref):
        @pl.when(pl.program_id(0) == 0)
        def _():
            dst_ref[...] = jnp.zeros_like(dst_ref)

        lane_masks, causal, anti_causal, blk = _gla_masks()
        b_ref[...] = _block_cumsum(la_ref[...], causal)
        for p in range(2):
            cols = slice(LANES * p, LANES * p + LANES)
            qs, kk, eb, ekl, eqm, ekm, gam = _gla_block_terms(q_ref, k_ref, b_ref, p)
            qh, kh, qm, km = qs * eb, kk * ekl, qs * eqm, kk * ekm
            vp = v_ref[:, 256 * p:256 * p + 256]
            dop = do_ref[:, 256 * p:256 * p + 256]
            dqm = jnp.zeros((GLA_BLOCK, LANES), F32)
            dkm = jnp.zeros((GLA_BLOCK, LANES), F32)
            dv_intra = []
            for j in range(2):
                hs = slice(128 * j, 128 * j + 128)
                a = jnp.where(causal, _mm_nt(qm * lane_masks[j], km), 0.0)
                da = jnp.where(causal, _mm_nt(dop[:, hs], vp[:, hs]), 0.0)
                dv_intra.append(_mm_tn(a, dop[:, hs]))
                dqm = dqm + lane_masks[j] * _mm(da, km)
                dkm = dkm + lane_masks[j] * _mm_tn(da, qm)
            grow = [blk * _mm_tn(dop[C * c:C * c + C], qh[C * c:C * c + C]) for c in range(NB)]
            dst = dst_ref[p]
            dst_after = [None] * NB
            for c in reversed(range(NB)):
                dst_after[c] = dst
                dst = dst * gam[c] + grow[c]
            dst_ref[p] = dst
            dqh, dkh, dv_state, extra = [], [], [], []
            for c in range(NB):
                rows = slice(C * c, C * c + C)
                packed = s_ref[c, p]
                st = jnp.concatenate([packed * lane_masks[0][0:LANES], packed * lane_masks[1][0:LANES]], axis=0)
                dqh.append(_mm(dop[rows], st))
                dkh.append(_mm(vp[rows], dst_after[c]))
                dv_state.append(_mm_nt(kh[rows], dst_after[c]))
                extra.append(jnp.sum(dkh[c] * kh[rows], axis=0, keepdims=True)
                             + jnp.sum(st * dst_after[c], axis=0, keepdims=True) * gam[c])
            dqs = jnp.concatenate(dqh, axis=0) * eb + dqm * eqm
            dk = jnp.concatenate(dkh, axis=0) * ekl + dkm * ekm
            dg = _mm_f32(anti_causal.astype(F32), dqs * qs - dk * kk) + _chunk_rows(extra)
            dq_ref[:, cols] = (dqs * 0.125).astype(BF16)
            dk_ref[:, cols] = dk.astype(BF16)
            dz_ref[:, cols] = dg * (1.0 - jnp.exp(GLA_TAU * la_ref[:, cols])) * (1.0 / GLA_TAU)
            dv_ref[:, 256 * p:256 * p + 256] = (jnp.concatenate(dv_state, axis=0) + jnp.concatenate(dv_intra, axis=1)).astype(BF16)

    rev = lambda i: (nblk - 1 - i, 0)
    rspec = lambda cols: pl.BlockSpec((GLA_BLOCK, cols), rev)
    return pl.pallas_call(
        body, name="gla_bwd", grid=(nblk,),
        out_shape=[jax.ShapeDtypeStruct((T, GLA_QK), BF16), jax.ShapeDtypeStruct((T, GLA_QK), BF16),
                   jax.ShapeDtypeStruct((T, GLA_V), BF16), jax.ShapeDtypeStruct((T, GLA_QK), F32)],
        in_specs=[rspec(GLA_QK), rspec(GLA_QK), rspec(GLA_V), rspec(GLA_QK),
                  pl.BlockSpec((NB, 2, LANES, LANES), lambda i: (nblk - 1 - i, 0, 0, 0)), rspec(GLA_V)],
        out_specs=[rspec(GLA_QK), rspec(GLA_QK), rspec(GLA_V), rspec(GLA_QK)],
        scratch_shapes=[pltpu.VMEM((2, 256, LANES), F32), pltpu.VMEM((GLA_BLOCK, GLA_QK), F32)],
        compiler_params=_params(),
    )(q, k, v, la, s_all, do)


SWA_GROUP = 4


def _swa_stack(ref, g, first):
    parts = []
    for j in range(SWA_GROUP):
        m = 2 * g + j // 2
        pair = ref[:, 128 * m:128 * m + 128]
        zero = jnp.zeros_like(pair)
        parts.append(jnp.where(first, pair, zero) if j % 2 == 0 else jnp.where(first, zero, pair))
    return jnp.concatenate(parts, axis=0)


def _swa_unstack(rows, mm, first):
    W = SWA_BLOCK
    return jnp.where(first, rows[W * 2 * mm:W * (2 * mm + 1)], rows[W * (2 * mm + 1):W * (2 * mm + 2)])


def _swa_probs(qs, kp, kc, vp, vc, i, g, sink_ref, first4):
    W = SWA_BLOCK
    R = SWA_GROUP * W
    r, c = _iota((R, W), 0) & (W - 1), _iota((R, W), 1)
    neg = -1e30
    s_p = jnp.where((c > r) & (i > 0), _mm_nt(qs, kp), neg)
    s_c = jnp.where(c <= r, _mm_nt(qs, kc), neg)
    head = _iota((R, 1), 0) >> 7
    sink = jnp.where(head == 0, sink_ref[4 * g], jnp.where(head == 1, sink_ref[4 * g + 1],
                                                           jnp.where(head == 2, sink_ref[4 * g + 2], sink_ref[4 * g + 3])))
    m = jnp.maximum(jnp.max(jnp.maximum(s_p, s_c), axis=-1, keepdims=True), sink)
    p_p = jnp.exp(s_p - m)
    p_c = jnp.exp(s_c - m)
    p_s = jnp.exp(sink - m)
    one = jnp.ones((W, LANES), BF16)
    first = _iota((W, LANES), 1) < 64
    acc = _mm(p_p, jnp.where(first, vp, one)) + _mm(p_c, jnp.where(first, vc, one))
    rolled = pltpu.roll(acc, 64, 1)
    denom = jnp.where(first4, rolled, acc) + p_s
    return p_p, p_c, p_s, denom, acc, rolled


def _swa_fwd(sq, kd, vd, sinks):
    T = sq.shape[0]
    W = SWA_BLOCK
    prev = lambda i: (jnp.maximum(i - 1, 0), 0)

    def body(sink_ref, q_ref, kp_ref, kc_ref, vp_ref, vc_ref, o_ref):
        i = pl.program_id(0)
        first4 = _iota((SWA_GROUP * W, LANES), 1) < 64
        first = _iota((W, LANES), 1) < 64
        for g in range(2):
            gs = slice(128 * g, 128 * g + 128)
            qs = _swa_stack(q_ref, g, first)
            _, _, _, denom, acc, rolled = _swa_probs(qs, kp_ref[:, gs], kc_ref[:, gs], vp_ref[:, gs], vc_ref[:, gs],
                                                     i, g, sink_ref, first4)
            pv = jnp.where(first4, acc, rolled)
            o = pv / denom
            for mm in range(2):
                m = 2 * g + mm
                o_ref[:, 128 * m:128 * m + 128] = _swa_unstack(o, mm, first).astype(BF16)

    return pl.pallas_call(
        body, name="swa_fwd", grid=(T // W,), out_shape=jax.ShapeDtypeStruct((T, SWA_Q), BF16),
        in_specs=[pl.BlockSpec(memory_space=pltpu.SMEM), _row_spec(W, SWA_Q), pl.BlockSpec((W, 256), prev),
                  _row_spec(W, 256), pl.BlockSpec((W, 256), prev), _row_spec(W, 256)],
        out_specs=_row_spec(W, SWA_Q),
        compiler_params=_params(),
    )(sinks, sq, kd, kd, vd, vd)


def _swa_bwd(sq, kd, vd, sinks, do):
    T = sq.shape[0]
    W = SWA_BLOCK
    n = T // W
    cur = lambda i: (jnp.minimum(i, n - 1), 0)
    prev = lambda i: (jnp.clip(i - 1, 0, n - 1), 0)

    def body(sink_ref, q_ref, kp_ref, kc_ref, vp_ref, vc_ref, do_ref, dq_ref, dk_ref, dv_ref, ds_ref, ck_ref, cv_ref):
        i = pl.program_id(0)

        @pl.when(i == 0)
        def _():
            ds_ref[...] = jnp.zeros_like(ds_ref)
            ck_ref[...] = jnp.zeros_like(ck_ref)
            cv_ref[...] = jnp.zeros_like(cv_ref)

        @pl.when(i < n)
        def _():
            first4 = _iota((SWA_GROUP * W, LANES), 1) < 64
            first = _iota((W, LANES), 1) < 64
            for g in range(2):
                gs = slice(128 * g, 128 * g + 128)
                kp, kc, vp, vc = kp_ref[:, gs], kc_ref[:, gs], vp_ref[:, gs], vc_ref[:, gs]
                qs = _swa_stack(q_ref, g, first)
                dos = _swa_stack(do_ref, g, first)
                p_p, p_c, p_s, denom, _, _ = _swa_probs(qs, kp, kc, vp, vc, i, g, sink_ref, first4)
                inv = 1.0 / denom
                p_p, p_c = p_p * inv, p_c * inv
                dp_p = _mm_nt(dos, vp)
                dp_c = _mm_nt(dos, vc)
                delta = jnp.sum(p_p * dp_p + p_c * dp_c, axis=-1, keepdims=True)
                ds_p = p_p * (dp_p - delta)
                ds_c = p_c * (dp_c - delta)
                rows = slice(SWA_GROUP * W * g, SWA_GROUP * W * (g + 1))
                ds_ref[rows, :] = ds_ref[rows, :] - (p_s * delta) * inv
                dq = (_mm(ds_p, kp) + _mm(ds_c, kc)) * 0.125
                for mm in range(2):
                    m = 2 * g + mm
                    dq_ref[:, 128 * m:128 * m + 128] = _swa_unstack(dq, mm, first).astype(BF16)
                dk_ref[:, gs] = (ck_ref[:, gs] + _mm_tn(ds_p, qs)).astype(BF16)
                dv_ref[:, gs] = (cv_ref[:, gs] + _mm_tn(p_p, dos)).astype(BF16)
                ck_ref[:, gs] = _mm_tn(ds_c, qs)
                cv_ref[:, gs] = _mm_tn(p_c, dos)

        @pl.when(i == n)
        def _():
            dk_ref[...] = ck_ref[...].astype(BF16)
            dv_ref[...] = cv_ref[...].astype(BF16)

    return pl.pallas_call(
        body, name="swa_bwd", grid=(n + 1,),
        out_shape=[jax.ShapeDtypeStruct((T, SWA_Q), BF16), jax.ShapeDtypeStruct((T, 256), BF16),
                   jax.ShapeDtypeStruct((T, 256), BF16), jax.ShapeDtypeStruct((SWA_HEADS * W, LANES), F32)],
        in_specs=[pl.BlockSpec(memory_space=pltpu.SMEM), pl.BlockSpec((W, SWA_Q), cur), pl.BlockSpec((W, 256), prev),
                  pl.BlockSpec((W, 256), cur), pl.BlockSpec((W, 256), prev), pl.BlockSpec((W, 256), cur),
                  pl.BlockSpec((W, SWA_Q), cur)],
        out_specs=[pl.BlockSpec((W, SWA_Q), cur), pl.BlockSpec((W, 256), prev), pl.BlockSpec((W, 256), prev),
                   _const_spec((SWA_HEADS * W, LANES))],
        scratch_shapes=[pltpu.VMEM((W, 256), F32), pltpu.VMEM((W, 256), F32)],
        compiler_params=_params(),
    )(sinks, sq, kd, kd, vd, vd, do)


def _mix_out_fwd(x, og, gg, osw, gnorm, wout, g2):
    T = x.shape[0]
    TM = _wide_tile(T)

    def body(x_ref, og_ref, gg_ref, osw_ref, gn_ref, wout_ref, g2_ref, x1_ref, cat_ref):
        gn = gn_ref[...]
        for j in range(GLA_HEADS):
            hs = slice(128 * j, 128 * j + 128)
            o = og_ref[:, hs].astype(F32)
            r = lax.rsqrt(jnp.mean(o * o, axis=-1, keepdims=True) + EPS)
            gate = gg_ref[:, hs].astype(F32)
            cat_ref[:, hs] = (o * r * gn * (gate * _sigmoid(gate))).astype(BF16)
        cat_ref[:, GLA_V:] = osw_ref[...]
        mix = _mm(cat_ref[...], wout_ref[...])
        r2 = lax.rsqrt(jnp.mean(mix * mix, axis=-1, keepdims=True) + EPS)
        x1_ref[...] = x_ref[...] + mix * r2 * g2_ref[...]

    return pl.pallas_call(
        body, name="mix_out_fwd", grid=(T // TM,),
        out_shape=[jax.ShapeDtypeStruct((T, D_MODEL), F32), jax.ShapeDtypeStruct((T, D_MODEL), BF16)],
        in_specs=[_row_spec(TM, D_MODEL), _row_spec(TM, GLA_V), _row_spec(TM, GLA_V), _row_spec(TM, SWA_Q),
                  _const_spec((1, LANES)), _const_spec((D_MODEL, D_MODEL)), _const_spec((1, D_MODEL))],
        out_specs=[_row_spec(TM, D_MODEL), _row_spec(TM, D_MODEL)],
        compiler_params=_params(),
    )(x, og, gg, osw, gnorm, wout, g2)


HALO = 8


def _rows_before(v, prev1, prev2):
    row = _iota(v.shape, 0)
    m1 = jnp.where(row == 0, prev1, pltpu.roll(v, 1, 0))
    m2 = jnp.where(row == 0, prev2, jnp.where(row == 1, prev1, pltpu.roll(v, 2, 0)))
    return m1, m2


def _rows_after(v, next1, next2):
    n = v.shape[0]
    row = _iota(v.shape, 0)
    p1 = jnp.where(row == n - 1, next1, pltpu.roll(v, n - 1, 0))
    p2 = jnp.where(row == n - 1, next2, jnp.where(row == n - 2, next1, pltpu.roll(v, n - 2, 0)))
    return p1, p2


def _ff_pieces():
    return [(j, off, wd) for j in range(2) for off, wd in FF_PIECES]


def _ffn_fwd(x1, g3, wup, cw, cb, wdown, g4, target):
    T = x1.shape[0]

    def body(x1_ref, g3_ref, wup_hbm, cw_ref, cb_ref, wdn_hbm, g4_ref, tg_ref,
             h2_ref, up_ref, a_ref, c1_ref, c2_ref, y_ref, dx2_ref, loss_ref, wup_v, wdn_v, carry_ref, sems):
        _load_once(wup_hbm, wup_v, sems.at[0])
        _load_once(wdn_hbm, wdn_v, sems.at[1])

        @pl.when(pl.program_id(0) == 0)
        def _():
            carry_ref[...] = jnp.zeros_like(carry_ref)
            loss_ref[...] = jnp.zeros_like(loss_ref)

        x1 = x1_ref[...]
        r3 = lax.rsqrt(jnp.mean(x1 * x1, axis=-1, keepdims=True) + EPS)
        h2 = (x1 * r3 * g3_ref[...]).astype(BF16)
        h2_ref[...] = h2
        for j, off, wd in _ff_pieces():
            base = SHARD_FF * j + off
            u = []
            for half in range(2):
                cs = slice(D_FF * half + base, D_FF * half + base + wd)
                upb = _mm(h2, wup_v[2 * half + j, :, off:off + wd]).astype(BF16)
                up_ref[:, cs] = upb
                upf = upb.astype(F32)
                m1, m2 = _rows_before(upf, carry_ref[HALO - 1:HALO, cs], carry_ref[HALO - 2:HALO - 1, cs])
                u.append(cb_ref[:, cs] + cw_ref[0:1, cs] * m2 + cw_ref[1:2, cs] * m1 + cw_ref[2:3, cs] * upf)
                carry_ref[:, cs] = upf[TM - HALO:TM, :]
            act, dact = _gelu_parts(u[1])
            a = (act * u[0]).astype(BF16)
            out = slice(base, base + wd)
            a_ref[:, out] = a
            c1_ref[:, out] = act.astype(BF16)
            c2_ref[:, out] = (u[0] * dact).astype(BF16)
        y = _mm(a_ref[...], wdn_v[...])
        y_ref[...] = y
        r4 = lax.rsqrt(jnp.mean(y * y, axis=-1, keepdims=True) + EPS)
        err = x1 + y * r4 * g4_ref[...] - tg_ref[...]
        dx2_ref[...] = err * (1.0 / D_MODEL)
        loss_ref[...] = loss_ref[...] + jnp.sum(err * err) * (0.5 / D_MODEL)

    outs = [
        jax.ShapeDtypeStruct((T, D_MODEL), BF16),
        jax.ShapeDtypeStruct((T, 2 * D_FF), BF16),
        jax.ShapeDtypeStruct((T, D_FF), BF16),
        jax.ShapeDtypeStruct((T, D_FF), BF16),
        jax.ShapeDtypeStruct((T, D_FF), BF16),
        jax.ShapeDtypeStruct((T, D_MODEL), F32),
        jax.ShapeDtypeStruct((T, D_MODEL), F32),
        jax.ShapeDtypeStruct((8, LANES), F32),
    ]
    return pl.pallas_call(
        body, name="ffn_fwd", grid=(T // TM,), out_shape=outs,
        in_specs=[_row_spec(TM, D_MODEL), _const_spec((1, D_MODEL)), _any_spec(), _const_spec((3, 2 * D_FF)),
                  _const_spec((1, 2 * D_FF)), _any_spec(), _const_spec((1, D_MODEL)), _row_spec(TM, D_MODEL)],
        out_specs=[_row_spec(TM, D_MODEL), _row_spec(TM, 2 * D_FF), _row_spec(TM, D_FF), _row_spec(TM, D_FF),
                   _row_spec(TM, D_FF), _row_spec(TM, D_MODEL), _row_spec(TM, D_MODEL), _const_spec((8, LANES))],
        scratch_shapes=[pltpu.VMEM((N_SHARD, D_MODEL, SHARD_FF), BF16), pltpu.VMEM((D_FF, D_MODEL), BF16),
                        pltpu.VMEM((HALO, 2 * D_FF), F32), pltpu.SemaphoreType.DMA((2,))],
        compiler_params=_params(),
    )(x1, g3, wup, cw, cb, wdown, g4, target)


def _ffn_bwd(dx2, y, g4, up, c1, c2, cw, wdown, wup, x1, g3):
    T = dx2.shape[0]
    nt = T // TM
    rev = lambda i: (nt - 1 - i, 0)

    def body(dn_ref, y_ref, g4_ref, up_ref, c1_ref, c2_ref, cw_ref, wdn_hbm, wup_hbm, x1_ref, g3_ref,
             dy_ref, dup_ref, dx1_ref, dg4_ref, dg3_ref, dcb_ref, dcw_ref, wup_v, wdn_v, carry_ref, sems):
        _load_once(wup_hbm, wup_v, sems.at[0])
        _load_once(wdn_hbm, wdn_v, sems.at[1])

        @pl.when(pl.program_id(0) == 0)
        def _():
            carry_ref[...] = jnp.zeros_like(carry_ref)
            dg4_ref[...] = jnp.zeros_like(dg4_ref)
            dg3_ref[...] = jnp.zeros_like(dg3_ref)
            dcb_ref[...] = jnp.zeros_like(dcb_ref)
            dcw_ref[...] = jnp.zeros_like(dcw_ref)

        dn = dn_ref[...]
        y = y_ref[...]
        g4v = g4_ref[...]
        r4 = lax.rsqrt(jnp.mean(y * y, axis=-1, keepdims=True) + EPS)
        dg4_ref[...] = dg4_ref[...] + jnp.sum(dn * y * r4, axis=0, keepdims=True)
        dy = _rms_bwd(y, r4, g4v, dn).astype(BF16)
        dy_ref[...] = dy
        dh2 = jnp.zeros((TM, D_MODEL), F32)
        for j, off, wd in _ff_pieces():
            base = SHARD_FF * j + off
            da = _mm_nt(dy, wdn_v[base:base + wd, :])
            for half, coef_ref in enumerate((c1_ref, c2_ref)):
                cs = slice(D_FF * half + base, D_FF * half + base + wd)
                du = da * coef_ref[:, base:base + wd].astype(F32)
                p1, p2 = _rows_after(du, carry_ref[0:1, cs], carry_ref[1:2, cs])
                carry_ref[:, cs] = du[0:HALO, :]
                upf = up_ref[:, cs].astype(F32)
                dcb_ref[:, cs] = dcb_ref[:, cs] + jnp.sum(du, axis=0, keepdims=True)
                dcw_ref[0:1, cs] = dcw_ref[0:1, cs] + jnp.sum(p2 * upf, axis=0, keepdims=True)
                dcw_ref[1:2, cs] = dcw_ref[1:2, cs] + jnp.sum(p1 * upf, axis=0, keepdims=True)
                dcw_ref[2:3, cs] = dcw_ref[2:3, cs] + jnp.sum(du * upf, axis=0, keepdims=True)
                dup = (cw_ref[2:3, cs] * du + cw_ref[1:2, cs] * p1 + cw_ref[0:1, cs] * p2).astype(BF16)
                dup_ref[:, cs] = dup
                dh2 = dh2 + _mm_nt(dup, wup_v[2 * half + j, :, off:off + wd])
        x1 = x1_ref[...]
        r3 = lax.rsqrt(jnp.mean(x1 * x1, axis=-1, keepdims=True) + EPS)
        dg3_ref[...] = dg3_ref[...] + jnp.sum(dh2 * x1 * r3, axis=0, keepdims=True)
        dx1_ref[...] = dn + _rms_bwd(x1, r3, g3_ref[...], dh2)

    outs = [
        jax.ShapeDtypeStruct((T, D_MODEL), BF16),
        jax.ShapeDtypeStruct((T, 2 * D_FF), BF16),
        jax.ShapeDtypeStruct((T, D_MODEL), F32),
        jax.ShapeDtypeStruct((1, D_MODEL), F32),
        jax.ShapeDtypeStruct((1, D_MODEL), F32),
        jax.ShapeDtypeStruct((1, 2 * D_FF), F32),
        jax.ShapeDtypeStruct((3, 2 * D_FF), F32),
    ]
    return pl.pallas_call(
        body, name="ffn_bwd", grid=(nt,), out_shape=outs,
        in_specs=[pl.BlockSpec((TM, D_MODEL), rev), pl.BlockSpec((TM, D_MODEL), rev), _const_spec((1, D_MODEL)),
                  pl.BlockSpec((TM, 2 * D_FF), rev), pl.BlockSpec((TM, D_FF), rev), pl.BlockSpec((TM, D_FF), rev),
                  _const_spec((3, 2 * D_FF)), _any_spec(), _any_spec(), pl.BlockSpec((TM, D_MODEL), rev),
                  _const_spec((1, D_MODEL))],
        out_specs=[pl.BlockSpec((TM, D_MODEL), rev), pl.BlockSpec((TM, 2 * D_FF), rev), pl.BlockSpec((TM, D_MODEL), rev),
                   _const_spec((1, D_MODEL)), _const_spec((1, D_MODEL)), _const_spec((1, 2 * D_FF)),
                   _const_spec((3, 2 * D_FF))],
        scratch_shapes=[pltpu.VMEM((N_SHARD, D_MODEL, SHARD_FF), BF16), pltpu.VMEM((D_FF, D_MODEL), BF16),
                        pltpu.VMEM((HALO, 2 * D_FF), F32), pltpu.SemaphoreType.DMA((2,))],
        compiler_params=_params(),
    )(dx2, y, g4, up, c1, c2, cw, wdown, wup, x1, g3)


def _mix_out_bwd(dx1, cat, g2, wout, og, gg, gnorm):
    T = dx1.shape[0]
    TM = _wide_tile(T)

    def body(dx1_ref, cat_ref, g2_ref, wout_ref, og_ref, gg_ref, gn_ref,
             dmix_ref, dog_ref, dgg_ref, dosw_ref, dg2_ref, dgn_ref):
        @pl.when(pl.program_id(0) == 0)
        def _():
            dg2_ref[...] = jnp.zeros_like(dg2_ref)
            dgn_ref[...] = jnp.zeros_like(dgn_ref)

        dx1 = dx1_ref[...]
        mix = _mm(cat_ref[...], wout_ref[...])
        r2 = lax.rsqrt(jnp.mean(mix * mix, axis=-1, keepdims=True) + EPS)
        dg2_ref[...] = dg2_ref[...] + jnp.sum(dx1 * mix * r2, axis=0, keepdims=True)
        dmix = _rms_bwd(mix, r2, g2_ref[...], dx1).astype(BF16)
        dmix_ref[...] = dmix
        dcat = _mm_nt(dmix, wout_ref[...])
        dosw_ref[...] = dcat[:, GLA_V:].astype(BF16)
        gn = gn_ref[...]
        dgn = jnp.zeros((1, LANES), F32)
        for j in range(GLA_HEADS):
            hs = slice(128 * j, 128 * j + 128)
            o = og_ref[:, hs].astype(F32)
            r = lax.rsqrt(jnp.mean(o * o, axis=-1, keepdims=True) + EPS)
            gate = gg_ref[:, hs].astype(F32)
            sg = _sigmoid(gate)
            dgated = dcat[:, hs]
            dnorm = dgated * (gate * sg)
            dgg_ref[:, hs] = (dgated * (o * r * gn) * (sg * (1.0 + gate * (1.0 - sg)))).astype(BF16)
            dgn = dgn + jnp.sum(dnorm * o * r, axis=0, keepdims=True)
            dog_ref[:, hs] = _rms_bwd(o, r, gn, dnorm)
        dgn_ref[...] = dgn_ref[...] + dgn

    return pl.pallas_call(
        body, name="mix_out_bwd", grid=(T // TM,),
        out_shape=[jax.ShapeDtypeStruct((T, D_MODEL), BF16), jax.ShapeDtypeStruct((T, GLA_V), F32),
                   jax.ShapeDtypeStruct((T, GLA_V), BF16), jax.ShapeDtypeStruct((T, SWA_Q), BF16),
                   jax.ShapeDtypeStruct((1, D_MODEL), F32), jax.ShapeDtypeStruct((1, LANES), F32)],
        in_specs=[_row_spec(TM, D_MODEL), _row_spec(TM, D_MODEL), _const_spec((1, D_MODEL)),
                  _const_spec((D_MODEL, D_MODEL)), _row_spec(TM, GLA_V), _row_spec(TM, GLA_V), _const_spec((1, LANES))],
        out_specs=[_row_spec(TM, D_MODEL), _row_spec(TM, GLA_V), _row_spec(TM, GLA_V), _row_spec(TM, SWA_Q),
                   _const_spec((1, D_MODEL)), _const_spec((1, LANES))],
        compiler_params=_params(),
    )(dx1, cat, g2, wout, og, gg, gnorm)


def _proj_bwd(x, g1, wp, gup, glr, dq, dk, dv, dgg, dsq, dkd, dvd, dz, rc, rsa, rsb, dx1):
    T = x.shape[0]
    TM = _wide_tile(T)

    def body(x_ref, g1_ref, wp_hbm, gup_ref, glr_ref, dq_ref, dk_ref, dv_ref, dgg_ref, dsq_ref, dkd_ref, dvd_ref,
             dz_ref, rc_ref, rsa_ref, rsb_ref, dx1_ref, dx_ref, dp_ref, dg1_ref, dgup_ref, dgb_ref, wp_v, sem):
        _load_once(wp_hbm, wp_v, sem)

        @pl.when(pl.program_id(0) == 0)
        def _():
            dg1_ref[...] = jnp.zeros_like(dg1_ref)
            dgup_ref[...] = jnp.zeros_like(dgup_ref)
            dgb_ref[...] = jnp.zeros_like(dgb_ref)

        rc_, rsa_, rsb_ = rc_ref[...], rsa_ref[...], rsb_ref[...]
        dp_ref[:, 0:256] = dq_ref[...]
        dp_ref[:, 256:512] = dk_ref[...]
        dp_ref[:, 512:1024] = dv_ref[...]
        dp_ref[:, 1024:1536] = dgg_ref[...]
        for s in range(4):
            cs = slice(128 * s, 128 * s + 128)
            dp_ref[:, 1536 + 128 * s:1664 + 128 * s] = _rotate_bwd(dsq_ref[:, cs].astype(F32), rc_, rsa_, rsb_).astype(BF16)
        first = _iota((TM, LANES), 1) < 64
        dk0 = dkd_ref[:, 0:128].astype(F32)
        dk1 = dkd_ref[:, 128:256].astype(F32)
        dkr = jnp.where(first, dk0 + pltpu.roll(dk0, 64, 1), dk1 + pltpu.roll(dk1, 64, 1))
        dp_ref[:, 2048:2176] = _rotate_bwd(dkr, rc_, rsa_, rsb_).astype(BF16)
        dv0 = dvd_ref[:, 0:128].astype(F32)
        dv1 = dvd_ref[:, 128:256].astype(F32)
        dp_ref[:, 2176:2304] = jnp.where(first, dv0 + pltpu.roll(dv0, 64, 1), dv1 + pltpu.roll(dv1, 64, 1)).astype(BF16)
        dz = dz_ref[...]
        dzb = dz.astype(BF16)
        dp_ref[:, 2304:2432] = _mm_nt(dzb, gup_ref[...]).astype(BF16)
        dgup_ref[...] = dgup_ref[...] + _mm_tn(glr_ref[...], dzb)
        dgb_ref[...] = dgb_ref[...] + jnp.sum(dz, axis=0, keepdims=True)
        dh1 = _mm_nt(dp_ref[...], wp_v[...])
        xt = x_ref[...]
        r = lax.rsqrt(jnp.mean(xt * xt, axis=-1, keepdims=True) + EPS)
        dg1_ref[...] = dg1_ref[...] + jnp.sum(dh1 * xt * r, axis=0, keepdims=True)
        dx_ref[...] = dx1_ref[...] + _rms_bwd(xt, r, g1_ref[...], dh1)

    row = lambda cols: _row_spec(TM, cols)
    return pl.pallas_call(
        body, name="proj_bwd", grid=(T // TM,),
        out_shape=[jax.ShapeDtypeStruct((T, D_MODEL), F32), jax.ShapeDtypeStruct((T, IN_WIDTH_PAD), BF16),
                   jax.ShapeDtypeStruct((1, D_MODEL), F32), jax.ShapeDtypeStruct((LANES, GLA_QK), F32),
                   jax.ShapeDtypeStruct((1, GLA_QK), F32)],
        in_specs=[row(D_MODEL), _const_spec((1, D_MODEL)), _any_spec(), _const_spec((LANES, GLA_QK)), row(LANES),
                  row(GLA_QK), row(GLA_QK), row(GLA_V), row(GLA_V), row(SWA_Q), row(256), row(256), row(GLA_QK),
                  row(LANES), row(LANES), row(LANES), row(D_MODEL)],
        out_specs=[row(D_MODEL), row(IN_WIDTH_PAD), _const_spec((1, D_MODEL)), _const_spec((LANES, GLA_QK)),
                   _const_spec((1, GLA_QK))],
        scratch_shapes=[pltpu.VMEM((D_MODEL, IN_WIDTH_PAD), BF16), pltpu.SemaphoreType.DMA],
        compiler_params=_params(),
    )(x, g1, wp, gup, glr, dq, dk, dv, dgg, dsq, dkd, dvd, dz, rc, rsa, rsb, dx1)


def _matmul_tn(a, b, tn, name, column_blocks_major=False):
    T, M = a.shape
    N = b.shape[1]
    tk = next(t for t in (2048, 1024, 512, TM) if T % t == 0 and t * (M + tn) <= 2048 * (D_MODEL + SHARD_FF))
    nk = T // tk
    if column_blocks_major:
        out_shape = jax.ShapeDtypeStruct((N // tn, M, tn), F32)
        out_spec = pl.BlockSpec((None, M, tn), lambda j, kk: (j, 0, 0))
    else:
        out_shape = jax.ShapeDtypeStruct((M, N), F32)
        out_spec = pl.BlockSpec((M, tn), lambda j, kk: (0, j))

    def body(a_ref, b_ref, o_ref):
        kk = pl.program_id(1)

        @pl.when(kk == 0)
        def _():
            o_ref[...] = jnp.zeros_like(o_ref)

        o_ref[...] = o_ref[...] + _mm_tn(a_ref[...], b_ref[...])

    return pl.pallas_call(
        body, name=name, grid=(N // tn, nk), out_shape=out_shape,
        in_specs=[pl.BlockSpec((tk, M), lambda j, kk: (kk, 0)), pl.BlockSpec((tk, tn), lambda j, kk: (kk, j))],
        out_specs=out_spec,
        compiler_params=_params(),
    )(a, b)


def _adamw_update(w_ref, g_ref, m_ref, v_ref, d_ref, m2_ref, v2_ref):
    g_ = g_ref[...]
    m2 = ADAM_B1 * m_ref[...] + (1.0 - ADAM_B1) * g_
    v2 = ADAM_B2 * v_ref[...] + (1.0 - ADAM_B2) * (g_ * g_)
    m_hat = m2 / (1.0 - ADAM_B1 ** ADAM_STEP)
    v_hat = v2 / (1.0 - ADAM_B2 ** ADAM_STEP)
    d_ref[...] = -ADAM_LR * (m_hat / (jnp.sqrt(v_hat) + ADAM_EPS) + ADAM_WD * w_ref[...])
    m2_ref[...] = m2
    v2_ref[...] = v2


def _adamw(w, g, m, v, rows, name):
    R, C = w.shape

    def body(*refs):
        _adamw_update(*refs)

    spec = pl.BlockSpec((rows, C), lambda i: (i, 0))
    return pl.pallas_call(
        body, name=name, grid=(R // rows,), out_shape=[jax.ShapeDtypeStruct((R, C), F32)] * 3,
        in_specs=[spec] * 4, out_specs=[spec] * 3, compiler_params=_params(),
    )(w, g, m, v)


def _adamw_small(ws, gs, ms, vs):
    n = len(ws)

    def body(*refs):
        w_, g_, m_, v_, d_, m2_, v2_ = (refs[n * i:n * (i + 1)] for i in range(7))
        for k in range(n):
            _adamw_update(w_[k], g_[k], m_[k], v_[k], d_[k], m2_[k], v2_[k])

    vm = pl.BlockSpec(memory_space=pltpu.VMEM)
    outs = pl.pallas_call(
        body, name="adamw_small", out_shape=[jax.ShapeDtypeStruct(w.shape, F32) for w in ws] * 3,
        in_specs=[vm] * (4 * n), out_specs=[vm] * (3 * n),
    )(*ws, *gs, *ms, *vs)
    return outs[:n], outs[n:2 * n], outs[2 * n:]


def _place():
    x, y, c = lax.axis_index("x"), lax.axis_index("y"), lax.axis_index("c")
    chips = [(1 - x, y), (x, 1 - y), (1 - x, 1 - y)]
    return x, y, c, chips


class _staged_copies:
    def __init__(self, srcs, dsts, stage, sems):
        n = len(srcs)
        self.loads = [pltpu.make_async_copy(srcs[k], stage[k], sems.at[k]) for k in range(n)]
        self.stores = [pltpu.make_async_copy(stage[k], dsts[k], sems.at[n + k]) for k in range(n)]

    def load(self):
        for cp in self.loads:
            cp.start()

    def store(self):
        for ld, st in zip(self.loads, self.stores):
            ld.wait()
            st.start()

    def finish(self):
        for cp in self.stores:
            cp.wait()


def _allgather_shards(parts, unit_rows):
    n = len(parts)
    units = [(k, r, unit_rows[k]) for k in range(n) for r in range(0, parts[k].shape[0] // 2, unit_rows[k])]
    nu = len(units)

    def body(*refs):
        ins, outs, stage = refs[:n], refs[n:2 * n], refs[2 * n:3 * n]
        send_sems, recv_sems, local_sems = refs[3 * n:]
        x, y, c, chips = _place()
        sibling = (x, y, 1 - c)
        own = _staged_copies(ins, [o.at[2 * x + y] for o in outs], stage, local_sems)

        def block(i, px, py, half):
            k, r, u = units[i]
            return outs[k].at[2 * px + py, pl.ds(half * (parts[k].shape[0] // 2) + r, u), :]

        def copy(i, j, px, py, half, to, src=None):
            return pltpu.make_async_remote_copy(
                src_ref=block(i, px, py, half) if src is None else src, dst_ref=block(i, px, py, half),
                send_sem=send_sems.at[nu * j + i], recv_sem=recv_sems.at[nu * j + i], device_id=to, device_id_type=MESH)

        own.load()
        first, passed = [], []
        for i, (k, r, u) in enumerate(units):
            for j, chip in enumerate(chips):
                src = ins[k].at[pl.ds(c * (parts[k].shape[0] // 2) + r, u), :]
                first.append(copy(i, j, x, y, c, (*chip, c), src=src))
                first[-1].start()
        own.store()
        for i in range(nu):
            for j, chip in enumerate(chips):
                copy(i, j, *chip, c, (x, y, c)).wait_recv()
                passed.append(copy(i, 3 + j, *chip, c, sibling))
                passed[-1].start()
        for i in range(nu):
            for j, chip in enumerate(chips):
                copy(i, 3 + j, *chip, 1 - c, (x, y, c)).wait_recv()
        for cp in first + passed:
            cp.wait_send()
        own.finish()

    return pl.pallas_call(
        body, name="allgather_shards", out_shape=[jax.ShapeDtypeStruct((N_SHARD,) + p.shape, p.dtype) for p in parts],
        in_specs=[_any_spec()] * n, out_specs=[_any_spec()] * n,
        scratch_shapes=[pltpu.VMEM(p.shape, p.dtype) for p in parts] + [
            pltpu.SemaphoreType.DMA((6 * nu,)), pltpu.SemaphoreType.DMA((6 * nu,)), pltpu.SemaphoreType.DMA((2 * n,))],
        compiler_params=_params(),
    )(*parts)


def _d2d_pieces(rows, piece_rows):
    return [(r, piece_rows) for r in range(0, rows, piece_rows)]


def _rs_pair_swap(arrs, piece_rows, name):
    n = len(arrs)

    def body(*refs):
        ins, outs = refs[:n], refs[n:2 * n]
        send_sems, recv_sems = refs[2 * n:]
        x, y, c, _ = _place()
        sibling = (x, y, 1 - c)
        for k in range(n):
            H = arrs[k].shape[1] // 2
            for s in range(N_SHARD):
                for r, pr in _d2d_pieces(H, piece_rows[k]):
                    pltpu.make_async_remote_copy(
                        src_ref=ins[k].at[s, pl.ds((1 - c) * H + r, pr), :], dst_ref=outs[k].at[s, pl.ds(r, pr), :],
                        send_sem=send_sems.at[k], recv_sem=recv_sems.at[k], device_id=sibling, device_id_type=MESH).start()
        for k in range(n):
            H = arrs[k].shape[1] // 2
            whole = pltpu.make_async_remote_copy(
                src_ref=ins[k].at[:, pl.ds(0, H), :], dst_ref=outs[k], send_sem=send_sems.at[k], recv_sem=recv_sems.at[k],
                device_id=sibling, device_id_type=MESH)
            whole.wait_recv()
            whole.wait_send()

    return pl.pallas_call(
        body, name=name,
        out_shape=[jax.ShapeDtypeStruct((N_SHARD, a.shape[1] // 2, a.shape[2]), F32) for a in arrs],
        in_specs=[_any_spec()] * n, out_specs=[_any_spec()] * n,
        scratch_shapes=[pltpu.SemaphoreType.DMA((n,)), pltpu.SemaphoreType.DMA((n,))],
    )(*arrs)


def _rs_add_pair(a, got, core, rows, name):
    _, H, C = got.shape
    nb = H // rows

    def body(c_ref, a_ref, b_ref, o_ref):
        o_ref[...] = (a_ref[...] + b_ref[...]).astype(BF16)

    spec = pl.BlockSpec((1, rows, C), lambda s, r, c_ref: (s, r, 0))
    return pl.pallas_call(
        body, name=name, out_shape=jax.ShapeDtypeStruct(got.shape, BF16),
        grid_spec=pltpu.PrefetchScalarGridSpec(
            num_scalar_prefetch=1, grid=(N_SHARD, nb),
            in_specs=[pl.BlockSpec((1, rows, C), lambda s, r, c_ref: (s, c_ref[0] * nb + r, 0)), spec], out_specs=spec),
        compiler_params=_params(),
    )(core, a, got)


def _rs_chip_scatter(parts):
    n = len(parts)

    def body(*refs):
        ins, outs, stage = refs[:n], refs[n:2 * n], refs[2 * n:3 * n]
        send_sems, recv_sems, local_sems = refs[3 * n:]
        x, y, c, chips = _place()
        me = 2 * x + y
        own = _staged_copies([i.at[me] for i in ins], [o.at[me] for o in outs], stage, local_sems)
        own.load()
        sends = []
        for k in range(n):
            for j, (px, py) in enumerate(chips):
                sends.append(pltpu.make_async_remote_copy(
                    src_ref=ins[k].at[2 * px + py], dst_ref=outs[k].at[me], send_sem=send_sems.at[3 * k + j],
                    recv_sem=recv_sems.at[3 * k + j], device_id=(px, py, c), device_id_type=MESH))
                sends[-1].start()
        own.store()
        for k in range(n):
            for j, (px, py) in enumerate(chips):
                pltpu.make_async_remote_copy(
                    src_ref=ins[k].at[me], dst_ref=outs[k].at[2 * px + py], send_sem=send_sems.at[3 * k + j],
                    recv_sem=recv_sems.at[3 * k + j], device_id=(px, py, c), device_id_type=MESH).wait_recv()
        for cp in sends:
            cp.wait_send()
        own.finish()

    return pl.pallas_call(
        body, name="rs_chip_scatter", out_shape=[jax.ShapeDtypeStruct(p.shape, p.dtype) for p in parts],
        in_specs=[_any_spec()] * n, out_specs=[_any_spec()] * n,
        scratch_shapes=[pltpu.VMEM(p.shape[1:], p.dtype) for p in parts] + [
            pltpu.SemaphoreType.DMA((3 * n,)), pltpu.SemaphoreType.DMA((3 * n,)), pltpu.SemaphoreType.DMA((2 * n,))],
        compiler_params=_params(),
    )(*parts)


def _rs_sum_chips(parts, rows, name):
    _, H, C = parts.shape

    def body(p_ref, o_ref):
        o_ref[...] = ((p_ref[0].astype(F32) + p_ref[1].astype(F32)) + p_ref[2].astype(F32)) + p_ref[3].astype(F32)

    return pl.pallas_call(
        body, name=name, grid=(H // rows,), out_shape=jax.ShapeDtypeStruct((H, C), F32),
        in_specs=[pl.BlockSpec((N_SHARD, rows, C), lambda r: (0, r, 0))],
        out_specs=pl.BlockSpec((rows, C), lambda r: (r, 0)), compiler_params=_params(),
    )(parts)


def _rs_pair_share(halves, piece_rows):
    n = len(halves)

    def body(*refs):
        ins, outs, stage = refs[:n], refs[n:2 * n], refs[2 * n:3 * n]
        send_sems, recv_sems, local_sems = refs[3 * n:]
        x, y, c, _ = _place()
        sibling = (x, y, 1 - c)
        own = _staged_copies(ins, [outs[k].at[pl.ds(c * halves[k].shape[0], halves[k].shape[0]), :] for k in range(n)],
                             stage, local_sems)
        own.load()
        for k in range(n):
            H = halves[k].shape[0]
            for r, pr in _d2d_pieces(H, piece_rows[k]):
                pltpu.make_async_remote_copy(
                    src_ref=ins[k].at[pl.ds(r, pr), :], dst_ref=outs[k].at[pl.ds(c * H + r, pr), :],
                    send_sem=send_sems.at[k], recv_sem=recv_sems.at[k], device_id=sibling, device_id_type=MESH).start()
        own.store()
        for k in range(n):
            H = halves[k].shape[0]
            whole = pltpu.make_async_remote_copy(
                src_ref=ins[k], dst_ref=outs[k].at[pl.ds((1 - c) * H, H), :], send_sem=send_sems.at[k],
                recv_sem=recv_sems.at[k], device_id=sibling, device_id_type=MESH)
            whole.wait_recv()
            whole.wait_send()
        own.finish()

    return pl.pallas_call(
        body, name="rs_pair_share", out_shape=[jax.ShapeDtypeStruct((2 * h.shape[0], h.shape[1]), F32) for h in halves],
        in_specs=[_any_spec()] * n, out_specs=[_any_spec()] * n,
        scratch_shapes=[pltpu.VMEM(h.shape, F32) for h in halves] + [
            pltpu.SemaphoreType.DMA((n,)), pltpu.SemaphoreType.DMA((n,)), pltpu.SemaphoreType.DMA((2 * n,))],
        compiler_params=_params(),
    )(*halves)


_HBM = pl.BlockSpec(memory_space=pltpu.HBM)
_SEM = pl.BlockSpec(memory_space=pltpu.SEMAPHORE)
_EFFECT = pltpu.SideEffectType.DATAFLOW_SIDE_EFFECTING


def _gather_plan(srcs, lands, x, y, c, chips):
    plan = []
    for k in range(len(srcs)):
        H = srcs[k].shape[0] // 2
        for px, py in chips:
            plan.append((srcs[k].at[pl.ds(c * H, H), :], lands[k].at[2 * x + y, pl.ds(c * H, H), :], (px, py, c),
                         lands[k].at[2 * px + py, pl.ds(c * H, H), :]))
    return plan


def _scatter_plan(srcs, lands, x, y, c, chips):
    plan = []
    for k in range(len(srcs)):
        for px, py in chips:
            plan.append((srcs[k].at[2 * px + py], lands[k].at[2 * x + y], (px, py, c), lands[k].at[2 * px + py]))
    return plan


def _ici_start(srcs, lands, make_plan, name):
    n = len(srcs)
    ncopy = 3 * n

    def body(*refs):
        ins, lnd = refs[:n], refs[n:2 * n]
        send_sems, recv_sems = refs[2 * n], refs[2 * n + 1]
        token = refs[-1]
        x, y, c, chips = _place()
        for i, (src, dst, peer, _) in enumerate(make_plan(ins, lnd, x, y, c, chips)):
            pltpu.make_async_remote_copy(src_ref=src, dst_ref=dst, send_sem=send_sems.at[i], recv_sem=recv_sems.at[i],
                                         device_id=peer, device_id_type=MESH).start()
        token[...] = jnp.zeros_like(token)

    arrays = list(srcs) + list(lands)
    return pl.pallas_call(
        body, name=name,
        out_shape=(pltpu.SemaphoreType.DMA((ncopy,)), pltpu.SemaphoreType.DMA((ncopy,)),
                   *[pltpu.HBM(a.shape, a.dtype) for a in arrays], jax.ShapeDtypeStruct((8, LANES), F32)),
        in_specs=[_HBM] * (2 * n), out_specs=(_SEM, _SEM, *[_HBM] * (2 * n), pl.BlockSpec(memory_space=pltpu.VMEM)),
        input_output_aliases={i: 2 + i for i in range(2 * n)},
        compiler_params=pltpu.CompilerParams(has_side_effects=_EFFECT),
    )(*[pltpu.with_memory_space_constraint(a, pltpu.HBM) for a in arrays])


def _ici_wait(started, after, make_plan, name):
    send_sems, recv_sems = started[0], started[1]
    arrays = list(started[2:-1])
    n = len(arrays) // 2

    def body(*refs):
        ins, lnd = refs[:n], refs[n:2 * n]
        send_sems, recv_sems = refs[2 * n], refs[2 * n + 1]
        x, y, c, chips = _place()
        for i, (src, _, peer, landed) in enumerate(make_plan(ins, lnd, x, y, c, chips)):
            cp = pltpu.make_async_remote_copy(src_ref=src, dst_ref=landed, send_sem=send_sems.at[i],
                                              recv_sem=recv_sems.at[i], device_id=peer, device_id_type=MESH)
            cp.wait_send()
            cp.wait_recv()

    outs = pl.pallas_call(
        body, name=name, out_shape=tuple(pltpu.HBM(a.shape, a.dtype) for a in arrays),
        in_specs=[_HBM] * (2 * n) + [_SEM, _SEM, pl.BlockSpec(memory_space=pl.ANY)], out_specs=tuple([_HBM] * (2 * n)),
        input_output_aliases={i: i for i in range(2 * n)},
        compiler_params=pltpu.CompilerParams(has_side_effects=_EFFECT),
    )(*arrays, send_sems, recv_sems, after)
    return list(outs[n:])


def _gather_finish(parts, lands):
    n = len(parts)

    def body(*refs):
        ins, lnd, outs, stage = refs[:n], refs[n:2 * n], refs[2 * n:3 * n], refs[3 * n:4 * n]
        send_sems, recv_sems, local_sems = refs[4 * n:]
        x, y, c, chips = _place()
        sibling = (x, y, 1 - c)
        own = _staged_copies(ins, [o.at[2 * x + y] for o in outs], stage, local_sems)
        own.load()
        sends = []
        for k in range(n):
            H = parts[k].shape[0] // 2
            for j, (px, py) in enumerate(chips):
                half = outs[k].at[2 * px + py, pl.ds(c * H, H), :]
                sends.append(pltpu.make_async_remote_copy(src_ref=half, dst_ref=half, send_sem=send_sems.at[3 * k + j],
                                                          recv_sem=recv_sems.at[3 * k + j], device_id=sibling, device_id_type=MESH))
                sends[-1].start()
        own.store()
        for k in range(n):
            H = parts[k].shape[0] // 2
            for j, (px, py) in enumerate(chips):
                other = outs[k].at[2 * px + py, pl.ds((1 - c) * H, H), :]
                pltpu.make_async_remote_copy(src_ref=other, dst_ref=other, send_sem=send_sems.at[3 * k + j],
                                             recv_sem=recv_sems.at[3 * k + j], device_id=sibling, device_id_type=MESH).wait_recv()
        for cp in sends:
            cp.wait_send()
        own.finish()

    return pl.pallas_call(
        body, name="gather_finish", out_shape=[jax.ShapeDtypeStruct(l.shape, l.dtype) for l in lands],
        in_specs=[_any_spec()] * (2 * n), out_specs=[_any_spec()] * n,
        input_output_aliases={n + k: k for k in range(n)},
        scratch_shapes=[pltpu.VMEM(p.shape, p.dtype) for p in parts] + [
            pltpu.SemaphoreType.DMA((3 * n,)), pltpu.SemaphoreType.DMA((3 * n,)), pltpu.SemaphoreType.DMA((2 * n,))],
        compiler_params=_params(),
    )(*parts, *lands)


def _place_own(part, chip, rows, name):
    _, H, C = part.shape

    def body(chip_ref, p_ref, o_ref):
        o_ref[...] = p_ref[...]

    spec = pl.BlockSpec((1, rows, C), lambda r, chip_ref: (chip_ref[0], r, 0))
    return pl.pallas_call(
        body, name=name, out_shape=jax.ShapeDtypeStruct(part.shape, part.dtype),
        grid_spec=pltpu.PrefetchScalarGridSpec(num_scalar_prefetch=1, grid=(H // rows,), in_specs=[spec], out_specs=spec),
        compiler_params=_params(),
    )(chip, part)


SMALL_COLS = 1024


def _small_rows(shapes):
    starts, row = [], 0
    for r, cdim in shapes:
        starts.append(row)
        row += r * (-(-cdim // SMALL_COLS))
    return starts, -(-row // 8) * 8


def _allreduce_small(arrays, sink_rows, loss):
    n = len(arrays)
    shapes = [a.shape for a in arrays] + [(1, SWA_HEADS), (1, 1)]
    starts, total_rows = _small_rows(shapes)

    def pieces(k):
        r, cdim = shapes[k]
        per = -(-cdim // SMALL_COLS)
        return [(i, SMALL_COLS * j, min(SMALL_COLS, cdim - SMALL_COLS * j), starts[k] + per * i + j)
                for i in range(r) for j in range(per)]

    def body(*refs):
        ins, sink_ref, loss_ref = refs[:n], refs[n], refs[n + 1]
        outs = refs[n + 2:2 * n + 4]
        mine, all_ref, tot_ref, send_sems, recv_sems = refs[2 * n + 4:]
        x, y, c, _ = _place()
        me = 4 * x + 2 * y + c
        mine[...] = jnp.zeros_like(mine)
        for k in range(n):
            for i, col, wd, row in pieces(k):
                mine[row:row + 1, 0:wd] = ins[k][i:i + 1, col:col + wd]
        lane = _iota((1, SMALL_COLS), 1)
        sinks = jnp.zeros((1, SMALL_COLS), F32)
        for h in range(SWA_HEADS):
            head = jnp.sum(sink_ref[SWA_BLOCK * h:SWA_BLOCK * (h + 1), :]) * (1.0 / LANES)
            sinks = jnp.where(lane == h, head, sinks)
        mine[starts[n]:starts[n] + 1, :] = sinks
        mine[starts[n + 1]:starts[n + 1] + 1, 0:LANES] = loss_ref[0:1, :]
        all_ref[me] = mine[...]
        sends = []
        for k in range(1, 8):
            kx, ky, kc = (k >> 2) & 1, (k >> 1) & 1, k & 1
            peer = (x ^ kx, y ^ ky, c ^ kc)
            cp = pltpu.make_async_remote_copy(
                src_ref=mine, dst_ref=all_ref.at[me], send_sem=send_sems.at[k - 1], recv_sem=recv_sems.at[k - 1],
                device_id=peer, device_id_type=MESH)
            cp.start()
            sends.append(cp)
        for k in range(1, 8):
            kx, ky, kc = (k >> 2) & 1, (k >> 1) & 1, k & 1
            src = 4 * (x ^ kx) + 2 * (y ^ ky) + (c ^ kc)
            pltpu.make_async_remote_copy(
                src_ref=mine, dst_ref=all_ref.at[src], send_sem=send_sems.at[k - 1], recv_sem=recv_sems.at[k - 1],
                device_id=(x, y, c), device_id_type=MESH).wait_recv()
        for cp in sends:
            cp.wait_send()
        total = all_ref[0]
        for d in range(1, 8):
            total = total + all_ref[d]
        tot_ref[...] = total
        for k in range(n + 2):
            for i, col, wd, row in pieces(k):
                outs[k][i:i + 1, col:col + wd] = tot_ref[row:row + 1, 0:wd]

    vm = pl.BlockSpec(memory_space=pltpu.VMEM)
    buf = pltpu.VMEM((total_rows, SMALL_COLS), F32)
    return pl.pallas_call(
        body, name="allreduce_small", out_shape=[jax.ShapeDtypeStruct(s, F32) for s in shapes],
        in_specs=[vm] * (n + 2), out_specs=[vm] * (n + 2),
        scratch_shapes=[buf, pltpu.VMEM((8, total_rows, SMALL_COLS), F32), buf,
                        pltpu.SemaphoreType.DMA((7,)), pltpu.SemaphoreType.DMA((7,))],
    )(*arrays, sink_rows, loss)


BIG_NAMES = ("w_in", "w_out", "w_up", "w_down")
MATRIX_NAMES = BIG_NAMES + ("gla_gate_up", "conv_w")
LATE_NAMES = ("w_out", "w_up", "w_down")
RS_LABELS = BIG_NAMES
GATE_SHARD = (16, GLA_QK // N_SHARD)
CONVW_SHARD = (3, SHARD_FF)
SMALL_W_ROWS = 96
PIECE_ROWS = (128, 128, 64, 88)
ADD_ROWS = (256, 128, 256, 176)
FIRST_UNIT_ROWS = (256, SMALL_W_ROWS // 2)


def _pad_rows(flat, rows):
    return jnp.pad(flat, (0, rows * LANES - flat.shape[0])).reshape(rows, LANES)


def _pack_small_weights(gate_up, conv_w):
    bits = lax.bitcast_convert_type(conv_w, BF16)
    return _pad_rows(jnp.concatenate([gate_up.astype(BF16).reshape(-1), bits.reshape(-1)]), SMALL_W_ROWS)


def _unpack_small_weights(packed):
    flat = packed.reshape(N_SHARD, -1)
    n_gate = GATE_SHARD[0] * GATE_SHARD[1]
    n_conv = 2 * CONVW_SHARD[0] * CONVW_SHARD[1]
    gate = flat[:, :n_gate].reshape((N_SHARD,) + GATE_SHARD)
    conv = lax.bitcast_convert_type(flat[:, n_gate:n_gate + n_conv].reshape((N_SHARD,) + CONVW_SHARD + (2,)), F32)
    return (jnp.transpose(gate, (1, 0, 2)).reshape(16, GLA_QK), jnp.transpose(conv, (1, 0, 2)).reshape(3, 2 * D_FF))


def _permute_w_in(w):
    pad = jnp.zeros((w.shape[0], IN_WIDTH_PAD - IN_WIDTH), w.dtype)
    return jnp.concatenate([w[:, 0:1024], w[:, 1040:2320], w[:, 1024:1040], pad], axis=1)


def _unpermute_w_in(wp):
    return jnp.concatenate([wp[:, 0:1024], wp[:, 2304:2320], wp[:, 1024:2304]], axis=1)


def _rope_tables(positions):
    half = ROPE_DIM // 2
    inv_freq = ROPE_THETA ** (-jnp.arange(half, dtype=F32) * (2.0 / ROPE_DIM))
    d = jnp.arange(LANES) % SWA_HD
    freq = jnp.where(d < ROPE_DIM, inv_freq[d % half], 0.0)
    ang = positions.astype(F32)[:, None] * freq
    cos, sin = jnp.cos(ang), jnp.sin(ang)
    return cos, jnp.where(d < half, -sin, 0.0), jnp.where((d >= half) & (d < ROPE_DIM), sin, 0.0)


SMALL_NAMES = (("pre_mix_norm", 1024), ("gla_gate_bias", 256), ("gla_out_norm", 128), ("swa_sinks", 8),
               ("post_mix_norm", 1024), ("pre_ffn_norm", 1024), ("conv_b", 5632), ("post_ffn_norm", 1024))


def _local_step(x, positions, target, w, small, late_weights, early_grads):
    rc, rsa, rsb = _rope_tables(positions)
    wp = w["wp"]
    gup = jnp.pad(w["gla_gate_up"], ((0, LANES - 16), (0, 0)))
    g1, g2, g3, g4 = (small[n] for n in ("pre_mix_norm", "post_mix_norm", "pre_ffn_norm", "post_ffn_norm"))
    gbias, gnorm, cb = small["gla_gate_bias"], small["gla_out_norm"], small["conv_b"]
    sinks = small["swa_sinks"].reshape(-1)
    cw = w["conv_w"]

    h1, q, k, v, la, gg, sq, kd, vd, glr = _proj_fwd(x, g1, wp, gup, gbias, rc, rsa, rsb)
    og, s_all = _gla_fwd(q, k, v, la)
    osw = _swa_fwd(sq, kd, vd, sinks)
    w_out, w_up4, w_down = late_weights(osw)
    x1, cat = _mix_out_fwd(x, og, gg, osw, gnorm, w_out, g2)
    h2, up, act, c1, c2, y, dx2, loss = _ffn_fwd(x1, g3, w_up4, cw, cb, w_down, g4, target)

    dy, dup, dx1, dg4, dg3, dcb, dcw = _ffn_bwd(dx2, y, g4, up, c1, c2, cw, w_down, w_up4, x1, g3)
    zero = early_grads(_matmul_tn(h2, dup, SHARD_FF, "grad_w_up", column_blocks_major=True),
                       _matmul_tn(act, dy, D_MODEL, "grad_w_down"))
    dmix, dog, dgg, dosw, dg2, dgn = _mix_out_bwd(dx1, cat, g2 + zero, w_out, og, gg, gnorm)
    dsq, dkd, dvd, dsink = _swa_bwd(sq, kd, vd, sinks, dosw)
    dq, dk, dv, dz = _gla_bwd(q, k, v, la, s_all, dog)
    dx, dproj, dg1, dgup, dgb = _proj_bwd(x, g1, wp, gup, glr, dq, dk, dv, dgg, dsq, dkd, dvd, dz, rc, rsa, rsb, dx1)

    grads = {
        "wp": _matmul_tn(h1, dproj, IN_WIDTH_PAD, "grad_w_in"),
        "w_out": _matmul_tn(cat, dmix, D_MODEL, "grad_w_out"),
        "gla_gate_up": dgup[0:16],
        "conv_w": dcw,
    }
    small_grads = {
        "pre_mix_norm": dg1, "gla_gate_bias": dgb, "gla_out_norm": dgn,
        "post_mix_norm": dg2, "pre_ffn_norm": dg3, "conv_b": dcb, "post_ffn_norm": dg4,
    }
    return loss, dx, grads, small_grads, dsink


ADAM_ROWS = {"w_in": 256, "w_out": 256, "w_up": 256, "w_down": 176}
WEIGHT_ORDER = ("pre_mix_norm", "w_in", "gla_gate_up", "gla_gate_bias", "gla_out_norm", "swa_sinks", "w_out",
                "post_mix_norm", "pre_ffn_norm", "w_up", "conv_w", "conv_b", "w_down", "post_ffn_norm")
def kernel(x, positions, pre_mix_norm, w_in, gla_gate_up, gla_gate_bias, gla_out_norm, swa_sinks, w_out, post_mix_norm, pre_ffn_norm, w_up, conv_w, conv_b, w_down, post_ffn_norm, loss_target, m_pre_mix_norm, m_w_in, m_gla_gate_up, m_gla_gate_bias, m_gla_out_norm, m_swa_sinks, m_w_out, m_post_mix_norm, m_pre_ffn_norm, m_w_up, m_conv_w, m_conv_b, m_w_down, m_post_ffn_norm, v_pre_mix_norm, v_w_in, v_gla_gate_up, v_gla_gate_bias, v_gla_out_norm, v_swa_sinks, v_w_out, v_post_mix_norm, v_pre_ffn_norm, v_w_up, v_conv_w, v_conv_b, v_w_down, v_post_ffn_norm):
    weights = dict(pre_mix_norm=pre_mix_norm, w_in=w_in, gla_gate_up=gla_gate_up, gla_gate_bias=gla_gate_bias,
                   gla_out_norm=gla_out_norm, swa_sinks=swa_sinks, w_out=w_out, post_mix_norm=post_mix_norm,
                   pre_ffn_norm=pre_ffn_norm, w_up=w_up, conv_w=conv_w, conv_b=conv_b, w_down=w_down,
                   post_ffn_norm=post_ffn_norm)
    mom = dict(pre_mix_norm=m_pre_mix_norm, w_in=m_w_in, gla_gate_up=m_gla_gate_up, gla_gate_bias=m_gla_gate_bias,
               gla_out_norm=m_gla_out_norm, swa_sinks=m_swa_sinks, w_out=m_w_out, post_mix_norm=m_post_mix_norm,
               pre_ffn_norm=m_pre_ffn_norm, w_up=m_w_up, conv_w=m_conv_w, conv_b=m_conv_b, w_down=m_w_down,
               post_ffn_norm=m_post_ffn_norm)
    var = dict(pre_mix_norm=v_pre_mix_norm, w_in=v_w_in, gla_gate_up=v_gla_gate_up, gla_gate_bias=v_gla_gate_bias,
               gla_out_norm=v_gla_out_norm, swa_sinks=v_swa_sinks, w_out=v_w_out, post_mix_norm=v_post_mix_norm,
               pre_ffn_norm=v_pre_ffn_norm, w_up=v_w_up, conv_w=v_conv_w, conv_b=v_conv_b, w_down=v_w_down,
               post_ffn_norm=v_post_ffn_norm)
    weights, mom, var = ({n: a[0] if a.ndim == 3 else a for n, a in d.items()} for d in (weights, mom, var))

    core = lax.axis_index("c").astype(jnp.int32).reshape(1)
    chip = (2 * lax.axis_index("x") + lax.axis_index("y")).astype(jnp.int32).reshape(1)
    small = {n: weights[n] for n, _ in SMALL_NAMES}

    win4, small4 = _allgather_shards(
        [weights["w_in"].astype(BF16), _pack_small_weights(weights["gla_gate_up"], weights["conv_w"])], FIRST_UNIT_ROWS)
    gate_full, convw_full = _unpack_small_weights(small4)
    first = {"wp": _permute_w_in(jnp.transpose(win4, (1, 0, 2)).reshape(D_MODEL, IN_WIDTH)),
             "gla_gate_up": gate_full, "conv_w": convw_full}
    late_shards = [weights[n].astype(BF16) for n in LATE_NAMES]
    gathering = _ici_start(late_shards, [lax.empty((N_SHARD,) + s.shape, BF16) for s in late_shards], _gather_plan, "gather_start")
    small["pre_mix_norm"] = small["pre_mix_norm"] + gathering[-1][0, 0]

    def late_weights(after):
        wout4, wup4, wdown4 = _gather_finish(late_shards, _ici_wait(gathering, after, _gather_plan, "gather_wait"))
        return wout4.reshape(D_MODEL, D_MODEL), wup4, wdown4.reshape(D_FF, D_MODEL)

    def pair_partials(arrs, which):
        got = _rs_pair_swap(arrs, [PIECE_ROWS[i] for i in which], "rs_pair_swap_" + RS_LABELS[which[0]])
        return [_rs_add_pair(a, g, core, ADD_ROWS[i], "rs_add_pair_" + RS_LABELS[i]) for a, g, i in zip(arrs, got, which)]

    early = {}

    def early_grads(g_up4, g_down):
        parts = pair_partials([g_up4, g_down.reshape(N_SHARD, D_FF // N_SHARD, D_MODEL)], (2, 3))
        lands = [_place_own(p, chip, ADD_ROWS[i], "rs_place_own_" + RS_LABELS[i]) for p, i in zip(parts, (2, 3))]
        early["scatter"] = _ici_start(parts, lands, _scatter_plan, "rs_scatter_start")
        return early["scatter"][-1][0, 0]

    loss, dx, grads, small_grads, sink_rows = _local_step(
        x[0], positions[0], loss_target[0], first, small, late_weights, early_grads)

    landed_ffn = _ici_wait(early["scatter"], dx, _scatter_plan, "rs_scatter_wait")
    rest = [jnp.transpose(_unpermute_w_in(grads["wp"]).reshape(D_MODEL, N_SHARD, IN_WIDTH // N_SHARD), (1, 0, 2)),
            grads["w_out"].reshape(N_SHARD, D_MODEL // N_SHARD, D_MODEL)]
    landed = list(_rs_chip_scatter(pair_partials(rest, (0, 1)))) + list(landed_ffn)
    halves = [_rs_sum_chips(p, rows, "rs_sum_chips_" + n) for p, rows, n in zip(landed, ADD_ROWS, BIG_NAMES)]
    reduced = _rs_pair_share(halves, PIECE_ROWS)
    vectors = [n for n, _ in SMALL_NAMES if n != "swa_sinks"]
    summed = _allreduce_small([small_grads[n] for n in vectors] + [grads["gla_gate_up"], grads["conv_w"]], sink_rows, loss)
    g_all = {**dict(zip(vectors, summed)), **dict(zip(BIG_NAMES, reduced)), "swa_sinks": summed[-2],
             "gla_gate_up": lax.dynamic_slice_in_dim(summed[-4], chip[0] * GATE_SHARD[1], GATE_SHARD[1], axis=1),
             "conv_w": lax.dynamic_slice_in_dim(summed[-3], chip[0] * SHARD_FF, SHARD_FF, axis=1)}
    loss_sum = summed[-1][0, 0]

    delta, new_m, new_v = {}, {}, {}
    for n in BIG_NAMES:
        delta[n], new_m[n], new_v[n] = _adamw(weights[n], g_all[n], mom[n], var[n], ADAM_ROWS[n], "adamw_" + n)
    tiny = [n for n, _ in SMALL_NAMES] + ["gla_gate_up", "conv_w"]
    for res, vals in zip((delta, new_m, new_v), _adamw_small(*([d[n] for n in tiny] for d in (weights, g_all, mom, var)))):
        res.update(zip(tiny, vals))

    def lead(n, a):
        return a[None] if n in MATRIX_NAMES else a

    outs = [loss_sum, dx[None]]
    for d in (g_all, delta, new_m, new_v):
        outs.extend(lead(n, d[n]) for n in WEIGHT_ORDER)
    return tuple(outs)
```

```python
import functools

import jax
import jax.numpy as jnp
from jax import lax
from jax.experimental import pallas as pl
from jax.experimental.pallas import tpu as pltpu

F32 = jnp.float32
BF16 = jnp.bfloat16
MESH = pl.DeviceIdType.MESH

D_MODEL = 1024
GLA_HEADS = 4
GLA_DK = 64
GLA_DV = 128
GLA_TAU = 16.0
GLA_CHUNK = 64
SWA_HEADS = 8
SWA_HD = 64
SWA_BLOCK = 128
ROPE_THETA = 500000.0
ROPE_DIM = 16
D_FF = 2816
EPS = 1e-6
GLA_QK = 256
GLA_V = 512
SWA_Q = 512
SWA_KV = 128
IN_WIDTH = 2320
IN_WIDTH_PAD = 2432
N_SHARD = 4

ADAM_LR = 0.001
ADAM_B1 = 0.9
ADAM_B2 = 0.999
ADAM_EPS = 1e-08
ADAM_WD = 0.01
ADAM_STEP = 10

LANES = 128
VMEM_LIMIT = 56 * 1024 * 1024
TM = 256
SHARD_FF = 2 * D_FF // N_SHARD
FF_PIECES = ((0, 1408),)
GLA_BLOCK = 256


def _wide_tile(T):
    return 2 * TM if T % (2 * TM) == 0 else TM


def _params(**kw):
    return pltpu.CompilerParams(vmem_limit_bytes=VMEM_LIMIT, **kw)


def _mm(a, b):
    return lax.dot_general(a.astype(BF16), b.astype(BF16), (((1,), (0,)), ((), ())), preferred_element_type=F32)


def _mm_nt(a, b):
    return lax.dot_general(a.astype(BF16), b.astype(BF16), (((1,), (1,)), ((), ())), preferred_element_type=F32)


def _mm_tn(a, b):
    return lax.dot_general(a.astype(BF16), b.astype(BF16), (((0,), (0,)), ((), ())), preferred_element_type=F32)


def _mm_f32(a, b):
    return lax.dot_general(a, b, (((1,), (0,)), ((), ())), preferred_element_type=F32, precision=lax.Precision.HIGHEST)


def _iota(shape, dim):
    return lax.broadcasted_iota(jnp.int32, shape, dim)


def _sigmoid(x):
    return 1.0 / (1.0 + jnp.exp(-x))


def _gelu_parts(x):
    c = 0.7978845608028654
    x2 = x * x
    t = jnp.tanh(c * (x + 0.044715 * (x2 * x)))
    cdf = 0.5 * (1.0 + t)
    dcdf = 0.5 * (1.0 - t * t) * c * (1.0 + 3.0 * 0.044715 * x2)
    return x * cdf, cdf + x * dcdf


def _rms_bwd(v, r, g, dout):
    gd = g * dout
    return r * gd - v * (r * r * r) * jnp.mean(v * gd, axis=-1, keepdims=True)


def _row_spec(tm, cols):
    return pl.BlockSpec((tm, cols), lambda i: (i, 0))


def _const_spec(shape):
    return pl.BlockSpec(shape, lambda i: (0,) * len(shape))


def _any_spec():
    return pl.BlockSpec(memory_space=pl.ANY)


def _load_once(src_hbm, dst_vmem, sem):
    @pl.when(pl.program_id(0) == 0)
    def _():
        cp = pltpu.make_async_copy(src_hbm, dst_vmem, sem)
        cp.start()
        cp.wait()


def _rotate(v, rc, rsa, rsb):
    return v * rc + pltpu.roll(v, 120, 1) * rsa + pltpu.roll(v, 8, 1) * rsb


def _rotate_bwd(dv, rc, rsa, rsb):
    return dv * rc + pltpu.roll(dv * rsa, 8, 1) + pltpu.roll(dv * rsb, 120, 1)


def _proj_fwd(x, g1, wp, gup, gbias, rc, rsa, rsb):
    T = x.shape[0]
    TM = _wide_tile(T)

    def body(x_ref, g1_ref, wp_hbm, gup_ref, gb_ref, rc_ref, rsa_ref, rsb_ref,
             h1_ref, q_ref, k_ref, v_ref, la_ref, gg_ref, sq_ref, kd_ref, vd_ref, glr_ref, wp_v, sem):
        _load_once(wp_hbm, wp_v, sem)
        xt = x_ref[...]
        r = lax.rsqrt(jnp.mean(xt * xt, axis=-1, keepdims=True) + EPS)
        h = (xt * r * g1_ref[...]).astype(BF16)
        h1_ref[...] = h
        q_ref[...] = _mm(h, wp_v[:, 0:256])
        k_ref[...] = _mm(h, wp_v[:, 256:512])
        v_ref[...] = _mm(h, wp_v[:, 512:1024]).astype(BF16)
        gg_ref[...] = _mm(h, wp_v[:, 1024:1536]).astype(BF16)
        glr = _mm(h, wp_v[:, 2304:2432]).astype(BF16)
        glr_ref[...] = glr
        z = _mm(glr, gup_ref[...]) + gb_ref[...]
        la_ref[...] = (jnp.minimum(z, 0.0) - jnp.log1p(jnp.exp(-jnp.abs(z)))) * (1.0 / GLA_TAU)
        rc_, rsa_, rsb_ = rc_ref[...], rsa_ref[...], rsb_ref[...]
        for s in range(4):
            qs = _mm(h, wp_v[:, 1536 + 128 * s:1664 + 128 * s])
            sq_ref[:, 128 * s:128 * s + 128] = (_rotate(qs, rc_, rsa_, rsb_) * 0.125).astype(BF16)
        lane = _iota((TM, LANES), 1)
        first = lane < 64
        kr = _rotate(_mm(h, wp_v[:, 2048:2176]), rc_, rsa_, rsb_)
        krr = pltpu.roll(kr, 64, 1)
        kd_ref[:, 0:128] = jnp.where(first, kr, krr).astype(BF16)
        kd_ref[:, 128:256] = jnp.where(first, krr, kr).astype(BF16)
        vr = _mm(h, wp_v[:, 2176:2304])
        vrr = pltpu.roll(vr, 64, 1)
        vd_ref[:, 0:128] = jnp.where(first, vr, vrr).astype(BF16)
        vd_ref[:, 128:256] = jnp.where(first, vrr, vr).astype(BF16)

    outs = [
        jax.ShapeDtypeStruct((T, D_MODEL), BF16),
        jax.ShapeDtypeStruct((T, GLA_QK), F32),
        jax.ShapeDtypeStruct((T, GLA_QK), F32),
        jax.ShapeDtypeStruct((T, GLA_V), BF16),
        jax.ShapeDtypeStruct((T, GLA_QK), F32),
        jax.ShapeDtypeStruct((T, GLA_V), BF16),
        jax.ShapeDtypeStruct((T, SWA_Q), BF16),
        jax.ShapeDtypeStruct((T, 256), BF16),
        jax.ShapeDtypeStruct((T, 256), BF16),
        jax.ShapeDtypeStruct((T, LANES), BF16),
    ]
    return pl.pallas_call(
        body, name="proj_fwd", grid=(T // TM,), out_shape=outs,
        in_specs=[_row_spec(TM, D_MODEL), _const_spec((1, D_MODEL)), _any_spec(), _const_spec((LANES, GLA_QK)),
                  _const_spec((1, GLA_QK)), _row_spec(TM, LANES), _row_spec(TM, LANES), _row_spec(TM, LANES)],
        out_specs=[_row_spec(TM, o.shape[1]) for o in outs],
        scratch_shapes=[pltpu.VMEM((D_MODEL, IN_WIDTH_PAD), BF16), pltpu.SemaphoreType.DMA],
        compiler_params=_params(),
    )(x, g1, wp, gup, gbias, rc, rsa, rsb)


GLA_NB = GLA_BLOCK // GLA_CHUNK


def _gla_masks():
    n = GLA_BLOCK
    lane = _iota((n, LANES), 1)
    lane_masks = [(lane < 64).astype(F32), (lane >= 64).astype(F32)]
    row, col = _iota((n, n), 0), _iota((n, n), 1)
    same_chunk = (row >> 6) == (col >> 6)
    blk = ((_iota((256, LANES), 0) >> 7) == (_iota((256, LANES), 1) >> 6)).astype(F32)
    return lane_masks, same_chunk & (col <= row), same_chunk & (col >= row), blk


def _chunk_rows(vals):
    return jnp.concatenate([jnp.broadcast_to(v, (GLA_CHUNK, LANES)) for v in vals], axis=0)


def _gla_block_terms(q_ref, k_ref, b_ref, p):
    C = GLA_CHUNK
    cols = slice(LANES * p, LANES * p + LANES)
    bc = b_ref[:, cols]
    bl_rows = [b_ref[C * c + C - 1:C * c + C, cols] for c in range(GLA_NB)]
    bl = _chunk_rows(bl_rows)
    bm = _chunk_rows([b_ref[C * c + C // 2 - 1:C * c + C // 2, cols] for c in range(GLA_NB)])
    qs = q_ref[:, cols] * 0.125
    kk = k_ref[:, cols]
    eb = jnp.exp(bc)
    ekl = jnp.exp(bl - bc)
    eqm = jnp.exp(bc - bm)
    ekm = jnp.exp(bm - bc)
    return qs, kk, eb, ekl, eqm, ekm, [jnp.exp(r) for r in bl_rows]


def _block_cumsum(la, mask):
    return _mm_f32(mask.astype(F32), la)


def _gla_fwd(q, k, v, la):
    T = q.shape[0]
    NB = GLA_BLOCK // GLA_CHUNK
    C = GLA_CHUNK

    def body(q_ref, k_ref, v_ref, la_ref, o_ref, s_ref, st_ref, b_ref):
        @pl.when(pl.program_id(0) == 0)
        def _():
            st_ref[...] = jnp.zeros_like(st_ref)

        lane_masks, causal, _, blk = _gla_masks()
        b_ref[...] = _block_cumsum(la_ref[...], causal)
        for p in range(2):
            qs, kk, eb, ekl, eqm, ekm, gam = _gla_block_terms(q_ref, k_ref, b_ref, p)
            qh, kh, qm, km = qs * eb, kk * ekl, qs * eqm, kk * ekm
            vp = v_ref[:, 256 * p:256 * p + 256]
            intra = []
            for j in range(2):
                a = jnp.where(causal, _mm_nt(qm * lane_masks[j], km), 0.0)
                intra.append(_mm(a, vp[:, 128 * j:128 * j + 128]))
            kv = [blk * _mm_tn(vp[C * c:C * c + C], kh[C * c:C * c + C]) for c in range(NB)]
            st = st_ref[p]
            inter = []
            for c in range(NB):
                s_ref[c, p] = st[0:LANES] + st[LANES:2 * LANES]
                inter.append(_mm_nt(qh[C * c:C * c + C], st))
                st = st * gam[c] + kv[c]
            st_ref[p] = st
            o_ref[:, 256 * p:256 * p + 256] = (jnp.concatenate(inter, axis=0) + jnp.concatenate(intra, axis=1)).astype(BF16)

    return pl.pallas_call(
        body, name="gla_fwd", grid=(T // GLA_BLOCK,),
        out_shape=[jax.ShapeDtypeStruct((T, GLA_V), BF16), jax.ShapeDtypeStruct((T // C, 2, LANES, LANES), F32)],
        in_specs=[_row_spec(GLA_BLOCK, GLA_QK), _row_spec(GLA_BLOCK, GLA_QK), _row_spec(GLA_BLOCK, GLA_V),
                  _row_spec(GLA_BLOCK, GLA_QK)],
        out_specs=[_row_spec(GLA_BLOCK, GLA_V), pl.BlockSpec((NB, 2, LANES, LANES), lambda i: (i, 0, 0, 0))],
        scratch_shapes=[pltpu.VMEM((2, 256, LANES), F32), pltpu.VMEM((GLA_BLOCK, GLA_QK), F32)],
        compiler_params=_params(),
    )(q, k, v, la)


def _gla_bwd(q, k, v, la, s_all, do):
    T = q.shape[0]
    NB = GLA_BLOCK // GLA_CHUNK
    C = GLA_CHUNK
    nblk = T // GLA_BLOCK

    def body(q_ref, k_ref, v_ref, la_ref, s_ref, do_ref, dq_ref, dk_ref, dv_ref, dz_ref, dst_ref, b_ref):
        @pl.when(pl.program_id(0) == 0)
        def _():
            dst_ref[...] = jnp.zeros_like(dst_ref)

        lane_masks, causal, anti_causal, blk = _gla_masks()
        b_ref[...] = _block_cumsum(la_ref[...], causal)
        for p in range(2):
            cols = slice(LANES * p, LANES * p + LANES)
            qs, kk, eb, ekl, eqm, ekm, gam = _gla_block_terms(q_ref, k_ref, b_ref, p)
            qh, kh, qm, km = qs * eb, kk * ekl, qs * eqm, kk * ekm
            vp = v_ref[:, 256 * p:256 * p + 256]
            dop = do_ref[:, 256 * p:256 * p + 256]
            dqm = jnp.zeros((GLA_BLOCK, LANES), F32)
            dkm = jnp.zeros((GLA_BLOCK, LANES), F32)
            dv_intra = []
            for j in range(2):
                hs = slice(128 * j, 128 * j + 128)
                a = jnp.where(causal, _mm_nt(qm * lane_masks[j], km), 0.0)
                da = jnp.where(causal, _mm_nt(dop[:, hs], vp[:, hs]), 0.0)
                dv_intra.append(_mm_tn(a, dop[:, hs]))
                dqm = dqm + lane_masks[j] * _mm(da, km)
                dkm = dkm + lane_masks[j] * _mm_tn(da, qm)
            grow = [blk * _mm_tn(dop[C * c:C * c + C], qh[C * c:C * c + C]) for c in range(NB)]
            dst = dst_ref[p]
            dst_after = [None] * NB
            for c in reversed(range(NB)):
                dst_after[c] = dst
                dst = dst * gam[c] + grow[c]
            dst_ref[p] = dst
            dqh, dkh, dv_state, extra = [], [], [], []
            for c in range(NB):
                rows = slice(C * c, C * c + C)
                packed = s_ref[c, p]
                st = jnp.concatenate([packed * lane_masks[0][0:LANES], packed * lane_masks[1][0:LANES]], axis=0)
                dqh.append(_mm(dop[rows], st))
                dkh.append(_mm(vp[rows], dst_after[c]))
                dv_state.append(_mm_nt(kh[rows], dst_after[c]))
                extra.append(jnp.sum(dkh[c] * kh[rows], axis=0, keepdims=True)
                             + jnp.sum(st * dst_after[c], axis=0, keepdims=True) * gam[c])
            dqs = jnp.concatenate(dqh, axis=0) * eb + dqm * eqm
            dk = jnp.concatenate(dkh, axis=0) * ekl + dkm * ekm
            dg = _mm_f32(anti_causal.astype(F32), dqs * qs - dk * kk) + _chunk_rows(extra)
            dq_ref[:, cols] = (dqs * 0.125).astype(BF16)
            dk_ref[:, cols] = dk.astype(BF16)
            dz_ref[:, cols] = dg * (1.0 - jnp.exp(GLA_TAU * la_ref[:, cols])) * (1.0 / GLA_TAU)
            dv_ref[:, 256 * p:256 * p + 256] = (jnp.concatenate(dv_state, axis=0) + jnp.concatenate(dv_intra, axis=1)).astype(BF16)

    rev = lambda i: (nblk - 1 - i, 0)
    rspec = lambda cols: pl.BlockSpec((GLA_BLOCK, cols), rev)
    return pl.pallas_call(
        body, name="gla_bwd", grid=(nblk,),
        out_shape=[jax.ShapeDtypeStruct((T, GLA_QK), BF16), jax.ShapeDtypeStruct((T, GLA_QK), BF16),
                   jax.ShapeDtypeStruct((T, GLA_V), BF16), jax.ShapeDtypeStruct((T, GLA_QK), F32)],
        in_specs=[rspec(GLA_QK), rspec(GLA_QK), rspec(GLA_V), rspec(GLA_QK),
                  pl.BlockSpec((NB, 2, LANES, LANES), lambda i: (nblk - 1 - i, 0, 0, 0)), rspec(GLA_V)],
        out_specs=[rspec(GLA_QK), rspec(GLA_QK), rspec(GLA_V), rspec(GLA_QK)],
        scratch_shapes=[pltpu.VMEM((2, 256, LANES), F32), pltpu.VMEM((GLA_BLOCK, GLA_QK), F32)],
        compiler_params=_params(),
    )(q, k, v, la, s_all, do)


SWA_GROUP = 4


def _swa_stack(ref, g, first):
    parts = []
    for j in range(SWA_GROUP):
        m = 2 * g + j // 2
        pair = ref[:, 128 * m:128 * m + 128]
        zero = jnp.zeros_like(pair)
        parts.append(jnp.where(first, pair, zero) if j % 2 == 0 else jnp.where(first, zero, pair))
    return jnp.concatenate(parts, axis=0)


def _swa_unstack(rows, mm, first):
    W = SWA_BLOCK
    return jnp.where(first, rows[W * 2 * mm:W * (2 * mm + 1)], rows[W * (2 * mm + 1):W * (2 * mm + 2)])


def _swa_probs(qs, kp, kc, vp, vc, i, g, sink_ref, first4):
    W = SWA_BLOCK
    R = SWA_GROUP * W
    r, c = _iota((R, W), 0) & (W - 1), _iota((R, W), 1)
    neg = -1e30
    s_p = jnp.where((c > r) & (i > 0), _mm_nt(qs, kp), neg)
    s_c = jnp.where(c <= r, _mm_nt(qs, kc), neg)
    head = _iota((R, 1), 0) >> 7
    sink = jnp.where(head == 0, sink_ref[4 * g], jnp.where(head == 1, sink_ref[4 * g + 1],
                                                           jnp.where(head == 2, sink_ref[4 * g + 2], sink_ref[4 * g + 3])))
    m = jnp.maximum(jnp.max(jnp.maximum(s_p, s_c), axis=-1, keepdims=True), sink)
    p_p = jnp.exp(s_p - m)
    p_c = jnp.exp(s_c - m)
    p_s = jnp.exp(sink - m)
    one = jnp.ones((W, LANES), BF16)
    first = _iota((W, LANES), 1) < 64
    acc = _mm(p_p, jnp.where(first, vp, one)) + _mm(p_c, jnp.where(first, vc, one))
    rolled = pltpu.roll(acc, 64, 1)
    denom = jnp.where(first4, rolled, acc) + p_s
    return p_p, p_c, p_s, denom, acc, rolled


def _swa_fwd(sq, kd, vd, sinks):
    T = sq.shape[0]
    W = SWA_BLOCK
    prev = lambda i: (jnp.maximum(i - 1, 0), 0)

    def body(sink_ref, q_ref, kp_ref, kc_ref, vp_ref, vc_ref, o_ref):
        i = pl.program_id(0)
        first4 = _iota((SWA_GROUP * W, LANES), 1) < 64
        first = _iota((W, LANES), 1) < 64
        for g in range(2):
            gs = slice(128 * g, 128 * g + 128)
            qs = _swa_stack(q_ref, g, first)
            _, _, _, denom, acc, rolled = _swa_probs(qs, kp_ref[:, gs], kc_ref[:, gs], vp_ref[:, gs], vc_ref[:, gs],
                                                     i, g, sink_ref, first4)
            pv = jnp.where(first4, acc, rolled)
            o = pv / denom
            for mm in range(2):
                m = 2 * g + mm
                o_ref[:, 128 * m:128 * m + 128] = _swa_unstack(o, mm, first).astype(BF16)

    return pl.pallas_call(
        body, name="swa_fwd", grid=(T // W,), out_shape=jax.ShapeDtypeStruct((T, SWA_Q), BF16),
        in_specs=[pl.BlockSpec(memory_space=pltpu.SMEM), _row_spec(W, SWA_Q), pl.BlockSpec((W, 256), prev),
                  _row_spec(W, 256), pl.BlockSpec((W, 256), prev), _row_spec(W, 256)],
        out_specs=_row_spec(W, SWA_Q),
        compiler_params=_params(),
    )(sinks, sq, kd, kd, vd, vd)


def _swa_bwd(sq, kd, vd, sinks, do):
    T = sq.shape[0]
    W = SWA_BLOCK
    n = T // W
    cur = lambda i: (jnp.minimum(i, n - 1), 0)
    prev = lambda i: (jnp.clip(i - 1, 0, n - 1), 0)

    def body(sink_ref, q_ref, kp_ref, kc_ref, vp_ref, vc_ref, do_ref, dq_ref, dk_ref, dv_ref, ds_ref, ck_ref, cv_ref):
        i = pl.program_id(0)

        @pl.when(i == 0)
        def _():
            ds_ref[...] = jnp.zeros_like(ds_ref)
            ck_ref[...] = jnp.zeros_like(ck_ref)
            cv_ref[...] = jnp.zeros_like(cv_ref)

        @pl.when(i < n)
        def _():
            first4 = _iota((SWA_GROUP * W, LANES), 1) < 64
            first = _iota((W, LANES), 1) < 64
            for g in range(2):
                gs = slice(128 * g, 128 * g + 128)
                kp, kc, vp, vc = kp_ref[:, gs], kc_ref[:, gs], vp_ref[:, gs], vc_ref[:, gs]
                qs = _swa_stack(q_ref, g, first)
                dos = _swa_stack(do_ref, g, first)
                p_p, p_c, p_s, denom, _, _ = _swa_probs(qs, kp, kc, vp, vc, i, g, sink_ref, first4)
                inv = 1.0 / denom
                p_p, p_c = p_p * inv, p_c * inv
                dp_p = _mm_nt(dos, vp)
                dp_c = _mm_nt(dos, vc)
                delta = jnp.sum(p_p * dp_p + p_c * dp_c, axis=-1, keepdims=True)
                ds_p = p_p * (dp_p - delta)
                ds_c = p_c * (dp_c - delta)
                rows = slice(SWA_GROUP * W * g, SWA_GROUP * W * (g + 1))
                ds_ref[rows, :] = ds_ref[rows, :] - (p_s * delta) * inv
                dq = (_mm(ds_p, kp) + _mm(ds_c, kc)) * 0.125
                for mm in range(2):
                    m = 2 * g + mm
                    dq_ref[:, 128 * m:128 * m + 128] = _swa_unstack(dq, mm, first).astype(BF16)
                dk_ref[:, gs] = (ck_ref[:, gs] + _mm_tn(ds_p, qs)).astype(BF16)
                dv_ref[:, gs] = (cv_ref[:, gs] + _mm_tn(p_p, dos)).astype(BF16)
                ck_ref[:, gs] = _mm_tn(ds_c, qs)
                cv_ref[:, gs] = _mm_tn(p_c, dos)

        @pl.when(i == n)
        def _():
            dk_ref[...] = ck_ref[...].astype(BF16)
            dv_ref[...] = cv_ref[...].astype(BF16)

    return pl.pallas_call(
        body, name="swa_bwd", grid=(n + 1,),
        out_shape=[jax.ShapeDtypeStruct((T, SWA_Q), BF16), jax.ShapeDtypeStruct((T, 256), BF16),
                   jax.ShapeDtypeStruct((T, 256), BF16), jax.ShapeDtypeStruct((SWA_HEADS * W, LANES), F32)],
        in_specs=[pl.BlockSpec(memory_space=pltpu.SMEM), pl.BlockSpec((W, SWA_Q), cur), pl.BlockSpec((W, 256), prev),
                  pl.BlockSpec((W, 256), cur), pl.BlockSpec((W, 256), prev), pl.BlockSpec((W, 256), cur),
                  pl.BlockSpec((W, SWA_Q), cur)],
        out_specs=[pl.BlockSpec((W, SWA_Q), cur), pl.BlockSpec((W, 256), prev), pl.BlockSpec((W, 256), prev),
                   _const_spec((SWA_HEADS * W, LANES))],
        scratch_shapes=[pltpu.VMEM((W, 256), F32), pltpu.VMEM((W, 256), F32)],
        compiler_params=_params(),
    )(sinks, sq, kd, kd, vd, vd, do)


def _mix_out_fwd(x, og, gg, osw, gnorm, wout, g2):
    T = x.shape[0]
    TM = _wide_tile(T)

    def body(x_ref, og_ref, gg_ref, osw_ref, gn_ref, wout_ref, g2_ref, x1_ref, cat_ref):
        gn = gn_ref[...]
        for j in range(GLA_HEADS):
            hs = slice(128 * j, 128 * j + 128)
            o = og_ref[:, hs].astype(F32)
            r = lax.rsqrt(jnp.mean(o * o, axis=-1, keepdims=True) + EPS)
            gate = gg_ref[:, hs].astype(F32)
            cat_ref[:, hs] = (o * r * gn * (gate * _sigmoid(gate))).astype(BF16)
        cat_ref[:, GLA_V:] = osw_ref[...]
        mix = _mm(cat_ref[...], wout_ref[...])
        r2 = lax.rsqrt(jnp.mean(mix * mix, axis=-1, keepdims=True) + EPS)
        x1_ref[...] = x_ref[...] + mix * r2 * g2_ref[...]

    return pl.pallas_call(
        body, name="mix_out_fwd", grid=(T // TM,),
        out_shape=[jax.ShapeDtypeStruct((T, D_MODEL), F32), jax.ShapeDtypeStruct((T, D_MODEL), BF16)],
        in_specs=[_row_spec(TM, D_MODEL), _row_spec(TM, GLA_V), _row_spec(TM, GLA_V), _row_spec(TM, SWA_Q),
                  _const_spec((1, LANES)), _const_spec((D_MODEL, D_MODEL)), _const_spec((1, D_MODEL))],
        out_specs=[_row_spec(TM, D_MODEL), _row_spec(TM, D_MODEL)],
        compiler_params=_params(),
    )(x, og, gg, osw, gnorm, wout, g2)


HALO = 8


def _rows_before(v, prev1, prev2):
    row = _iota(v.shape, 0)
    m1 = jnp.where(row == 0, prev1, pltpu.roll(v, 1, 0))
    m2 = jnp.where(row == 0, prev2, jnp.where(row == 1, prev1, pltpu.roll(v, 2, 0)))
    return m1, m2


def _rows_after(v, next1, next2):
    n = v.shape[0]
    row = _iota(v.shape, 0)
    p1 = jnp.where(row == n - 1, next1, pltpu.roll(v, n - 1, 0))
    p2 = jnp.where(row == n - 1, next2, jnp.where(row == n - 2, next1, pltpu.roll(v, n - 2, 0)))
    return p1, p2


def _ff_pieces():
    return [(j, off, wd) for j in range(2) for off, wd in FF_PIECES]


def _ffn_fwd(x1, g3, wup, cw, cb, wdown, g4, target):
    T = x1.shape[0]

    def body(x1_ref, g3_ref, wup_hbm, cw_ref, cb_ref, wdn_hbm, g4_ref, tg_ref,
             h2_ref, up_ref, a_ref, c1_ref, c2_ref, y_ref, dx2_ref, loss_ref, wup_v, wdn_v, carry_ref, sems):
        _load_once(wup_hbm, wup_v, sems.at[0])
        _load_once(wdn_hbm, wdn_v, sems.at[1])

        @pl.when(pl.program_id(0) == 0)
        def _():
            carry_ref[...] = jnp.zeros_like(carry_ref)
            loss_ref[...] = jnp.zeros_like(loss_ref)

        x1 = x1_ref[...]
        r3 = lax.rsqrt(jnp.mean(x1 * x1, axis=-1, keepdims=True) + EPS)
        h2 = (x1 * r3 * g3_ref[...]).astype(BF16)
        h2_ref[...] = h2
        for j, off, wd in _ff_pieces():
            base = SHARD_FF * j + off
            u = []
            for half in range(2):
                cs = slice(D_FF * half + base, D_FF * half + base + wd)
                upb = _mm(h2, wup_v[2 * half + j, :, off:off + wd]).astype(BF16)
                up_ref[:, cs] = upb
                upf = upb.astype(F32)
                m1, m2 = _rows_before(upf, carry_ref[HALO - 1:HALO, cs], carry_ref[HALO - 2:HALO - 1, cs])
                u.append(cb_ref[:, cs] + cw_ref[0:1, cs] * m2 + cw_ref[1:2, cs] * m1 + cw_ref[2:3, cs] * upf)
                carry_ref[:, cs] = upf[TM - HALO:TM, :]
            act, dact = _gelu_parts(u[1])
            a = (act * u[0]).astype(BF16)
            out = slice(base, base + wd)
            a_ref[:, out] = a
            c1_ref[:, out] = act.astype(BF16)
            c2_ref[:, out] = (u[0] * dact).astype(BF16)
        y = _mm(a_ref[...], wdn_v[...])
        y_ref[...] = y
        r4 = lax.rsqrt(jnp.mean(y * y, axis=-1, keepdims=True) + EPS)
        err = x1 + y * r4 * g4_ref[...] - tg_ref[...]
        dx2_ref[...] = err * (1.0 / D_MODEL)
        loss_ref[...] = loss_ref[...] + jnp.sum(err * err) * (0.5 / D_MODEL)

    outs = [
        jax.ShapeDtypeStruct((T, D_MODEL), BF16),
        jax.ShapeDtypeStruct((T, 2 * D_FF), BF16),
        jax.ShapeDtypeStruct((T, D_FF), BF16),
        jax.ShapeDtypeStruct((T, D_FF), BF16),
        jax.ShapeDtypeStruct((T, D_FF), BF16),
        jax.ShapeDtypeStruct((T, D_MODEL), F32),
        jax.ShapeDtypeStruct((T, D_MODEL), F32),
        jax.ShapeDtypeStruct((8, LANES), F32),
    ]
    return pl.pallas_call(
        body, name="ffn_fwd", grid=(T // TM,), out_shape=outs,
        in_specs=[_row_spec(TM, D_MODEL), _const_spec((1, D_MODEL)), _any_spec(), _const_spec((3, 2 * D_FF)),
                  _const_spec((1, 2 * D_FF)), _any_spec(), _const_spec((1, D_MODEL)), _row_spec(TM, D_MODEL)],
        out_specs=[_row_spec(TM, D_MODEL), _row_spec(TM, 2 * D_FF), _row_spec(TM, D_FF), _row_spec(TM, D_FF),
                   _row_spec(TM, D_FF), _row_spec(TM, D_MODEL), _row_spec(TM, D_MODEL), _const_spec((8, LANES))],
        scratch_shapes=[pltpu.VMEM((N_SHARD, D_MODEL, SHARD_FF), BF16), pltpu.VMEM((D_FF, D_MODEL), BF16),
                        pltpu.VMEM((HALO, 2 * D_FF), F32), pltpu.SemaphoreType.DMA((2,))],
        compiler_params=_params(),
    )(x1, g3, wup, cw, cb, wdown, g4, target)


def _ffn_bwd(dx2, y, g4, up, c1, c2, cw, wdown, wup, x1, g3):
    T = dx2.shape[0]
    nt = T // TM
    rev = lambda i: (nt - 1 - i, 0)

    def body(dn_ref, y_ref, g4_ref, up_ref, c1_ref, c2_ref, cw_ref, wdn_hbm, wup_hbm, x1_ref, g3_ref,
             dy_ref, dup_ref, dx1_ref, dg4_ref, dg3_ref, dcb_ref, dcw_ref, wup_v, wdn_v, carry_ref, sems):
        _load_once(wup_hbm, wup_v, sems.at[0])
        _load_once(wdn_hbm, wdn_v, sems.at[1])

        @pl.when(pl.program_id(0) == 0)
        def _():
            carry_ref[...] = jnp.zeros_like(carry_ref)
            dg4_ref[...] = jnp.zeros_like(dg4_ref)
            dg3_ref[...] = jnp.zeros_like(dg3_ref)
            dcb_ref[...] = jnp.zeros_like(dcb_ref)
            dcw_ref[...] = jnp.zeros_like(dcw_ref)

        dn = dn_ref[...]
        y = y_ref[...]
        g4v = g4_ref[...]
        r4 = lax.rsqrt(jnp.mean(y * y, axis=-1, keepdims=True) + EPS)
        dg4_ref[...] = dg4_ref[...] + jnp.sum(dn * y * r4, axis=0, keepdims=True)
        dy = _rms_bwd(y, r4, g4v, dn).astype(BF16)
        dy_ref[...] = dy
        dh2 = jnp.zeros((TM, D_MODEL), F32)
        for j, off, wd in _ff_pieces():
            base = SHARD_FF * j + off
            da = _mm_nt(dy, wdn_v[base:base + wd, :])
            for half, coef_ref in enumerate((c1_ref, c2_ref)):
                cs = slice(D_FF * half + base, D_FF * half + base + wd)
                du = da * coef_ref[:, base:base + wd].astype(F32)
                p1, p2 = _rows_after(du, carry_ref[0:1, cs], carry_ref[1:2, cs])
                carry_ref[:, cs] = du[0:HALO, :]
                upf = up_ref[:, cs].astype(F32)
                dcb_ref[:, cs] = dcb_ref[:, cs] + jnp.sum(du, axis=0, keepdims=True)
                dcw_ref[0:1, cs] = dcw_ref[0:1, cs] + jnp.sum(p2 * upf, axis=0, keepdims=True)
                dcw_ref[1:2, cs] = dcw_ref[1:2, cs] + jnp.sum(p1 * upf, axis=0, keepdims=True)
                dcw_ref[2:3, cs] = dcw_ref[2:3, cs] + jnp.sum(du * upf, axis=0, keepdims=True)
                dup = (cw_ref[2:3, cs] * du + cw_ref[1:2, cs] * p1 + cw_ref[0:1, cs] * p2).astype(BF16)
                dup_ref[:, cs] = dup
                dh2 = dh2 + _mm_nt(dup, wup_v[2 * half + j, :, off:off + wd])
        x1 = x1_ref[...]
        r3 = lax.rsqrt(jnp.mean(x1 * x1, axis=-1, keepdims=True) + EPS)
        dg3_ref[...] = dg3_ref[...] + jnp.sum(dh2 * x1 * r3, axis=0, keepdims=True)
        dx1_ref[...] = dn + _rms_bwd(x1, r3, g3_ref[...], dh2)

    outs = [
        jax.ShapeDtypeStruct((T, D_MODEL), BF16),
        jax.ShapeDtypeStruct((T, 2 * D_FF), BF16),
        jax.ShapeDtypeStruct((T, D_MODEL), F32),
        jax.ShapeDtypeStruct((1, D_MODEL), F32),
        jax.ShapeDtypeStruct((1, D_MODEL), F32),
        jax.ShapeDtypeStruct((1, 2 * D_FF), F32),
        jax.ShapeDtypeStruct((3, 2 * D_FF), F32),
    ]
    return pl.pallas_call(
        body, name="ffn_bwd", grid=(nt,), out_shape=outs,
        in_specs=[pl.BlockSpec((TM, D_MODEL), rev), pl.BlockSpec((TM, D_MODEL), rev), _const_spec((1, D_MODEL)),
                  pl.BlockSpec((TM, 2 * D_FF), rev), pl.BlockSpec((TM, D_FF), rev), pl.BlockSpec((TM, D_FF), rev),
                  _const_spec((3, 2 * D_FF)), _any_spec(), _any_spec(), pl.BlockSpec((TM, D_MODEL), rev),
                  _const_spec((1, D_MODEL))],
        out_specs=[pl.BlockSpec((TM, D_MODEL), rev), pl.BlockSpec((TM, 2 * D_FF), rev), pl.BlockSpec((TM, D_MODEL), rev),
                   _const_spec((1, D_MODEL)), _const_spec((1, D_MODEL)), _const_spec((1, 2 * D_FF)),
                   _const_spec((3, 2 * D_FF))],
        scratch_shapes=[pltpu.VMEM((N_SHARD, D_MODEL, SHARD_FF), BF16), pltpu.VMEM((D_FF, D_MODEL), BF16),
                        pltpu.VMEM((HALO, 2 * D_FF), F32), pltpu.SemaphoreType.DMA((2,))],
        compiler_params=_params(),
    )(dx2, y, g4, up, c1, c2, cw, wdown, wup, x1, g3)


def _mix_out_bwd(dx1, cat, g2, wout, og, gg, gnorm):
    T = dx1.shape[0]
    TM = _wide_tile(T)

    def body(dx1_ref, cat_ref, g2_ref, wout_ref, og_ref, gg_ref, gn_ref,
             dmix_ref, dog_ref, dgg_ref, dosw_ref, dg2_ref, dgn_ref):
        @pl.when(pl.program_id(0) == 0)
        def _():
            dg2_ref[...] = jnp.zeros_like(dg2_ref)
            dgn_ref[...] = jnp.zeros_like(dgn_ref)

        dx1 = dx1_ref[...]
        mix = _mm(cat_ref[...], wout_ref[...])
        r2 = lax.rsqrt(jnp.mean(mix * mix, axis=-1, keepdims=True) + EPS)
        dg2_ref[...] = dg2_ref[...] + jnp.sum(dx1 * mix * r2, axis=0, keepdims=True)
        dmix = _rms_bwd(mix, r2, g2_ref[...], dx1).astype(BF16)
        dmix_ref[...] = dmix
        dcat = _mm_nt(dmix, wout_ref[...])
        dosw_ref[...] = dcat[:, GLA_V:].astype(BF16)
        gn = gn_ref[...]
        dgn = jnp.zeros((1, LANES), F32)
        for j in range(GLA_HEADS):
            hs = slice(128 * j, 128 * j + 128)
            o = og_ref[:, hs].astype(F32)
            r = lax.rsqrt(jnp.mean(o * o, axis=-1, keepdims=True) + EPS)
            gate = gg_ref[:, hs].astype(F32)
            sg = _sigmoid(gate)
            dgated = dcat[:, hs]
            dnorm = dgated * (gate * sg)
            dgg_ref[:, hs] = (dgated * (o * r * gn) * (sg * (1.0 + gate * (1.0 - sg)))).astype(BF16)
            dgn = dgn + jnp.sum(dnorm * o * r, axis=0, keepdims=True)
            dog_ref[:, hs] = _rms_bwd(o, r, gn, dnorm)
        dgn_ref[...] = dgn_ref[...] + dgn

    return pl.pallas_call(
        body, name="mix_out_bwd", grid=(T // TM,),
        out_shape=[jax.ShapeDtypeStruct((T, D_MODEL), BF16), jax.ShapeDtypeStruct((T, GLA_V), F32),
                   jax.ShapeDtypeStruct((T, GLA_V), BF16), jax.ShapeDtypeStruct((T, SWA_Q), BF16),
                   jax.ShapeDtypeStruct((1, D_MODEL), F32), jax.ShapeDtypeStruct((1, LANES), F32)],
        in_specs=[_row_spec(TM, D_MODEL), _row_spec(TM, D_MODEL), _const_spec((1, D_MODEL)),
                  _const_spec((D_MODEL, D_MODEL)), _row_spec(TM, GLA_V), _row_spec(TM, GLA_V), _const_spec((1, LANES))],
        out_specs=[_row_spec(TM, D_MODEL), _row_spec(TM, GLA_V), _row_spec(TM, GLA_V), _row_spec(TM, SWA_Q),
                   _const_spec((1, D_MODEL)), _const_spec((1, LANES))],
        compiler_params=_params(),
    )(dx1, cat, g2, wout, og, gg, gnorm)


def _proj_bwd(x, g1, wp, gup, glr, dq, dk, dv, dgg, dsq, dkd, dvd, dz, rc, rsa, rsb, dx1):
    T = x.shape[0]
    TM = _wide_tile(T)

    def body(x_ref, g1_ref, wp_hbm, gup_ref, glr_ref, dq_ref, dk_ref, dv_ref, dgg_ref, dsq_ref, dkd_ref, dvd_ref,
             dz_ref, rc_ref, rsa_ref, rsb_ref, dx1_ref, dx_ref, dp_ref, dg1_ref, dgup_ref, dgb_ref, wp_v, sem):
        _load_once(wp_hbm, wp_v, sem)

        @pl.when(pl.program_id(0) == 0)
        def _():
            dg1_ref[...] = jnp.zeros_like(dg1_ref)
            dgup_ref[...] = jnp.zeros_like(dgup_ref)
            dgb_ref[...] = jnp.zeros_like(dgb_ref)

        rc_, rsa_, rsb_ = rc_ref[...], rsa_ref[...], rsb_ref[...]
        dp_ref[:, 0:256] = dq_ref[...]
        dp_ref[:, 256:512] = dk_ref[...]
        dp_ref[:, 512:1024] = dv_ref[...]
        dp_ref[:, 1024:1536] = dgg_ref[...]
        for s in range(4):
            cs = slice(128 * s, 128 * s + 128)
            dp_ref[:, 1536 + 128 * s:1664 + 128 * s] = _rotate_bwd(dsq_ref[:, cs].astype(F32), rc_, rsa_, rsb_).astype(BF16)
        first = _iota((TM, LANES), 1) < 64
        dk0 = dkd_ref[:, 0:128].astype(F32)
        dk1 = dkd_ref[:, 128:256].astype(F32)
        dkr = jnp.where(first, dk0 + pltpu.roll(dk0, 64, 1), dk1 + pltpu.roll(dk1, 64, 1))
        dp_ref[:, 2048:2176] = _rotate_bwd(dkr, rc_, rsa_, rsb_).astype(BF16)
        dv0 = dvd_ref[:, 0:128].astype(F32)
        dv1 = dvd_ref[:, 128:256].astype(F32)
        dp_ref[:, 2176:2304] = jnp.where(first, dv0 + pltpu.roll(dv0, 64, 1), dv1 + pltpu.roll(dv1, 64, 1)).astype(BF16)
        dz = dz_ref[...]
        dzb = dz.astype(BF16)
        dp_ref[:, 2304:2432] = _mm_nt(dzb, gup_ref[...]).astype(BF16)
        dgup_ref[...] = dgup_ref[...] + _mm_tn(glr_ref[...], dzb)
        dgb_ref[...] = dgb_ref[...] + jnp.sum(dz, axis=0, keepdims=True)
        dh1 = _mm_nt(dp_ref[...], wp_v[...])
        xt = x_ref[...]
        r = lax.rsqrt(jnp.mean(xt * xt, axis=-1, keepdims=True) + EPS)
        dg1_ref[...] = dg1_ref[...] + jnp.sum(dh1 * xt * r, axis=0, keepdims=True)
        dx_ref[...] = dx1_ref[...] + _rms_bwd(xt, r, g1_ref[...], dh1)

    row = lambda cols: _row_spec(TM, cols)
    return pl.pallas_call(
        body, name="proj_bwd", grid=(T // TM,),
        out_shape=[jax.ShapeDtypeStruct((T, D_MODEL), F32), jax.ShapeDtypeStruct((T, IN_WIDTH_PAD), BF16),
                   jax.ShapeDtypeStruct((1, D_MODEL), F32), jax.ShapeDtypeStruct((LANES, GLA_QK), F32),
                   jax.ShapeDtypeStruct((1, GLA_QK), F32)],
        in_specs=[row(D_MODEL), _const_spec((1, D_MODEL)), _any_spec(), _const_spec((LANES, GLA_QK)), row(LANES),
                  row(GLA_QK), row(GLA_QK), row(GLA_V), row(GLA_V), row(SWA_Q), row(256), row(256), row(GLA_QK),
                  row(LANES), row(LANES), row(LANES), row(D_MODEL)],
        out_specs=[row(D_MODEL), row(IN_WIDTH_PAD), _const_spec((1, D_MODEL)), _const_spec((LANES, GLA_QK)),
                   _const_spec((1, GLA_QK))],
        scratch_shapes=[pltpu.VMEM((D_MODEL, IN_WIDTH_PAD), BF16), pltpu.SemaphoreType.DMA],
        compiler_params=_params(),
    )(x, g1, wp, gup, glr, dq, dk, dv, dgg, dsq, dkd, dvd, dz, rc, rsa, rsb, dx1)


def _matmul_tn(a, b, tn, name, column_blocks_major=False):
    T, M = a.shape
    N = b.shape[1]
    tk = next(t for t in (2048, 1024, 512, TM) if T % t == 0 and t * (M + tn) <= 2048 * (D_MODEL + SHARD_FF))
    nk = T // tk
    if column_blocks_major:
        out_shape = jax.ShapeDtypeStruct((N // tn, M, tn), F32)
        out_spec = pl.BlockSpec((None, M, tn), lambda j, kk: (j, 0, 0))
    else:
        out_shape = jax.ShapeDtypeStruct((M, N), F32)
        out_spec = pl.BlockSpec((M, tn), lambda j, kk: (0, j))

    def body(a_ref, b_ref, o_ref):
        kk = pl.program_id(1)

        @pl.when(kk == 0)
        def _():
            o_ref[...] = jnp.zeros_like(o_ref)

        o_ref[...] = o_ref[...] + _mm_tn(a_ref[...], b_ref[...])

    return pl.pallas_call(
        body, name=name, grid=(N // tn, nk), out_shape=out_shape,
        in_specs=[pl.BlockSpec((tk, M), lambda j, kk: (kk, 0)), pl.BlockSpec((tk, tn), lambda j, kk: (kk, j))],
        out_specs=out_spec,
        compiler_params=_params(),
    )(a, b)


def _adamw_update(w_ref, g_ref, m_ref, v_ref, d_ref, m2_ref, v2_ref):
    g_ = g_ref[...]
    m2 = ADAM_B1 * m_ref[...] + (1.0 - ADAM_B1) * g_
    v2 = ADAM_B2 * v_ref[...] + (1.0 - ADAM_B2) * (g_ * g_)
    m_hat = m2 / (1.0 - ADAM_B1 ** ADAM_STEP)
    v_hat = v2 / (1.0 - ADAM_B2 ** ADAM_STEP)
    d_ref[...] = -ADAM_LR * (m_hat / (jnp.sqrt(v_hat) + ADAM_EPS) + ADAM_WD * w_ref[...])
    m2_ref[...] = m2
    v2_ref[...] = v2


def _adamw(w, g, m, v, rows, name):
    R, C = w.shape

    def body(*refs):
        _adamw_update(*refs[:7])
        refs[7][...] = refs[1][...]

    spec = pl.BlockSpec((rows, C), lambda i: (i, 0))
    return pl.pallas_call(
        body, name=name, grid=(R // rows,), out_shape=[jax.ShapeDtypeStruct((R, C), F32)] * 4,
        in_specs=[spec] * 4, out_specs=[spec] * 4, compiler_params=_params(),
    )(w, g, m, v)


def _adamw_small(ws, gs, ms, vs):
    n = len(ws)

    def body(*refs):
        w_, g_, m_, v_, d_, m2_, v2_, g2_ = (refs[n * i:n * (i + 1)] for i in range(8))
        for k in range(n):
            _adamw_update(w_[k], g_[k], m_[k], v_[k], d_[k], m2_[k], v2_[k])
            g2_[k][...] = g_[k][...]

    vm = pl.BlockSpec(memory_space=pltpu.VMEM)
    outs = pl.pallas_call(
        body, name="adamw_small", out_shape=[jax.ShapeDtypeStruct(w.shape, F32) for w in ws] * 4,
        in_specs=[vm] * (4 * n), out_specs=[vm] * (4 * n),
    )(*ws, *gs, *ms, *vs)
    return outs[:n], outs[n:2 * n], outs[2 * n:3 * n], outs[3 * n:]


def _place():
    x, y, c = lax.axis_index("x"), lax.axis_index("y"), lax.axis_index("c")
    chips = [(1 - x, y), (x, 1 - y), (1 - x, 1 - y)]
    return x, y, c, chips


class _staged_copies:
    def __init__(self, srcs, dsts, stage, sems):
        n = len(srcs)
        self.loads = [pltpu.make_async_copy(srcs[k], stage[k], sems.at[k]) for k in range(n)]
        self.stores = [pltpu.make_async_copy(stage[k], dsts[k], sems.at[n + k]) for k in range(n)]

    def load(self):
        for cp in self.loads:
            cp.start()

    def store(self):
        for ld, st in zip(self.loads, self.stores):
            ld.wait()
            st.start()

    def finish(self):
        for cp in self.stores:
            cp.wait()


def _allgather_shards(parts, unit_rows):
    n = len(parts)
    units = [(k, r, unit_rows[k]) for k in range(n) for r in range(0, parts[k].shape[0] // 2, unit_rows[k])]
    nu = len(units)

    def body(*refs):
        ins, outs, stage = refs[:n], refs[n:2 * n], refs[2 * n:3 * n]
        send_sems, recv_sems, local_sems = refs[3 * n:]
        x, y, c, chips = _place()
        sibling = (x, y, 1 - c)
        own = _staged_copies(ins, [o.at[2 * x + y] for o in outs], stage, local_sems)

        def block(i, px, py, half):
            k, r, u = units[i]
            return outs[k].at[2 * px + py, pl.ds(half * (parts[k].shape[0] // 2) + r, u), :]

        def copy(i, j, px, py, half, to, src=None):
            return pltpu.make_async_remote_copy(
                src_ref=block(i, px, py, half) if src is None else src, dst_ref=block(i, px, py, half),
                send_sem=send_sems.at[nu * j + i], recv_sem=recv_sems.at[nu * j + i], device_id=to, device_id_type=MESH)

        own.load()
        first, passed = [], []
        for i, (k, r, u) in enumerate(units):
            for j, chip in enumerate(chips):
                src = ins[k].at[pl.ds(c * (parts[k].shape[0] // 2) + r, u), :]
                first.append(copy(i, j, x, y, c, (*chip, c), src=src))
                first[-1].start()
        own.store()
        for i in range(nu):
            for j, chip in enumerate(chips):
                copy(i, j, *chip, c, (x, y, c)).wait_recv()
                passed.append(copy(i, 3 + j, *chip, c, sibling))
                passed[-1].start()
        for i in range(nu):
            for j, chip in enumerate(chips):
                copy(i, 3 + j, *chip, 1 - c, (x, y, c)).wait_recv()
        for cp in first + passed:
            cp.wait_send()
        own.finish()

    return pl.pallas_call(
        body, name="allgather_shards", out_shape=[jax.ShapeDtypeStruct((N_SHARD,) + p.shape, p.dtype) for p in parts],
        in_specs=[_any_spec()] * n, out_specs=[_any_spec()] * n,
        scratch_shapes=[pltpu.VMEM(p.shape, p.dtype) for p in parts] + [
            pltpu.SemaphoreType.DMA((6 * nu,)), pltpu.SemaphoreType.DMA((6 * nu,)), pltpu.SemaphoreType.DMA((2 * n,))],
        compiler_params=_params(),
    )(*parts)


def _d2d_pieces(rows, piece_rows):
    return [(r, piece_rows) for r in range(0, rows, piece_rows)]


def _rs_pair_swap(arrs, piece_rows, name):
    n = len(arrs)

    def body(*refs):
        ins, outs = refs[:n], refs[n:2 * n]
        send_sems, recv_sems = refs[2 * n:]
        x, y, c, _ = _place()
        sibling = (x, y, 1 - c)
        for k in range(n):
            H = arrs[k].shape[1] // 2
            for s in range(N_SHARD):
                for r, pr in _d2d_pieces(H, piece_rows[k]):
                    pltpu.make_async_remote_copy(
                        src_ref=ins[k].at[s, pl.ds((1 - c) * H + r, pr), :], dst_ref=outs[k].at[s, pl.ds(r, pr), :],
                        send_sem=send_sems.at[k], recv_sem=recv_sems.at[k], device_id=sibling, device_id_type=MESH).start()
        for k in range(n):
            H = arrs[k].shape[1] // 2
            whole = pltpu.make_async_remote_copy(
                src_ref=ins[k].at[:, pl.ds(0, H), :], dst_ref=outs[k], send_sem=send_sems.at[k], recv_sem=recv_sems.at[k],
                device_id=sibling, device_id_type=MESH)
            whole.wait_recv()
            whole.wait_send()

    return pl.pallas_call(
        body, name=name,
        out_shape=[jax.ShapeDtypeStruct((N_SHARD, a.shape[1] // 2, a.shape[2]), F32) for a in arrs],
        in_specs=[_any_spec()] * n, out_specs=[_any_spec()] * n,
        scratch_shapes=[pltpu.SemaphoreType.DMA((n,)), pltpu.SemaphoreType.DMA((n,))],
    )(*arrs)


def _rs_add_pair(a, got, core, rows, name):
    _, H, C = got.shape
    nb = H // rows

    def body(c_ref, a_ref, b_ref, o_ref):
        o_ref[...] = (a_ref[...] + b_ref[...]).astype(BF16)

    spec = pl.BlockSpec((1, rows, C), lambda s, r, c_ref: (s, r, 0))
    return pl.pallas_call(
        body, name=name, out_shape=jax.ShapeDtypeStruct(got.shape, BF16),
        grid_spec=pltpu.PrefetchScalarGridSpec(
            num_scalar_prefetch=1, grid=(N_SHARD, nb),
            in_specs=[pl.BlockSpec((1, rows, C), lambda s, r, c_ref: (s, c_ref[0] * nb + r, 0)), spec], out_specs=spec),
        compiler_params=_params(),
    )(core, a, got)


def _rs_sum_chips(parts, rows, name):
    _, H, C = parts.shape

    def body(p_ref, o_ref):
        o_ref[...] = ((p_ref[0].astype(F32) + p_ref[1].astype(F32)) + p_ref[2].astype(F32)) + p_ref[3].astype(F32)

    return pl.pallas_call(
        body, name=name, grid=(H // rows,), out_shape=jax.ShapeDtypeStruct((H, C), F32),
        in_specs=[pl.BlockSpec((N_SHARD, rows, C), lambda r: (0, r, 0))],
        out_specs=pl.BlockSpec((rows, C), lambda r: (r, 0)), compiler_params=_params(),
    )(parts)


def _rs_pair_share(halves, piece_rows, name):
    n = len(halves)

    def body(*refs):
        ins, outs, stage = refs[:n], refs[n:2 * n], refs[2 * n:3 * n]
        send_sems, recv_sems, local_sems = refs[3 * n:]
        x, y, c, _ = _place()
        sibling = (x, y, 1 - c)
        own = _staged_copies(ins, [outs[k].at[pl.ds(c * halves[k].shape[0], halves[k].shape[0]), :] for k in range(n)],
                             stage, local_sems)
        own.load()
        for k in range(n):
            H = halves[k].shape[0]
            for r, pr in _d2d_pieces(H, piece_rows[k]):
                pltpu.make_async_remote_copy(
                    src_ref=ins[k].at[pl.ds(r, pr), :], dst_ref=outs[k].at[pl.ds(c * H + r, pr), :],
                    send_sem=send_sems.at[k], recv_sem=recv_sems.at[k], device_id=sibling, device_id_type=MESH).start()
        own.store()
        for k in range(n):
            H = halves[k].shape[0]
            whole = pltpu.make_async_remote_copy(
                src_ref=ins[k], dst_ref=outs[k].at[pl.ds((1 - c) * H, H), :], send_sem=send_sems.at[k],
                recv_sem=recv_sems.at[k], device_id=sibling, device_id_type=MESH)
            whole.wait_recv()
            whole.wait_send()
        own.finish()

    return pl.pallas_call(
        body, name=name, out_shape=[jax.ShapeDtypeStruct((2 * h.shape[0], h.shape[1]), F32) for h in halves],
        in_specs=[_any_spec()] * n, out_specs=[_any_spec()] * n,
        scratch_shapes=[pltpu.VMEM(h.shape, F32) for h in halves] + [
            pltpu.SemaphoreType.DMA((n,)), pltpu.SemaphoreType.DMA((n,)), pltpu.SemaphoreType.DMA((2 * n,))],
        compiler_params=_params(),
    )(*halves)


_HBM = pl.BlockSpec(memory_space=pltpu.HBM)
_SEM = pl.BlockSpec(memory_space=pltpu.SEMAPHORE)
_EFFECT = pltpu.SideEffectType.DATAFLOW_SIDE_EFFECTING


def _gather_plan(srcs, lands, x, y, c, chips):
    plan = []
    for k in range(len(srcs)):
        H = srcs[k].shape[0] // 2
        for px, py in chips:
            plan.append((srcs[k].at[pl.ds(c * H, H), :], lands[k].at[2 * x + y, pl.ds(c * H, H), :], (px, py, c),
                         lands[k].at[2 * px + py, pl.ds(c * H, H), :]))
    return plan


def _scatter_plan(srcs, lands, x, y, c, chips):
    plan = []
    for k in range(len(srcs)):
        for px, py in chips:
            plan.append((srcs[k].at[2 * px + py], lands[k].at[2 * x + y], (px, py, c), lands[k].at[2 * px + py]))
    return plan


def _ici_start(srcs, lands, make_plan, name):
    n = len(srcs)
    ncopy = 3 * n

    def body(*refs):
        ins, lnd = refs[:n], refs[n:2 * n]
        send_sems, recv_sems = refs[2 * n], refs[2 * n + 1]
        token = refs[-1]
        x, y, c, chips = _place()
        for i, (src, dst, peer, _) in enumerate(make_plan(ins, lnd, x, y, c, chips)):
            pltpu.make_async_remote_copy(src_ref=src, dst_ref=dst, send_sem=send_sems.at[i], recv_sem=recv_sems.at[i],
                                         device_id=peer, device_id_type=MESH).start()
        token[...] = jnp.zeros_like(token)

    arrays = list(srcs) + list(lands)
    return pl.pallas_call(
        body, name=name,
        out_shape=(pltpu.SemaphoreType.DMA((ncopy,)), pltpu.SemaphoreType.DMA((ncopy,)),
                   *[pltpu.HBM(a.shape, a.dtype) for a in arrays], jax.ShapeDtypeStruct((8, LANES), F32)),
        in_specs=[_HBM] * (2 * n), out_specs=(_SEM, _SEM, *[_HBM] * (2 * n), pl.BlockSpec(memory_space=pltpu.VMEM)),
        input_output_aliases={i: 2 + i for i in range(2 * n)},
        compiler_params=pltpu.CompilerParams(has_side_effects=_EFFECT),
    )(*[pltpu.with_memory_space_constraint(a, pltpu.HBM) for a in arrays])


def _ici_wait(started, after, make_plan, name):
    send_sems, recv_sems = started[0], started[1]
    arrays = list(started[2:-1])
    n = len(arrays) // 2

    def body(*refs):
        ins, lnd = refs[:n], refs[n:2 * n]
        send_sems, recv_sems = refs[2 * n], refs[2 * n + 1]
        x, y, c, chips = _place()
        for i, (src, _, peer, landed) in enumerate(make_plan(ins, lnd, x, y, c, chips)):
            cp = pltpu.make_async_remote_copy(src_ref=src, dst_ref=landed, send_sem=send_sems.at[i],
                                              recv_sem=recv_sems.at[i], device_id=peer, device_id_type=MESH)
            cp.wait_send()
            cp.wait_recv()

    outs = pl.pallas_call(
        body, name=name, out_shape=tuple(pltpu.HBM(a.shape, a.dtype) for a in arrays),
        in_specs=[_HBM] * (2 * n) + [_SEM, _SEM, pl.BlockSpec(memory_space=pl.ANY)], out_specs=tuple([_HBM] * (2 * n)),
        input_output_aliases={i: i for i in range(2 * n)},
        compiler_params=pltpu.CompilerParams(has_side_effects=_EFFECT),
    )(*arrays, send_sems, recv_sems, after)
    return list(outs[n:])


def _gather_finish(parts, lands):
    n = len(parts)

    def body(*refs):
        ins, lnd, outs, stage = refs[:n], refs[n:2 * n], refs[2 * n:3 * n], refs[3 * n:4 * n]
        send_sems, recv_sems, local_sems = refs[4 * n:]
        x, y, c, chips = _place()
        sibling = (x, y, 1 - c)
        own = _staged_copies(ins, [o.at[2 * x + y] for o in outs], stage, local_sems)
        own.load()
        sends = []
        for k in range(n):
            H = parts[k].shape[0] // 2
            for j, (px, py) in enumerate(chips):
                half = outs[k].at[2 * px + py, pl.ds(c * H, H), :]
                sends.append(pltpu.make_async_remote_copy(src_ref=half, dst_ref=half, send_sem=send_sems.at[3 * k + j],
                                                          recv_sem=recv_sems.at[3 * k + j], device_id=sibling, device_id_type=MESH))
                sends[-1].start()
        own.store()
        for k in range(n):
            H = parts[k].shape[0] // 2
            for j, (px, py) in enumerate(chips):
                other = outs[k].at[2 * px + py, pl.ds((1 - c) * H, H), :]
                pltpu.make_async_remote_copy(src_ref=other, dst_ref=other, send_sem=send_sems.at[3 * k + j],
                                             recv_sem=recv_sems.at[3 * k + j], device_id=sibling, device_id_type=MESH).wait_recv()
        for cp in sends:
            cp.wait_send()
        own.finish()

    return pl.pallas_call(
        body, name="gather_finish", out_shape=[jax.ShapeDtypeStruct(l.shape, l.dtype) for l in lands],
        in_specs=[_any_spec()] * (2 * n), out_specs=[_any_spec()] * n,
        input_output_aliases={n + k: k for k in range(n)},
        scratch_shapes=[pltpu.VMEM(p.shape, p.dtype) for p in parts] + [
            pltpu.SemaphoreType.DMA((3 * n,)), pltpu.SemaphoreType.DMA((3 * n,)), pltpu.SemaphoreType.DMA((2 * n,))],
        compiler_params=_params(),
    )(*parts, *lands)


def _place_own(part, chip, rows, name):
    _, H, C = part.shape

    def body(chip_ref, p_ref, o_ref):
        o_ref[...] = p_ref[...]

    spec = pl.BlockSpec((1, rows, C), lambda r, chip_ref: (chip_ref[0], r, 0))
    return pl.pallas_call(
        body, name=name, out_shape=jax.ShapeDtypeStruct(part.shape, part.dtype),
        grid_spec=pltpu.PrefetchScalarGridSpec(num_scalar_prefetch=1, grid=(H // rows,), in_specs=[spec], out_specs=spec),
        compiler_params=_params(),
    )(chip, part)


SMALL_COLS = 1024


def _small_rows(shapes):
    starts, row = [], 0
    for r, cdim in shapes:
        starts.append(row)
        row += r * (-(-cdim // SMALL_COLS))
    return starts, -(-row // 8) * 8


def _allreduce_small(arrays, sink_rows, loss):
    n = len(arrays)
    shapes = [a.shape for a in arrays] + [(1, SWA_HEADS), (1, 1)]
    starts, total_rows = _small_rows(shapes)

    def pieces(k):
        r, cdim = shapes[k]
        per = -(-cdim // SMALL_COLS)
        return [(i, SMALL_COLS * j, min(SMALL_COLS, cdim - SMALL_COLS * j), starts[k] + per * i + j)
                for i in range(r) for j in range(per)]

    def body(*refs):
        ins, sink_ref, loss_ref = refs[:n], refs[n], refs[n + 1]
        outs = refs[n + 2:2 * n + 4]
        mine, all_ref, tot_ref, send_sems, recv_sems = refs[2 * n + 4:]
        x, y, c, _ = _place()
        me = 4 * x + 2 * y + c
        mine[...] = jnp.zeros_like(mine)
        for k in range(n):
            for i, col, wd, row in pieces(k):
                mine[row:row + 1, 0:wd] = ins[k][i:i + 1, col:col + wd]
        lane = _iota((1, SMALL_COLS), 1)
        sinks = jnp.zeros((1, SMALL_COLS), F32)
        for h in range(SWA_HEADS):
            head = jnp.sum(sink_ref[SWA_BLOCK * h:SWA_BLOCK * (h + 1), :]) * (1.0 / LANES)
            sinks = jnp.where(lane == h, head, sinks)
        mine[starts[n]:starts[n] + 1, :] = sinks
        mine[starts[n + 1]:starts[n + 1] + 1, 0:LANES] = loss_ref[0:1, :]
        all_ref[me] = mine[...]
        sends = []
        for k in range(1, 8):
            kx, ky, kc = (k >> 2) & 1, (k >> 1) & 1, k & 1
            peer = (x ^ kx, y ^ ky, c ^ kc)
            cp = pltpu.make_async_remote_copy(
                src_ref=mine, dst_ref=all_ref.at[me], send_sem=send_sems.at[k - 1], recv_sem=recv_sems.at[k - 1],
                device_id=peer, device_id_type=MESH)
            cp.start()
            sends.append(cp)
        for k in range(1, 8):
            kx, ky, kc = (k >> 2) & 1, (k >> 1) & 1, k & 1
            src = 4 * (x ^ kx) + 2 * (y ^ ky) + (c ^ kc)
            pltpu.make_async_remote_copy(
                src_ref=mine, dst_ref=all_ref.at[src], send_sem=send_sems.at[k - 1], recv_sem=recv_sems.at[k - 1],
                device_id=(x, y, c), device_id_type=MESH).wait_recv()
        for cp in sends:
            cp.wait_send()
        total = all_ref[0]
        for d in range(1, 8):
            total = total + all_ref[d]
        tot_ref[...] = total
        for k in range(n + 2):
            for i, col, wd, row in pieces(k):
                outs[k][i:i + 1, col:col + wd] = tot_ref[row:row + 1, 0:wd]

    vm = pl.BlockSpec(memory_space=pltpu.VMEM)
    buf = pltpu.VMEM((total_rows, SMALL_COLS), F32)
    return pl.pallas_call(
        body, name="allreduce_small", out_shape=[jax.ShapeDtypeStruct(s, F32) for s in shapes],
        in_specs=[vm] * (n + 2), out_specs=[vm] * (n + 2),
        scratch_shapes=[buf, pltpu.VMEM((8, total_rows, SMALL_COLS), F32), buf,
                        pltpu.SemaphoreType.DMA((7,)), pltpu.SemaphoreType.DMA((7,))],
    )(*arrays, sink_rows, loss)


BIG_NAMES = ("w_in", "w_out", "w_up", "w_down")
MATRIX_NAMES = BIG_NAMES + ("gla_gate_up", "conv_w")
LATE_NAMES = ("w_out", "w_up", "w_down")
RS_LABELS = BIG_NAMES
GATE_SHARD = (16, GLA_QK // N_SHARD)
CONVW_SHARD = (3, SHARD_FF)
SMALL_W_ROWS = 96
PIECE_ROWS = (128, 128, 64, 88)
ADD_ROWS = (256, 128, 256, 176)
FIRST_UNIT_ROWS = (256, SMALL_W_ROWS // 2)


def _pad_rows(flat, rows):
    return jnp.pad(flat, (0, rows * LANES - flat.shape[0])).reshape(rows, LANES)


def _pack_small_weights(gate_up, conv_w):
    bits = lax.bitcast_convert_type(conv_w, BF16)
    return _pad_rows(jnp.concatenate([gate_up.astype(BF16).reshape(-1), bits.reshape(-1)]), SMALL_W_ROWS)


def _unpack_small_weights(packed):
    flat = packed.reshape(N_SHARD, -1)
    n_gate = GATE_SHARD[0] * GATE_SHARD[1]
    n_conv = 2 * CONVW_SHARD[0] * CONVW_SHARD[1]
    gate = flat[:, :n_gate].reshape((N_SHARD,) + GATE_SHARD)
    conv = lax.bitcast_convert_type(flat[:, n_gate:n_gate + n_conv].reshape((N_SHARD,) + CONVW_SHARD + (2,)), F32)
    return (jnp.transpose(gate, (1, 0, 2)).reshape(16, GLA_QK), jnp.transpose(conv, (1, 0, 2)).reshape(3, 2 * D_FF))


def _permute_w_in(w):
    pad = jnp.zeros((w.shape[0], IN_WIDTH_PAD - IN_WIDTH), w.dtype)
    return jnp.concatenate([w[:, 0:1024], w[:, 1040:2320], w[:, 1024:1040], pad], axis=1)


def _unpermute_w_in(wp):
    return jnp.concatenate([wp[:, 0:1024], wp[:, 2304:2320], wp[:, 1024:2304]], axis=1)


def _rope_tables(positions):
    half = ROPE_DIM // 2
    inv_freq = ROPE_THETA ** (-jnp.arange(half, dtype=F32) * (2.0 / ROPE_DIM))
    d = jnp.arange(LANES) % SWA_HD
    freq = jnp.where(d < ROPE_DIM, inv_freq[d % half], 0.0)
    ang = positions.astype(F32)[:, None] * freq
    cos, sin = jnp.cos(ang), jnp.sin(ang)
    return cos, jnp.where(d < half, -sin, 0.0), jnp.where((d >= half) & (d < ROPE_DIM), sin, 0.0)


SMALL_NAMES = (("pre_mix_norm", 1024), ("gla_gate_bias", 256), ("gla_out_norm", 128), ("swa_sinks", 8),
               ("post_mix_norm", 1024), ("pre_ffn_norm", 1024), ("conv_b", 5632), ("post_ffn_norm", 1024))


def _local_step(x, positions, target, w, small, late_weights, early_grads):
    rc, rsa, rsb = _rope_tables(positions)
    wp = w["wp"]
    gup = jnp.pad(w["gla_gate_up"], ((0, LANES - 16), (0, 0)))
    g1, g2, g3, g4 = (small[n] for n in ("pre_mix_norm", "post_mix_norm", "pre_ffn_norm", "post_ffn_norm"))
    gbias, gnorm, cb = small["gla_gate_bias"], small["gla_out_norm"], small["conv_b"]
    sinks = small["swa_sinks"].reshape(-1)
    cw = w["conv_w"]

    h1, q, k, v, la, gg, sq, kd, vd, glr = _proj_fwd(x, g1, wp, gup, gbias, rc, rsa, rsb)
    og, s_all = _gla_fwd(q, k, v, la)
    osw = _swa_fwd(sq, kd, vd, sinks)
    w_out, w_up4, w_down = late_weights(osw)
    x1, cat = _mix_out_fwd(x, og, gg, osw, gnorm, w_out, g2)
    h2, up, act, c1, c2, y, dx2, loss = _ffn_fwd(x1, g3, w_up4, cw, cb, w_down, g4, target)

    dy, dup, dx1, dg4, dg3, dcb, dcw = _ffn_bwd(dx2, y, g4, up, c1, c2, cw, w_down, w_up4, x1, g3)
    zero = early_grads(_matmul_tn(h2, dup, SHARD_FF, "grad_w_up", column_blocks_major=True),
                       _matmul_tn(act, dy, D_MODEL, "grad_w_down"))
    dmix, dog, dgg, dosw, dg2, dgn = _mix_out_bwd(dx1, cat, g2 + zero, w_out, og, gg, gnorm)
    dsq, dkd, dvd, dsink = _swa_bwd(sq, kd, vd, sinks, dosw)
    dq, dk, dv, dz = _gla_bwd(q, k, v, la, s_all, dog)
    dx, dproj, dg1, dgup, dgb = _proj_bwd(x, g1, wp, gup, glr, dq, dk, dv, dgg, dsq, dkd, dvd, dz, rc, rsa, rsb, dx1)

    grads = {
        "wp": _matmul_tn(h1, dproj, IN_WIDTH_PAD, "grad_w_in"),
        "w_out": _matmul_tn(cat, dmix, D_MODEL, "grad_w_out"),
        "gla_gate_up": dgup[0:16],
        "conv_w": dcw,
    }
    small_grads = {
        "pre_mix_norm": dg1, "gla_gate_bias": dgb, "gla_out_norm": dgn,
        "post_mix_norm": dg2, "pre_ffn_norm": dg3, "conv_b": dcb, "post_ffn_norm": dg4,
    }
    return loss, dx, grads, small_grads, dsink


ADAM_ROWS = {"w_in": 256, "w_out": 256, "w_up": 256, "w_down": 176}
WEIGHT_ORDER = ("pre_mix_norm", "w_in", "gla_gate_up", "gla_gate_bias", "gla_out_norm", "swa_sinks", "w_out",
                "post_mix_norm", "pre_ffn_norm", "w_up", "conv_w", "conv_b", "w_down", "post_ffn_norm")
def kernel(x, positions, pre_mix_norm, w_in, gla_gate_up, gla_gate_bias, gla_out_norm, swa_sinks, w_out, post_mix_norm, pre_ffn_norm, w_up, conv_w, conv_b, w_down, post_ffn_norm, loss_target, m_pre_mix_norm, m_w_in, m_gla_gate_up, m_gla_gate_bias, m_gla_out_norm, m_swa_sinks, m_w_out, m_post_mix_norm, m_pre_ffn_norm, m_w_up, m_conv_w, m_conv_b, m_w_down, m_post_ffn_norm, v_pre_mix_norm, v_w_in, v_gla_gate_up, v_gla_gate_bias, v_gla_out_norm, v_swa_sinks, v_w_out, v_post_mix_norm, v_pre_ffn_norm, v_w_up, v_conv_w, v_conv_b, v_w_down, v_post_ffn_norm):
    weights = dict(pre_mix_norm=pre_mix_norm, w_in=w_in, gla_gate_up=gla_gate_up, gla_gate_bias=gla_gate_bias,
                   gla_out_norm=gla_out_norm, swa_sinks=swa_sinks, w_out=w_out, post_mix_norm=post_mix_norm,
                   pre_ffn_norm=pre_ffn_norm, w_up=w_up, conv_w=conv_w, conv_b=conv_b, w_down=w_down,
                   post_ffn_norm=post_ffn_norm)
    mom = dict(pre_mix_norm=m_pre_mix_norm, w_in=m_w_in, gla_gate_up=m_gla_gate_up, gla_gate_bias=m_gla_gate_bias,
               gla_out_norm=m_gla_out_norm, swa_sinks=m_swa_sinks, w_out=m_w_out, post_mix_norm=m_post_mix_norm,
               pre_ffn_norm=m_pre_ffn_norm, w_up=m_w_up, conv_w=m_conv_w, conv_b=m_conv_b, w_down=m_w_down,
               post_ffn_norm=m_post_ffn_norm)
    var = dict(pre_mix_norm=v_pre_mix_norm, w_in=v_w_in, gla_gate_up=v_gla_gate_up, gla_gate_bias=v_gla_gate_bias,
               gla_out_norm=v_gla_out_norm, swa_sinks=v_swa_sinks, w_out=v_w_out, post_mix_norm=v_post_mix_norm,
               pre_ffn_norm=v_pre_ffn_norm, w_up=v_w_up, conv_w=v_conv_w, conv_b=v_conv_b, w_down=v_w_down,
               post_ffn_norm=v_post_ffn_norm)
    weights, mom, var = ({n: a[0] if a.ndim == 3 else a for n, a in d.items()} for d in (weights, mom, var))

    core = lax.axis_index("c").astype(jnp.int32).reshape(1)
    chip = (2 * lax.axis_index("x") + lax.axis_index("y")).astype(jnp.int32).reshape(1)
    small = {n: weights[n] for n, _ in SMALL_NAMES}

    win4, small4 = _allgather_shards(
        [weights["w_in"].astype(BF16), _pack_small_weights(weights["gla_gate_up"], weights["conv_w"])], FIRST_UNIT_ROWS)
    gate_full, convw_full = _unpack_small_weights(small4)
    first = {"wp": _permute_w_in(jnp.transpose(win4, (1, 0, 2)).reshape(D_MODEL, IN_WIDTH)),
             "gla_gate_up": gate_full, "conv_w": convw_full}
    late_shards = [weights[n].astype(BF16) for n in LATE_NAMES]
    gathering = _ici_start(late_shards, [lax.empty((N_SHARD,) + s.shape, BF16) for s in late_shards], _gather_plan, "gather_start")
    small["pre_mix_norm"] = small["pre_mix_norm"] + gathering[-1][0, 0]

    def late_weights(after):
        wout4, wup4, wdown4 = _gather_finish(late_shards, _ici_wait(gathering, after, _gather_plan, "gather_wait"))
        return wout4.reshape(D_MODEL, D_MODEL), wup4, wdown4.reshape(D_FF, D_MODEL)

    def pair_partials(arrs, which):
        got = _rs_pair_swap(arrs, [PIECE_ROWS[i] for i in which], "rs_pair_swap_" + RS_LABELS[which[0]])
        return [_rs_add_pair(a, g, core, ADD_ROWS[i], "rs_add_pair_" + RS_LABELS[i]) for a, g, i in zip(arrs, got, which)]

    early = {}
    delta, new_m, new_v = {}, {}, {}

    def scatter_start(arrs, which, name):
        parts = pair_partials(arrs, which)
        lands = [_place_own(p, chip, ADD_ROWS[i], "rs_place_own_" + RS_LABELS[i]) for p, i in zip(parts, which)]
        return _ici_start(parts, lands, _scatter_plan, name)

    def finish(landed, which, grads_out):
        names = [BIG_NAMES[i] for i in which]
        halves = [_rs_sum_chips(p, ADD_ROWS[i], "rs_sum_chips_" + BIG_NAMES[i]) for p, i in zip(landed, which)]
        reduced = _rs_pair_share(halves, [PIECE_ROWS[i] for i in which], "rs_pair_share_" + names[0])
        for n, g in zip(names, reduced):
            delta[n], new_m[n], new_v[n], grads_out[n] = _adamw(weights[n], g, mom[n], var[n], ADAM_ROWS[n], "adamw_" + n)

    def early_grads(g_up4, g_down):
        early["scatter"] = scatter_start([g_up4, g_down.reshape(N_SHARD, D_FF // N_SHARD, D_MODEL)], (2, 3), "rs_scatter_start")
        return early["scatter"][-1][0, 0]

    loss, dx, grads, small_grads, sink_rows = _local_step(
        x[0], positions[0], loss_target[0], first, small, late_weights, early_grads)

    rest = [jnp.transpose(_unpermute_w_in(grads["wp"]).reshape(D_MODEL, N_SHARD, IN_WIDTH // N_SHARD), (1, 0, 2)),
            grads["w_out"].reshape(N_SHARD, D_MODEL // N_SHARD, D_MODEL)]
    late_scatter = scatter_start(rest, (0, 1), "rs_scatter_start_rest")
    g_all = {}
    finish(_ici_wait(early["scatter"], late_scatter[-1], _scatter_plan, "rs_scatter_wait"), (2, 3), g_all)
    finish(_ici_wait(late_scatter, delta["w_down"], _scatter_plan, "rs_scatter_wait_rest"), (0, 1), g_all)
    vectors = [n for n, _ in SMALL_NAMES if n != "swa_sinks"]
    summed = _allreduce_small([small_grads[n] for n in vectors] + [grads["gla_gate_up"], grads["conv_w"]], sink_rows, loss)
    g_all.update(zip(vectors, summed))
    g_all.update({"swa_sinks": summed[-2],
                  "gla_gate_up": lax.dynamic_slice_in_dim(summed[-4], chip[0] * GATE_SHARD[1], GATE_SHARD[1], axis=1),
                  "conv_w": lax.dynamic_slice_in_dim(summed[-3], chip[0] * SHARD_FF, SHARD_FF, axis=1)})
    loss_sum = summed[-1][0, 0]

    tiny = [n for n, _ in SMALL_NAMES] + ["gla_gate_up", "conv_w"]
    for res, vals in zip((delta, new_m, new_v, g_all), _adamw_small(*([d[n] for n in tiny] for d in (weights, g_all, mom, var)))):
        res.update(zip(tiny, vals))

    def lead(n, a):
        return a[None] if n in MATRIX_NAMES else a

    outs = [loss_sum, dx[None]]
    for d in (g_all, delta, new_m, new_v):
        outs.extend(lead(n, d[n]) for n in WEIGHT_ORDER)
    return tuple(outs)
```

```python
import functools

import jax
import jax.numpy as jnp
from jax import lax
from jax.experimental import pallas as pl
from jax.experimental.pallas import tpu as pltpu

F32 = jnp.float32
BF16 = jnp.bfloat16
MESH = pl.DeviceIdType.MESH

D_MODEL = 1024
GLA_HEADS = 4
GLA_DK = 64
GLA_DV = 128
GLA_TAU = 16.0
GLA_CHUNK = 64
SWA_HEADS = 8
SWA_HD = 64
SWA_BLOCK = 128
ROPE_THETA = 500000.0
ROPE_DIM = 16
D_FF = 2816
EPS = 1e-6
GLA_QK = 256
GLA_V = 512
SWA_Q = 512
SWA_KV = 128
IN_WIDTH = 2320
IN_WIDTH_PAD = 2432
N_SHARD = 4

ADAM_LR = 0.001
ADAM_B1 = 0.9
ADAM_B2 = 0.999
ADAM_EPS = 1e-08
ADAM_WD = 0.01
ADAM_STEP = 10

LANES = 128
VMEM_LIMIT = 56 * 1024 * 1024
TM = 256
SHARD_FF = 2 * D_FF // N_SHARD
FF_PIECES = ((0, 1408),)
GLA_BLOCK = 256


def _wide_tile(T):
    return 2 * TM if T % (2 * TM) == 0 else TM


def _params(**kw):
    return pltpu.CompilerParams(vmem_limit_bytes=VMEM_LIMIT, **kw)


def _mm(a, b):
    return lax.dot_general(a.astype(BF16), b.astype(BF16), (((1,), (0,)), ((), ())), preferred_element_type=F32)


def _mm_nt(a, b):
    return lax.dot_general(a.astype(BF16), b.astype(BF16), (((1,), (1,)), ((), ())), preferred_element_type=F32)


def _mm_tn(a, b):
    return lax.dot_general(a.astype(BF16), b.astype(BF16), (((0,), (0,)), ((), ())), preferred_element_type=F32)


def _mm_f32(a, b):
    return lax.dot_general(a, b, (((1,), (0,)), ((), ())), preferred_element_type=F32, precision=lax.Precision.HIGHEST)


def _iota(shape, dim):
    return lax.broadcasted_iota(jnp.int32, shape, dim)


def _sigmoid(x):
    return 1.0 / (1.0 + jnp.exp(-x))


def _gelu_parts(x):
    c = 0.7978845608028654
    x2 = x * x
    t = jnp.tanh(c * (x + 0.044715 * (x2 * x)))
    cdf = 0.5 * (1.0 + t)
    dcdf = 0.5 * (1.0 - t * t) * c * (1.0 + 3.0 * 0.044715 * x2)
    return x * cdf, cdf + x * dcdf


def _rms_bwd(v, r, g, dout):
    gd = g * dout
    return r * gd - v * (r * r * r) * jnp.mean(v * gd, axis=-1, keepdims=True)


def _row_spec(tm, cols):
    return pl.BlockSpec((tm, cols), lambda i: (i, 0))


def _const_spec(shape):
    return pl.BlockSpec(shape, lambda i: (0,) * len(shape))


def _any_spec():
    return pl.BlockSpec(memory_space=pl.ANY)


def _load_once(src_hbm, dst_vmem, sem):
    @pl.when(pl.program_id(0) == 0)
    def _():
        cp = pltpu.make_async_copy(src_hbm, dst_vmem, sem)
        cp.start()
        cp.wait()


def _rotate(v, rc, rsa, rsb):
    return v * rc + pltpu.roll(v, 120, 1) * rsa + pltpu.roll(v, 8, 1) * rsb


def _rotate_bwd(dv, rc, rsa, rsb):
    return dv * rc + pltpu.roll(dv * rsa, 8, 1) + pltpu.roll(dv * rsb, 120, 1)


def _proj_fwd(x, g1, wp, gup, gbias, rc, rsa, rsb):
    T = x.shape[0]
    TM = _wide_tile(T)

    def body(x_ref, g1_ref, wp_hbm, gup_ref, gb_ref, rc_ref, rsa_ref, rsb_ref,
             h1_ref, q_ref, k_ref, v_ref, la_ref, gg_ref, sq_ref, kd_ref, vd_ref, glr_ref, wp_v, sem):
        _load_once(wp_hbm, wp_v, sem)
        xt = x_ref[...]
        r = lax.rsqrt(jnp.mean(xt * xt, axis=-1, keepdims=True) + EPS)
        h = (xt * r * g1_ref[...]).astype(BF16)
        h1_ref[...] = h
        q_ref[...] = _mm(h, wp_v[:, 0:256])
        k_ref[...] = _mm(h, wp_v[:, 256:512])
        v_ref[...] = _mm(h, wp_v[:, 512:1024]).astype(BF16)
        gg_ref[...] = _mm(h, wp_v[:, 1024:1536]).astype(BF16)
        glr = _mm(h, wp_v[:, 2304:2432]).astype(BF16)
        glr_ref[...] = glr
        z = _mm(glr, gup_ref[...]) + gb_ref[...]
        la_ref[...] = (jnp.minimum(z, 0.0) - jnp.log1p(jnp.exp(-jnp.abs(z)))) * (1.0 / GLA_TAU)
        rc_, rsa_, rsb_ = rc_ref[...], rsa_ref[...], rsb_ref[...]
        for s in range(4):
            qs = _mm(h, wp_v[:, 1536 + 128 * s:1664 + 128 * s])
            sq_ref[:, 128 * s:128 * s + 128] = (_rotate(qs, rc_, rsa_, rsb_) * 0.125).astype(BF16)
        lane = _iota((TM, LANES), 1)
        first = lane < 64
        kr = _rotate(_mm(h, wp_v[:, 2048:2176]), rc_, rsa_, rsb_)
        krr = pltpu.roll(kr, 64, 1)
        kd_ref[:, 0:128] = jnp.where(first, kr, krr).astype(BF16)
        kd_ref[:, 128:256] = jnp.where(first, krr, kr).astype(BF16)
        vr = _mm(h, wp_v[:, 2176:2304])
        vrr = pltpu.roll(vr, 64, 1)
        vd_ref[:, 0:128] = jnp.where(first, vr, vrr).astype(BF16)
        vd_ref[:, 128:256] = jnp.where(first, vrr, vr).astype(BF16)

    outs = [
        jax.ShapeDtypeStruct((T, D_MODEL), BF16),
        jax.ShapeDtypeStruct((T, GLA_QK), F32),
        jax.ShapeDtypeStruct((T, GLA_QK), F32),
        jax.ShapeDtypeStruct((T, GLA_V), BF16),
        jax.ShapeDtypeStruct((T, GLA_QK), F32),
        jax.ShapeDtypeStruct((T, GLA_V), BF16),
        jax.ShapeDtypeStruct((T, SWA_Q), BF16),
        jax.ShapeDtypeStruct((T, 256), BF16),
        jax.ShapeDtypeStruct((T, 256), BF16),
        jax.ShapeDtypeStruct((T, LANES), BF16),
    ]
    return pl.pallas_call(
        body, name="proj_fwd", grid=(T // TM,), out_shape=outs,
        in_specs=[_row_spec(TM, D_MODEL), _const_spec((1, D_MODEL)), _any_spec(), _const_spec((LANES, GLA_QK)),
                  _const_spec((1, GLA_QK)), _row_spec(TM, LANES), _row_spec(TM, LANES), _row_spec(TM, LANES)],
        out_specs=[_row_spec(TM, o.shape[1]) for o in outs],
        scratch_shapes=[pltpu.VMEM((D_MODEL, IN_WIDTH_PAD), BF16), pltpu.SemaphoreType.DMA],
        compiler_params=_params(),
    )(x, g1, wp, gup, gbias, rc, rsa, rsb)


GLA_NB = GLA_BLOCK // GLA_CHUNK


def _gla_masks():
    n = GLA_BLOCK
    lane = _iota((n, LANES), 1)
    lane_masks = [(lane < 64).astype(F32), (lane >= 64).astype(F32)]
    row, col = _iota((n, n), 0), _iota((n, n), 1)
    same_chunk = (row >> 6) == (col >> 6)
    blk = ((_iota((256, LANES), 0) >> 7) == (_iota((256, LANES), 1) >> 6)).astype(F32)
    return lane_masks, same_chunk & (col <= row), same_chunk & (col >= row), blk


def _chunk_rows(vals):
    return jnp.concatenate([jnp.broadcast_to(v, (GLA_CHUNK, LANES)) for v in vals], axis=0)


def _gla_block_terms(q_ref, k_ref, b_ref, p):
    C = GLA_CHUNK
    cols = slice(LANES * p, LANES * p + LANES)
    bc = b_ref[:, cols]
    bl_rows = [b_ref[C * c + C - 1:C * c + C, cols] for c in range(GLA_NB)]
    bl = _chunk_rows(bl_rows)
    bm = _chunk_rows([b_ref[C * c + C // 2 - 1:C * c + C // 2, cols] for c in range(GLA_NB)])
    qs = q_ref[:, cols] * 0.125
    kk = k_ref[:, cols]
    eb = jnp.exp(bc)
    ekl = jnp.exp(bl - bc)
    eqm = jnp.exp(bc - bm)
    ekm = jnp.exp(bm - bc)
    return qs, kk, eb, ekl, eqm, ekm, [jnp.exp(r) for r in bl_rows]


def _block_cumsum(la, mask):
    return _mm_f32(mask.astype(F32), la)


def _gla_fwd(q, k, v, la):
    T = q.shape[0]
    NB = GLA_BLOCK // GLA_CHUNK
    C = GLA_CHUNK

    def body(q_ref, k_ref, v_ref, la_ref, o_ref, s_ref, st_ref, b_ref):
        @pl.when(pl.program_id(0) == 0)
        def _():
            st_ref[...] = jnp.zeros_like(st_ref)

        lane_masks, causal, _, blk = _gla_masks()
        b_ref[...] = _block_cumsum(la_ref[...], causal)
        for p in range(2):
            qs, kk, eb, ekl, eqm, ekm, gam = _gla_block_terms(q_ref, k_ref, b_ref, p)
            qh, kh, qm, km = qs * eb, kk * ekl, qs * eqm, kk * ekm
            vp = v_ref[:, 256 * p:256 * p + 256]
            intra = []
            for j in range(2):
                a = jnp.where(causal, _mm_nt(qm * lane_masks[j], km), 0.0)
                intra.append(_mm(a, vp[:, 128 * j:128 * j + 128]))
            kv = [blk * _mm_tn(vp[C * c:C * c + C], kh[C * c:C * c + C]) for c in range(NB)]
            st = st_ref[p]
            inter = []
            for c in range(NB):
                s_ref[c, p] = st[0:LANES] + st[LANES:2 * LANES]
                inter.append(_mm_nt(qh[C * c:C * c + C], st))
                st = st * gam[c] + kv[c]
            st_ref[p] = st
            o_ref[:, 256 * p:256 * p + 256] = (jnp.concatenate(inter, axis=0) + jnp.concatenate(intra, axis=1)).astype(BF16)

    return pl.pallas_call(
        body, name="gla_fwd", grid=(T // GLA_BLOCK,),
        out_shape=[jax.ShapeDtypeStruct((T, GLA_V), BF16), jax.ShapeDtypeStruct((T // C, 2, LANES, LANES), F32)],
        in_specs=[_row_spec(GLA_BLOCK, GLA_QK), _row_spec(GLA_BLOCK, GLA_QK), _row_spec(GLA_BLOCK, GLA_V),
                  _row_spec(GLA_BLOCK, GLA_QK)],
        out_specs=[_row_spec(GLA_BLOCK, GLA_V), pl.BlockSpec((NB, 2, LANES, LANES), lambda i: (i, 0, 0, 0))],
        scratch_shapes=[pltpu.VMEM((2, 256, LANES), F32), pltpu.VMEM((GLA_BLOCK, GLA_QK), F32)],
        compiler_params=_params(),
    )(q, k, v, la)


def _gla_bwd(q, k, v, la, s_all, do):
    T = q.shape[0]
    NB = GLA_BLOCK // GLA_CHUNK
    C = GLA_CHUNK
    nblk = T // GLA_BLOCK

    def body(q_ref, k_ref, v_ref, la_ref, s_ref, do_ref, dq_ref, dk_ref, dv_ref, dz_ref, dst_ref, b_ref):
        @pl.when(pl.program_id(0) == 0)
        def _():
            dst_ref[...] = jnp.zeros_like(dst_ref)

        lane_masks, causal, anti_causal, blk = _gla_masks()
        b_ref[...] = _block_cumsum(la_ref[...], causal)
        for p in range(2):
            cols = slice(LANES * p, LANES * p + LANES)
            qs, kk, eb, ekl, eqm, ekm, gam = _gla_block_terms(q_ref, k_ref, b_ref, p)
            qh, kh, qm, km = qs * eb, kk * ekl, qs * eqm, kk * ekm
            vp = v_ref[:, 256 * p:256 * p + 256]
            dop = do_ref[:, 256 * p:256 * p + 256]
            dqm = jnp.zeros((GLA_BLOCK, LANES), F32)
            dkm = jnp.zeros((GLA_BLOCK, LANES), F32)
            dv_intra = []
            for j in range(2):
                hs = slice(128 * j, 128 * j + 128)
                a = jnp.where(causal, _mm_nt(qm * lane_masks[j], km), 0.0)
                da = jnp.where(causal, _mm_nt(dop[:, hs], vp[:, hs]), 0.0)
                dv_intra.append(_mm_tn(a, dop[:, hs]))
                dqm = dqm + lane_masks[j] * _mm(da, km)
                dkm = dkm + lane_masks[j] * _mm_tn(da, qm)
            grow = [blk * _mm_tn(dop[C * c:C * c + C], qh[C * c:C * c + C]) for c in range(NB)]
            dst = dst_ref[p]
            dst_after = [None] * NB
            for c in reversed(range(NB)):
                dst_after[c] = dst
                dst = dst * gam[c] + grow[c]
            dst_ref[p] = dst
            dqh, dkh, dv_state, extra = [], [], [], []
            for c in range(NB):
                rows = slice(C * c, C * c + C)
                packed = s_ref[c, p]
                st = jnp.concatenate([packed * lane_masks[0][0:LANES], packed * lane_masks[1][0:LANES]], axis=0)
                dqh.append(_mm(dop[rows], st))
                dkh.append(_mm(vp[rows], dst_after[c]))
                dv_state.append(_mm_nt(kh[rows], dst_after[c]))
                extra.append(jnp.sum(dkh[c] * kh[rows], axis=0, keepdims=True)
                             + jnp.sum(st * dst_after[c], axis=0, keepdims=True) * gam[c])
            dqs = jnp.concatenate(dqh, axis=0) * eb + dqm * eqm
            dk = jnp.concatenate(dkh, axis=0) * ekl + dkm * ekm
            dg = _mm_f32(anti_causal.astype(F32), dqs * qs - dk * kk) + _chunk_rows(extra)
            dq_ref[:, cols] = (dqs * 0.125).astype(BF16)
            dk_ref[:, cols] = dk.astype(BF16)
            dz_ref[:, cols] = dg * (1.0 - jnp.exp(GLA_TAU * la_ref[:, cols])) * (1.0 / GLA_TAU)
            dv_ref[:, 256 * p:256 * p + 256] = (jnp.concatenate(dv_state, axis=0) + jnp.concatenate(dv_intra, axis=1)).astype(BF16)

    rev = lambda i: (nblk - 1 - i, 0)
    rspec = lambda cols: pl.BlockSpec((GLA_BLOCK, cols), rev)
    return pl.pallas_call(
        body, name="gla_bwd", grid=(nblk,),
        out_shape=[jax.ShapeDtypeStruct((T, GLA_QK), BF16), jax.ShapeDtypeStruct((T, GLA_QK), BF16),
                   jax.ShapeDtypeStruct((T, GLA_V), BF16), jax.ShapeDtypeStruct((T, GLA_QK), F32)],
        in_specs=[rspec(GLA_QK), rspec(GLA_QK), rspec(GLA_V), rspec(GLA_QK),
                  pl.BlockSpec((NB, 2, LANES, LANES), lambda i: (nblk - 1 - i, 0, 0, 0)), rspec(GLA_V)],
        out_specs=[rspec(GLA_QK), rspec(GLA_QK), rspec(GLA_V), rspec(GLA_QK)],
        scratch_shapes=[pltpu.VMEM((2, 256, LANES), F32), pltpu.VMEM((GLA_BLOCK, GLA_QK), F32)],
        compiler_params=_params(),
    )(q, k, v, la, s_all, do)


SWA_GROUP = 4


def _swa_stack(ref, g, first):
    parts = []
    for j in range(SWA_GROUP):
        m = 2 * g + j // 2
        pair = ref[:, 128 * m:128 * m + 128]
        zero = jnp.zeros_like(pair)
        parts.append(jnp.where(first, pair, zero) if j % 2 == 0 else jnp.where(first, zero, pair))
    return jnp.concatenate(parts, axis=0)


def _swa_unstack(rows, mm, first):
    W = SWA_BLOCK
    return jnp.where(first, rows[W * 2 * mm:W * (2 * mm + 1)], rows[W * (2 * mm + 1):W * (2 * mm + 2)])


def _swa_probs(qs, kp, kc, vp, vc, i, g, sink_ref, first4):
    W = SWA_BLOCK
    R = SWA_GROUP * W
    r, c = _iota((R, W), 0) & (W - 1), _iota((R, W), 1)
    neg = -1e30
    s_p = jnp.where((c > r) & (i > 0), _mm_nt(qs, kp), neg)
    s_c = jnp.where(c <= r, _mm_nt(qs, kc), neg)
    head = _iota((R, 1), 0) >> 7
    sink = jnp.where(head == 0, sink_ref[4 * g], jnp.where(head == 1, sink_ref[4 * g + 1],
                                                           jnp.where(head == 2, sink_ref[4 * g + 2], sink_ref[4 * g + 3])))
    m = jnp.maximum(jnp.max(jnp.maximum(s_p, s_c), axis=-1, keepdims=True), sink)
    p_p = jnp.exp(s_p - m)
    p_c = jnp.exp(s_c - m)
    p_s = jnp.exp(sink - m)
    one = jnp.ones((W, LANES), BF16)
    first = _iota((W, LANES), 1) < 64
    acc = _mm(p_p, jnp.where(first, vp, one)) + _mm(p_c, jnp.where(first, vc, one))
    rolled = pltpu.roll(acc, 64, 1)
    denom = jnp.where(first4, rolled, acc) + p_s
    return p_p, p_c, p_s, denom, acc, rolled


def _swa_fwd(sq, kd, vd, sinks):
    T = sq.shape[0]
    W = SWA_BLOCK
    prev = lambda i: (jnp.maximum(i - 1, 0), 0)

    def body(sink_ref, q_ref, kp_ref, kc_ref, vp_ref, vc_ref, o_ref):
        i = pl.program_id(0)
        first4 = _iota((SWA_GROUP * W, LANES), 1) < 64
        first = _iota((W, LANES), 1) < 64
        for g in range(2):
            gs = slice(128 * g, 128 * g + 128)
            qs = _swa_stack(q_ref, g, first)
            _, _, _, denom, acc, rolled = _swa_probs(qs, kp_ref[:, gs], kc_ref[:, gs], vp_ref[:, gs], vc_ref[:, gs],
                                                     i, g, sink_ref, first4)
            pv = jnp.where(first4, acc, rolled)
            o = pv / denom
            for mm in range(2):
                m = 2 * g + mm
                o_ref[:, 128 * m:128 * m + 128] = _swa_unstack(o, mm, first).astype(BF16)

    return pl.pallas_call(
        body, name="swa_fwd", grid=(T // W,), out_shape=jax.ShapeDtypeStruct((T, SWA_Q), BF16),
        in_specs=[pl.BlockSpec(memory_space=pltpu.SMEM), _row_spec(W, SWA_Q), pl.BlockSpec((W, 256), prev),
                  _row_spec(W, 256), pl.BlockSpec((W, 256), prev), _row_spec(W, 256)],
        out_specs=_row_spec(W, SWA_Q),
        compiler_params=_params(),
    )(sinks, sq, kd, kd, vd, vd)


def _swa_bwd(sq, kd, vd, sinks, do):
    T = sq.shape[0]
    W = SWA_BLOCK
    n = T // W
    cur = lambda i: (jnp.minimum(i, n - 1), 0)
    prev = lambda i: (jnp.clip(i - 1, 0, n - 1), 0)

    def body(sink_ref, q_ref, kp_ref, kc_ref, vp_ref, vc_ref, do_ref, dq_ref, dk_ref, dv_ref, ds_ref, ck_ref, cv_ref):
        i = pl.program_id(0)

        @pl.when(i == 0)
        def _():
            ds_ref[...] = jnp.zeros_like(ds_ref)
            ck_ref[...] = jnp.zeros_like(ck_ref)
            cv_ref[...] = jnp.zeros_like(cv_ref)

        @pl.when(i < n)
        def _():
            first4 = _iota((SWA_GROUP * W, LANES), 1) < 64
            first = _iota((W, LANES), 1) < 64
            for g in range(2):
                gs = slice(128 * g, 128 * g + 128)
                kp, kc, vp, vc = kp_ref[:, gs], kc_ref[:, gs], vp_ref[:, gs], vc_ref[:, gs]
                qs = _swa_stack(q_ref, g, first)
                dos = _swa_stack(do_ref, g, first)
                p_p, p_c, p_s, denom, _, _ = _swa_probs(qs, kp, kc, vp, vc, i, g, sink_ref, first4)
                inv = 1.0 / denom
                p_p, p_c = p_p * inv, p_c * inv
                dp_p = _mm_nt(dos, vp)
                dp_c = _mm_nt(dos, vc)
                delta = jnp.sum(p_p * dp_p + p_c * dp_c, axis=-1, keepdims=True)
                ds_p = p_p * (dp_p - delta)
                ds_c = p_c * (dp_c - delta)
                rows = slice(SWA_GROUP * W * g, SWA_GROUP * W * (g + 1))
                ds_ref[rows, :] = ds_ref[rows, :] - (p_s * delta) * inv
                dq = (_mm(ds_p, kp) + _mm(ds_c, kc)) * 0.125
                for mm in range(2):
                    m = 2 * g + mm
                    dq_ref[:, 128 * m:128 * m + 128] = _swa_unstack(dq, mm, first).astype(BF16)
                dk_ref[:, gs] = (ck_ref[:, gs] + _mm_tn(ds_p, qs)).astype(BF16)
                dv_ref[:, gs] = (cv_ref[:, gs] + _mm_tn(p_p, dos)).astype(BF16)
                ck_ref[:, gs] = _mm_tn(ds_c, qs)
                cv_ref[:, gs] = _mm_tn(p_c, dos)

        @pl.when(i == n)
        def _():
            dk_ref[...] = ck_ref[...].astype(BF16)
            dv_ref[...] = cv_ref[...].astype(BF16)

    return pl.pallas_call(
        body, name="swa_bwd", grid=(n + 1,),
        out_shape=[jax.ShapeDtypeStruct((T, SWA_Q), BF16), jax.ShapeDtypeStruct((T, 256), BF16),
                   jax.ShapeDtypeStruct((T, 256), BF16), jax.ShapeDtypeStruct((SWA_HEADS * W, LANES), F32)],
        in_specs=[pl.BlockSpec(memory_space=pltpu.SMEM), pl.BlockSpec((W, SWA_Q), cur), pl.BlockSpec((W, 256), prev),
                  pl.BlockSpec((W, 256), cur), pl.BlockSpec((W, 256), prev), pl.BlockSpec((W, 256), cur),
                  pl.BlockSpec((W, SWA_Q), cur)],
        out_specs=[pl.BlockSpec((W, SWA_Q), cur), pl.BlockSpec((W, 256), prev), pl.BlockSpec((W, 256), prev),
                   _const_spec((SWA_HEADS * W, LANES))],
        scratch_shapes=[pltpu.VMEM((W, 256), F32), pltpu.VMEM((W, 256), F32)],
        compiler_params=_params(),
    )(sinks, sq, kd, kd, vd, vd, do)


def _mix_out_fwd(x, og, gg, osw, gnorm, wout, g2):
    T = x.shape[0]
    TM = _wide_tile(T)

    def body(x_ref, og_ref, gg_ref, osw_ref, gn_ref, wout_ref, g2_ref, x1_ref, cat_ref):
        gn = gn_ref[...]
        for j in range(GLA_HEADS):
            hs = slice(128 * j, 128 * j + 128)
            o = og_ref[:, hs].astype(F32)
            r = lax.rsqrt(jnp.mean(o * o, axis=-1, keepdims=True) + EPS)
            gate = gg_ref[:, hs].astype(F32)
            cat_ref[:, hs] = (o * r * gn * (gate * _sigmoid(gate))).astype(BF16)
        cat_ref[:, GLA_V:] = osw_ref[...]
        mix = _mm(cat_ref[...], wout_ref[...])
        r2 = lax.rsqrt(jnp.mean(mix * mix, axis=-1, keepdims=True) + EPS)
        x1_ref[...] = x_ref[...] + mix * r2 * g2_ref[...]

    return pl.pallas_call(
        body, name="mix_out_fwd", grid=(T // TM,),
        out_shape=[jax.ShapeDtypeStruct((T, D_MODEL), F32), jax.ShapeDtypeStruct((T, D_MODEL), BF16)],
        in_specs=[_row_spec(TM, D_MODEL), _row_spec(TM, GLA_V), _row_spec(TM, GLA_V), _row_spec(TM, SWA_Q),
                  _const_spec((1, LANES)), _const_spec((D_MODEL, D_MODEL)), _const_spec((1, D_MODEL))],
        out_specs=[_row_spec(TM, D_MODEL), _row_spec(TM, D_MODEL)],
        compiler_params=_params(),
    )(x, og, gg, osw, gnorm, wout, g2)


HALO = 8


def _rows_before(v, prev1, prev2):
    row = _iota(v.shape, 0)
    m1 = jnp.where(row == 0, prev1, pltpu.roll(v, 1, 0))
    m2 = jnp.where(row == 0, prev2, jnp.where(row == 1, prev1, pltpu.roll(v, 2, 0)))
    return m1, m2


def _rows_after(v, next1, next2):
    n = v.shape[0]
    row = _iota(v.shape, 0)
    p1 = jnp.where(row == n - 1, next1, pltpu.roll(v, n - 1, 0))
    p2 = jnp.where(row == n - 1, next2, jnp.where(row == n - 2, next1, pltpu.roll(v, n - 2, 0)))
    return p1, p2


def _ff_pieces():
    return [(j, off, wd) for j in range(2) for off, wd in FF_PIECES]


def _ffn_fwd(x1, g3, wup, cw, cb, wdown, g4, target):
    T = x1.shape[0]

    def body(x1_ref, g3_ref, wup_hbm, cw_ref, cb_ref, wdn_hbm, g4_ref, tg_ref,
             h2_ref, up_ref, a_ref, c1_ref, c2_ref, y_ref, dx2_ref, loss_ref, wup_v, wdn_v, carry_ref, sems):
        _load_once(wup_hbm, wup_v, sems.at[0])
        _load_once(wdn_hbm, wdn_v, sems.at[1])

        @pl.when(pl.program_id(0) == 0)
        def _():
            carry_ref[...] = jnp.zeros_like(carry_ref)
            loss_ref[...] = jnp.zeros_like(loss_ref)

        x1 = x1_ref[...]
        r3 = lax.rsqrt(jnp.mean(x1 * x1, axis=-1, keepdims=True) + EPS)
        h2 = (x1 * r3 * g3_ref[...]).astype(BF16)
        h2_ref[...] = h2
        for j, off, wd in _ff_pieces():
            base = SHARD_FF * j + off
            u = []
            for half in range(2):
                cs = slice(D_FF * half + base, D_FF * half + base + wd)
                upb = _mm(h2, wup_v[2 * half + j, :, off:off + wd]).astype(BF16)
                up_ref[:, cs] = upb
                upf = upb.astype(F32)
                m1, m2 = _rows_before(upf, carry_ref[HALO - 1:HALO, cs], carry_ref[HALO - 2:HALO - 1, cs])
                u.append(cb_ref[:, cs] + cw_ref[0:1, cs] * m2 + cw_ref[1:2, cs] * m1 + cw_ref[2:3, cs] * upf)
                carry_ref[:, cs] = upf[TM - HALO:TM, :]
            act, dact = _gelu_parts(u[1])
            a = (act * u[0]).astype(BF16)
            out = slice(base, base + wd)
            a_ref[:, out] = a
            c1_ref[:, out] = act.astype(BF16)
            c2_ref[:, out] = (u[0] * dact).astype(BF16)
        y = _mm(a_ref[...], wdn_v[...])
        y_ref[...] = y
        r4 = lax.rsqrt(jnp.mean(y * y, axis=-1, keepdims=True) + EPS)
        err = x1 + y * r4 * g4_ref[...] - tg_ref[...]
        dx2_ref[...] = err * (1.0 / D_MODEL)
        loss_ref[...] = loss_ref[...] + jnp.sum(err * err) * (0.5 / D_MODEL)

    outs = [
        jax.ShapeDtypeStruct((T, D_MODEL), BF16),
        jax.ShapeDtypeStruct((T, 2 * D_FF), BF16),
        jax.ShapeDtypeStruct((T, D_FF), BF16),
        jax.ShapeDtypeStruct((T, D_FF), BF16),
        jax.ShapeDtypeStruct((T, D_FF), BF16),
        jax.ShapeDtypeStruct((T, D_MODEL), F32),
        jax.ShapeDtypeStruct((T, D_MODEL), F32),
        jax.ShapeDtypeStruct((8, LANES), F32),
    ]
    return pl.pallas_call(
        body, name="ffn_fwd", grid=(T // TM,), out_shape=outs,
        in_specs=[_row_spec(TM, D_MODEL), _const_spec((1, D_MODEL)), _any_spec(), _const_spec((3, 2 * D_FF)),
                  _const_spec((1, 2 * D_FF)), _any_spec(), _const_spec((1, D_MODEL)), _row_spec(TM, D_MODEL)],
        out_specs=[_row_spec(TM, D_MODEL), _row_spec(TM, 2 * D_FF), _row_spec(TM, D_FF), _row_spec(TM, D_FF),
                   _row_spec(TM, D_FF), _row_spec(TM, D_MODEL), _row_spec(TM, D_MODEL), _const_spec((8, LANES))],
        scratch_shapes=[pltpu.VMEM((N_SHARD, D_MODEL, SHARD_FF), BF16), pltpu.VMEM((D_FF, D_MODEL), BF16),
                        pltpu.VMEM((HALO, 2 * D_FF), F32), pltpu.SemaphoreType.DMA((2,))],
        compiler_params=_params(),
    )(x1, g3, wup, cw, cb, wdown, g4, target)


def _ffn_bwd(dx2, y, g4, up, c1, c2, cw, wdown, wup, x1, g3):
    T = dx2.shape[0]
    nt = T // TM
    rev = lambda i: (nt - 1 - i, 0)

    def body(dn_ref, y_ref, g4_ref, up_ref, c1_ref, c2_ref, cw_ref, wdn_hbm, wup_hbm, x1_ref, g3_ref,
             dy_ref, dup_ref, dx1_ref, dg4_ref, dg3_ref, dcb_ref, dcw_ref, wup_v, wdn_v, carry_ref, sems):
        _load_once(wup_hbm, wup_v, sems.at[0])
        _load_once(wdn_hbm, wdn_v, sems.at[1])

        @pl.when(pl.program_id(0) == 0)
        def _():
            carry_ref[...] = jnp.zeros_like(carry_ref)
            dg4_ref[...] = jnp.zeros_like(dg4_ref)
            dg3_ref[...] = jnp.zeros_like(dg3_ref)
            dcb_ref[...] = jnp.zeros_like(dcb_ref)
            dcw_ref[...] = jnp.zeros_like(dcw_ref)

        dn = dn_ref[...]
        y = y_ref[...]
        g4v = g4_ref[...]
        r4 = lax.rsqrt(jnp.mean(y * y, axis=-1, keepdims=True) + EPS)
        dg4_ref[...] = dg4_ref[...] + jnp.sum(dn * y * r4, axis=0, keepdims=True)
        dy = _rms_bwd(y, r4, g4v, dn).astype(BF16)
        dy_ref[...] = dy
        dh2 = jnp.zeros((TM, D_MODEL), F32)
        for j, off, wd in _ff_pieces():
            base = SHARD_FF * j + off
            da = _mm_nt(dy, wdn_v[base:base + wd, :])
            for half, coef_ref in enumerate((c1_ref, c2_ref)):
                cs = slice(D_FF * half + base, D_FF * half + base + wd)
                du = da * coef_ref[:, base:base + wd].astype(F32)
                p1, p2 = _rows_after(du, carry_ref[0:1, cs], carry_ref[1:2, cs])
                carry_ref[:, cs] = du[0:HALO, :]
                upf = up_ref[:, cs].astype(F32)
                dcb_ref[:, cs] = dcb_ref[:, cs] + jnp.sum(du, axis=0, keepdims=True)
                dcw_ref[0:1, cs] = dcw_ref[0:1, cs] + jnp.sum(p2 * upf, axis=0, keepdims=True)
                dcw_ref[1:2, cs] = dcw_ref[1:2, cs] + jnp.sum(p1 * upf, axis=0, keepdims=True)
                dcw_ref[2:3, cs] = dcw_ref[2:3, cs] + jnp.sum(du * upf, axis=0, keepdims=True)
                dup = (cw_ref[2:3, cs] * du + cw_ref[1:2, cs] * p1 + cw_ref[0:1, cs] * p2).astype(BF16)
                dup_ref[:, cs] = dup
                dh2 = dh2 + _mm_nt(dup, wup_v[2 * half + j, :, off:off + wd])
        x1 = x1_ref[...]
        r3 = lax.rsqrt(jnp.mean(x1 * x1, axis=-1, keepdims=True) + EPS)
        dg3_ref[...] = dg3_ref[...] + jnp.sum(dh2 * x1 * r3, axis=0, keepdims=True)
        dx1_ref[...] = dn + _rms_bwd(x1, r3, g3_ref[...], dh2)

    outs = [
        jax.ShapeDtypeStruct((T, D_MODEL), BF16),
        jax.ShapeDtypeStruct((T, 2 * D_FF), BF16),
        jax.ShapeDtypeStruct((T, D_MODEL), F32),
        jax.ShapeDtypeStruct((1, D_MODEL), F32),
        jax.ShapeDtypeStruct((1, D_MODEL), F32),
        jax.ShapeDtypeStruct((1, 2 * D_FF), F32),
        jax.ShapeDtypeStruct((3, 2 * D_FF), F32),
    ]
    return pl.pallas_call(
        body, name="ffn_bwd", grid=(nt,), out_shape=outs,
        in_specs=[pl.BlockSpec((TM, D_MODEL), rev), pl.BlockSpec((TM, D_MODEL), rev), _const_spec((1, D_MODEL)),
                  pl.BlockSpec((TM, 2 * D_FF), rev), pl.BlockSpec((TM, D_FF), rev), pl.BlockSpec((TM, D_FF), rev),
                  _const_spec((3, 2 * D_FF)), _any_spec(), _any_spec(), pl.BlockSpec((TM, D_MODEL), rev),
                  _const_spec((1, D_MODEL))],
        out_specs=[pl.BlockSpec((TM, D_MODEL), rev), pl.BlockSpec((TM, 2 * D_FF), rev), pl.BlockSpec((TM, D_MODEL), rev),
                   _const_spec((1, D_MODEL)), _const_spec((1, D_MODEL)), _const_spec((1, 2 * D_FF)),
                   _const_spec((3, 2 * D_FF))],
        scratch_shapes=[pltpu.VMEM((N_SHARD, D_MODEL, SHARD_FF), BF16), pltpu.VMEM((D_FF, D_MODEL), BF16),
                        pltpu.VMEM((HALO, 2 * D_FF), F32), pltpu.SemaphoreType.DMA((2,))],
        compiler_params=_params(),
    )(dx2, y, g4, up, c1, c2, cw, wdown, wup, x1, g3)


def _mix_out_bwd(dx1, cat, g2, wout, og, gg, gnorm):
    T = dx1.shape[0]
    TM = _wide_tile(T)

    def body(dx1_ref, cat_ref, g2_ref, wout_ref, og_ref, gg_ref, gn_ref,
             dmix_ref, dog_ref, dgg_ref, dosw_ref, dg2_ref, dgn_ref):
        @pl.when(pl.program_id(0) == 0)
        def _():
            dg2_ref[...] = jnp.zeros_like(dg2_ref)
            dgn_ref[...] = jnp.zeros_like(dgn_ref)

        dx1 = dx1_ref[...]
        mix = _mm(cat_ref[...], wout_ref[...])
        r2 = lax.rsqrt(jnp.mean(mix * mix, axis=-1, keepdims=True) + EPS)
        dg2_ref[...] = dg2_ref[...] + jnp.sum(dx1 * mix * r2, axis=0, keepdims=True)
        dmix = _rms_bwd(mix, r2, g2_ref[...], dx1).astype(BF16)
        dmix_ref[...] = dmix
        dcat = _mm_nt(dmix, wout_ref[...])
        dosw_ref[...] = dcat[:, GLA_V:].astype(BF16)
        gn = gn_ref[...]
        dgn = jnp.zeros((1, LANES), F32)
        for j in range(GLA_HEADS):
            hs = slice(128 * j, 128 * j + 128)
            o = og_ref[:, hs].astype(F32)
            r = lax.rsqrt(jnp.mean(o * o, axis=-1, keepdims=True) + EPS)
            gate = gg_ref[:, hs].astype(F32)
            sg = _sigmoid(gate)
            dgated = dcat[:, hs]
            dnorm = dgated * (gate * sg)
            dgg_ref[:, hs] = (dgated * (o * r * gn) * (sg * (1.0 + gate * (1.0 - sg)))).astype(BF16)
            dgn = dgn + jnp.sum(dnorm * o * r, axis=0, keepdims=True)
            dog_ref[:, hs] = _rms_bwd(o, r, gn, dnorm)
        dgn_ref[...] = dgn_ref[...] + dgn

    return pl.pallas_call(
        body, name="mix_out_bwd", grid=(T // TM,),
        out_shape=[jax.ShapeDtypeStruct((T, D_MODEL), BF16), jax.ShapeDtypeStruct((T, GLA_V), F32),
                   jax.ShapeDtypeStruct((T, GLA_V), BF16), jax.ShapeDtypeStruct((T, SWA_Q), BF16),
                   jax.ShapeDtypeStruct((1, D_MODEL), F32), jax.ShapeDtypeStruct((1, LANES), F32)],
        in_specs=[_row_spec(TM, D_MODEL), _row_spec(TM, D_MODEL), _const_spec((1, D_MODEL)),
                  _const_spec((D_MODEL, D_MODEL)), _row_spec(TM, GLA_V), _row_spec(TM, GLA_V), _const_spec((1, LANES))],
        out_specs=[_row_spec(TM, D_MODEL), _row_spec(TM, GLA_V), _row_spec(TM, GLA_V), _row_spec(TM, SWA_Q),
                   _const_spec((1, D_MODEL)), _const_spec((1, LANES))],
        compiler_params=_params(),
    )(dx1, cat, g2, wout, og, gg, gnorm)


def _proj_bwd(x, g1, wp, gup, glr, dq, dk, dv, dgg, dsq, dkd, dvd, dz, rc, rsa, rsb, dx1):
    T = x.shape[0]
    TM = _wide_tile(T)

    def body(x_ref, g1_ref, wp_hbm, gup_ref, glr_ref, dq_ref, dk_ref, dv_ref, dgg_ref, dsq_ref, dkd_ref, dvd_ref,
             dz_ref, rc_ref, rsa_ref, rsb_ref, dx1_ref, dx_ref, dp_ref, dg1_ref, dgup_ref, dgb_ref, wp_v, sem):
        _load_once(wp_hbm, wp_v, sem)

        @pl.when(pl.program_id(0) == 0)
        def _():
            dg1_ref[...] = jnp.zeros_like(dg1_ref)
            dgup_ref[...] = jnp.zeros_like(dgup_ref)
            dgb_ref[...] = jnp.zeros_like(dgb_ref)

        rc_, rsa_, rsb_ = rc_ref[...], rsa_ref[...], rsb_ref[...]
        dp_ref[:, 0:256] = dq_ref[...]
        dp_ref[:, 256:512] = dk_ref[...]
        dp_ref[:, 512:1024] = dv_ref[...]
        dp_ref[:, 1024:1536] = dgg_ref[...]
        for s in range(4):
            cs = slice(128 * s, 128 * s + 128)
            dp_ref[:, 1536 + 128 * s:1664 + 128 * s] = _rotate_bwd(dsq_ref[:, cs].astype(F32), rc_, rsa_, rsb_).astype(BF16)
        first = _iota((TM, LANES), 1) < 64
        dk0 = dkd_ref[:, 0:128].astype(F32)
        dk1 = dkd_ref[:, 128:256].astype(F32)
        dkr = jnp.where(first, dk0 + pltpu.roll(dk0, 64, 1), dk1 + pltpu.roll(dk1, 64, 1))
        dp_ref[:, 2048:2176] = _rotate_bwd(dkr, rc_, rsa_, rsb_).astype(BF16)
        dv0 = dvd_ref[:, 0:128].astype(F32)
        dv1 = dvd_ref[:, 128:256].astype(F32)
        dp_ref[:, 2176:2304] = jnp.where(first, dv0 + pltpu.roll(dv0, 64, 1), dv1 + pltpu.roll(dv1, 64, 1)).astype(BF16)
        dz = dz_ref[...]
        dzb = dz.astype(BF16)
        dp_ref[:, 2304:2432] = _mm_nt(dzb, gup_ref[...]).astype(BF16)
        dgup_ref[...] = dgup_ref[...] + _mm_tn(glr_ref[...], dzb)
        dgb_ref[...] = dgb_ref[...] + jnp.sum(dz, axis=0, keepdims=True)
        dh1 = _mm_nt(dp_ref[...], wp_v[...])
        xt = x_ref[...]
        r = lax.rsqrt(jnp.mean(xt * xt, axis=-1, keepdims=True) + EPS)
        dg1_ref[...] = dg1_ref[...] + jnp.sum(dh1 * xt * r, axis=0, keepdims=True)
        dx_ref[...] = dx1_ref[...] + _rms_bwd(xt, r, g1_ref[...], dh1)

    row = lambda cols: _row_spec(TM, cols)
    return pl.pallas_call(
        body, name="proj_bwd", grid=(T // TM,),
        out_shape=[jax.ShapeDtypeStruct((T, D_MODEL), F32), jax.ShapeDtypeStruct((T, IN_WIDTH_PAD), BF16),
                   jax.ShapeDtypeStruct((1, D_MODEL), F32), jax.ShapeDtypeStruct((LANES, GLA_QK), F32),
                   jax.ShapeDtypeStruct((1, GLA_QK), F32)],
        in_specs=[row(D_MODEL), _const_spec((1, D_MODEL)), _any_spec(), _const_spec((LANES, GLA_QK)), row(LANES),
                  row(GLA_QK), row(GLA_QK), row(GLA_V), row(GLA_V), row(SWA_Q), row(256), row(256), row(GLA_QK),
                  row(LANES), row(LANES), row(LANES), row(D_MODEL)],
        out_specs=[row(D_MODEL), row(IN_WIDTH_PAD), _const_spec((1, D_MODEL)), _const_spec((LANES, GLA_QK)),
                   _const_spec((1, GLA_QK))],
        scratch_shapes=[pltpu.VMEM((D_MODEL, IN_WIDTH_PAD), BF16), pltpu.SemaphoreType.DMA],
        compiler_params=_params(),
    )(x, g1, wp, gup, glr, dq, dk, dv, dgg, dsq, dkd, dvd, dz, rc, rsa, rsb, dx1)


def _matmul_tn(a, b, tn, name, column_blocks_major=False, after=None):
    T, M = a.shape
    N = b.shape[1]
    tk = next(t for t in (2048, 1024, 512, TM) if T % t == 0 and t * (M + tn) <= 2048 * (D_MODEL + SHARD_FF))
    nk = T // tk
    if column_blocks_major:
        out_shape = jax.ShapeDtypeStruct((N // tn, M, tn), F32)
        out_spec = pl.BlockSpec((None, M, tn), lambda j, kk: (j, 0, 0))
    else:
        out_shape = jax.ShapeDtypeStruct((M, N), F32)
        out_spec = pl.BlockSpec((M, tn), lambda j, kk: (0, j))

    def body(a_ref, b_ref, *rest):
        o_ref = rest[-1]
        kk = pl.program_id(1)

        @pl.when(kk == 0)
        def _():
            o_ref[...] = jnp.zeros_like(o_ref)

        o_ref[...] = o_ref[...] + _mm_tn(a_ref[...], b_ref[...])

    ordering = [] if after is None else [after]
    return pl.pallas_call(
        body, name=name, grid=(N // tn, nk), out_shape=out_shape,
        in_specs=[pl.BlockSpec((tk, M), lambda j, kk: (kk, 0)), pl.BlockSpec((tk, tn), lambda j, kk: (kk, j))]
        + [_any_spec()] * len(ordering),
        out_specs=out_spec,
        compiler_params=_params(),
    )(a, b, *ordering)


def _adamw_update(w_ref, g_ref, m_ref, v_ref, d_ref, m2_ref, v2_ref):
    g_ = g_ref[...]
    m2 = ADAM_B1 * m_ref[...] + (1.0 - ADAM_B1) * g_
    v2 = ADAM_B2 * v_ref[...] + (1.0 - ADAM_B2) * (g_ * g_)
    m_hat = m2 / (1.0 - ADAM_B1 ** ADAM_STEP)
    v_hat = v2 / (1.0 - ADAM_B2 ** ADAM_STEP)
    d_ref[...] = -ADAM_LR * (m_hat / (jnp.sqrt(v_hat) + ADAM_EPS) + ADAM_WD * w_ref[...])
    m2_ref[...] = m2
    v2_ref[...] = v2


def _adamw(w, g, m, v, rows, name):
    R, C = w.shape

    def body(*refs):
        _adamw_update(*refs[:7])
        refs[7][...] = refs[1][...]

    spec = pl.BlockSpec((rows, C), lambda i: (i, 0))
    return pl.pallas_call(
        body, name=name, grid=(R // rows,), out_shape=[jax.ShapeDtypeStruct((R, C), F32)] * 4,
        in_specs=[spec] * 4, out_specs=[spec] * 4, compiler_params=_params(),
    )(w, g, m, v)


def _adamw_small(ws, gs, ms, vs):
    n = len(ws)

    def body(*refs):
        w_, g_, m_, v_, d_, m2_, v2_, g2_ = (refs[n * i:n * (i + 1)] for i in range(8))
        for k in range(n):
            _adamw_update(w_[k], g_[k], m_[k], v_[k], d_[k], m2_[k], v2_[k])
            g2_[k][...] = g_[k][...]

    vm = pl.BlockSpec(memory_space=pltpu.VMEM)
    outs = pl.pallas_call(
        body, name="adamw_small", out_shape=[jax.ShapeDtypeStruct(w.shape, F32) for w in ws] * 4,
        in_specs=[vm] * (4 * n), out_specs=[vm] * (4 * n),
    )(*ws, *gs, *ms, *vs)
    return outs[:n], outs[n:2 * n], outs[2 * n:3 * n], outs[3 * n:]


def _place():
    x, y, c = lax.axis_index("x"), lax.axis_index("y"), lax.axis_index("c")
    chips = [(1 - x, y), (x, 1 - y), (1 - x, 1 - y)]
    return x, y, c, chips


class _staged_copies:
    def __init__(self, srcs, dsts, stage, sems):
        n = len(srcs)
        self.loads = [pltpu.make_async_copy(srcs[k], stage[k], sems.at[k]) for k in range(n)]
        self.stores = [pltpu.make_async_copy(stage[k], dsts[k], sems.at[n + k]) for k in range(n)]

    def load(self):
        for cp in self.loads:
            cp.start()

    def store(self):
        for ld, st in zip(self.loads, self.stores):
            ld.wait()
            st.start()

    def finish(self):
        for cp in self.stores:
            cp.wait()


def _allgather_shards(parts, unit_rows):
    n = len(parts)
    units = [(k, r, unit_rows[k]) for k in range(n) for r in range(0, parts[k].shape[0] // 2, unit_rows[k])]
    nu = len(units)

    def body(*refs):
        ins, outs, stage = refs[:n], refs[n:2 * n], refs[2 * n:3 * n]
        send_sems, recv_sems, local_sems = refs[3 * n:]
        x, y, c, chips = _place()
        sibling = (x, y, 1 - c)
        own = _staged_copies(ins, [o.at[2 * x + y] for o in outs], stage, local_sems)

        def block(i, px, py, half):
            k, r, u = units[i]
            return outs[k].at[2 * px + py, pl.ds(half * (parts[k].shape[0] // 2) + r, u), :]

        def copy(i, j, px, py, half, to, src=None):
            return pltpu.make_async_remote_copy(
                src_ref=block(i, px, py, half) if src is None else src, dst_ref=block(i, px, py, half),
                send_sem=send_sems.at[nu * j + i], recv_sem=recv_sems.at[nu * j + i], device_id=to, device_id_type=MESH)

        own.load()
        first, passed = [], []
        for i, (k, r, u) in enumerate(units):
            for j, chip in enumerate(chips):
                src = ins[k].at[pl.ds(c * (parts[k].shape[0] // 2) + r, u), :]
                first.append(copy(i, j, x, y, c, (*chip, c), src=src))
                first[-1].start()
        own.store()
        for i in range(nu):
            for j, chip in enumerate(chips):
                copy(i, j, *chip, c, (x, y, c)).wait_recv()
                passed.append(copy(i, 3 + j, *chip, c, sibling))
                passed[-1].start()
        for i in range(nu):
            for j, chip in enumerate(chips):
                copy(i, 3 + j, *chip, 1 - c, (x, y, c)).wait_recv()
        for cp in first + passed:
            cp.wait_send()
        own.finish()

    return pl.pallas_call(
        body, name="allgather_shards", out_shape=[jax.ShapeDtypeStruct((N_SHARD,) + p.shape, p.dtype) for p in parts],
        in_specs=[_any_spec()] * n, out_specs=[_any_spec()] * n,
        scratch_shapes=[pltpu.VMEM(p.shape, p.dtype) for p in parts] + [
            pltpu.SemaphoreType.DMA((6 * nu,)), pltpu.SemaphoreType.DMA((6 * nu,)), pltpu.SemaphoreType.DMA((2 * n,))],
        compiler_params=_params(),
    )(*parts)


def _d2d_pieces(rows, piece_rows):
    return [(r, piece_rows) for r in range(0, rows, piece_rows)]


def _rs_pair_swap(arrs, piece_rows, name):
    n = len(arrs)

    def body(*refs):
        ins, outs = refs[:n], refs[n:2 * n]
        send_sems, recv_sems = refs[2 * n:]
        x, y, c, _ = _place()
        sibling = (x, y, 1 - c)
        for k in range(n):
            H = arrs[k].shape[1] // 2
            for s in range(N_SHARD):
                for r, pr in _d2d_pieces(H, piece_rows[k]):
                    pltpu.make_async_remote_copy(
                        src_ref=ins[k].at[s, pl.ds((1 - c) * H + r, pr), :], dst_ref=outs[k].at[s, pl.ds(r, pr), :],
                        send_sem=send_sems.at[k], recv_sem=recv_sems.at[k], device_id=sibling, device_id_type=MESH).start()
        for k in range(n):
            H = arrs[k].shape[1] // 2
            whole = pltpu.make_async_remote_copy(
                src_ref=ins[k].at[:, pl.ds(0, H), :], dst_ref=outs[k], send_sem=send_sems.at[k], recv_sem=recv_sems.at[k],
                device_id=sibling, device_id_type=MESH)
            whole.wait_recv()
            whole.wait_send()

    return pl.pallas_call(
        body, name=name,
        out_shape=[jax.ShapeDtypeStruct((N_SHARD, a.shape[1] // 2, a.shape[2]), F32) for a in arrs],
        in_specs=[_any_spec()] * n, out_specs=[_any_spec()] * n,
        scratch_shapes=[pltpu.SemaphoreType.DMA((n,)), pltpu.SemaphoreType.DMA((n,))],
    )(*arrs)


def _rs_add_pair(a, got, core, rows, name):
    _, H, C = got.shape
    nb = H // rows

    def body(c_ref, a_ref, b_ref, o_ref):
        o_ref[...] = (a_ref[...] + b_ref[...]).astype(BF16)

    spec = pl.BlockSpec((1, rows, C), lambda s, r, c_ref: (s, r, 0))
    return pl.pallas_call(
        body, name=name, out_shape=jax.ShapeDtypeStruct(got.shape, BF16),
        grid_spec=pltpu.PrefetchScalarGridSpec(
            num_scalar_prefetch=1, grid=(N_SHARD, nb),
            in_specs=[pl.BlockSpec((1, rows, C), lambda s, r, c_ref: (s, c_ref[0] * nb + r, 0)), spec], out_specs=spec),
        compiler_params=_params(),
    )(core, a, got)


def _rs_sum_chips(parts, rows, name):
    _, H, C = parts.shape

    def body(p_ref, o_ref):
        o_ref[...] = ((p_ref[0].astype(F32) + p_ref[1].astype(F32)) + p_ref[2].astype(F32)) + p_ref[3].astype(F32)

    return pl.pallas_call(
        body, name=name, grid=(H // rows,), out_shape=jax.ShapeDtypeStruct((H, C), F32),
        in_specs=[pl.BlockSpec((N_SHARD, rows, C), lambda r: (0, r, 0))],
        out_specs=pl.BlockSpec((rows, C), lambda r: (r, 0)), compiler_params=_params(),
    )(parts)


def _rs_pair_share(halves, piece_rows, name):
    n = len(halves)

    def body(*refs):
        ins, outs, stage = refs[:n], refs[n:2 * n], refs[2 * n:3 * n]
        send_sems, recv_sems, local_sems = refs[3 * n:]
        x, y, c, _ = _place()
        sibling = (x, y, 1 - c)
        own = _staged_copies(ins, [outs[k].at[pl.ds(c * halves[k].shape[0], halves[k].shape[0]), :] for k in range(n)],
                             stage, local_sems)
        own.load()
        for k in range(n):
            H = halves[k].shape[0]
            for r, pr in _d2d_pieces(H, piece_rows[k]):
                pltpu.make_async_remote_copy(
                    src_ref=ins[k].at[pl.ds(r, pr), :], dst_ref=outs[k].at[pl.ds(c * H + r, pr), :],
                    send_sem=send_sems.at[k], recv_sem=recv_sems.at[k], device_id=sibling, device_id_type=MESH).start()
        own.store()
        for k in range(n):
            H = halves[k].shape[0]
            whole = pltpu.make_async_remote_copy(
                src_ref=ins[k], dst_ref=outs[k].at[pl.ds((1 - c) * H, H), :], send_sem=send_sems.at[k],
                recv_sem=recv_sems.at[k], device_id=sibling, device_id_type=MESH)
            whole.wait_recv()
            whole.wait_send()
        own.finish()

    return pl.pallas_call(
        body, name=name, out_shape=[jax.ShapeDtypeStruct((2 * h.shape[0], h.shape[1]), F32) for h in halves],
        in_specs=[_any_spec()] * n, out_specs=[_any_spec()] * n,
        scratch_shapes=[pltpu.VMEM(h.shape, F32) for h in halves] + [
            pltpu.SemaphoreType.DMA((n,)), pltpu.SemaphoreType.DMA((n,)), pltpu.SemaphoreType.DMA((2 * n,))],
        compiler_params=_params(),
    )(*halves)


_HBM = pl.BlockSpec(memory_space=pltpu.HBM)
_SEM = pl.BlockSpec(memory_space=pltpu.SEMAPHORE)
_EFFECT = pltpu.SideEffectType.DATAFLOW_SIDE_EFFECTING


def _gather_plan(srcs, lands, x, y, c, chips):
    plan = []
    for k in range(len(srcs)):
        H = srcs[k].shape[0] // 2
        for px, py in chips:
            plan.append((srcs[k].at[pl.ds(c * H, H), :], lands[k].at[2 * x + y, pl.ds(c * H, H), :], (px, py, c),
                         lands[k].at[2 * px + py, pl.ds(c * H, H), :]))
    return plan


def _scatter_plan(srcs, lands, x, y, c, chips):
    plan = []
    for k in range(len(srcs)):
        for px, py in chips:
            plan.append((srcs[k].at[2 * px + py], lands[k].at[2 * x + y], (px, py, c), lands[k].at[2 * px + py]))
    return plan


def _swap_plan(srcs, lands, x, y, c, chips):
    plan = []
    for k in range(len(srcs)):
        H = srcs[k].shape[1] // 2
        rows = H // SWAP_PIECES
        for s in range(N_SHARD):
            for r in range(0, H, rows):
                piece = lands[k].at[s, pl.ds(r, rows), :]
                plan.append((srcs[k].at[s, pl.ds((1 - c) * H + r, rows), :], piece, (x, y, 1 - c), piece))
    return plan


SWAP_PIECES = 8
COPIES_PER_ARRAY = {_gather_plan: 3, _scatter_plan: 3, _swap_plan: N_SHARD * SWAP_PIECES}


def _ici_start(srcs, lands, make_plan, name):
    n = len(srcs)
    ncopy = COPIES_PER_ARRAY[make_plan] * n

    def body(*refs):
        ins, lnd = refs[:n], refs[n:2 * n]
        send_sems, recv_sems = refs[2 * n], refs[2 * n + 1]
        token = refs[-1]
        x, y, c, chips = _place()
        for i, (src, dst, peer, _) in enumerate(make_plan(ins, lnd, x, y, c, chips)):
            pltpu.make_async_remote_copy(src_ref=src, dst_ref=dst, send_sem=send_sems.at[i], recv_sem=recv_sems.at[i],
                                         device_id=peer, device_id_type=MESH).start()
        token[...] = jnp.zeros_like(token)

    arrays = list(srcs) + list(lands)
    return pl.pallas_call(
        body, name=name,
        out_shape=(pltpu.SemaphoreType.DMA((ncopy,)), pltpu.SemaphoreType.DMA((ncopy,)),
                   *[pltpu.HBM(a.shape, a.dtype) for a in arrays], jax.ShapeDtypeStruct((8, LANES), F32)),
        in_specs=[_HBM] * (2 * n), out_specs=(_SEM, _SEM, *[_HBM] * (2 * n), pl.BlockSpec(memory_space=pltpu.VMEM)),
        input_output_aliases={i: 2 + i for i in range(2 * n)},
        compiler_params=pltpu.CompilerParams(has_side_effects=_EFFECT),
    )(*[pltpu.with_memory_space_constraint(a, pltpu.HBM) for a in arrays])


def _ici_wait(started, after, make_plan, name):
    send_sems, recv_sems = started[0], started[1]
    arrays = list(started[2:-1])
    n = len(arrays) // 2

    def body(*refs):
        ins, lnd = refs[:n], refs[n:2 * n]
        send_sems, recv_sems = refs[2 * n], refs[2 * n + 1]
        x, y, c, chips = _place()
        for i, (src, _, peer, landed) in enumerate(make_plan(ins, lnd, x, y, c, chips)):
            cp = pltpu.make_async_remote_copy(src_ref=src, dst_ref=landed, send_sem=send_sems.at[i],
                                              recv_sem=recv_sems.at[i], device_id=peer, device_id_type=MESH)
            cp.wait_send()
            cp.wait_recv()

    outs = pl.pallas_call(
        body, name=name, out_shape=tuple(pltpu.HBM(a.shape, a.dtype) for a in arrays),
        in_specs=[_HBM] * (2 * n) + [_SEM, _SEM, pl.BlockSpec(memory_space=pl.ANY)], out_specs=tuple([_HBM] * (2 * n)),
        input_output_aliases={i: i for i in range(2 * n)},
        compiler_params=pltpu.CompilerParams(has_side_effects=_EFFECT),
    )(*arrays, send_sems, recv_sems, after)
    return list(outs[:n]), list(outs[n:])


def _gather_finish(parts, lands):
    n = len(parts)

    def body(*refs):
        ins, lnd, outs, stage = refs[:n], refs[n:2 * n], refs[2 * n:3 * n], refs[3 * n:4 * n]
        send_sems, recv_sems, local_sems = refs[4 * n:]
        x, y, c, chips = _place()
        sibling = (x, y, 1 - c)
        own = _staged_copies(ins, [o.at[2 * x + y] for o in outs], stage, local_sems)
        own.load()
        sends = []
        for k in range(n):
            H = parts[k].shape[0] // 2
            for j, (px, py) in enumerate(chips):
                half = outs[k].at[2 * px + py, pl.ds(c * H, H), :]
                sends.append(pltpu.make_async_remote_copy(src_ref=half, dst_ref=half, send_sem=send_sems.at[3 * k + j],
                                                          recv_sem=recv_sems.at[3 * k + j], device_id=sibling, device_id_type=MESH))
                sends[-1].start()
        own.store()
        for k in range(n):
            H = parts[k].shape[0] // 2
            for j, (px, py) in enumerate(chips):
                other = outs[k].at[2 * px + py, pl.ds((1 - c) * H, H), :]
                pltpu.make_async_remote_copy(src_ref=other, dst_ref=other, send_sem=send_sems.at[3 * k + j],
                                             recv_sem=recv_sems.at[3 * k + j], device_id=sibling, device_id_type=MESH).wait_recv()
        for cp in sends:
            cp.wait_send()
        own.finish()

    return pl.pallas_call(
        body, name="gather_finish", out_shape=[jax.ShapeDtypeStruct(l.shape, l.dtype) for l in lands],
        in_specs=[_any_spec()] * (2 * n), out_specs=[_any_spec()] * n,
        input_output_aliases={n + k: k for k in range(n)},
        scratch_shapes=[pltpu.VMEM(p.shape, p.dtype) for p in parts] + [
            pltpu.SemaphoreType.DMA((3 * n,)), pltpu.SemaphoreType.DMA((3 * n,)), pltpu.SemaphoreType.DMA((2 * n,))],
        compiler_params=_params(),
    )(*parts, *lands)


def _place_own(part, chip, rows, name):
    _, H, C = part.shape

    def body(chip_ref, p_ref, o_ref):
        o_ref[...] = p_ref[...]

    spec = pl.BlockSpec((1, rows, C), lambda r, chip_ref: (chip_ref[0], r, 0))
    return pl.pallas_call(
        body, name=name, out_shape=jax.ShapeDtypeStruct(part.shape, part.dtype),
        grid_spec=pltpu.PrefetchScalarGridSpec(num_scalar_prefetch=1, grid=(H // rows,), in_specs=[spec], out_specs=spec),
        compiler_params=_params(),
    )(chip, part)


SMALL_COLS = 1024


def _small_rows(shapes):
    starts, row = [], 0
    for r, cdim in shapes:
        starts.append(row)
        row += r * (-(-cdim // SMALL_COLS))
    return starts, -(-row // 8) * 8


def _allreduce_small(arrays, sink_rows, loss):
    n = len(arrays)
    shapes = [a.shape for a in arrays] + [(1, SWA_HEADS), (1, 1)]
    starts, total_rows = _small_rows(shapes)

    def pieces(k):
        r, cdim = shapes[k]
        per = -(-cdim // SMALL_COLS)
        return [(i, SMALL_COLS * j, min(SMALL_COLS, cdim - SMALL_COLS * j), starts[k] + per * i + j)
                for i in range(r) for j in range(per)]

    def body(*refs):
        ins, sink_ref, loss_ref = refs[:n], refs[n], refs[n + 1]
        outs = refs[n + 2:2 * n + 4]
        mine, all_ref, tot_ref, send_sems, recv_sems = refs[2 * n + 4:]
        x, y, c, _ = _place()
        me = 4 * x + 2 * y + c
        mine[...] = jnp.zeros_like(mine)
        for k in range(n):
            for i, col, wd, row in pieces(k):
                mine[row:row + 1, 0:wd] = ins[k][i:i + 1, col:col + wd]
        lane = _iota((1, SMALL_COLS), 1)
        sinks = jnp.zeros((1, SMALL_COLS), F32)
        for h in range(SWA_HEADS):
            head = jnp.sum(sink_ref[SWA_BLOCK * h:SWA_BLOCK * (h + 1), :]) * (1.0 / LANES)
            sinks = jnp.where(lane == h, head, sinks)
        mine[starts[n]:starts[n] + 1, :] = sinks
        mine[starts[n + 1]:starts[n + 1] + 1, 0:LANES] = loss_ref[0:1, :]
        all_ref[me] = mine[...]
        sends = []
        for k in range(1, 8):
            kx, ky, kc = (k >> 2) & 1, (k >> 1) & 1, k & 1
            peer = (x ^ kx, y ^ ky, c ^ kc)
            cp = pltpu.make_async_remote_copy(
                src_ref=mine, dst_ref=all_ref.at[me], send_sem=send_sems.at[k - 1], recv_sem=recv_sems.at[k - 1],
                device_id=peer, device_id_type=MESH)
            cp.start()
            sends.append(cp)
        for k in range(1, 8):
            kx, ky, kc = (k >> 2) & 1, (k >> 1) & 1, k & 1
            src = 4 * (x ^ kx) + 2 * (y ^ ky) + (c ^ kc)
            pltpu.make_async_remote_copy(
                src_ref=mine, dst_ref=all_ref.at[src], send_sem=send_sems.at[k - 1], recv_sem=recv_sems.at[k - 1],
                device_id=(x, y, c), device_id_type=MESH).wait_recv()
        for cp in sends:
            cp.wait_send()
        total = all_ref[0]
        for d in range(1, 8):
            total = total + all_ref[d]
        tot_ref[...] = total
        for k in range(n + 2):
            for i, col, wd, row in pieces(k):
                outs[k][i:i + 1, col:col + wd] = tot_ref[row:row + 1, 0:wd]

    vm = pl.BlockSpec(memory_space=pltpu.VMEM)
    buf = pltpu.VMEM((total_rows, SMALL_COLS), F32)
    return pl.pallas_call(
        body, name="allreduce_small", out_shape=[jax.ShapeDtypeStruct(s, F32) for s in shapes],
        in_specs=[vm] * (n + 2), out_specs=[vm] * (n + 2),
        scratch_shapes=[buf, pltpu.VMEM((8, total_rows, SMALL_COLS), F32), buf,
                        pltpu.SemaphoreType.DMA((7,)), pltpu.SemaphoreType.DMA((7,))],
    )(*arrays, sink_rows, loss)


BIG_NAMES = ("w_in", "w_out", "w_up", "w_down")
MATRIX_NAMES = BIG_NAMES + ("gla_gate_up", "conv_w")
LATE_NAMES = ("w_out", "w_up", "w_down")
GATE_SHARD = (16, GLA_QK // N_SHARD)
CONVW_SHARD = (3, SHARD_FF)
SMALL_W_ROWS = 96
PIECE_ROWS = (128, 128, 64, 88)
ADD_ROWS = (256, 128, 256, 176)
FIRST_UNIT_ROWS = (256, SMALL_W_ROWS // 2)


def _pad_rows(flat, rows):
    return jnp.pad(flat, (0, rows * LANES - flat.shape[0])).reshape(rows, LANES)


def _pack_small_weights(gate_up, conv_w):
    bits = lax.bitcast_convert_type(conv_w, BF16)
    return _pad_rows(jnp.concatenate([gate_up.astype(BF16).reshape(-1), bits.reshape(-1)]), SMALL_W_ROWS)


def _unpack_small_weights(packed):
    flat = packed.reshape(N_SHARD, -1)
    n_gate = GATE_SHARD[0] * GATE_SHARD[1]
    n_conv = 2 * CONVW_SHARD[0] * CONVW_SHARD[1]
    gate = flat[:, :n_gate].reshape((N_SHARD,) + GATE_SHARD)
    conv = lax.bitcast_convert_type(flat[:, n_gate:n_gate + n_conv].reshape((N_SHARD,) + CONVW_SHARD + (2,)), F32)
    return (jnp.transpose(gate, (1, 0, 2)).reshape(16, GLA_QK), jnp.transpose(conv, (1, 0, 2)).reshape(3, 2 * D_FF))


def _permute_w_in(w):
    pad = jnp.zeros((w.shape[0], IN_WIDTH_PAD - IN_WIDTH), w.dtype)
    return jnp.concatenate([w[:, 0:1024], w[:, 1040:2320], w[:, 1024:1040], pad], axis=1)


def _unpermute_w_in(wp):
    return jnp.concatenate([wp[:, 0:1024], wp[:, 2304:2320], wp[:, 1024:2304]], axis=1)


def _rope_tables(positions):
    half = ROPE_DIM // 2
    inv_freq = ROPE_THETA ** (-jnp.arange(half, dtype=F32) * (2.0 / ROPE_DIM))
    d = jnp.arange(LANES) % SWA_HD
    freq = jnp.where(d < ROPE_DIM, inv_freq[d % half], 0.0)
    ang = positions.astype(F32)[:, None] * freq
    cos, sin = jnp.cos(ang), jnp.sin(ang)
    return cos, jnp.where(d < half, -sin, 0.0), jnp.where((d >= half) & (d < ROPE_DIM), sin, 0.0)


SMALL_NAMES = (("pre_mix_norm", 1024), ("gla_gate_bias", 256), ("gla_out_norm", 128), ("swa_sinks", 8),
               ("post_mix_norm", 1024), ("pre_ffn_norm", 1024), ("conv_b", 5632), ("post_ffn_norm", 1024))


def _local_step(x, positions, target, w, small, late_weights, early_grads, last_grads):
    rc, rsa, rsb = _rope_tables(positions)
    wp = w["wp"]
    gup = jnp.pad(w["gla_gate_up"], ((0, LANES - 16), (0, 0)))
    g1, g2, g3, g4 = (small[n] for n in ("pre_mix_norm", "post_mix_norm", "pre_ffn_norm", "post_ffn_norm"))
    gbias, gnorm, cb = small["gla_gate_bias"], small["gla_out_norm"], small["conv_b"]
    sinks = small["swa_sinks"].reshape(-1)
    cw = w["conv_w"]

    h1, q, k, v, la, gg, sq, kd, vd, glr = _proj_fwd(x, g1, wp, gup, gbias, rc, rsa, rsb)
    og, s_all = _gla_fwd(q, k, v, la)
    osw = _swa_fwd(sq, kd, vd, sinks)
    w_out, w_up4, w_down = late_weights(osw)
    x1, cat = _mix_out_fwd(x, og, gg, osw, gnorm, w_out, g2)
    h2, up, act, c1, c2, y, dx2, loss = _ffn_fwd(x1, g3, w_up4, cw, cb, w_down, g4, target)

    dy, dup, dx1, dg4, dg3, dcb, dcw = _ffn_bwd(dx2, y, g4, up, c1, c2, cw, w_down, w_up4, x1, g3)
    zero = early_grads(lambda: _matmul_tn(h2, dup, SHARD_FF, "grad_w_up", column_blocks_major=True),
                       lambda after=None: _matmul_tn(act, dy, D_MODEL, "grad_w_down", after=after))
    dmix, dog, dgg, dosw, dg2, dgn = _mix_out_bwd(dx1, cat, g2 + zero, w_out, og, gg, gnorm)
    dsq, dkd, dvd, dsink = _swa_bwd(sq, kd, vd, sinks, dosw)
    dq, dk, dv, dz = _gla_bwd(q, k, v, la, s_all, dog)
    dx, dproj, dg1, dgup, dgb = _proj_bwd(x, g1, wp, gup, glr, dq, dk, dv, dgg, dsq, dkd, dvd, dz, rc, rsa, rsb, dx1)

    grads = {
        "last": last_grads(lambda: _matmul_tn(h1, dproj, IN_WIDTH_PAD, "grad_w_in"),
                           lambda after=None: _matmul_tn(cat, dmix, D_MODEL, "grad_w_out", after=after)),
        "gla_gate_up": dgup[0:16],
        "conv_w": dcw,
    }
    small_grads = {
        "pre_mix_norm": dg1, "gla_gate_bias": dgb, "gla_out_norm": dgn,
        "post_mix_norm": dg2, "pre_ffn_norm": dg3, "conv_b": dcb, "post_ffn_norm": dg4,
    }
    return loss, dx, grads, small_grads, dsink


ADAM_ROWS = {"w_in": 256, "w_out": 256, "w_up": 256, "w_down": 176}
WEIGHT_ORDER = ("pre_mix_norm", "w_in", "gla_gate_up", "gla_gate_bias", "gla_out_norm", "swa_sinks", "w_out",
                "post_mix_norm", "pre_ffn_norm", "w_up", "conv_w", "conv_b", "w_down", "post_ffn_norm")
def kernel(x, positions, pre_mix_norm, w_in, gla_gate_up, gla_gate_bias, gla_out_norm, swa_sinks, w_out, post_mix_norm, pre_ffn_norm, w_up, conv_w, conv_b, w_down, post_ffn_norm, loss_target, m_pre_mix_norm, m_w_in, m_gla_gate_up, m_gla_gate_bias, m_gla_out_norm, m_swa_sinks, m_w_out, m_post_mix_norm, m_pre_ffn_norm, m_w_up, m_conv_w, m_conv_b, m_w_down, m_post_ffn_norm, v_pre_mix_norm, v_w_in, v_gla_gate_up, v_gla_gate_bias, v_gla_out_norm, v_swa_sinks, v_w_out, v_post_mix_norm, v_pre_ffn_norm, v_w_up, v_conv_w, v_conv_b, v_w_down, v_post_ffn_norm):
    weights = dict(pre_mix_norm=pre_mix_norm, w_in=w_in, gla_gate_up=gla_gate_up, gla_gate_bias=gla_gate_bias,
                   gla_out_norm=gla_out_norm, swa_sinks=swa_sinks, w_out=w_out, post_mix_norm=post_mix_norm,
                   pre_ffn_norm=pre_ffn_norm, w_up=w_up, conv_w=conv_w, conv_b=conv_b, w_down=w_down,
                   post_ffn_norm=post_ffn_norm)
    mom = dict(pre_mix_norm=m_pre_mix_norm, w_in=m_w_in, gla_gate_up=m_gla_gate_up, gla_gate_bias=m_gla_gate_bias,
               gla_out_norm=m_gla_out_norm, swa_sinks=m_swa_sinks, w_out=m_w_out, post_mix_norm=m_post_mix_norm,
               pre_ffn_norm=m_pre_ffn_norm, w_up=m_w_up, conv_w=m_conv_w, conv_b=m_conv_b, w_down=m_w_down,
               post_ffn_norm=m_post_ffn_norm)
    var = dict(pre_mix_norm=v_pre_mix_norm, w_in=v_w_in, gla_gate_up=v_gla_gate_up, gla_gate_bias=v_gla_gate_bias,
               gla_out_norm=v_gla_out_norm, swa_sinks=v_swa_sinks, w_out=v_w_out, post_mix_norm=v_post_mix_norm,
               pre_ffn_norm=v_pre_ffn_norm, w_up=v_w_up, conv_w=v_conv_w, conv_b=v_conv_b, w_down=v_w_down,
               post_ffn_norm=v_post_ffn_norm)
    weights, mom, var = ({n: a[0] if a.ndim == 3 else a for n, a in d.items()} for d in (weights, mom, var))

    core = lax.axis_index("c").astype(jnp.int32).reshape(1)
    chip = (2 * lax.axis_index("x") + lax.axis_index("y")).astype(jnp.int32).reshape(1)
    small = {n: weights[n] for n, _ in SMALL_NAMES}

    win4, small4 = _allgather_shards(
        [weights["w_in"].astype(BF16), _pack_small_weights(weights["gla_gate_up"], weights["conv_w"])], FIRST_UNIT_ROWS)
    gate_full, convw_full = _unpack_small_weights(small4)
    first = {"wp": _permute_w_in(jnp.transpose(win4, (1, 0, 2)).reshape(D_MODEL, IN_WIDTH)),
             "gla_gate_up": gate_full, "conv_w": convw_full}
    late_shards = [weights[n].astype(BF16) for n in LATE_NAMES]
    gathering = _ici_start(late_shards, [lax.empty((N_SHARD,) + s.shape, BF16) for s in late_shards], _gather_plan, "gather_start")
    small["pre_mix_norm"] = small["pre_mix_norm"] + gathering[-1][0, 0]

    def late_weights(after):
        wout4, wup4, wdown4 = _gather_finish(*_ici_wait(gathering, after, _gather_plan, "gather_wait"))
        return wout4.reshape(D_MODEL, D_MODEL), wup4, wdown4.reshape(D_FF, D_MODEL)

    early = {}
    delta, new_m, new_v = {}, {}, {}

    def scatter_start(make_a, make_b, i, j, name):
        a = make_a()
        swapping = _ici_start([a], [lax.empty((N_SHARD, a.shape[1] // 2, a.shape[2]), F32)], _swap_plan,
                              "rs_swap_start_" + BIG_NAMES[i])
        b = make_b(swapping[-1])
        (a,), (got_a,) = _ici_wait(swapping, b, _swap_plan, "rs_swap_wait_" + BIG_NAMES[i])
        got_b = _rs_pair_swap([b], [PIECE_ROWS[j]], "rs_pair_swap_" + BIG_NAMES[j])[0]
        parts = [_rs_add_pair(g, got, core, ADD_ROWS[k], "rs_add_pair_" + BIG_NAMES[k])
                 for g, got, k in ((a, got_a, i), (b, got_b, j))]
        lands = [_place_own(p, chip, ADD_ROWS[k], "rs_place_own_" + BIG_NAMES[k]) for p, k in zip(parts, (i, j))]
        return _ici_start(parts, lands, _scatter_plan, name)

    def finish(landed, which, grads_out):
        names = [BIG_NAMES[i] for i in which]
        halves = [_rs_sum_chips(p, ADD_ROWS[i], "rs_sum_chips_" + BIG_NAMES[i]) for p, i in zip(landed, which)]
        reduced = _rs_pair_share(halves, [PIECE_ROWS[i] for i in which], "rs_pair_share_" + names[0])
        for n, g in zip(names, reduced):
            delta[n], new_m[n], new_v[n], grads_out[n] = _adamw(weights[n], g, mom[n], var[n], ADAM_ROWS[n], "adamw_" + n)

    def early_grads(make_up4, make_down):
        early["scatter"] = scatter_start(make_up4, lambda after: make_down(after).reshape(N_SHARD, D_FF // N_SHARD, D_MODEL), 2, 3,
                                         "rs_scatter_start")
        return early["scatter"][-1][0, 0]

    def last_grads(make_wp, make_out):
        return scatter_start(
            lambda: jnp.transpose(_unpermute_w_in(make_wp()).reshape(D_MODEL, N_SHARD, IN_WIDTH // N_SHARD), (1, 0, 2)),
            lambda after: make_out(after).reshape(N_SHARD, D_MODEL // N_SHARD, D_MODEL), 0, 1, "rs_scatter_start_rest")

    loss, dx, grads, small_grads, sink_rows = _local_step(
        x[0], positions[0], loss_target[0], first, small, late_weights, early_grads, last_grads)

    late_scatter = grads["last"]
    g_all = {}
    finish(_ici_wait(early["scatter"], late_scatter[-1], _scatter_plan, "rs_scatter_wait")[1], (2, 3), g_all)
    finish(_ici_wait(late_scatter, delta["w_down"], _scatter_plan, "rs_scatter_wait_rest")[1], (0, 1), g_all)
    vectors = [n for n, _ in SMALL_NAMES if n != "swa_sinks"]
    summed = _allreduce_small([small_grads[n] for n in vectors] + [grads["gla_gate_up"], grads["conv_w"]], sink_rows, loss)
    g_all.update(zip(vectors, summed))
    g_all.update({"swa_sinks": summed[-2],
                  "gla_gate_up": lax.dynamic_slice_in_dim(summed[-4], chip[0] * GATE_SHARD[1], GATE_SHARD[1], axis=1),
                  "conv_w": lax.dynamic_slice_in_dim(summed[-3], chip[0] * SHARD_FF, SHARD_FF, axis=1)})
    loss_sum = summed[-1][0, 0]

    tiny = [n for n, _ in SMALL_NAMES] + ["gla_gate_up", "conv_w"]
    for res, vals in zip((delta, new_m, new_v, g_all), _adamw_small(*([d[n] for n in tiny] for d in (weights, g_all, mom, var)))):
        res.update(zip(tiny, vals))

    def lead(n, a):
        return a[None] if n in MATRIX_NAMES else a

    outs = [loss_sum, dx[None]]
    for d in (g_all, delta, new_m, new_v):
        outs.extend(lead(n, d[n]) for n in WEIGHT_ORDER)
    return tuple(outs)
```

```python
import functools

import jax
import jax.numpy as jnp
from jax import lax
from jax.experimental import pallas as pl
from jax.experimental.pallas import tpu as pltpu

F32 = jnp.float32
BF16 = jnp.bfloat16
MESH = pl.DeviceIdType.MESH

D_MODEL = 1024
GLA_HEADS = 4
GLA_DK = 64
GLA_DV = 128
GLA_TAU = 16.0
GLA_CHUNK = 64
SWA_HEADS = 8
SWA_HD = 64
SWA_BLOCK = 128
ROPE_THETA = 500000.0
ROPE_DIM = 16
D_FF = 2816
EPS = 1e-6
GLA_QK = 256
GLA_V = 512
SWA_Q = 512
SWA_KV = 128
IN_WIDTH = 2320
IN_WIDTH_PAD = 2432
N_SHARD = 4

ADAM_LR = 0.001
ADAM_B1 = 0.9
ADAM_B2 = 0.999
ADAM_EPS = 1e-08
ADAM_WD = 0.01
ADAM_STEP = 10

LANES = 128
VMEM_LIMIT = 56 * 1024 * 1024
TM = 256
SHARD_FF = 2 * D_FF // N_SHARD
FF_PIECES = ((0, 1408),)
GLA_BLOCK = 256


def _wide_tile(T):
    return 2 * TM if T % (2 * TM) == 0 else TM


def _params(**kw):
    return pltpu.CompilerParams(vmem_limit_bytes=VMEM_LIMIT, **kw)


def _mm(a, b):
    return lax.dot_general(a.astype(BF16), b.astype(BF16), (((1,), (0,)), ((), ())), preferred_element_type=F32)


def _mm_nt(a, b):
    return lax.dot_general(a.astype(BF16), b.astype(BF16), (((1,), (1,)), ((), ())), preferred_element_type=F32)


def _mm_tn(a, b):
    return lax.dot_general(a.astype(BF16), b.astype(BF16), (((0,), (0,)), ((), ())), preferred_element_type=F32)


def _mm_f32(a, b):
    return lax.dot_general(a, b, (((1,), (0,)), ((), ())), preferred_element_type=F32, precision=lax.Precision.HIGHEST)


def _iota(shape, dim):
    return lax.broadcasted_iota(jnp.int32, shape, dim)


def _sigmoid(x):
    return 1.0 / (1.0 + jnp.exp(-x))


def _gelu_parts(x):
    c = 0.7978845608028654
    x2 = x * x
    t = jnp.tanh(c * (x + 0.044715 * (x2 * x)))
    cdf = 0.5 * (1.0 + t)
    dcdf = 0.5 * (1.0 - t * t) * c * (1.0 + 3.0 * 0.044715 * x2)
    return x * cdf, cdf + x * dcdf


def _rms_bwd(v, r, g, dout):
    gd = g * dout
    return r * gd - v * (r * r * r) * jnp.mean(v * gd, axis=-1, keepdims=True)


def _row_spec(tm, cols):
    return pl.BlockSpec((tm, cols), lambda i: (i, 0))


def _const_spec(shape):
    return pl.BlockSpec(shape, lambda i: (0,) * len(shape))


def _any_spec():
    return pl.BlockSpec(memory_space=pl.ANY)


def _load_once(src_hbm, dst_vmem, sem):
    @pl.when(pl.program_id(0) == 0)
    def _():
        cp = pltpu.make_async_copy(src_hbm, dst_vmem, sem)
        cp.start()
        cp.wait()


def _rotate(v, rc, rsa, rsb):
    return v * rc + pltpu.roll(v, 120, 1) * rsa + pltpu.roll(v, 8, 1) * rsb


def _rotate_bwd(dv, rc, rsa, rsb):
    return dv * rc + pltpu.roll(dv * rsa, 8, 1) + pltpu.roll(dv * rsb, 120, 1)


def _proj_fwd(x, g1, wp, gup, gbias, rc, rsa, rsb):
    T = x.shape[0]
    TM = _wide_tile(T)

    def body(x_ref, g1_ref, wp_hbm, gup_ref, gb_ref, rc_ref, rsa_ref, rsb_ref,
             h1_ref, q_ref, k_ref, v_ref, la_ref, gg_ref, sq_ref, kd_ref, vd_ref, glr_ref, wp_v, sem):
        _load_once(wp_hbm, wp_v, sem)
        xt = x_ref[...]
        r = lax.rsqrt(jnp.mean(xt * xt, axis=-1, keepdims=True) + EPS)
        h = (xt * r * g1_ref[...]).astype(BF16)
        h1_ref[...] = h
        q_ref[...] = _mm(h, wp_v[:, 0:256])
        k_ref[...] = _mm(h, wp_v[:, 256:512])
        v_ref[...] = _mm(h, wp_v[:, 512:1024]).astype(BF16)
        gg_ref[...] = _mm(h, wp_v[:, 1024:1536]).astype(BF16)
        glr = _mm(h, wp_v[:, 2304:2432]).astype(BF16)
        glr_ref[...] = glr
        z = _mm(glr, gup_ref[...]) + gb_ref[...]
        la_ref[...] = (jnp.minimum(z, 0.0) - jnp.log1p(jnp.exp(-jnp.abs(z)))) * (1.0 / GLA_TAU)
        rc_, rsa_, rsb_ = rc_ref[...], rsa_ref[...], rsb_ref[...]
        for s in range(4):
            qs = _mm(h, wp_v[:, 1536 + 128 * s:1664 + 128 * s])
            sq_ref[:, 128 * s:128 * s + 128] = (_rotate(qs, rc_, rsa_, rsb_) * 0.125).astype(BF16)
        lane = _iota((TM, LANES), 1)
        first = lane < 64
        kr = _rotate(_mm(h, wp_v[:, 2048:2176]), rc_, rsa_, rsb_)
        krr = pltpu.roll(kr, 64, 1)
        kd_ref[:, 0:128] = jnp.where(first, kr, krr).astype(BF16)
        kd_ref[:, 128:256] = jnp.where(first, krr, kr).astype(BF16)
        vr = _mm(h, wp_v[:, 2176:2304])
        vrr = pltpu.roll(vr, 64, 1)
        vd_ref[:, 0:128] = jnp.where(first, vr, vrr).astype(BF16)
        vd_ref[:, 128:256] = jnp.where(first, vrr, vr).astype(BF16)

    outs = [
        jax.ShapeDtypeStruct((T, D_MODEL), BF16),
        jax.ShapeDtypeStruct((T, GLA_QK), F32),
        jax.ShapeDtypeStruct((T, GLA_QK), F32),
        jax.ShapeDtypeStruct((T, GLA_V), BF16),
        jax.ShapeDtypeStruct((T, GLA_QK), F32),
        jax.ShapeDtypeStruct((T, GLA_V), BF16),
        jax.ShapeDtypeStruct((T, SWA_Q), BF16),
        jax.ShapeDtypeStruct((T, 256), BF16),
        jax.ShapeDtypeStruct((T, 256), BF16),
        jax.ShapeDtypeStruct((T, LANES), BF16),
    ]
    return pl.pallas_call(
        body, name="proj_fwd", grid=(T // TM,), out_shape=outs,
        in_specs=[_row_spec(TM, D_MODEL), _const_spec((1, D_MODEL)), _any_spec(), _const_spec((LANES, GLA_QK)),
                  _const_spec((1, GLA_QK)), _row_spec(TM, LANES), _row_spec(TM, LANES), _row_spec(TM, LANES)],
        out_specs=[_row_spec(TM, o.shape[1]) for o in outs],
        scratch_shapes=[pltpu.VMEM((D_MODEL, IN_WIDTH_PAD), BF16), pltpu.SemaphoreType.DMA],
        compiler_params=_params(),
    )(x, g1, wp, gup, gbias, rc, rsa, rsb)


GLA_NB = GLA_BLOCK // GLA_CHUNK


def _gla_masks():
    n = GLA_BLOCK
    lane = _iota((n, LANES), 1)
    lane_masks = [(lane < 64).astype(F32), (lane >= 64).astype(F32)]
    row, col = _iota((n, n), 0), _iota((n, n), 1)
    same_chunk = (row >> 6) == (col >> 6)
    blk = ((_iota((256, LANES), 0) >> 7) == (_iota((256, LANES), 1) >> 6)).astype(F32)
    return lane_masks, same_chunk & (col <= row), same_chunk & (col >= row), blk


def _chunk_rows(vals):
    return jnp.concatenate([jnp.broadcast_to(v, (GLA_CHUNK, LANES)) for v in vals], axis=0)


def _gla_block_terms(q_ref, k_ref, b_ref, p):
    C = GLA_CHUNK
    cols = slice(LANES * p, LANES * p + LANES)
    bc = b_ref[:, cols]
    bl_rows = [b_ref[C * c + C - 1:C * c + C, cols] for c in range(GLA_NB)]
    bl = _chunk_rows(bl_rows)
    bm = _chunk_rows([b_ref[C * c + C // 2 - 1:C * c + C // 2, cols] for c in range(GLA_NB)])
    qs = q_ref[:, cols] * 0.125
    kk = k_ref[:, cols]
    eb = jnp.exp(bc)
    ekl = jnp.exp(bl - bc)
    eqm = jnp.exp(bc - bm)
    ekm = jnp.exp(bm - bc)
    return qs, kk, eb, ekl, eqm, ekm, [jnp.exp(r) for r in bl_rows]


def _block_cumsum(la, mask):
    return _mm_f32(mask.astype(F32), la)


def _gla_fwd(q, k, v, la):
    T = q.shape[0]
    NB = GLA_BLOCK // GLA_CHUNK
    C = GLA_CHUNK

    def body(q_ref, k_ref, v_ref, la_ref, o_ref, s_ref, st_ref, b_ref):
        @pl.when(pl.program_id(0) == 0)
        def _():
            st_ref[...] = jnp.zeros_like(st_ref)

        lane_masks, causal, _, blk = _gla_masks()
        b_ref[...] = _block_cumsum(la_ref[...], causal)
        for p in range(2):
            qs, kk, eb, ekl, eqm, ekm, gam = _gla_block_terms(q_ref, k_ref, b_ref, p)
            qh, kh, qm, km = qs * eb, kk * ekl, qs * eqm, kk * ekm
            vp = v_ref[:, 256 * p:256 * p + 256]
            intra = []
            for j in range(2):
                a = jnp.where(causal, _mm_nt(qm * lane_masks[j], km), 0.0)
                intra.append(_mm(a, vp[:, 128 * j:128 * j + 128]))
            kv = [blk * _mm_tn(vp[C * c:C * c + C], kh[C * c:C * c + C]) for c in range(NB)]
            st = st_ref[p]
            inter = []
            for c in range(NB):
                s_ref[c, p] = st[0:LANES] + st[LANES:2 * LANES]
                inter.append(_mm_nt(qh[C * c:C * c + C], st))
                st = st * gam[c] + kv[c]
            st_ref[p] = st
            o_ref[:, 256 * p:256 * p + 256] = (jnp.concatenate(inter, axis=0) + jnp.concatenate(intra, axis=1)).astype(BF16)

    return pl.pallas_call(
        body, name="gla_fwd", grid=(T // GLA_BLOCK,),
        out_shape=[jax.ShapeDtypeStruct((T, GLA_V), BF16), jax.ShapeDtypeStruct((T // C, 2, LANES, LANES), F32)],
        in_specs=[_row_spec(GLA_BLOCK, GLA_QK), _row_spec(GLA_BLOCK, GLA_QK), _row_spec(GLA_BLOCK, GLA_V),
                  _row_spec(GLA_BLOCK, GLA_QK)],
        out_specs=[_row_spec(GLA_BLOCK, GLA_V), pl.BlockSpec((NB, 2, LANES, LANES), lambda i: (i, 0, 0, 0))],
        scratch_shapes=[pltpu.VMEM((2, 256, LANES), F32), pltpu.VMEM((GLA_BLOCK, GLA_QK), F32)],
        compiler_params=_params(),
    )(q, k, v, la)


def _gla_bwd(q, k, v, la, s_all, do):
    T = q.shape[0]
    NB = GLA_BLOCK // GLA_CHUNK
    C = GLA_CHUNK
    nblk = T // GLA_BLOCK

    def body(q_ref, k_ref, v_ref, la_ref, s_ref, do_ref, dq_ref, dk_ref, dv_ref, dz_ref, dst_ref, b_ref):
        @pl.when(pl.program_id(0) == 0)
        def _():
            dst_ref[...] = jnp.zeros_like(dst_ref)

        lane_masks, causal, anti_causal, blk = _gla_masks()
        b_ref[...] = _block_cumsum(la_ref[...], causal)
        for p in range(2):
            cols = slice(LANES * p, LANES * p + LANES)
            qs, kk, eb, ekl, eqm, ekm, gam = _gla_block_terms(q_ref, k_ref, b_ref, p)
            qh, kh, qm, km = qs * eb, kk * ekl, qs * eqm, kk * ekm
            vp = v_ref[:, 256 * p:256 * p + 256]
            dop = do_ref[:, 256 * p:256 * p + 256]
            dqm = jnp.zeros((GLA_BLOCK, LANES), F32)
            dkm = jnp.zeros((GLA_BLOCK, LANES), F32)
            dv_intra = []
            for j in range(2):
                hs = slice(128 * j, 128 * j + 128)
                a = jnp.where(causal, _mm_nt(qm * lane_masks[j], km), 0.0)
                da = jnp.where(causal, _mm_nt(dop[:, hs], vp[:, hs]), 0.0)
                dv_intra.append(_mm_tn(a, dop[:, hs]))
                dqm = dqm + lane_masks[j] * _mm(da, km)
                dkm = dkm + lane_masks[j] * _mm_tn(da, qm)
            grow = [blk * _mm_tn(dop[C * c:C * c + C], qh[C * c:C * c + C]) for c in range(NB)]
            dst = dst_ref[p]
            dst_after = [None] * NB
            for c in reversed(range(NB)):
                dst_after[c] = dst
                dst = dst * gam[c] + grow[c]
            dst_ref[p] = dst
            dqh, dkh, dv_state, extra = [], [], [], []
            for c in range(NB):
                rows = slice(C * c, C * c + C)
                packed = s_ref[c, p]
                st = jnp.concatenate([packed * lane_masks[0][0:LANES], packed * lane_masks[1][0:LANES]], axis=0)
                dqh.append(_mm(dop[rows], st))
                dkh.append(_mm(vp[rows], dst_after[c]))
                dv_state.append(_mm_nt(kh[rows], dst_after[c]))
                extra.append(jnp.sum(dkh[c] * kh[rows], axis=0, keepdims=True)
                             + jnp.sum(st * dst_after[c], axis=0, keepdims=True) * gam[c])
            dqs = jnp.concatenate(dqh, axis=0) * eb + dqm * eqm
            dk = jnp.concatenate(dkh, axis=0) * ekl + dkm * ekm
            dg = _mm_f32(anti_causal.astype(F32), dqs * qs - dk * kk) + _chunk_rows(extra)
            dq_ref[:, cols] = (dqs * 0.125).astype(BF16)
            dk_ref[:, cols] = dk.astype(BF16)
            dz_ref[:, cols] = dg * (1.0 - jnp.exp(GLA_TAU * la_ref[:, cols])) * (1.0 / GLA_TAU)
            dv_ref[:, 256 * p:256 * p + 256] = (jnp.concatenate(dv_state, axis=0) + jnp.concatenate(dv_intra, axis=1)).astype(BF16)

    rev = lambda i: (nblk - 1 - i, 0)
    rspec = lambda cols: pl.BlockSpec((GLA_BLOCK, cols), rev)
    return pl.pallas_call(
        body, name="gla_bwd", grid=(nblk,),
        out_shape=[jax.ShapeDtypeStruct((T, GLA_QK), BF16), jax.ShapeDtypeStruct((T, GLA_QK), BF16),
                   jax.ShapeDtypeStruct((T, GLA_V), BF16), jax.ShapeDtypeStruct((T, GLA_QK), F32)],
        in_specs=[rspec(GLA_QK), rspec(GLA_QK), rspec(GLA_V), rspec(GLA_QK),
                  pl.BlockSpec((NB, 2, LANES, LANES), lambda i: (nblk - 1 - i, 0, 0, 0)), rspec(GLA_V)],
        out_specs=[rspec(GLA_QK), rspec(GLA_QK), rspec(GLA_V), rspec(GLA_QK)],
        scratch_shapes=[pltpu.VMEM((2, 256, LANES), F32), pltpu.VMEM((GLA_BLOCK, GLA_QK), F32)],
        compiler_params=_params(),
    )(q, k, v, la, s_all, do)


SWA_GROUP = 4


def _swa_stack(ref, g, first):
    parts = []
    for j in range(SWA_GROUP):
        m = 2 * g + j // 2
        pair = ref[:, 128 * m:128 * m + 128]
        zero = jnp.zeros_like(pair)
        parts.append(jnp.where(first, pair, zero) if j % 2 == 0 else jnp.where(first, zero, pair))
    return jnp.concatenate(parts, axis=0)


def _swa_unstack(rows, mm, first):
    W = SWA_BLOCK
    return jnp.where(first, rows[W * 2 * mm:W * (2 * mm + 1)], rows[W * (2 * mm + 1):W * (2 * mm + 2)])


def _swa_probs(qs, kp, kc, vp, vc, i, g, sink_ref, first4):
    W = SWA_BLOCK
    R = SWA_GROUP * W
    r, c = _iota((R, W), 0) & (W - 1), _iota((R, W), 1)
    neg = -1e30
    s_p = jnp.where((c > r) & (i > 0), _mm_nt(qs, kp), neg)
    s_c = jnp.where(c <= r, _mm_nt(qs, kc), neg)
    head = _iota((R, 1), 0) >> 7
    sink = jnp.where(head == 0, sink_ref[4 * g], jnp.where(head == 1, sink_ref[4 * g + 1],
                                                           jnp.where(head == 2, sink_ref[4 * g + 2], sink_ref[4 * g + 3])))
    m = jnp.maximum(jnp.max(jnp.maximum(s_p, s_c), axis=-1, keepdims=True), sink)
    p_p = jnp.exp(s_p - m)
    p_c = jnp.exp(s_c - m)
    p_s = jnp.exp(sink - m)
    one = jnp.ones((W, LANES), BF16)
    first = _iota((W, LANES), 1) < 64
    acc = _mm(p_p, jnp.where(first, vp, one)) + _mm(p_c, jnp.where(first, vc, one))
    rolled = pltpu.roll(acc, 64, 1)
    denom = jnp.where(first4, rolled, acc) + p_s
    return p_p, p_c, p_s, denom, acc, rolled


def _swa_fwd(sq, kd, vd, sinks):
    T = sq.shape[0]
    W = SWA_BLOCK
    prev = lambda i: (jnp.maximum(i - 1, 0), 0)

    def body(sink_ref, q_ref, kp_ref, kc_ref, vp_ref, vc_ref, o_ref):
        i = pl.program_id(0)
        first4 = _iota((SWA_GROUP * W, LANES), 1) < 64
        first = _iota((W, LANES), 1) < 64
        for g in range(2):
            gs = slice(128 * g, 128 * g + 128)
            qs = _swa_stack(q_ref, g, first)
            _, _, _, denom, acc, rolled = _swa_probs(qs, kp_ref[:, gs], kc_ref[:, gs], vp_ref[:, gs], vc_ref[:, gs],
                                                     i, g, sink_ref, first4)
            pv = jnp.where(first4, acc, rolled)
            o = pv / denom
            for mm in range(2):
                m = 2 * g + mm
                o_ref[:, 128 * m:128 * m + 128] = _swa_unstack(o, mm, first).astype(BF16)

    return pl.pallas_call(
        body, name="swa_fwd", grid=(T // W,), out_shape=jax.ShapeDtypeStruct((T, SWA_Q), BF16),
        in_specs=[pl.BlockSpec(memory_space=pltpu.SMEM), _row_spec(W, SWA_Q), pl.BlockSpec((W, 256), prev),
                  _row_spec(W, 256), pl.BlockSpec((W, 256), prev), _row_spec(W, 256)],
        out_specs=_row_spec(W, SWA_Q),
        compiler_params=_params(),
    )(sinks, sq, kd, kd, vd, vd)


def _swa_bwd(sq, kd, vd, sinks, do):
    T = sq.shape[0]
    W = SWA_BLOCK
    n = T // W
    cur = lambda i: (jnp.minimum(i, n - 1), 0)
    prev = lambda i: (jnp.clip(i - 1, 0, n - 1), 0)

    def body(sink_ref, q_ref, kp_ref, kc_ref, vp_ref, vc_ref, do_ref, dq_ref, dk_ref, dv_ref, ds_ref, ck_ref, cv_ref):
        i = pl.program_id(0)

        @pl.when(i == 0)
        def _():
            ds_ref[...] = jnp.zeros_like(ds_ref)
            ck_ref[...] = jnp.zeros_like(ck_ref)
            cv_ref[...] = jnp.zeros_like(cv_ref)

        @pl.when(i < n)
        def _():
            first4 = _iota((SWA_GROUP * W, LANES), 1) < 64
            first = _iota((W, LANES), 1) < 64
            for g in range(2):
                gs = slice(128 * g, 128 * g + 128)
                kp, kc, vp, vc = kp_ref[:, gs], kc_ref[:, gs], vp_ref[:, gs], vc_ref[:, gs]
                qs = _swa_stack(q_ref, g, first)
                dos = _swa_stack(do_ref, g, first)
                p_p, p_c, p_s, denom, _, _ = _swa_probs(qs, kp, kc, vp, vc, i, g, sink_ref, first4)
                inv = 1.0 / denom
                p_p, p_c = p_p * inv, p_c * inv
                dp_p = _mm_nt(dos, vp)
                dp_c = _mm_nt(dos, vc)
                delta = jnp.sum(p_p * dp_p + p_c * dp_c, axis=-1, keepdims=True)
                ds_p = p_p * (dp_p - delta)
                ds_c = p_c * (dp_c - delta)
                rows = slice(SWA_GROUP * W * g, SWA_GROUP * W * (g + 1))
                ds_ref[rows, :] = ds_ref[rows, :] - (p_s * delta) * inv
                dq = (_mm(ds_p, kp) + _mm(ds_c, kc)) * 0.125
                for mm in range(2):
                    m = 2 * g + mm
                    dq_ref[:, 128 * m:128 * m + 128] = _swa_unstack(dq, mm, first).astype(BF16)
                dk_ref[:, gs] = (ck_ref[:, gs] + _mm_tn(ds_p, qs)).astype(BF16)
                dv_ref[:, gs] = (cv_ref[:, gs] + _mm_tn(p_p, dos)).astype(BF16)
                ck_ref[:, gs] = _mm_tn(ds_c, qs)
                cv_ref[:, gs] = _mm_tn(p_c, dos)

        @pl.when(i == n)
        def _():
            dk_ref[...] = ck_ref[...].astype(BF16)
            dv_ref[...] = cv_ref[...].astype(BF16)

    return pl.pallas_call(
        body, name="swa_bwd", grid=(n + 1,),
        out_shape=[jax.ShapeDtypeStruct((T, SWA_Q), BF16), jax.ShapeDtypeStruct((T, 256), BF16),
                   jax.ShapeDtypeStruct((T, 256), BF16), jax.ShapeDtypeStruct((SWA_HEADS * W, LANES), F32)],
        in_specs=[pl.BlockSpec(memory_space=pltpu.SMEM), pl.BlockSpec((W, SWA_Q), cur), pl.BlockSpec((W, 256), prev),
                  pl.BlockSpec((W, 256), cur), pl.BlockSpec((W, 256), prev), pl.BlockSpec((W, 256), cur),
                  pl.BlockSpec((W, SWA_Q), cur)],
        out_specs=[pl.BlockSpec((W, SWA_Q), cur), pl.BlockSpec((W, 256), prev), pl.BlockSpec((W, 256), prev),
                   _const_spec((SWA_HEADS * W, LANES))],
        scratch_shapes=[pltpu.VMEM((W, 256), F32), pltpu.VMEM((W, 256), F32)],
        compiler_params=_params(),
    )(sinks, sq, kd, kd, vd, vd, do)


def _mix_out_fwd(x, og, gg, osw, gnorm, wout, g2):
    T = x.shape[0]
    TM = _wide_tile(T)

    def body(x_ref, og_ref, gg_ref, osw_ref, gn_ref, wout_ref, g2_ref, x1_ref, cat_ref):
        gn = gn_ref[...]
        for j in range(GLA_HEADS):
            hs = slice(128 * j, 128 * j + 128)
            o = og_ref[:, hs].astype(F32)
            r = lax.rsqrt(jnp.mean(o * o, axis=-1, keepdims=True) + EPS)
            gate = gg_ref[:, hs].astype(F32)
            cat_ref[:, hs] = (o * r * gn * (gate * _sigmoid(gate))).astype(BF16)
        cat_ref[:, GLA_V:] = osw_ref[...]
        mix = _mm(cat_ref[...], wout_ref[...])
        r2 = lax.rsqrt(jnp.mean(mix * mix, axis=-1, keepdims=True) + EPS)
        x1_ref[...] = x_ref[...] + mix * r2 * g2_ref[...]

    return pl.pallas_call(
        body, name="mix_out_fwd", grid=(T // TM,),
        out_shape=[jax.ShapeDtypeStruct((T, D_MODEL), F32), jax.ShapeDtypeStruct((T, D_MODEL), BF16)],
        in_specs=[_row_spec(TM, D_MODEL), _row_spec(TM, GLA_V), _row_spec(TM, GLA_V), _row_spec(TM, SWA_Q),
                  _const_spec((1, LANES)), _const_spec((D_MODEL, D_MODEL)), _const_spec((1, D_MODEL))],
        out_specs=[_row_spec(TM, D_MODEL), _row_spec(TM, D_MODEL)],
        compiler_params=_params(),
    )(x, og, gg, osw, gnorm, wout, g2)


HALO = 8


def _rows_before(v, prev1, prev2):
    row = _iota(v.shape, 0)
    m1 = jnp.where(row == 0, prev1, pltpu.roll(v, 1, 0))
    m2 = jnp.where(row == 0, prev2, jnp.where(row == 1, prev1, pltpu.roll(v, 2, 0)))
    return m1, m2


def _rows_after(v, next1, next2):
    n = v.shape[0]
    row = _iota(v.shape, 0)
    p1 = jnp.where(row == n - 1, next1, pltpu.roll(v, n - 1, 0))
    p2 = jnp.where(row == n - 1, next2, jnp.where(row == n - 2, next1, pltpu.roll(v, n - 2, 0)))
    return p1, p2


def _ff_pieces():
    return [(j, off, wd) for j in range(2) for off, wd in FF_PIECES]


def _ffn_fwd(x1, g3, wup, cw, cb, wdown, g4, target):
    T = x1.shape[0]

    def body(x1_ref, g3_ref, wup_hbm, cw_ref, cb_ref, wdn_hbm, g4_ref, tg_ref,
             h2_ref, up_ref, a_ref, c1_ref, c2_ref, y_ref, dx2_ref, loss_ref, wup_v, wdn_v, carry_ref, sems):
        _load_once(wup_hbm, wup_v, sems.at[0])
        _load_once(wdn_hbm, wdn_v, sems.at[1])

        @pl.when(pl.program_id(0) == 0)
        def _():
            carry_ref[...] = jnp.zeros_like(carry_ref)
            loss_ref[...] = jnp.zeros_like(loss_ref)

        x1 = x1_ref[...]
        r3 = lax.rsqrt(jnp.mean(x1 * x1, axis=-1, keepdims=True) + EPS)
        h2 = (x1 * r3 * g3_ref[...]).astype(BF16)
        h2_ref[...] = h2
        for j, off, wd in _ff_pieces():
            base = SHARD_FF * j + off
            u = []
            for half in range(2):
                cs = slice(D_FF * half + base, D_FF * half + base + wd)
                upb = _mm(h2, wup_v[2 * half + j, :, off:off + wd]).astype(BF16)
                up_ref[:, cs] = upb
                upf = upb.astype(F32)
                m1, m2 = _rows_before(upf, carry_ref[HALO - 1:HALO, cs], carry_ref[HALO - 2:HALO - 1, cs])
                u.append(cb_ref[:, cs] + cw_ref[0:1, cs] * m2 + cw_ref[1:2, cs] * m1 + cw_ref[2:3, cs] * upf)
                carry_ref[:, cs] = upf[TM - HALO:TM, :]
            act, dact = _gelu_parts(u[1])
            a = (act * u[0]).astype(BF16)
            out = slice(base, base + wd)
            a_ref[:, out] = a
            c1_ref[:, out] = act.astype(BF16)
            c2_ref[:, out] = (u[0] * dact).astype(BF16)
        y = _mm(a_ref[...], wdn_v[...])
        y_ref[...] = y
        r4 = lax.rsqrt(jnp.mean(y * y, axis=-1, keepdims=True) + EPS)
        err = x1 + y * r4 * g4_ref[...] - tg_ref[...]
        dx2_ref[...] = err * (1.0 / D_MODEL)
        loss_ref[...] = loss_ref[...] + jnp.sum(err * err) * (0.5 / D_MODEL)

    outs = [
        jax.ShapeDtypeStruct((T, D_MODEL), BF16),
        jax.ShapeDtypeStruct((T, 2 * D_FF), BF16),
        jax.ShapeDtypeStruct((T, D_FF), BF16),
        jax.ShapeDtypeStruct((T, D_FF), BF16),
        jax.ShapeDtypeStruct((T, D_FF), BF16),
        jax.ShapeDtypeStruct((T, D_MODEL), F32),
        jax.ShapeDtypeStruct((T, D_MODEL), F32),
        jax.ShapeDtypeStruct((8, LANES), F32),
    ]
    return pl.pallas_call(
        body, name="ffn_fwd", grid=(T // TM,), out_shape=outs,
        in_specs=[_row_spec(TM, D_MODEL), _const_spec((1, D_MODEL)), _any_spec(), _const_spec((3, 2 * D_FF)),
                  _const_spec((1, 2 * D_FF)), _any_spec(), _const_spec((1, D_MODEL)), _row_spec(TM, D_MODEL)],
        out_specs=[_row_spec(TM, D_MODEL), _row_spec(TM, 2 * D_FF), _row_spec(TM, D_FF), _row_spec(TM, D_FF),
                   _row_spec(TM, D_FF), _row_spec(TM, D_MODEL), _row_spec(TM, D_MODEL), _const_spec((8, LANES))],
        scratch_shapes=[pltpu.VMEM((N_SHARD, D_MODEL, SHARD_FF), BF16), pltpu.VMEM((D_FF, D_MODEL), BF16),
                        pltpu.VMEM((HALO, 2 * D_FF), F32), pltpu.SemaphoreType.DMA((2,))],
        compiler_params=_params(),
    )(x1, g3, wup, cw, cb, wdown, g4, target)


def _ffn_bwd(dx2, y, g4, up, c1, c2, cw, wdown, wup, x1, g3):
    T = dx2.shape[0]
    nt = T // TM
    rev = lambda i: (nt - 1 - i, 0)

    def body(dn_ref, y_ref, g4_ref, up_ref, c1_ref, c2_ref, cw_ref, wdn_hbm, wup_hbm, x1_ref, g3_ref,
             dy_ref, dup_ref, dx1_ref, dg4_ref, dg3_ref, dcb_ref, dcw_ref, wup_v, wdn_v, carry_ref, sems):
        _load_once(wup_hbm, wup_v, sems.at[0])
        _load_once(wdn_hbm, wdn_v, sems.at[1])

        @pl.when(pl.program_id(0) == 0)
        def _():
            carry_ref[...] = jnp.zeros_like(carry_ref)
            dg4_ref[...] = jnp.zeros_like(dg4_ref)
            dg3_ref[...] = jnp.zeros_like(dg3_ref)
            dcb_ref[...] = jnp.zeros_like(dcb_ref)
            dcw_ref[...] = jnp.zeros_like(dcw_ref)

        dn = dn_ref[...]
        y = y_ref[...]
        g4v = g4_ref[...]
        r4 = lax.rsqrt(jnp.mean(y * y, axis=-1, keepdims=True) + EPS)
        dg4_ref[...] = dg4_ref[...] + jnp.sum(dn * y * r4, axis=0, keepdims=True)
        dy = _rms_bwd(y, r4, g4v, dn).astype(BF16)
        dy_ref[...] = dy
        dh2 = jnp.zeros((TM, D_MODEL), F32)
        for j, off, wd in _ff_pieces():
            base = SHARD_FF * j + off
            da = _mm_nt(dy, wdn_v[base:base + wd, :])
            for half, coef_ref in enumerate((c1_ref, c2_ref)):
                cs = slice(D_FF * half + base, D_FF * half + base + wd)
                du = da * coef_ref[:, base:base + wd].astype(F32)
                p1, p2 = _rows_after(du, carry_ref[0:1, cs], carry_ref[1:2, cs])
                carry_ref[:, cs] = du[0:HALO, :]
                upf = up_ref[:, cs].astype(F32)
                dcb_ref[:, cs] = dcb_ref[:, cs] + jnp.sum(du, axis=0, keepdims=True)
                dcw_ref[0:1, cs] = dcw_ref[0:1, cs] + jnp.sum(p2 * upf, axis=0, keepdims=True)
                dcw_ref[1:2, cs] = dcw_ref[1:2, cs] + jnp.sum(p1 * upf, axis=0, keepdims=True)
                dcw_ref[2:3, cs] = dcw_ref[2:3, cs] + jnp.sum(du * upf, axis=0, keepdims=True)
                dup = (cw_ref[2:3, cs] * du + cw_ref[1:2, cs] * p1 + cw_ref[0:1, cs] * p2).astype(BF16)
                dup_ref[:, cs] = dup
                dh2 = dh2 + _mm_nt(dup, wup_v[2 * half + j, :, off:off + wd])
        x1 = x1_ref[...]
        r3 = lax.rsqrt(jnp.mean(x1 * x1, axis=-1, keepdims=True) + EPS)
        dg3_ref[...] = dg3_ref[...] + jnp.sum(dh2 * x1 * r3, axis=0, keepdims=True)
        dx1_ref[...] = dn + _rms_bwd(x1, r3, g3_ref[...], dh2)

    outs = [
        jax.ShapeDtypeStruct((T, D_MODEL), BF16),
        jax.ShapeDtypeStruct((T, 2 * D_FF), BF16),
        jax.ShapeDtypeStruct((T, D_MODEL), F32),
        jax.ShapeDtypeStruct((1, D_MODEL), F32),
        jax.ShapeDtypeStruct((1, D_MODEL), F32),
        jax.ShapeDtypeStruct((1, 2 * D_FF), F32),
        jax.ShapeDtypeStruct((3, 2 * D_FF), F32),
    ]
    return pl.pallas_call(
        body, name="ffn_bwd", grid=(nt,), out_shape=outs,
        in_specs=[pl.BlockSpec((TM, D_MODEL), rev), pl.BlockSpec((TM, D_MODEL), rev), _const_spec((1, D_MODEL)),
                  pl.BlockSpec((TM, 2 * D_FF), rev), pl.BlockSpec((TM, D_FF), rev), pl.BlockSpec((TM, D_FF), rev),
                  _const_spec((3, 2 * D_FF)), _any_spec(), _any_spec(), pl.BlockSpec((TM, D_MODEL), rev),
                  _const_spec((1, D_MODEL))],
        out_specs=[pl.BlockSpec((TM, D_MODEL), rev), pl.BlockSpec((TM, 2 * D_FF), rev), pl.BlockSpec((TM, D_MODEL), rev),
                   _const_spec((1, D_MODEL)), _const_spec((1, D_MODEL)), _const_spec((1, 2 * D_FF)),
                   _const_spec((3, 2 * D_FF))],
        scratch_shapes=[pltpu.VMEM((N_SHARD, D_MODEL, SHARD_FF), BF16), pltpu.VMEM((D_FF, D_MODEL), BF16),
                        pltpu.VMEM((HALO, 2 * D_FF), F32), pltpu.SemaphoreType.DMA((2,))],
        compiler_params=_params(),
    )(dx2, y, g4, up, c1, c2, cw, wdown, wup, x1, g3)


def _mix_out_bwd(dx1, cat, g2, wout, og, gg, gnorm):
    T = dx1.shape[0]
    TM = _wide_tile(T)

    def body(dx1_ref, cat_ref, g2_ref, wout_ref, og_ref, gg_ref, gn_ref,
             dmix_ref, dog_ref, dgg_ref, dosw_ref, dg2_ref, dgn_ref):
        @pl.when(pl.program_id(0) == 0)
        def _():
            dg2_ref[...] = jnp.zeros_like(dg2_ref)
            dgn_ref[...] = jnp.zeros_like(dgn_ref)

        dx1 = dx1_ref[...]
        mix = _mm(cat_ref[...], wout_ref[...])
        r2 = lax.rsqrt(jnp.mean(mix * mix, axis=-1, keepdims=True) + EPS)
        dg2_ref[...] = dg2_ref[...] + jnp.sum(dx1 * mix * r2, axis=0, keepdims=True)
        dmix = _rms_bwd(mix, r2, g2_ref[...], dx1).astype(BF16)
        dmix_ref[...] = dmix
        dcat = _mm_nt(dmix, wout_ref[...])
        dosw_ref[...] = dcat[:, GLA_V:].astype(BF16)
        gn = gn_ref[...]
        dgn = jnp.zeros((1, LANES), F32)
        for j in range(GLA_HEADS):
            hs = slice(128 * j, 128 * j + 128)
            o = og_ref[:, hs].astype(F32)
            r = lax.rsqrt(jnp.mean(o * o, axis=-1, keepdims=True) + EPS)
            gate = gg_ref[:, hs].astype(F32)
            sg = _sigmoid(gate)
            dgated = dcat[:, hs]
            dnorm = dgated * (gate * sg)
            dgg_ref[:, hs] = (dgated * (o * r * gn) * (sg * (1.0 + gate * (1.0 - sg)))).astype(BF16)
            dgn = dgn + jnp.sum(dnorm * o * r, axis=0, keepdims=True)
            dog_ref[:, hs] = _rms_bwd(o, r, gn, dnorm)
        dgn_ref[...] = dgn_ref[...] + dgn

    return pl.pallas_call(
        body, name="mix_out_bwd", grid=(T // TM,),
        out_shape=[jax.ShapeDtypeStruct((T, D_MODEL), BF16), jax.ShapeDtypeStruct((T, GLA_V), F32),
                   jax.ShapeDtypeStruct((T, GLA_V), BF16), jax.ShapeDtypeStruct((T, SWA_Q), BF16),
                   jax.ShapeDtypeStruct((1, D_MODEL), F32), jax.ShapeDtypeStruct((1, LANES), F32)],
        in_specs=[_row_spec(TM, D_MODEL), _row_spec(TM, D_MODEL), _const_spec((1, D_MODEL)),
                  _const_spec((D_MODEL, D_MODEL)), _row_spec(TM, GLA_V), _row_spec(TM, GLA_V), _const_spec((1, LANES))],
        out_specs=[_row_spec(TM, D_MODEL), _row_spec(TM, GLA_V), _row_spec(TM, GLA_V), _row_spec(TM, SWA_Q),
                   _const_spec((1, D_MODEL)), _const_spec((1, LANES))],
        compiler_params=_params(),
    )(dx1, cat, g2, wout, og, gg, gnorm)


def _proj_bwd(x, g1, wp, gup, glr, dq, dk, dv, dgg, dsq, dkd, dvd, dz, rc, rsa, rsb, dx1):
    T = x.shape[0]
    TM = _wide_tile(T)

    def body(x_ref, g1_ref, wp_hbm, gup_ref, glr_ref, dq_ref, dk_ref, dv_ref, dgg_ref, dsq_ref, dkd_ref, dvd_ref,
             dz_ref, rc_ref, rsa_ref, rsb_ref, dx1_ref, dx_ref, dp_ref, dg1_ref, dgup_ref, dgb_ref, wp_v, sem):
        _load_once(wp_hbm, wp_v, sem)

        @pl.when(pl.program_id(0) == 0)
        def _():
            dg1_ref[...] = jnp.zeros_like(dg1_ref)
            dgup_ref[...] = jnp.zeros_like(dgup_ref)
            dgb_ref[...] = jnp.zeros_like(dgb_ref)

        rc_, rsa_, rsb_ = rc_ref[...], rsa_ref[...], rsb_ref[...]
        dp_ref[:, 0:256] = dq_ref[...]
        dp_ref[:, 256:512] = dk_ref[...]
        dp_ref[:, 512:1024] = dv_ref[...]
        dp_ref[:, 1024:1536] = dgg_ref[...]
        for s in range(4):
            cs = slice(128 * s, 128 * s + 128)
            dp_ref[:, 1536 + 128 * s:1664 + 128 * s] = _rotate_bwd(dsq_ref[:, cs].astype(F32), rc_, rsa_, rsb_).astype(BF16)
        first = _iota((TM, LANES), 1) < 64
        dk0 = dkd_ref[:, 0:128].astype(F32)
        dk1 = dkd_ref[:, 128:256].astype(F32)
        dkr = jnp.where(first, dk0 + pltpu.roll(dk0, 64, 1), dk1 + pltpu.roll(dk1, 64, 1))
        dp_ref[:, 2048:2176] = _rotate_bwd(dkr, rc_, rsa_, rsb_).astype(BF16)
        dv0 = dvd_ref[:, 0:128].astype(F32)
        dv1 = dvd_ref[:, 128:256].astype(F32)
        dp_ref[:, 2176:2304] = jnp.where(first, dv0 + pltpu.roll(dv0, 64, 1), dv1 + pltpu.roll(dv1, 64, 1)).astype(BF16)
        dz = dz_ref[...]
        dzb = dz.astype(BF16)
        dp_ref[:, 2304:2432] = _mm_nt(dzb, gup_ref[...]).astype(BF16)
        dgup_ref[...] = dgup_ref[...] + _mm_tn(glr_ref[...], dzb)
        dgb_ref[...] = dgb_ref[...] + jnp.sum(dz, axis=0, keepdims=True)
        dh1 = _mm_nt(dp_ref[...], wp_v[...])
        xt = x_ref[...]
        r = lax.rsqrt(jnp.mean(xt * xt, axis=-1, keepdims=True) + EPS)
        dg1_ref[...] = dg1_ref[...] + jnp.sum(dh1 * xt * r, axis=0, keepdims=True)
        dx_ref[...] = dx1_ref[...] + _rms_bwd(xt, r, g1_ref[...], dh1)

    row = lambda cols: _row_spec(TM, cols)
    return pl.pallas_call(
        body, name="proj_bwd", grid=(T // TM,),
        out_shape=[jax.ShapeDtypeStruct((T, D_MODEL), F32), jax.ShapeDtypeStruct((T, IN_WIDTH_PAD), BF16),
                   jax.ShapeDtypeStruct((1, D_MODEL), F32), jax.ShapeDtypeStruct((LANES, GLA_QK), F32),
                   jax.ShapeDtypeStruct((1, GLA_QK), F32)],
        in_specs=[row(D_MODEL), _const_spec((1, D_MODEL)), _any_spec(), _const_spec((LANES, GLA_QK)), row(LANES),
                  row(GLA_QK), row(GLA_QK), row(GLA_V), row(GLA_V), row(SWA_Q), row(256), row(256), row(GLA_QK),
                  row(LANES), row(LANES), row(LANES), row(D_MODEL)],
        out_specs=[row(D_MODEL), row(IN_WIDTH_PAD), _const_spec((1, D_MODEL)), _const_spec((LANES, GLA_QK)),
                   _const_spec((1, GLA_QK))],
        scratch_shapes=[pltpu.VMEM((D_MODEL, IN_WIDTH_PAD), BF16), pltpu.SemaphoreType.DMA],
        compiler_params=_params(),
    )(x, g1, wp, gup, glr, dq, dk, dv, dgg, dsq, dkd, dvd, dz, rc, rsa, rsb, dx1)


def _matmul_tn(a, b, tn, name, column_blocks_major=False, after=None):
    T, M = a.shape
    N = b.shape[1]
    tk = next(t for t in (2048, 1024, 512, TM) if T % t == 0 and t * (M + tn) <= 2048 * (D_MODEL + SHARD_FF))
    nk = T // tk
    if column_blocks_major:
        out_shape = jax.ShapeDtypeStruct((N // tn, M, tn), F32)
        out_spec = pl.BlockSpec((None, M, tn), lambda j, kk: (j, 0, 0))
    else:
        out_shape = jax.ShapeDtypeStruct((M, N), F32)
        out_spec = pl.BlockSpec((M, tn), lambda j, kk: (0, j))

    def body(a_ref, b_ref, *rest):
        o_ref = rest[-1]
        kk = pl.program_id(1)

        @pl.when(kk == 0)
        def _():
            o_ref[...] = jnp.zeros_like(o_ref)

        o_ref[...] = o_ref[...] + _mm_tn(a_ref[...], b_ref[...])

    ordering = [] if after is None else [after]
    return pl.pallas_call(
        body, name=name, grid=(N // tn, nk), out_shape=out_shape,
        in_specs=[pl.BlockSpec((tk, M), lambda j, kk: (kk, 0)), pl.BlockSpec((tk, tn), lambda j, kk: (kk, j))]
        + [_any_spec()] * len(ordering),
        out_specs=out_spec,
        compiler_params=_params(),
    )(a, b, *ordering)


def _adamw_update(w_ref, g_ref, m_ref, v_ref, d_ref, m2_ref, v2_ref):
    g_ = g_ref[...]
    m2 = ADAM_B1 * m_ref[...] + (1.0 - ADAM_B1) * g_
    v2 = ADAM_B2 * v_ref[...] + (1.0 - ADAM_B2) * (g_ * g_)
    m_hat = m2 / (1.0 - ADAM_B1 ** ADAM_STEP)
    v_hat = v2 / (1.0 - ADAM_B2 ** ADAM_STEP)
    d_ref[...] = -ADAM_LR * (m_hat / (jnp.sqrt(v_hat) + ADAM_EPS) + ADAM_WD * w_ref[...])
    m2_ref[...] = m2
    v2_ref[...] = v2


def _adamw(w, g, m, v, rows, name):
    R, C = w.shape

    def body(*refs):
        _adamw_update(*refs[:7])
        refs[7][...] = refs[1][...]

    spec = pl.BlockSpec((rows, C), lambda i: (i, 0))
    return pl.pallas_call(
        body, name=name, grid=(R // rows,), out_shape=[jax.ShapeDtypeStruct((R, C), F32)] * 4,
        in_specs=[spec] * 4, out_specs=[spec] * 4, compiler_params=_params(),
    )(w, g, m, v)


def _adamw_small(ws, gs, ms, vs):
    n = len(ws)

    def body(*refs):
        w_, g_, m_, v_, d_, m2_, v2_, g2_ = (refs[n * i:n * (i + 1)] for i in range(8))
        for k in range(n):
            _adamw_update(w_[k], g_[k], m_[k], v_[k], d_[k], m2_[k], v2_[k])
            g2_[k][...] = g_[k][...]

    vm = pl.BlockSpec(memory_space=pltpu.VMEM)
    outs = pl.pallas_call(
        body, name="adamw_small", out_shape=[jax.ShapeDtypeStruct(w.shape, F32) for w in ws] * 4,
        in_specs=[vm] * (4 * n), out_specs=[vm] * (4 * n),
    )(*ws, *gs, *ms, *vs)
    return outs[:n], outs[n:2 * n], outs[2 * n:3 * n], outs[3 * n:]


def _place():
    x, y, c = lax.axis_index("x"), lax.axis_index("y"), lax.axis_index("c")
    chips = [(1 - x, y), (x, 1 - y), (1 - x, 1 - y)]
    return x, y, c, chips


class _staged_copies:
    def __init__(self, srcs, dsts, stage, sems):
        n = len(srcs)
        self.loads = [pltpu.make_async_copy(srcs[k], stage[k], sems.at[k]) for k in range(n)]
        self.stores = [pltpu.make_async_copy(stage[k], dsts[k], sems.at[n + k]) for k in range(n)]

    def load(self):
        for cp in self.loads:
            cp.start()

    def store(self):
        for ld, st in zip(self.loads, self.stores):
            ld.wait()
            st.start()

    def finish(self):
        for cp in self.stores:
            cp.wait()


def _allgather_shards(parts, unit_rows):
    n = len(parts)
    units = [(k, r, unit_rows[k]) for k in range(n) for r in range(0, parts[k].shape[0] // 2, unit_rows[k])]
    nu = len(units)

    def body(*refs):
        ins, outs, stage = refs[:n], refs[n:2 * n], refs[2 * n:3 * n]
        send_sems, recv_sems, local_sems = refs[3 * n:]
        x, y, c, chips = _place()
        sibling = (x, y, 1 - c)
        own = _staged_copies(ins, [o.at[2 * x + y] for o in outs], stage, local_sems)

        def block(i, px, py, half):
            k, r, u = units[i]
            return outs[k].at[2 * px + py, pl.ds(half * (parts[k].shape[0] // 2) + r, u), :]

        def copy(i, j, px, py, half, to, src=None):
            return pltpu.make_async_remote_copy(
                src_ref=block(i, px, py, half) if src is None else src, dst_ref=block(i, px, py, half),
                send_sem=send_sems.at[nu * j + i], recv_sem=recv_sems.at[nu * j + i], device_id=to, device_id_type=MESH)

        own.load()
        first, passed = [], []
        for i, (k, r, u) in enumerate(units):
            for j, chip in enumerate(chips):
                src = ins[k].at[pl.ds(c * (parts[k].shape[0] // 2) + r, u), :]
                first.append(copy(i, j, x, y, c, (*chip, c), src=src))
                first[-1].start()
        own.store()
        for i in range(nu):
            for j, chip in enumerate(chips):
                copy(i, j, *chip, c, (x, y, c)).wait_recv()
                passed.append(copy(i, 3 + j, *chip, c, sibling))
                passed[-1].start()
        for i in range(nu):
            for j, chip in enumerate(chips):
                copy(i, 3 + j, *chip, 1 - c, (x, y, c)).wait_recv()
        for cp in first + passed:
            cp.wait_send()
        own.finish()

    return pl.pallas_call(
        body, name="allgather_shards", out_shape=[jax.ShapeDtypeStruct((N_SHARD,) + p.shape, p.dtype) for p in parts],
        in_specs=[_any_spec()] * n, out_specs=[_any_spec()] * n,
        scratch_shapes=[pltpu.VMEM(p.shape, p.dtype) for p in parts] + [
            pltpu.SemaphoreType.DMA((6 * nu,)), pltpu.SemaphoreType.DMA((6 * nu,)), pltpu.SemaphoreType.DMA((2 * n,))],
        compiler_params=_params(),
    )(*parts)


def _d2d_pieces(rows, piece_rows):
    return [(r, piece_rows) for r in range(0, rows, piece_rows)]


def _rs_pair_swap(arrs, piece_rows, name):
    n = len(arrs)

    def body(*refs):
        ins, outs = refs[:n], refs[n:2 * n]
        send_sems, recv_sems = refs[2 * n:]
        x, y, c, _ = _place()
        sibling = (x, y, 1 - c)
        for k in range(n):
            H = arrs[k].shape[1] // 2
            for s in range(N_SHARD):
                for r, pr in _d2d_pieces(H, piece_rows[k]):
                    pltpu.make_async_remote_copy(
                        src_ref=ins[k].at[s, pl.ds((1 - c) * H + r, pr), :], dst_ref=outs[k].at[s, pl.ds(r, pr), :],
                        send_sem=send_sems.at[k], recv_sem=recv_sems.at[k], device_id=sibling, device_id_type=MESH).start()
        for k in range(n):
            H = arrs[k].shape[1] // 2
            whole = pltpu.make_async_remote_copy(
                src_ref=ins[k].at[:, pl.ds(0, H), :], dst_ref=outs[k], send_sem=send_sems.at[k], recv_sem=recv_sems.at[k],
                device_id=sibling, device_id_type=MESH)
            whole.wait_recv()
            whole.wait_send()

    return pl.pallas_call(
        body, name=name,
        out_shape=[jax.ShapeDtypeStruct((N_SHARD, a.shape[1] // 2, a.shape[2]), F32) for a in arrs],
        in_specs=[_any_spec()] * n, out_specs=[_any_spec()] * n,
        scratch_shapes=[pltpu.SemaphoreType.DMA((n,)), pltpu.SemaphoreType.DMA((n,))],
    )(*arrs)


def _rs_add_pair(a, got, core, rows, name):
    _, H, C = got.shape
    nb = H // rows

    def body(c_ref, a_ref, b_ref, o_ref):
        o_ref[...] = (a_ref[...] + b_ref[...]).astype(BF16)

    spec = pl.BlockSpec((1, rows, C), lambda s, r, c_ref: (s, r, 0))
    return pl.pallas_call(
        body, name=name, out_shape=jax.ShapeDtypeStruct(got.shape, BF16),
        grid_spec=pltpu.PrefetchScalarGridSpec(
            num_scalar_prefetch=1, grid=(N_SHARD, nb),
            in_specs=[pl.BlockSpec((1, rows, C), lambda s, r, c_ref: (s, c_ref[0] * nb + r, 0)), spec], out_specs=spec),
        compiler_params=_params(),
    )(core, a, got)


def _rs_sum_chips(parts, rows, name):
    _, H, C = parts.shape

    def body(p_ref, o_ref):
        o_ref[...] = ((p_ref[0].astype(F32) + p_ref[1].astype(F32)) + p_ref[2].astype(F32)) + p_ref[3].astype(F32)

    return pl.pallas_call(
        body, name=name, grid=(H // rows,), out_shape=jax.ShapeDtypeStruct((H, C), F32),
        in_specs=[pl.BlockSpec((N_SHARD, rows, C), lambda r: (0, r, 0))],
        out_specs=pl.BlockSpec((rows, C), lambda r: (r, 0)), compiler_params=_params(),
    )(parts)


def _rs_pair_share(halves, piece_rows, name):
    n = len(halves)

    def body(*refs):
        ins, outs, stage = refs[:n], refs[n:2 * n], refs[2 * n:3 * n]
        send_sems, recv_sems, local_sems = refs[3 * n:]
        x, y, c, _ = _place()
        sibling = (x, y, 1 - c)
        own = _staged_copies(ins, [outs[k].at[pl.ds(c * halves[k].shape[0], halves[k].shape[0]), :] for k in range(n)],
                             stage, local_sems)
        own.load()
        for k in range(n):
            H = halves[k].shape[0]
            for r, pr in _d2d_pieces(H, piece_rows[k]):
                pltpu.make_async_remote_copy(
                    src_ref=ins[k].at[pl.ds(r, pr), :], dst_ref=outs[k].at[pl.ds(c * H + r, pr), :],
                    send_sem=send_sems.at[k], recv_sem=recv_sems.at[k], device_id=sibling, device_id_type=MESH).start()
        own.store()
        for k in range(n):
            H = halves[k].shape[0]
            whole = pltpu.make_async_remote_copy(
                src_ref=ins[k], dst_ref=outs[k].at[pl.ds((1 - c) * H, H), :], send_sem=send_sems.at[k],
                recv_sem=recv_sems.at[k], device_id=sibling, device_id_type=MESH)
            whole.wait_recv()
            whole.wait_send()
        own.finish()

    return pl.pallas_call(
        body, name=name, out_shape=[jax.ShapeDtypeStruct((2 * h.shape[0], h.shape[1]), F32) for h in halves],
        in_specs=[_any_spec()] * n, out_specs=[_any_spec()] * n,
        scratch_shapes=[pltpu.VMEM(h.shape, F32) for h in halves] + [
            pltpu.SemaphoreType.DMA((n,)), pltpu.SemaphoreType.DMA((n,)), pltpu.SemaphoreType.DMA((2 * n,))],
        compiler_params=_params(),
    )(*halves)


_HBM = pl.BlockSpec(memory_space=pltpu.HBM)
_SEM = pl.BlockSpec(memory_space=pltpu.SEMAPHORE)
_EFFECT = pltpu.SideEffectType.DATAFLOW_SIDE_EFFECTING


def _gather_plan(srcs, lands, x, y, c, chips):
    plan = []
    for k in range(len(srcs)):
        H = srcs[k].shape[0] // 2
        for px, py in chips:
            plan.append((srcs[k].at[pl.ds(c * H, H), :], lands[k].at[2 * x + y, pl.ds(c * H, H), :], (px, py, c),
                         lands[k].at[2 * px + py, pl.ds(c * H, H), :]))
    return plan


def _scatter_plan(srcs, lands, x, y, c, chips):
    plan = []
    for k in range(len(srcs)):
        for px, py in chips:
            plan.append((srcs[k].at[2 * px + py], lands[k].at[2 * x + y], (px, py, c), lands[k].at[2 * px + py]))
    return plan


def _swap_plan(srcs, lands, x, y, c, chips):
    plan = []
    for k in range(len(srcs)):
        H = srcs[k].shape[1] // 2
        rows = H // SWAP_PIECES
        for s in range(N_SHARD):
            for r in range(0, H, rows):
                piece = lands[k].at[s, pl.ds(r, rows), :]
                plan.append((srcs[k].at[s, pl.ds((1 - c) * H + r, rows), :], piece, (x, y, 1 - c), piece))
    return plan


SWAP_PIECES = 8
COPIES_PER_ARRAY = {_gather_plan: 3, _scatter_plan: 3, _swap_plan: N_SHARD * SWAP_PIECES}


def _ici_start(srcs, lands, make_plan, name):
    n = len(srcs)
    ncopy = COPIES_PER_ARRAY[make_plan] * n

    def body(*refs):
        ins, lnd = refs[:n], refs[n:2 * n]
        send_sems, recv_sems = refs[2 * n], refs[2 * n + 1]
        token = refs[-1]
        x, y, c, chips = _place()
        for i, (src, dst, peer, _) in enumerate(make_plan(ins, lnd, x, y, c, chips)):
            pltpu.make_async_remote_copy(src_ref=src, dst_ref=dst, send_sem=send_sems.at[i], recv_sem=recv_sems.at[i],
                                         device_id=peer, device_id_type=MESH).start()
        token[...] = jnp.zeros_like(token)

    arrays = list(srcs) + list(lands)
    return pl.pallas_call(
        body, name=name,
        out_shape=(pltpu.SemaphoreType.DMA((ncopy,)), pltpu.SemaphoreType.DMA((ncopy,)),
                   *[pltpu.HBM(a.shape, a.dtype) for a in arrays], jax.ShapeDtypeStruct((8, LANES), F32)),
        in_specs=[_HBM] * (2 * n), out_specs=(_SEM, _SEM, *[_HBM] * (2 * n), pl.BlockSpec(memory_space=pltpu.VMEM)),
        input_output_aliases={i: 2 + i for i in range(2 * n)},
        compiler_params=pltpu.CompilerParams(has_side_effects=_EFFECT),
    )(*[pltpu.with_memory_space_constraint(a, pltpu.HBM) for a in arrays])


def _ici_wait(started, after, make_plan, name):
    send_sems, recv_sems = started[0], started[1]
    arrays = list(started[2:-1])
    n = len(arrays) // 2

    def body(*refs):
        ins, lnd = refs[:n], refs[n:2 * n]
        send_sems, recv_sems = refs[2 * n], refs[2 * n + 1]
        x, y, c, chips = _place()
        for i, (src, _, peer, landed) in enumerate(make_plan(ins, lnd, x, y, c, chips)):
            cp = pltpu.make_async_remote_copy(src_ref=src, dst_ref=landed, send_sem=send_sems.at[i],
                                              recv_sem=recv_sems.at[i], device_id=peer, device_id_type=MESH)
            cp.wait_send()
            cp.wait_recv()

    outs = pl.pallas_call(
        body, name=name, out_shape=tuple(pltpu.HBM(a.shape, a.dtype) for a in arrays),
        in_specs=[_HBM] * (2 * n) + [_SEM, _SEM, pl.BlockSpec(memory_space=pl.ANY)], out_specs=tuple([_HBM] * (2 * n)),
        input_output_aliases={i: i for i in range(2 * n)},
        compiler_params=pltpu.CompilerParams(has_side_effects=_EFFECT),
    )(*arrays, send_sems, recv_sems, after)
    return list(outs[:n]), list(outs[n:])


def _gather_finish(parts, lands, name):
    n = len(parts)

    def body(*refs):
        ins, lnd, outs, stage = refs[:n], refs[n:2 * n], refs[2 * n:3 * n], refs[3 * n:4 * n]
        send_sems, recv_sems, local_sems = refs[4 * n:]
        x, y, c, chips = _place()
        sibling = (x, y, 1 - c)
        own = _staged_copies(ins, [o.at[2 * x + y] for o in outs], stage, local_sems)
        own.load()
        sends = []
        for k in range(n):
            H = parts[k].shape[0] // 2
            for j, (px, py) in enumerate(chips):
                half = outs[k].at[2 * px + py, pl.ds(c * H, H), :]
                sends.append(pltpu.make_async_remote_copy(src_ref=half, dst_ref=half, send_sem=send_sems.at[3 * k + j],
                                                          recv_sem=recv_sems.at[3 * k + j], device_id=sibling, device_id_type=MESH))
                sends[-1].start()
        own.store()
        for k in range(n):
            H = parts[k].shape[0] // 2
            for j, (px, py) in enumerate(chips):
                other = outs[k].at[2 * px + py, pl.ds((1 - c) * H, H), :]
                pltpu.make_async_remote_copy(src_ref=other, dst_ref=other, send_sem=send_sems.at[3 * k + j],
                                             recv_sem=recv_sems.at[3 * k + j], device_id=sibling, device_id_type=MESH).wait_recv()
        for cp in sends:
            cp.wait_send()
        own.finish()

    return pl.pallas_call(
        body, name=name, out_shape=[jax.ShapeDtypeStruct(l.shape, l.dtype) for l in lands],
        in_specs=[_any_spec()] * (2 * n), out_specs=[_any_spec()] * n,
        input_output_aliases={n + k: k for k in range(n)},
        scratch_shapes=[pltpu.VMEM(p.shape, p.dtype) for p in parts] + [
            pltpu.SemaphoreType.DMA((3 * n,)), pltpu.SemaphoreType.DMA((3 * n,)), pltpu.SemaphoreType.DMA((2 * n,))],
        compiler_params=_params(),
    )(*parts, *lands)


def _place_own(part, chip, rows, name):
    _, H, C = part.shape

    def body(chip_ref, p_ref, o_ref):
        o_ref[...] = p_ref[...]

    spec = pl.BlockSpec((1, rows, C), lambda r, chip_ref: (chip_ref[0], r, 0))
    return pl.pallas_call(
        body, name=name, out_shape=jax.ShapeDtypeStruct(part.shape, part.dtype),
        grid_spec=pltpu.PrefetchScalarGridSpec(num_scalar_prefetch=1, grid=(H // rows,), in_specs=[spec], out_specs=spec),
        compiler_params=_params(),
    )(chip, part)


SMALL_COLS = 1024


def _small_rows(shapes):
    starts, row = [], 0
    for r, cdim in shapes:
        starts.append(row)
        row += r * (-(-cdim // SMALL_COLS))
    return starts, -(-row // 8) * 8


def _allreduce_small(arrays, sink_rows, loss):
    n = len(arrays)
    shapes = [a.shape for a in arrays] + [(1, SWA_HEADS), (1, 1)]
    starts, total_rows = _small_rows(shapes)

    def pieces(k):
        r, cdim = shapes[k]
        per = -(-cdim // SMALL_COLS)
        return [(i, SMALL_COLS * j, min(SMALL_COLS, cdim - SMALL_COLS * j), starts[k] + per * i + j)
                for i in range(r) for j in range(per)]

    def body(*refs):
        ins, sink_ref, loss_ref = refs[:n], refs[n], refs[n + 1]
        outs = refs[n + 2:2 * n + 4]
        mine, all_ref, tot_ref, send_sems, recv_sems = refs[2 * n + 4:]
        x, y, c, _ = _place()
        me = 4 * x + 2 * y + c
        mine[...] = jnp.zeros_like(mine)
        for k in range(n):
            for i, col, wd, row in pieces(k):
                mine[row:row + 1, 0:wd] = ins[k][i:i + 1, col:col + wd]
        lane = _iota((1, SMALL_COLS), 1)
        sinks = jnp.zeros((1, SMALL_COLS), F32)
        for h in range(SWA_HEADS):
            head = jnp.sum(sink_ref[SWA_BLOCK * h:SWA_BLOCK * (h + 1), :]) * (1.0 / LANES)
            sinks = jnp.where(lane == h, head, sinks)
        mine[starts[n]:starts[n] + 1, :] = sinks
        mine[starts[n + 1]:starts[n + 1] + 1, 0:LANES] = loss_ref[0:1, :]
        all_ref[me] = mine[...]
        sends = []
        for k in range(1, 8):
            kx, ky, kc = (k >> 2) & 1, (k >> 1) & 1, k & 1
            peer = (x ^ kx, y ^ ky, c ^ kc)
            cp = pltpu.make_async_remote_copy(
                src_ref=mine, dst_ref=all_ref.at[me], send_sem=send_sems.at[k - 1], recv_sem=recv_sems.at[k - 1],
                device_id=peer, device_id_type=MESH)
            cp.start()
            sends.append(cp)
        for k in range(1, 8):
            kx, ky, kc = (k >> 2) & 1, (k >> 1) & 1, k & 1
            src = 4 * (x ^ kx) + 2 * (y ^ ky) + (c ^ kc)
            pltpu.make_async_remote_copy(
                src_ref=mine, dst_ref=all_ref.at[src], send_sem=send_sems.at[k - 1], recv_sem=recv_sems.at[k - 1],
                device_id=(x, y, c), device_id_type=MESH).wait_recv()
        for cp in sends:
            cp.wait_send()
        total = all_ref[0]
        for d in range(1, 8):
            total = total + all_ref[d]
        tot_ref[...] = total
        for k in range(n + 2):
            for i, col, wd, row in pieces(k):
                outs[k][i:i + 1, col:col + wd] = tot_ref[row:row + 1, 0:wd]

    vm = pl.BlockSpec(memory_space=pltpu.VMEM)
    buf = pltpu.VMEM((total_rows, SMALL_COLS), F32)
    return pl.pallas_call(
        body, name="allreduce_small", out_shape=[jax.ShapeDtypeStruct(s, F32) for s in shapes],
        in_specs=[vm] * (n + 2), out_specs=[vm] * (n + 2),
        scratch_shapes=[buf, pltpu.VMEM((8, total_rows, SMALL_COLS), F32), buf,
                        pltpu.SemaphoreType.DMA((7,)), pltpu.SemaphoreType.DMA((7,))],
    )(*arrays, sink_rows, loss)


BIG_NAMES = ("w_in", "w_out", "w_up", "w_down")
MATRIX_NAMES = BIG_NAMES + ("gla_gate_up", "conv_w")
LATE_NAMES = ("w_out", "w_up", "w_down")
GATE_SHARD = (16, GLA_QK // N_SHARD)
CONVW_SHARD = (3, SHARD_FF)
SMALL_W_ROWS = 96
PIECE_ROWS = (128, 128, 64, 88)
ADD_ROWS = (256, 128, 256, 176)
FIRST_UNIT_ROWS = (256, SMALL_W_ROWS // 2)


def _pad_rows(flat, rows):
    return jnp.pad(flat, (0, rows * LANES - flat.shape[0])).reshape(rows, LANES)


def _pack_small_weights(gate_up, conv_w):
    bits = lax.bitcast_convert_type(conv_w, BF16)
    return _pad_rows(jnp.concatenate([gate_up.astype(BF16).reshape(-1), bits.reshape(-1)]), SMALL_W_ROWS)


def _unpack_small_weights(packed):
    flat = packed.reshape(N_SHARD, -1)
    n_gate = GATE_SHARD[0] * GATE_SHARD[1]
    n_conv = 2 * CONVW_SHARD[0] * CONVW_SHARD[1]
    gate = flat[:, :n_gate].reshape((N_SHARD,) + GATE_SHARD)
    conv = lax.bitcast_convert_type(flat[:, n_gate:n_gate + n_conv].reshape((N_SHARD,) + CONVW_SHARD + (2,)), F32)
    return (jnp.transpose(gate, (1, 0, 2)).reshape(16, GLA_QK), jnp.transpose(conv, (1, 0, 2)).reshape(3, 2 * D_FF))


def _permute_w_in(w):
    pad = jnp.zeros((w.shape[0], IN_WIDTH_PAD - IN_WIDTH), w.dtype)
    return jnp.concatenate([w[:, 0:1024], w[:, 1040:2320], w[:, 1024:1040], pad], axis=1)


def _unpermute_w_in(wp):
    return jnp.concatenate([wp[:, 0:1024], wp[:, 2304:2320], wp[:, 1024:2304]], axis=1)


def _rope_tables(positions, zero):
    half = ROPE_DIM // 2
    inv_freq = ROPE_THETA ** (-jnp.arange(half, dtype=F32) * (2.0 / ROPE_DIM))
    d = jnp.arange(LANES) % SWA_HD
    freq = jnp.where(d < ROPE_DIM, inv_freq[d % half], 0.0)
    ang = positions.astype(F32)[:, None] * freq + zero
    cos, sin = jnp.cos(ang), jnp.sin(ang)
    return cos, jnp.where(d < half, -sin, 0.0), jnp.where((d >= half) & (d < ROPE_DIM), sin, 0.0)


SMALL_NAMES = (("pre_mix_norm", 1024), ("gla_gate_bias", 256), ("gla_out_norm", 128), ("swa_sinks", 8),
               ("post_mix_norm", 1024), ("pre_ffn_norm", 1024), ("conv_b", 5632), ("post_ffn_norm", 1024))


def _local_step(x, target, w, small, late_weights, early_grads, last_grads):
    rc, rsa, rsb = w["rope"]
    wp = w["wp"]
    gup = jnp.pad(w["gla_gate_up"], ((0, LANES - 16), (0, 0)))
    g1, g2, g3, g4 = (small[n] for n in ("pre_mix_norm", "post_mix_norm", "pre_ffn_norm", "post_ffn_norm"))
    gbias, gnorm, cb = small["gla_gate_bias"], small["gla_out_norm"], small["conv_b"]
    sinks = small["swa_sinks"].reshape(-1)
    cw = w["conv_w"]

    h1, q, k, v, la, gg, sq, kd, vd, glr = _proj_fwd(x, g1, wp, gup, gbias, rc, rsa, rsb)
    og, s_all = _gla_fwd(q, k, v, la)
    osw = _swa_fwd(sq, kd, vd, sinks)
    w_out, w_up4, w_down = late_weights(osw)
    x1, cat = _mix_out_fwd(x, og, gg, osw, gnorm, w_out, g2)
    h2, up, act, c1, c2, y, dx2, loss = _ffn_fwd(x1, g3, w_up4, cw, cb, w_down, g4, target)

    dy, dup, dx1, dg4, dg3, dcb, dcw = _ffn_bwd(dx2, y, g4, up, c1, c2, cw, w_down, w_up4, x1, g3)
    zero = early_grads(lambda: _matmul_tn(h2, dup, SHARD_FF, "grad_w_up", column_blocks_major=True),
                       lambda after=None: _matmul_tn(act, dy, D_MODEL, "grad_w_down", after=after))
    dmix, dog, dgg, dosw, dg2, dgn = _mix_out_bwd(dx1, cat, g2 + zero, w_out, og, gg, gnorm)
    dsq, dkd, dvd, dsink = _swa_bwd(sq, kd, vd, sinks, dosw)
    dq, dk, dv, dz = _gla_bwd(q, k, v, la, s_all, dog)
    dx, dproj, dg1, dgup, dgb = _proj_bwd(x, g1, wp, gup, glr, dq, dk, dv, dgg, dsq, dkd, dvd, dz, rc, rsa, rsb, dx1)

    grads = {
        "last": last_grads(lambda: _matmul_tn(h1, dproj, IN_WIDTH_PAD, "grad_w_in"),
                           lambda after=None: _matmul_tn(cat, dmix, D_MODEL, "grad_w_out", after=after)),
        "gla_gate_up": dgup[0:16],
        "conv_w": dcw,
    }
    small_grads = {
        "pre_mix_norm": dg1, "gla_gate_bias": dgb, "gla_out_norm": dgn,
        "post_mix_norm": dg2, "pre_ffn_norm": dg3, "conv_b": dcb, "post_ffn_norm": dg4,
    }
    return loss, dx, grads, small_grads, dsink


ADAM_ROWS = {"w_in": 256, "w_out": 256, "w_up": 256, "w_down": 176}
WEIGHT_ORDER = ("pre_mix_norm", "w_in", "gla_gate_up", "gla_gate_bias", "gla_out_norm", "swa_sinks", "w_out",
                "post_mix_norm", "pre_ffn_norm", "w_up", "conv_w", "conv_b", "w_down", "post_ffn_norm")
def kernel(x, positions, pre_mix_norm, w_in, gla_gate_up, gla_gate_bias, gla_out_norm, swa_sinks, w_out, post_mix_norm, pre_ffn_norm, w_up, conv_w, conv_b, w_down, post_ffn_norm, loss_target, m_pre_mix_norm, m_w_in, m_gla_gate_up, m_gla_gate_bias, m_gla_out_norm, m_swa_sinks, m_w_out, m_post_mix_norm, m_pre_ffn_norm, m_w_up, m_conv_w, m_conv_b, m_w_down, m_post_ffn_norm, v_pre_mix_norm, v_w_in, v_gla_gate_up, v_gla_gate_bias, v_gla_out_norm, v_swa_sinks, v_w_out, v_post_mix_norm, v_pre_ffn_norm, v_w_up, v_conv_w, v_conv_b, v_w_down, v_post_ffn_norm):
    weights = dict(pre_mix_norm=pre_mix_norm, w_in=w_in, gla_gate_up=gla_gate_up, gla_gate_bias=gla_gate_bias,
                   gla_out_norm=gla_out_norm, swa_sinks=swa_sinks, w_out=w_out, post_mix_norm=post_mix_norm,
                   pre_ffn_norm=pre_ffn_norm, w_up=w_up, conv_w=conv_w, conv_b=conv_b, w_down=w_down,
                   post_ffn_norm=post_ffn_norm)
    mom = dict(pre_mix_norm=m_pre_mix_norm, w_in=m_w_in, gla_gate_up=m_gla_gate_up, gla_gate_bias=m_gla_gate_bias,
               gla_out_norm=m_gla_out_norm, swa_sinks=m_swa_sinks, w_out=m_w_out, post_mix_norm=m_post_mix_norm,
               pre_ffn_norm=m_pre_ffn_norm, w_up=m_w_up, conv_w=m_conv_w, conv_b=m_conv_b, w_down=m_w_down,
               post_ffn_norm=m_post_ffn_norm)
    var = dict(pre_mix_norm=v_pre_mix_norm, w_in=v_w_in, gla_gate_up=v_gla_gate_up, gla_gate_bias=v_gla_gate_bias,
               gla_out_norm=v_gla_out_norm, swa_sinks=v_swa_sinks, w_out=v_w_out, post_mix_norm=v_post_mix_norm,
               pre_ffn_norm=v_pre_ffn_norm, w_up=v_w_up, conv_w=v_conv_w, conv_b=v_conv_b, w_down=v_w_down,
               post_ffn_norm=v_post_ffn_norm)
    weights, mom, var = ({n: a[0] if a.ndim == 3 else a for n, a in d.items()} for d in (weights, mom, var))

    core = lax.axis_index("c").astype(jnp.int32).reshape(1)
    chip = (2 * lax.axis_index("x") + lax.axis_index("y")).astype(jnp.int32).reshape(1)
    small = {n: weights[n] for n, _ in SMALL_NAMES}

    landing = lambda shards: [lax.empty((N_SHARD,) + s.shape, BF16) for s in shards]
    first_shards = [weights["w_in"].astype(BF16), _pack_small_weights(weights["gla_gate_up"], weights["conv_w"])]
    first_gather = _ici_start(first_shards, landing(first_shards), _gather_plan, "gather_start_first")
    rope = _rope_tables(positions[0], first_gather[-1][0, 0])
    win4, small4 = _gather_finish(*_ici_wait(first_gather, rope[0], _gather_plan, "gather_wait_first"), "gather_finish_first")
    gate_full, convw_full = _unpack_small_weights(small4)
    first = {"wp": _permute_w_in(jnp.transpose(win4, (1, 0, 2)).reshape(D_MODEL, IN_WIDTH)),
             "gla_gate_up": gate_full, "conv_w": convw_full, "rope": rope}
    after_first = (small4[0, 0, 0] * 0).astype(F32)
    late_shards = [(weights[n] + after_first).astype(BF16) for n in LATE_NAMES]
    gathering = _ici_start(late_shards, landing(late_shards), _gather_plan, "gather_start")
    small["pre_mix_norm"] = small["pre_mix_norm"] + gathering[-1][0, 0]

    def late_weights(after):
        wout4, wup4, wdown4 = _gather_finish(*_ici_wait(gathering, after, _gather_plan, "gather_wait"), "gather_finish")
        return wout4.reshape(D_MODEL, D_MODEL), wup4, wdown4.reshape(D_FF, D_MODEL)

    early = {}
    delta, new_m, new_v = {}, {}, {}

    def scatter_start(make_a, make_b, i, j, name):
        a = make_a()
        swapping = _ici_start([a], [lax.empty((N_SHARD, a.shape[1] // 2, a.shape[2]), F32)], _swap_plan,
                              "rs_swap_start_" + BIG_NAMES[i])
        b = make_b(swapping[-1])
        (a,), (got_a,) = _ici_wait(swapping, b, _swap_plan, "rs_swap_wait_" + BIG_NAMES[i])
        got_b = _rs_pair_swap([b], [PIECE_ROWS[j]], "rs_pair_swap_" + BIG_NAMES[j])[0]
        parts = [_rs_add_pair(g, got, core, ADD_ROWS[k], "rs_add_pair_" + BIG_NAMES[k])
                 for g, got, k in ((a, got_a, i), (b, got_b, j))]
        lands = [_place_own(p, chip, ADD_ROWS[k], "rs_place_own_" + BIG_NAMES[k]) for p, k in zip(parts, (i, j))]
        return _ici_start(parts, lands, _scatter_plan, name)

    def finish(landed, which, grads_out):
        names = [BIG_NAMES[i] for i in which]
        halves = [_rs_sum_chips(p, ADD_ROWS[i], "rs_sum_chips_" + BIG_NAMES[i]) for p, i in zip(landed, which)]
        reduced = _rs_pair_share(halves, [PIECE_ROWS[i] for i in which], "rs_pair_share_" + names[0])
        for n, g in zip(names, reduced):
            delta[n], new_m[n], new_v[n], grads_out[n] = _adamw(weights[n], g, mom[n], var[n], ADAM_ROWS[n], "adamw_" + n)

    def early_grads(make_up4, make_down):
        early["scatter"] = scatter_start(make_up4, lambda after: make_down(after).reshape(N_SHARD, D_FF // N_SHARD, D_MODEL), 2, 3,
                                         "rs_scatter_start")
        return early["scatter"][-1][0, 0]

    def last_grads(make_wp, make_out):
        return scatter_start(
            lambda: jnp.transpose(_unpermute_w_in(make_wp()).reshape(D_MODEL, N_SHARD, IN_WIDTH // N_SHARD), (1, 0, 2)),
            lambda after: make_out(after).reshape(N_SHARD, D_MODEL // N_SHARD, D_MODEL), 0, 1, "rs_scatter_start_rest")

    loss, dx, grads, small_grads, sink_rows = _local_step(
        x[0], loss_target[0], first, small, late_weights, early_grads, last_grads)

    late_scatter = grads["last"]
    g_all = {}
    finish(_ici_wait(early["scatter"], late_scatter[-1], _scatter_plan, "rs_scatter_wait")[1], (2, 3), g_all)
    finish(_ici_wait(late_scatter, delta["w_down"], _scatter_plan, "rs_scatter_wait_rest")[1], (0, 1), g_all)
    vectors = [n for n, _ in SMALL_NAMES if n != "swa_sinks"]
    summed = _allreduce_small([small_grads[n] for n in vectors] + [grads["gla_gate_up"], grads["conv_w"]], sink_rows, loss)
    g_all.update(zip(vectors, summed))
    g_all.update({"swa_sinks": summed[-2],
                  "gla_gate_up": lax.dynamic_slice_in_dim(summed[-4], chip[0] * GATE_SHARD[1], GATE_SHARD[1], axis=1),
                  "conv_w": lax.dynamic_slice_in_dim(summed[-3], chip[0] * SHARD_FF, SHARD_FF, axis=1)})
    loss_sum = summed[-1][0, 0]

    tiny = [n for n, _ in SMALL_NAMES] + ["gla_gate_up", "conv_w"]
    for res, vals in zip((delta, new_m, new_v, g_all), _adamw_small(*([d[n] for n in tiny] for d in (weights, g_all, mom, var)))):
        res.update(zip(tiny, vals))

    def lead(n, a):
        return a[None] if n in MATRIX_NAMES else a

    outs = [loss_sum, dx[None]]
    for d in (g_all, delta, new_m, new_v):
        outs.extend(lead(n, d[n]) for n in WEIGHT_ORDER)
    return tuple(outs)
```

```python
import functools

import jax
import jax.numpy as jnp
from jax import lax
from jax.experimental import pallas as pl
from jax.experimental.pallas import tpu as pltpu

F32 = jnp.float32
BF16 = jnp.bfloat16
MESH = pl.DeviceIdType.MESH

D_MODEL = 1024
GLA_HEADS = 4
GLA_DK = 64
GLA_DV = 128
GLA_TAU = 16.0
GLA_CHUNK = 64
SWA_HEADS = 8
SWA_HD = 64
SWA_BLOCK = 128
ROPE_THETA = 500000.0
ROPE_DIM = 16
D_FF = 2816
EPS = 1e-6
GLA_QK = 256
GLA_V = 512
SWA_Q = 512
SWA_KV = 128
IN_WIDTH = 2320
IN_WIDTH_PAD = 2432
N_SHARD = 4

ADAM_LR = 0.001
ADAM_B1 = 0.9
ADAM_B2 = 0.999
ADAM_EPS = 1e-08
ADAM_WD = 0.01
ADAM_STEP = 10

LANES = 128
VMEM_LIMIT = 56 * 1024 * 1024
TM = 256
SHARD_FF = 2 * D_FF // N_SHARD
FF_PIECES = ((0, 1408),)
GLA_BLOCK = 256


def _wide_tile(T):
    return 2 * TM if T % (2 * TM) == 0 else TM


def _params(**kw):
    return pltpu.CompilerParams(vmem_limit_bytes=VMEM_LIMIT, **kw)


def _mm(a, b):
    return lax.dot_general(a.astype(BF16), b.astype(BF16), (((1,), (0,)), ((), ())), preferred_element_type=F32)


def _mm_nt(a, b):
    return lax.dot_general(a.astype(BF16), b.astype(BF16), (((1,), (1,)), ((), ())), preferred_element_type=F32)


def _mm_tn(a, b):
    return lax.dot_general(a.astype(BF16), b.astype(BF16), (((0,), (0,)), ((), ())), preferred_element_type=F32)


def _mm_f32(a, b):
    return lax.dot_general(a, b, (((1,), (0,)), ((), ())), preferred_element_type=F32, precision=lax.Precision.HIGHEST)


def _iota(shape, dim):
    return lax.broadcasted_iota(jnp.int32, shape, dim)


def _sigmoid(x):
    return 1.0 / (1.0 + jnp.exp(-x))


def _gelu_parts(x):
    c = 0.7978845608028654
    x2 = x * x
    t = jnp.tanh(c * (x + 0.044715 * (x2 * x)))
    cdf = 0.5 * (1.0 + t)
    dcdf = 0.5 * (1.0 - t * t) * c * (1.0 + 3.0 * 0.044715 * x2)
    return x * cdf, cdf + x * dcdf


def _rms_bwd(v, r, g, dout):
    gd = g * dout
    return r * gd - v * (r * r * r) * jnp.mean(v * gd, axis=-1, keepdims=True)


def _row_spec(tm, cols):
    return pl.BlockSpec((tm, cols), lambda i: (i, 0))


def _const_spec(shape):
    return pl.BlockSpec(shape, lambda i: (0,) * len(shape))


def _any_spec():
    return pl.BlockSpec(memory_space=pl.ANY)


def _load_once(src_hbm, dst_vmem, sem):
    @pl.when(pl.program_id(0) == 0)
    def _():
        cp = pltpu.make_async_copy(src_hbm, dst_vmem, sem)
        cp.start()
        cp.wait()


def _rotate(v, rc, rsa, rsb):
    return v * rc + pltpu.roll(v, 120, 1) * rsa + pltpu.roll(v, 8, 1) * rsb


def _rotate_bwd(dv, rc, rsa, rsb):
    return dv * rc + pltpu.roll(dv * rsa, 8, 1) + pltpu.roll(dv * rsb, 120, 1)


def _proj_fwd(x, g1, wp, gup, gbias, rc, rsa, rsb):
    T = x.shape[0]
    TM = _wide_tile(T)

    def body(x_ref, g1_ref, wp_hbm, gup_ref, gb_ref, rc_ref, rsa_ref, rsb_ref,
             h1_ref, q_ref, k_ref, v_ref, la_ref, gg_ref, sq_ref, kd_ref, vd_ref, glr_ref, wp_v, sem):
        _load_once(wp_hbm, wp_v, sem)
        xt = x_ref[...]
        r = lax.rsqrt(jnp.mean(xt * xt, axis=-1, keepdims=True) + EPS)
        h = (xt * r * g1_ref[...]).astype(BF16)
        h1_ref[...] = h
        q_ref[...] = _mm(h, wp_v[:, 0:256])
        k_ref[...] = _mm(h, wp_v[:, 256:512])
        v_ref[...] = _mm(h, wp_v[:, 512:1024]).astype(BF16)
        gg_ref[...] = _mm(h, wp_v[:, 1024:1536]).astype(BF16)
        glr = _mm(h, wp_v[:, 2304:2432]).astype(BF16)
        glr_ref[...] = glr
        z = _mm(glr, gup_ref[...]) + gb_ref[...]
        la_ref[...] = (jnp.minimum(z, 0.0) - jnp.log1p(jnp.exp(-jnp.abs(z)))) * (1.0 / GLA_TAU)
        rc_, rsa_, rsb_ = rc_ref[...], rsa_ref[...], rsb_ref[...]
        for s in range(4):
            qs = _mm(h, wp_v[:, 1536 + 128 * s:1664 + 128 * s])
            sq_ref[:, 128 * s:128 * s + 128] = (_rotate(qs, rc_, rsa_, rsb_) * 0.125).astype(BF16)
        lane = _iota((TM, LANES), 1)
        first = lane < 64
        kr = _rotate(_mm(h, wp_v[:, 2048:2176]), rc_, rsa_, rsb_)
        krr = pltpu.roll(kr, 64, 1)
        kd_ref[:, 0:128] = jnp.where(first, kr, krr).astype(BF16)
        kd_ref[:, 128:256] = jnp.where(first, krr, kr).astype(BF16)
        vr = _mm(h, wp_v[:, 2176:2304])
        vrr = pltpu.roll(vr, 64, 1)
        vd_ref[:, 0:128] = jnp.where(first, vr, vrr).astype(BF16)
        vd_ref[:, 128:256] = jnp.where(first, vrr, vr).astype(BF16)

    outs = [
        jax.ShapeDtypeStruct((T, D_MODEL), BF16),
        jax.ShapeDtypeStruct((T, GLA_QK), F32),
        jax.ShapeDtypeStruct((T, GLA_QK), F32),
        jax.ShapeDtypeStruct((T, GLA_V), BF16),
        jax.ShapeDtypeStruct((T, GLA_QK), F32),
        jax.ShapeDtypeStruct((T, GLA_V), BF16),
        jax.ShapeDtypeStruct((T, SWA_Q), BF16),
        jax.ShapeDtypeStruct((T, 256), BF16),
        jax.ShapeDtypeStruct((T, 256), BF16),
        jax.ShapeDtypeStruct((T, LANES), BF16),
    ]
    return pl.pallas_call(
        body, name="proj_fwd", grid=(T // TM,), out_shape=outs,
        in_specs=[_row_spec(TM, D_MODEL), _const_spec((1, D_MODEL)), _any_spec(), _const_spec((LANES, GLA_QK)),
                  _const_spec((1, GLA_QK)), _row_spec(TM, LANES), _row_spec(TM, LANES), _row_spec(TM, LANES)],
        out_specs=[_row_spec(TM, o.shape[1]) for o in outs],
        scratch_shapes=[pltpu.VMEM((D_MODEL, IN_WIDTH_PAD), BF16), pltpu.SemaphoreType.DMA],
        compiler_params=_params(),
    )(x, g1, wp, gup, gbias, rc, rsa, rsb)


GLA_NB = GLA_BLOCK // GLA_CHUNK


def _gla_masks():
    n = GLA_BLOCK
    lane = _iota((n, LANES), 1)
    lane_masks = [(lane < 64).astype(F32), (lane >= 64).astype(F32)]
    row, col = _iota((n, n), 0), _iota((n, n), 1)
    same_chunk = (row >> 6) == (col >> 6)
    blk = ((_iota((256, LANES), 0) >> 7) == (_iota((256, LANES), 1) >> 6)).astype(F32)
    return lane_masks, same_chunk & (col <= row), same_chunk & (col >= row), blk


def _chunk_rows(vals):
    return jnp.concatenate([jnp.broadcast_to(v, (GLA_CHUNK, LANES)) for v in vals], axis=0)


def _gla_block_terms(q_ref, k_ref, b_ref, p):
    C = GLA_CHUNK
    cols = slice(LANES * p, LANES * p + LANES)
    bc = b_ref[:, cols]
    bl_rows = [b_ref[C * c + C - 1:C * c + C, cols] for c in range(GLA_NB)]
    bl = _chunk_rows(bl_rows)
    bm = _chunk_rows([b_ref[C * c + C // 2 - 1:C * c + C // 2, cols] for c in range(GLA_NB)])
    qs = q_ref[:, cols] * 0.125
    kk = k_ref[:, cols]
    eb = jnp.exp(bc)
    ekl = jnp.exp(bl - bc)
    eqm = jnp.exp(bc - bm)
    ekm = jnp.exp(bm - bc)
    return qs, kk, eb, ekl, eqm, ekm, [jnp.exp(r) for r in bl_rows]


def _block_cumsum(la, mask):
    return _mm_f32(mask.astype(F32), la)


def _gla_fwd(q, k, v, la):
    T = q.shape[0]
    NB = GLA_BLOCK // GLA_CHUNK
    C = GLA_CHUNK

    def body(q_ref, k_ref, v_ref, la_ref, o_ref, s_ref, st_ref, b_ref):
        @pl.when(pl.program_id(0) == 0)
        def _():
            st_ref[...] = jnp.zeros_like(st_ref)

        lane_masks, causal, _, blk = _gla_masks()
        b_ref[...] = _block_cumsum(la_ref[...], causal)
        for p in range(2):
            qs, kk, eb, ekl, eqm, ekm, gam = _gla_block_terms(q_ref, k_ref, b_ref, p)
            qh, kh, qm, km = qs * eb, kk * ekl, qs * eqm, kk * ekm
            vp = v_ref[:, 256 * p:256 * p + 256]
            intra = []
            for j in range(2):
                a = jnp.where(causal, _mm_nt(qm * lane_masks[j], km), 0.0)
                intra.append(_mm(a, vp[:, 128 * j:128 * j + 128]))
            kv = [blk * _mm_tn(vp[C * c:C * c + C], kh[C * c:C * c + C]) for c in range(NB)]
            st = st_ref[p]
            inter = []
            for c in range(NB):
                s_ref[c, p] = st[0:LANES] + st[LANES:2 * LANES]
                inter.append(_mm_nt(qh[C * c:C * c + C], st))
                st = st * gam[c] + kv[c]
            st_ref[p] = st
            o_ref[:, 256 * p:256 * p + 256] = (jnp.concatenate(inter, axis=0) + jnp.concatenate(intra, axis=1)).astype(BF16)

    return pl.pallas_call(
        body, name="gla_fwd", grid=(T // GLA_BLOCK,),
        out_shape=[jax.ShapeDtypeStruct((T, GLA_V), BF16), jax.ShapeDtypeStruct((T // C, 2, LANES, LANES), F32)],
        in_specs=[_row_spec(GLA_BLOCK, GLA_QK), _row_spec(GLA_BLOCK, GLA_QK), _row_spec(GLA_BLOCK, GLA_V),
                  _row_spec(GLA_BLOCK, GLA_QK)],
        out_specs=[_row_spec(GLA_BLOCK, GLA_V), pl.BlockSpec((NB, 2, LANES, LANES), lambda i: (i, 0, 0, 0))],
        scratch_shapes=[pltpu.VMEM((2, 256, LANES), F32), pltpu.VMEM((GLA_BLOCK, GLA_QK), F32)],
        compiler_params=_params(),
    )(q, k, v, la)


def _gla_bwd(q, k, v, la, s_all, do):
    T = q.shape[0]
    NB = GLA_BLOCK // GLA_CHUNK
    C = GLA_CHUNK
    nblk = T // GLA_BLOCK

    def body(q_ref, k_ref, v_ref, la_ref, s_ref, do_ref, dq_ref, dk_ref, dv_ref, dz_ref, dst_ref, b_ref):
        @pl.when(pl.program_id(0) == 0)
        def _():
            dst_ref[...] = jnp.zeros_like(dst_ref)

        lane_masks, causal, anti_causal, blk = _gla_masks()
        b_ref[...] = _block_cumsum(la_ref[...], causal)
        for p in range(2):
            cols = slice(LANES * p, LANES * p + LANES)
            qs, kk, eb, ekl, eqm, ekm, gam = _gla_block_terms(q_ref, k_ref, b_ref, p)
            qh, kh, qm, km = qs * eb, kk * ekl, qs * eqm, kk * ekm
            vp = v_ref[:, 256 * p:256 * p + 256]
            dop = do_ref[:, 256 * p:256 * p + 256]
            dqm = jnp.zeros((GLA_BLOCK, LANES), F32)
            dkm = jnp.zeros((GLA_BLOCK, LANES), F32)
            dv_intra = []
            for j in range(2):
                hs = slice(128 * j, 128 * j + 128)
                a = jnp.where(causal, _mm_nt(qm * lane_masks[j], km), 0.0)
                da = jnp.where(causal, _mm_nt(dop[:, hs], vp[:, hs]), 0.0)
                dv_intra.append(_mm_tn(a, dop[:, hs]))
                dqm = dqm + lane_masks[j] * _mm(da, km)
                dkm = dkm + lane_masks[j] * _mm_tn(da, qm)
            grow = [blk * _mm_tn(dop[C * c:C * c + C], qh[C * c:C * c + C]) for c in range(NB)]
            dst = dst_ref[p]
            dst_after = [None] * NB
            for c in reversed(range(NB)):
                dst_after[c] = dst
                dst = dst * gam[c] + grow[c]
            dst_ref[p] = dst
            dqh, dkh, dv_state, extra = [], [], [], []
            for c in range(NB):
                rows = slice(C * c, C * c + C)
                packed = s_ref[c, p]
                st = jnp.concatenate([packed * lane_masks[0][0:LANES], packed * lane_masks[1][0:LANES]], axis=0)
                dqh.append(_mm(dop[rows], st))
                dkh.append(_mm(vp[rows], dst_after[c]))
                dv_state.append(_mm_nt(kh[rows], dst_after[c]))
                extra.append(jnp.sum(dkh[c] * kh[rows], axis=0, keepdims=True)
                             + jnp.sum(st * dst_after[c], axis=0, keepdims=True) * gam[c])
            dqs = jnp.concatenate(dqh, axis=0) * eb + dqm * eqm
            dk = jnp.concatenate(dkh, axis=0) * ekl + dkm * ekm
            dg = _mm_f32(anti_causal.astype(F32), dqs * qs - dk * kk) + _chunk_rows(extra)
            dq_ref[:, cols] = (dqs * 0.125).astype(BF16)
            dk_ref[:, cols] = dk.astype(BF16)
            dz_ref[:, cols] = dg * (1.0 - jnp.exp(GLA_TAU * la_ref[:, cols])) * (1.0 / GLA_TAU)
            dv_ref[:, 256 * p:256 * p + 256] = (jnp.concatenate(dv_state, axis=0) + jnp.concatenate(dv_intra, axis=1)).astype(BF16)

    rev = lambda i: (nblk - 1 - i, 0)
    rspec = lambda cols: pl.BlockSpec((GLA_BLOCK, cols), rev)
    return pl.pallas_call(
        body, name="gla_bwd", grid=(nblk,),
        out_shape=[jax.ShapeDtypeStruct((T, GLA_QK), BF16), jax.ShapeDtypeStruct((T, GLA_QK), BF16),
                   jax.ShapeDtypeStruct((T, GLA_V), BF16), jax.ShapeDtypeStruct((T, GLA_QK), F32)],
        in_specs=[rspec(GLA_QK), rspec(GLA_QK), rspec(GLA_V), rspec(GLA_QK),
                  pl.BlockSpec((NB, 2, LANES, LANES), lambda i: (nblk - 1 - i, 0, 0, 0)), rspec(GLA_V)],
        out_specs=[rspec(GLA_QK), rspec(GLA_QK), rspec(GLA_V), rspec(GLA_QK)],
        scratch_shapes=[pltpu.VMEM((2, 256, LANES), F32), pltpu.VMEM((GLA_BLOCK, GLA_QK), F32)],
        compiler_params=_params(),
    )(q, k, v, la, s_all, do)


SWA_GROUP = 4


def _swa_stack(ref, g, first):
    parts = []
    for j in range(SWA_GROUP):
        m = 2 * g + j // 2
        pair = ref[:, 128 * m:128 * m + 128]
        zero = jnp.zeros_like(pair)
        parts.append(jnp.where(first, pair, zero) if j % 2 == 0 else jnp.where(first, zero, pair))
    return jnp.concatenate(parts, axis=0)


def _swa_unstack(rows, mm, first):
    W = SWA_BLOCK
    return jnp.where(first, rows[W * 2 * mm:W * (2 * mm + 1)], rows[W * (2 * mm + 1):W * (2 * mm + 2)])


def _swa_probs(qs, kp, kc, vp, vc, i, g, sink_ref, first4):
    W = SWA_BLOCK
    R = SWA_GROUP * W
    r, c = _iota((R, W), 0) & (W - 1), _iota((R, W), 1)
    neg = -1e30
    s_p = jnp.where((c > r) & (i > 0), _mm_nt(qs, kp), neg)
    s_c = jnp.where(c <= r, _mm_nt(qs, kc), neg)
    head = _iota((R, 1), 0) >> 7
    sink = jnp.where(head == 0, sink_ref[4 * g], jnp.where(head == 1, sink_ref[4 * g + 1],
                                                           jnp.where(head == 2, sink_ref[4 * g + 2], sink_ref[4 * g + 3])))
    m = jnp.maximum(jnp.max(jnp.maximum(s_p, s_c), axis=-1, keepdims=True), sink)
    p_p = jnp.exp(s_p - m)
    p_c = jnp.exp(s_c - m)
    p_s = jnp.exp(sink - m)
    one = jnp.ones((W, LANES), BF16)
    first = _iota((W, LANES), 1) < 64
    acc = _mm(p_p, jnp.where(first, vp, one)) + _mm(p_c, jnp.where(first, vc, one))
    rolled = pltpu.roll(acc, 64, 1)
    denom = jnp.where(first4, rolled, acc) + p_s
    return p_p, p_c, p_s, denom, acc, rolled


def _swa_fwd(sq, kd, vd, sinks):
    T = sq.shape[0]
    W = SWA_BLOCK
    prev = lambda i: (jnp.maximum(i - 1, 0), 0)

    def body(sink_ref, q_ref, kp_ref, kc_ref, vp_ref, vc_ref, o_ref):
        i = pl.program_id(0)
        first4 = _iota((SWA_GROUP * W, LANES), 1) < 64
        first = _iota((W, LANES), 1) < 64
        for g in range(2):
            gs = slice(128 * g, 128 * g + 128)
            qs = _swa_stack(q_ref, g, first)
            _, _, _, denom, acc, rolled = _swa_probs(qs, kp_ref[:, gs], kc_ref[:, gs], vp_ref[:, gs], vc_ref[:, gs],
                                                     i, g, sink_ref, first4)
            pv = jnp.where(first4, acc, rolled)
            o = pv / denom
            for mm in range(2):
                m = 2 * g + mm
                o_ref[:, 128 * m:128 * m + 128] = _swa_unstack(o, mm, first).astype(BF16)

    return pl.pallas_call(
        body, name="swa_fwd", grid=(T // W,), out_shape=jax.ShapeDtypeStruct((T, SWA_Q), BF16),
        in_specs=[pl.BlockSpec(memory_space=pltpu.SMEM), _row_spec(W, SWA_Q), pl.BlockSpec((W, 256), prev),
                  _row_spec(W, 256), pl.BlockSpec((W, 256), prev), _row_spec(W, 256)],
        out_specs=_row_spec(W, SWA_Q),
        compiler_params=_params(),
    )(sinks, sq, kd, kd, vd, vd)


def _swa_bwd(sq, kd, vd, sinks, do):
    T = sq.shape[0]
    W = SWA_BLOCK
    n = T // W
    cur = lambda i: (jnp.minimum(i, n - 1), 0)
    prev = lambda i: (jnp.clip(i - 1, 0, n - 1), 0)

    def body(sink_ref, q_ref, kp_ref, kc_ref, vp_ref, vc_ref, do_ref, dq_ref, dk_ref, dv_ref, ds_ref, ck_ref, cv_ref):
        i = pl.program_id(0)

        @pl.when(i == 0)
        def _():
            ds_ref[...] = jnp.zeros_like(ds_ref)
            ck_ref[...] = jnp.zeros_like(ck_ref)
            cv_ref[...] = jnp.zeros_like(cv_ref)

        @pl.when(i < n)
        def _():
            first4 = _iota((SWA_GROUP * W, LANES), 1) < 64
            first = _iota((W, LANES), 1) < 64
            for g in range(2):
                gs = slice(128 * g, 128 * g + 128)
                kp, kc, vp, vc = kp_ref[:, gs], kc_ref[:, gs], vp_ref[:, gs], vc_ref[:, gs]
                qs = _swa_stack(q_ref, g, first)
                dos = _swa_stack(do_ref, g, first)
                p_p, p_c, p_s, denom, _, _ = _swa_probs(qs, kp, kc, vp, vc, i, g, sink_ref, first4)
                inv = 1.0 / denom
                p_p, p_c = p_p * inv, p_c * inv
                dp_p = _mm_nt(dos, vp)
                dp_c = _mm_nt(dos, vc)
                delta = jnp.sum(p_p * dp_p + p_c * dp_c, axis=-1, keepdims=True)
                ds_p = p_p * (dp_p - delta)
                ds_c = p_c * (dp_c - delta)
                rows = slice(SWA_GROUP * W * g, SWA_GROUP * W * (g + 1))
                ds_ref[rows, :] = ds_ref[rows, :] - (p_s * delta) * inv
                dq = (_mm(ds_p, kp) + _mm(ds_c, kc)) * 0.125
                for mm in range(2):
                    m = 2 * g + mm
                    dq_ref[:, 128 * m:128 * m + 128] = _swa_unstack(dq, mm, first).astype(BF16)
                dk_ref[:, gs] = (ck_ref[:, gs] + _mm_tn(ds_p, qs)).astype(BF16)
                dv_ref[:, gs] = (cv_ref[:, gs] + _mm_tn(p_p, dos)).astype(BF16)
                ck_ref[:, gs] = _mm_tn(ds_c, qs)
                cv_ref[:, gs] = _mm_tn(p_c, dos)

        @pl.when(i == n)
        def _():
            dk_ref[...] = ck_ref[...].astype(BF16)
            dv_ref[...] = cv_ref[...].astype(BF16)

    return pl.pallas_call(
        body, name="swa_bwd", grid=(n + 1,),
        out_shape=[jax.ShapeDtypeStruct((T, SWA_Q), BF16), jax.ShapeDtypeStruct((T, 256), BF16),
                   jax.ShapeDtypeStruct((T, 256), BF16), jax.ShapeDtypeStruct((SWA_HEADS * W, LANES), F32)],
        in_specs=[pl.BlockSpec(memory_space=pltpu.SMEM), pl.BlockSpec((W, SWA_Q), cur), pl.BlockSpec((W, 256), prev),
                  pl.BlockSpec((W, 256), cur), pl.BlockSpec((W, 256), prev), pl.BlockSpec((W, 256), cur),
                  pl.BlockSpec((W, SWA_Q), cur)],
        out_specs=[pl.BlockSpec((W, SWA_Q), cur), pl.BlockSpec((W, 256), prev), pl.BlockSpec((W, 256), prev),
                   _const_spec((SWA_HEADS * W, LANES))],
        scratch_shapes=[pltpu.VMEM((W, 256), F32), pltpu.VMEM((W, 256), F32)],
        compiler_params=_params(),
    )(sinks, sq, kd, kd, vd, vd, do)


def _mix_out_fwd(x, og, gg, osw, gnorm, wout, g2):
    T = x.shape[0]
    TM = _wide_tile(T)

    def body(x_ref, og_ref, gg_ref, osw_ref, gn_ref, wout_ref, g2_ref, x1_ref, cat_ref):
        gn = gn_ref[...]
        for j in range(GLA_HEADS):
            hs = slice(128 * j, 128 * j + 128)
            o = og_ref[:, hs].astype(F32)
            r = lax.rsqrt(jnp.mean(o * o, axis=-1, keepdims=True) + EPS)
            gate = gg_ref[:, hs].astype(F32)
            cat_ref[:, hs] = (o * r * gn * (gate * _sigmoid(gate))).astype(BF16)
        cat_ref[:, GLA_V:] = osw_ref[...]
        mix = _mm(cat_ref[...], wout_ref[...])
        r2 = lax.rsqrt(jnp.mean(mix * mix, axis=-1, keepdims=True) + EPS)
        x1_ref[...] = x_ref[...] + mix * r2 * g2_ref[...]

    return pl.pallas_call(
        body, name="mix_out_fwd", grid=(T // TM,),
        out_shape=[jax.ShapeDtypeStruct((T, D_MODEL), F32), jax.ShapeDtypeStruct((T, D_MODEL), BF16)],
        in_specs=[_row_spec(TM, D_MODEL), _row_spec(TM, GLA_V), _row_spec(TM, GLA_V), _row_spec(TM, SWA_Q),
                  _const_spec((1, LANES)), _const_spec((D_MODEL, D_MODEL)), _const_spec((1, D_MODEL))],
        out_specs=[_row_spec(TM, D_MODEL), _row_spec(TM, D_MODEL)],
        compiler_params=_params(),
    )(x, og, gg, osw, gnorm, wout, g2)


HALO = 8


def _rows_before(v, prev1, prev2):
    row = _iota(v.shape, 0)
    m1 = jnp.where(row == 0, prev1, pltpu.roll(v, 1, 0))
    m2 = jnp.where(row == 0, prev2, jnp.where(row == 1, prev1, pltpu.roll(v, 2, 0)))
    return m1, m2


def _rows_after(v, next1, next2):
    n = v.shape[0]
    row = _iota(v.shape, 0)
    p1 = jnp.where(row == n - 1, next1, pltpu.roll(v, n - 1, 0))
    p2 = jnp.where(row == n - 1, next2, jnp.where(row == n - 2, next1, pltpu.roll(v, n - 2, 0)))
    return p1, p2


def _ff_pieces():
    return [(j, off, wd) for j in range(2) for off, wd in FF_PIECES]


def _ffn_fwd(x1, g3, wup, cw, cb, wdown, g4, target):
    T = x1.shape[0]

    def body(x1_ref, g3_ref, wup_hbm, cw_ref, cb_ref, wdn_hbm, g4_ref, tg_ref,
             h2_ref, up_ref, a_ref, c1_ref, c2_ref, y_ref, dx2_ref, loss_ref, wup_v, wdn_v, carry_ref, sems):
        _load_once(wup_hbm, wup_v, sems.at[0])
        _load_once(wdn_hbm, wdn_v, sems.at[1])

        @pl.when(pl.program_id(0) == 0)
        def _():
            carry_ref[...] = jnp.zeros_like(carry_ref)
            loss_ref[...] = jnp.zeros_like(loss_ref)

        x1 = x1_ref[...]
        r3 = lax.rsqrt(jnp.mean(x1 * x1, axis=-1, keepdims=True) + EPS)
        h2 = (x1 * r3 * g3_ref[...]).astype(BF16)
        h2_ref[...] = h2
        for j, off, wd in _ff_pieces():
            base = SHARD_FF * j + off
            u = []
            for half in range(2):
                cs = slice(D_FF * half + base, D_FF * half + base + wd)
                upb = _mm(h2, wup_v[2 * half + j, :, off:off + wd]).astype(BF16)
                up_ref[:, cs] = upb
                upf = upb.astype(F32)
                m1, m2 = _rows_before(upf, carry_ref[HALO - 1:HALO, cs], carry_ref[HALO - 2:HALO - 1, cs])
                u.append(cb_ref[:, cs] + cw_ref[0:1, cs] * m2 + cw_ref[1:2, cs] * m1 + cw_ref[2:3, cs] * upf)
                carry_ref[:, cs] = upf[TM - HALO:TM, :]
            act, dact = _gelu_parts(u[1])
            a = (act * u[0]).astype(BF16)
            out = slice(base, base + wd)
            a_ref[:, out] = a
            c1_ref[:, out] = act.astype(BF16)
            c2_ref[:, out] = (u[0] * dact).astype(BF16)
        y = _mm(a_ref[...], wdn_v[...])
        y_ref[...] = y
        r4 = lax.rsqrt(jnp.mean(y * y, axis=-1, keepdims=True) + EPS)
        err = x1 + y * r4 * g4_ref[...] - tg_ref[...]
        dx2_ref[...] = err * (1.0 / D_MODEL)
        loss_ref[...] = loss_ref[...] + jnp.sum(err * err) * (0.5 / D_MODEL)

    outs = [
        jax.ShapeDtypeStruct((T, D_MODEL), BF16),
        jax.ShapeDtypeStruct((T, 2 * D_FF), BF16),
        jax.ShapeDtypeStruct((T, D_FF), BF16),
        jax.ShapeDtypeStruct((T, D_FF), BF16),
        jax.ShapeDtypeStruct((T, D_FF), BF16),
        jax.ShapeDtypeStruct((T, D_MODEL), F32),
        jax.ShapeDtypeStruct((T, D_MODEL), F32),
        jax.ShapeDtypeStruct((8, LANES), F32),
    ]
    return pl.pallas_call(
        body, name="ffn_fwd", grid=(T // TM,), out_shape=outs,
        in_specs=[_row_spec(TM, D_MODEL), _const_spec((1, D_MODEL)), _any_spec(), _const_spec((3, 2 * D_FF)),
                  _const_spec((1, 2 * D_FF)), _any_spec(), _const_spec((1, D_MODEL)), _row_spec(TM, D_MODEL)],
        out_specs=[_row_spec(TM, D_MODEL), _row_spec(TM, 2 * D_FF), _row_spec(TM, D_FF), _row_spec(TM, D_FF),
                   _row_spec(TM, D_FF), _row_spec(TM, D_MODEL), _row_spec(TM, D_MODEL), _const_spec((8, LANES))],
        scratch_shapes=[pltpu.VMEM((N_SHARD, D_MODEL, SHARD_FF), BF16), pltpu.VMEM((D_FF, D_MODEL), BF16),
                        pltpu.VMEM((HALO, 2 * D_FF), F32), pltpu.SemaphoreType.DMA((2,))],
        compiler_params=_params(),
    )(x1, g3, wup, cw, cb, wdown, g4, target)


def _ffn_bwd(dx2, y, g4, up, c1, c2, cw, wdown, wup, x1, g3):
    T = dx2.shape[0]
    nt = T // TM
    rev = lambda i: (nt - 1 - i, 0)

    def body(dn_ref, y_ref, g4_ref, up_ref, c1_ref, c2_ref, cw_ref, wdn_hbm, wup_hbm, x1_ref, g3_ref,
             dy_ref, dup_ref, dx1_ref, dg4_ref, dg3_ref, dcb_ref, dcw_ref, wup_v, wdn_v, carry_ref, sems):
        _load_once(wup_hbm, wup_v, sems.at[0])
        _load_once(wdn_hbm, wdn_v, sems.at[1])

        @pl.when(pl.program_id(0) == 0)
        def _():
            carry_ref[...] = jnp.zeros_like(carry_ref)
            dg4_ref[...] = jnp.zeros_like(dg4_ref)
            dg3_ref[...] = jnp.zeros_like(dg3_ref)
            dcb_ref[...] = jnp.zeros_like(dcb_ref)
            dcw_ref[...] = jnp.zeros_like(dcw_ref)

        dn = dn_ref[...]
        y = y_ref[...]
        g4v = g4_ref[...]
        r4 = lax.rsqrt(jnp.mean(y * y, axis=-1, keepdims=True) + EPS)
        dg4_ref[...] = dg4_ref[...] + jnp.sum(dn * y * r4, axis=0, keepdims=True)
        dy = _rms_bwd(y, r4, g4v, dn).astype(BF16)
        dy_ref[...] = dy
        dh2 = jnp.zeros((TM, D_MODEL), F32)
        for j, off, wd in _ff_pieces():
            base = SHARD_FF * j + off
            da = _mm_nt(dy, wdn_v[base:base + wd, :])
            for half, coef_ref in enumerate((c1_ref, c2_ref)):
                cs = slice(D_FF * half + base, D_FF * half + base + wd)
                du = da * coef_ref[:, base:base + wd].astype(F32)
                p1, p2 = _rows_after(du, carry_ref[0:1, cs], carry_ref[1:2, cs])
                carry_ref[:, cs] = du[0:HALO, :]
                upf = up_ref[:, cs].astype(F32)
                dcb_ref[:, cs] = dcb_ref[:, cs] + jnp.sum(du, axis=0, keepdims=True)
                dcw_ref[0:1, cs] = dcw_ref[0:1, cs] + jnp.sum(p2 * upf, axis=0, keepdims=True)
                dcw_ref[1:2, cs] = dcw_ref[1:2, cs] + jnp.sum(p1 * upf, axis=0, keepdims=True)
                dcw_ref[2:3, cs] = dcw_ref[2:3, cs] + jnp.sum(du * upf, axis=0, keepdims=True)
                dup = (cw_ref[2:3, cs] * du + cw_ref[1:2, cs] * p1 + cw_ref[0:1, cs] * p2).astype(BF16)
                dup_ref[:, cs] = dup
                dh2 = dh2 + _mm_nt(dup, wup_v[2 * half + j, :, off:off + wd])
        x1 = x1_ref[...]
        r3 = lax.rsqrt(jnp.mean(x1 * x1, axis=-1, keepdims=True) + EPS)
        dg3_ref[...] = dg3_ref[...] + jnp.sum(dh2 * x1 * r3, axis=0, keepdims=True)
        dx1_ref[...] = dn + _rms_bwd(x1, r3, g3_ref[...], dh2)

    outs = [
        jax.ShapeDtypeStruct((T, D_MODEL), BF16),
        jax.ShapeDtypeStruct((T, 2 * D_FF), BF16),
        jax.ShapeDtypeStruct((T, D_MODEL), F32),
        jax.ShapeDtypeStruct((1, D_MODEL), F32),
        jax.ShapeDtypeStruct((1, D_MODEL), F32),
        jax.ShapeDtypeStruct((1, 2 * D_FF), F32),
        jax.ShapeDtypeStruct((3, 2 * D_FF), F32),
    ]
    return pl.pallas_call(
        body, name="ffn_bwd", grid=(nt,), out_shape=outs,
        in_specs=[pl.BlockSpec((TM, D_MODEL), rev), pl.BlockSpec((TM, D_MODEL), rev), _const_spec((1, D_MODEL)),
                  pl.BlockSpec((TM, 2 * D_FF), rev), pl.BlockSpec((TM, D_FF), rev), pl.BlockSpec((TM, D_FF), rev),
                  _const_spec((3, 2 * D_FF)), _any_spec(), _any_spec(), pl.BlockSpec((TM, D_MODEL), rev),
                  _const_spec((1, D_MODEL))],
        out_specs=[pl.BlockSpec((TM, D_MODEL), rev), pl.BlockSpec((TM, 2 * D_FF), rev), pl.BlockSpec((TM, D_MODEL), rev),
                   _const_spec((1, D_MODEL)), _const_spec((1, D_MODEL)), _const_spec((1, 2 * D_FF)),
                   _const_spec((3, 2 * D_FF))],
        scratch_shapes=[pltpu.VMEM((N_SHARD, D_MODEL, SHARD_FF), BF16), pltpu.VMEM((D_FF, D_MODEL), BF16),
                        pltpu.VMEM((HALO, 2 * D_FF), F32), pltpu.SemaphoreType.DMA((2,))],
        compiler_params=_params(),
    )(dx2, y, g4, up, c1, c2, cw, wdown, wup, x1, g3)


def _mix_out_bwd(dx1, cat, g2, wout, og, gg, gnorm):
    T = dx1.shape[0]
    TM = _wide_tile(T)

    def body(dx1_ref, cat_ref, g2_ref, wout_ref, og_ref, gg_ref, gn_ref,
             dmix_ref, dog_ref, dgg_ref, dosw_ref, dg2_ref, dgn_ref):
        @pl.when(pl.program_id(0) == 0)
        def _():
            dg2_ref[...] = jnp.zeros_like(dg2_ref)
            dgn_ref[...] = jnp.zeros_like(dgn_ref)

        dx1 = dx1_ref[...]
        mix = _mm(cat_ref[...], wout_ref[...])
        r2 = lax.rsqrt(jnp.mean(mix * mix, axis=-1, keepdims=True) + EPS)
        dg2_ref[...] = dg2_ref[...] + jnp.sum(dx1 * mix * r2, axis=0, keepdims=True)
        dmix = _rms_bwd(mix, r2, g2_ref[...], dx1).astype(BF16)
        dmix_ref[...] = dmix
        dcat = _mm_nt(dmix, wout_ref[...])
        dosw_ref[...] = dcat[:, GLA_V:].astype(BF16)
        gn = gn_ref[...]
        dgn = jnp.zeros((1, LANES), F32)
        for j in range(GLA_HEADS):
            hs = slice(128 * j, 128 * j + 128)
            o = og_ref[:, hs].astype(F32)
            r = lax.rsqrt(jnp.mean(o * o, axis=-1, keepdims=True) + EPS)
            gate = gg_ref[:, hs].astype(F32)
            sg = _sigmoid(gate)
            dgated = dcat[:, hs]
            dnorm = dgated * (gate * sg)
            dgg_ref[:, hs] = (dgated * (o * r * gn) * (sg * (1.0 + gate * (1.0 - sg)))).astype(BF16)
            dgn = dgn + jnp.sum(dnorm * o * r, axis=0, keepdims=True)
            dog_ref[:, hs] = _rms_bwd(o, r, gn, dnorm)
        dgn_ref[...] = dgn_ref[...] + dgn

    return pl.pallas_call(
        body, name="mix_out_bwd", grid=(T // TM,),
        out_shape=[jax.ShapeDtypeStruct((T, D_MODEL), BF16), jax.ShapeDtypeStruct((T, GLA_V), F32),
                   jax.ShapeDtypeStruct((T, GLA_V), BF16), jax.ShapeDtypeStruct((T, SWA_Q), BF16),
                   jax.ShapeDtypeStruct((1, D_MODEL), F32), jax.ShapeDtypeStruct((1, LANES), F32)],
        in_specs=[_row_spec(TM, D_MODEL), _row_spec(TM, D_MODEL), _const_spec((1, D_MODEL)),
                  _const_spec((D_MODEL, D_MODEL)), _row_spec(TM, GLA_V), _row_spec(TM, GLA_V), _const_spec((1, LANES))],
        out_specs=[_row_spec(TM, D_MODEL), _row_spec(TM, GLA_V), _row_spec(TM, GLA_V), _row_spec(TM, SWA_Q),
                   _const_spec((1, D_MODEL)), _const_spec((1, LANES))],
        compiler_params=_params(),
    )(dx1, cat, g2, wout, og, gg, gnorm)


def _proj_bwd(x, g1, wp, gup, glr, dq, dk, dv, dgg, dsq, dkd, dvd, dz, rc, rsa, rsb, dx1):
    T = x.shape[0]
    TM = _wide_tile(T)

    def body(x_ref, g1_ref, wp_hbm, gup_ref, glr_ref, dq_ref, dk_ref, dv_ref, dgg_ref, dsq_ref, dkd_ref, dvd_ref,
             dz_ref, rc_ref, rsa_ref, rsb_ref, dx1_ref, dx_ref, dp_ref, dg1_ref, dgup_ref, dgb_ref, wp_v, sem):
        _load_once(wp_hbm, wp_v, sem)

        @pl.when(pl.program_id(0) == 0)
        def _():
            dg1_ref[...] = jnp.zeros_like(dg1_ref)
            dgup_ref[...] = jnp.zeros_like(dgup_ref)
            dgb_ref[...] = jnp.zeros_like(dgb_ref)

        rc_, rsa_, rsb_ = rc_ref[...], rsa_ref[...], rsb_ref[...]
        dp_ref[:, 0:256] = dq_ref[...]
        dp_ref[:, 256:512] = dk_ref[...]
        dp_ref[:, 512:1024] = dv_ref[...]
        dp_ref[:, 1024:1536] = dgg_ref[...]
        for s in range(4):
            cs = slice(128 * s, 128 * s + 128)
            dp_ref[:, 1536 + 128 * s:1664 + 128 * s] = _rotate_bwd(dsq_ref[:, cs].astype(F32), rc_, rsa_, rsb_).astype(BF16)
        first = _iota((TM, LANES), 1) < 64
        dk0 = dkd_ref[:, 0:128].astype(F32)
        dk1 = dkd_ref[:, 128:256].astype(F32)
        dkr = jnp.where(first, dk0 + pltpu.roll(dk0, 64, 1), dk1 + pltpu.roll(dk1, 64, 1))
        dp_ref[:, 2048:2176] = _rotate_bwd(dkr, rc_, rsa_, rsb_).astype(BF16)
        dv0 = dvd_ref[:, 0:128].astype(F32)
        dv1 = dvd_ref[:, 128:256].astype(F32)
        dp_ref[:, 2176:2304] = jnp.where(first, dv0 + pltpu.roll(dv0, 64, 1), dv1 + pltpu.roll(dv1, 64, 1)).astype(BF16)
        dz = dz_ref[...]
        dzb = dz.astype(BF16)
        dp_ref[:, 2304:2432] = _mm_nt(dzb, gup_ref[...]).astype(BF16)
        dgup_ref[...] = dgup_ref[...] + _mm_tn(glr_ref[...], dzb)
        dgb_ref[...] = dgb_ref[...] + jnp.sum(dz, axis=0, keepdims=True)
        dh1 = _mm_nt(dp_ref[...], wp_v[...])
        xt = x_ref[...]
        r = lax.rsqrt(jnp.mean(xt * xt, axis=-1, keepdims=True) + EPS)
        dg1_ref[...] = dg1_ref[...] + jnp.sum(dh1 * xt * r, axis=0, keepdims=True)
        dx_ref[...] = dx1_ref[...] + _rms_bwd(xt, r, g1_ref[...], dh1)

    row = lambda cols: _row_spec(TM, cols)
    return pl.pallas_call(
        body, name="proj_bwd", grid=(T // TM,),
        out_shape=[jax.ShapeDtypeStruct((T, D_MODEL), F32), jax.ShapeDtypeStruct((T, IN_WIDTH_PAD), BF16),
                   jax.ShapeDtypeStruct((1, D_MODEL), F32), jax.ShapeDtypeStruct((LANES, GLA_QK), F32),
                   jax.ShapeDtypeStruct((1, GLA_QK), F32)],
        in_specs=[row(D_MODEL), _const_spec((1, D_MODEL)), _any_spec(), _const_spec((LANES, GLA_QK)), row(LANES),
                  row(GLA_QK), row(GLA_QK), row(GLA_V), row(GLA_V), row(SWA_Q), row(256), row(256), row(GLA_QK),
                  row(LANES), row(LANES), row(LANES), row(D_MODEL)],
        out_specs=[row(D_MODEL), row(IN_WIDTH_PAD), _const_spec((1, D_MODEL)), _const_spec((LANES, GLA_QK)),
                   _const_spec((1, GLA_QK))],
        scratch_shapes=[pltpu.VMEM((D_MODEL, IN_WIDTH_PAD), BF16), pltpu.SemaphoreType.DMA],
        compiler_params=_params(),
    )(x, g1, wp, gup, glr, dq, dk, dv, dgg, dsq, dkd, dvd, dz, rc, rsa, rsb, dx1)


def _matmul_tn(a, b, tn, name, column_blocks_major=False, after=None):
    T, M = a.shape
    N = b.shape[1]
    tk = next(t for t in (2048, 1024, 512, TM) if T % t == 0 and t * (M + tn) <= 2048 * (D_MODEL + SHARD_FF))
    nk = T // tk
    if column_blocks_major:
        out_shape = jax.ShapeDtypeStruct((N // tn, M, tn), F32)
        out_spec = pl.BlockSpec((None, M, tn), lambda j, kk: (j, 0, 0))
    else:
        out_shape = jax.ShapeDtypeStruct((M, N), F32)
        out_spec = pl.BlockSpec((M, tn), lambda j, kk: (0, j))

    def body(a_ref, b_ref, *rest):
        o_ref = rest[-1]
        kk = pl.program_id(1)

        @pl.when(kk == 0)
        def _():
            o_ref[...] = jnp.zeros_like(o_ref)

        o_ref[...] = o_ref[...] + _mm_tn(a_ref[...], b_ref[...])

    ordering = [] if after is None else [after]
    return pl.pallas_call(
        body, name=name, grid=(N // tn, nk), out_shape=out_shape,
        in_specs=[pl.BlockSpec((tk, M), lambda j, kk: (kk, 0)), pl.BlockSpec((tk, tn), lambda j, kk: (kk, j))]
        + [_any_spec()] * len(ordering),
        out_specs=out_spec,
        compiler_params=_params(),
    )(a, b, *ordering)


def _adamw_update(w_ref, g_ref, m_ref, v_ref, d_ref, m2_ref, v2_ref):
    g_ = g_ref[...]
    m2 = ADAM_B1 * m_ref[...] + (1.0 - ADAM_B1) * g_
    v2 = ADAM_B2 * v_ref[...] + (1.0 - ADAM_B2) * (g_ * g_)
    m_hat = m2 / (1.0 - ADAM_B1 ** ADAM_STEP)
    v_hat = v2 / (1.0 - ADAM_B2 ** ADAM_STEP)
    d_ref[...] = -ADAM_LR * (m_hat / (jnp.sqrt(v_hat) + ADAM_EPS) + ADAM_WD * w_ref[...])
    m2_ref[...] = m2
    v2_ref[...] = v2


def _adamw(w, g, m, v, rows, name):
    R, C = w.shape

    def body(*refs):
        _adamw_update(*refs[:7])
        refs[7][...] = refs[1][...]

    spec = pl.BlockSpec((rows, C), lambda i: (i, 0))
    return pl.pallas_call(
        body, name=name, grid=(R // rows,), out_shape=[jax.ShapeDtypeStruct((R, C), F32)] * 4,
        in_specs=[spec] * 4, out_specs=[spec] * 4, compiler_params=_params(),
    )(w, g, m, v)


def _adamw_small(ws, gs, ms, vs):
    n = len(ws)

    def body(*refs):
        w_, g_, m_, v_, d_, m2_, v2_, g2_ = (refs[n * i:n * (i + 1)] for i in range(8))
        for k in range(n):
            _adamw_update(w_[k], g_[k], m_[k], v_[k], d_[k], m2_[k], v2_[k])
            g2_[k][...] = g_[k][...]

    vm = pl.BlockSpec(memory_space=pltpu.VMEM)
    outs = pl.pallas_call(
        body, name="adamw_small", out_shape=[jax.ShapeDtypeStruct(w.shape, F32) for w in ws] * 4,
        in_specs=[vm] * (4 * n), out_specs=[vm] * (4 * n),
    )(*ws, *gs, *ms, *vs)
    return outs[:n], outs[n:2 * n], outs[2 * n:3 * n], outs[3 * n:]


def _place():
    x, y, c = lax.axis_index("x"), lax.axis_index("y"), lax.axis_index("c")
    chips = [(1 - x, y), (x, 1 - y), (1 - x, 1 - y)]
    return x, y, c, chips


class _staged_copies:
    def __init__(self, srcs, dsts, stage, sems):
        n = len(srcs)
        self.loads = [pltpu.make_async_copy(srcs[k], stage[k], sems.at[k]) for k in range(n)]
        self.stores = [pltpu.make_async_copy(stage[k], dsts[k], sems.at[n + k]) for k in range(n)]

    def load(self):
        for cp in self.loads:
            cp.start()

    def store(self):
        for ld, st in zip(self.loads, self.stores):
            ld.wait()
            st.start()

    def finish(self):
        for cp in self.stores:
            cp.wait()


def _allgather_shards(parts, unit_rows):
    n = len(parts)
    units = [(k, r, unit_rows[k]) for k in range(n) for r in range(0, parts[k].shape[0] // 2, unit_rows[k])]
    nu = len(units)

    def body(*refs):
        ins, outs, stage = refs[:n], refs[n:2 * n], refs[2 * n:3 * n]
        send_sems, recv_sems, local_sems = refs[3 * n:]
        x, y, c, chips = _place()
        sibling = (x, y, 1 - c)
        own = _staged_copies(ins, [o.at[2 * x + y] for o in outs], stage, local_sems)

        def block(i, px, py, half):
            k, r, u = units[i]
            return outs[k].at[2 * px + py, pl.ds(half * (parts[k].shape[0] // 2) + r, u), :]

        def copy(i, j, px, py, half, to, src=None):
            return pltpu.make_async_remote_copy(
                src_ref=block(i, px, py, half) if src is None else src, dst_ref=block(i, px, py, half),
                send_sem=send_sems.at[nu * j + i], recv_sem=recv_sems.at[nu * j + i], device_id=to, device_id_type=MESH)

        own.load()
        first, passed = [], []
        for i, (k, r, u) in enumerate(units):
            for j, chip in enumerate(chips):
                src = ins[k].at[pl.ds(c * (parts[k].shape[0] // 2) + r, u), :]
                first.append(copy(i, j, x, y, c, (*chip, c), src=src))
                first[-1].start()
        own.store()
        for i in range(nu):
            for j, chip in enumerate(chips):
                copy(i, j, *chip, c, (x, y, c)).wait_recv()
                passed.append(copy(i, 3 + j, *chip, c, sibling))
                passed[-1].start()
        for i in range(nu):
            for j, chip in enumerate(chips):
                copy(i, 3 + j, *chip, 1 - c, (x, y, c)).wait_recv()
        for cp in first + passed:
            cp.wait_send()
        own.finish()

    return pl.pallas_call(
        body, name="allgather_shards", out_shape=[jax.ShapeDtypeStruct((N_SHARD,) + p.shape, p.dtype) for p in parts],
        in_specs=[_any_spec()] * n, out_specs=[_any_spec()] * n,
        scratch_shapes=[pltpu.VMEM(p.shape, p.dtype) for p in parts] + [
            pltpu.SemaphoreType.DMA((6 * nu,)), pltpu.SemaphoreType.DMA((6 * nu,)), pltpu.SemaphoreType.DMA((2 * n,))],
        compiler_params=_params(),
    )(*parts)


def _d2d_pieces(rows, piece_rows):
    return [(r, piece_rows) for r in range(0, rows, piece_rows)]


def _rs_pair_swap(arrs, piece_rows, name):
    n = len(arrs)

    def body(*refs):
        ins, outs = refs[:n], refs[n:2 * n]
        send_sems, recv_sems = refs[2 * n:]
        x, y, c, _ = _place()
        sibling = (x, y, 1 - c)
        for k in range(n):
            H = arrs[k].shape[1] // 2
            for s in range(N_SHARD):
                for r, pr in _d2d_pieces(H, piece_rows[k]):
                    pltpu.make_async_remote_copy(
                        src_ref=ins[k].at[s, pl.ds((1 - c) * H + r, pr), :], dst_ref=outs[k].at[s, pl.ds(r, pr), :],
                        send_sem=send_sems.at[k], recv_sem=recv_sems.at[k], device_id=sibling, device_id_type=MESH).start()
        for k in range(n):
            H = arrs[k].shape[1] // 2
            whole = pltpu.make_async_remote_copy(
                src_ref=ins[k].at[:, pl.ds(0, H), :], dst_ref=outs[k], send_sem=send_sems.at[k], recv_sem=recv_sems.at[k],
                device_id=sibling, device_id_type=MESH)
            whole.wait_recv()
            whole.wait_send()

    return pl.pallas_call(
        body, name=name,
        out_shape=[jax.ShapeDtypeStruct((N_SHARD, a.shape[1] // 2, a.shape[2]), F32) for a in arrs],
        in_specs=[_any_spec()] * n, out_specs=[_any_spec()] * n,
        scratch_shapes=[pltpu.SemaphoreType.DMA((n,)), pltpu.SemaphoreType.DMA((n,))],
    )(*arrs)


def _rs_add_pair(a, got, core, chip, rows, name):
    _, H, C = got.shape
    nb = H // rows

    def body(c_ref, chip_ref, a_ref, b_ref, o_ref, land_ref):
        total = (a_ref[...] + b_ref[...]).astype(BF16)
        o_ref[...] = total

        @pl.when(pl.program_id(1) == chip_ref[0])
        def _():
            land_ref[...] = total

    spec = pl.BlockSpec((1, rows, C), lambda r, s, c_ref, chip_ref: (s, r, 0))
    shape = jax.ShapeDtypeStruct(got.shape, BF16)
    return pl.pallas_call(
        body, name=name, out_shape=[shape, shape],
        grid_spec=pltpu.PrefetchScalarGridSpec(
            num_scalar_prefetch=2, grid=(nb, N_SHARD),
            in_specs=[pl.BlockSpec((1, rows, C), lambda r, s, c_ref, chip_ref: (s, c_ref[0] * nb + r, 0)), spec],
            out_specs=[spec, pl.BlockSpec((1, rows, C), lambda r, s, c_ref, chip_ref: (chip_ref[0], r, 0))]),
        compiler_params=_params(),
    )(core, chip, a, got)


def _rs_sum_chips(parts, rows, name):
    _, H, C = parts.shape

    def body(p_ref, o_ref):
        o_ref[...] = ((p_ref[0].astype(F32) + p_ref[1].astype(F32)) + p_ref[2].astype(F32)) + p_ref[3].astype(F32)

    return pl.pallas_call(
        body, name=name, grid=(H // rows,), out_shape=jax.ShapeDtypeStruct((H, C), F32),
        in_specs=[pl.BlockSpec((N_SHARD, rows, C), lambda r: (0, r, 0))],
        out_specs=pl.BlockSpec((rows, C), lambda r: (r, 0)), compiler_params=_params(),
    )(parts)


def _rs_pair_share(halves, piece_rows, name):
    n = len(halves)

    def body(*refs):
        ins, outs, stage = refs[:n], refs[n:2 * n], refs[2 * n:3 * n]
        send_sems, recv_sems, local_sems = refs[3 * n:]
        x, y, c, _ = _place()
        sibling = (x, y, 1 - c)
        own = _staged_copies(ins, [outs[k].at[pl.ds(c * halves[k].shape[0], halves[k].shape[0]), :] for k in range(n)],
                             stage, local_sems)
        own.load()
        for k in range(n):
            H = halves[k].shape[0]
            for r, pr in _d2d_pieces(H, piece_rows[k]):
                pltpu.make_async_remote_copy(
                    src_ref=ins[k].at[pl.ds(r, pr), :], dst_ref=outs[k].at[pl.ds(c * H + r, pr), :],
                    send_sem=send_sems.at[k], recv_sem=recv_sems.at[k], device_id=sibling, device_id_type=MESH).start()
        own.store()
        for k in range(n):
            H = halves[k].shape[0]
            whole = pltpu.make_async_remote_copy(
                src_ref=ins[k], dst_ref=outs[k].at[pl.ds((1 - c) * H, H), :], send_sem=send_sems.at[k],
                recv_sem=recv_sems.at[k], device_id=sibling, device_id_type=MESH)
            whole.wait_recv()
            whole.wait_send()
        own.finish()

    return pl.pallas_call(
        body, name=name, out_shape=[jax.ShapeDtypeStruct((2 * h.shape[0], h.shape[1]), F32) for h in halves],
        in_specs=[_any_spec()] * n, out_specs=[_any_spec()] * n,
        scratch_shapes=[pltpu.VMEM(h.shape, F32) for h in halves] + [
            pltpu.SemaphoreType.DMA((n,)), pltpu.SemaphoreType.DMA((n,)), pltpu.SemaphoreType.DMA((2 * n,))],
        compiler_params=_params(),
    )(*halves)


_HBM = pl.BlockSpec(memory_space=pltpu.HBM)
_SEM = pl.BlockSpec(memory_space=pltpu.SEMAPHORE)
_EFFECT = pltpu.SideEffectType.DATAFLOW_SIDE_EFFECTING


def _gather_plan(srcs, lands, x, y, c, chips):
    plan = []
    for k in range(len(srcs)):
        H = srcs[k].shape[0] // 2
        for px, py in chips:
            plan.append((srcs[k].at[pl.ds(c * H, H), :], lands[k].at[2 * x + y, pl.ds(c * H, H), :], (px, py, c),
                         lands[k].at[2 * px + py, pl.ds(c * H, H), :]))
    return plan


def _scatter_plan(srcs, lands, x, y, c, chips):
    plan = []
    for k in range(len(srcs)):
        for px, py in chips:
            plan.append((srcs[k].at[2 * px + py], lands[k].at[2 * x + y], (px, py, c), lands[k].at[2 * px + py]))
    return plan


def _swap_plan(srcs, lands, x, y, c, chips):
    plan = []
    for k in range(len(srcs)):
        H = srcs[k].shape[1] // 2
        rows = H // SWAP_PIECES
        for s in range(N_SHARD):
            for r in range(0, H, rows):
                piece = lands[k].at[s, pl.ds(r, rows), :]
                plan.append((srcs[k].at[s, pl.ds((1 - c) * H + r, rows), :], piece, (x, y, 1 - c), piece))
    return plan


SWAP_PIECES = 8
COPIES_PER_ARRAY = {_gather_plan: 3, _scatter_plan: 3, _swap_plan: N_SHARD * SWAP_PIECES}


def _ici_start(srcs, lands, make_plan, name):
    n = len(srcs)
    ncopy = COPIES_PER_ARRAY[make_plan] * n

    def body(*refs):
        ins, lnd = refs[:n], refs[n:2 * n]
        send_sems, recv_sems = refs[2 * n], refs[2 * n + 1]
        token = refs[-1]
        x, y, c, chips = _place()
        for i, (src, dst, peer, _) in enumerate(make_plan(ins, lnd, x, y, c, chips)):
            pltpu.make_async_remote_copy(src_ref=src, dst_ref=dst, send_sem=send_sems.at[i], recv_sem=recv_sems.at[i],
                                         device_id=peer, device_id_type=MESH).start()
        token[...] = jnp.zeros_like(token)

    arrays = list(srcs) + list(lands)
    return pl.pallas_call(
        body, name=name,
        out_shape=(pltpu.SemaphoreType.DMA((ncopy,)), pltpu.SemaphoreType.DMA((ncopy,)),
                   *[pltpu.HBM(a.shape, a.dtype) for a in arrays], jax.ShapeDtypeStruct((8, LANES), F32)),
        in_specs=[_HBM] * (2 * n), out_specs=(_SEM, _SEM, *[_HBM] * (2 * n), pl.BlockSpec(memory_space=pltpu.VMEM)),
        input_output_aliases={i: 2 + i for i in range(2 * n)},
        compiler_params=pltpu.CompilerParams(has_side_effects=_EFFECT),
    )(*[pltpu.with_memory_space_constraint(a, pltpu.HBM) for a in arrays])


def _ici_wait(started, after, make_plan, name):
    send_sems, recv_sems = started[0], started[1]
    arrays = list(started[2:-1])
    n = len(arrays) // 2

    def body(*refs):
        ins, lnd = refs[:n], refs[n:2 * n]
        send_sems, recv_sems = refs[2 * n], refs[2 * n + 1]
        x, y, c, chips = _place()
        for i, (src, _, peer, landed) in enumerate(make_plan(ins, lnd, x, y, c, chips)):
            cp = pltpu.make_async_remote_copy(src_ref=src, dst_ref=landed, send_sem=send_sems.at[i],
                                              recv_sem=recv_sems.at[i], device_id=peer, device_id_type=MESH)
            cp.wait_send()
            cp.wait_recv()

    outs = pl.pallas_call(
        body, name=name, out_shape=tuple(pltpu.HBM(a.shape, a.dtype) for a in arrays),
        in_specs=[_HBM] * (2 * n) + [_SEM, _SEM, pl.BlockSpec(memory_space=pl.ANY)], out_specs=tuple([_HBM] * (2 * n)),
        input_output_aliases={i: i for i in range(2 * n)},
        compiler_params=pltpu.CompilerParams(has_side_effects=_EFFECT),
    )(*arrays, send_sems, recv_sems, after)
    return list(outs[:n]), list(outs[n:])


def _gather_finish(parts, lands, name):
    n = len(parts)

    def body(*refs):
        ins, lnd, outs, stage = refs[:n], refs[n:2 * n], refs[2 * n:3 * n], refs[3 * n:4 * n]
        send_sems, recv_sems, local_sems = refs[4 * n:]
        x, y, c, chips = _place()
        sibling = (x, y, 1 - c)
        own = _staged_copies(ins, [o.at[2 * x + y] for o in outs], stage, local_sems)
        own.load()
        sends = []
        for k in range(n):
            H = parts[k].shape[0] // 2
            for j, (px, py) in enumerate(chips):
                half = outs[k].at[2 * px + py, pl.ds(c * H, H), :]
                sends.append(pltpu.make_async_remote_copy(src_ref=half, dst_ref=half, send_sem=send_sems.at[3 * k + j],
                                                          recv_sem=recv_sems.at[3 * k + j], device_id=sibling, device_id_type=MESH))
                sends[-1].start()
        own.store()
        for k in range(n):
            H = parts[k].shape[0] // 2
            for j, (px, py) in enumerate(chips):
                other = outs[k].at[2 * px + py, pl.ds((1 - c) * H, H), :]
                pltpu.make_async_remote_copy(src_ref=other, dst_ref=other, send_sem=send_sems.at[3 * k + j],
                                             recv_sem=recv_sems.at[3 * k + j], device_id=sibling, device_id_type=MESH).wait_recv()
        for cp in sends:
            cp.wait_send()
        own.finish()

    return pl.pallas_call(
        body, name=name, out_shape=[jax.ShapeDtypeStruct(l.shape, l.dtype) for l in lands],
        in_specs=[_any_spec()] * (2 * n), out_specs=[_any_spec()] * n,
        input_output_aliases={n + k: k for k in range(n)},
        scratch_shapes=[pltpu.VMEM(p.shape, p.dtype) for p in parts] + [
            pltpu.SemaphoreType.DMA((3 * n,)), pltpu.SemaphoreType.DMA((3 * n,)), pltpu.SemaphoreType.DMA((2 * n,))],
        compiler_params=_params(),
    )(*parts, *lands)


SMALL_COLS = 1024


def _small_rows(shapes):
    starts, row = [], 0
    for r, cdim in shapes:
        starts.append(row)
        row += r * (-(-cdim // SMALL_COLS))
    return starts, -(-row // 8) * 8


def _allreduce_small(arrays, sink_rows, loss):
    n = len(arrays)
    shapes = [a.shape for a in arrays] + [(1, SWA_HEADS), (1, 1)]
    starts, total_rows = _small_rows(shapes)

    def pieces(k):
        r, cdim = shapes[k]
        per = -(-cdim // SMALL_COLS)
        return [(i, SMALL_COLS * j, min(SMALL_COLS, cdim - SMALL_COLS * j), starts[k] + per * i + j)
                for i in range(r) for j in range(per)]

    def body(*refs):
        ins, sink_ref, loss_ref = refs[:n], refs[n], refs[n + 1]
        outs = refs[n + 2:2 * n + 4]
        mine, all_ref, tot_ref, send_sems, recv_sems = refs[2 * n + 4:]
        x, y, c, _ = _place()
        me = 4 * x + 2 * y + c
        mine[...] = jnp.zeros_like(mine)
        for k in range(n):
            for i, col, wd, row in pieces(k):
                mine[row:row + 1, 0:wd] = ins[k][i:i + 1, col:col + wd]
        lane = _iota((1, SMALL_COLS), 1)
        sinks = jnp.zeros((1, SMALL_COLS), F32)
        for h in range(SWA_HEADS):
            head = jnp.sum(sink_ref[SWA_BLOCK * h:SWA_BLOCK * (h + 1), :]) * (1.0 / LANES)
            sinks = jnp.where(lane == h, head, sinks)
        mine[starts[n]:starts[n] + 1, :] = sinks
        mine[starts[n + 1]:starts[n + 1] + 1, 0:LANES] = loss_ref[0:1, :]
        all_ref[me] = mine[...]
        sends = []
        for k in range(1, 8):
            kx, ky, kc = (k >> 2) & 1, (k >> 1) & 1, k & 1
            peer = (x ^ kx, y ^ ky, c ^ kc)
            cp = pltpu.make_async_remote_copy(
                src_ref=mine, dst_ref=all_ref.at[me], send_sem=send_sems.at[k - 1], recv_sem=recv_sems.at[k - 1],
                device_id=peer, device_id_type=MESH)
            cp.start()
            sends.append(cp)
        for k in range(1, 8):
            kx, ky, kc = (k >> 2) & 1, (k >> 1) & 1, k & 1
            src = 4 * (x ^ kx) + 2 * (y ^ ky) + (c ^ kc)
            pltpu.make_async_remote_copy(
                src_ref=mine, dst_ref=all_ref.at[src], send_sem=send_sems.at[k - 1], recv_sem=recv_sems.at[k - 1],
                device_id=(x, y, c), device_id_type=MESH).wait_recv()
        for cp in sends:
            cp.wait_send()
        total = all_ref[0]
        for d in range(1, 8):
            total = total + all_ref[d]
        tot_ref[...] = total
        for k in range(n + 2):
            for i, col, wd, row in pieces(k):
                outs[k][i:i + 1, col:col + wd] = tot_ref[row:row + 1, 0:wd]

    vm = pl.BlockSpec(memory_space=pltpu.VMEM)
    buf = pltpu.VMEM((total_rows, SMALL_COLS), F32)
    return pl.pallas_call(
        body, name="allreduce_small", out_shape=[jax.ShapeDtypeStruct(s, F32) for s in shapes],
        in_specs=[vm] * (n + 2), out_specs=[vm] * (n + 2),
        scratch_shapes=[buf, pltpu.VMEM((8, total_rows, SMALL_COLS), F32), buf,
                        pltpu.SemaphoreType.DMA((7,)), pltpu.SemaphoreType.DMA((7,))],
    )(*arrays, sink_rows, loss)


BIG_NAMES = ("w_in", "w_out", "w_up", "w_down")
MATRIX_NAMES = BIG_NAMES + ("gla_gate_up", "conv_w")
LATE_NAMES = ("w_out", "w_up", "w_down")
GATE_SHARD = (16, GLA_QK // N_SHARD)
CONVW_SHARD = (3, SHARD_FF)
SMALL_W_ROWS = 96
PIECE_ROWS = (128, 128, 64, 88)
ADD_ROWS = (256, 128, 256, 176)
FIRST_UNIT_ROWS = (256, SMALL_W_ROWS // 2)


def _pad_rows(flat, rows):
    return jnp.pad(flat, (0, rows * LANES - flat.shape[0])).reshape(rows, LANES)


def _pack_small_weights(gate_up, conv_w):
    bits = lax.bitcast_convert_type(conv_w, BF16)
    return _pad_rows(jnp.concatenate([gate_up.astype(BF16).reshape(-1), bits.reshape(-1)]), SMALL_W_ROWS)


def _unpack_small_weights(packed):
    flat = packed.reshape(N_SHARD, -1)
    n_gate = GATE_SHARD[0] * GATE_SHARD[1]
    n_conv = 2 * CONVW_SHARD[0] * CONVW_SHARD[1]
    gate = flat[:, :n_gate].reshape((N_SHARD,) + GATE_SHARD)
    conv = lax.bitcast_convert_type(flat[:, n_gate:n_gate + n_conv].reshape((N_SHARD,) + CONVW_SHARD + (2,)), F32)
    return (jnp.transpose(gate, (1, 0, 2)).reshape(16, GLA_QK), jnp.transpose(conv, (1, 0, 2)).reshape(3, 2 * D_FF))


def _permute_w_in(w):
    pad = jnp.zeros((w.shape[0], IN_WIDTH_PAD - IN_WIDTH), w.dtype)
    return jnp.concatenate([w[:, 0:1024], w[:, 1040:2320], w[:, 1024:1040], pad], axis=1)


def _unpermute_w_in(wp):
    return jnp.concatenate([wp[:, 0:1024], wp[:, 2304:2320], wp[:, 1024:2304]], axis=1)


def _rope_tables(positions, zero):
    half = ROPE_DIM // 2
    inv_freq = ROPE_THETA ** (-jnp.arange(half, dtype=F32) * (2.0 / ROPE_DIM))
    d = jnp.arange(LANES) % SWA_HD
    freq = jnp.where(d < ROPE_DIM, inv_freq[d % half], 0.0)
    ang = positions.astype(F32)[:, None] * freq + zero
    cos, sin = jnp.cos(ang), jnp.sin(ang)
    return cos, jnp.where(d < half, -sin, 0.0), jnp.where((d >= half) & (d < ROPE_DIM), sin, 0.0)


SMALL_NAMES = (("pre_mix_norm", 1024), ("gla_gate_bias", 256), ("gla_out_norm", 128), ("swa_sinks", 8),
               ("post_mix_norm", 1024), ("pre_ffn_norm", 1024), ("conv_b", 5632), ("post_ffn_norm", 1024))


def _local_step(x, target, w, small, late_weights, early_grads, last_grads):
    rc, rsa, rsb = w["rope"]
    wp = w["wp"]
    gup = jnp.pad(w["gla_gate_up"], ((0, LANES - 16), (0, 0)))
    g1, g2, g3, g4 = (small[n] for n in ("pre_mix_norm", "post_mix_norm", "pre_ffn_norm", "post_ffn_norm"))
    gbias, gnorm, cb = small["gla_gate_bias"], small["gla_out_norm"], small["conv_b"]
    sinks = small["swa_sinks"].reshape(-1)
    cw = w["conv_w"]

    h1, q, k, v, la, gg, sq, kd, vd, glr = _proj_fwd(x, g1, wp, gup, gbias, rc, rsa, rsb)
    og, s_all = _gla_fwd(q, k, v, la)
    osw = _swa_fwd(sq, kd, vd, sinks)
    w_out, w_up4, w_down = late_weights(osw)
    x1, cat = _mix_out_fwd(x, og, gg, osw, gnorm, w_out, g2)
    h2, up, act, c1, c2, y, dx2, loss = _ffn_fwd(x1, g3, w_up4, cw, cb, w_down, g4, target)

    dy, dup, dx1, dg4, dg3, dcb, dcw = _ffn_bwd(dx2, y, g4, up, c1, c2, cw, w_down, w_up4, x1, g3)
    zero = early_grads(lambda: _matmul_tn(h2, dup, SHARD_FF, "grad_w_up", column_blocks_major=True),
                       lambda after=None: _matmul_tn(act, dy, D_MODEL, "grad_w_down", after=after))
    dmix, dog, dgg, dosw, dg2, dgn = _mix_out_bwd(dx1, cat, g2 + zero, w_out, og, gg, gnorm)
    dsq, dkd, dvd, dsink = _swa_bwd(sq, kd, vd, sinks, dosw)
    dq, dk, dv, dz = _gla_bwd(q, k, v, la, s_all, dog)
    dx, dproj, dg1, dgup, dgb = _proj_bwd(x, g1, wp, gup, glr, dq, dk, dv, dgg, dsq, dkd, dvd, dz, rc, rsa, rsb, dx1)

    grads = {
        "last": last_grads(lambda: _matmul_tn(h1, dproj, IN_WIDTH_PAD, "grad_w_in"),
                           lambda after=None: _matmul_tn(cat, dmix, D_MODEL, "grad_w_out", after=after)),
        "gla_gate_up": dgup[0:16],
        "conv_w": dcw,
    }
    small_grads = {
        "pre_mix_norm": dg1, "gla_gate_bias": dgb, "gla_out_norm": dgn,
        "post_mix_norm": dg2, "pre_ffn_norm": dg3, "conv_b": dcb, "post_ffn_norm": dg4,
    }
    return loss, dx, grads, small_grads, dsink


ADAM_ROWS = {"w_in": 256, "w_out": 256, "w_up": 256, "w_down": 176}
WEIGHT_ORDER = ("pre_mix_norm", "w_in", "gla_gate_up", "gla_gate_bias", "gla_out_norm", "swa_sinks", "w_out",
                "post_mix_norm", "pre_ffn_norm", "w_up", "conv_w", "conv_b", "w_down", "post_ffn_norm")
def kernel(x, positions, pre_mix_norm, w_in, gla_gate_up, gla_gate_bias, gla_out_norm, swa_sinks, w_out, post_mix_norm, pre_ffn_norm, w_up, conv_w, conv_b, w_down, post_ffn_norm, loss_target, m_pre_mix_norm, m_w_in, m_gla_gate_up, m_gla_gate_bias, m_gla_out_norm, m_swa_sinks, m_w_out, m_post_mix_norm, m_pre_ffn_norm, m_w_up, m_conv_w, m_conv_b, m_w_down, m_post_ffn_norm, v_pre_mix_norm, v_w_in, v_gla_gate_up, v_gla_gate_bias, v_gla_out_norm, v_swa_sinks, v_w_out, v_post_mix_norm, v_pre_ffn_norm, v_w_up, v_conv_w, v_conv_b, v_w_down, v_post_ffn_norm):
    weights = dict(pre_mix_norm=pre_mix_norm, w_in=w_in, gla_gate_up=gla_gate_up, gla_gate_bias=gla_gate_bias,
                   gla_out_norm=gla_out_norm, swa_sinks=swa_sinks, w_out=w_out, post_mix_norm=post_mix_norm,
                   pre_ffn_norm=pre_ffn_norm, w_up=w_up, conv_w=conv_w, conv_b=conv_b, w_down=w_down,
                   post_ffn_norm=post_ffn_norm)
    mom = dict(pre_mix_norm=m_pre_mix_norm, w_in=m_w_in, gla_gate_up=m_gla_gate_up, gla_gate_bias=m_gla_gate_bias,
               gla_out_norm=m_gla_out_norm, swa_sinks=m_swa_sinks, w_out=m_w_out, post_mix_norm=m_post_mix_norm,
               pre_ffn_norm=m_pre_ffn_norm, w_up=m_w_up, conv_w=m_conv_w, conv_b=m_conv_b, w_down=m_w_down,
               post_ffn_norm=m_post_ffn_norm)
    var = dict(pre_mix_norm=v_pre_mix_norm, w_in=v_w_in, gla_gate_up=v_gla_gate_up, gla_gate_bias=v_gla_gate_bias,
               gla_out_norm=v_gla_out_norm, swa_sinks=v_swa_sinks, w_out=v_w_out, post_mix_norm=v_post_mix_norm,
               pre_ffn_norm=v_pre_ffn_norm, w_up=v_w_up, conv_w=v_conv_w, conv_b=v_conv_b, w_down=v_w_down,
               post_ffn_norm=v_post_ffn_norm)
    weights, mom, var = ({n: a[0] if a.ndim == 3 else a for n, a in d.items()} for d in (weights, mom, var))

    core = lax.axis_index("c").astype(jnp.int32).reshape(1)
    chip = (2 * lax.axis_index("x") + lax.axis_index("y")).astype(jnp.int32).reshape(1)
    small = {n: weights[n] for n, _ in SMALL_NAMES}

    landing = lambda shards: [lax.empty((N_SHARD,) + s.shape, BF16) for s in shards]
    first_shards = [weights["w_in"].astype(BF16), _pack_small_weights(weights["gla_gate_up"], weights["conv_w"])]
    first_gather = _ici_start(first_shards, landing(first_shards), _gather_plan, "gather_start_first")
    rope = _rope_tables(positions[0], first_gather[-1][0, 0])
    win4, small4 = _gather_finish(*_ici_wait(first_gather, rope[0], _gather_plan, "gather_wait_first"), "gather_finish_first")
    gate_full, convw_full = _unpack_small_weights(small4)
    first = {"wp": _permute_w_in(jnp.transpose(win4, (1, 0, 2)).reshape(D_MODEL, IN_WIDTH)),
             "gla_gate_up": gate_full, "conv_w": convw_full, "rope": rope}
    after_first = (small4[0, 0, 0] * 0).astype(F32)
    late_shards = [(weights[n] + after_first).astype(BF16) for n in LATE_NAMES]
    gathering = _ici_start(late_shards, landing(late_shards), _gather_plan, "gather_start")
    small["pre_mix_norm"] = small["pre_mix_norm"] + gathering[-1][0, 0]

    def late_weights(after):
        wout4, wup4, wdown4 = _gather_finish(*_ici_wait(gathering, after, _gather_plan, "gather_wait"), "gather_finish")
        return wout4.reshape(D_MODEL, D_MODEL), wup4, wdown4.reshape(D_FF, D_MODEL)

    early = {}
    delta, new_m, new_v = {}, {}, {}

    def scatter_start(make_a, make_b, i, j, name):
        a = make_a()
        swapping = _ici_start([a], [lax.empty((N_SHARD, a.shape[1] // 2, a.shape[2]), F32)], _swap_plan,
                              "rs_swap_start_" + BIG_NAMES[i])
        b = make_b(swapping[-1])
        (a,), (got_a,) = _ici_wait(swapping, b, _swap_plan, "rs_swap_wait_" + BIG_NAMES[i])
        got_b = _rs_pair_swap([b], [PIECE_ROWS[j]], "rs_pair_swap_" + BIG_NAMES[j])[0]
        parts, lands = zip(*(_rs_add_pair(g, got, core, chip, ADD_ROWS[k], "rs_add_pair_" + BIG_NAMES[k])
                             for g, got, k in ((a, got_a, i), (b, got_b, j))))
        return _ici_start(list(parts), list(lands), _scatter_plan, name)

    def finish(landed, which, grads_out):
        names = [BIG_NAMES[i] for i in which]
        halves = [_rs_sum_chips(p, ADD_ROWS[i], "rs_sum_chips_" + BIG_NAMES[i]) for p, i in zip(landed, which)]
        reduced = _rs_pair_share(halves, [PIECE_ROWS[i] for i in which], "rs_pair_share_" + names[0])
        for n, g in zip(names, reduced):
            delta[n], new_m[n], new_v[n], grads_out[n] = _adamw(weights[n], g, mom[n], var[n], ADAM_ROWS[n], "adamw_" + n)

    def early_grads(make_up4, make_down):
        early["scatter"] = scatter_start(make_up4, lambda after: make_down(after).reshape(N_SHARD, D_FF // N_SHARD, D_MODEL), 2, 3,
                                         "rs_scatter_start")
        return early["scatter"][-1][0, 0]

    def last_grads(make_wp, make_out):
        return scatter_start(
            lambda: jnp.transpose(_unpermute_w_in(make_wp()).reshape(D_MODEL, N_SHARD, IN_WIDTH // N_SHARD), (1, 0, 2)),
            lambda after: make_out(after).reshape(N_SHARD, D_MODEL // N_SHARD, D_MODEL), 0, 1, "rs_scatter_start_rest")

    loss, dx, grads, small_grads, sink_rows = _local_step(
        x[0], loss_target[0], first, small, late_weights, early_grads, last_grads)

    late_scatter = grads["last"]
    g_all = {}
    finish(_ici_wait(early["scatter"], late_scatter[-1], _scatter_plan, "rs_scatter_wait")[1], (2, 3), g_all)
    finish(_ici_wait(late_scatter, delta["w_down"], _scatter_plan, "rs_scatter_wait_rest")[1], (0, 1), g_all)
    vectors = [n for n, _ in SMALL_NAMES if n != "swa_sinks"]
    summed = _allreduce_small([small_grads[n] for n in vectors] + [grads["gla_gate_up"], grads["conv_w"]], sink_rows, loss)
    g_all.update(zip(vectors, summed))
    g_all.update({"swa_sinks": summed[-2],
                  "gla_gate_up": lax.dynamic_slice_in_dim(summed[-4], chip[0] * GATE_SHARD[1], GATE_SHARD[1], axis=1),
                  "conv_w": lax.dynamic_slice_in_dim(summed[-3], chip[0] * SHARD_FF, SHARD_FF, axis=1)})
    loss_sum = summed[-1][0, 0]

    tiny = [n for n, _ in SMALL_NAMES] + ["gla_gate_up", "conv_w"]
    for res, vals in zip((delta, new_m, new_v, g_all), _adamw_small(*([d[n] for n in tiny] for d in (weights, g_all, mom, var)))):
        res.update(zip(tiny, vals))

    def lead(n, a):
        return a[None] if n in MATRIX_NAMES else a

    outs = [loss_sum, dx[None]]
    for d in (g_all, delta, new_m, new_v):
        outs.extend(lead(n, d[n]) for n in WEIGHT_ORDER)
    return tuple(outs)
```

```python
import functools

import jax
import jax.numpy as jnp
from jax import lax
from jax.experimental import pallas as pl
from jax.experimental.pallas import tpu as pltpu

F32 = jnp.float32
BF16 = jnp.bfloat16
MESH = pl.DeviceIdType.MESH

D_MODEL = 1024
GLA_HEADS = 4
GLA_DK = 64
GLA_DV = 128
GLA_TAU = 16.0
GLA_CHUNK = 64
SWA_HEADS = 8
SWA_HD = 64
SWA_BLOCK = 128
ROPE_THETA = 500000.0
ROPE_DIM = 16
D_FF = 2816
EPS = 1e-6
GLA_QK = 256
GLA_V = 512
SWA_Q = 512
SWA_KV = 128
IN_WIDTH = 2320
IN_WIDTH_PAD = 2432
N_SHARD = 4

ADAM_LR = 0.001
ADAM_B1 = 0.9
ADAM_B2 = 0.999
ADAM_EPS = 1e-08
ADAM_WD = 0.01
ADAM_STEP = 10

LANES = 128
VMEM_LIMIT = 56 * 1024 * 1024
TM = 256
SHARD_FF = 2 * D_FF // N_SHARD
FF_PIECES = ((0, 1408),)
GLA_BLOCK = 256


def _wide_tile(T):
    return 2 * TM if T % (2 * TM) == 0 else TM


def _params(**kw):
    return pltpu.CompilerParams(vmem_limit_bytes=VMEM_LIMIT, **kw)


def _mm(a, b):
    return lax.dot_general(a.astype(BF16), b.astype(BF16), (((1,), (0,)), ((), ())), preferred_element_type=F32)


def _mm_nt(a, b):
    return lax.dot_general(a.astype(BF16), b.astype(BF16), (((1,), (1,)), ((), ())), preferred_element_type=F32)


def _mm_tn(a, b):
    return lax.dot_general(a.astype(BF16), b.astype(BF16), (((0,), (0,)), ((), ())), preferred_element_type=F32)


def _mm_f32(a, b):
    return lax.dot_general(a, b, (((1,), (0,)), ((), ())), preferred_element_type=F32, precision=lax.Precision.HIGHEST)


def _iota(shape, dim):
    return lax.broadcasted_iota(jnp.int32, shape, dim)


def _sigmoid(x):
    return 1.0 / (1.0 + jnp.exp(-x))


def _gelu_parts(x):
    c = 0.7978845608028654
    x2 = x * x
    t = jnp.tanh(c * (x + 0.044715 * (x2 * x)))
    cdf = 0.5 * (1.0 + t)
    dcdf = 0.5 * (1.0 - t * t) * c * (1.0 + 3.0 * 0.044715 * x2)
    return x * cdf, cdf + x * dcdf


def _rms_bwd(v, r, g, dout):
    gd = g * dout
    return r * gd - v * (r * r * r) * jnp.mean(v * gd, axis=-1, keepdims=True)


def _row_spec(tm, cols):
    return pl.BlockSpec((tm, cols), lambda i: (i, 0))


def _const_spec(shape):
    return pl.BlockSpec(shape, lambda i: (0,) * len(shape))


def _any_spec():
    return pl.BlockSpec(memory_space=pl.ANY)


def _load_once(src_hbm, dst_vmem, sem):
    @pl.when(pl.program_id(0) == 0)
    def _():
        cp = pltpu.make_async_copy(src_hbm, dst_vmem, sem)
        cp.start()
        cp.wait()


def _rotate(v, rc, rsa, rsb):
    return v * rc + pltpu.roll(v, 120, 1) * rsa + pltpu.roll(v, 8, 1) * rsb


def _rotate_bwd(dv, rc, rsa, rsb):
    return dv * rc + pltpu.roll(dv * rsa, 8, 1) + pltpu.roll(dv * rsb, 120, 1)


def _proj_fwd(x, g1, wp, gup, gbias, rc, rsa, rsb):
    T = x.shape[0]
    TM = _wide_tile(T)

    def body(x_ref, g1_ref, wp_hbm, gup_ref, gb_ref, rc_ref, rsa_ref, rsb_ref,
             h1_ref, q_ref, k_ref, v_ref, la_ref, gg_ref, sq_ref, kd_ref, vd_ref, glr_ref, wp_v, sem):
        _load_once(wp_hbm, wp_v, sem)
        xt = x_ref[...]
        r = lax.rsqrt(jnp.mean(xt * xt, axis=-1, keepdims=True) + EPS)
        h = (xt * r * g1_ref[...]).astype(BF16)
        h1_ref[...] = h
        q_ref[...] = _mm(h, wp_v[:, 0:256])
        k_ref[...] = _mm(h, wp_v[:, 256:512])
        v_ref[...] = _mm(h, wp_v[:, 512:1024]).astype(BF16)
        gg_ref[...] = _mm(h, wp_v[:, 1024:1536]).astype(BF16)
        glr = _mm(h, wp_v[:, 2304:2432]).astype(BF16)
        glr_ref[...] = glr
        z = _mm(glr, gup_ref[...]) + gb_ref[...]
        la_ref[...] = (jnp.minimum(z, 0.0) - jnp.log1p(jnp.exp(-jnp.abs(z)))) * (1.0 / GLA_TAU)
        rc_, rsa_, rsb_ = rc_ref[...], rsa_ref[...], rsb_ref[...]
        for s in range(4):
            qs = _mm(h, wp_v[:, 1536 + 128 * s:1664 + 128 * s])
            sq_ref[:, 128 * s:128 * s + 128] = (_rotate(qs, rc_, rsa_, rsb_) * 0.125).astype(BF16)
        lane = _iota((TM, LANES), 1)
        first = lane < 64
        kr = _rotate(_mm(h, wp_v[:, 2048:2176]), rc_, rsa_, rsb_)
        krr = pltpu.roll(kr, 64, 1)
        kd_ref[:, 0:128] = jnp.where(first, kr, krr).astype(BF16)
        kd_ref[:, 128:256] = jnp.where(first, krr, kr).astype(BF16)
        vr = _mm(h, wp_v[:, 2176:2304])
        vrr = pltpu.roll(vr, 64, 1)
        vd_ref[:, 0:128] = jnp.where(first, vr, vrr).astype(BF16)
        vd_ref[:, 128:256] = jnp.where(first, vrr, vr).astype(BF16)

    outs = [
        jax.ShapeDtypeStruct((T, D_MODEL), BF16),
        jax.ShapeDtypeStruct((T, GLA_QK), F32),
        jax.ShapeDtypeStruct((T, GLA_QK), F32),
        jax.ShapeDtypeStruct((T, GLA_V), BF16),
        jax.ShapeDtypeStruct((T, GLA_QK), F32),
        jax.ShapeDtypeStruct((T, GLA_V), BF16),
        jax.ShapeDtypeStruct((T, SWA_Q), BF16),
        jax.ShapeDtypeStruct((T, 256), BF16),
        jax.ShapeDtypeStruct((T, 256), BF16),
        jax.ShapeDtypeStruct((T, LANES), BF16),
    ]
    return pl.pallas_call(
        body, name="proj_fwd", grid=(T // TM,), out_shape=outs,
        in_specs=[_row_spec(TM, D_MODEL), _const_spec((1, D_MODEL)), _any_spec(), _const_spec((LANES, GLA_QK)),
                  _const_spec((1, GLA_QK)), _row_spec(TM, LANES), _row_spec(TM, LANES), _row_spec(TM, LANES)],
        out_specs=[_row_spec(TM, o.shape[1]) for o in outs],
        scratch_shapes=[pltpu.VMEM((D_MODEL, IN_WIDTH_PAD), BF16), pltpu.SemaphoreType.DMA],
        compiler_params=_params(),
    )(x, g1, wp, gup, gbias, rc, rsa, rsb)


GLA_NB = GLA_BLOCK // GLA_CHUNK


def _gla_masks():
    n = GLA_BLOCK
    lane = _iota((n, LANES), 1)
    lane_masks = [(lane < 64).astype(F32), (lane >= 64).astype(F32)]
    row, col = _iota((n, n), 0), _iota((n, n), 1)
    same_chunk = (row >> 6) == (col >> 6)
    blk = ((_iota((256, LANES), 0) >> 7) == (_iota((256, LANES), 1) >> 6)).astype(F32)
    return lane_masks, same_chunk & (col <= row), same_chunk & (col >= row), blk


def _chunk_rows(vals):
    return jnp.concatenate([jnp.broadcast_to(v, (GLA_CHUNK, LANES)) for v in vals], axis=0)


def _gla_block_terms(q_ref, k_ref, b_ref, p):
    C = GLA_CHUNK
    cols = slice(LANES * p, LANES * p + LANES)
    bc = b_ref[:, cols]
    bl_rows = [b_ref[C * c + C - 1:C * c + C, cols] for c in range(GLA_NB)]
    bl = _chunk_rows(bl_rows)
    bm = _chunk_rows([b_ref[C * c + C // 2 - 1:C * c + C // 2, cols] for c in range(GLA_NB)])
    qs = q_ref[:, cols] * 0.125
    kk = k_ref[:, cols]
    eb = jnp.exp(bc)
    ekl = jnp.exp(bl - bc)
    eqm = jnp.exp(bc - bm)
    ekm = jnp.exp(bm - bc)
    return qs, kk, eb, ekl, eqm, ekm, [jnp.exp(r) for r in bl_rows]


def _block_cumsum(la, mask):
    return _mm_f32(mask.astype(F32), la)


def _gla_fwd(q, k, v, la):
    T = q.shape[0]
    NB = GLA_BLOCK // GLA_CHUNK
    C = GLA_CHUNK

    def body(q_ref, k_ref, v_ref, la_ref, o_ref, s_ref, st_ref, b_ref):
        @pl.when(pl.program_id(0) == 0)
        def _():
            st_ref[...] = jnp.zeros_like(st_ref)

        lane_masks, causal, _, blk = _gla_masks()
        b_ref[...] = _block_cumsum(la_ref[...], causal)
        for p in range(2):
            qs, kk, eb, ekl, eqm, ekm, gam = _gla_block_terms(q_ref, k_ref, b_ref, p)
            qh, kh, qm, km = qs * eb, kk * ekl, qs * eqm, kk * ekm
            vp = v_ref[:, 256 * p:256 * p + 256]
            intra = []
            for j in range(2):
                a = jnp.where(causal, _mm_nt(qm * lane_masks[j], km), 0.0)
                intra.append(_mm(a, vp[:, 128 * j:128 * j + 128]))
            kv = [blk * _mm_tn(vp[C * c:C * c + C], kh[C * c:C * c + C]) for c in range(NB)]
            st = st_ref[p]
            inter = []
            for c in range(NB):
                s_ref[c, p] = st[0:LANES] + st[LANES:2 * LANES]
                inter.append(_mm_nt(qh[C * c:C * c + C], st))
                st = st * gam[c] + kv[c]
            st_ref[p] = st
            o_ref[:, 256 * p:256 * p + 256] = (jnp.concatenate(inter, axis=0) + jnp.concatenate(intra, axis=1)).astype(BF16)

    return pl.pallas_call(
        body, name="gla_fwd", grid=(T // GLA_BLOCK,),
        out_shape=[jax.ShapeDtypeStruct((T, GLA_V), BF16), jax.ShapeDtypeStruct((T // C, 2, LANES, LANES), F32)],
        in_specs=[_row_spec(GLA_BLOCK, GLA_QK), _row_spec(GLA_BLOCK, GLA_QK), _row_spec(GLA_BLOCK, GLA_V),
                  _row_spec(GLA_BLOCK, GLA_QK)],
        out_specs=[_row_spec(GLA_BLOCK, GLA_V), pl.BlockSpec((NB, 2, LANES, LANES), lambda i: (i, 0, 0, 0))],
        scratch_shapes=[pltpu.VMEM((2, 256, LANES), F32), pltpu.VMEM((GLA_BLOCK, GLA_QK), F32)],
        compiler_params=_params(),
    )(q, k, v, la)


def _gla_bwd(q, k, v, la, s_all, do):
    T = q.shape[0]
    NB = GLA_BLOCK // GLA_CHUNK
    C = GLA_CHUNK
    nblk = T // GLA_BLOCK

    def body(q_ref, k_ref, v_ref, la_ref, s_ref, do_ref, dq_ref, dk_ref, dv_ref, dz_ref, dst_ref, b_ref):
        @pl.when(pl.program_id(0) == 0)
        def _():
            dst_ref[...] = jnp.zeros_like(dst_ref)

        lane_masks, causal, anti_causal, blk = _gla_masks()
        b_ref[...] = _block_cumsum(la_ref[...], causal)
        for p in range(2):
            cols = slice(LANES * p, LANES * p + LANES)
            qs, kk, eb, ekl, eqm, ekm, gam = _gla_block_terms(q_ref, k_ref, b_ref, p)
            qh, kh, qm, km = qs * eb, kk * ekl, qs * eqm, kk * ekm
            vp = v_ref[:, 256 * p:256 * p + 256]
            dop = do_ref[:, 256 * p:256 * p + 256]
            dqm = jnp.zeros((GLA_BLOCK, LANES), F32)
            dkm = jnp.zeros((GLA_BLOCK, LANES), F32)
            dv_intra = []
            for j in range(2):
                hs = slice(128 * j, 128 * j + 128)
                a = jnp.where(causal, _mm_nt(qm * lane_masks[j], km), 0.0)
                da = jnp.where(causal, _mm_nt(dop[:, hs], vp[:, hs]), 0.0)
                dv_intra.append(_mm_tn(a, dop[:, hs]))
                dqm = dqm + lane_masks[j] * _mm(da, km)
                dkm = dkm + lane_masks[j] * _mm_tn(da, qm)
            grow = [blk * _mm_tn(dop[C * c:C * c + C], qh[C * c:C * c + C]) for c in range(NB)]
            dst = dst_ref[p]
            dst_after = [None] * NB
            for c in reversed(range(NB)):
                dst_after[c] = dst
                dst = dst * gam[c] + grow[c]
            dst_ref[p] = dst
            dqh, dkh, dv_state, extra = [], [], [], []
            for c in range(NB):
                rows = slice(C * c, C * c + C)
                packed = s_ref[c, p]
                st = jnp.concatenate([packed * lane_masks[0][0:LANES], packed * lane_masks[1][0:LANES]], axis=0)
                dqh.append(_mm(dop[rows], st))
                dkh.append(_mm(vp[rows], dst_after[c]))
                dv_state.append(_mm_nt(kh[rows], dst_after[c]))
                extra.append(jnp.sum(dkh[c] * kh[rows], axis=0, keepdims=True)
                             + jnp.sum(st * dst_after[c], axis=0, keepdims=True) * gam[c])
            dqs = jnp.concatenate(dqh, axis=0) * eb + dqm * eqm
            dk = jnp.concatenate(dkh, axis=0) * ekl + dkm * ekm
            dg = _mm_f32(anti_causal.astype(F32), dqs * qs - dk * kk) + _chunk_rows(extra)
            dq_ref[:, cols] = (dqs * 0.125).astype(BF16)
            dk_ref[:, cols] = dk.astype(BF16)
            dz_ref[:, cols] = dg * (1.0 - jnp.exp(GLA_TAU * la_ref[:, cols])) * (1.0 / GLA_TAU)
            dv_ref[:, 256 * p:256 * p + 256] = (jnp.concatenate(dv_state, axis=0) + jnp.concatenate(dv_intra, axis=1)).astype(BF16)

    rev = lambda i: (nblk - 1 - i, 0)
    rspec = lambda cols: pl.BlockSpec((GLA_BLOCK, cols), rev)
    return pl.pallas_call(
        body, name="gla_bwd", grid=(nblk,),
        out_shape=[jax.ShapeDtypeStruct((T, GLA_QK), BF16), jax.ShapeDtypeStruct((T, GLA_QK), BF16),
                   jax.ShapeDtypeStruct((T, GLA_V), BF16), jax.ShapeDtypeStruct((T, GLA_QK), F32)],
        in_specs=[rspec(GLA_QK), rspec(GLA_QK), rspec(GLA_V), rspec(GLA_QK),
                  pl.BlockSpec((NB, 2, LANES, LANES), lambda i: (nblk - 1 - i, 0, 0, 0)), rspec(GLA_V)],
        out_specs=[rspec(GLA_QK), rspec(GLA_QK), rspec(GLA_V), rspec(GLA_QK)],
        scratch_shapes=[pltpu.VMEM((2, 256, LANES), F32), pltpu.VMEM((GLA_BLOCK, GLA_QK), F32)],
        compiler_params=_params(),
    )(q, k, v, la, s_all, do)


SWA_GROUP = 4


def _swa_stack(ref, g, first):
    parts = []
    for j in range(SWA_GROUP):
        m = 2 * g + j // 2
        pair = ref[:, 128 * m:128 * m + 128]
        zero = jnp.zeros_like(pair)
        parts.append(jnp.where(first, pair, zero) if j % 2 == 0 else jnp.where(first, zero, pair))
    return jnp.concatenate(parts, axis=0)


def _swa_unstack(rows, mm, first):
    W = SWA_BLOCK
    return jnp.where(first, rows[W * 2 * mm:W * (2 * mm + 1)], rows[W * (2 * mm + 1):W * (2 * mm + 2)])


def _swa_probs(qs, kp, kc, vp, vc, i, g, sink_ref, first4):
    W = SWA_BLOCK
    R = SWA_GROUP * W
    r, c = _iota((R, W), 0) & (W - 1), _iota((R, W), 1)
    neg = -1e30
    s_p = jnp.where((c > r) & (i > 0), _mm_nt(qs, kp), neg)
    s_c = jnp.where(c <= r, _mm_nt(qs, kc), neg)
    head = _iota((R, 1), 0) >> 7
    sink = jnp.where(head == 0, sink_ref[4 * g], jnp.where(head == 1, sink_ref[4 * g + 1],
                                                           jnp.where(head == 2, sink_ref[4 * g + 2], sink_ref[4 * g + 3])))
    m = jnp.maximum(jnp.max(jnp.maximum(s_p, s_c), axis=-1, keepdims=True), sink)
    p_p = jnp.exp(s_p - m)
    p_c = jnp.exp(s_c - m)
    p_s = jnp.exp(sink - m)
    one = jnp.ones((W, LANES), BF16)
    first = _iota((W, LANES), 1) < 64
    acc = _mm(p_p, jnp.where(first, vp, one)) + _mm(p_c, jnp.where(first, vc, one))
    rolled = pltpu.roll(acc, 64, 1)
    denom = jnp.where(first4, rolled, acc) + p_s
    return p_p, p_c, p_s, denom, acc, rolled


def _swa_fwd(sq, kd, vd, sinks):
    T = sq.shape[0]
    W = SWA_BLOCK
    prev = lambda i: (jnp.maximum(i - 1, 0), 0)

    def body(sink_ref, q_ref, kp_ref, kc_ref, vp_ref, vc_ref, o_ref):
        i = pl.program_id(0)
        first4 = _iota((SWA_GROUP * W, LANES), 1) < 64
        first = _iota((W, LANES), 1) < 64
        for g in range(2):
            gs = slice(128 * g, 128 * g + 128)
            qs = _swa_stack(q_ref, g, first)
            _, _, _, denom, acc, rolled = _swa_probs(qs, kp_ref[:, gs], kc_ref[:, gs], vp_ref[:, gs], vc_ref[:, gs],
                                                     i, g, sink_ref, first4)
            pv = jnp.where(first4, acc, rolled)
            o = pv / denom
            for mm in range(2):
                m = 2 * g + mm
                o_ref[:, 128 * m:128 * m + 128] = _swa_unstack(o, mm, first).astype(BF16)

    return pl.pallas_call(
        body, name="swa_fwd", grid=(T // W,), out_shape=jax.ShapeDtypeStruct((T, SWA_Q), BF16),
        in_specs=[pl.BlockSpec(memory_space=pltpu.SMEM), _row_spec(W, SWA_Q), pl.BlockSpec((W, 256), prev),
                  _row_spec(W, 256), pl.BlockSpec((W, 256), prev), _row_spec(W, 256)],
        out_specs=_row_spec(W, SWA_Q),
        compiler_params=_params(),
    )(sinks, sq, kd, kd, vd, vd)


def _swa_bwd(sq, kd, vd, sinks, do):
    T = sq.shape[0]
    W = SWA_BLOCK
    n = T // W
    cur = lambda i: (jnp.minimum(i, n - 1), 0)
    prev = lambda i: (jnp.clip(i - 1, 0, n - 1), 0)

    def body(sink_ref, q_ref, kp_ref, kc_ref, vp_ref, vc_ref, do_ref, dq_ref, dk_ref, dv_ref, ds_ref, ck_ref, cv_ref):
        i = pl.program_id(0)

        @pl.when(i == 0)
        def _():
            ds_ref[...] = jnp.zeros_like(ds_ref)
            ck_ref[...] = jnp.zeros_like(ck_ref)
            cv_ref[...] = jnp.zeros_like(cv_ref)

        @pl.when(i < n)
        def _():
            first4 = _iota((SWA_GROUP * W, LANES), 1) < 64
            first = _iota((W, LANES), 1) < 64
            for g in range(2):
                gs = slice(128 * g, 128 * g + 128)
                kp, kc, vp, vc = kp_ref[:, gs], kc_ref[:, gs], vp_ref[:, gs], vc_ref[:, gs]
                qs = _swa_stack(q_ref, g, first)
                dos = _swa_stack(do_ref, g, first)
                p_p, p_c, p_s, denom, _, _ = _swa_probs(qs, kp, kc, vp, vc, i, g, sink_ref, first4)
                inv = 1.0 / denom
                p_p, p_c = p_p * inv, p_c * inv
                dp_p = _mm_nt(dos, vp)
                dp_c = _mm_nt(dos, vc)
                delta = jnp.sum(p_p * dp_p + p_c * dp_c, axis=-1, keepdims=True)
                ds_p = p_p * (dp_p - delta)
                ds_c = p_c * (dp_c - delta)
                rows = slice(SWA_GROUP * W * g, SWA_GROUP * W * (g + 1))
                ds_ref[rows, :] = ds_ref[rows, :] - (p_s * delta) * inv
                dq = (_mm(ds_p, kp) + _mm(ds_c, kc)) * 0.125
                for mm in range(2):
                    m = 2 * g + mm
                    dq_ref[:, 128 * m:128 * m + 128] = _swa_unstack(dq, mm, first).astype(BF16)
                dk_ref[:, gs] = (ck_ref[:, gs] + _mm_tn(ds_p, qs)).astype(BF16)
                dv_ref[:, gs] = (cv_ref[:, gs] + _mm_tn(p_p, dos)).astype(BF16)
                ck_ref[:, gs] = _mm_tn(ds_c, qs)
                cv_ref[:, gs] = _mm_tn(p_c, dos)

        @pl.when(i == n)
        def _():
            dk_ref[...] = ck_ref[...].astype(BF16)
            dv_ref[...] = cv_ref[...].astype(BF16)

    return pl.pallas_call(
        body, name="swa_bwd", grid=(n + 1,),
        out_shape=[jax.ShapeDtypeStruct((T, SWA_Q), BF16), jax.ShapeDtypeStruct((T, 256), BF16),
                   jax.ShapeDtypeStruct((T, 256), BF16), jax.ShapeDtypeStruct((SWA_HEADS * W, LANES), F32)],
        in_specs=[pl.BlockSpec(memory_space=pltpu.SMEM), pl.BlockSpec((W, SWA_Q), cur), pl.BlockSpec((W, 256), prev),
                  pl.BlockSpec((W, 256), cur), pl.BlockSpec((W, 256), prev), pl.BlockSpec((W, 256), cur),
                  pl.BlockSpec((W, SWA_Q), cur)],
        out_specs=[pl.BlockSpec((W, SWA_Q), cur), pl.BlockSpec((W, 256), prev), pl.BlockSpec((W, 256), prev),
                   _const_spec((SWA_HEADS * W, LANES))],
        scratch_shapes=[pltpu.VMEM((W, 256), F32), pltpu.VMEM((W, 256), F32)],
        compiler_params=_params(),
    )(sinks, sq, kd, kd, vd, vd, do)


def _mix_out_fwd(x, og, gg, osw, gnorm, wout, g2):
    T = x.shape[0]
    TM = _wide_tile(T)

    def body(x_ref, og_ref, gg_ref, osw_ref, gn_ref, wout_ref, g2_ref, x1_ref, cat_ref):
        gn = gn_ref[...]
        for j in range(GLA_HEADS):
            hs = slice(128 * j, 128 * j + 128)
            o = og_ref[:, hs].astype(F32)
            r = lax.rsqrt(jnp.mean(o * o, axis=-1, keepdims=True) + EPS)
            gate = gg_ref[:, hs].astype(F32)
            cat_ref[:, hs] = (o * r * gn * (gate * _sigmoid(gate))).astype(BF16)
        cat_ref[:, GLA_V:] = osw_ref[...]
        mix = _mm(cat_ref[...], wout_ref[...])
        r2 = lax.rsqrt(jnp.mean(mix * mix, axis=-1, keepdims=True) + EPS)
        x1_ref[...] = x_ref[...] + mix * r2 * g2_ref[...]

    return pl.pallas_call(
        body, name="mix_out_fwd", grid=(T // TM,),
        out_shape=[jax.ShapeDtypeStruct((T, D_MODEL), F32), jax.ShapeDtypeStruct((T, D_MODEL), BF16)],
        in_specs=[_row_spec(TM, D_MODEL), _row_spec(TM, GLA_V), _row_spec(TM, GLA_V), _row_spec(TM, SWA_Q),
                  _const_spec((1, LANES)), _const_spec((D_MODEL, D_MODEL)), _const_spec((1, D_MODEL))],
        out_specs=[_row_spec(TM, D_MODEL), _row_spec(TM, D_MODEL)],
        compiler_params=_params(),
    )(x, og, gg, osw, gnorm, wout, g2)


HALO = 8


def _rows_before(v, prev1, prev2):
    row = _iota(v.shape, 0)
    m1 = jnp.where(row == 0, prev1, pltpu.roll(v, 1, 0))
    m2 = jnp.where(row == 0, prev2, jnp.where(row == 1, prev1, pltpu.roll(v, 2, 0)))
    return m1, m2


def _rows_after(v, next1, next2):
    n = v.shape[0]
    row = _iota(v.shape, 0)
    p1 = jnp.where(row == n - 1, next1, pltpu.roll(v, n - 1, 0))
    p2 = jnp.where(row == n - 1, next2, jnp.where(row == n - 2, next1, pltpu.roll(v, n - 2, 0)))
    return p1, p2


def _ff_pieces():
    return [(j, off, wd) for j in range(2) for off, wd in FF_PIECES]


def _ffn_fwd(x1, g3, wup, cw, cb, wdown, g4, target):
    T = x1.shape[0]

    def body(x1_ref, g3_ref, wup_hbm, cw_ref, cb_ref, wdn_hbm, g4_ref, tg_ref,
             h2_ref, up_ref, a_ref, c1_ref, c2_ref, y_ref, dx2_ref, loss_ref, wup_v, wdn_v, carry_ref, sems):
        _load_once(wup_hbm, wup_v, sems.at[0])
        _load_once(wdn_hbm, wdn_v, sems.at[1])

        @pl.when(pl.program_id(0) == 0)
        def _():
            carry_ref[...] = jnp.zeros_like(carry_ref)
            loss_ref[...] = jnp.zeros_like(loss_ref)

        x1 = x1_ref[...]
        r3 = lax.rsqrt(jnp.mean(x1 * x1, axis=-1, keepdims=True) + EPS)
        h2 = (x1 * r3 * g3_ref[...]).astype(BF16)
        h2_ref[...] = h2
        for j, off, wd in _ff_pieces():
            base = SHARD_FF * j + off
            u = []
            for half in range(2):
                cs = slice(D_FF * half + base, D_FF * half + base + wd)
                upb = _mm(h2, wup_v[2 * half + j, :, off:off + wd]).astype(BF16)
                up_ref[:, cs] = upb
                upf = upb.astype(F32)
                m1, m2 = _rows_before(upf, carry_ref[HALO - 1:HALO, cs], carry_ref[HALO - 2:HALO - 1, cs])
                u.append(cb_ref[:, cs] + cw_ref[0:1, cs] * m2 + cw_ref[1:2, cs] * m1 + cw_ref[2:3, cs] * upf)
                carry_ref[:, cs] = upf[TM - HALO:TM, :]
            act, dact = _gelu_parts(u[1])
            a = (act * u[0]).astype(BF16)
            out = slice(base, base + wd)
            a_ref[:, out] = a
            c1_ref[:, out] = act.astype(BF16)
            c2_ref[:, out] = (u[0] * dact).astype(BF16)
        y = _mm(a_ref[...], wdn_v[...])
        y_ref[...] = y
        r4 = lax.rsqrt(jnp.mean(y * y, axis=-1, keepdims=True) + EPS)
        err = x1 + y * r4 * g4_ref[...] - tg_ref[...]
        dx2_ref[...] = err * (1.0 / D_MODEL)
        loss_ref[...] = loss_ref[...] + jnp.sum(err * err) * (0.5 / D_MODEL)

    outs = [
        jax.ShapeDtypeStruct((T, D_MODEL), BF16),
        jax.ShapeDtypeStruct((T, 2 * D_FF), BF16),
        jax.ShapeDtypeStruct((T, D_FF), BF16),
        jax.ShapeDtypeStruct((T, D_FF), BF16),
        jax.ShapeDtypeStruct((T, D_FF), BF16),
        jax.ShapeDtypeStruct((T, D_MODEL), F32),
        jax.ShapeDtypeStruct((T, D_MODEL), F32),
        jax.ShapeDtypeStruct((8, LANES), F32),
    ]
    return pl.pallas_call(
        body, name="ffn_fwd", grid=(T // TM,), out_shape=outs,
        in_specs=[_row_spec(TM, D_MODEL), _const_spec((1, D_MODEL)), _any_spec(), _const_spec((3, 2 * D_FF)),
                  _const_spec((1, 2 * D_FF)), _any_spec(), _const_spec((1, D_MODEL)), _row_spec(TM, D_MODEL)],
        out_specs=[_row_spec(TM, D_MODEL), _row_spec(TM, 2 * D_FF), _row_spec(TM, D_FF), _row_spec(TM, D_FF),
                   _row_spec(TM, D_FF), _row_spec(TM, D_MODEL), _row_spec(TM, D_MODEL), _const_spec((8, LANES))],
        scratch_shapes=[pltpu.VMEM((N_SHARD, D_MODEL, SHARD_FF), BF16), pltpu.VMEM((D_FF, D_MODEL), BF16),
                        pltpu.VMEM((HALO, 2 * D_FF), F32), pltpu.SemaphoreType.DMA((2,))],
        compiler_params=_params(),
    )(x1, g3, wup, cw, cb, wdown, g4, target)


def _ffn_bwd(dx2, y, g4, up, c1, c2, cw, wdown, wup, x1, g3):
    T = dx2.shape[0]
    nt = T // TM
    rev = lambda i: (nt - 1 - i, 0)

    def body(dn_ref, y_ref, g4_ref, up_ref, c1_ref, c2_ref, cw_ref, wdn_hbm, wup_hbm, x1_ref, g3_ref,
             dy_ref, dup_ref, dx1_ref, dg4_ref, dg3_ref, dcb_ref, dcw_ref, wup_v, wdn_v, carry_ref, sems):
        _load_once(wup_hbm, wup_v, sems.at[0])
        _load_once(wdn_hbm, wdn_v, sems.at[1])

        @pl.when(pl.program_id(0) == 0)
        def _():
            carry_ref[...] = jnp.zeros_like(carry_ref)
            dg4_ref[...] = jnp.zeros_like(dg4_ref)
            dg3_ref[...] = jnp.zeros_like(dg3_ref)
            dcb_ref[...] = jnp.zeros_like(dcb_ref)
            dcw_ref[...] = jnp.zeros_like(dcw_ref)

        dn = dn_ref[...]
        y = y_ref[...]
        g4v = g4_ref[...]
        r4 = lax.rsqrt(jnp.mean(y * y, axis=-1, keepdims=True) + EPS)
        dg4_ref[...] = dg4_ref[...] + jnp.sum(dn * y * r4, axis=0, keepdims=True)
        dy = _rms_bwd(y, r4, g4v, dn).astype(BF16)
        dy_ref[...] = dy
        dh2 = jnp.zeros((TM, D_MODEL), F32)
        for j, off, wd in _ff_pieces():
            base = SHARD_FF * j + off
            da = _mm_nt(dy, wdn_v[base:base + wd, :])
            for half, coef_ref in enumerate((c1_ref, c2_ref)):
                cs = slice(D_FF * half + base, D_FF * half + base + wd)
                du = da * coef_ref[:, base:base + wd].astype(F32)
                p1, p2 = _rows_after(du, carry_ref[0:1, cs], carry_ref[1:2, cs])
                carry_ref[:, cs] = du[0:HALO, :]
                upf = up_ref[:, cs].astype(F32)
                dcb_ref[:, cs] = dcb_ref[:, cs] + jnp.sum(du, axis=0, keepdims=True)
                dcw_ref[0:1, cs] = dcw_ref[0:1, cs] + jnp.sum(p2 * upf, axis=0, keepdims=True)
                dcw_ref[1:2, cs] = dcw_ref[1:2, cs] + jnp.sum(p1 * upf, axis=0, keepdims=True)
                dcw_ref[2:3, cs] = dcw_ref[2:3, cs] + jnp.sum(du * upf, axis=0, keepdims=True)
                dup = (cw_ref[2:3, cs] * du + cw_ref[1:2, cs] * p1 + cw_ref[0:1, cs] * p2).astype(BF16)
                dup_ref[:, cs] = dup
                dh2 = dh2 + _mm_nt(dup, wup_v[2 * half + j, :, off:off + wd])
        x1 = x1_ref[...]
        r3 = lax.rsqrt(jnp.mean(x1 * x1, axis=-1, keepdims=True) + EPS)
        dg3_ref[...] = dg3_ref[...] + jnp.sum(dh2 * x1 * r3, axis=0, keepdims=True)
        dx1_ref[...] = dn + _rms_bwd(x1, r3, g3_ref[...], dh2)

    outs = [
        jax.ShapeDtypeStruct((T, D_MODEL), BF16),
        jax.ShapeDtypeStruct((T, 2 * D_FF), BF16),
        jax.ShapeDtypeStruct((T, D_MODEL), F32),
        jax.ShapeDtypeStruct((1, D_MODEL), F32),
        jax.ShapeDtypeStruct((1, D_MODEL), F32),
        jax.ShapeDtypeStruct((1, 2 * D_FF), F32),
        jax.ShapeDtypeStruct((3, 2 * D_FF), F32),
    ]
    return pl.pallas_call(
        body, name="ffn_bwd", grid=(nt,), out_shape=outs,
        in_specs=[pl.BlockSpec((TM, D_MODEL), rev), pl.BlockSpec((TM, D_MODEL), rev), _const_spec((1, D_MODEL)),
                  pl.BlockSpec((TM, 2 * D_FF), rev), pl.BlockSpec((TM, D_FF), rev), pl.BlockSpec((TM, D_FF), rev),
                  _const_spec((3, 2 * D_FF)), _any_spec(), _any_spec(), pl.BlockSpec((TM, D_MODEL), rev),
                  _const_spec((1, D_MODEL))],
        out_specs=[pl.BlockSpec((TM, D_MODEL), rev), pl.BlockSpec((TM, 2 * D_FF), rev), pl.BlockSpec((TM, D_MODEL), rev),
                   _const_spec((1, D_MODEL)), _const_spec((1, D_MODEL)), _const_spec((1, 2 * D_FF)),
                   _const_spec((3, 2 * D_FF))],
        scratch_shapes=[pltpu.VMEM((N_SHARD, D_MODEL, SHARD_FF), BF16), pltpu.VMEM((D_FF, D_MODEL), BF16),
                        pltpu.VMEM((HALO, 2 * D_FF), F32), pltpu.SemaphoreType.DMA((2,))],
        compiler_params=_params(),
    )(dx2, y, g4, up, c1, c2, cw, wdown, wup, x1, g3)


def _mix_out_bwd(dx1, cat, g2, wout, og, gg, gnorm):
    T = dx1.shape[0]
    TM = _wide_tile(T)

    def body(dx1_ref, cat_ref, g2_ref, wout_ref, og_ref, gg_ref, gn_ref,
             dmix_ref, dog_ref, dgg_ref, dosw_ref, dg2_ref, dgn_ref):
        @pl.when(pl.program_id(0) == 0)
        def _():
            dg2_ref[...] = jnp.zeros_like(dg2_ref)
            dgn_ref[...] = jnp.zeros_like(dgn_ref)

        dx1 = dx1_ref[...]
        mix = _mm(cat_ref[...], wout_ref[...])
        r2 = lax.rsqrt(jnp.mean(mix * mix, axis=-1, keepdims=True) + EPS)
        dg2_ref[...] = dg2_ref[...] + jnp.sum(dx1 * mix * r2, axis=0, keepdims=True)
        dmix = _rms_bwd(mix, r2, g2_ref[...], dx1).astype(BF16)
        dmix_ref[...] = dmix
        dcat = _mm_nt(dmix, wout_ref[...])
        dosw_ref[...] = dcat[:, GLA_V:].astype(BF16)
        gn = gn_ref[...]
        dgn = jnp.zeros((1, LANES), F32)
        for j in range(GLA_HEADS):
            hs = slice(128 * j, 128 * j + 128)
            o = og_ref[:, hs].astype(F32)
            r = lax.rsqrt(jnp.mean(o * o, axis=-1, keepdims=True) + EPS)
            gate = gg_ref[:, hs].astype(F32)
            sg = _sigmoid(gate)
            dgated = dcat[:, hs]
            dnorm = dgated * (gate * sg)
            dgg_ref[:, hs] = (dgated * (o * r * gn) * (sg * (1.0 + gate * (1.0 - sg)))).astype(BF16)
            dgn = dgn + jnp.sum(dnorm * o * r, axis=0, keepdims=True)
            dog_ref[:, hs] = _rms_bwd(o, r, gn, dnorm)
        dgn_ref[...] = dgn_ref[...] + dgn

    return pl.pallas_call(
        body, name="mix_out_bwd", grid=(T // TM,),
        out_shape=[jax.ShapeDtypeStruct((T, D_MODEL), BF16), jax.ShapeDtypeStruct((T, GLA_V), F32),
                   jax.ShapeDtypeStruct((T, GLA_V), BF16), jax.ShapeDtypeStruct((T, SWA_Q), BF16),
                   jax.ShapeDtypeStruct((1, D_MODEL), F32), jax.ShapeDtypeStruct((1, LANES), F32)],
        in_specs=[_row_spec(TM, D_MODEL), _row_spec(TM, D_MODEL), _const_spec((1, D_MODEL)),
                  _const_spec((D_MODEL, D_MODEL)), _row_spec(TM, GLA_V), _row_spec(TM, GLA_V), _const_spec((1, LANES))],
        out_specs=[_row_spec(TM, D_MODEL), _row_spec(TM, GLA_V), _row_spec(TM, GLA_V), _row_spec(TM, SWA_Q),
                   _const_spec((1, D_MODEL)), _const_spec((1, LANES))],
        compiler_params=_params(),
    )(dx1, cat, g2, wout, og, gg, gnorm)


def _proj_bwd(x, g1, wp, gup, glr, dq, dk, dv, dgg, dsq, dkd, dvd, dz, rc, rsa, rsb, dx1):
    T = x.shape[0]
    TM = _wide_tile(T)

    def body(x_ref, g1_ref, wp_hbm, gup_ref, glr_ref, dq_ref, dk_ref, dv_ref, dgg_ref, dsq_ref, dkd_ref, dvd_ref,
             dz_ref, rc_ref, rsa_ref, rsb_ref, dx1_ref, dx_ref, dp_ref, dg1_ref, dgup_ref, dgb_ref, wp_v, sem):
        _load_once(wp_hbm, wp_v, sem)

        @pl.when(pl.program_id(0) == 0)
        def _():
            dg1_ref[...] = jnp.zeros_like(dg1_ref)
            dgup_ref[...] = jnp.zeros_like(dgup_ref)
            dgb_ref[...] = jnp.zeros_like(dgb_ref)

        rc_, rsa_, rsb_ = rc_ref[...], rsa_ref[...], rsb_ref[...]
        dp_ref[:, 0:256] = dq_ref[...]
        dp_ref[:, 256:512] = dk_ref[...]
        dp_ref[:, 512:1024] = dv_ref[...]
        dp_ref[:, 1024:1536] = dgg_ref[...]
        for s in range(4):
            cs = slice(128 * s, 128 * s + 128)
            dp_ref[:, 1536 + 128 * s:1664 + 128 * s] = _rotate_bwd(dsq_ref[:, cs].astype(F32), rc_, rsa_, rsb_).astype(BF16)
        first = _iota((TM, LANES), 1) < 64
        dk0 = dkd_ref[:, 0:128].astype(F32)
        dk1 = dkd_ref[:, 128:256].astype(F32)
        dkr = jnp.where(first, dk0 + pltpu.roll(dk0, 64, 1), dk1 + pltpu.roll(dk1, 64, 1))
        dp_ref[:, 2048:2176] = _rotate_bwd(dkr, rc_, rsa_, rsb_).astype(BF16)
        dv0 = dvd_ref[:, 0:128].astype(F32)
        dv1 = dvd_ref[:, 128:256].astype(F32)
        dp_ref[:, 2176:2304] = jnp.where(first, dv0 + pltpu.roll(dv0, 64, 1), dv1 + pltpu.roll(dv1, 64, 1)).astype(BF16)
        dz = dz_ref[...]
        dzb = dz.astype(BF16)
        dp_ref[:, 2304:2432] = _mm_nt(dzb, gup_ref[...]).astype(BF16)
        dgup_ref[...] = dgup_ref[...] + _mm_tn(glr_ref[...], dzb)
        dgb_ref[...] = dgb_ref[...] + jnp.sum(dz, axis=0, keepdims=True)
        dh1 = _mm_nt(dp_ref[...], wp_v[...])
        xt = x_ref[...]
        r = lax.rsqrt(jnp.mean(xt * xt, axis=-1, keepdims=True) + EPS)
        dg1_ref[...] = dg1_ref[...] + jnp.sum(dh1 * xt * r, axis=0, keepdims=True)
        dx_ref[...] = dx1_ref[...] + _rms_bwd(xt, r, g1_ref[...], dh1)

    row = lambda cols: _row_spec(TM, cols)
    return pl.pallas_call(
        body, name="proj_bwd", grid=(T // TM,),
        out_shape=[jax.ShapeDtypeStruct((T, D_MODEL), F32), jax.ShapeDtypeStruct((T, IN_WIDTH_PAD), BF16),
                   jax.ShapeDtypeStruct((1, D_MODEL), F32), jax.ShapeDtypeStruct((LANES, GLA_QK), F32),
                   jax.ShapeDtypeStruct((1, GLA_QK), F32)],
        in_specs=[row(D_MODEL), _const_spec((1, D_MODEL)), _any_spec(), _const_spec((LANES, GLA_QK)), row(LANES),
                  row(GLA_QK), row(GLA_QK), row(GLA_V), row(GLA_V), row(SWA_Q), row(256), row(256), row(GLA_QK),
                  row(LANES), row(LANES), row(LANES), row(D_MODEL)],
        out_specs=[row(D_MODEL), row(IN_WIDTH_PAD), _const_spec((1, D_MODEL)), _const_spec((LANES, GLA_QK)),
                   _const_spec((1, GLA_QK))],
        scratch_shapes=[pltpu.VMEM((D_MODEL, IN_WIDTH_PAD), BF16), pltpu.SemaphoreType.DMA],
        compiler_params=_params(),
    )(x, g1, wp, gup, glr, dq, dk, dv, dgg, dsq, dkd, dvd, dz, rc, rsa, rsb, dx1)


def _matmul_tn(a, b, tn, name, column_blocks_major=False, after=None):
    T, M = a.shape
    N = b.shape[1]
    tk = next(t for t in (2048, 1024, 512, TM) if T % t == 0 and t * (M + tn) <= 2048 * (D_MODEL + SHARD_FF))
    nk = T // tk
    if column_blocks_major:
        out_shape = jax.ShapeDtypeStruct((N // tn, M, tn), F32)
        out_spec = pl.BlockSpec((None, M, tn), lambda j, kk: (j, 0, 0))
    else:
        out_shape = jax.ShapeDtypeStruct((M, N), F32)
        out_spec = pl.BlockSpec((M, tn), lambda j, kk: (0, j))

    def body(a_ref, b_ref, *rest):
        o_ref = rest[-1]
        kk = pl.program_id(1)

        @pl.when(kk == 0)
        def _():
            o_ref[...] = jnp.zeros_like(o_ref)

        o_ref[...] = o_ref[...] + _mm_tn(a_ref[...], b_ref[...])

    ordering = [] if after is None else [after]
    return pl.pallas_call(
        body, name=name, grid=(N // tn, nk), out_shape=out_shape,
        in_specs=[pl.BlockSpec((tk, M), lambda j, kk: (kk, 0)), pl.BlockSpec((tk, tn), lambda j, kk: (kk, j))]
        + [_any_spec()] * len(ordering),
        out_specs=out_spec,
        compiler_params=_params(),
    )(a, b, *ordering)


def _adamw_update(w_ref, g_ref, m_ref, v_ref, d_ref, m2_ref, v2_ref):
    g_ = g_ref[...]
    m2 = ADAM_B1 * m_ref[...] + (1.0 - ADAM_B1) * g_
    v2 = ADAM_B2 * v_ref[...] + (1.0 - ADAM_B2) * (g_ * g_)
    m_hat = m2 / (1.0 - ADAM_B1 ** ADAM_STEP)
    v_hat = v2 / (1.0 - ADAM_B2 ** ADAM_STEP)
    d_ref[...] = -ADAM_LR * (m_hat / (jnp.sqrt(v_hat) + ADAM_EPS) + ADAM_WD * w_ref[...])
    m2_ref[...] = m2
    v2_ref[...] = v2


def _adamw(w, g, m, v, rows, name):
    R, C = w.shape

    def body(*refs):
        _adamw_update(*refs[:7])
        refs[7][...] = refs[1][...]

    spec = pl.BlockSpec((rows, C), lambda i: (i, 0))
    return pl.pallas_call(
        body, name=name, grid=(R // rows,), out_shape=[jax.ShapeDtypeStruct((R, C), F32)] * 4,
        in_specs=[spec] * 4, out_specs=[spec] * 4, compiler_params=_params(),
    )(w, g, m, v)


def _adamw_group(ws, gs, ms, vs, rows, name):
    n = len(ws)
    steps = ws[0].shape[0] // rows[0]
    assert all(w.shape[0] == steps * r for w, r in zip(ws, rows))

    def body(*refs):
        w_, g_, m_, v_, d_, m2_, v2_, g2_ = (refs[n * i:n * (i + 1)] for i in range(8))
        for k in range(n):
            _adamw_update(w_[k], g_[k], m_[k], v_[k], d_[k], m2_[k], v2_[k])
            g2_[k][...] = g_[k][...]

    specs = [pl.BlockSpec((r, w.shape[1]), lambda i: (i, 0)) for w, r in zip(ws, rows)]
    outs = pl.pallas_call(
        body, name=name, grid=(steps,), out_shape=[jax.ShapeDtypeStruct(w.shape, F32) for w in ws] * 4,
        in_specs=specs * 4, out_specs=specs * 4, compiler_params=_params(),
    )(*ws, *gs, *ms, *vs)
    return [tuple(outs[n * i + k] for i in range(4)) for k in range(n)]


def _adamw_small(ws, gs, ms, vs):
    n = len(ws)

    def body(*refs):
        w_, g_, m_, v_, d_, m2_, v2_, g2_ = (refs[n * i:n * (i + 1)] for i in range(8))
        for k in range(n):
            _adamw_update(w_[k], g_[k], m_[k], v_[k], d_[k], m2_[k], v2_[k])
            g2_[k][...] = g_[k][...]

    vm = pl.BlockSpec(memory_space=pltpu.VMEM)
    outs = pl.pallas_call(
        body, name="adamw_small", out_shape=[jax.ShapeDtypeStruct(w.shape, F32) for w in ws] * 4,
        in_specs=[vm] * (4 * n), out_specs=[vm] * (4 * n),
    )(*ws, *gs, *ms, *vs)
    return outs[:n], outs[n:2 * n], outs[2 * n:3 * n], outs[3 * n:]


def _place():
    x, y, c = lax.axis_index("x"), lax.axis_index("y"), lax.axis_index("c")
    chips = [(1 - x, y), (x, 1 - y), (1 - x, 1 - y)]
    return x, y, c, chips


class _staged_copies:
    def __init__(self, srcs, dsts, stage, sems):
        n = len(srcs)
        self.loads = [pltpu.make_async_copy(srcs[k], stage[k], sems.at[k]) for k in range(n)]
        self.stores = [pltpu.make_async_copy(stage[k], dsts[k], sems.at[n + k]) for k in range(n)]

    def load(self):
        for cp in self.loads:
            cp.start()

    def store(self):
        for ld, st in zip(self.loads, self.stores):
            ld.wait()
            st.start()

    def finish(self):
        for cp in self.stores:
            cp.wait()


def _allgather_shards(parts, unit_rows):
    n = len(parts)
    units = [(k, r, unit_rows[k]) for k in range(n) for r in range(0, parts[k].shape[0] // 2, unit_rows[k])]
    nu = len(units)

    def body(*refs):
        ins, outs, stage = refs[:n], refs[n:2 * n], refs[2 * n:3 * n]
        send_sems, recv_sems, local_sems = refs[3 * n:]
        x, y, c, chips = _place()
        sibling = (x, y, 1 - c)
        own = _staged_copies(ins, [o.at[2 * x + y] for o in outs], stage, local_sems)

        def block(i, px, py, half):
            k, r, u = units[i]
            return outs[k].at[2 * px + py, pl.ds(half * (parts[k].shape[0] // 2) + r, u), :]

        def copy(i, j, px, py, half, to, src=None):
            return pltpu.make_async_remote_copy(
                src_ref=block(i, px, py, half) if src is None else src, dst_ref=block(i, px, py, half),
                send_sem=send_sems.at[nu * j + i], recv_sem=recv_sems.at[nu * j + i], device_id=to, device_id_type=MESH)

        own.load()
        first, passed = [], []
        for i, (k, r, u) in enumerate(units):
            for j, chip in enumerate(chips):
                src = ins[k].at[pl.ds(c * (parts[k].shape[0] // 2) + r, u), :]
                first.append(copy(i, j, x, y, c, (*chip, c), src=src))
                first[-1].start()
        own.store()
        for i in range(nu):
            for j, chip in enumerate(chips):
                copy(i, j, *chip, c, (x, y, c)).wait_recv()
                passed.append(copy(i, 3 + j, *chip, c, sibling))
                passed[-1].start()
        for i in range(nu):
            for j, chip in enumerate(chips):
                copy(i, 3 + j, *chip, 1 - c, (x, y, c)).wait_recv()
        for cp in first + passed:
            cp.wait_send()
        own.finish()

    return pl.pallas_call(
        body, name="allgather_shards", out_shape=[jax.ShapeDtypeStruct((N_SHARD,) + p.shape, p.dtype) for p in parts],
        in_specs=[_any_spec()] * n, out_specs=[_any_spec()] * n,
        scratch_shapes=[pltpu.VMEM(p.shape, p.dtype) for p in parts] + [
            pltpu.SemaphoreType.DMA((6 * nu,)), pltpu.SemaphoreType.DMA((6 * nu,)), pltpu.SemaphoreType.DMA((2 * n,))],
        compiler_params=_params(),
    )(*parts)


def _d2d_pieces(rows, piece_rows):
    return [(r, piece_rows) for r in range(0, rows, piece_rows)]


def _rs_pair_swap(arrs, piece_rows, name):
    n = len(arrs)

    def body(*refs):
        ins, outs = refs[:n], refs[n:2 * n]
        send_sems, recv_sems = refs[2 * n:]
        x, y, c, _ = _place()
        sibling = (x, y, 1 - c)
        for k in range(n):
            H = arrs[k].shape[1] // 2
            for s in range(N_SHARD):
                for r, pr in _d2d_pieces(H, piece_rows[k]):
                    pltpu.make_async_remote_copy(
                        src_ref=ins[k].at[s, pl.ds((1 - c) * H + r, pr), :], dst_ref=outs[k].at[s, pl.ds(r, pr), :],
                        send_sem=send_sems.at[k], recv_sem=recv_sems.at[k], device_id=sibling, device_id_type=MESH).start()
        for k in range(n):
            H = arrs[k].shape[1] // 2
            whole = pltpu.make_async_remote_copy(
                src_ref=ins[k].at[:, pl.ds(0, H), :], dst_ref=outs[k], send_sem=send_sems.at[k], recv_sem=recv_sems.at[k],
                device_id=sibling, device_id_type=MESH)
            whole.wait_recv()
            whole.wait_send()

    return pl.pallas_call(
        body, name=name,
        out_shape=[jax.ShapeDtypeStruct((N_SHARD, a.shape[1] // 2, a.shape[2]), F32) for a in arrs],
        in_specs=[_any_spec()] * n, out_specs=[_any_spec()] * n,
        scratch_shapes=[pltpu.SemaphoreType.DMA((n,)), pltpu.SemaphoreType.DMA((n,))],
    )(*arrs)


def _rs_add_pair(a, got, core, chip, rows, name):
    _, H, C = got.shape
    nb = H // rows

    def body(c_ref, chip_ref, a_ref, b_ref, o_ref, land_ref):
        total = (a_ref[...] + b_ref[...]).astype(BF16)
        o_ref[...] = total

        @pl.when(pl.program_id(1) == chip_ref[0])
        def _():
            land_ref[...] = total

    spec = pl.BlockSpec((1, rows, C), lambda r, s, c_ref, chip_ref: (s, r, 0))
    shape = jax.ShapeDtypeStruct(got.shape, BF16)
    return pl.pallas_call(
        body, name=name, out_shape=[shape, shape],
        grid_spec=pltpu.PrefetchScalarGridSpec(
            num_scalar_prefetch=2, grid=(nb, N_SHARD),
            in_specs=[pl.BlockSpec((1, rows, C), lambda r, s, c_ref, chip_ref: (s, c_ref[0] * nb + r, 0)), spec],
            out_specs=[spec, pl.BlockSpec((1, rows, C), lambda r, s, c_ref, chip_ref: (chip_ref[0], r, 0))]),
        compiler_params=_params(),
    )(core, chip, a, got)


def _rs_sum_chips(parts, rows, name):
    _, H, C = parts.shape

    def body(p_ref, o_ref):
        o_ref[...] = ((p_ref[0].astype(F32) + p_ref[1].astype(F32)) + p_ref[2].astype(F32)) + p_ref[3].astype(F32)

    return pl.pallas_call(
        body, name=name, grid=(H // rows,), out_shape=jax.ShapeDtypeStruct((H, C), F32),
        in_specs=[pl.BlockSpec((N_SHARD, rows, C), lambda r: (0, r, 0))],
        out_specs=pl.BlockSpec((rows, C), lambda r: (r, 0)), compiler_params=_params(),
    )(parts)


def _rs_pair_share(halves, piece_rows, name):
    n = len(halves)

    def body(*refs):
        ins, outs, stage = refs[:n], refs[n:2 * n], refs[2 * n:3 * n]
        send_sems, recv_sems, local_sems = refs[3 * n:]
        x, y, c, _ = _place()
        sibling = (x, y, 1 - c)
        own = _staged_copies(ins, [outs[k].at[pl.ds(c * halves[k].shape[0], halves[k].shape[0]), :] for k in range(n)],
                             stage, local_sems)
        own.load()
        for k in range(n):
            H = halves[k].shape[0]
            for r, pr in _d2d_pieces(H, piece_rows[k]):
                pltpu.make_async_remote_copy(
                    src_ref=ins[k].at[pl.ds(r, pr), :], dst_ref=outs[k].at[pl.ds(c * H + r, pr), :],
                    send_sem=send_sems.at[k], recv_sem=recv_sems.at[k], device_id=sibling, device_id_type=MESH).start()
        own.store()
        for k in range(n):
            H = halves[k].shape[0]
            whole = pltpu.make_async_remote_copy(
                src_ref=ins[k], dst_ref=outs[k].at[pl.ds((1 - c) * H, H), :], send_sem=send_sems.at[k],
                recv_sem=recv_sems.at[k], device_id=sibling, device_id_type=MESH)
            whole.wait_recv()
            whole.wait_send()
        own.finish()

    return pl.pallas_call(
        body, name=name, out_shape=[jax.ShapeDtypeStruct((2 * h.shape[0], h.shape[1]), F32) for h in halves],
        in_specs=[_any_spec()] * n, out_specs=[_any_spec()] * n,
        scratch_shapes=[pltpu.VMEM(h.shape, F32) for h in halves] + [
            pltpu.SemaphoreType.DMA((n,)), pltpu.SemaphoreType.DMA((n,)), pltpu.SemaphoreType.DMA((2 * n,))],
        compiler_params=_params(),
    )(*halves)


_HBM = pl.BlockSpec(memory_space=pltpu.HBM)
_SEM = pl.BlockSpec(memory_space=pltpu.SEMAPHORE)
_EFFECT = pltpu.SideEffectType.DATAFLOW_SIDE_EFFECTING


def _gather_plan(srcs, lands, x, y, c, chips):
    plan = []
    for k in range(len(srcs)):
        H = srcs[k].shape[0] // 2
        for px, py in chips:
            plan.append((srcs[k].at[pl.ds(c * H, H), :], lands[k].at[2 * x + y, pl.ds(c * H, H), :], (px, py, c),
                         lands[k].at[2 * px + py, pl.ds(c * H, H), :]))
    return plan


def _scatter_plan(srcs, lands, x, y, c, chips):
    plan = []
    for k in range(len(srcs)):
        for px, py in chips:
            plan.append((srcs[k].at[2 * px + py], lands[k].at[2 * x + y], (px, py, c), lands[k].at[2 * px + py]))
    return plan


def _swap_plan(srcs, lands, x, y, c, chips):
    plan = []
    for k in range(len(srcs)):
        H = srcs[k].shape[1] // 2
        rows = H // SWAP_PIECES
        for s in range(N_SHARD):
            for r in range(0, H, rows):
                piece = lands[k].at[s, pl.ds(r, rows), :]
                plan.append((srcs[k].at[s, pl.ds((1 - c) * H + r, rows), :], piece, (x, y, 1 - c), piece))
    return plan


SWAP_PIECES = 8
COPIES_PER_ARRAY = {_gather_plan: 3, _scatter_plan: 3, _swap_plan: N_SHARD * SWAP_PIECES}


def _ici_start(srcs, lands, make_plan, name):
    n = len(srcs)
    ncopy = COPIES_PER_ARRAY[make_plan] * n

    def body(*refs):
        ins, lnd = refs[:n], refs[n:2 * n]
        send_sems, recv_sems = refs[2 * n], refs[2 * n + 1]
        token = refs[-1]
        x, y, c, chips = _place()
        for i, (src, dst, peer, _) in enumerate(make_plan(ins, lnd, x, y, c, chips)):
            pltpu.make_async_remote_copy(src_ref=src, dst_ref=dst, send_sem=send_sems.at[i], recv_sem=recv_sems.at[i],
                                         device_id=peer, device_id_type=MESH).start()
        token[...] = jnp.zeros_like(token)

    arrays = list(srcs) + list(lands)
    return pl.pallas_call(
        body, name=name,
        out_shape=(pltpu.SemaphoreType.DMA((ncopy,)), pltpu.SemaphoreType.DMA((ncopy,)),
                   *[pltpu.HBM(a.shape, a.dtype) for a in arrays], jax.ShapeDtypeStruct((8, LANES), F32)),
        in_specs=[_HBM] * (2 * n), out_specs=(_SEM, _SEM, *[_HBM] * (2 * n), pl.BlockSpec(memory_space=pltpu.VMEM)),
        input_output_aliases={i: 2 + i for i in range(2 * n)},
        compiler_params=pltpu.CompilerParams(has_side_effects=_EFFECT),
    )(*[pltpu.with_memory_space_constraint(a, pltpu.HBM) for a in arrays])


def _ici_wait(started, after, make_plan, name):
    send_sems, recv_sems = started[0], started[1]
    arrays = list(started[2:-1])
    n = len(arrays) // 2

    def body(*refs):
        ins, lnd = refs[:n], refs[n:2 * n]
        send_sems, recv_sems = refs[2 * n], refs[2 * n + 1]
        x, y, c, chips = _place()
        for i, (src, _, peer, landed) in enumerate(make_plan(ins, lnd, x, y, c, chips)):
            cp = pltpu.make_async_remote_copy(src_ref=src, dst_ref=landed, send_sem=send_sems.at[i],
                                              recv_sem=recv_sems.at[i], device_id=peer, device_id_type=MESH)
            cp.wait_send()
            cp.wait_recv()

    outs = pl.pallas_call(
        body, name=name, out_shape=tuple(pltpu.HBM(a.shape, a.dtype) for a in arrays),
        in_specs=[_HBM] * (2 * n) + [_SEM, _SEM, pl.BlockSpec(memory_space=pl.ANY)], out_specs=tuple([_HBM] * (2 * n)),
        input_output_aliases={i: i for i in range(2 * n)},
        compiler_params=pltpu.CompilerParams(has_side_effects=_EFFECT),
    )(*arrays, send_sems, recv_sems, after)
    return list(outs[:n]), list(outs[n:])


def _gather_finish(parts, lands, name):
    n = len(parts)

    def body(*refs):
        ins, lnd, outs, stage = refs[:n], refs[n:2 * n], refs[2 * n:3 * n], refs[3 * n:4 * n]
        send_sems, recv_sems, local_sems = refs[4 * n:]
        x, y, c, chips = _place()
        sibling = (x, y, 1 - c)
        own = _staged_copies(ins, [o.at[2 * x + y] for o in outs], stage, local_sems)
        own.load()
        sends = []
        for k in range(n):
            H = parts[k].shape[0] // 2
            for j, (px, py) in enumerate(chips):
                half = outs[k].at[2 * px + py, pl.ds(c * H, H), :]
                sends.append(pltpu.make_async_remote_copy(src_ref=half, dst_ref=half, send_sem=send_sems.at[3 * k + j],
                                                          recv_sem=recv_sems.at[3 * k + j], device_id=sibling, device_id_type=MESH))
                sends[-1].start()
        own.store()
        for k in range(n):
            H = parts[k].shape[0] // 2
            for j, (px, py) in enumerate(chips):
                other = outs[k].at[2 * px + py, pl.ds((1 - c) * H, H), :]
                pltpu.make_async_remote_copy(src_ref=other, dst_ref=other, send_sem=send_sems.at[3 * k + j],
                                             recv_sem=recv_sems.at[3 * k + j], device_id=sibling, device_id_type=MESH).wait_recv()
        for cp in sends:
            cp.wait_send()
        own.finish()

    return pl.pallas_call(
        body, name=name, out_shape=[jax.ShapeDtypeStruct(l.shape, l.dtype) for l in lands],
        in_specs=[_any_spec()] * (2 * n), out_specs=[_any_spec()] * n,
        input_output_aliases={n + k: k for k in range(n)},
        scratch_shapes=[pltpu.VMEM(p.shape, p.dtype) for p in parts] + [
            pltpu.SemaphoreType.DMA((3 * n,)), pltpu.SemaphoreType.DMA((3 * n,)), pltpu.SemaphoreType.DMA((2 * n,))],
        compiler_params=_params(),
    )(*parts, *lands)


SMALL_COLS = 1024


def _small_rows(shapes):
    starts, row = [], 0
    for r, cdim in shapes:
        starts.append(row)
        row += r * (-(-cdim // SMALL_COLS))
    return starts, -(-row // 8) * 8


def _allreduce_small(arrays, sink_rows, loss):
    n = len(arrays)
    shapes = [a.shape for a in arrays] + [(1, SWA_HEADS), (1, 1)]
    starts, total_rows = _small_rows(shapes)

    def pieces(k):
        r, cdim = shapes[k]
        per = -(-cdim // SMALL_COLS)
        return [(i, SMALL_COLS * j, min(SMALL_COLS, cdim - SMALL_COLS * j), starts[k] + per * i + j)
                for i in range(r) for j in range(per)]

    def body(*refs):
        ins, sink_ref, loss_ref = refs[:n], refs[n], refs[n + 1]
        outs = refs[n + 2:2 * n + 4]
        mine, all_ref, tot_ref, send_sems, recv_sems = refs[2 * n + 4:]
        x, y, c, _ = _place()
        me = 4 * x + 2 * y + c
        mine[...] = jnp.zeros_like(mine)
        for k in range(n):
            for i, col, wd, row in pieces(k):
                mine[row:row + 1, 0:wd] = ins[k][i:i + 1, col:col + wd]
        lane = _iota((1, SMALL_COLS), 1)
        sinks = jnp.zeros((1, SMALL_COLS), F32)
        for h in range(SWA_HEADS):
            head = jnp.sum(sink_ref[SWA_BLOCK * h:SWA_BLOCK * (h + 1), :]) * (1.0 / LANES)
            sinks = jnp.where(lane == h, head, sinks)
        mine[starts[n]:starts[n] + 1, :] = sinks
        mine[starts[n + 1]:starts[n + 1] + 1, 0:LANES] = loss_ref[0:1, :]
        all_ref[me] = mine[...]
        sends = []
        for k in range(1, 8):
            kx, ky, kc = (k >> 2) & 1, (k >> 1) & 1, k & 1
            peer = (x ^ kx, y ^ ky, c ^ kc)
            cp = pltpu.make_async_remote_copy(
                src_ref=mine, dst_ref=all_ref.at[me], send_sem=send_sems.at[k - 1], recv_sem=recv_sems.at[k - 1],
                device_id=peer, device_id_type=MESH)
            cp.start()
            sends.append(cp)
        for k in range(1, 8):
            kx, ky, kc = (k >> 2) & 1, (k >> 1) & 1, k & 1
            src = 4 * (x ^ kx) + 2 * (y ^ ky) + (c ^ kc)
            pltpu.make_async_remote_copy(
                src_ref=mine, dst_ref=all_ref.at[src], send_sem=send_sems.at[k - 1], recv_sem=recv_sems.at[k - 1],
                device_id=(x, y, c), device_id_type=MESH).wait_recv()
        for cp in sends:
            cp.wait_send()
        total = all_ref[0]
        for d in range(1, 8):
            total = total + all_ref[d]
        tot_ref[...] = total
        for k in range(n + 2):
            for i, col, wd, row in pieces(k):
                outs[k][i:i + 1, col:col + wd] = tot_ref[row:row + 1, 0:wd]

    vm = pl.BlockSpec(memory_space=pltpu.VMEM)
    buf = pltpu.VMEM((total_rows, SMALL_COLS), F32)
    return pl.pallas_call(
        body, name="allreduce_small", out_shape=[jax.ShapeDtypeStruct(s, F32) for s in shapes],
        in_specs=[vm] * (n + 2), out_specs=[vm] * (n + 2),
        scratch_shapes=[buf, pltpu.VMEM((8, total_rows, SMALL_COLS), F32), buf,
                        pltpu.SemaphoreType.DMA((7,)), pltpu.SemaphoreType.DMA((7,))],
    )(*arrays, sink_rows, loss)


BIG_NAMES = ("w_in", "w_out", "w_up", "w_down")
MATRIX_NAMES = BIG_NAMES + ("gla_gate_up", "conv_w")
LATE_NAMES = ("w_out", "w_up", "w_down")
GATE_SHARD = (16, GLA_QK // N_SHARD)
CONVW_SHARD = (3, SHARD_FF)
SMALL_W_ROWS = 96
PIECE_ROWS = (128, 128, 64, 88)
ADD_ROWS = (256, 128, 256, 176)
FIRST_UNIT_ROWS = (256, SMALL_W_ROWS // 2)


def _pad_rows(flat, rows):
    return jnp.pad(flat, (0, rows * LANES - flat.shape[0])).reshape(rows, LANES)


def _pack_small_weights(gate_up, conv_w):
    bits = lax.bitcast_convert_type(conv_w, BF16)
    return _pad_rows(jnp.concatenate([gate_up.astype(BF16).reshape(-1), bits.reshape(-1)]), SMALL_W_ROWS)


def _unpack_small_weights(packed):
    flat = packed.reshape(N_SHARD, -1)
    n_gate = GATE_SHARD[0] * GATE_SHARD[1]
    n_conv = 2 * CONVW_SHARD[0] * CONVW_SHARD[1]
    gate = flat[:, :n_gate].reshape((N_SHARD,) + GATE_SHARD)
    conv = lax.bitcast_convert_type(flat[:, n_gate:n_gate + n_conv].reshape((N_SHARD,) + CONVW_SHARD + (2,)), F32)
    return (jnp.transpose(gate, (1, 0, 2)).reshape(16, GLA_QK), jnp.transpose(conv, (1, 0, 2)).reshape(3, 2 * D_FF))


def _permute_w_in(w):
    pad = jnp.zeros((w.shape[0], IN_WIDTH_PAD - IN_WIDTH), w.dtype)
    return jnp.concatenate([w[:, 0:1024], w[:, 1040:2320], w[:, 1024:1040], pad], axis=1)


def _unpermute_w_in(wp):
    return jnp.concatenate([wp[:, 0:1024], wp[:, 2304:2320], wp[:, 1024:2304]], axis=1)


def _rope_tables(positions, zero):
    half = ROPE_DIM // 2
    inv_freq = ROPE_THETA ** (-jnp.arange(half, dtype=F32) * (2.0 / ROPE_DIM))
    d = jnp.arange(LANES) % SWA_HD
    freq = jnp.where(d < ROPE_DIM, inv_freq[d % half], 0.0)
    ang = positions.astype(F32)[:, None] * freq + zero
    cos, sin = jnp.cos(ang), jnp.sin(ang)
    return cos, jnp.where(d < half, -sin, 0.0), jnp.where((d >= half) & (d < ROPE_DIM), sin, 0.0)


SMALL_NAMES = (("pre_mix_norm", 1024), ("gla_gate_bias", 256), ("gla_out_norm", 128), ("swa_sinks", 8),
               ("post_mix_norm", 1024), ("pre_ffn_norm", 1024), ("conv_b", 5632), ("post_ffn_norm", 1024))


def _local_step(x, target, w, small, late_weights, early_grads, last_grads):
    rc, rsa, rsb = w["rope"]
    wp = w["wp"]
    gup = jnp.pad(w["gla_gate_up"], ((0, LANES - 16), (0, 0)))
    g1, g2, g3, g4 = (small[n] for n in ("pre_mix_norm", "post_mix_norm", "pre_ffn_norm", "post_ffn_norm"))
    gbias, gnorm, cb = small["gla_gate_bias"], small["gla_out_norm"], small["conv_b"]
    sinks = small["swa_sinks"].reshape(-1)
    cw = w["conv_w"]

    h1, q, k, v, la, gg, sq, kd, vd, glr = _proj_fwd(x, g1, wp, gup, gbias, rc, rsa, rsb)
    og, s_all = _gla_fwd(q, k, v, la)
    osw = _swa_fwd(sq, kd, vd, sinks)
    w_out, w_up4, w_down = late_weights(osw)
    x1, cat = _mix_out_fwd(x, og, gg, osw, gnorm, w_out, g2)
    h2, up, act, c1, c2, y, dx2, loss = _ffn_fwd(x1, g3, w_up4, cw, cb, w_down, g4, target)

    dy, dup, dx1, dg4, dg3, dcb, dcw = _ffn_bwd(dx2, y, g4, up, c1, c2, cw, w_down, w_up4, x1, g3)
    zero = early_grads(lambda: _matmul_tn(h2, dup, SHARD_FF, "grad_w_up", column_blocks_major=True),
                       lambda after=None: _matmul_tn(act, dy, D_MODEL, "grad_w_down", after=after))
    dmix, dog, dgg, dosw, dg2, dgn = _mix_out_bwd(dx1, cat, g2 + zero, w_out, og, gg, gnorm)
    dsq, dkd, dvd, dsink = _swa_bwd(sq, kd, vd, sinks, dosw)
    dq, dk, dv, dz = _gla_bwd(q, k, v, la, s_all, dog)
    dx, dproj, dg1, dgup, dgb = _proj_bwd(x, g1, wp, gup, glr, dq, dk, dv, dgg, dsq, dkd, dvd, dz, rc, rsa, rsb, dx1)

    grads = {
        "last": last_grads(lambda: _matmul_tn(h1, dproj, IN_WIDTH_PAD, "grad_w_in"),
                           lambda after=None: _matmul_tn(cat, dmix, D_MODEL, "grad_w_out", after=after)),
        "gla_gate_up": dgup[0:16],
        "conv_w": dcw,
    }
    small_grads = {
        "pre_mix_norm": dg1, "gla_gate_bias": dgb, "gla_out_norm": dgn,
        "post_mix_norm": dg2, "pre_ffn_norm": dg3, "conv_b": dcb, "post_ffn_norm": dg4,
    }
    return loss, dx, grads, small_grads, dsink


ADAM_ROWS = {"w_in": 256, "w_out": 64, "w_up": 256, "w_down": 176}
WEIGHT_ORDER = ("pre_mix_norm", "w_in", "gla_gate_up", "gla_gate_bias", "gla_out_norm", "swa_sinks", "w_out",
                "post_mix_norm", "pre_ffn_norm", "w_up", "conv_w", "conv_b", "w_down", "post_ffn_norm")
def kernel(x, positions, pre_mix_norm, w_in, gla_gate_up, gla_gate_bias, gla_out_norm, swa_sinks, w_out, post_mix_norm, pre_ffn_norm, w_up, conv_w, conv_b, w_down, post_ffn_norm, loss_target, m_pre_mix_norm, m_w_in, m_gla_gate_up, m_gla_gate_bias, m_gla_out_norm, m_swa_sinks, m_w_out, m_post_mix_norm, m_pre_ffn_norm, m_w_up, m_conv_w, m_conv_b, m_w_down, m_post_ffn_norm, v_pre_mix_norm, v_w_in, v_gla_gate_up, v_gla_gate_bias, v_gla_out_norm, v_swa_sinks, v_w_out, v_post_mix_norm, v_pre_ffn_norm, v_w_up, v_conv_w, v_conv_b, v_w_down, v_post_ffn_norm):
    weights = dict(pre_mix_norm=pre_mix_norm, w_in=w_in, gla_gate_up=gla_gate_up, gla_gate_bias=gla_gate_bias,
                   gla_out_norm=gla_out_norm, swa_sinks=swa_sinks, w_out=w_out, post_mix_norm=post_mix_norm,
                   pre_ffn_norm=pre_ffn_norm, w_up=w_up, conv_w=conv_w, conv_b=conv_b, w_down=w_down,
                   post_ffn_norm=post_ffn_norm)
    mom = dict(pre_mix_norm=m_pre_mix_norm, w_in=m_w_in, gla_gate_up=m_gla_gate_up, gla_gate_bias=m_gla_gate_bias,
               gla_out_norm=m_gla_out_norm, swa_sinks=m_swa_sinks, w_out=m_w_out, post_mix_norm=m_post_mix_norm,
               pre_ffn_norm=m_pre_ffn_norm, w_up=m_w_up, conv_w=m_conv_w, conv_b=m_conv_b, w_down=m_w_down,
               post_ffn_norm=m_post_ffn_norm)
    var = dict(pre_mix_norm=v_pre_mix_norm, w_in=v_w_in, gla_gate_up=v_gla_gate_up, gla_gate_bias=v_gla_gate_bias,
               gla_out_norm=v_gla_out_norm, swa_sinks=v_swa_sinks, w_out=v_w_out, post_mix_norm=v_post_mix_norm,
               pre_ffn_norm=v_pre_ffn_norm, w_up=v_w_up, conv_w=v_conv_w, conv_b=v_conv_b, w_down=v_w_down,
               post_ffn_norm=v_post_ffn_norm)
    weights, mom, var = ({n: a[0] if a.ndim == 3 else a for n, a in d.items()} for d in (weights, mom, var))

    core = lax.axis_index("c").astype(jnp.int32).reshape(1)
    chip = (2 * lax.axis_index("x") + lax.axis_index("y")).astype(jnp.int32).reshape(1)
    small = {n: weights[n] for n, _ in SMALL_NAMES}

    landing = lambda shards: [lax.empty((N_SHARD,) + s.shape, BF16) for s in shards]
    first_shards = [weights["w_in"].astype(BF16), _pack_small_weights(weights["gla_gate_up"], weights["conv_w"])]
    first_gather = _ici_start(first_shards, landing(first_shards), _gather_plan, "gather_start_first")
    rope = _rope_tables(positions[0], first_gather[-1][0, 0])
    win4, small4 = _gather_finish(*_ici_wait(first_gather, rope[0], _gather_plan, "gather_wait_first"), "gather_finish_first")
    gate_full, convw_full = _unpack_small_weights(small4)
    first = {"wp": _permute_w_in(jnp.transpose(win4, (1, 0, 2)).reshape(D_MODEL, IN_WIDTH)),
             "gla_gate_up": gate_full, "conv_w": convw_full, "rope": rope}
    after_first = (small4[0, 0, 0] * 0).astype(F32)
    late_shards = [(weights[n] + after_first).astype(BF16) for n in LATE_NAMES]
    gathering = _ici_start(late_shards, landing(late_shards), _gather_plan, "gather_start")
    small["pre_mix_norm"] = small["pre_mix_norm"] + gathering[-1][0, 0]

    def late_weights(after):
        wout4, wup4, wdown4 = _gather_finish(*_ici_wait(gathering, after, _gather_plan, "gather_wait"), "gather_finish")
        return wout4.reshape(D_MODEL, D_MODEL), wup4, wdown4.reshape(D_FF, D_MODEL)

    early = {}
    delta, new_m, new_v = {}, {}, {}

    def scatter_start(make_a, make_b, i, j, name):
        a = make_a()
        swapping = _ici_start([a], [lax.empty((N_SHARD, a.shape[1] // 2, a.shape[2]), F32)], _swap_plan,
                              "rs_swap_start_" + BIG_NAMES[i])
        b = make_b(swapping[-1])
        (a,), (got_a,) = _ici_wait(swapping, b, _swap_plan, "rs_swap_wait_" + BIG_NAMES[i])
        got_b = _rs_pair_swap([b], [PIECE_ROWS[j]], "rs_pair_swap_" + BIG_NAMES[j])[0]
        parts, lands = zip(*(_rs_add_pair(g, got, core, chip, ADD_ROWS[k], "rs_add_pair_" + BIG_NAMES[k])
                             for g, got, k in ((a, got_a, i), (b, got_b, j))))
        return _ici_start(list(parts), list(lands), _scatter_plan, name)

    def finish(landed, which, grads_out):
        names = [BIG_NAMES[i] for i in which]
        halves = [_rs_sum_chips(p, ADD_ROWS[i], "rs_sum_chips_" + BIG_NAMES[i]) for p, i in zip(landed, which)]
        reduced = _rs_pair_share(halves, [PIECE_ROWS[i] for i in which], "rs_pair_share_" + names[0])
        updated = _adamw_group([weights[n] for n in names], list(reduced), [mom[n] for n in names],
                               [var[n] for n in names], [ADAM_ROWS[n] for n in names], "adamw_" + names[0])
        for n, out in zip(names, updated):
            delta[n], new_m[n], new_v[n], grads_out[n] = out

    def early_grads(make_up4, make_down):
        early["scatter"] = scatter_start(make_up4, lambda after: make_down(after).reshape(N_SHARD, D_FF // N_SHARD, D_MODEL), 2, 3,
                                         "rs_scatter_start")
        return early["scatter"][-1][0, 0]

    def last_grads(make_wp, make_out):
        return scatter_start(
            lambda: jnp.transpose(_unpermute_w_in(make_wp()).reshape(D_MODEL, N_SHARD, IN_WIDTH // N_SHARD), (1, 0, 2)),
            lambda after: make_out(after).reshape(N_SHARD, D_MODEL // N_SHARD, D_MODEL), 0, 1, "rs_scatter_start_rest")

    loss, dx, grads, small_grads, sink_rows = _local_step(
        x[0], loss_target[0], first, small, late_weights, early_grads, last_grads)

    late_scatter = grads["last"]
    g_all = {}
    finish(_ici_wait(early["scatter"], late_scatter[-1], _scatter_plan, "rs_scatter_wait")[1], (2, 3), g_all)
    finish(_ici_wait(late_scatter, delta["w_down"], _scatter_plan, "rs_scatter_wait_rest")[1], (0, 1), g_all)
    vectors = [n for n, _ in SMALL_NAMES if n != "swa_sinks"]
    summed = _allreduce_small([small_grads[n] for n in vectors] + [grads["gla_gate_up"], grads["conv_w"]], sink_rows, loss)
    g_all.update(zip(vectors, summed))
    g_all.update({"swa_sinks": summed[-2],
                  "gla_gate_up": lax.dynamic_slice_in_dim(summed[-4], chip[0] * GATE_SHARD[1], GATE_SHARD[1], axis=1),
                  "conv_w": lax.dynamic_slice_in_dim(summed[-3], chip[0] * SHARD_FF, SHARD_FF, axis=1)})
    loss_sum = summed[-1][0, 0]

    tiny = [n for n, _ in SMALL_NAMES] + ["gla_gate_up", "conv_w"]
    for res, vals in zip((delta, new_m, new_v, g_all), _adamw_small(*([d[n] for n in tiny] for d in (weights, g_all, mom, var)))):
        res.update(zip(tiny, vals))

    def lead(n, a):
        return a[None] if n in MATRIX_NAMES else a

    outs = [loss_sum, dx[None]]
    for d in (g_all, delta, new_m, new_v):
        outs.extend(lead(n, d[n]) for n in WEIGHT_ORDER)
    return tuple(outs)
```
